```python
import jax, jax.numpy as jnp
from jax import lax
import numpy as np

D_MODEL = 1024
BATCH = 16
SEQ = 4096
DEPTH = 4

HEAD_DIM = 64
ROPE_THETA = 10000.0
NORM_EPS = 1e-6
BLOCK = 128
NEG_INF = -1e30

A_Q_HEADS = 6
A_KV_HEADS = 2
A_WINDOW = 128
A_WIDTH = A_Q_HEADS * HEAD_DIM

B_HEADS = 6
B_Q_RANK = 384
B_KV_RANK = 256
B_NOPE = 64
B_ROPE = 32
B_V = 64
B_WIDTH = B_HEADS * B_V

C_GROUPS = 4
C_GROUP_DIM = 64
C_WIDTH = C_GROUPS * C_GROUP_DIM
C_CHUNK = 128

MIX_WIDTH = A_WIDTH + B_WIDTH + C_WIDTH
IN_SIZES = (A_WIDTH, A_KV_HEADS * HEAD_DIM, A_KV_HEADS * HEAD_DIM,
            B_Q_RANK, B_KV_RANK, B_ROPE, C_WIDTH, C_WIDTH)
IN_COLS = A_WIDTH + 2 * A_KV_HEADS * HEAD_DIM + B_Q_RANK + B_KV_RANK + B_ROPE + 2 * C_WIDTH

FFN_HIDDEN = ((8 * D_MODEL // 3 + 255) // 256) * 256
N_MOD = 6

kernel_name = "hybrid_parallel_swa_mla_sgu_block"


def rms_norm(x, w):
    xf = x.astype(jnp.float32)
    y = xf * lax.rsqrt(jnp.mean(xf * xf, axis=-1, keepdims=True) + NORM_EPS)
    return (y * w.astype(jnp.float32)).astype(x.dtype)


def layer_norm(x, w, b):
    xf = x.astype(jnp.float32)
    mu = jnp.mean(xf, axis=-1, keepdims=True)
    var = jnp.mean(jnp.square(xf - mu), axis=-1, keepdims=True)
    y = (xf - mu) * lax.rsqrt(var + NORM_EPS)
    return (y * w.astype(jnp.float32) + b.astype(jnp.float32)).astype(x.dtype)


def rope_tables(positions, dim):
    inv = 1.0 / (ROPE_THETA ** (jnp.arange(0, dim, 2, dtype=jnp.float32) / dim))
    ang = positions.astype(jnp.float32)[..., None] * inv
    return jnp.cos(ang), jnp.sin(ang)


def apply_rope(x, cos, sin):
    x1, x2 = jnp.split(x, 2, axis=-1)
    c = cos[:, :, None, :].astype(x.dtype)
    s = sin[:, :, None, :].astype(x.dtype)
    return jnp.concatenate([x1 * c - x2 * s, x2 * c + x1 * s], axis=-1)


def sliding_window_gqa(q, k, v, sinks):
    b_, s_, hq, d = q.shape
    g = hq // A_KV_HEADS
    nb = s_ // BLOCK
    qb = q.reshape(b_, nb, BLOCK, A_KV_HEADS, g, d)
    pad = ((0, 0), (BLOCK, 0), (0, 0), (0, 0))
    kb = jnp.pad(k, pad).reshape(b_, nb + 1, BLOCK, A_KV_HEADS, d)
    vb = jnp.pad(v, pad).reshape(b_, nb + 1, BLOCK, A_KV_HEADS, d)
    kcat = jnp.concatenate([kb[:, :-1], kb[:, 1:]], axis=2)
    vcat = jnp.concatenate([vb[:, :-1], vb[:, 1:]], axis=2)
    scores = jnp.einsum('bnqhgd,bnkhd->bnhgqk', qb, kcat).astype(jnp.float32) * (d ** -0.5)
    qi = jnp.arange(BLOCK)[:, None]
    kj = jnp.arange(2 * BLOCK)[None, :]
    rel = qi + BLOCK - kj
    band = (rel >= 0) & (rel < A_WINDOW)
    not_pad = (jnp.arange(nb)[:, None, None] > 0) | (kj[None] >= BLOCK)
    mask = (band[None] & not_pad)[None, :, None, None]
    scores = jnp.where(mask, scores, NEG_INF)
    sink = sinks.astype(jnp.float32).reshape(A_KV_HEADS, g)[None, None, :, :, None, None]
    m = jnp.maximum(jnp.max(scores, axis=-1, keepdims=True), sink)
    p = jnp.exp(scores - m)
    p = p / (jnp.sum(p, axis=-1, keepdims=True) + jnp.exp(sink - m))
    out = jnp.einsum('bnhgqk,bnkhd->bnqhgd', p.astype(v.dtype), vcat)
    return out.reshape(b_, s_, hq * d)


def mla(c_q, c_kv, k_rope, q_norm_w, w_uq, kv_norm_w, w_ukv, cos, sin):
    b_, s_, _ = c_q.shape
    q = (rms_norm(c_q, q_norm_w) @ w_uq).reshape(b_, s_, B_HEADS, B_NOPE + B_ROPE)
    q_nope, q_rope = q[..., :B_NOPE], q[..., B_NOPE:]
    q_rope = apply_rope(q_rope, cos, sin)
    kv = (rms_norm(c_kv, kv_norm_w) @ w_ukv).reshape(b_, s_, B_HEADS, B_NOPE + B_V)
    k_nope, v = kv[..., :B_NOPE], kv[..., B_NOPE:]
    k_r = apply_rope(k_rope[:, :, None, :], cos, sin)[:, :, 0, :]
    scale = (B_NOPE + B_ROPE) ** -0.5
    nb = s_ // BLOCK
    qn = q_nope.reshape(b_, nb, BLOCK, B_HEADS, B_NOPE).transpose(1, 0, 2, 3, 4)
    qr = q_rope.reshape(b_, nb, BLOCK, B_HEADS, B_ROPE).transpose(1, 0, 2, 3, 4)
    key_pos = jnp.arange(s_)

    def one_block(args):
        qn_b, qr_b, n = args
        sc = (jnp.einsum('bqhd,bkhd->bhqk', qn_b, k_nope)
              + jnp.einsum('bqhd,bkd->bhqk', qr_b, k_r)).astype(jnp.float32) * scale
        qpos = n * BLOCK + jnp.arange(BLOCK)
        sc = jnp.where(key_pos[None, :] <= qpos[:, None], sc, NEG_INF)
        p = jax.nn.softmax(sc, axis=-1)
        return jnp.einsum('bhqk,bkhd->bqhd', p.astype(v.dtype), v)

    out = lax.map(one_block, (qn, qr, jnp.arange(nb)))
    return out.transpose(1, 0, 2, 3, 4).reshape(b_, s_, B_WIDTH)


def chunked_spatial_gating(u, v, ln_w, ln_b, w_s, b_s):
    v = layer_norm(v, ln_w, ln_b)
    b_, s_, _ = v.shape
    nc = s_ // C_CHUNK
    vc = v.reshape(b_, nc, C_CHUNK, C_GROUPS, C_GROUP_DIM)
    tri = jnp.tril(jnp.ones((C_CHUNK, C_CHUNK), dtype=bool))
    w = jnp.where(tri[None], w_s, jnp.zeros_like(w_s))
    mixed = jnp.einsum('gts,bnsgd->bntgd', w, vc) + b_s.T[None, None, :, :, None]
    return u * mixed.reshape(b_, s_, C_WIDTH)


def _fwd_setup_inputs(seed: int = 0) -> dict:
    key = jax.random.key(seed)
    ks = jax.random.split(key, 24)
    nrm = lambda k, shape, s: jax.random.normal(k, shape, dtype=jnp.float32) * s
    gain = lambda k, shape: 1.0 + nrm(k, shape, 0.02)
    x = nrm(ks[0], (BATCH, SEQ, D_MODEL), 1.0)
    c = nrm(ks[1], (BATCH, D_MODEL), 1.0)
    offs = jax.random.randint(ks[2], (BATCH,), 0, 1024, dtype=jnp.int32)
    positions = offs[:, None] + jnp.arange(SEQ, dtype=jnp.int32)[None, :]
    return {
        "x": x,
        "c": c,
        "positions": positions,
        "ada_w": nrm(ks[3], (DEPTH, D_MODEL, N_MOD * D_MODEL), 0.5 * D_MODEL ** -0.5),
        "ada_b": nrm(ks[4], (DEPTH, N_MOD * D_MODEL), 0.02),
        "norm1_w": gain(ks[5], (DEPTH, D_MODEL)),
        "w_in": nrm(ks[6], (DEPTH, D_MODEL, IN_COLS), D_MODEL ** -0.5),
        "a_sinks": nrm(ks[7], (DEPTH, A_Q_HEADS), 0.5),
        "b_q_norm_w": gain(ks[8], (DEPTH, B_Q_RANK)),
        "b_w_uq": nrm(ks[9], (DEPTH, B_Q_RANK, B_HEADS * (B_NOPE + B_ROPE)), B_Q_RANK ** -0.5),
        "b_kv_norm_w": gain(ks[10], (DEPTH, B_KV_RANK)),
        "b_w_ukv": nrm(ks[11], (DEPTH, B_KV_RANK, B_HEADS * (B_NOPE + B_V)), B_KV_RANK ** -0.5),
        "c_ln_w": gain(ks[12], (DEPTH, C_WIDTH)),
        "c_ln_b": nrm(ks[13], (DEPTH, C_WIDTH), 0.02),
        "c_w_s": nrm(ks[14], (DEPTH, C_GROUPS, C_CHUNK, C_CHUNK), C_CHUNK ** -0.5),
        "c_b_s": gain(ks[15], (DEPTH, C_GROUPS, C_CHUNK)),
        "out_norm_w": gain(ks[16], (DEPTH, MIX_WIDTH)),
        "w_out": nrm(ks[17], (DEPTH, MIX_WIDTH, D_MODEL), MIX_WIDTH ** -0.5),
        "norm2_w": gain(ks[18], (DEPTH, D_MODEL)),
        "w_gate_up": nrm(ks[19], (DEPTH, D_MODEL, 2 * FFN_HIDDEN), D_MODEL ** -0.5),
        "w_down": nrm(ks[20], (DEPTH, FFN_HIDDEN, D_MODEL), FFN_HIDDEN ** -0.5),
        "final_norm_w": gain(ks[21], (D_MODEL,)),
    }


def _fwd_reference(x, c, positions, ada_w, ada_b, norm1_w, w_in, a_sinks, b_q_norm_w, b_w_uq,
              b_kv_norm_w, b_w_ukv, c_ln_w, c_ln_b, c_w_s, c_b_s, out_norm_w, w_out,
              norm2_w, w_gate_up, w_down, final_norm_w):
    b_, s_, _ = x.shape
    cos_a, sin_a = rope_tables(positions, HEAD_DIM)
    cos_b, sin_b = rope_tables(positions, B_ROPE)
    split_at = np.cumsum(IN_SIZES)[:-1].tolist()
    c_act = jax.nn.silu(c)
    for l in range(DEPTH):
        mod = (c_act @ ada_w[l] + ada_b[l])[:, None, :]
        sh1, sc1, g1, sh2, sc2, g2 = jnp.split(mod, N_MOD, axis=-1)

        h = rms_norm(x, norm1_w[l]) * (1.0 + sc1) + sh1
        proj = h @ w_in[l]
        a_q, a_k, a_v, b_cq, b_ckv, b_kr, c_u, c_v = jnp.split(proj, split_at, axis=-1)

        qa = apply_rope(a_q.reshape(b_, s_, A_Q_HEADS, HEAD_DIM), cos_a, sin_a)
        ka = apply_rope(a_k.reshape(b_, s_, A_KV_HEADS, HEAD_DIM), cos_a, sin_a)
        va = a_v.reshape(b_, s_, A_KV_HEADS, HEAD_DIM)
        y_a = sliding_window_gqa(qa, ka, va, a_sinks[l])

        y_b = mla(b_cq, b_ckv, b_kr, b_q_norm_w[l], b_w_uq[l], b_kv_norm_w[l], b_w_ukv[l],
                  cos_b, sin_b)

        y_c = chunked_spatial_gating(jax.nn.gelu(c_u, approximate=False),
                                     jax.nn.gelu(c_v, approximate=False),
                                     c_ln_w[l], c_ln_b[l], c_w_s[l], c_b_s[l])

        gw = out_norm_w[l]
        y = jnp.concatenate([
            rms_norm(y_a, gw[:A_WIDTH]),
            rms_norm(y_b, gw[A_WIDTH:A_WIDTH + B_WIDTH]),
            rms_norm(y_c, gw[A_WIDTH + B_WIDTH:]),
        ], axis=-1)
        x = x + g1 * (y @ w_out[l])

        h = rms_norm(x, norm2_w[l]) * (1.0 + sc2) + sh2
        gate, up = jnp.split(h @ w_gate_up[l], 2, axis=-1)
        x = x + g2 * ((jax.nn.silu(gate) * up) @ w_down[l])
    return rms_norm(x, final_norm_w)


import jax as _jax
import jax.numpy as _jnp

TWIN_FORMAT = 'train_step'
FWD_PARAMS = ['x', 'c', 'positions', 'ada_w', 'ada_b', 'norm1_w', 'w_in', 'a_sinks', 'b_q_norm_w', 'b_w_uq', 'b_kv_norm_w', 'b_w_ukv', 'c_ln_w', 'c_ln_b', 'c_w_s', 'c_b_s', 'out_norm_w', 'w_out', 'norm2_w', 'w_gate_up', 'w_down', 'final_norm_w']
TWIN_WEIGHTS = ['ada_w', 'ada_b', 'norm1_w', 'w_in', 'a_sinks', 'b_q_norm_w', 'b_w_uq', 'b_kv_norm_w', 'b_w_ukv', 'c_ln_w', 'c_ln_b', 'c_w_s', 'c_b_s', 'out_norm_w', 'w_out', 'norm2_w', 'w_gate_up', 'w_down', 'final_norm_w']
TWIN_DIFF_INPUT = 'x'
TWIN_INPUTS = ['x', 'c', 'positions', 'ada_w', 'ada_b', 'norm1_w', 'w_in', 'a_sinks', 'b_q_norm_w', 'b_w_uq', 'b_kv_norm_w', 'b_w_ukv', 'c_ln_w', 'c_ln_b', 'c_w_s', 'c_b_s', 'out_norm_w', 'w_out', 'norm2_w', 'w_gate_up', 'w_down', 'final_norm_w', 'loss_target', 'm_ada_w', 'm_ada_b', 'm_norm1_w', 'm_w_in', 'm_a_sinks', 'm_b_q_norm_w', 'm_b_w_uq', 'm_b_kv_norm_w', 'm_b_w_ukv', 'm_c_ln_w', 'm_c_ln_b', 'm_c_w_s', 'm_c_b_s', 'm_out_norm_w', 'm_w_out', 'm_norm2_w', 'm_w_gate_up', 'm_w_down', 'm_final_norm_w', 'v_ada_w', 'v_ada_b', 'v_norm1_w', 'v_w_in', 'v_a_sinks', 'v_b_q_norm_w', 'v_b_w_uq', 'v_b_kv_norm_w', 'v_b_w_ukv', 'v_c_ln_w', 'v_c_ln_b', 'v_c_w_s', 'v_c_b_s', 'v_out_norm_w', 'v_w_out', 'v_norm2_w', 'v_w_gate_up', 'v_w_down', 'v_final_norm_w']
TWIN_OUTPUTS = ['loss', 'grad_x', 'grad_ada_w', 'grad_ada_b', 'grad_norm1_w', 'grad_w_in', 'grad_a_sinks', 'grad_b_q_norm_w', 'grad_b_w_uq', 'grad_b_kv_norm_w', 'grad_b_w_ukv', 'grad_c_ln_w', 'grad_c_ln_b', 'grad_c_w_s', 'grad_c_b_s', 'grad_out_norm_w', 'grad_w_out', 'grad_norm2_w', 'grad_w_gate_up', 'grad_w_down', 'grad_final_norm_w', 'delta_ada_w', 'delta_ada_b', 'delta_norm1_w', 'delta_w_in', 'delta_a_sinks', 'delta_b_q_norm_w', 'delta_b_w_uq', 'delta_b_kv_norm_w', 'delta_b_w_ukv', 'delta_c_ln_w', 'delta_c_ln_b', 'delta_c_w_s', 'delta_c_b_s', 'delta_out_norm_w', 'delta_w_out', 'delta_norm2_w', 'delta_w_gate_up', 'delta_w_down', 'delta_final_norm_w', 'new_m_ada_w', 'new_m_ada_b', 'new_m_norm1_w', 'new_m_w_in', 'new_m_a_sinks', 'new_m_b_q_norm_w', 'new_m_b_w_uq', 'new_m_b_kv_norm_w', 'new_m_b_w_ukv', 'new_m_c_ln_w', 'new_m_c_ln_b', 'new_m_c_w_s', 'new_m_c_b_s', 'new_m_out_norm_w', 'new_m_w_out', 'new_m_norm2_w', 'new_m_w_gate_up', 'new_m_w_down', 'new_m_final_norm_w', 'new_v_ada_w', 'new_v_ada_b', 'new_v_norm1_w', 'new_v_w_in', 'new_v_a_sinks', 'new_v_b_q_norm_w', 'new_v_b_w_uq', 'new_v_b_kv_norm_w', 'new_v_b_w_ukv', 'new_v_c_ln_w', 'new_v_c_ln_b', 'new_v_c_w_s', 'new_v_c_b_s', 'new_v_out_norm_w', 'new_v_w_out', 'new_v_norm2_w', 'new_v_w_gate_up', 'new_v_w_down', 'new_v_final_norm_w']
TWIN_LEAF_KINDS = {'loss': 'loss', 'grad_x': 'grad_x', 'grad_ada_w': 'grad_w', 'grad_ada_b': 'grad_w', 'grad_norm1_w': 'grad_w', 'grad_w_in': 'grad_w', 'grad_a_sinks': 'grad_w', 'grad_b_q_norm_w': 'grad_w', 'grad_b_w_uq': 'grad_w', 'grad_b_kv_norm_w': 'grad_w', 'grad_b_w_ukv': 'grad_w', 'grad_c_ln_w': 'grad_w', 'grad_c_ln_b': 'grad_w', 'grad_c_w_s': 'grad_w', 'grad_c_b_s': 'grad_w', 'grad_out_norm_w': 'grad_w', 'grad_w_out': 'grad_w', 'grad_norm2_w': 'grad_w', 'grad_w_gate_up': 'grad_w', 'grad_w_down': 'grad_w', 'grad_final_norm_w': 'grad_w', 'delta_ada_w': 'delta_w', 'delta_ada_b': 'delta_w', 'delta_norm1_w': 'delta_w', 'delta_w_in': 'delta_w', 'delta_a_sinks': 'delta_w', 'delta_b_q_norm_w': 'delta_w', 'delta_b_w_uq': 'delta_w', 'delta_b_kv_norm_w': 'delta_w', 'delta_b_w_ukv': 'delta_w', 'delta_c_ln_w': 'delta_w', 'delta_c_ln_b': 'delta_w', 'delta_c_w_s': 'delta_w', 'delta_c_b_s': 'delta_w', 'delta_out_norm_w': 'delta_w', 'delta_w_out': 'delta_w', 'delta_norm2_w': 'delta_w', 'delta_w_gate_up': 'delta_w', 'delta_w_down': 'delta_w', 'delta_final_norm_w': 'delta_w', 'new_m_ada_w': 'new_m', 'new_m_ada_b': 'new_m', 'new_m_norm1_w': 'new_m', 'new_m_w_in': 'new_m', 'new_m_a_sinks': 'new_m', 'new_m_b_q_norm_w': 'new_m', 'new_m_b_w_uq': 'new_m', 'new_m_b_kv_norm_w': 'new_m', 'new_m_b_w_ukv': 'new_m', 'new_m_c_ln_w': 'new_m', 'new_m_c_ln_b': 'new_m', 'new_m_c_w_s': 'new_m', 'new_m_c_b_s': 'new_m', 'new_m_out_norm_w': 'new_m', 'new_m_w_out': 'new_m', 'new_m_norm2_w': 'new_m', 'new_m_w_gate_up': 'new_m', 'new_m_w_down': 'new_m', 'new_m_final_norm_w': 'new_m', 'new_v_ada_w': 'new_v', 'new_v_ada_b': 'new_v', 'new_v_norm1_w': 'new_v', 'new_v_w_in': 'new_v', 'new_v_a_sinks': 'new_v', 'new_v_b_q_norm_w': 'new_v', 'new_v_b_w_uq': 'new_v', 'new_v_b_kv_norm_w': 'new_v', 'new_v_b_w_ukv': 'new_v', 'new_v_c_ln_w': 'new_v', 'new_v_c_ln_b': 'new_v', 'new_v_c_w_s': 'new_v', 'new_v_c_b_s': 'new_v', 'new_v_out_norm_w': 'new_v', 'new_v_w_out': 'new_v', 'new_v_norm2_w': 'new_v', 'new_v_w_gate_up': 'new_v', 'new_v_w_down': 'new_v', 'new_v_final_norm_w': 'new_v'}


def _forward(args):
    return _fwd_reference(*[args[k] for k in FWD_PARAMS])


def _output_shape():
    out = _jax.eval_shape(lambda: _forward(_fwd_setup_inputs(0)))
    return out.shape, out.dtype

N_MICROBATCH = 1
ADAM_LR = 0.001
ADAM_B1 = 0.9
ADAM_B2 = 0.999
ADAM_EPS = 1e-08
ADAM_WD = 0.01
ADAM_STEP = 10
PER_EXAMPLE_BATCH_AXIS = {'x': 0, 'c': 0, 'positions': 0, 'loss_target': 0}
SHARED_INPUTS = []
_WEIGHT_DTYPES = {'ada_w': _jnp.float32, 'ada_b': _jnp.float32, 'norm1_w': _jnp.float32, 'w_in': _jnp.float32, 'a_sinks': _jnp.float32, 'b_q_norm_w': _jnp.float32, 'b_w_uq': _jnp.float32, 'b_kv_norm_w': _jnp.float32, 'b_w_ukv': _jnp.float32, 'c_ln_w': _jnp.float32, 'c_ln_b': _jnp.float32, 'c_w_s': _jnp.float32, 'c_b_s': _jnp.float32, 'out_norm_w': _jnp.float32, 'w_out': _jnp.float32, 'norm2_w': _jnp.float32, 'w_gate_up': _jnp.float32, 'w_down': _jnp.float32, 'final_norm_w': _jnp.float32}
MOMENT_SCALE = {'ada_w': 1.283226e-01, 'ada_b': 2.089539e-01, 'norm1_w': 7.244137e-02, 'w_in': 7.626748e-02, 'a_sinks': 2.644589e-02, 'b_q_norm_w': 2.911213e-02, 'b_w_uq': 2.322206e-02, 'b_kv_norm_w': 1.345680e-01, 'b_w_ukv': 7.930637e-02, 'c_ln_w': 3.787766e-02, 'c_ln_b': 4.114341e-02, 'c_w_s': 2.733207e-02, 'c_b_s': 3.871836e-02, 'out_norm_w': 1.035359e-01, 'w_out': 1.065950e-01, 'norm2_w': 6.985316e-02, 'w_gate_up': 3.048604e-02, 'w_down': 4.986853e-02, 'final_norm_w': 6.516486e+01}


def _to_microbatches(a, axis):
    t = _jnp.moveaxis(a, axis, 0)
    t = t.reshape((N_MICROBATCH, t.shape[0] // N_MICROBATCH) + t.shape[1:])
    return _jnp.moveaxis(t, 1, axis + 1)


def setup_inputs(seed: int = 0) -> dict:
    inp = _fwd_setup_inputs(seed)
    key = _jax.random.fold_in(_jax.random.key(seed), 7919)
    shape, _ = _output_shape()
    out = dict(inp)
    out["loss_target"] = _jax.random.normal(_jax.random.fold_in(key, 0), shape, _jnp.float32)
    for i, name in enumerate(TWIN_WEIGHTS):
        w = inp[name].astype(_jnp.float32)
        if MOMENT_SCALE is None:
            s = _jnp.sqrt(_jnp.mean(_jnp.square(w)) + 1e-30)
        else:
            s = MOMENT_SCALE[name]
        km, kv = _jax.random.split(_jax.random.fold_in(key, i + 1))
        out[name] = w
        out["m_" + name] = s * _jax.random.normal(km, w.shape, _jnp.float32)
        out["v_" + name] = (s * s) * _jax.random.uniform(kv, w.shape, _jnp.float32, 0.5, 1.5)
    if N_MICROBATCH > 1:
        for name, axis in PER_EXAMPLE_BATCH_AXIS.items():
            out[name] = _to_microbatches(out[name], axis)
    return {'x': out['x'], 'c': out['c'], 'positions': out['positions'], 'ada_w': out['ada_w'], 'ada_b': out['ada_b'], 'norm1_w': out['norm1_w'], 'w_in': out['w_in'], 'a_sinks': out['a_sinks'], 'b_q_norm_w': out['b_q_norm_w'], 'b_w_uq': out['b_w_uq'], 'b_kv_norm_w': out['b_kv_norm_w'], 'b_w_ukv': out['b_w_ukv'], 'c_ln_w': out['c_ln_w'], 'c_ln_b': out['c_ln_b'], 'c_w_s': out['c_w_s'], 'c_b_s': out['c_b_s'], 'out_norm_w': out['out_norm_w'], 'w_out': out['w_out'], 'norm2_w': out['norm2_w'], 'w_gate_up': out['w_gate_up'], 'w_down': out['w_down'], 'final_norm_w': out['final_norm_w'], 'loss_target': out['loss_target'], 'm_ada_w': out['m_ada_w'], 'm_ada_b': out['m_ada_b'], 'm_norm1_w': out['m_norm1_w'], 'm_w_in': out['m_w_in'], 'm_a_sinks': out['m_a_sinks'], 'm_b_q_norm_w': out['m_b_q_norm_w'], 'm_b_w_uq': out['m_b_w_uq'], 'm_b_kv_norm_w': out['m_b_kv_norm_w'], 'm_b_w_ukv': out['m_b_w_ukv'], 'm_c_ln_w': out['m_c_ln_w'], 'm_c_ln_b': out['m_c_ln_b'], 'm_c_w_s': out['m_c_w_s'], 'm_c_b_s': out['m_c_b_s'], 'm_out_norm_w': out['m_out_norm_w'], 'm_w_out': out['m_w_out'], 'm_norm2_w': out['m_norm2_w'], 'm_w_gate_up': out['m_w_gate_up'], 'm_w_down': out['m_w_down'], 'm_final_norm_w': out['m_final_norm_w'], 'v_ada_w': out['v_ada_w'], 'v_ada_b': out['v_ada_b'], 'v_norm1_w': out['v_norm1_w'], 'v_w_in': out['v_w_in'], 'v_a_sinks': out['v_a_sinks'], 'v_b_q_norm_w': out['v_b_q_norm_w'], 'v_b_w_uq': out['v_b_w_uq'], 'v_b_kv_norm_w': out['v_b_kv_norm_w'], 'v_b_w_ukv': out['v_b_w_ukv'], 'v_c_ln_w': out['v_c_ln_w'], 'v_c_ln_b': out['v_c_ln_b'], 'v_c_w_s': out['v_c_w_s'], 'v_c_b_s': out['v_c_b_s'], 'v_out_norm_w': out['v_out_norm_w'], 'v_w_out': out['v_w_out'], 'v_norm2_w': out['v_norm2_w'], 'v_w_gate_up': out['v_w_gate_up'], 'v_w_down': out['v_w_down'], 'v_final_norm_w': out['v_final_norm_w']}


def _loss(weights, diff, rest, loss_target):
    with _jax.named_scope("forward"):
        args = {**rest, TWIN_DIFF_INPUT: diff, **{k: w.astype(_WEIGHT_DTYPES[k]) for k, w in weights.items()}}
        y = _forward(args)
    with _jax.named_scope("loss_head"):
        err = _jnp.square(y.astype(_jnp.float32) - loss_target)
        return 0.5 * _jnp.sum(_jnp.mean(err, axis=-1)) if err.ndim else 0.5 * err


def _adamw(w, g, m, v):
    m = ADAM_B1 * m + (1.0 - ADAM_B1) * g
    v = ADAM_B2 * v + (1.0 - ADAM_B2) * _jnp.square(g)
    m_hat = m / (1.0 - ADAM_B1 ** ADAM_STEP)
    v_hat = v / (1.0 - ADAM_B2 ** ADAM_STEP)
    delta = -ADAM_LR * (m_hat / (_jnp.sqrt(v_hat) + ADAM_EPS) + ADAM_WD * w)
    return delta, m, v


def reference(x, c, positions, ada_w, ada_b, norm1_w, w_in, a_sinks, b_q_norm_w, b_w_uq, b_kv_norm_w, b_w_ukv, c_ln_w, c_ln_b, c_w_s, c_b_s, out_norm_w, w_out, norm2_w, w_gate_up, w_down, final_norm_w, loss_target, m_ada_w, m_ada_b, m_norm1_w, m_w_in, m_a_sinks, m_b_q_norm_w, m_b_w_uq, m_b_kv_norm_w, m_b_w_ukv, m_c_ln_w, m_c_ln_b, m_c_w_s, m_c_b_s, m_out_norm_w, m_w_out, m_norm2_w, m_w_gate_up, m_w_down, m_final_norm_w, v_ada_w, v_ada_b, v_norm1_w, v_w_in, v_a_sinks, v_b_q_norm_w, v_b_w_uq, v_b_kv_norm_w, v_b_w_ukv, v_c_ln_w, v_c_ln_b, v_c_w_s, v_c_b_s, v_out_norm_w, v_w_out, v_norm2_w, v_w_gate_up, v_w_down, v_final_norm_w):
    given = dict(x=x, c=c, positions=positions, ada_w=ada_w, ada_b=ada_b, norm1_w=norm1_w, w_in=w_in, a_sinks=a_sinks, b_q_norm_w=b_q_norm_w, b_w_uq=b_w_uq, b_kv_norm_w=b_kv_norm_w, b_w_ukv=b_w_ukv, c_ln_w=c_ln_w, c_ln_b=c_ln_b, c_w_s=c_w_s, c_b_s=c_b_s, out_norm_w=out_norm_w, w_out=w_out, norm2_w=norm2_w, w_gate_up=w_gate_up, w_down=w_down, final_norm_w=final_norm_w, loss_target=loss_target, m_ada_w=m_ada_w, m_ada_b=m_ada_b, m_norm1_w=m_norm1_w, m_w_in=m_w_in, m_a_sinks=m_a_sinks, m_b_q_norm_w=m_b_q_norm_w, m_b_w_uq=m_b_w_uq, m_b_kv_norm_w=m_b_kv_norm_w, m_b_w_ukv=m_b_w_ukv, m_c_ln_w=m_c_ln_w, m_c_ln_b=m_c_ln_b, m_c_w_s=m_c_w_s, m_c_b_s=m_c_b_s, m_out_norm_w=m_out_norm_w, m_w_out=m_w_out, m_norm2_w=m_norm2_w, m_w_gate_up=m_w_gate_up, m_w_down=m_w_down, m_final_norm_w=m_final_norm_w, v_ada_w=v_ada_w, v_ada_b=v_ada_b, v_norm1_w=v_norm1_w, v_w_in=v_w_in, v_a_sinks=v_a_sinks, v_b_q_norm_w=v_b_q_norm_w, v_b_w_uq=v_b_w_uq, v_b_kv_norm_w=v_b_kv_norm_w, v_b_w_ukv=v_b_w_ukv, v_c_ln_w=v_c_ln_w, v_c_ln_b=v_c_ln_b, v_c_w_s=v_c_w_s, v_c_b_s=v_c_b_s, v_out_norm_w=v_out_norm_w, v_w_out=v_w_out, v_norm2_w=v_norm2_w, v_w_gate_up=v_w_gate_up, v_w_down=v_w_down, v_final_norm_w=v_final_norm_w)
    weights = {n: given[n] for n in TWIN_WEIGHTS}
    shared = {n: given[n] for n in SHARED_INPUTS}
    per_example = {n: given[n] for n in ['x', 'c', 'positions']}
    grad_fn = _jax.value_and_grad(_loss, argnums=(0, 1))

    def one_microbatch(ex, loss_target):
        ex = dict(ex)
        diff = ex.pop(TWIN_DIFF_INPUT)
        return grad_fn(weights, diff, {**shared, **ex}, loss_target)

    if N_MICROBATCH == 1:
        loss, (grad_w, grad_x) = one_microbatch(per_example, given["loss_target"])
    else:
        def body(carry, xs):
            loss_sum, grad_sum = carry
            l_k, (gw_k, gx_k) = one_microbatch(xs[0], xs[1])
            with _jax.named_scope("update"):
                return (loss_sum + l_k, _jax.tree.map(_jnp.add, grad_sum, gw_k)), gx_k

        init = (_jnp.zeros((), _jnp.float32), _jax.tree.map(_jnp.zeros_like, weights))
        (loss, grad_w), grad_x = _jax.lax.scan(body, init, (per_example, given["loss_target"]))
    with _jax.named_scope("update"):
        delta_w, new_m, new_v = {}, {}, {}
        for n in TWIN_WEIGHTS:
            delta_w[n], new_m[n], new_v[n] = _adamw(weights[n], grad_w[n], given["m_" + n], given["v_" + n])
    return (loss, grad_x, *[grad_w[n] for n in TWIN_WEIGHTS], *[delta_w[n] for n in TWIN_WEIGHTS],
            *[new_m[n] for n in TWIN_WEIGHTS], *[new_v[n] for n in TWIN_WEIGHTS])
```

```python
import functools
import math

import numpy as np
import jax
import jax.numpy as jnp
from jax import lax
from jax.experimental import pallas as pl
from jax.experimental.pallas import tpu as pltpu

F32 = jnp.float32
BF16 = jnp.bfloat16

D_MODEL = 1024
DEPTH = 4
HEAD_DIM = 64
ROPE_THETA = 10000.0
NORM_EPS = 1e-6
NEG_INF = -1e30
A_Q_HEADS = 6
A_KV_HEADS = 2
A_WINDOW = 128
B_HEADS = 6
B_Q_RANK = 384
B_KV_RANK = 256
B_NOPE = 64
B_ROPE = 32
B_V = 64
C_GROUPS = 4
C_GROUP_DIM = 64
C_WIDTH = 256
C_CHUNK = 128
IN_COLS = 1824
FFN_HIDDEN = 2816
N_MOD = 6
ADAM_LR = 0.001
ADAM_B1 = 0.9
ADAM_B2 = 0.999
ADAM_EPS = 1e-08
ADAM_WD = 0.01
ADAM_STEP = 10

LANES = 128
VMEM_LIMIT = 56 * 1024 * 1024
N_CHIPS = 4
N_DEV = 8

P_AQ, P_AK, P_AV, P_CQ, P_CKV, P_KR, P_CU, P_CV, P_END = 0, 768, 1024, 1280, 1664, 1920, 2048, 2304, 2560
Y_A, Y_B, Y_C, Y_END = 0, 768, 1536, 1792


def _map_w_in():
    idx = -np.ones(P_END, np.int64)
    half = HEAD_DIM // 2
    for h in range(A_Q_HEADS):
        idx[P_AQ + h * LANES + np.arange(half)] = h * HEAD_DIM + np.arange(half)
        idx[P_AQ + h * LANES + 64 + np.arange(half)] = h * HEAD_DIM + half + np.arange(half)
    for h in range(A_KV_HEADS):
        idx[P_AK + h * LANES + np.arange(half)] = 384 + h * HEAD_DIM + np.arange(half)
        idx[P_AK + h * LANES + 64 + np.arange(half)] = 384 + h * HEAD_DIM + half + np.arange(half)
        idx[P_AV + h * LANES + np.arange(HEAD_DIM)] = 512 + h * HEAD_DIM + np.arange(HEAD_DIM)
    idx[P_CQ:P_CQ + 384] = 640 + np.arange(384)
    idx[P_CKV:P_CKV + 256] = 1024 + np.arange(256)
    idx[P_KR + 48 + np.arange(16)] = 1280 + np.arange(16)
    idx[P_KR + 112 + np.arange(16)] = 1296 + np.arange(16)
    idx[P_CU:P_CU + 256] = 1312 + np.arange(256)
    idx[P_CV:P_CV + 256] = 1568 + np.arange(256)
    return idx


def _map_w_uq():
    idx = -np.ones(B_HEADS * LANES, np.int64)
    for h in range(B_HEADS):
        b = h * (B_NOPE + B_ROPE)
        idx[h * LANES + np.arange(48)] = b + np.arange(48)
        idx[h * LANES + 48 + np.arange(16)] = b + 64 + np.arange(16)
        idx[h * LANES + 64 + np.arange(16)] = b + 48 + np.arange(16)
        idx[h * LANES + 112 + np.arange(16)] = b + 80 + np.arange(16)
    return idx


def _map_w_ukv():
    idx = -np.ones(2 * B_HEADS * LANES, np.int64)
    for h in range(B_HEADS):
        b = h * (B_NOPE + B_V)
        idx[h * LANES + np.arange(48)] = b + np.arange(48)
        idx[h * LANES + 64 + np.arange(16)] = b + 48 + np.arange(16)
        idx[B_HEADS * LANES + h * LANES + np.arange(B_V)] = b + B_NOPE + np.arange(B_V)
    return idx


def _map_w_out():
    idx = -np.ones(Y_END, np.int64)
    for h in range(A_Q_HEADS):
        idx[Y_A + h * LANES + np.arange(64)] = h * 64 + np.arange(64)
    for h in range(B_HEADS):
        idx[Y_B + h * LANES + np.arange(64)] = 384 + h * 64 + np.arange(64)
    idx[Y_C:Y_C + 256] = 768 + np.arange(256)
    return idx


def _inverse(idx, n):
    inv = np.zeros(n, np.int64)
    pos = np.nonzero(idx >= 0)[0]
    inv[idx[pos]] = pos
    return inv


def _runs(idx):
    runs, i, n = [], 0, len(idx)
    while i < n:
        j = i + 1
        while j < n and ((idx[i] < 0 and idx[j] < 0) or (idx[i] >= 0 and idx[j] == idx[i] + (j - i))):
            j += 1
        runs.append((int(idx[i]), j - i))
        i = j
    return runs


def _select_axis(w, idx, axis):
    pieces = []
    for start, length in _runs(idx):
        if start < 0:
            shape = list(w.shape)
            shape[axis] = length
            pieces.append(jnp.zeros(shape, w.dtype))
        else:
            pieces.append(lax.slice_in_dim(w, start, start + length, axis=axis))
    return jnp.concatenate(pieces, axis=axis)


def _pad_axis(w, idx, axis):
    return _select_axis(w, idx, axis)


def _unpad_axis(g, idx, n, axis):
    return _select_axis(g, _inverse(idx, n), axis)


def _params(sem):
    return pltpu.CompilerParams(dimension_semantics=sem, vmem_limit_bytes=VMEM_LIMIT)


def _tile(dim, target):
    if dim <= target:
        return dim
    best = None
    for t in range(LANES, target + 1, LANES):
        if dim % t == 0:
            best = t
    assert best is not None, dim
    return best


def _row_div(rows, target):
    if rows <= target:
        return rows
    best = None
    for t in range(8, target + 1, 8):
        if rows % t == 0:
            best = t
    assert best is not None, rows
    return best


_DIMS = {"nn": (((1,), (0,)), ((), ())), "nt": (((1,), (1,)), ((), ())), "tn": (((0,), (0,)), ((), ()))}


def _matmul(a, b, mode, out_dtype, name, tm=512, tn=1408, tk=1408):
    if mode == "nn":
        (m, k), (_, n) = a.shape, b.shape
    elif mode == "nt":
        (m, k), (n, _) = a.shape, b.shape
    else:
        (k, m), (_, n) = a.shape, b.shape
    tm, tn, tk = _tile(m, tm), _tile(n, tn), _tile(k, tk)
    nk = k // tk
    dims = _DIMS[mode]

    def kern(a_ref, b_ref, o_ref, acc_ref):
        kk = pl.program_id(2)

        @pl.when(kk == 0)
        def _():
            acc_ref[...] = jnp.zeros_like(acc_ref)

        acc_ref[...] += lax.dot_general(a_ref[...], b_ref[...], dims, preferred_element_type=F32)

        @pl.when(kk == nk - 1)
        def _():
            o_ref[...] = acc_ref[...].astype(o_ref.dtype)

    if mode == "nn":
        a_spec = pl.BlockSpec((tm, tk), lambda i, j, kk: (i, kk))
        b_spec = pl.BlockSpec((tk, tn), lambda i, j, kk: (kk, j))
    elif mode == "nt":
        a_spec = pl.BlockSpec((tm, tk), lambda i, j, kk: (i, kk))
        b_spec = pl.BlockSpec((tn, tk), lambda i, j, kk: (j, kk))
    else:
        a_spec = pl.BlockSpec((tk, tm), lambda i, j, kk: (kk, i))
        b_spec = pl.BlockSpec((tk, tn), lambda i, j, kk: (kk, j))
    return pl.pallas_call(
        kern, name=name, grid=(m // tm, n // tn, nk),
        in_specs=[a_spec, b_spec], out_specs=pl.BlockSpec((tm, tn), lambda i, j, kk: (i, j)),
        out_shape=jax.ShapeDtypeStruct((m, n), out_dtype),
        scratch_shapes=[pltpu.VMEM((tm, tn), F32)],
        compiler_params=_params(("parallel", "parallel", "arbitrary")),
    )(a, b)


def _rowcall(body, *, name, T, S, tr, rows, exs=(), pars=(), row_outs=(), ex_outs=(), par_outs=(), aliases=None):
    assert S % tr == 0 and T % S == 0
    per_ex = S // tr
    nb = T // S
    n_rows, n_exs, n_pars = len(rows), len(exs), len(pars)
    n_ro, n_eo, n_po = len(row_outs), len(ex_outs), len(par_outs)

    def kern(*refs):
        ins = refs[:n_rows + n_exs + n_pars]
        outs = refs[n_rows + n_exs + n_pars:]
        rv = [r[...] for r in ins[:n_rows]]
        ev = [r[0] for r in ins[n_rows:n_rows + n_exs]]
        pv = [r[...] for r in ins[n_rows + n_exs:]]
        ro, eo, po = body(rv, ev, pv)
        i = pl.program_id(0)
        for ref, val in zip(outs[:n_ro], ro):
            if isinstance(val, (list, tuple)):
                off = 0
                for piece in val:
                    w = piece.shape[-1]
                    ref[:, off:off + w] = piece.astype(ref.dtype)
                    off += w
            else:
                ref[...] = val.astype(ref.dtype)
        first_of_ex = (i % per_ex) == 0
        for ref, val in zip(outs[n_ro:n_ro + n_eo], eo):
            @pl.when(first_of_ex)
            def _(ref=ref, val=val):
                ref[0] = val

            @pl.when(jnp.logical_not(first_of_ex))
            def _(ref=ref, val=val):
                ref[0] += val
        for ref, val in zip(outs[n_ro + n_eo:], po):
            @pl.when(i == 0)
            def _(ref=ref, val=val):
                ref[...] = val

            @pl.when(i != 0)
            def _(ref=ref, val=val):
                ref[...] += val

    in_specs = [pl.BlockSpec((tr, w), functools.partial(lambda i, cb: (i, cb), cb=cb)) for (_, w, cb) in rows]
    in_specs += [pl.BlockSpec((1, 1, e.shape[-1]), lambda i: (i // per_ex, 0, 0)) for e in exs]
    in_specs += [pl.BlockSpec(p.shape, functools.partial(lambda i, nd: (0,) * nd, nd=p.ndim)) for p in pars]
    out_specs = [pl.BlockSpec((tr, w), functools.partial(lambda i, cb: (i, cb), cb=cb)) for (_, _, w, cb) in row_outs]
    out_specs += [pl.BlockSpec((1, 1, f), lambda i: (i // per_ex, 0, 0)) for f in ex_outs]
    out_specs += [pl.BlockSpec(tuple(s), functools.partial(lambda i, nd: (0,) * nd, nd=len(s))) for s in par_outs]
    out_shape = [jax.ShapeDtypeStruct((T, tw), dt) for (tw, dt, _, _) in row_outs]
    out_shape += [jax.ShapeDtypeStruct((nb, 1, f), F32) for f in ex_outs]
    out_shape += [jax.ShapeDtypeStruct(tuple(s), F32) for s in par_outs]
    res = pl.pallas_call(
        kern, name=name, grid=(T // tr,), in_specs=in_specs, out_specs=out_specs, out_shape=out_shape,
        input_output_aliases=aliases or {}, compiler_params=_params(("arbitrary",)),
    )(*[r[0] for r in rows], *exs, *pars)
    return res[:n_ro], res[n_ro:n_ro + n_eo], res[n_ro + n_eo:]


def _rms(x, w, n=None):
    n = x.shape[-1] if n is None else n
    ms = jnp.sum(x * x, axis=-1, keepdims=True) * (1.0 / n)
    return x * lax.rsqrt(ms + NORM_EPS) * w


def _gelu(x):
    return 0.5 * x * (1.0 + lax.erf(x * np.float32(1.0 / math.sqrt(2.0))))


def _silu(x):
    return x * jax.nn.sigmoid(x)


@jax.custom_vjp
def _rope(x, cos, sin):
    return x * cos + pltpu.roll(x, 64, 1) * sin


def _rope_fwd(x, cos, sin):
    return _rope(x, cos, sin), (cos, sin)


def _rope_bwd(res, dy):
    cos, sin = res
    return dy * cos + pltpu.roll(dy * sin, 64, 1), None, None


_rope.defvjp(_rope_fwd, _rope_bwd)


def _heads(x, n):
    return [x[:, h * LANES:(h + 1) * LANES] for h in range(n)]


def _f_norm_mod(x, w, sc, sh):
    return _rms(x, w) * (1.0 + sc) + sh


def _f_resid_norm_mod(xa, delta, g, w, sc, sh):
    xn = xa + g * delta
    return xn, _f_norm_mod(xn, w, sc, sh)


def _f_mixprep(proj, ca, sa, cb, sb, qnw, kvnw):
    qa = [_rope(p, ca, sa) for p in _heads(proj[:, P_AQ:P_AK], A_Q_HEADS)]
    ka = [_rope(p, ca, sa) for p in _heads(proj[:, P_AK:P_AV], A_KV_HEADS)]
    va = proj[:, P_AV:P_CQ]
    cqn = _rms(proj[:, P_CQ:P_CKV], qnw)
    ckvn = _rms(proj[:, P_CKV:P_KR], kvnw)
    kr = _rope(proj[:, P_KR:P_CU], cb, sb)
    return jnp.concatenate(qa, -1), jnp.concatenate(ka, -1), va, cqn, ckvn, kr


def _f_mlaprep(q, kv, kr, cb, sb):
    qs = [_rope(p, cb, sb) for p in _heads(q, B_HEADS)]
    ks = [p + kr for p in _heads(kv[:, :B_HEADS * LANES], B_HEADS)]
    return jnp.concatenate(qs, -1), jnp.concatenate(ks, -1), kv[:, B_HEADS * LANES:]


def _f_sgu(cu, cv, ln_w, ln_b, w_s, b_col):
    u = _gelu(cu)
    v = _gelu(cv)
    mu = jnp.mean(v, axis=-1, keepdims=True)
    var = jnp.mean(jnp.square(v - mu), axis=-1, keepdims=True)
    vn = (v - mu) * lax.rsqrt(var + NORM_EPS) * ln_w + ln_b
    r = lax.broadcasted_iota(jnp.int32, (C_CHUNK, C_CHUNK), 0)
    c = lax.broadcasted_iota(jnp.int32, (C_CHUNK, C_CHUNK), 1)
    lane = lax.broadcasted_iota(jnp.int32, (1, C_WIDTH), 1)
    mixed = jnp.zeros(cu.shape, F32)
    for g in range(C_GROUPS):
        gm = (lane // C_GROUP_DIM == g).astype(F32)
        wg = jnp.where(r >= c, w_s[g], 0.0).astype(BF16)
        mixed = mixed + jnp.dot(wg, (vn * gm).astype(BF16), preferred_element_type=F32) + b_col[g] * gm
    return u * mixed


def _f_outnorm(oa, ob, yc, gw):
    ya = _rms(oa, gw[:, Y_A:Y_B], A_Q_HEADS * HEAD_DIM)
    yb = _rms(ob, gw[:, Y_B:Y_C], B_HEADS * B_V)
    ycn = _rms(yc, gw[:, Y_C:Y_END])
    return jnp.concatenate([ya, yb, ycn], -1)


def _f_swiglu(gate, up):
    return _silu(gate) * up


def _mask(q_start, k_start, tq, tk, window):
    qpos = q_start + lax.broadcasted_iota(jnp.int32, (tq, tk), 0)
    kpos = k_start + lax.broadcasted_iota(jnp.int32, (tq, tk), 1)
    m = kpos <= qpos
    if window is not None:
        m = jnp.logical_and(m, qpos - kpos < window)
    return m


def _kv_range(q_start, tq, tk, window):
    lo = 0 if window is None else jnp.maximum(q_start - (window - 1), 0) // tk
    hi = (q_start + tq - 1) // tk + 1
    return lo, hi


def _attn_fwd(q, k, v, sinks, *, B, S, HQ, HK, window, scale, tq, tk, name):
    G = HQ // HK
    nq = S // tq
    T = B * S
    has_sink = sinks is not None

    def kern(*refs):
        if has_sink:
            q_ref, k_ref, v_ref, s_ref, o_ref, lse_ref = refs
        else:
            q_ref, k_ref, v_ref, o_ref, lse_ref = refs
        q_start = pl.program_id(2) * tq
        qv = q_ref[...]
        lo, hi = _kv_range(q_start, tq, tk, window)
        if has_sink:
            m0 = jnp.broadcast_to(s_ref[0][:, :1], (tq, 1))
            l0 = jnp.ones((tq, 1), F32)
        else:
            m0 = jnp.full((tq, 1), NEG_INF, F32)
            l0 = jnp.zeros((tq, 1), F32)

        def body(j, carry):
            m, l, acc = carry
            ks = pl.multiple_of(j * tk, tk)
            kk = k_ref[pl.ds(ks, tk), :]
            vv = v_ref[pl.ds(ks, tk), :]
            s = lax.dot_general(qv, kk, _DIMS["nt"], preferred_element_type=F32) * scale
            msk = _mask(q_start, ks, tq, tk, window)
            s = jnp.where(msk, s, NEG_INF)
            m_new = jnp.maximum(m, jnp.max(s, axis=-1, keepdims=True))
            alpha = jnp.exp(m - m_new)
            p = jnp.where(msk, jnp.exp(s - m_new), 0.0)
            l = alpha * l + jnp.sum(p, axis=-1, keepdims=True)
            acc = alpha * acc + jnp.dot(p.astype(BF16), vv, preferred_element_type=F32)
            return m_new, l, acc

        m, l, acc = lax.fori_loop(lo, hi, body, (m0, l0, jnp.zeros((tq, LANES), F32)))
        o_ref[...] = (acc / l).astype(o_ref.dtype)
        lse_ref[...] = jnp.broadcast_to(m + jnp.log(l), (tq, LANES))

    q_spec = pl.BlockSpec((tq, LANES), lambda b, h, i: (b * nq + i, h))
    kv_spec = pl.BlockSpec((S, LANES), lambda b, h, i: (b, h // G))
    in_specs = [q_spec, kv_spec, kv_spec]
    args = [q, k, v]
    if has_sink:
        in_specs.append(pl.BlockSpec((1, 1, LANES), lambda b, h, i: (h, 0, 0)))
        args.append(sinks)
    return pl.pallas_call(
        kern, name=name, grid=(B, HQ, nq), in_specs=in_specs, out_specs=[q_spec, q_spec],
        out_shape=[jax.ShapeDtypeStruct((T, HQ * LANES), F32), jax.ShapeDtypeStruct((T, HQ * LANES), F32)],
        compiler_params=_params(("parallel", "parallel", "arbitrary")),
    )(*args)


def _attn_bwd(q, k, v, o, lse, do, sinks, *, B, S, HQ, HK, window, scale, tq, tk, name):
    G = HQ // HK
    nq = S // tq
    T = B * S
    has_sink = sinks is not None

    def kern(*refs):
        if has_sink:
            q_ref, k_ref, v_ref, o_ref, lse_ref, do_ref, s_ref, dq_ref, dk_ref, dv_ref, ds_ref = refs
        else:
            q_ref, k_ref, v_ref, o_ref, lse_ref, do_ref, dq_ref, dk_ref, dv_ref = refs
        gi = pl.program_id(2)
        qi = pl.program_id(3)
        q_start = qi * tq

        @pl.when(jnp.logical_and(gi == 0, qi == 0))
        def _():
            dk_ref[...] = jnp.zeros_like(dk_ref)
            dv_ref[...] = jnp.zeros_like(dv_ref)

        qv = q_ref[...]
        dof = do_ref[...]
        dob = dof.astype(BF16)
        delta = jnp.sum(dof * o_ref[...], axis=-1, keepdims=True)
        lse_v = lse_ref[...][:, :1]
        lo, hi = _kv_range(q_start, tq, tk, window)

        def body(j, dq):
            ks = pl.multiple_of(j * tk, tk)
            kk = k_ref[pl.ds(ks, tk), :]
            vv = v_ref[pl.ds(ks, tk), :]
            s = lax.dot_general(qv, kk, _DIMS["nt"], preferred_element_type=F32) * scale
            msk = _mask(q_start, ks, tq, tk, window)
            p = jnp.where(msk, jnp.exp(s - lse_v), 0.0)
            dp = lax.dot_general(dob, vv, _DIMS["nt"], preferred_element_type=F32)
            ds = (p * (dp - delta) * scale).astype(BF16)
            dk_ref[pl.ds(ks, tk), :] += lax.dot_general(ds, qv, _DIMS["tn"], preferred_element_type=F32)
            dv_ref[pl.ds(ks, tk), :] += lax.dot_general(p.astype(BF16), dob, _DIMS["tn"], preferred_element_type=F32)
            return dq + jnp.dot(ds, kk, preferred_element_type=F32)

        dq_ref[...] = lax.fori_loop(lo, hi, body, jnp.zeros((tq, LANES), F32))
        if has_sink:
            sink = s_ref[0][:, :1]
            part = -jnp.sum(jnp.exp(sink - lse_v) * delta, axis=0, keepdims=True)
            part = jnp.broadcast_to(part, (1, LANES))

            @pl.when(qi == 0)
            def _():
                ds_ref[0] = part

            @pl.when(qi != 0)
            def _():
                ds_ref[0] += part

    q_spec = pl.BlockSpec((tq, LANES), lambda b, hk, g, i: (b * nq + i, hk * G + g))
    kv_spec = pl.BlockSpec((S, LANES), lambda b, hk, g, i: (b, hk))
    in_specs = [q_spec, kv_spec, kv_spec, q_spec, q_spec, q_spec]
    args = [q, k, v, o, lse, do]
    out_specs = [q_spec, kv_spec, kv_spec]
    out_shape = [jax.ShapeDtypeStruct((T, HQ * LANES), F32), jax.ShapeDtypeStruct((T, HK * LANES), F32),
                 jax.ShapeDtypeStruct((T, HK * LANES), F32)]
    if has_sink:
        in_specs.append(pl.BlockSpec((1, 1, LANES), lambda b, hk, g, i: (hk * G + g, 0, 0)))
        args.append(sinks)
        out_specs.append(pl.BlockSpec((1, 1, LANES), lambda b, hk, g, i: (b * HQ + hk * G + g, 0, 0)))
        out_shape.append(jax.ShapeDtypeStruct((B * HQ, 1, LANES), F32))
    res = pl.pallas_call(
        kern, name=name, grid=(B, HK, G, nq), in_specs=in_specs, out_specs=out_specs, out_shape=out_shape,
        compiler_params=_params(("parallel", "parallel", "arbitrary", "arbitrary")),
    )(*args)
    return res if has_sink else (*res, None)


def _rope_tables(pos_col, freq, sign, name):
    T = pos_col.shape[0]
    tr = _tile(T, 1024)

    def kern(p_ref, f_ref, s_ref, c_out, s_out):
        ang = p_ref[...] * f_ref[...]
        c_out[...] = jnp.cos(ang)
        s_out[...] = jnp.sin(ang) * s_ref[...]

    spec = pl.BlockSpec((tr, LANES), lambda i: (i, 0))
    par = pl.BlockSpec((1, LANES), lambda i: (0, 0))
    return pl.pallas_call(
        kern, name=name, grid=(T // tr,), in_specs=[pl.BlockSpec((tr, 1), lambda i: (i, 0)), par, par],
        out_specs=[spec, spec], out_shape=[jax.ShapeDtypeStruct((T, LANES), F32)] * 2,
        compiler_params=_params(("parallel",)),
    )(pos_col, freq, sign)


def _ada_fwd(c8, ada_w, ada_b):
    L, D, N = ada_w.shape
    tn = _tile(N, 1536)

    def kern(c_ref, w_ref, b_ref, o_ref):
        act = _silu(c_ref[...]).astype(BF16)
        o_ref[0] = jnp.dot(act, w_ref[0], preferred_element_type=F32) + b_ref[0]

    return pl.pallas_call(
        kern, name="ada_fwd", grid=(L, N // tn),
        in_specs=[pl.BlockSpec((8, D), lambda l, j: (0, 0)), pl.BlockSpec((1, D, tn), lambda l, j: (l, 0, j)),
                  pl.BlockSpec((1, 1, tn), lambda l, j: (l, 0, j))],
        out_specs=pl.BlockSpec((1, 8, tn), lambda l, j: (l, 0, j)),
        out_shape=jax.ShapeDtypeStruct((L, 8, N), F32), compiler_params=_params(("parallel", "parallel")),
    )(c8, ada_w, ada_b)


def _ada_bwd(c8, dmod):
    L, _, N = dmod.shape
    D = c8.shape[1]
    tn = _tile(N, 1536)

    def kern(c_ref, d_ref, gw_ref, gb_ref):
        act = _silu(c_ref[...]).astype(BF16)
        d = d_ref[0]
        gw_ref[0] = lax.dot_general(act, d.astype(BF16), _DIMS["tn"], preferred_element_type=F32)
        gb_ref[0] = jnp.sum(d, axis=0, keepdims=True)

    return pl.pallas_call(
        kern, name="ada_bwd", grid=(L, N // tn),
        in_specs=[pl.BlockSpec((8, D), lambda l, j: (0, 0)), pl.BlockSpec((1, 8, tn), lambda l, j: (l, 0, j))],
        out_specs=[pl.BlockSpec((1, D, tn), lambda l, j: (l, 0, j)), pl.BlockSpec((1, 1, tn), lambda l, j: (l, 0, j))],
        out_shape=[jax.ShapeDtypeStruct((L, D, N), F32), jax.ShapeDtypeStruct((L, 1, N), F32)],
        compiler_params=_params(("parallel", "parallel")),
    )(c8, dmod)


def _adamw(w, g, m, v, name):
    rows, cols = w.shape
    tr = _row_div(rows, 512)

    def kern(w_ref, g_ref, m_ref, v_ref, d_out, m_out, v_out):
        gv = g_ref[...]
        mn = ADAM_B1 * m_ref[...] + (1.0 - ADAM_B1) * gv
        vn = ADAM_B2 * v_ref[...] + (1.0 - ADAM_B2) * jnp.square(gv)
        m_hat = mn / (1.0 - ADAM_B1 ** ADAM_STEP)
        v_hat = vn / (1.0 - ADAM_B2 ** ADAM_STEP)
        d_out[...] = -ADAM_LR * (m_hat / (jnp.sqrt(v_hat) + ADAM_EPS) + ADAM_WD * w_ref[...])
        m_out[...] = mn
        v_out[...] = vn

    spec = pl.BlockSpec((tr, cols), lambda i: (i, 0))
    return pl.pallas_call(
        kern, name=name, grid=(rows // tr,), in_specs=[spec] * 4, out_specs=[spec] * 3,
        out_shape=[jax.ShapeDtypeStruct((rows, cols), F32)] * 3, compiler_params=_params(("parallel",)),
    )(w, g, m, v)


def _add_pair(a, b, name, out_dtypes):
    rows, cols = a.shape
    tr = _tile(rows, 1024)

    def kern(a_ref, b_ref, *outs):
        s = a_ref[...] + b_ref[...]
        for o in outs:
            o[...] = s.astype(o.dtype)

    spec = pl.BlockSpec((tr, cols), lambda i: (i, 0))
    return pl.pallas_call(
        kern, name=name, grid=(rows // tr,), in_specs=[spec, spec], out_specs=[spec] * len(out_dtypes),
        out_shape=[jax.ShapeDtypeStruct((rows, cols), dt) for dt in out_dtypes], compiler_params=_params(("parallel",)),
    )(a, b)


def _sum_slots(x, name):
    n, rows, cols = x.shape
    tr = _tile(rows, 1024)

    def kern(x_ref, o_ref):
        acc = x_ref[0].astype(F32)
        for j in range(1, n):
            acc = acc + x_ref[j].astype(F32)
        o_ref[...] = acc

    return pl.pallas_call(
        kern, name=name, grid=(rows // tr,), in_specs=[pl.BlockSpec((n, tr, cols), lambda i: (0, i, 0))],
        out_specs=pl.BlockSpec((tr, cols), lambda i: (i, 0)), out_shape=jax.ShapeDtypeStruct((rows, cols), F32),
        compiler_params=_params(("parallel",)),
    )(x)


_ANY = pl.BlockSpec(memory_space=pl.ANY)
_MESH = pl.DeviceIdType.MESH


def _chip_exchange(blob, name):
    n, R, C = blob.shape
    assert n == N_CHIPS

    def kern(src, dst, send_sems, recv_sems, local_sem):
        x, y, c = lax.axis_index("x"), lax.axis_index("y"), lax.axis_index("c")
        me = 2 * x + y
        flips = [(1, 0), (0, 1), (1, 1)]

        def peer(f):
            return (1 - x if f[0] else x, 1 - y if f[1] else y)

        own = pltpu.make_async_copy(src.at[me], dst.at[me], local_sem)
        own.start()
        copies = []
        for kk, f in enumerate(flips):
            px, py = peer(f)
            copies.append(pltpu.make_async_remote_copy(
                src_ref=src.at[2 * px + py], dst_ref=dst.at[me], send_sem=send_sems.at[kk], recv_sem=recv_sems.at[kk],
                device_id=(px, py, c), device_id_type=_MESH))
        for cp in copies:
            cp.start()
        for kk, f in enumerate(flips):
            px, py = peer(f)
            pltpu.make_async_remote_copy(
                src_ref=src.at[me], dst_ref=dst.at[2 * px + py], send_sem=send_sems.at[kk], recv_sem=recv_sems.at[kk],
                device_id=(px, py, c), device_id_type=_MESH).wait_recv()
        for cp in copies:
            cp.wait_send()
        own.wait()

    return pl.pallas_call(
        kern, name=name, in_specs=[_ANY], out_specs=_ANY, out_shape=jax.ShapeDtypeStruct((n, R, C), blob.dtype),
        scratch_shapes=[pltpu.SemaphoreType.DMA((3,)), pltpu.SemaphoreType.DMA((3,)), pltpu.SemaphoreType.DMA],
    )(blob)


def _chip_allgather(shard, name):
    R, C = shard.shape

    def kern(src, dst, send_sems, recv_sems, local_sem):
        x, y, c = lax.axis_index("x"), lax.axis_index("y"), lax.axis_index("c")
        me = 2 * x + y
        peers = [(1 - x, y), (x, 1 - y), (1 - x, 1 - y)]
        own = pltpu.make_async_copy(src, dst.at[me], local_sem)
        own.start()
        copies = [pltpu.make_async_remote_copy(
            src_ref=src, dst_ref=dst.at[me], send_sem=send_sems.at[kk], recv_sem=recv_sems.at[kk],
            device_id=(px, py, c), device_id_type=_MESH) for kk, (px, py) in enumerate(peers)]
        for cp in copies:
            cp.start()
        for kk, (px, py) in enumerate(peers):
            pltpu.make_async_remote_copy(
                src_ref=src, dst_ref=dst.at[2 * px + py], send_sem=send_sems.at[kk], recv_sem=recv_sems.at[kk],
                device_id=(px, py, c), device_id_type=_MESH).wait_recv()
        for cp in copies:
            cp.wait_send()
        own.wait()

    return pl.pallas_call(
        kern, name=name, in_specs=[_ANY], out_specs=_ANY, out_shape=jax.ShapeDtypeStruct((N_CHIPS, R, C), shard.dtype),
        scratch_shapes=[pltpu.SemaphoreType.DMA((3,)), pltpu.SemaphoreType.DMA((3,)), pltpu.SemaphoreType.DMA],
    )(shard)


def _sibling_swap(a, name):
    def kern(src, dst, send_sem, recv_sem):
        x, y, c = lax.axis_index("x"), lax.axis_index("y"), lax.axis_index("c")
        cp = pltpu.make_async_remote_copy(src_ref=src, dst_ref=dst, send_sem=send_sem, recv_sem=recv_sem,
                                          device_id=(x, y, 1 - c), device_id_type=_MESH)
        cp.start()
        cp.wait()

    return pl.pallas_call(
        kern, name=name, in_specs=[_ANY], out_specs=_ANY, out_shape=jax.ShapeDtypeStruct(a.shape, a.dtype),
        scratch_shapes=[pltpu.SemaphoreType.DMA, pltpu.SemaphoreType.DMA],
    )(a)


def _all_reduce_small(blob, name):
    R, C = blob.shape

    def kern(src, out, buf, send_sems, recv_sems):
        x, y, c = lax.axis_index("x"), lax.axis_index("y"), lax.axis_index("c")
        me = 4 * x + 2 * y + c
        buf[me] = src[...]
        flips = [(fx, fy, fc) for fx in (0, 1) for fy in (0, 1) for fc in (0, 1)][1:]

        def peer(f):
            return (x if f[0] == 0 else 1 - x, y if f[1] == 0 else 1 - y, c if f[2] == 0 else 1 - c)

        copies = []
        for kk, f in enumerate(flips):
            p = peer(f)
            copies.append(pltpu.make_async_remote_copy(
                src_ref=src, dst_ref=buf.at[me], send_sem=send_sems.at[kk], recv_sem=recv_sems.at[kk],
                device_id=p, device_id_type=_MESH))
        for cp in copies:
            cp.start()
        for kk, f in enumerate(flips):
            px, py, pc = peer(f)
            pltpu.make_async_remote_copy(
                src_ref=src, dst_ref=buf.at[4 * px + 2 * py + pc], send_sem=send_sems.at[kk], recv_sem=recv_sems.at[kk],
                device_id=(px, py, pc), device_id_type=_MESH).wait_recv()
        for cp in copies:
            cp.wait_send()
        acc = buf[0]
        for j in range(1, N_DEV):
            acc = acc + buf[j]
        out[...] = acc

    vm = pl.BlockSpec(memory_space=pltpu.VMEM)
    return pl.pallas_call(
        kern, name=name, in_specs=[vm], out_specs=vm, out_shape=jax.ShapeDtypeStruct((R, C), F32),
        scratch_shapes=[pltpu.VMEM((N_DEV, R, C), F32), pltpu.SemaphoreType.DMA((7,)), pltpu.SemaphoreType.DMA((7,))],
        compiler_params=pltpu.CompilerParams(vmem_limit_bytes=VMEM_LIMIT),
    )(blob)


def _row_tile(S):
    return 256 if S % 256 == 0 else 128


def _layer_fwd(xin, prev, mods, w, tabs, B, S):
    T = B * S
    tr = _row_tile(S)
    sh1, sc1, g1, sh2, sc2, g2 = mods
    ca, sa, cb, sb = tabs
    sv = {}
    if prev is None:
        def body(rv, ev, pv):
            return [_f_norm_mod(rv[0], pv[0], ev[0], ev[1])], [], []
        (h,), _, _ = _rowcall(body, name="f_norm_mod", T=T, S=S, tr=tr, rows=[(xin, D_MODEL, 0)], exs=[sc1, sh1],
                              pars=[w["norm1_w"]], row_outs=[(D_MODEL, BF16, D_MODEL, 0)])
        x = xin
    else:
        x1p, dp, g2p = prev

        def body(rv, ev, pv):
            xn, hh = _f_resid_norm_mod(rv[0], rv[1], ev[0], pv[0], ev[1], ev[2])
            return [xn, hh], [], []
        (x, h), _, _ = _rowcall(body, name="f_resid_norm_mod1", T=T, S=S, tr=tr,
                                rows=[(x1p, D_MODEL, 0), (dp, D_MODEL, 0)], exs=[g2p, sc1, sh1], pars=[w["norm1_w"]],
                                row_outs=[(D_MODEL, F32, D_MODEL, 0), (D_MODEL, BF16, D_MODEL, 0)])
    sv["x"], sv["h"] = x, h
    proj = _matmul(h, w["w_in"], "nn", F32, "mm_in")
    sv["proj"] = proj

    def body(rv, ev, pv):
        outs = _f_mixprep(rv[0], rv[1], rv[2], rv[3], rv[4], pv[0], pv[1])
        return list(outs), [], []
    (qa, ka, va, cqn, ckvn, kr), _, _ = _rowcall(
        body, name="f_mixprep", T=T, S=S, tr=tr,
        rows=[(proj, P_CU, 0), (ca, LANES, 0), (sa, LANES, 0), (cb, LANES, 0), (sb, LANES, 0)],
        pars=[w["b_q_norm_w"], w["b_kv_norm_w"]],
        row_outs=[(768, BF16, 768, 0), (256, BF16, 256, 0), (256, BF16, 256, 0), (384, BF16, 384, 0),
                  (256, BF16, 256, 0), (LANES, F32, LANES, 0)])
    sv.update(qa=qa, ka=ka, va=va, cqn=cqn, ckvn=ckvn)
    q = _matmul(cqn, w["b_w_uq"], "nn", F32, "mm_uq")
    kv = _matmul(ckvn, w["b_w_ukv"], "nn", F32, "mm_ukv")

    def body(rv, ev, pv):
        Q, K, V = _f_mlaprep(rv[0], rv[1], rv[2], rv[3], rv[4])
        return [Q, K, V], [], []
    (Q, K, V), _, _ = _rowcall(
        body, name="f_mlaprep", T=T, S=S, tr=tr,
        rows=[(q, 768, 0), (kv, 1536, 0), (kr, LANES, 0), (cb, LANES, 0), (sb, LANES, 0)],
        row_outs=[(768, BF16, 768, 0)] * 3)
    sv.update(Q=Q, K=K, V=V)
    ta = min(S, 128)
    tb = min(S, 256)
    oa, lse_a = _attn_fwd(qa, ka, va, w["sinks"], B=B, S=S, HQ=A_Q_HEADS, HK=A_KV_HEADS, window=A_WINDOW,
                          scale=HEAD_DIM ** -0.5, tq=ta, tk=ta, name="attn_a_fwd")
    ob, lse_b = _attn_fwd(Q, K, V, None, B=B, S=S, HQ=B_HEADS, HK=B_HEADS, window=None,
                          scale=(B_NOPE + B_ROPE) ** -0.5, tq=tb, tk=tb, name="attn_b_fwd")
    sv.update(oa=oa, lse_a=lse_a, ob=ob, lse_b=lse_b)

    def body(rv, ev, pv):
        return [_f_sgu(rv[0], rv[1], pv[0], pv[1], pv[2], pv[3])], [], []
    (yc,), _, _ = _rowcall(body, name="f_sgu", T=T, S=S, tr=C_CHUNK,
                           rows=[(proj, C_WIDTH, P_CU // C_WIDTH), (proj, C_WIDTH, P_CV // C_WIDTH)],
                           pars=[w["c_ln_w"], w["c_ln_b"], w["c_w_s"], w["c_b_col"]],
                           row_outs=[(C_WIDTH, F32, C_WIDTH, 0)])
    sv["yc"] = yc

    def body(rv, ev, pv):
        return [_f_outnorm(rv[0], rv[1], rv[2], pv[0])], [], []
    (y,), _, _ = _rowcall(body, name="f_outnorm", T=T, S=S, tr=tr,
                          rows=[(oa, 768, 0), (ob, 768, 0), (yc, C_WIDTH, 0)], pars=[w["out_norm_w"]],
                          row_outs=[(Y_END, BF16, Y_END, 0)])
    sv["y"] = y
    o = _matmul(y, w["w_out"], "nn", F32, "mm_out")
    sv["o"] = o

    def body(rv, ev, pv):
        xn, hh = _f_resid_norm_mod(rv[0], rv[1], ev[0], pv[0], ev[1], ev[2])
        return [xn, hh], [], []
    (x1, h2), _, _ = _rowcall(body, name="f_resid_norm_mod2", T=T, S=S, tr=tr,
                              rows=[(x, D_MODEL, 0), (o, D_MODEL, 0)], exs=[g1, sc2, sh2], pars=[w["norm2_w"]],
                              row_outs=[(D_MODEL, F32, D_MODEL, 0), (D_MODEL, BF16, D_MODEL, 0)])
    sv["h2"] = h2
    gu = _matmul(h2, w["w_gate_up"], "nn", F32, "mm_gate_up")
    sv["gu"] = gu

    def body(rv, ev, pv):
        return [_f_swiglu(rv[0], rv[1])], [], []
    (act,), _, _ = _rowcall(body, name="f_swiglu", T=T, S=S, tr=tr,
                            rows=[(gu, FFN_HIDDEN, 0), (gu, FFN_HIDDEN, 1)], row_outs=[(FFN_HIDDEN, BF16, FFN_HIDDEN, 0)])
    sv["act"] = act
    d = _matmul(act, w["w_down"], "nn", F32, "mm_down")
    sv["x1"], sv["d"] = x1, d
    return (x1, d, g2), sv


def _final(x1, d, g2, fw, target, B, S):
    T = B * S
    tr = _row_tile(S)

    def loss_fn(x1v, dv, gv, wv, tv):
        yf = _rms(x1v + gv * dv, wv)
        return 0.5 * jnp.sum(jnp.mean(jnp.square(yf - tv), axis=-1))

    def body(rv, ev, pv):
        x1v, dv, tv = rv
        val, vjp = jax.vjp(lambda a, b_, g, ww: loss_fn(a, b_, g, ww, tv), x1v, dv, ev[0], pv[0])
        dx1, dd, dg, dw = vjp(jnp.ones((), F32))
        return [dx1, dd], [dg], [dw, jnp.full((1, LANES), val, F32)]
    (dx1, dd), (dg2,), (dfw, loss) = _rowcall(
        body, name="final_loss", T=T, S=S, tr=tr, rows=[(x1, D_MODEL, 0), (d, D_MODEL, 0), (target, D_MODEL, 0)],
        exs=[g2], pars=[fw], row_outs=[(D_MODEL, F32, D_MODEL, 0), (D_MODEL, BF16, D_MODEL, 0)],
        ex_outs=[D_MODEL], par_outs=[(1, D_MODEL), (1, LANES)])
    return loss, dx1, dd, dg2, dfw


def _layer_bwd(sv, prev, mods, w, tabs, dx1, dd, B, S):
    T = B * S
    tr = _row_tile(S)
    sh1, sc1, g1, sh2, sc2, g2 = mods
    ca, sa, cb, sb = tabs
    gr = {}
    dact = _matmul(dd, w["w_down"], "nt", F32, "mm_down_dx")
    gr["w_down"] = _matmul(sv["act"], dd, "tn", F32, "mm_down_dw")

    def body(rv, ev, pv):
        _, vjp = jax.vjp(_f_swiglu, rv[0], rv[1])
        dgate, dup = vjp(rv[2])
        return [[dgate, dup]], [], []
    (dgu,), _, _ = _rowcall(body, name="b_swiglu", T=T, S=S, tr=tr,
                            rows=[(sv["gu"], FFN_HIDDEN, 0), (sv["gu"], FFN_HIDDEN, 1), (dact, FFN_HIDDEN, 0)],
                            row_outs=[(2 * FFN_HIDDEN, BF16, 2 * FFN_HIDDEN, 0)])
    dh2 = _matmul(dgu, w["w_gate_up"], "nt", F32, "mm_gate_up_dx")
    gr["w_gate_up"] = _matmul(sv["h2"], dgu, "tn", F32, "mm_gate_up_dw")

    def body(rv, ev, pv):
        xa, delta, dh, dxn = rv
        _, vjp = jax.vjp(_f_resid_norm_mod, xa, delta, ev[0], pv[0], ev[1], ev[2])
        dxa, ddelta, dg, dw, dsc, dsh = vjp((dxn, dh))
        return [dxa, ddelta], [dg, dsc, dsh], [dw]
    (dx, do), (dg1, dsc2, dsh2), (gr["norm2_w"],) = _rowcall(
        body, name="b_resid_norm_mod2", T=T, S=S, tr=tr,
        rows=[(sv["x"], D_MODEL, 0), (sv["o"], D_MODEL, 0), (dh2, D_MODEL, 0), (dx1, D_MODEL, 0)],
        exs=[g1, sc2, sh2], pars=[w["norm2_w"]],
        row_outs=[(D_MODEL, F32, D_MODEL, 0), (D_MODEL, BF16, D_MODEL, 0)], ex_outs=[D_MODEL] * 3,
        par_outs=[(1, D_MODEL)])
    dy = _matmul(do, w["w_out"], "nt", F32, "mm_out_dx")
    gr["w_out"] = _matmul(sv["y"], do, "tn", F32, "mm_out_dw")

    def body(rv, ev, pv):
        _, vjp = jax.vjp(_f_outnorm, rv[0], rv[1], rv[2], pv[0])
        doa, dob, dyc, dgw = vjp(rv[3])
        return [doa, dob, dyc], [], [dgw]
    (doa, dob, dyc), _, (gr["out_norm_w"],) = _rowcall(
        body, name="b_outnorm", T=T, S=S, tr=tr,
        rows=[(sv["oa"], 768, 0), (sv["ob"], 768, 0), (sv["yc"], C_WIDTH, 0), (dy, Y_END, 0)], pars=[w["out_norm_w"]],
        row_outs=[(768, F32, 768, 0), (768, F32, 768, 0), (C_WIDTH, F32, C_WIDTH, 0)], par_outs=[(1, Y_END)])

    tb = min(S, 256)
    ta = min(S, 128)
    dQ, dK, dV, _ = _attn_bwd(sv["Q"], sv["K"], sv["V"], sv["ob"], sv["lse_b"], dob, None, B=B, S=S, HQ=B_HEADS,
                              HK=B_HEADS, window=None, scale=(B_NOPE + B_ROPE) ** -0.5, tq=tb, tk=tb, name="attn_b_bwd")
    dqa, dka, dva, dsink = _attn_bwd(sv["qa"], sv["ka"], sv["va"], sv["oa"], sv["lse_a"], doa, w["sinks"], B=B, S=S,
                                     HQ=A_Q_HEADS, HK=A_KV_HEADS, window=A_WINDOW, scale=HEAD_DIM ** -0.5, tq=ta,
                                     tk=ta, name="attn_a_bwd")
    gr["sinks"] = dsink

    def body(rv, ev, pv):
        dQv, dKv, dVv, cbv, sbv = rv
        dq = [_rope_bwd((cbv, sbv), p)[0] for p in _heads(dQv, B_HEADS)]
        dkr = None
        for p in _heads(dKv, B_HEADS):
            dkr = p if dkr is None else dkr + p
        return [dq, [dKv, dVv], dkr], [], []
    (dq, dkv, dkr), _, _ = _rowcall(
        body, name="b_mlaprep", T=T, S=S, tr=tr,
        rows=[(dQ, 768, 0), (dK, 768, 0), (dV, 768, 0), (cb, LANES, 0), (sb, LANES, 0)],
        row_outs=[(768, BF16, 768, 0), (1536, BF16, 1536, 0), (LANES, F32, LANES, 0)])
    dcqn = _matmul(dq, w["b_w_uq"], "nt", F32, "mm_uq_dx")
    gr["b_w_uq"] = _matmul(sv["cqn"], dq, "tn", F32, "mm_uq_dw")
    dckvn = _matmul(dkv, w["b_w_ukv"], "nt", F32, "mm_ukv_dx")
    gr["b_w_ukv"] = _matmul(sv["ckvn"], dkv, "tn", F32, "mm_ukv_dw")

    def body(rv, ev, pv):
        proj, cav, sav, cbv, sbv, dqa_, dka_, dva_, dcqn_, dckvn_, dkr_ = rv
        _, vjp = jax.vjp(lambda p, a, b_: _f_mixprep(p, cav, sav, cbv, sbv, a, b_), proj, pv[0], pv[1])
        dproj, dqn, dkvn = vjp((dqa_, dka_, dva_, dcqn_, dckvn_, dkr_))
        return [dproj], [], [dqn, dkvn]
    (dproj,), _, (gr["b_q_norm_w"], gr["b_kv_norm_w"]) = _rowcall(
        body, name="b_mixprep", T=T, S=S, tr=tr,
        rows=[(sv["proj"], P_CU, 0), (ca, LANES, 0), (sa, LANES, 0), (cb, LANES, 0), (sb, LANES, 0),
              (dqa, 768, 0), (dka, 256, 0), (dva, 256, 0), (dcqn, 384, 0), (dckvn, 256, 0), (dkr, LANES, 0)],
        pars=[w["b_q_norm_w"], w["b_kv_norm_w"]], row_outs=[(P_END, BF16, P_CU, 0)],
        par_outs=[(1, B_Q_RANK), (1, B_KV_RANK)])

    def body(rv, ev, pv):
        cu, cv, dycv, _ = rv
        _, vjp = jax.vjp(_f_sgu, cu, cv, pv[0], pv[1], pv[2], pv[3])
        dcu, dcv, dlw, dlb, dws, dbc = vjp(dycv)
        return [[dcu, dcv]], [], [dlw, dlb, dws, dbc]
    (dproj,), _, (gr["c_ln_w"], gr["c_ln_b"], gr["c_w_s"], gr["c_b_col"]) = _rowcall(
        body, name="b_sgu", T=T, S=S, tr=C_CHUNK,
        rows=[(sv["proj"], C_WIDTH, P_CU // C_WIDTH), (sv["proj"], C_WIDTH, P_CV // C_WIDTH), (dyc, C_WIDTH, 0),
              (dproj, 2 * C_WIDTH, P_CU // (2 * C_WIDTH))],
        pars=[w["c_ln_w"], w["c_ln_b"], w["c_w_s"], w["c_b_col"]],
        row_outs=[(P_END, BF16, 2 * C_WIDTH, P_CU // (2 * C_WIDTH))],
        par_outs=[(1, C_WIDTH), (1, C_WIDTH), (C_GROUPS, C_CHUNK, C_CHUNK), (C_GROUPS, C_CHUNK, 1)],
        aliases={3: 0})
    dh = _matmul(dproj, w["w_in"], "nt", F32, "mm_in_dx")
    gr["w_in"] = _matmul(sv["h"], dproj, "tn", F32, "mm_in_dw")

    if prev is None:
        def body(rv, ev, pv):
            xv, dhv, dxd = rv
            _, vjp = jax.vjp(_f_norm_mod, xv, pv[0], ev[0], ev[1])
            dxa, dw, dsc, dsh = vjp(dhv)
            return [dxa + dxd], [dsc, dsh], [dw]
        (dxin,), (dsc1, dsh1), (gr["norm1_w"],) = _rowcall(
            body, name="b_norm_mod", T=T, S=S, tr=tr, rows=[(sv["x"], D_MODEL, 0), (dh, D_MODEL, 0), (dx, D_MODEL, 0)],
            exs=[sc1, sh1], pars=[w["norm1_w"]], row_outs=[(D_MODEL, F32, D_MODEL, 0)], ex_outs=[D_MODEL] * 2,
            par_outs=[(1, D_MODEL)])
        nxt = (dxin, None, None)
    else:
        x1p, dp, g2p = prev

        def body(rv, ev, pv):
            xa, delta, dhv, dxn = rv
            _, vjp = jax.vjp(_f_resid_norm_mod, xa, delta, ev[0], pv[0], ev[1], ev[2])
            dxa, ddelta, dg, dw, dsc, dsh = vjp((dxn, dhv))
            return [dxa, ddelta], [dg, dsc, dsh], [dw]
        (dx1p, ddp), (dg2p, dsc1, dsh1), (gr["norm1_w"],) = _rowcall(
            body, name="b_resid_norm_mod1", T=T, S=S, tr=tr,
            rows=[(x1p, D_MODEL, 0), (dp, D_MODEL, 0), (dh, D_MODEL, 0), (dx, D_MODEL, 0)],
            exs=[g2p, sc1, sh1], pars=[w["norm1_w"]],
            row_outs=[(D_MODEL, F32, D_MODEL, 0), (D_MODEL, BF16, D_MODEL, 0)], ex_outs=[D_MODEL] * 3,
            par_outs=[(1, D_MODEL)])
        nxt = (dx1p, ddp, dg2p)
    return gr, (dsh1, dsc1, dg1, dsh2, dsc2), nxt


def _lane_table(lanes_neg, lanes_pos, inv):
    freq = np.zeros((LANES,), np.int64) - 1
    sign = np.zeros((1, LANES), np.float32)
    n = len(lanes_neg)
    freq[lanes_neg] = np.arange(n)
    freq[lanes_pos] = np.arange(n)
    sign[0, lanes_neg] = -1.0
    sign[0, lanes_pos] = 1.0
    return _select_axis(inv, freq, 0).reshape(1, LANES), jnp.asarray(sign)


def _local_step(x, c, positions, target, wf):
    B, S, D = x.shape
    T = B * S
    xt = x.reshape(T, D)
    tgt = target.reshape(T, D)
    pos_col = positions.astype(F32).reshape(T, 1)
    inv_a = 1.0 / (ROPE_THETA ** (jnp.arange(0, HEAD_DIM, 2, dtype=F32) / HEAD_DIM))
    inv_b = 1.0 / (ROPE_THETA ** (jnp.arange(0, B_ROPE, 2, dtype=F32) / B_ROPE))
    fa, sga = _lane_table(np.arange(32), 64 + np.arange(32), inv_a)
    fb, sgb = _lane_table(48 + np.arange(16), 112 + np.arange(16), inv_b)
    ca, sa = _rope_tables(pos_col, fa, sga, "rope_a")
    cb, sb = _rope_tables(pos_col, fb, sgb, "rope_b")
    tabs = (ca, sa, cb, sb)

    c8 = jnp.zeros((8, D), F32).at[:B].set(c)
    mod = _ada_fwd(c8, wf["ada_w"], wf["ada_b"].reshape(DEPTH, 1, N_MOD * D))

    def mods_of(l):
        return tuple(mod[l, :B, i * D:(i + 1) * D].reshape(B, 1, D) for i in range(N_MOD))

    def layer_w(l):
        return {
            "norm1_w": wf["norm1_w"][l].reshape(1, D), "w_in": wf["w_in"][l],
            "sinks": jnp.broadcast_to(wf["a_sinks"][l].reshape(A_Q_HEADS, 1, 1), (A_Q_HEADS, 1, LANES)),
            "b_q_norm_w": wf["b_q_norm_w"][l].reshape(1, B_Q_RANK), "b_w_uq": wf["b_w_uq"][l],
            "b_kv_norm_w": wf["b_kv_norm_w"][l].reshape(1, B_KV_RANK), "b_w_ukv": wf["b_w_ukv"][l],
            "c_ln_w": wf["c_ln_w"][l].reshape(1, C_WIDTH), "c_ln_b": wf["c_ln_b"][l].reshape(1, C_WIDTH),
            "c_w_s": wf["c_w_s"][l], "c_b_col": wf["c_b_s"][l].reshape(C_GROUPS, C_CHUNK, 1),
            "out_norm_w": wf["out_norm_w"][l].reshape(1, Y_END), "w_out": wf["w_out"][l],
            "norm2_w": wf["norm2_w"][l].reshape(1, D), "w_gate_up": wf["w_gate_up"][l], "w_down": wf["w_down"][l],
        }

    saved, prevs = [], []
    prev = None
    for l in range(DEPTH):
        prevs.append(prev)
        prev, sv = _layer_fwd(xt, prev, mods_of(l), layer_w(l), tabs, B, S)
        saved.append(sv)
    x1, d, g2 = prev
    loss, dx1, dd, dg2, dfw = _final(x1, d, g2, wf["final_norm_w"].reshape(1, D), tgt, B, S)

    grads = [None] * DEPTH
    dmods = [None] * DEPTH
    for l in reversed(range(DEPTH)):
        gr, (dsh1, dsc1, dg1, dsh2, dsc2), nxt = _layer_bwd(saved[l], prevs[l], mods_of(l), layer_w(l), tabs, dx1, dd, B, S)
        dmods[l] = jnp.concatenate([dsh1, dsc1, dg1, dsh2, dsc2, dg2], axis=-1).reshape(B, N_MOD * D)
        grads[l] = gr
        dx1, dd, dg2 = nxt
    grad_x = dx1.reshape(B, S, D)
    dmod8 = jnp.zeros((DEPTH, 8, N_MOD * D), F32).at[:, :B].set(jnp.stack(dmods))
    g_ada_w, g_ada_b = _ada_bwd(c8, dmod8)

    def stack(k):
        return jnp.stack([grads[l][k] for l in range(DEPTH)])
    out = {
        "ada_w": g_ada_w, "ada_b": g_ada_b.reshape(DEPTH, N_MOD * D),
        "norm1_w": stack("norm1_w").reshape(DEPTH, D), "w_in": stack("w_in"),
        "a_sinks": stack("sinks")[:, :, 0, 0].reshape(DEPTH, B, A_Q_HEADS).sum(axis=1),
        "b_q_norm_w": stack("b_q_norm_w").reshape(DEPTH, B_Q_RANK), "b_w_uq": stack("b_w_uq"),
        "b_kv_norm_w": stack("b_kv_norm_w").reshape(DEPTH, B_KV_RANK), "b_w_ukv": stack("b_w_ukv"),
        "c_ln_w": stack("c_ln_w").reshape(DEPTH, C_WIDTH), "c_ln_b": stack("c_ln_b").reshape(DEPTH, C_WIDTH),
        "c_w_s": stack("c_w_s"), "c_b_s": stack("c_b_col").reshape(DEPTH, C_GROUPS, C_CHUNK),
        "out_norm_w": stack("out_norm_w").reshape(DEPTH, Y_END), "w_out": stack("w_out"),
        "norm2_w": stack("norm2_w").reshape(DEPTH, D), "w_gate_up": stack("w_gate_up"), "w_down": stack("w_down"),
        "final_norm_w": dfw.reshape(D),
    }
    return loss[0, 0], grad_x, out


SHARDED = ("ada_w", "w_in", "b_w_uq", "b_w_ukv", "w_out", "w_gate_up", "w_down")
ROW_SHARDED = ("w_out", "w_down")
SMALL = ("ada_b", "norm1_w", "a_sinks", "b_q_norm_w", "b_kv_norm_w", "c_ln_w", "c_ln_b", "c_w_s", "c_b_s",
         "out_norm_w", "norm2_w", "final_norm_w")
BLOB_COLS = 1024
BLOB_ROW_MULT = 256


def _pack(arrs, cols, mult):
    flat = jnp.concatenate([a.reshape(-1) for a in arrs])
    n = flat.shape[0]
    rows = -(-n // cols)
    rows = -(-rows // mult) * mult
    return jnp.pad(flat, (0, rows * cols - n)).reshape(rows, cols)


def _unpack(blob, shapes):
    flat = blob.reshape(-1)
    out, off = [], 0
    for s in shapes:
        n = int(np.prod(s))
        out.append(flat[off:off + n].reshape(s))
        off += n
    return out


def _gather_weights(shards):
    shapes = [shards[k].shape for k in SHARDED]
    blob = _pack([shards[k].astype(BF16) for k in SHARDED], BLOB_COLS, BLOB_ROW_MULT)
    full = _chip_allgather(blob, "gather_weights")
    parts = [_unpack(full[j], shapes) for j in range(N_CHIPS)]
    out = {}
    for i, k in enumerate(SHARDED):
        axis = 1 if k in ROW_SHARDED else 2
        out[k] = jnp.concatenate([parts[j][i] for j in range(N_CHIPS)], axis=axis)
    return out


def _reduce_sharded(gfull, chip, core):
    shard_shapes = {}
    chunks = []
    for j in range(N_CHIPS):
        arrs = []
        for k in SHARDED:
            g = gfull[k]
            axis = 1 if k in ROW_SHARDED else 2
            n = g.shape[axis] // N_CHIPS
            piece = lax.slice_in_dim(g, j * n, (j + 1) * n, axis=axis)
            shard_shapes[k] = piece.shape
            arrs.append(piece)
        chunks.append(_pack(arrs, BLOB_COLS, 2 * BLOB_ROW_MULT))
    blob = jnp.stack(chunks)
    R = blob.shape[1]
    half = R // 2
    mine = lax.dynamic_slice_in_dim(blob, core * half, half, axis=1).reshape(N_CHIPS * half, BLOB_COLS)
    theirs = lax.dynamic_slice_in_dim(blob, (1 - core) * half, half, axis=1).reshape(N_CHIPS * half, BLOB_COLS)
    got = _sibling_swap(theirs, "grad_pair_swap")
    (pair,) = _add_pair(mine, got, "grad_pair_sum", [BF16])
    landed = _chip_exchange(pair.reshape(N_CHIPS, half, BLOB_COLS), "grad_chip_exchange")
    total_half = _sum_slots(landed, "grad_chip_sum")
    other_half = _sibling_swap(total_half, "grad_half_swap")
    total = jnp.where(core == 0, jnp.concatenate([total_half, other_half]), jnp.concatenate([other_half, total_half]))
    vals = _unpack(total, [shard_shapes[k] for k in SHARDED])
    return dict(zip(SHARDED, vals))


def _pad_weights(full, small):
    i_in, i_uq, i_ukv, i_out = _map_w_in(), _map_w_uq(), _map_w_ukv(), _map_w_out()
    wf = dict(small)
    wf["ada_w"] = full["ada_w"]
    wf["w_in"] = _pad_axis(full["w_in"], i_in, 2)
    wf["b_w_uq"] = _pad_axis(full["b_w_uq"], i_uq, 2)
    wf["b_w_ukv"] = _pad_axis(full["b_w_ukv"], i_ukv, 2)
    wf["w_out"] = _pad_axis(full["w_out"], i_out, 1)
    wf["out_norm_w"] = _pad_axis(small["out_norm_w"], i_out, 1)
    wf["w_gate_up"] = full["w_gate_up"]
    wf["w_down"] = full["w_down"]
    return wf


def _unpad_grads(gp):
    i_in, i_uq, i_ukv, i_out = _map_w_in(), _map_w_uq(), _map_w_ukv(), _map_w_out()
    gfull = {
        "ada_w": gp["ada_w"], "w_in": _unpad_axis(gp["w_in"], i_in, IN_COLS, 2),
        "b_w_uq": _unpad_axis(gp["b_w_uq"], i_uq, B_HEADS * (B_NOPE + B_ROPE), 2),
        "b_w_ukv": _unpad_axis(gp["b_w_ukv"], i_ukv, B_HEADS * (B_NOPE + B_V), 2),
        "w_out": _unpad_axis(gp["w_out"], i_out, D_MODEL, 1), "w_gate_up": gp["w_gate_up"], "w_down": gp["w_down"],
    }
    gsmall = {k: gp[k] for k in SMALL}
    gsmall["out_norm_w"] = _unpad_axis(gp["out_norm_w"], i_out, D_MODEL, 1)
    return gfull, gsmall


def kernel(x, c, positions, ada_w, ada_b, norm1_w, w_in, a_sinks, b_q_norm_w, b_w_uq, b_kv_norm_w, b_w_ukv, c_ln_w, c_ln_b, c_w_s, c_b_s, out_norm_w, w_out, norm2_w, w_gate_up, w_down, final_norm_w, loss_target, m_ada_w, m_ada_b, m_norm1_w, m_w_in, m_a_sinks, m_b_q_norm_w, m_b_w_uq, m_b_kv_norm_w, m_b_w_ukv, m_c_ln_w, m_c_ln_b, m_c_w_s, m_c_b_s, m_out_norm_w, m_w_out, m_norm2_w, m_w_gate_up, m_w_down, m_final_norm_w, v_ada_w, v_ada_b, v_norm1_w, v_w_in, v_a_sinks, v_b_q_norm_w, v_b_w_uq, v_b_kv_norm_w, v_b_w_ukv, v_c_ln_w, v_c_ln_b, v_c_w_s, v_c_b_s, v_out_norm_w, v_w_out, v_norm2_w, v_w_gate_up, v_w_down, v_final_norm_w):
    names = ("ada_w", "ada_b", "norm1_w", "w_in", "a_sinks", "b_q_norm_w", "b_w_uq", "b_kv_norm_w", "b_w_ukv", "c_ln_w",
             "c_ln_b", "c_w_s", "c_b_s", "out_norm_w", "w_out", "norm2_w", "w_gate_up", "w_down", "final_norm_w")
    ws = dict(zip(names, (ada_w, ada_b, norm1_w, w_in, a_sinks, b_q_norm_w, b_w_uq, b_kv_norm_w, b_w_ukv, c_ln_w, c_ln_b,
                          c_w_s, c_b_s, out_norm_w, w_out, norm2_w, w_gate_up, w_down, final_norm_w)))
    ms = dict(zip(names, (m_ada_w, m_ada_b, m_norm1_w, m_w_in, m_a_sinks, m_b_q_norm_w, m_b_w_uq, m_b_kv_norm_w, m_b_w_ukv,
                          m_c_ln_w, m_c_ln_b, m_c_w_s, m_c_b_s, m_out_norm_w, m_w_out, m_norm2_w, m_w_gate_up, m_w_down,
                          m_final_norm_w)))
    vs = dict(zip(names, (v_ada_w, v_ada_b, v_norm1_w, v_w_in, v_a_sinks, v_b_q_norm_w, v_b_w_uq, v_b_kv_norm_w, v_b_w_ukv,
                          v_c_ln_w, v_c_ln_b, v_c_w_s, v_c_b_s, v_out_norm_w, v_w_out, v_norm2_w, v_w_gate_up, v_w_down,
                          v_final_norm_w)))
    chip = 2 * lax.axis_index("x") + lax.axis_index("y")
    core = lax.axis_index("c")

    full = _gather_weights({k: ws[k] for k in SHARDED})
    wf = _pad_weights(full, {k: ws[k] for k in SMALL})
    loss_local, grad_x, gp = _local_step(x, c, positions, loss_target, wf)
    gfull, gsmall = _unpad_grads(gp)
    gsh = _reduce_sharded(gfull, chip, core)
    small_shapes = [ws[k].shape for k in SMALL]
    sblob = _pack([gsmall[k] for k in SMALL] + [loss_local.reshape(1)], LANES, 8)
    sred = _all_reduce_small(sblob, "small_all_reduce")
    svals = _unpack(sred, small_shapes + [(1,)])
    loss = svals[-1].reshape(())
    grads = dict(gsh)
    grads.update(dict(zip(SMALL, svals[:-1])))

    delta, new_m, new_v = {}, {}, {}
    for k in SHARDED:
        shp = ws[k].shape
        two = (shp[0] * shp[1], shp[2])
        dlt, nm, nv = _adamw(ws[k].reshape(two), grads[k].reshape(two), ms[k].reshape(two), vs[k].reshape(two), "adamw_" + k)
        delta[k], new_m[k], new_v[k] = dlt.reshape(shp), nm.reshape(shp), nv.reshape(shp)
    pw = _pack([ws[k] for k in SMALL], LANES, 8)
    pg = _pack([grads[k] for k in SMALL], LANES, 8)
    pm = _pack([ms[k] for k in SMALL], LANES, 8)
    pv = _pack([vs[k] for k in SMALL], LANES, 8)
    dlt, nm, nv = _adamw(pw, pg, pm, pv, "adamw_small")
    for k, a, b_, c_ in zip(SMALL, _unpack(dlt, small_shapes), _unpack(nm, small_shapes), _unpack(nv, small_shapes)):
        delta[k], new_m[k], new_v[k] = a, b_, c_

    return (loss, grad_x, *[grads[k] for k in names], *[delta[k] for k in names], *[new_m[k] for k in names],
            *[new_v[k] for k in names])
```

```python
import functools
import math

import numpy as np
import jax
import jax.numpy as jnp
from jax import lax
from jax.experimental import pallas as pl
from jax.experimental.pallas import tpu as pltpu

F32 = jnp.float32
BF16 = jnp.bfloat16

D_MODEL = 1024
DEPTH = 4
HEAD_DIM = 64
ROPE_THETA = 10000.0
NORM_EPS = 1e-6
NEG_INF = -1e30
A_Q_HEADS = 6
A_KV_HEADS = 2
A_WINDOW = 128
B_HEADS = 6
B_Q_RANK = 384
B_KV_RANK = 256
B_NOPE = 64
B_ROPE = 32
B_V = 64
C_GROUPS = 4
C_GROUP_DIM = 64
C_WIDTH = 256
C_CHUNK = 128
IN_COLS = 1824
FFN_HIDDEN = 2816
N_MOD = 6
ADAM_LR = 0.001
ADAM_B1 = 0.9
ADAM_B2 = 0.999
ADAM_EPS = 1e-08
ADAM_WD = 0.01
ADAM_STEP = 10

LANES = 128
VMEM_LIMIT = 56 * 1024 * 1024
N_CHIPS = 4
N_DEV = 8

P_AQ, P_AK, P_AV, P_CQ, P_CKV, P_KR, P_CU, P_CV, P_END = 0, 768, 1024, 1280, 1664, 1920, 2048, 2304, 2560
Y_A, Y_B, Y_C, Y_END = 0, 768, 1536, 1792


def _map_w_in():
    idx = -np.ones(P_END, np.int64)
    half = HEAD_DIM // 2
    for h in range(A_Q_HEADS):
        idx[P_AQ + h * LANES + np.arange(half)] = h * HEAD_DIM + np.arange(half)
        idx[P_AQ + h * LANES + 64 + np.arange(half)] = h * HEAD_DIM + half + np.arange(half)
    for h in range(A_KV_HEADS):
        idx[P_AK + h * LANES + np.arange(half)] = 384 + h * HEAD_DIM + np.arange(half)
        idx[P_AK + h * LANES + 64 + np.arange(half)] = 384 + h * HEAD_DIM + half + np.arange(half)
        idx[P_AV + h * LANES + np.arange(HEAD_DIM)] = 512 + h * HEAD_DIM + np.arange(HEAD_DIM)
    idx[P_CQ:P_CQ + 384] = 640 + np.arange(384)
    idx[P_CKV:P_CKV + 256] = 1024 + np.arange(256)
    idx[P_KR + 48 + np.arange(16)] = 1280 + np.arange(16)
    idx[P_KR + 112 + np.arange(16)] = 1296 + np.arange(16)
    idx[P_CU:P_CU + 256] = 1312 + np.arange(256)
    idx[P_CV:P_CV + 256] = 1568 + np.arange(256)
    return idx


def _map_w_uq():
    idx = -np.ones(B_HEADS * LANES, np.int64)
    for h in range(B_HEADS):
        b = h * (B_NOPE + B_ROPE)
        idx[h * LANES + np.arange(48)] = b + np.arange(48)
        idx[h * LANES + 48 + np.arange(16)] = b + 64 + np.arange(16)
        idx[h * LANES + 64 + np.arange(16)] = b + 48 + np.arange(16)
        idx[h * LANES + 112 + np.arange(16)] = b + 80 + np.arange(16)
    return idx


def _map_w_ukv():
    idx = -np.ones(2 * B_HEADS * LANES, np.int64)
    for h in range(B_HEADS):
        b = h * (B_NOPE + B_V)
        idx[h * LANES + np.arange(48)] = b + np.arange(48)
        idx[h * LANES + 64 + np.arange(16)] = b + 48 + np.arange(16)
        idx[B_HEADS * LANES + h * LANES + np.arange(B_V)] = b + B_NOPE + np.arange(B_V)
    return idx


def _map_w_out():
    idx = -np.ones(Y_END, np.int64)
    for h in range(A_Q_HEADS):
        idx[Y_A + h * LANES + np.arange(64)] = h * 64 + np.arange(64)
    for h in range(B_HEADS):
        idx[Y_B + h * LANES + np.arange(64)] = 384 + h * 64 + np.arange(64)
    idx[Y_C:Y_C + 256] = 768 + np.arange(256)
    return idx


def _inverse(idx, n):
    inv = np.zeros(n, np.int64)
    pos = np.nonzero(idx >= 0)[0]
    inv[idx[pos]] = pos
    return inv


def _runs(idx):
    runs, i, n = [], 0, len(idx)
    while i < n:
        j = i + 1
        while j < n and ((idx[i] < 0 and idx[j] < 0) or (idx[i] >= 0 and idx[j] == idx[i] + (j - i))):
            j += 1
        runs.append((int(idx[i]), j - i))
        i = j
    return runs


def _select_axis(w, idx, axis):
    pieces = []
    for start, length in _runs(idx):
        if start < 0:
            shape = list(w.shape)
            shape[axis] = length
            pieces.append(jnp.zeros(shape, w.dtype))
        else:
            pieces.append(lax.slice_in_dim(w, start, start + length, axis=axis))
    return jnp.concatenate(pieces, axis=axis)


def _pad_axis(w, idx, axis):
    return _select_axis(w, idx, axis)


def _unpad_axis(g, idx, n, axis):
    return _select_axis(g, _inverse(idx, n), axis)


def _params(sem):
    return pltpu.CompilerParams(dimension_semantics=sem, vmem_limit_bytes=VMEM_LIMIT)


def _tile(dim, target):
    if dim <= target:
        return dim
    best = None
    for t in range(LANES, target + 1, LANES):
        if dim % t == 0:
            best = t
    assert best is not None, dim
    return best


def _row_div(rows, target):
    if rows <= target:
        return rows
    best = None
    for t in range(8, target + 1, 8):
        if rows % t == 0:
            best = t
    assert best is not None, rows
    return best


_DIMS = {"nn": (((1,), (0,)), ((), ())), "nt": (((1,), (1,)), ((), ())), "tn": (((0,), (0,)), ((), ()))}


def _matmul(a, b, mode, out_dtype, name, tn=1408):
    if mode == "nn":
        (m, k), (_, n) = a.shape, b.shape
    elif mode == "nt":
        (m, k), (n, _) = a.shape, b.shape
    else:
        (k, m), (_, n) = a.shape, b.shape
    tm, tk = (1408, 1024) if mode == "tn" else (1024, 1408)
    tm, tn, tk = _tile(m, tm), _tile(n, tn), _tile(k, tk)
    nk = k // tk
    dims = _DIMS[mode]

    def kern(a_ref, b_ref, o_ref, acc_ref):
        kk = pl.program_id(2)

        @pl.when(kk == 0)
        def _():
            acc_ref[...] = jnp.zeros_like(acc_ref)

        acc_ref[...] += lax.dot_general(a_ref[...], b_ref[...], dims, preferred_element_type=F32)

        @pl.when(kk == nk - 1)
        def _():
            o_ref[...] = acc_ref[...].astype(o_ref.dtype)

    if mode == "nn":
        a_spec = pl.BlockSpec((tm, tk), lambda i, j, kk: (i, kk))
        b_spec = pl.BlockSpec((tk, tn), lambda i, j, kk: (kk, j))
    elif mode == "nt":
        a_spec = pl.BlockSpec((tm, tk), lambda i, j, kk: (i, kk))
        b_spec = pl.BlockSpec((tn, tk), lambda i, j, kk: (j, kk))
    else:
        a_spec = pl.BlockSpec((tk, tm), lambda i, j, kk: (kk, i))
        b_spec = pl.BlockSpec((tk, tn), lambda i, j, kk: (kk, j))
    return pl.pallas_call(
        kern, name=name, grid=(m // tm, n // tn, nk),
        in_specs=[a_spec, b_spec], out_specs=pl.BlockSpec((tm, tn), lambda i, j, kk: (i, j)),
        out_shape=jax.ShapeDtypeStruct((m, n), out_dtype),
        scratch_shapes=[pltpu.VMEM((tm, tn), F32)],
        compiler_params=_params(("parallel", "parallel", "arbitrary")),
    )(a, b)


def _rowcall(body, *, name, T, S, tr, rows, exs=(), pars=(), row_outs=(), ex_outs=(), par_outs=(), aliases=None):
    assert S % tr == 0 and T % S == 0
    per_ex = S // tr
    nb = T // S
    n_rows, n_exs, n_pars = len(rows), len(exs), len(pars)
    n_ro, n_eo, n_po = len(row_outs), len(ex_outs), len(par_outs)

    def kern(*refs):
        ins = refs[:n_rows + n_exs + n_pars]
        outs = refs[n_rows + n_exs + n_pars:]
        rv = [r[...] for r in ins[:n_rows]]
        ev = [r[0] for r in ins[n_rows:n_rows + n_exs]]
        pv = [r[...] for r in ins[n_rows + n_exs:]]
        ro, eo, po = body(rv, ev, pv)
        i = pl.program_id(0)
        for ref, val in zip(outs[:n_ro], ro):
            if isinstance(val, (list, tuple)):
                off = 0
                for piece in val:
                    w = piece.shape[-1]
                    ref[:, off:off + w] = piece.astype(ref.dtype)
                    off += w
            else:
                ref[...] = val.astype(ref.dtype)
        first_of_ex = (i % per_ex) == 0
        for ref, val in zip(outs[n_ro:n_ro + n_eo], eo):
            @pl.when(first_of_ex)
            def _(ref=ref, val=val):
                ref[0] = val

            @pl.when(jnp.logical_not(first_of_ex))
            def _(ref=ref, val=val):
                ref[0] += val
        for ref, val in zip(outs[n_ro + n_eo:], po):
            @pl.when(i == 0)
            def _(ref=ref, val=val):
                ref[...] = val

            @pl.when(i != 0)
            def _(ref=ref, val=val):
                ref[...] += val

    in_specs = [pl.BlockSpec((tr, w), functools.partial(lambda i, cb: (i, cb), cb=cb)) for (_, w, cb) in rows]
    in_specs += [pl.BlockSpec((1, 1, e.shape[-1]), lambda i: (i // per_ex, 0, 0)) for e in exs]
    in_specs += [pl.BlockSpec(p.shape, functools.partial(lambda i, nd: (0,) * nd, nd=p.ndim)) for p in pars]
    out_specs = [pl.BlockSpec((tr, w), functools.partial(lambda i, cb: (i, cb), cb=cb)) for (_, _, w, cb) in row_outs]
    out_specs += [pl.BlockSpec((1, 1, f), lambda i: (i // per_ex, 0, 0)) for f in ex_outs]
    out_specs += [pl.BlockSpec(tuple(s), functools.partial(lambda i, nd: (0,) * nd, nd=len(s))) for s in par_outs]
    out_shape = [jax.ShapeDtypeStruct((T, tw), dt) for (tw, dt, _, _) in row_outs]
    out_shape += [jax.ShapeDtypeStruct((nb, 1, f), F32) for f in ex_outs]
    out_shape += [jax.ShapeDtypeStruct(tuple(s), F32) for s in par_outs]
    res = pl.pallas_call(
        kern, name=name, grid=(T // tr,), in_specs=in_specs, out_specs=out_specs, out_shape=out_shape,
        input_output_aliases=aliases or {}, compiler_params=_params(("arbitrary",)),
    )(*[r[0] for r in rows], *exs, *pars)
    return res[:n_ro], res[n_ro:n_ro + n_eo], res[n_ro + n_eo:]


def _rms(x, w, n=None):
    n = x.shape[-1] if n is None else n
    ms = jnp.sum(x * x, axis=-1, keepdims=True) * (1.0 / n)
    return x * lax.rsqrt(ms + NORM_EPS) * w


def _gelu(x):
    return 0.5 * x * (1.0 + lax.erf(x * np.float32(1.0 / math.sqrt(2.0))))


def _silu(x):
    return x * jax.nn.sigmoid(x)


@jax.custom_vjp
def _rope(x, cos, sin):
    return x * cos + pltpu.roll(x, 64, 1) * sin


def _rope_fwd(x, cos, sin):
    return _rope(x, cos, sin), (cos, sin)


def _rope_bwd(res, dy):
    cos, sin = res
    return dy * cos + pltpu.roll(dy * sin, 64, 1), None, None


_rope.defvjp(_rope_fwd, _rope_bwd)


def _heads(x, n):
    return [x[:, h * LANES:(h + 1) * LANES] for h in range(n)]


def _f_norm_mod(x, w, sc, sh):
    return _rms(x, w) * (1.0 + sc) + sh


def _f_resid_norm_mod(xa, delta, g, w, sc, sh):
    xn = xa + g * delta
    return xn, _f_norm_mod(xn, w, sc, sh)


def _f_mixprep(proj, ca, sa, cb, sb, qnw, kvnw):
    qa = [_rope(p, ca, sa) for p in _heads(proj[:, P_AQ:P_AK], A_Q_HEADS)]
    ka = [_rope(p, ca, sa) for p in _heads(proj[:, P_AK:P_AV], A_KV_HEADS)]
    va = proj[:, P_AV:P_CQ]
    cqn = _rms(proj[:, P_CQ:P_CKV], qnw)
    ckvn = _rms(proj[:, P_CKV:P_KR], kvnw)
    kr = _rope(proj[:, P_KR:P_CU], cb, sb)
    return jnp.concatenate(qa, -1), jnp.concatenate(ka, -1), va, cqn, ckvn, kr


def _f_mlaprep(q, kv, kr, cb, sb):
    qs = [_rope(p, cb, sb) for p in _heads(q, B_HEADS)]
    ks = [p + kr for p in _heads(kv[:, :B_HEADS * LANES], B_HEADS)]
    return jnp.concatenate(qs, -1), jnp.concatenate(ks, -1), kv[:, B_HEADS * LANES:]


def _f_sgu(cu, cv, ln_w, ln_b, w_s, b_col):
    u = _gelu(cu)
    v = _gelu(cv)
    mu = jnp.mean(v, axis=-1, keepdims=True)
    var = jnp.mean(jnp.square(v - mu), axis=-1, keepdims=True)
    vn = (v - mu) * lax.rsqrt(var + NORM_EPS) * ln_w + ln_b
    r = lax.broadcasted_iota(jnp.int32, (C_CHUNK, C_CHUNK), 0)
    c = lax.broadcasted_iota(jnp.int32, (C_CHUNK, C_CHUNK), 1)
    lane = lax.broadcasted_iota(jnp.int32, (1, C_WIDTH), 1)
    mixed = jnp.zeros(cu.shape, F32)
    for g in range(C_GROUPS):
        gm = (lane // C_GROUP_DIM == g).astype(F32)
        wg = jnp.where(r >= c, w_s[g], 0.0).astype(BF16)
        mixed = mixed + jnp.dot(wg, (vn * gm).astype(BF16), preferred_element_type=F32) + b_col[g] * gm
    return u * mixed


def _f_outnorm(oa, ob, yc, gw):
    ya = _rms(oa, gw[:, Y_A:Y_B], A_Q_HEADS * HEAD_DIM)
    yb = _rms(ob, gw[:, Y_B:Y_C], B_HEADS * B_V)
    ycn = _rms(yc, gw[:, Y_C:Y_END])
    return jnp.concatenate([ya, yb, ycn], -1)


def _f_swiglu(gate, up):
    return _silu(gate) * up


def _mask(q_start, k_start, tq, tk, window):
    qpos = q_start + lax.broadcasted_iota(jnp.int32, (tq, tk), 0)
    kpos = k_start + lax.broadcasted_iota(jnp.int32, (tq, tk), 1)
    m = kpos <= qpos
    if window is not None:
        m = jnp.logical_and(m, qpos - kpos < window)
    return m


def _tile_fwd(qv, kk, vv, q_start, k_start, n_free, scale, window, m0, l0):
    tq = qv.shape[0]
    W = kk.shape[0]
    parts = []
    if n_free > 0:
        parts.append((lax.dot_general(qv, kk[:n_free], _DIMS["nt"], preferred_element_type=F32) * scale, vv[:n_free]))
    if W > n_free:
        s = lax.dot_general(qv, kk[n_free:], _DIMS["nt"], preferred_element_type=F32) * scale
        s = jnp.where(_mask(q_start, k_start + n_free, tq, W - n_free, window), s, NEG_INF)
        parts.append((s, vv[n_free:]))
    m = m0
    for s, _ in parts:
        mx = jnp.max(s, axis=-1, keepdims=True)
        m = mx if m is None else jnp.maximum(m, mx)
    l = None if l0 is None else l0 * jnp.exp(m0 - m)
    o = None
    for s, vpart in parts:
        p = jnp.exp(s - m)
        ps = jnp.sum(p, axis=-1, keepdims=True)
        l = ps if l is None else l + ps
        po = jnp.dot(p.astype(BF16), vpart, preferred_element_type=F32)
        o = po if o is None else o + po
    return o / l, m + jnp.log(l)


def _tile_bwd(qv, kk, vv, dof, ov, lse, q_start, k_start, n_free, scale, window):
    tq = qv.shape[0]
    W = kk.shape[0]
    dob = dof.astype(BF16)
    delta = jnp.sum(dof * ov, axis=-1, keepdims=True)
    dq = None
    outs = []
    for (a, b, masked) in ((0, n_free, False), (n_free, W, True)):
        if b <= a:
            continue
        kp, vp = kk[a:b], vv[a:b]
        s = lax.dot_general(qv, kp, _DIMS["nt"], preferred_element_type=F32) * scale
        if masked:
            s = jnp.where(_mask(q_start, k_start + a, tq, b - a, window), s, NEG_INF)
        p = jnp.exp(s - lse)
        dp = lax.dot_general(dob, vp, _DIMS["nt"], preferred_element_type=F32)
        ds = (p * (dp - delta) * scale).astype(BF16)
        d = jnp.dot(ds, kp, preferred_element_type=F32)
        dq = d if dq is None else dq + d
        dkp = lax.dot_general(ds, qv, _DIMS["tn"], preferred_element_type=F32)
        dvp = lax.dot_general(p.astype(BF16), dob, _DIMS["tn"], preferred_element_type=F32)
        outs.append((a, dkp, dvp))
    return dq, outs


def _attn_fwd(q, k, v, sinks, *, B, S, HQ, HK, window, scale, tq, band, name):
    G = HQ // HK
    nq = S // tq
    T = B * S
    has_sink = sinks is not None

    def kern(*refs):
        if has_sink:
            q_ref, k_ref, v_ref, s_ref, o_ref, lse_ref = refs
        else:
            q_ref, k_ref, v_ref, o_ref, lse_ref = refs
        q_start = pl.program_id(2) * tq
        qv = q_ref[...]
        if has_sink:
            m0 = jnp.broadcast_to(s_ref[0][:, :1], (tq, 1))
            l0 = jnp.ones((tq, 1), F32)
        else:
            m0 = l0 = None

        def finish(o, lse):
            o_ref[...] = o
            lse_ref[...] = jnp.broadcast_to(lse, (tq, LANES))

        if window is None:
            bidx = q_start // band
            for bb in range(S // band):
                @pl.when(bidx == bb)
                def _(bb=bb):
                    W = (bb + 1) * band
                    finish(*_tile_fwd(qv, k_ref[0:W, :], v_ref[0:W, :], q_start, 0, bb * band, scale, None, m0, l0))
        else:
            W = min(S, tq + window)
            k_start = pl.multiple_of(jnp.maximum(q_start - window, 0), window)
            finish(*_tile_fwd(qv, k_ref[pl.ds(k_start, W), :], v_ref[pl.ds(k_start, W), :], q_start, k_start, 0, scale,
                              window, m0, l0))

    q_spec = pl.BlockSpec((tq, LANES), lambda b, h, i: (b * nq + i, h))
    kv_spec = pl.BlockSpec((S, LANES), lambda b, h, i: (b, h // G))
    in_specs = [q_spec, kv_spec, kv_spec]
    args = [q, k, v]
    if has_sink:
        in_specs.append(pl.BlockSpec((1, 1, LANES), lambda b, h, i: (h, 0, 0)))
        args.append(sinks)
    return pl.pallas_call(
        kern, name=name, grid=(B, HQ, nq), in_specs=in_specs, out_specs=[q_spec, q_spec],
        out_shape=[jax.ShapeDtypeStruct((T, HQ * LANES), F32), jax.ShapeDtypeStruct((T, HQ * LANES), F32)],
        compiler_params=_params(("parallel", "parallel", "arbitrary")),
    )(*args)


def _attn_bwd(q, k, v, o, lse, do, sinks, *, B, S, HQ, HK, window, scale, tq, band, name):
    G = HQ // HK
    nq = S // tq
    T = B * S
    has_sink = sinks is not None

    def kern(*refs):
        if has_sink:
            q_ref, k_ref, v_ref, o_ref, lse_ref, do_ref, s_ref, dq_ref, dk_ref, dv_ref, ds_ref = refs
        else:
            q_ref, k_ref, v_ref, o_ref, lse_ref, do_ref, dq_ref, dk_ref, dv_ref = refs
        gi = pl.program_id(2)
        qi = pl.program_id(3)
        q_start = qi * tq

        @pl.when(jnp.logical_and(gi == 0, qi == 0))
        def _():
            dk_ref[...] = jnp.zeros_like(dk_ref)
            dv_ref[...] = jnp.zeros_like(dv_ref)

        qv = q_ref[...]
        dof = do_ref[...]
        ov = o_ref[...]
        lse_v = lse_ref[...][:, :1]
        if window is None:
            bidx = q_start // band
            for bb in range(S // band):
                @pl.when(bidx == bb)
                def _(bb=bb):
                    W = (bb + 1) * band
                    dq, outs = _tile_bwd(qv, k_ref[0:W, :], v_ref[0:W, :], dof, ov, lse_v, q_start, 0, bb * band, scale, None)
                    dq_ref[...] = dq
                    for a, dkp, dvp in outs:
                        dk_ref[a:a + dkp.shape[0], :] += dkp
                        dv_ref[a:a + dvp.shape[0], :] += dvp
        else:
            W = min(S, tq + window)
            k_start = pl.multiple_of(jnp.maximum(q_start - window, 0), window)
            dq, outs = _tile_bwd(qv, k_ref[pl.ds(k_start, W), :], v_ref[pl.ds(k_start, W), :], dof, ov, lse_v, q_start,
                                 k_start, 0, scale, window)
            dq_ref[...] = dq
            (_, dkp, dvp), = outs
            dk_ref[pl.ds(k_start, W), :] += dkp
            dv_ref[pl.ds(k_start, W), :] += dvp
        if has_sink:
            delta = jnp.sum(dof * ov, axis=-1, keepdims=True)
            sink = s_ref[0][:, :1]
            part = -jnp.sum(jnp.exp(sink - lse_v) * delta, axis=0, keepdims=True)
            part = jnp.broadcast_to(part, (1, LANES))

            @pl.when(qi == 0)
            def _():
                ds_ref[0] = part

            @pl.when(qi != 0)
            def _():
                ds_ref[0] += part

    q_spec = pl.BlockSpec((tq, LANES), lambda b, hk, g, i: (b * nq + i, hk * G + g))
    kv_spec = pl.BlockSpec((S, LANES), lambda b, hk, g, i: (b, hk))
    in_specs = [q_spec, kv_spec, kv_spec, q_spec, q_spec, q_spec]
    args = [q, k, v, o, lse, do]
    out_specs = [q_spec, kv_spec, kv_spec]
    out_shape = [jax.ShapeDtypeStruct((T, HQ * LANES), F32), jax.ShapeDtypeStruct((T, HK * LANES), F32),
                 jax.ShapeDtypeStruct((T, HK * LANES), F32)]
    if has_sink:
        in_specs.append(pl.BlockSpec((1, 1, LANES), lambda b, hk, g, i: (hk * G + g, 0, 0)))
        args.append(sinks)
        out_specs.append(pl.BlockSpec((1, 1, LANES), lambda b, hk, g, i: (b * HQ + hk * G + g, 0, 0)))
        out_shape.append(jax.ShapeDtypeStruct((B * HQ, 1, LANES), F32))
    res = pl.pallas_call(
        kern, name=name, grid=(B, HK, G, nq), in_specs=in_specs, out_specs=out_specs, out_shape=out_shape,
        compiler_params=_params(("parallel", "parallel", "arbitrary", "arbitrary")),
    )(*args)
    return res if has_sink else (*res, None)


def _rope_tables(pos_col, freq, sign, name):
    T = pos_col.shape[0]
    tr = _tile(T, 1024)

    def kern(p_ref, f_ref, s_ref, c_out, s_out):
        ang = p_ref[...] * f_ref[...]
        c_out[...] = jnp.cos(ang)
        s_out[...] = jnp.sin(ang) * s_ref[...]

    spec = pl.BlockSpec((tr, LANES), lambda i: (i, 0))
    par = pl.BlockSpec((1, LANES), lambda i: (0, 0))
    return pl.pallas_call(
        kern, name=name, grid=(T // tr,), in_specs=[pl.BlockSpec((tr, 1), lambda i: (i, 0)), par, par],
        out_specs=[spec, spec], out_shape=[jax.ShapeDtypeStruct((T, LANES), F32)] * 2,
        compiler_params=_params(("parallel",)),
    )(pos_col, freq, sign)


def _ada_fwd(c8, ada_w, ada_b):
    L, D, N = ada_w.shape
    tn = _tile(N, 1536)

    def kern(c_ref, w_ref, b_ref, o_ref):
        act = _silu(c_ref[...]).astype(BF16)
        o_ref[0] = jnp.dot(act, w_ref[0], preferred_element_type=F32) + b_ref[0]

    return pl.pallas_call(
        kern, name="ada_fwd", grid=(L, N // tn),
        in_specs=[pl.BlockSpec((8, D), lambda l, j: (0, 0)), pl.BlockSpec((1, D, tn), lambda l, j: (l, 0, j)),
                  pl.BlockSpec((1, 1, tn), lambda l, j: (l, 0, j))],
        out_specs=pl.BlockSpec((1, 8, tn), lambda l, j: (l, 0, j)),
        out_shape=jax.ShapeDtypeStruct((L, 8, N), F32), compiler_params=_params(("parallel", "parallel")),
    )(c8, ada_w, ada_b)


def _ada_bwd(c8, dmod):
    L, _, N = dmod.shape
    D = c8.shape[1]
    tn = _tile(N, 1536)

    def kern(c_ref, d_ref, gw_ref, gb_ref):
        act = _silu(c_ref[...]).astype(BF16)
        d = d_ref[0]
        gw_ref[0] = lax.dot_general(act, d.astype(BF16), _DIMS["tn"], preferred_element_type=F32)
        gb_ref[0] = jnp.sum(d, axis=0, keepdims=True)

    return pl.pallas_call(
        kern, name="ada_bwd", grid=(L, N // tn),
        in_specs=[pl.BlockSpec((8, D), lambda l, j: (0, 0)), pl.BlockSpec((1, 8, tn), lambda l, j: (l, 0, j))],
        out_specs=[pl.BlockSpec((1, D, tn), lambda l, j: (l, 0, j)), pl.BlockSpec((1, 1, tn), lambda l, j: (l, 0, j))],
        out_shape=[jax.ShapeDtypeStruct((L, D, N), F32), jax.ShapeDtypeStruct((L, 1, N), F32)],
        compiler_params=_params(("parallel", "parallel")),
    )(c8, dmod)


def _adamw(w, g, m, v, name):
    rows, cols = w.shape
    tr = _row_div(rows, 512)

    def kern(w_ref, g_ref, m_ref, v_ref, d_out, m_out, v_out):
        gv = g_ref[...]
        mn = ADAM_B1 * m_ref[...] + (1.0 - ADAM_B1) * gv
        vn = ADAM_B2 * v_ref[...] + (1.0 - ADAM_B2) * jnp.square(gv)
        m_hat = mn / (1.0 - ADAM_B1 ** ADAM_STEP)
        v_hat = vn / (1.0 - ADAM_B2 ** ADAM_STEP)
        d_out[...] = -ADAM_LR * (m_hat / (jnp.sqrt(v_hat) + ADAM_EPS) + ADAM_WD * w_ref[...])
        m_out[...] = mn
        v_out[...] = vn

    spec = pl.BlockSpec((tr, cols), lambda i: (i, 0))
    return pl.pallas_call(
        kern, name=name, grid=(rows // tr,), in_specs=[spec] * 4, out_specs=[spec] * 3,
        out_shape=[jax.ShapeDtypeStruct((rows, cols), F32)] * 3, compiler_params=_params(("parallel",)),
    )(w, g, m, v)


def _add_pair(a, b, name, out_dtypes):
    rows, cols = a.shape
    tr = _tile(rows, 1024)

    def kern(a_ref, b_ref, *outs):
        s = a_ref[...] + b_ref[...]
        for o in outs:
            o[...] = s.astype(o.dtype)

    spec = pl.BlockSpec((tr, cols), lambda i: (i, 0))
    return pl.pallas_call(
        kern, name=name, grid=(rows // tr,), in_specs=[spec, spec], out_specs=[spec] * len(out_dtypes),
        out_shape=[jax.ShapeDtypeStruct((rows, cols), dt) for dt in out_dtypes], compiler_params=_params(("parallel",)),
    )(a, b)


def _sum_slots(x, name):
    n, rows, cols = x.shape
    tr = _tile(rows, 1024)

    def kern(x_ref, o_ref):
        acc = x_ref[0].astype(F32)
        for j in range(1, n):
            acc = acc + x_ref[j].astype(F32)
        o_ref[...] = acc

    return pl.pallas_call(
        kern, name=name, grid=(rows // tr,), in_specs=[pl.BlockSpec((n, tr, cols), lambda i: (0, i, 0))],
        out_specs=pl.BlockSpec((tr, cols), lambda i: (i, 0)), out_shape=jax.ShapeDtypeStruct((rows, cols), F32),
        compiler_params=_params(("parallel",)),
    )(x)


_ANY = pl.BlockSpec(memory_space=pl.ANY)
_MESH = pl.DeviceIdType.MESH


def _chip_exchange(blob, name):
    n, R, C = blob.shape
    assert n == N_CHIPS

    def kern(src, dst, send_sems, recv_sems, local_sem):
        x, y, c = lax.axis_index("x"), lax.axis_index("y"), lax.axis_index("c")
        me = 2 * x + y
        flips = [(1, 0), (0, 1), (1, 1)]

        def peer(f):
            return (1 - x if f[0] else x, 1 - y if f[1] else y)

        own = pltpu.make_async_copy(src.at[me], dst.at[me], local_sem)
        own.start()
        copies = []
        for kk, f in enumerate(flips):
            px, py = peer(f)
            copies.append(pltpu.make_async_remote_copy(
                src_ref=src.at[2 * px + py], dst_ref=dst.at[me], send_sem=send_sems.at[kk], recv_sem=recv_sems.at[kk],
                device_id=(px, py, c), device_id_type=_MESH))
        for cp in copies:
            cp.start()
        for kk, f in enumerate(flips):
            px, py = peer(f)
            pltpu.make_async_remote_copy(
                src_ref=src.at[me], dst_ref=dst.at[2 * px + py], send_sem=send_sems.at[kk], recv_sem=recv_sems.at[kk],
                device_id=(px, py, c), device_id_type=_MESH).wait_recv()
        for cp in copies:
            cp.wait_send()
        own.wait()

    return pl.pallas_call(
        kern, name=name, in_specs=[_ANY], out_specs=_ANY, out_shape=jax.ShapeDtypeStruct((n, R, C), blob.dtype),
        scratch_shapes=[pltpu.SemaphoreType.DMA((3,)), pltpu.SemaphoreType.DMA((3,)), pltpu.SemaphoreType.DMA],
    )(blob)


def _chip_allgather(shard, name):
    R, C = shard.shape

    def kern(src, dst, send_sems, recv_sems, local_sem):
        x, y, c = lax.axis_index("x"), lax.axis_index("y"), lax.axis_index("c")
        me = 2 * x + y
        peers = [(1 - x, y), (x, 1 - y), (1 - x, 1 - y)]
        own = pltpu.make_async_copy(src, dst.at[me], local_sem)
        own.start()
        copies = [pltpu.make_async_remote_copy(
            src_ref=src, dst_ref=dst.at[me], send_sem=send_sems.at[kk], recv_sem=recv_sems.at[kk],
            device_id=(px, py, c), device_id_type=_MESH) for kk, (px, py) in enumerate(peers)]
        for cp in copies:
            cp.start()
        for kk, (px, py) in enumerate(peers):
            pltpu.make_async_remote_copy(
                src_ref=src, dst_ref=dst.at[2 * px + py], send_sem=send_sems.at[kk], recv_sem=recv_sems.at[kk],
                device_id=(px, py, c), device_id_type=_MESH).wait_recv()
        for cp in copies:
            cp.wait_send()
        own.wait()

    return pl.pallas_call(
        kern, name=name, in_specs=[_ANY], out_specs=_ANY, out_shape=jax.ShapeDtypeStruct((N_CHIPS, R, C), shard.dtype),
        scratch_shapes=[pltpu.SemaphoreType.DMA((3,)), pltpu.SemaphoreType.DMA((3,)), pltpu.SemaphoreType.DMA],
    )(shard)


def _sibling_swap(a, name):
    def kern(src, dst, send_sem, recv_sem):
        x, y, c = lax.axis_index("x"), lax.axis_index("y"), lax.axis_index("c")
        cp = pltpu.make_async_remote_copy(src_ref=src, dst_ref=dst, send_sem=send_sem, recv_sem=recv_sem,
                                          device_id=(x, y, 1 - c), device_id_type=_MESH)
        cp.start()
        cp.wait()

    return pl.pallas_call(
        kern, name=name, in_specs=[_ANY], out_specs=_ANY, out_shape=jax.ShapeDtypeStruct(a.shape, a.dtype),
        scratch_shapes=[pltpu.SemaphoreType.DMA, pltpu.SemaphoreType.DMA],
    )(a)


def _all_reduce_small(blob, name):
    R, C = blob.shape

    def kern(src, out, pair, chips, send_sems, recv_sems):
        x, y, c = lax.axis_index("x"), lax.axis_index("y"), lax.axis_index("c")
        me = 2 * x + y
        to_sibling = pltpu.make_async_remote_copy(
            src_ref=src, dst_ref=pair, send_sem=send_sems.at[0], recv_sem=recv_sems.at[0],
            device_id=(x, y, 1 - c), device_id_type=_MESH)
        to_sibling.start()
        to_sibling.wait()
        chips[me] = src[...] + pair[...]
        peers = [(1 - x, y), (x, 1 - y), (1 - x, 1 - y)]
        copies = [pltpu.make_async_remote_copy(
            src_ref=chips.at[me], dst_ref=chips.at[me], send_sem=send_sems.at[1 + kk], recv_sem=recv_sems.at[1 + kk],
            device_id=(px, py, c), device_id_type=_MESH) for kk, (px, py) in enumerate(peers)]
        for cp in copies:
            cp.start()
        for kk, (px, py) in enumerate(peers):
            pltpu.make_async_remote_copy(
                src_ref=chips.at[me], dst_ref=chips.at[2 * px + py], send_sem=send_sems.at[1 + kk],
                recv_sem=recv_sems.at[1 + kk], device_id=(px, py, c), device_id_type=_MESH).wait_recv()
        for cp in copies:
            cp.wait_send()
        acc = chips[0]
        for j in range(1, N_CHIPS):
            acc = acc + chips[j]
        out[...] = acc

    vm = pl.BlockSpec(memory_space=pltpu.VMEM)
    return pl.pallas_call(
        kern, name=name, in_specs=[vm], out_specs=vm, out_shape=jax.ShapeDtypeStruct((R, C), F32),
        scratch_shapes=[pltpu.VMEM((R, C), F32), pltpu.VMEM((N_CHIPS, R, C), F32), pltpu.SemaphoreType.DMA((4,)),
                        pltpu.SemaphoreType.DMA((4,))],
        compiler_params=pltpu.CompilerParams(vmem_limit_bytes=VMEM_LIMIT),
    )(blob)


def _row_tile(S):
    return 256 if S % 256 == 0 else 128


def _attn_tiles(S):
    return min(S, 512), min(S, 256), min(S, 1024)


def _layer_fwd(xin, prev, mods, w, tabs, B, S):
    T = B * S
    tr = _row_tile(S)
    sh1, sc1, g1, sh2, sc2, g2 = mods
    ca, sa, cb, sb = tabs
    sv = {}
    if prev is None:
        def body(rv, ev, pv):
            return [_f_norm_mod(rv[0], pv[0], ev[0], ev[1])], [], []
        (h,), _, _ = _rowcall(body, name="f_norm_mod", T=T, S=S, tr=tr, rows=[(xin, D_MODEL, 0)], exs=[sc1, sh1],
                              pars=[w["norm1_w"]], row_outs=[(D_MODEL, BF16, D_MODEL, 0)])
        x = xin
    else:
        x1p, dp, g2p = prev

        def body(rv, ev, pv):
            xn, hh = _f_resid_norm_mod(rv[0], rv[1], ev[0], pv[0], ev[1], ev[2])
            return [xn, hh], [], []
        (x, h), _, _ = _rowcall(body, name="f_resid_norm_mod1", T=T, S=S, tr=tr,
                                rows=[(x1p, D_MODEL, 0), (dp, D_MODEL, 0)], exs=[g2p, sc1, sh1], pars=[w["norm1_w"]],
                                row_outs=[(D_MODEL, F32, D_MODEL, 0), (D_MODEL, BF16, D_MODEL, 0)])
    sv["x"], sv["h"] = x, h
    proj = _matmul(h, w["w_in"], "nn", F32, "mm_in")
    sv["proj"] = proj

    def body(rv, ev, pv):
        outs = _f_mixprep(rv[0], rv[1], rv[2], rv[3], rv[4], pv[0], pv[1])
        return list(outs), [], []
    (qa, ka, va, cqn, ckvn, kr), _, _ = _rowcall(
        body, name="f_mixprep", T=T, S=S, tr=tr,
        rows=[(proj, P_CU, 0), (ca, LANES, 0), (sa, LANES, 0), (cb, LANES, 0), (sb, LANES, 0)],
        pars=[w["b_q_norm_w"], w["b_kv_norm_w"]],
        row_outs=[(768, BF16, 768, 0), (256, BF16, 256, 0), (256, BF16, 256, 0), (384, BF16, 384, 0),
                  (256, BF16, 256, 0), (LANES, F32, LANES, 0)])
    sv.update(qa=qa, ka=ka, va=va, cqn=cqn, ckvn=ckvn)
    q = _matmul(cqn, w["b_w_uq"], "nn", F32, "mm_uq")
    kv = _matmul(ckvn, w["b_w_ukv"], "nn", F32, "mm_ukv")

    def body(rv, ev, pv):
        Q, K, V = _f_mlaprep(rv[0], rv[1], rv[2], rv[3], rv[4])
        return [Q, K, V], [], []
    (Q, K, V), _, _ = _rowcall(
        body, name="f_mlaprep", T=T, S=S, tr=tr,
        rows=[(q, 768, 0), (kv, 1536, 0), (kr, LANES, 0), (cb, LANES, 0), (sb, LANES, 0)],
        row_outs=[(768, BF16, 768, 0)] * 3)
    sv.update(Q=Q, K=K, V=V)
    ta, tb, band = _attn_tiles(S)
    oa, lse_a = _attn_fwd(qa, ka, va, w["sinks"], B=B, S=S, HQ=A_Q_HEADS, HK=A_KV_HEADS, window=A_WINDOW,
                          scale=HEAD_DIM ** -0.5, tq=ta, band=None, name="attn_a_fwd")
    ob, lse_b = _attn_fwd(Q, K, V, None, B=B, S=S, HQ=B_HEADS, HK=B_HEADS, window=None,
                          scale=(B_NOPE + B_ROPE) ** -0.5, tq=tb, band=band, name="attn_b_fwd")
    sv.update(oa=oa, lse_a=lse_a, ob=ob, lse_b=lse_b)

    def body(rv, ev, pv):
        return [_f_sgu(rv[0], rv[1], pv[0], pv[1], pv[2], pv[3])], [], []
    (yc,), _, _ = _rowcall(body, name="f_sgu", T=T, S=S, tr=C_CHUNK,
                           rows=[(proj, C_WIDTH, P_CU // C_WIDTH), (proj, C_WIDTH, P_CV // C_WIDTH)],
                           pars=[w["c_ln_w"], w["c_ln_b"], w["c_w_s"], w["c_b_col"]],
                           row_outs=[(C_WIDTH, F32, C_WIDTH, 0)])
    sv["yc"] = yc

    def body(rv, ev, pv):
        return [_f_outnorm(rv[0], rv[1], rv[2], pv[0])], [], []
    (y,), _, _ = _rowcall(body, name="f_outnorm", T=T, S=S, tr=tr,
                          rows=[(oa, 768, 0), (ob, 768, 0), (yc, C_WIDTH, 0)], pars=[w["out_norm_w"]],
                          row_outs=[(Y_END, BF16, Y_END, 0)])
    sv["y"] = y
    o = _matmul(y, w["w_out"], "nn", F32, "mm_out")
    sv["o"] = o

    def body(rv, ev, pv):
        xn, hh = _f_resid_norm_mod(rv[0], rv[1], ev[0], pv[0], ev[1], ev[2])
        return [xn, hh], [], []
    (x1, h2), _, _ = _rowcall(body, name="f_resid_norm_mod2", T=T, S=S, tr=tr,
                              rows=[(x, D_MODEL, 0), (o, D_MODEL, 0)], exs=[g1, sc2, sh2], pars=[w["norm2_w"]],
                              row_outs=[(D_MODEL, F32, D_MODEL, 0), (D_MODEL, BF16, D_MODEL, 0)])
    sv["h2"] = h2
    gu = _matmul(h2, w["w_gate_up"], "nn", BF16, "mm_gate_up")
    sv["gu"] = gu

    def body(rv, ev, pv):
        return [_f_swiglu(rv[0].astype(F32), rv[1].astype(F32))], [], []
    (act,), _, _ = _rowcall(body, name="f_swiglu", T=T, S=S, tr=tr,
                            rows=[(gu, FFN_HIDDEN, 0), (gu, FFN_HIDDEN, 1)], row_outs=[(FFN_HIDDEN, BF16, FFN_HIDDEN, 0)])
    sv["act"] = act
    d = _matmul(act, w["w_down"], "nn", F32, "mm_down")
    sv["x1"], sv["d"] = x1, d
    return (x1, d, g2), sv


def _final(x1, d, g2, fw, target, B, S):
    T = B * S
    tr = _row_tile(S)

    def loss_fn(x1v, dv, gv, wv, tv):
        yf = _rms(x1v + gv * dv, wv)
        return 0.5 * jnp.sum(jnp.mean(jnp.square(yf - tv), axis=-1))

    def body(rv, ev, pv):
        x1v, dv, tv = rv
        val, vjp = jax.vjp(lambda a, b_, g, ww: loss_fn(a, b_, g, ww, tv), x1v, dv, ev[0], pv[0])
        dx1, dd, dg, dw = vjp(jnp.ones((), F32))
        return [dx1, dd], [dg], [dw, jnp.full((1, LANES), val, F32)]
    (dx1, dd), (dg2,), (dfw, loss) = _rowcall(
        body, name="final_loss", T=T, S=S, tr=tr, rows=[(x1, D_MODEL, 0), (d, D_MODEL, 0), (target, D_MODEL, 0)],
        exs=[g2], pars=[fw], row_outs=[(D_MODEL, F32, D_MODEL, 0), (D_MODEL, BF16, D_MODEL, 0)],
        ex_outs=[D_MODEL], par_outs=[(1, D_MODEL), (1, LANES)])
    return loss, dx1, dd, dg2, dfw


def _layer_bwd(sv, prev, mods, w, tabs, dx1, dd, B, S):
    T = B * S
    tr = _row_tile(S)
    sh1, sc1, g1, sh2, sc2, g2 = mods
    ca, sa, cb, sb = tabs
    gr = {}
    dact = _matmul(dd, w["w_down"], "nt", BF16, "mm_down_dx")
    gr["w_down"] = _matmul(sv["act"], dd, "tn", F32, "mm_down_dw")

    def body(rv, ev, pv):
        _, vjp = jax.vjp(_f_swiglu, rv[0].astype(F32), rv[1].astype(F32))
        dgate, dup = vjp(rv[2].astype(F32))
        return [[dgate, dup]], [], []
    (dgu,), _, _ = _rowcall(body, name="b_swiglu", T=T, S=S, tr=tr,
                            rows=[(sv["gu"], FFN_HIDDEN, 0), (sv["gu"], FFN_HIDDEN, 1), (dact, FFN_HIDDEN, 0)],
                            row_outs=[(2 * FFN_HIDDEN, BF16, 2 * FFN_HIDDEN, 0)])
    dh2 = _matmul(dgu, w["w_gate_up"], "nt", F32, "mm_gate_up_dx")
    gr["w_gate_up"] = _matmul(sv["h2"], dgu, "tn", F32, "mm_gate_up_dw")

    def body(rv, ev, pv):
        xa, delta, dh, dxn = rv
        _, vjp = jax.vjp(_f_resid_norm_mod, xa, delta, ev[0], pv[0], ev[1], ev[2])
        dxa, ddelta, dg, dw, dsc, dsh = vjp((dxn, dh))
        return [dxa, ddelta], [dg, dsc, dsh], [dw]
    (dx, do), (dg1, dsc2, dsh2), (gr["norm2_w"],) = _rowcall(
        body, name="b_resid_norm_mod2", T=T, S=S, tr=tr,
        rows=[(sv["x"], D_MODEL, 0), (sv["o"], D_MODEL, 0), (dh2, D_MODEL, 0), (dx1, D_MODEL, 0)],
        exs=[g1, sc2, sh2], pars=[w["norm2_w"]],
        row_outs=[(D_MODEL, F32, D_MODEL, 0), (D_MODEL, BF16, D_MODEL, 0)], ex_outs=[D_MODEL] * 3,
        par_outs=[(1, D_MODEL)])
    dy = _matmul(do, w["w_out"], "nt", F32, "mm_out_dx")
    gr["w_out"] = _matmul(sv["y"], do, "tn", F32, "mm_out_dw")

    def body(rv, ev, pv):
        _, vjp = jax.vjp(_f_outnorm, rv[0], rv[1], rv[2], pv[0])
        doa, dob, dyc, dgw = vjp(rv[3])
        return [doa, dob, dyc], [], [dgw]
    (doa, dob, dyc), _, (gr["out_norm_w"],) = _rowcall(
        body, name="b_outnorm", T=T, S=S, tr=tr,
        rows=[(sv["oa"], 768, 0), (sv["ob"], 768, 0), (sv["yc"], C_WIDTH, 0), (dy, Y_END, 0)], pars=[w["out_norm_w"]],
        row_outs=[(768, F32, 768, 0), (768, F32, 768, 0), (C_WIDTH, F32, C_WIDTH, 0)], par_outs=[(1, Y_END)])

    ta, tb, band = _attn_tiles(S)
    dQ, dK, dV, _ = _attn_bwd(sv["Q"], sv["K"], sv["V"], sv["ob"], sv["lse_b"], dob, None, B=B, S=S, HQ=B_HEADS,
                              HK=B_HEADS, window=None, scale=(B_NOPE + B_ROPE) ** -0.5, tq=tb, band=band,
                              name="attn_b_bwd")
    dqa, dka, dva, dsink = _attn_bwd(sv["qa"], sv["ka"], sv["va"], sv["oa"], sv["lse_a"], doa, w["sinks"], B=B, S=S,
                                     HQ=A_Q_HEADS, HK=A_KV_HEADS, window=A_WINDOW, scale=HEAD_DIM ** -0.5, tq=ta,
                                     band=None, name="attn_a_bwd")
    gr["sinks"] = dsink

    def body(rv, ev, pv):
        dQv, dKv, dVv, cbv, sbv = rv
        dq = [_rope_bwd((cbv, sbv), p)[0] for p in _heads(dQv, B_HEADS)]
        dkr = None
        for p in _heads(dKv, B_HEADS):
            dkr = p if dkr is None else dkr + p
        return [dq, [dKv, dVv], dkr], [], []
    (dq, dkv, dkr), _, _ = _rowcall(
        body, name="b_mlaprep", T=T, S=S, tr=tr,
        rows=[(dQ, 768, 0), (dK, 768, 0), (dV, 768, 0), (cb, LANES, 0), (sb, LANES, 0)],
        row_outs=[(768, BF16, 768, 0), (1536, BF16, 1536, 0), (LANES, F32, LANES, 0)])
    dcqn = _matmul(dq, w["b_w_uq"], "nt", F32, "mm_uq_dx")
    gr["b_w_uq"] = _matmul(sv["cqn"], dq, "tn", F32, "mm_uq_dw")
    dckvn = _matmul(dkv, w["b_w_ukv"], "nt", F32, "mm_ukv_dx")
    gr["b_w_ukv"] = _matmul(sv["ckvn"], dkv, "tn", F32, "mm_ukv_dw")

    def body(rv, ev, pv):
        proj, cav, sav, cbv, sbv, dqa_, dka_, dva_, dcqn_, dckvn_, dkr_ = rv
        _, vjp = jax.vjp(lambda p, a, b_: _f_mixprep(p, cav, sav, cbv, sbv, a, b_), proj, pv[0], pv[1])
        dproj, dqn, dkvn = vjp((dqa_, dka_, dva_, dcqn_, dckvn_, dkr_))
        return [dproj], [], [dqn, dkvn]
    (dproj,), _, (gr["b_q_norm_w"], gr["b_kv_norm_w"]) = _rowcall(
        body, name="b_mixprep", T=T, S=S, tr=tr,
        rows=[(sv["proj"], P_CU, 0), (ca, LANES, 0), (sa, LANES, 0), (cb, LANES, 0), (sb, LANES, 0),
              (dqa, 768, 0), (dka, 256, 0), (dva, 256, 0), (dcqn, 384, 0), (dckvn, 256, 0), (dkr, LANES, 0)],
        pars=[w["b_q_norm_w"], w["b_kv_norm_w"]], row_outs=[(P_END, BF16, P_CU, 0)],
        par_outs=[(1, B_Q_RANK), (1, B_KV_RANK)])

    def body(rv, ev, pv):
        cu, cv, dycv, _ = rv
        _, vjp = jax.vjp(_f_sgu, cu, cv, pv[0], pv[1], pv[2], pv[3])
        dcu, dcv, dlw, dlb, dws, dbc = vjp(dycv)
        return [[dcu, dcv]], [], [dlw, dlb, dws, dbc]
    (dproj,), _, (gr["c_ln_w"], gr["c_ln_b"], gr["c_w_s"], gr["c_b_col"]) = _rowcall(
        body, name="b_sgu", T=T, S=S, tr=C_CHUNK,
        rows=[(sv["proj"], C_WIDTH, P_CU // C_WIDTH), (sv["proj"], C_WIDTH, P_CV // C_WIDTH), (dyc, C_WIDTH, 0),
              (dproj, 2 * C_WIDTH, P_CU // (2 * C_WIDTH))],
        pars=[w["c_ln_w"], w["c_ln_b"], w["c_w_s"], w["c_b_col"]],
        row_outs=[(P_END, BF16, 2 * C_WIDTH, P_CU // (2 * C_WIDTH))],
        par_outs=[(1, C_WIDTH), (1, C_WIDTH), (C_GROUPS, C_CHUNK, C_CHUNK), (C_GROUPS, C_CHUNK, 1)],
        aliases={3: 0})
    dh = _matmul(dproj, w["w_in"], "nt", F32, "mm_in_dx")
    gr["w_in"] = _matmul(sv["h"], dproj, "tn", F32, "mm_in_dw")

    if prev is None:
        def body(rv, ev, pv):
            xv, dhv, dxd = rv
            _, vjp = jax.vjp(_f_norm_mod, xv, pv[0], ev[0], ev[1])
            dxa, dw, dsc, dsh = vjp(dhv)
            return [dxa + dxd], [dsc, dsh], [dw]
        (dxin,), (dsc1, dsh1), (gr["norm1_w"],) = _rowcall(
            body, name="b_norm_mod", T=T, S=S, tr=tr, rows=[(sv["x"], D_MODEL, 0), (dh, D_MODEL, 0), (dx, D_MODEL, 0)],
            exs=[sc1, sh1], pars=[w["norm1_w"]], row_outs=[(D_MODEL, F32, D_MODEL, 0)], ex_outs=[D_MODEL] * 2,
            par_outs=[(1, D_MODEL)])
        nxt = (dxin, None, None)
    else:
        x1p, dp, g2p = prev

        def body(rv, ev, pv):
            xa, delta, dhv, dxn = rv
            _, vjp = jax.vjp(_f_resid_norm_mod, xa, delta, ev[0], pv[0], ev[1], ev[2])
            dxa, ddelta, dg, dw, dsc, dsh = vjp((dxn, dhv))
            return [dxa, ddelta], [dg, dsc, dsh], [dw]
        (dx1p, ddp), (dg2p, dsc1, dsh1), (gr["norm1_w"],) = _rowcall(
            body, name="b_resid_norm_mod1", T=T, S=S, tr=tr,
            rows=[(x1p, D_MODEL, 0), (dp, D_MODEL, 0), (dh, D_MODEL, 0), (dx, D_MODEL, 0)],
            exs=[g2p, sc1, sh1], pars=[w["norm1_w"]],
            row_outs=[(D_MODEL, F32, D_MODEL, 0), (D_MODEL, BF16, D_MODEL, 0)], ex_outs=[D_MODEL] * 3,
            par_outs=[(1, D_MODEL)])
        nxt = (dx1p, ddp, dg2p)
    return gr, (dsh1, dsc1, dg1, dsh2, dsc2), nxt


def _lane_table(lanes_neg, lanes_pos, inv):
    freq = np.zeros((LANES,), np.int64) - 1
    sign = np.zeros((1, LANES), np.float32)
    n = len(lanes_neg)
    freq[lanes_neg] = np.arange(n)
    freq[lanes_pos] = np.arange(n)
    sign[0, lanes_neg] = -1.0
    sign[0, lanes_pos] = 1.0
    return _select_axis(inv, freq, 0).reshape(1, LANES), jnp.asarray(sign)


def _local_step(x, c, positions, target, wf):
    B, S, D = x.shape
    T = B * S
    xt = x.reshape(T, D)
    tgt = target.reshape(T, D)
    pos_col = positions.astype(F32).reshape(T, 1)
    inv_a = 1.0 / (ROPE_THETA ** (jnp.arange(0, HEAD_DIM, 2, dtype=F32) / HEAD_DIM))
    inv_b = 1.0 / (ROPE_THETA ** (jnp.arange(0, B_ROPE, 2, dtype=F32) / B_ROPE))
    fa, sga = _lane_table(np.arange(32), 64 + np.arange(32), inv_a)
    fb, sgb = _lane_table(48 + np.arange(16), 112 + np.arange(16), inv_b)
    ca, sa = _rope_tables(pos_col, fa, sga, "rope_a")
    cb, sb = _rope_tables(pos_col, fb, sgb, "rope_b")
    tabs = (ca, sa, cb, sb)

    c8 = jnp.zeros((8, D), F32).at[:B].set(c)
    mod = _ada_fwd(c8, wf["ada_w"], wf["ada_b"].reshape(DEPTH, 1, N_MOD * D))

    def mods_of(l):
        return tuple(mod[l, :B, i * D:(i + 1) * D].reshape(B, 1, D) for i in range(N_MOD))

    def layer_w(l):
        return {
            "norm1_w": wf["norm1_w"][l].reshape(1, D), "w_in": wf["w_in"][l],
            "sinks": jnp.broadcast_to(wf["a_sinks"][l].reshape(A_Q_HEADS, 1, 1), (A_Q_HEADS, 1, LANES)),
            "b_q_norm_w": wf["b_q_norm_w"][l].reshape(1, B_Q_RANK), "b_w_uq": wf["b_w_uq"][l],
            "b_kv_norm_w": wf["b_kv_norm_w"][l].reshape(1, B_KV_RANK), "b_w_ukv": wf["b_w_ukv"][l],
            "c_ln_w": wf["c_ln_w"][l].reshape(1, C_WIDTH), "c_ln_b": wf["c_ln_b"][l].reshape(1, C_WIDTH),
            "c_w_s": wf["c_w_s"][l], "c_b_col": wf["c_b_s"][l].reshape(C_GROUPS, C_CHUNK, 1),
            "out_norm_w": wf["out_norm_w"][l].reshape(1, Y_END), "w_out": wf["w_out"][l],
            "norm2_w": wf["norm2_w"][l].reshape(1, D), "w_gate_up": wf["w_gate_up"][l], "w_down": wf["w_down"][l],
        }

    saved, prevs = [], []
    prev = None
    for l in range(DEPTH):
        prevs.append(prev)
        prev, sv = _layer_fwd(xt, prev, mods_of(l), layer_w(l), tabs, B, S)
        saved.append(sv)
    x1, d, g2 = prev
    loss, dx1, dd, dg2, dfw = _final(x1, d, g2, wf["final_norm_w"].reshape(1, D), tgt, B, S)

    grads = [None] * DEPTH
    dmods = [None] * DEPTH
    for l in reversed(range(DEPTH)):
        gr, (dsh1, dsc1, dg1, dsh2, dsc2), nxt = _layer_bwd(saved[l], prevs[l], mods_of(l), layer_w(l), tabs, dx1, dd, B, S)
        dmods[l] = jnp.concatenate([dsh1, dsc1, dg1, dsh2, dsc2, dg2], axis=-1).reshape(B, N_MOD * D)
        grads[l] = gr
        dx1, dd, dg2 = nxt
    grad_x = dx1.reshape(B, S, D)
    dmod8 = jnp.zeros((DEPTH, 8, N_MOD * D), F32).at[:, :B].set(jnp.stack(dmods))
    g_ada_w, g_ada_b = _ada_bwd(c8, dmod8)

    def stack(k):
        return jnp.stack([grads[l][k] for l in range(DEPTH)])
    out = {
        "ada_w": g_ada_w, "ada_b": g_ada_b.reshape(DEPTH, N_MOD * D),
        "norm1_w": stack("norm1_w").reshape(DEPTH, D), "w_in": stack("w_in"),
        "a_sinks": stack("sinks")[:, :, 0, 0].reshape(DEPTH, B, A_Q_HEADS).sum(axis=1),
        "b_q_norm_w": stack("b_q_norm_w").reshape(DEPTH, B_Q_RANK), "b_w_uq": stack("b_w_uq"),
        "b_kv_norm_w": stack("b_kv_norm_w").reshape(DEPTH, B_KV_RANK), "b_w_ukv": stack("b_w_ukv"),
        "c_ln_w": stack("c_ln_w").reshape(DEPTH, C_WIDTH), "c_ln_b": stack("c_ln_b").reshape(DEPTH, C_WIDTH),
        "c_w_s": stack("c_w_s"), "c_b_s": stack("c_b_col").reshape(DEPTH, C_GROUPS, C_CHUNK),
        "out_norm_w": stack("out_norm_w").reshape(DEPTH, Y_END), "w_out": stack("w_out"),
        "norm2_w": stack("norm2_w").reshape(DEPTH, D), "w_gate_up": stack("w_gate_up"), "w_down": stack("w_down"),
        "final_norm_w": dfw.reshape(D),
    }
    return loss[0, 0], grad_x, out


SHARDED = ("ada_w", "w_in", "b_w_uq", "b_w_ukv", "w_out", "w_gate_up", "w_down")
ROW_SHARDED = ("w_out", "w_down")
SMALL = ("ada_b", "norm1_w", "a_sinks", "b_q_norm_w", "b_kv_norm_w", "c_ln_w", "c_ln_b", "c_w_s", "c_b_s",
         "out_norm_w", "norm2_w", "final_norm_w")
BLOB_COLS = 1024
BLOB_ROW_MULT = 256


def _pack(arrs, cols, mult):
    flat = jnp.concatenate([a.reshape(-1) for a in arrs])
    n = flat.shape[0]
    rows = -(-n // cols)
    rows = -(-rows // mult) * mult
    return jnp.pad(flat, (0, rows * cols - n)).reshape(rows, cols)


def _unpack(blob, shapes):
    flat = blob.reshape(-1)
    out, off = [], 0
    for s in shapes:
        n = int(np.prod(s))
        out.append(flat[off:off + n].reshape(s))
        off += n
    return out


def _gather_weights(shards):
    shapes = [shards[k].shape for k in SHARDED]
    blob = _pack([shards[k].astype(BF16) for k in SHARDED], BLOB_COLS, BLOB_ROW_MULT)
    full = _chip_allgather(blob, "gather_weights")
    parts = [_unpack(full[j], shapes) for j in range(N_CHIPS)]
    out = {}
    for i, k in enumerate(SHARDED):
        axis = 1 if k in ROW_SHARDED else 2
        out[k] = jnp.concatenate([parts[j][i] for j in range(N_CHIPS)], axis=axis)
    return out


def _reduce_sharded(gfull, chip, core):
    shard_shapes = {}
    chunks = []
    for j in range(N_CHIPS):
        arrs = []
        for k in SHARDED:
            g = gfull[k]
            axis = 1 if k in ROW_SHARDED else 2
            n = g.shape[axis] // N_CHIPS
            piece = lax.slice_in_dim(g, j * n, (j + 1) * n, axis=axis)
            shard_shapes[k] = piece.shape
            arrs.append(piece)
        chunks.append(_pack(arrs, BLOB_COLS, 2 * BLOB_ROW_MULT))
    blob = jnp.stack(chunks)
    R = blob.shape[1]
    half = R // 2
    mine = lax.dynamic_slice_in_dim(blob, core * half, half, axis=1).reshape(N_CHIPS * half, BLOB_COLS)
    theirs = lax.dynamic_slice_in_dim(blob, (1 - core) * half, half, axis=1).reshape(N_CHIPS * half, BLOB_COLS)
    got = _sibling_swap(theirs, "grad_pair_swap")
    (pair,) = _add_pair(mine, got, "grad_pair_sum", [BF16])
    landed = _chip_exchange(pair.reshape(N_CHIPS, half, BLOB_COLS), "grad_chip_exchange")
    total_half = _sum_slots(landed, "grad_chip_sum")
    other_half = _sibling_swap(total_half, "grad_half_swap")
    total = jnp.where(core == 0, jnp.concatenate([total_half, other_half]), jnp.concatenate([other_half, total_half]))
    vals = _unpack(total, [shard_shapes[k] for k in SHARDED])
    return dict(zip(SHARDED, vals))


def _pad_weights(full, small):
    i_in, i_uq, i_ukv, i_out = _map_w_in(), _map_w_uq(), _map_w_ukv(), _map_w_out()
    wf = dict(small)
    wf["ada_w"] = full["ada_w"]
    wf["w_in"] = _pad_axis(full["w_in"], i_in, 2)
    wf["b_w_uq"] = _pad_axis(full["b_w_uq"], i_uq, 2)
    wf["b_w_ukv"] = _pad_axis(full["b_w_ukv"], i_ukv, 2)
    wf["w_out"] = _pad_axis(full["w_out"], i_out, 1)
    wf["out_norm_w"] = _pad_axis(small["out_norm_w"], i_out, 1)
    wf["w_gate_up"] = full["w_gate_up"]
    wf["w_down"] = full["w_down"]
    return wf


def _unpad_grads(gp):
    i_in, i_uq, i_ukv, i_out = _map_w_in(), _map_w_uq(), _map_w_ukv(), _map_w_out()
    gfull = {
        "ada_w": gp["ada_w"], "w_in": _unpad_axis(gp["w_in"], i_in, IN_COLS, 2),
        "b_w_uq": _unpad_axis(gp["b_w_uq"], i_uq, B_HEADS * (B_NOPE + B_ROPE), 2),
        "b_w_ukv": _unpad_axis(gp["b_w_ukv"], i_ukv, B_HEADS * (B_NOPE + B_V), 2),
        "w_out": _unpad_axis(gp["w_out"], i_out, D_MODEL, 1), "w_gate_up": gp["w_gate_up"], "w_down": gp["w_down"],
    }
    gsmall = {k: gp[k] for k in SMALL}
    gsmall["out_norm_w"] = _unpad_axis(gp["out_norm_w"], i_out, D_MODEL, 1)
    return gfull, gsmall


def kernel(x, c, positions, ada_w, ada_b, norm1_w, w_in, a_sinks, b_q_norm_w, b_w_uq, b_kv_norm_w, b_w_ukv, c_ln_w, c_ln_b, c_w_s, c_b_s, out_norm_w, w_out, norm2_w, w_gate_up, w_down, final_norm_w, loss_target, m_ada_w, m_ada_b, m_norm1_w, m_w_in, m_a_sinks, m_b_q_norm_w, m_b_w_uq, m_b_kv_norm_w, m_b_w_ukv, m_c_ln_w, m_c_ln_b, m_c_w_s, m_c_b_s, m_out_norm_w, m_w_out, m_norm2_w, m_w_gate_up, m_w_down, m_final_norm_w, v_ada_w, v_ada_b, v_norm1_w, v_w_in, v_a_sinks, v_b_q_norm_w, v_b_w_uq, v_b_kv_norm_w, v_b_w_ukv, v_c_ln_w, v_c_ln_b, v_c_w_s, v_c_b_s, v_out_norm_w, v_w_out, v_norm2_w, v_w_gate_up, v_w_down, v_final_norm_w):
    names = ("ada_w", "ada_b", "norm1_w", "w_in", "a_sinks", "b_q_norm_w", "b_w_uq", "b_kv_norm_w", "b_w_ukv", "c_ln_w",
             "c_ln_b", "c_w_s", "c_b_s", "out_norm_w", "w_out", "norm2_w", "w_gate_up", "w_down", "final_norm_w")
    ws = dict(zip(names, (ada_w, ada_b, norm1_w, w_in, a_sinks, b_q_norm_w, b_w_uq, b_kv_norm_w, b_w_ukv, c_ln_w, c_ln_b,
                          c_w_s, c_b_s, out_norm_w, w_out, norm2_w, w_gate_up, w_down, final_norm_w)))
    ms = dict(zip(names, (m_ada_w, m_ada_b, m_norm1_w, m_w_in, m_a_sinks, m_b_q_norm_w, m_b_w_uq, m_b_kv_norm_w, m_b_w_ukv,
                          m_c_ln_w, m_c_ln_b, m_c_w_s, m_c_b_s, m_out_norm_w, m_w_out, m_norm2_w, m_w_gate_up, m_w_down,
                          m_final_norm_w)))
    vs = dict(zip(names, (v_ada_w, v_ada_b, v_norm1_w, v_w_in, v_a_sinks, v_b_q_norm_w, v_b_w_uq, v_b_kv_norm_w, v_b_w_ukv,
                          v_c_ln_w, v_c_ln_b, v_c_w_s, v_c_b_s, v_out_norm_w, v_w_out, v_norm2_w, v_w_gate_up, v_w_down,
                          v_final_norm_w)))
    chip = 2 * lax.axis_index("x") + lax.axis_index("y")
    core = lax.axis_index("c")

    full = _gather_weights({k: ws[k] for k in SHARDED})
    wf = _pad_weights(full, {k: ws[k] for k in SMALL})
    loss_local, grad_x, gp = _local_step(x, c, positions, loss_target, wf)
    gfull, gsmall = _unpad_grads(gp)
    gsh = _reduce_sharded(gfull, chip, core)
    small_shapes = [ws[k].shape for k in SMALL]
    sblob = _pack([gsmall[k] for k in SMALL] + [loss_local.reshape(1)], LANES, 8)
    sred = _all_reduce_small(sblob, "small_all_reduce")
    svals = _unpack(sred, small_shapes + [(1,)])
    loss = svals[-1].reshape(())
    grads = dict(gsh)
    grads.update(dict(zip(SMALL, svals[:-1])))

    delta, new_m, new_v = {}, {}, {}
    for k in SHARDED:
        shp = ws[k].shape
        two = (shp[0] * shp[1], shp[2])
        dlt, nm, nv = _adamw(ws[k].reshape(two), grads[k].reshape(two), ms[k].reshape(two), vs[k].reshape(two), "adamw_" + k)
        delta[k], new_m[k], new_v[k] = dlt.reshape(shp), nm.reshape(shp), nv.reshape(shp)
    pw = _pack([ws[k] for k in SMALL], LANES, 8)
    pg = _pack([grads[k] for k in SMALL], LANES, 8)
    pm = _pack([ms[k] for k in SMALL], LANES, 8)
    pv = _pack([vs[k] for k in SMALL], LANES, 8)
    dlt, nm, nv = _adamw(pw, pg, pm, pv, "adamw_small")
    for k, a, b_, c_ in zip(SMALL, _unpack(dlt, small_shapes), _unpack(nm, small_shapes), _unpack(nv, small_shapes)):
        delta[k], new_m[k], new_v[k] = a, b_, c_

    return (loss, grad_x, *[grads[k] for k in names], *[delta[k] for k in names], *[new_m[k] for k in names],
            *[new_v[k] for k in names])
```

```python
import functools
import math

import numpy as np
import jax
import jax.numpy as jnp
from jax import lax
from jax.experimental import pallas as pl
from jax.experimental.pallas import tpu as pltpu

F32 = jnp.float32
BF16 = jnp.bfloat16

D_MODEL = 1024
DEPTH = 4
HEAD_DIM = 64
ROPE_THETA = 10000.0
NORM_EPS = 1e-6
NEG_INF = -1e30
A_Q_HEADS = 6
A_KV_HEADS = 2
A_WINDOW = 128
B_HEADS = 6
B_Q_RANK = 384
B_KV_RANK = 256
B_NOPE = 64
B_ROPE = 32
B_V = 64
C_GROUPS = 4
C_GROUP_DIM = 64
C_WIDTH = 256
C_CHUNK = 128
IN_COLS = 1824
FFN_HIDDEN = 2816
N_MOD = 6
ADAM_LR = 0.001
ADAM_B1 = 0.9
ADAM_B2 = 0.999
ADAM_EPS = 1e-08
ADAM_WD = 0.01
ADAM_STEP = 10

LANES = 128
VMEM_LIMIT = 56 * 1024 * 1024
N_CHIPS = 4

P_AQ, P_AK, P_AV, P_CQ, P_CKV, P_KR, P_CU, P_CV, P_END = 0, 768, 1024, 1280, 1664, 1920, 2048, 2304, 2560
Y_A, Y_B, Y_C, Y_END = 0, 768, 1536, 1792


def _map_w_in():
    idx = -np.ones(P_END, np.int64)
    half = HEAD_DIM // 2
    for h in range(A_Q_HEADS):
        idx[P_AQ + h * LANES + np.arange(half)] = h * HEAD_DIM + np.arange(half)
        idx[P_AQ + h * LANES + 64 + np.arange(half)] = h * HEAD_DIM + half + np.arange(half)
    for h in range(A_KV_HEADS):
        idx[P_AK + h * LANES + np.arange(half)] = 384 + h * HEAD_DIM + np.arange(half)
        idx[P_AK + h * LANES + 64 + np.arange(half)] = 384 + h * HEAD_DIM + half + np.arange(half)
        idx[P_AV + h * LANES + np.arange(HEAD_DIM)] = 512 + h * HEAD_DIM + np.arange(HEAD_DIM)
    idx[P_CQ:P_CQ + 384] = 640 + np.arange(384)
    idx[P_CKV:P_CKV + 256] = 1024 + np.arange(256)
    idx[P_KR + 48 + np.arange(16)] = 1280 + np.arange(16)
    idx[P_KR + 112 + np.arange(16)] = 1296 + np.arange(16)
    idx[P_CU:P_CU + 256] = 1312 + np.arange(256)
    idx[P_CV:P_CV + 256] = 1568 + np.arange(256)
    return idx


def _map_w_uq():
    idx = -np.ones(B_HEADS * LANES, np.int64)
    for h in range(B_HEADS):
        b = h * (B_NOPE + B_ROPE)
        idx[h * LANES + np.arange(48)] = b + np.arange(48)
        idx[h * LANES + 48 + np.arange(16)] = b + 64 + np.arange(16)
        idx[h * LANES + 64 + np.arange(16)] = b + 48 + np.arange(16)
        idx[h * LANES + 112 + np.arange(16)] = b + 80 + np.arange(16)
    return idx


def _map_w_ukv():
    idx = -np.ones(2 * B_HEADS * LANES, np.int64)
    for h in range(B_HEADS):
        b = h * (B_NOPE + B_V)
        idx[h * LANES + np.arange(48)] = b + np.arange(48)
        idx[h * LANES + 64 + np.arange(16)] = b + 48 + np.arange(16)
        idx[B_HEADS * LANES + h * LANES + np.arange(B_V)] = b + B_NOPE + np.arange(B_V)
    return idx


def _map_w_out():
    idx = -np.ones(Y_END, np.int64)
    for h in range(A_Q_HEADS):
        idx[Y_A + h * LANES + np.arange(64)] = h * 64 + np.arange(64)
    for h in range(B_HEADS):
        idx[Y_B + h * LANES + np.arange(64)] = 384 + h * 64 + np.arange(64)
    idx[Y_C:Y_C + 256] = 768 + np.arange(256)
    return idx


def _inverse(idx, n):
    inv = np.zeros(n, np.int64)
    pos = np.nonzero(idx >= 0)[0]
    inv[idx[pos]] = pos
    return inv


def _runs(idx):
    runs, i, n = [], 0, len(idx)
    while i < n:
        j = i + 1
        while j < n and ((idx[i] < 0 and idx[j] < 0) or (idx[i] >= 0 and idx[j] == idx[i] + (j - i))):
            j += 1
        runs.append((int(idx[i]), j - i))
        i = j
    return runs


def _select_axis(w, idx, axis):
    pieces = []
    for start, length in _runs(idx):
        if start < 0:
            shape = list(w.shape)
            shape[axis] = length
            pieces.append(jnp.zeros(shape, w.dtype))
        else:
            pieces.append(lax.slice_in_dim(w, start, start + length, axis=axis))
    return jnp.concatenate(pieces, axis=axis)


def _pad_axis(w, idx, axis):
    return _select_axis(w, idx, axis)


def _unpad_axis(g, idx, n, axis):
    return _select_axis(g, _inverse(idx, n), axis)


def _params(sem):
    return pltpu.CompilerParams(dimension_semantics=sem, vmem_limit_bytes=VMEM_LIMIT)


def _tile(dim, target):
    if dim <= target:
        return dim
    best = None
    for t in range(LANES, target + 1, LANES):
        if dim % t == 0:
            best = t
    assert best is not None, dim
    return best


def _row_div(rows, target):
    if rows <= target:
        return rows
    best = None
    for t in range(8, target + 1, 8):
        if rows % t == 0:
            best = t
    assert best is not None, rows
    return best


_ANY = pl.BlockSpec(memory_space=pl.ANY)
_MESH = pl.DeviceIdType.MESH


def _xchg_out_shapes(kind, srcs):
    if kind == "gather":
        return [jax.ShapeDtypeStruct((N_CHIPS,) + s.shape, s.dtype) for s in srcs]
    return [jax.ShapeDtypeStruct(s.shape, s.dtype) for s in srcs]


def _xchg_scratch(kind, n):
    per = 1 if kind == "swap" else N_CHIPS - 1
    return [pltpu.SemaphoreType.DMA((per * n,)), pltpu.SemaphoreType.DMA((per * n,)), pltpu.SemaphoreType.DMA((n,))]


def _xchg_copies(kind, srcs, dsts, send_sems, recv_sems, local_sems, arrivals):
    x, y, c = lax.axis_index("x"), lax.axis_index("y"), lax.axis_index("c")
    me = 2 * x + y
    peers = [(1 - x, y), (x, 1 - y), (1 - x, 1 - y)]
    local, out, back = [], [], []
    for i, (s, d) in enumerate(zip(srcs, dsts)):
        if kind == "swap":
            cp = pltpu.make_async_remote_copy(src_ref=s, dst_ref=d, send_sem=send_sems.at[i], recv_sem=recv_sems.at[i],
                                              device_id=(x, y, 1 - c), device_id_type=_MESH)
            out.append(cp)
            back.append(cp)
            continue
        local.append(pltpu.make_async_copy(s if kind == "gather" else s.at[me], d.at[me], local_sems.at[i]))
        for kk, (px, py) in enumerate(peers):
            j = (N_CHIPS - 1) * i + kk
            theirs = 2 * px + py
            out.append(pltpu.make_async_remote_copy(
                src_ref=s if kind == "gather" else s.at[theirs], dst_ref=d.at[me], send_sem=send_sems.at[j],
                recv_sem=recv_sems.at[j], device_id=(px, py, c), device_id_type=_MESH))
            if arrivals:
                back.append(pltpu.make_async_remote_copy(
                    src_ref=s if kind == "gather" else s.at[me], dst_ref=d.at[theirs], send_sem=send_sems.at[j],
                    recv_sem=recv_sems.at[j], device_id=(px, py, c), device_id_type=_MESH))
    return local, out, back


def _xchg_start(kind, srcs, dsts, sems):
    local, out, _ = _xchg_copies(kind, srcs, dsts, *sems, arrivals=False)
    for cp in local + out:
        cp.start()


def _xchg_wait(kind, srcs, dsts, sems):
    local, out, back = _xchg_copies(kind, srcs, dsts, *sems, arrivals=True)
    for cp in back:
        cp.wait_recv()
    for cp in out:
        cp.wait_send()
    for cp in local:
        cp.wait()


def _xchg_call(kind, srcs, name):
    n = len(srcs)

    def kern(*refs):
        s, d, sems = refs[:n], refs[n:2 * n], refs[2 * n:]
        _xchg_start(kind, s, d, sems)
        _xchg_wait(kind, s, d, sems)

    return pl.pallas_call(
        kern, name=name, in_specs=[_ANY] * n, out_specs=[_ANY] * n, out_shape=_xchg_out_shapes(kind, srcs),
        scratch_shapes=_xchg_scratch(kind, n),
    )(*srcs)


_DIMS = {"nn": (((1,), (0,)), ((), ())), "nt": (((1,), (1,)), ((), ())), "tn": (((0,), (0,)), ((), ()))}


def _matmul(a, b, mode, out_dtype, name, *, b_chunks=False, out_chunks=False, xchg=None):
    if b_chunks:
        nchunk, brows, bcols = b.shape
        bshape = (brows, nchunk * bcols)
    else:
        bshape = b.shape
    if mode == "nn":
        (m, k), (_, n) = a.shape, bshape
    elif mode == "nt":
        (m, k), (n, _) = a.shape, bshape
    else:
        (k, m), (_, n) = a.shape, bshape
    tm, tk = (1408, 1024) if mode == "tn" else (1024, 1408)
    tm, tn, tk = _tile(m, tm), _tile(n, 1408), _tile(k, tk)
    if b_chunks:
        if mode == "nn":
            tn = bcols
        else:
            assert mode == "nt"
            tk = bcols
    if out_chunks:
        assert n % N_CHIPS == 0
        tn = n // N_CHIPS
    ni, nj, nk = m // tm, n // tn, k // tk
    dims = _DIMS[mode]
    n_x = 0 if xchg is None else len(xchg[1])

    def kern(*refs):
        a_ref, b_ref = refs[0], refs[1]
        xs = refs[2:2 + n_x]
        o_ref = refs[2 + n_x]
        xd = refs[3 + n_x:3 + 2 * n_x]
        acc_ref = refs[3 + 2 * n_x]
        sems = refs[4 + 2 * n_x:]
        i, j, kk = pl.program_id(0), pl.program_id(1), pl.program_id(2)
        if n_x:
            @pl.when(jnp.logical_and(jnp.logical_and(i == 0, j == 0), kk == 0))
            def _():
                _xchg_start(xchg[0], xs, xd, sems)

        @pl.when(kk == 0)
        def _():
            acc_ref[...] = jnp.zeros_like(acc_ref)

        acc_ref[...] += lax.dot_general(a_ref[...], b_ref[...], dims, preferred_element_type=F32)

        @pl.when(kk == nk - 1)
        def _():
            o_ref[...] = acc_ref[...].astype(o_ref.dtype)

        if n_x:
            @pl.when(jnp.logical_and(jnp.logical_and(i == ni - 1, j == nj - 1), kk == nk - 1))
            def _():
                _xchg_wait(xchg[0], xs, xd, sems)

    if mode == "tn":
        a_spec = pl.BlockSpec((tk, tm), lambda i, j, kk: (kk, i))
    else:
        a_spec = pl.BlockSpec((tm, tk), lambda i, j, kk: (i, kk))
    if b_chunks and mode == "nn":
        b_spec = pl.BlockSpec((None, tk, tn), lambda i, j, kk: (j, kk, 0))
    elif b_chunks:
        b_spec = pl.BlockSpec((None, tn, tk), lambda i, j, kk: (kk, j, 0))
    elif mode == "nt":
        b_spec = pl.BlockSpec((tn, tk), lambda i, j, kk: (j, kk))
    else:
        b_spec = pl.BlockSpec((tk, tn), lambda i, j, kk: (kk, j))
    if out_chunks:
        o_spec = pl.BlockSpec((None, tm, tn), lambda i, j, kk: (j, i, 0))
        o_shape = jax.ShapeDtypeStruct((N_CHIPS, m, tn), out_dtype)
    else:
        o_spec = pl.BlockSpec((tm, tn), lambda i, j, kk: (i, j))
        o_shape = jax.ShapeDtypeStruct((m, n), out_dtype)
    xs = [] if xchg is None else list(xchg[1])
    res = pl.pallas_call(
        kern, name=name, grid=(ni, nj, nk),
        in_specs=[a_spec, b_spec] + [_ANY] * n_x, out_specs=[o_spec] + [_ANY] * n_x,
        out_shape=[o_shape] + (_xchg_out_shapes(xchg[0], xs) if n_x else []),
        scratch_shapes=[pltpu.VMEM((tm, tn), F32)] + (_xchg_scratch(xchg[0], n_x) if n_x else []),
        compiler_params=_params(("arbitrary", "arbitrary", "arbitrary") if n_x else ("parallel", "parallel", "arbitrary")),
    )(a, b, *xs)
    return (res[0], list(res[1:])) if n_x else res[0]


def _rowcall(body, *, name, T, S, tr, rows, exs=(), pars=(), row_outs=(), ex_outs=(), par_outs=(), aliases=None):
    assert S % tr == 0 and T % S == 0
    per_ex = S // tr
    nb = T // S
    n_rows, n_exs, n_pars = len(rows), len(exs), len(pars)
    n_ro, n_eo, n_po = len(row_outs), len(ex_outs), len(par_outs)

    def kern(*refs):
        ins = refs[:n_rows + n_exs + n_pars]
        outs = refs[n_rows + n_exs + n_pars:]
        rv = [r[...] for r in ins[:n_rows]]
        ev = [r[0] for r in ins[n_rows:n_rows + n_exs]]
        pv = [r[...] for r in ins[n_rows + n_exs:]]
        ro, eo, po = body(rv, ev, pv)
        i = pl.program_id(0)
        for ref, val in zip(outs[:n_ro], ro):
            if isinstance(val, (list, tuple)):
                off = 0
                for piece in val:
                    w = piece.shape[-1]
                    ref[:, off:off + w] = piece.astype(ref.dtype)
                    off += w
            else:
                ref[...] = val.astype(ref.dtype)
        first_of_ex = (i % per_ex) == 0
        for ref, val in zip(outs[n_ro:n_ro + n_eo], eo):
            @pl.when(first_of_ex)
            def _(ref=ref, val=val):
                ref[0] = val

            @pl.when(jnp.logical_not(first_of_ex))
            def _(ref=ref, val=val):
                ref[0] += val
        for ref, val in zip(outs[n_ro + n_eo:], po):
            @pl.when(i == 0)
            def _(ref=ref, val=val):
                ref[...] = val

            @pl.when(i != 0)
            def _(ref=ref, val=val):
                ref[...] += val

    in_specs = [pl.BlockSpec((tr, w), functools.partial(lambda i, cb: (i, cb), cb=cb)) for (_, w, cb) in rows]
    in_specs += [pl.BlockSpec((1, 1, e.shape[-1]), lambda i: (i // per_ex, 0, 0)) for e in exs]
    in_specs += [pl.BlockSpec(p.shape, functools.partial(lambda i, nd: (0,) * nd, nd=p.ndim)) for p in pars]
    out_specs = [pl.BlockSpec((tr, w), functools.partial(lambda i, cb: (i, cb), cb=cb)) for (_, _, w, cb) in row_outs]
    out_specs += [pl.BlockSpec((1, 1, f), lambda i: (i // per_ex, 0, 0)) for f in ex_outs]
    out_specs += [pl.BlockSpec(tuple(s), functools.partial(lambda i, nd: (0,) * nd, nd=len(s))) for s in par_outs]
    out_shape = [jax.ShapeDtypeStruct((T, tw), dt) for (tw, dt, _, _) in row_outs]
    out_shape += [jax.ShapeDtypeStruct((nb, 1, f), F32) for f in ex_outs]
    out_shape += [jax.ShapeDtypeStruct(tuple(s), F32) for s in par_outs]
    res = pl.pallas_call(
        kern, name=name, grid=(T // tr,), in_specs=in_specs, out_specs=out_specs, out_shape=out_shape,
        input_output_aliases=aliases or {}, compiler_params=_params(("arbitrary",)),
    )(*[r[0] for r in rows], *exs, *pars)
    return res[:n_ro], res[n_ro:n_ro + n_eo], res[n_ro + n_eo:]


def _rms(x, w, n=None):
    n = x.shape[-1] if n is None else n
    ms = jnp.sum(x * x, axis=-1, keepdims=True) * (1.0 / n)
    return x * lax.rsqrt(ms + NORM_EPS) * w


def _gelu(x):
    return 0.5 * x * (1.0 + lax.erf(x * np.float32(1.0 / math.sqrt(2.0))))


def _silu(x):
    return x * jax.nn.sigmoid(x)


@jax.custom_vjp
def _rope(x, cos, sin):
    return x * cos + pltpu.roll(x, 64, 1) * sin


def _rope_fwd(x, cos, sin):
    return _rope(x, cos, sin), (cos, sin)


def _rope_bwd(res, dy):
    cos, sin = res
    return dy * cos + pltpu.roll(dy * sin, 64, 1), None, None


_rope.defvjp(_rope_fwd, _rope_bwd)


def _heads(x, n):
    return [x[:, h * LANES:(h + 1) * LANES] for h in range(n)]


def _f_norm_mod(x, w, sc, sh):
    return _rms(x, w) * (1.0 + sc) + sh


def _f_resid_norm_mod(xa, delta, g, w, sc, sh):
    xn = xa + g * delta
    return xn, _f_norm_mod(xn, w, sc, sh)


def _f_mixprep(proj, ca, sa, cb, sb, qnw, kvnw):
    qa = [_rope(p, ca, sa) for p in _heads(proj[:, P_AQ:P_AK], A_Q_HEADS)]
    ka = [_rope(p, ca, sa) for p in _heads(proj[:, P_AK:P_AV], A_KV_HEADS)]
    va = proj[:, P_AV:P_CQ]
    cqn = _rms(proj[:, P_CQ:P_CKV], qnw)
    ckvn = _rms(proj[:, P_CKV:P_KR], kvnw)
    kr = _rope(proj[:, P_KR:P_CU], cb, sb)
    return jnp.concatenate(qa, -1), jnp.concatenate(ka, -1), va, cqn, ckvn, kr


def _f_mlaprep(q, kv, kr, cb, sb):
    qs = [_rope(p, cb, sb) for p in _heads(q, B_HEADS)]
    ks = [p + kr for p in _heads(kv[:, :B_HEADS * LANES], B_HEADS)]
    return jnp.concatenate(qs, -1), jnp.concatenate(ks, -1), kv[:, B_HEADS * LANES:]


def _f_sgu(cu, cv, ln_w, ln_b, w_s, b_col):
    u = _gelu(cu)
    v = _gelu(cv)
    mu = jnp.mean(v, axis=-1, keepdims=True)
    var = jnp.mean(jnp.square(v - mu), axis=-1, keepdims=True)
    vn = (v - mu) * lax.rsqrt(var + NORM_EPS) * ln_w + ln_b
    r = lax.broadcasted_iota(jnp.int32, (C_CHUNK, C_CHUNK), 0)
    c = lax.broadcasted_iota(jnp.int32, (C_CHUNK, C_CHUNK), 1)
    lane = lax.broadcasted_iota(jnp.int32, (1, C_WIDTH), 1)
    mixed = jnp.zeros(cu.shape, F32)
    for g in range(C_GROUPS):
        gm = (lane // C_GROUP_DIM == g).astype(F32)
        wg = jnp.where(r >= c, w_s[g], 0.0).astype(BF16)
        mixed = mixed + jnp.dot(wg, (vn * gm).astype(BF16), preferred_element_type=F32) + b_col[g] * gm
    return u * mixed


def _f_outnorm(oa, ob, yc, gw):
    ya = _rms(oa, gw[:, Y_A:Y_B], A_Q_HEADS * HEAD_DIM)
    yb = _rms(ob, gw[:, Y_B:Y_C], B_HEADS * B_V)
    ycn = _rms(yc, gw[:, Y_C:Y_END])
    return jnp.concatenate([ya, yb, ycn], -1)


def _f_swiglu(gate, up):
    return _silu(gate) * up


def _mask(q_start, k_start, tq, tk, window):
    qpos = q_start + lax.broadcasted_iota(jnp.int32, (tq, tk), 0)
    kpos = k_start + lax.broadcasted_iota(jnp.int32, (tq, tk), 1)
    m = kpos <= qpos
    if window is not None:
        m = jnp.logical_and(m, qpos - kpos < window)
    return m


def _tile_fwd(qv, kk, vv, q_start, k_start, n_free, scale, window, m0, l0):
    tq = qv.shape[0]
    W = kk.shape[0]
    parts = []
    if n_free > 0:
        parts.append((lax.dot_general(qv, kk[:n_free], _DIMS["nt"], preferred_element_type=F32) * scale, vv[:n_free]))
    if W > n_free:
        s = lax.dot_general(qv, kk[n_free:], _DIMS["nt"], preferred_element_type=F32) * scale
        s = jnp.where(_mask(q_start, k_start + n_free, tq, W - n_free, window), s, NEG_INF)
        parts.append((s, vv[n_free:]))
    m = m0
    for s, _ in parts:
        mx = jnp.max(s, axis=-1, keepdims=True)
        m = mx if m is None else jnp.maximum(m, mx)
    l = None if l0 is None else l0 * jnp.exp(m0 - m)
    o = None
    for s, vpart in parts:
        p = jnp.exp(s - m)
        ps = jnp.sum(p, axis=-1, keepdims=True)
        l = ps if l is None else l + ps
        po = jnp.dot(p.astype(BF16), vpart, preferred_element_type=F32)
        o = po if o is None else o + po
    return o / l, m + jnp.log(l)


def _tile_bwd(qv, kk, vv, dof, ov, lse, q_start, k_start, n_free, scale, window):
    tq = qv.shape[0]
    W = kk.shape[0]
    dob = dof.astype(BF16)
    delta = jnp.sum(dof * ov, axis=-1, keepdims=True)
    dq = None
    outs = []
    for (a, b, masked) in ((0, n_free, False), (n_free, W, True)):
        if b <= a:
            continue
        kp, vp = kk[a:b], vv[a:b]
        s = lax.dot_general(qv, kp, _DIMS["nt"], preferred_element_type=F32) * scale
        if masked:
            s = jnp.where(_mask(q_start, k_start + a, tq, b - a, window), s, NEG_INF)
        p = jnp.exp(s - lse)
        dp = lax.dot_general(dob, vp, _DIMS["nt"], preferred_element_type=F32)
        ds = (p * (dp - delta) * scale).astype(BF16)
        d = jnp.dot(ds, kp, preferred_element_type=F32)
        dq = d if dq is None else dq + d
        dkp = lax.dot_general(ds, qv, _DIMS["tn"], preferred_element_type=F32)
        dvp = lax.dot_general(p.astype(BF16), dob, _DIMS["tn"], preferred_element_type=F32)
        outs.append((a, dkp, dvp))
    return dq, outs


def _attn_fwd(q, k, v, sinks, *, B, S, HQ, HK, window, scale, tq, band, name):
    G = HQ // HK
    nq = S // tq
    T = B * S
    has_sink = sinks is not None

    def kern(*refs):
        if has_sink:
            q_ref, k_ref, v_ref, s_ref, o_ref, lse_ref = refs
        else:
            q_ref, k_ref, v_ref, o_ref, lse_ref = refs
        q_start = pl.program_id(2) * tq
        qv = q_ref[...]
        if has_sink:
            m0 = jnp.broadcast_to(s_ref[0][:, :1], (tq, 1))
            l0 = jnp.ones((tq, 1), F32)
        else:
            m0 = l0 = None

        def finish(o, lse):
            o_ref[...] = o
            lse_ref[...] = jnp.broadcast_to(lse, (tq, LANES))

        if window is None:
            bidx = q_start // band
            for bb in range(S // band):
                @pl.when(bidx == bb)
                def _(bb=bb):
                    W = (bb + 1) * band
                    finish(*_tile_fwd(qv, k_ref[0:W, :], v_ref[0:W, :], q_start, 0, bb * band, scale, None, m0, l0))
        else:
            W = min(S, tq + window)
            k_start = pl.multiple_of(jnp.maximum(q_start - window, 0), window)
            finish(*_tile_fwd(qv, k_ref[pl.ds(k_start, W), :], v_ref[pl.ds(k_start, W), :], q_start, k_start, 0, scale,
                              window, m0, l0))

    q_spec = pl.BlockSpec((tq, LANES), lambda b, h, i: (b * nq + i, h))
    kv_spec = pl.BlockSpec((S, LANES), lambda b, h, i: (b, h // G))
    in_specs = [q_spec, kv_spec, kv_spec]
    args = [q, k, v]
    if has_sink:
        in_specs.append(pl.BlockSpec((1, 1, LANES), lambda b, h, i: (h, 0, 0)))
        args.append(sinks)
    return pl.pallas_call(
        kern, name=name, grid=(B, HQ, nq), in_specs=in_specs, out_specs=[q_spec, q_spec],
        out_shape=[jax.ShapeDtypeStruct((T, HQ * LANES), F32), jax.ShapeDtypeStruct((T, HQ * LANES), F32)],
        compiler_params=_params(("parallel", "parallel", "arbitrary")),
    )(*args)


def _attn_bwd(q, k, v, o, lse, do, sinks, *, B, S, HQ, HK, window, scale, tq, band, name):
    G = HQ // HK
    nq = S // tq
    T = B * S
    has_sink = sinks is not None

    def kern(*refs):
        if has_sink:
            q_ref, k_ref, v_ref, o_ref, lse_ref, do_ref, s_ref, dq_ref, dk_ref, dv_ref, ds_ref = refs
        else:
            q_ref, k_ref, v_ref, o_ref, lse_ref, do_ref, dq_ref, dk_ref, dv_ref = refs
        gi = pl.program_id(2)
        qi = pl.program_id(3)
        q_start = qi * tq

        @pl.when(jnp.logical_and(gi == 0, qi == 0))
        def _():
            dk_ref[...] = jnp.zeros_like(dk_ref)
            dv_ref[...] = jnp.zeros_like(dv_ref)

        qv = q_ref[...]
        dof = do_ref[...]
        ov = o_ref[...]
        lse_v = lse_ref[...][:, :1]
        if window is None:
            bidx = q_start // band
            for bb in range(S // band):
                @pl.when(bidx == bb)
                def _(bb=bb):
                    W = (bb + 1) * band
                    dq, outs = _tile_bwd(qv, k_ref[0:W, :], v_ref[0:W, :], dof, ov, lse_v, q_start, 0, bb * band, scale, None)
                    dq_ref[...] = dq
                    for a, dkp, dvp in outs:
                        dk_ref[a:a + dkp.shape[0], :] += dkp
                        dv_ref[a:a + dvp.shape[0], :] += dvp
        else:
            W = min(S, tq + window)
            k_start = pl.multiple_of(jnp.maximum(q_start - window, 0), window)
            dq, outs = _tile_bwd(qv, k_ref[pl.ds(k_start, W), :], v_ref[pl.ds(k_start, W), :], dof, ov, lse_v, q_start,
                                 k_start, 0, scale, window)
            dq_ref[...] = dq
            (_, dkp, dvp), = outs
            dk_ref[pl.ds(k_start, W), :] += dkp
            dv_ref[pl.ds(k_start, W), :] += dvp
        if has_sink:
            delta = jnp.sum(dof * ov, axis=-1, keepdims=True)
            sink = s_ref[0][:, :1]
            part = -jnp.sum(jnp.exp(sink - lse_v) * delta, axis=0, keepdims=True)
            part = jnp.broadcast_to(part, (1, LANES))

            @pl.when(qi == 0)
            def _():
                ds_ref[0] = part

            @pl.when(qi != 0)
            def _():
                ds_ref[0] += part

    q_spec = pl.BlockSpec((tq, LANES), lambda b, hk, g, i: (b * nq + i, hk * G + g))
    kv_spec = pl.BlockSpec((S, LANES), lambda b, hk, g, i: (b, hk))
    in_specs = [q_spec, kv_spec, kv_spec, q_spec, q_spec, q_spec]
    args = [q, k, v, o, lse, do]
    out_specs = [q_spec, kv_spec, kv_spec]
    out_shape = [jax.ShapeDtypeStruct((T, HQ * LANES), F32), jax.ShapeDtypeStruct((T, HK * LANES), F32),
                 jax.ShapeDtypeStruct((T, HK * LANES), F32)]
    if has_sink:
        in_specs.append(pl.BlockSpec((1, 1, LANES), lambda b, hk, g, i: (hk * G + g, 0, 0)))
        args.append(sinks)
        out_specs.append(pl.BlockSpec((1, 1, LANES), lambda b, hk, g, i: (b * HQ + hk * G + g, 0, 0)))
        out_shape.append(jax.ShapeDtypeStruct((B * HQ, 1, LANES), F32))
    res = pl.pallas_call(
        kern, name=name, grid=(B, HK, G, nq), in_specs=in_specs, out_specs=out_specs, out_shape=out_shape,
        compiler_params=_params(("parallel", "parallel", "arbitrary", "arbitrary")),
    )(*args)
    return res if has_sink else (*res, None)


def _rope_tables(pos_col, freq, sign, name):
    T = pos_col.shape[0]
    tr = _tile(T, 1024)

    def kern(p_ref, f_ref, s_ref, c_out, s_out):
        ang = p_ref[...] * f_ref[...]
        c_out[...] = jnp.cos(ang)
        s_out[...] = jnp.sin(ang) * s_ref[...]

    spec = pl.BlockSpec((tr, LANES), lambda i: (i, 0))
    par = pl.BlockSpec((1, LANES), lambda i: (0, 0))
    return pl.pallas_call(
        kern, name=name, grid=(T // tr,), in_specs=[pl.BlockSpec((tr, 1), lambda i: (i, 0)), par, par],
        out_specs=[spec, spec], out_shape=[jax.ShapeDtypeStruct((T, LANES), F32)] * 2,
        compiler_params=_params(("parallel",)),
    )(pos_col, freq, sign)


def _ada_fwd(c8, ada_w, ada_b):
    nch, D, tn = ada_w.shape

    def kern(c_ref, w_ref, b_ref, o_ref):
        act = _silu(c_ref[...]).astype(BF16)
        o_ref[...] = jnp.dot(act, w_ref[...], preferred_element_type=F32) + b_ref[...]

    return pl.pallas_call(
        kern, name="ada_fwd", grid=(nch,),
        in_specs=[pl.BlockSpec((8, D), lambda j: (0, 0)), pl.BlockSpec((None, D, tn), lambda j: (j, 0, 0)),
                  pl.BlockSpec((1, tn), lambda j: (0, j))],
        out_specs=pl.BlockSpec((8, tn), lambda j: (0, j)),
        out_shape=jax.ShapeDtypeStruct((8, nch * tn), F32), compiler_params=_params(("parallel",)),
    )(c8, ada_w, ada_b)


def _ada_bwd(c8, dmod):
    _, N = dmod.shape
    D = c8.shape[1]
    tn = N // N_CHIPS

    def kern(c_ref, d_ref, gw_ref, gb_ref):
        act = _silu(c_ref[...]).astype(BF16)
        d = d_ref[...]
        gw_ref[...] = lax.dot_general(act, d.astype(BF16), _DIMS["tn"], preferred_element_type=F32).astype(BF16)
        gb_ref[...] = jnp.sum(d, axis=0, keepdims=True)

    return pl.pallas_call(
        kern, name="ada_bwd", grid=(N_CHIPS,),
        in_specs=[pl.BlockSpec((8, D), lambda j: (0, 0)), pl.BlockSpec((8, tn), lambda j: (0, j))],
        out_specs=[pl.BlockSpec((None, D, tn), lambda j: (j, 0, 0)), pl.BlockSpec((1, tn), lambda j: (0, j))],
        out_shape=[jax.ShapeDtypeStruct((N_CHIPS, D, tn), BF16), jax.ShapeDtypeStruct((1, N), F32)],
        compiler_params=_params(("parallel",)),
    )(c8, dmod)


def _adamw(w, ga, gb, m, v, name):
    rows, cols = w.shape
    tr = _row_div(rows, 256)
    two = gb is not None

    def kern(*refs):
        if two:
            w_ref, ga_ref, gb_ref, m_ref, v_ref, g_out, d_out, m_out, v_out = refs
            gv = ga_ref[...] + gb_ref[...]
        else:
            w_ref, ga_ref, m_ref, v_ref, g_out, d_out, m_out, v_out = refs
            gv = ga_ref[...]
        mn = ADAM_B1 * m_ref[...] + (1.0 - ADAM_B1) * gv
        vn = ADAM_B2 * v_ref[...] + (1.0 - ADAM_B2) * jnp.square(gv)
        m_hat = mn / (1.0 - ADAM_B1 ** ADAM_STEP)
        v_hat = vn / (1.0 - ADAM_B2 ** ADAM_STEP)
        g_out[...] = gv
        d_out[...] = -ADAM_LR * (m_hat / (jnp.sqrt(v_hat) + ADAM_EPS) + ADAM_WD * w_ref[...])
        m_out[...] = mn
        v_out[...] = vn

    spec = pl.BlockSpec((tr, cols), lambda i: (i, 0))
    args = [w, ga, gb, m, v] if two else [w, ga, m, v]
    return pl.pallas_call(
        kern, name=name, grid=(rows // tr,), in_specs=[spec] * len(args), out_specs=[spec] * 4,
        out_shape=[jax.ShapeDtypeStruct((rows, cols), F32)] * 4, compiler_params=_params(("parallel",)),
    )(*args)


def _sum_slots(x, name):
    n, rows, cols = x.shape
    tr = _row_div(rows, 256)

    def kern(x_ref, o_ref):
        acc = x_ref[0].astype(F32)
        for j in range(1, n):
            acc = acc + x_ref[j].astype(F32)
        o_ref[...] = acc

    return pl.pallas_call(
        kern, name=name, grid=(rows // tr,), in_specs=[pl.BlockSpec((n, tr, cols), lambda i: (0, i, 0))],
        out_specs=pl.BlockSpec((tr, cols), lambda i: (i, 0)), out_shape=jax.ShapeDtypeStruct((rows, cols), F32),
        compiler_params=_params(("parallel",)),
    )(x)


def _all_reduce_small(blob, name):
    R, C = blob.shape

    def kern(src, out, pair, chips, send_sems, recv_sems):
        x, y, c = lax.axis_index("x"), lax.axis_index("y"), lax.axis_index("c")
        me = 2 * x + y
        to_sibling = pltpu.make_async_remote_copy(
            src_ref=src, dst_ref=pair, send_sem=send_sems.at[0], recv_sem=recv_sems.at[0],
            device_id=(x, y, 1 - c), device_id_type=_MESH)
        to_sibling.start()
        to_sibling.wait()
        chips[me] = src[...] + pair[...]
        peers = [(1 - x, y), (x, 1 - y), (1 - x, 1 - y)]
        copies = [pltpu.make_async_remote_copy(
            src_ref=chips.at[me], dst_ref=chips.at[me], send_sem=send_sems.at[1 + kk], recv_sem=recv_sems.at[1 + kk],
            device_id=(px, py, c), device_id_type=_MESH) for kk, (px, py) in enumerate(peers)]
        for cp in copies:
            cp.start()
        for kk, (px, py) in enumerate(peers):
            pltpu.make_async_remote_copy(
                src_ref=chips.at[me], dst_ref=chips.at[2 * px + py], send_sem=send_sems.at[1 + kk],
                recv_sem=recv_sems.at[1 + kk], device_id=(px, py, c), device_id_type=_MESH).wait_recv()
        for cp in copies:
            cp.wait_send()
        acc = chips[0]
        for j in range(1, N_CHIPS):
            acc = acc + chips[j]
        out[...] = acc

    vm = pl.BlockSpec(memory_space=pltpu.VMEM)
    return pl.pallas_call(
        kern, name=name, in_specs=[vm], out_specs=vm, out_shape=jax.ShapeDtypeStruct((R, C), F32),
        scratch_shapes=[pltpu.VMEM((R, C), F32), pltpu.VMEM((N_CHIPS, R, C), F32), pltpu.SemaphoreType.DMA((4,)),
                        pltpu.SemaphoreType.DMA((4,))],
        compiler_params=pltpu.CompilerParams(vmem_limit_bytes=VMEM_LIMIT),
    )(blob)


def _row_tile(S):
    return 256 if S % 256 == 0 else 128


def _attn_tiles(S):
    return min(S, 512), min(S, 256), min(S, 1024)


def _mm(host, got, kind, a, b, mode, out_dtype, name, **kw):
    if host is not None and name in host:
        res, xs = _matmul(a, b, mode, out_dtype, name, xchg=(kind, host[name]), **kw)
        got[name] = xs
        return res
    return _matmul(a, b, mode, out_dtype, name, **kw)


def _layer_fwd(xin, prev, mods, w, tabs, B, S, host=None):
    T = B * S
    tr = _row_tile(S)
    sh1, sc1, g1, sh2, sc2, g2 = mods
    ca, sa, cb, sb = tabs
    sv = {}
    got = {}
    if prev is None:
        def body(rv, ev, pv):
            return [_f_norm_mod(rv[0], pv[0], ev[0], ev[1])], [], []
        (h,), _, _ = _rowcall(body, name="f_norm_mod", T=T, S=S, tr=tr, rows=[(xin, D_MODEL, 0)], exs=[sc1, sh1],
                              pars=[w["norm1_w"]], row_outs=[(D_MODEL, BF16, D_MODEL, 0)])
        x = xin
    else:
        x1p, dp, g2p = prev

        def body(rv, ev, pv):
            xn, hh = _f_resid_norm_mod(rv[0], rv[1], ev[0], pv[0], ev[1], ev[2])
            return [xn, hh], [], []
        (x, h), _, _ = _rowcall(body, name="f_resid_norm_mod1", T=T, S=S, tr=tr,
                                rows=[(x1p, D_MODEL, 0), (dp, D_MODEL, 0)], exs=[g2p, sc1, sh1], pars=[w["norm1_w"]],
                                row_outs=[(D_MODEL, F32, D_MODEL, 0), (D_MODEL, BF16, D_MODEL, 0)])
    sv["x"], sv["h"] = x, h
    proj = _mm(host, got, "gather", h, w["w_in"], "nn", F32, "mm_in")
    sv["proj"] = proj

    def body(rv, ev, pv):
        outs = _f_mixprep(rv[0], rv[1], rv[2], rv[3], rv[4], pv[0], pv[1])
        return list(outs), [], []
    (qa, ka, va, cqn, ckvn, kr), _, _ = _rowcall(
        body, name="f_mixprep", T=T, S=S, tr=tr,
        rows=[(proj, P_CU, 0), (ca, LANES, 0), (sa, LANES, 0), (cb, LANES, 0), (sb, LANES, 0)],
        pars=[w["b_q_norm_w"], w["b_kv_norm_w"]],
        row_outs=[(768, BF16, 768, 0), (256, BF16, 256, 0), (256, BF16, 256, 0), (384, BF16, 384, 0),
                  (256, BF16, 256, 0), (LANES, F32, LANES, 0)])
    sv.update(qa=qa, ka=ka, va=va, cqn=cqn, ckvn=ckvn)
    q = _matmul(cqn, w["b_w_uq"], "nn", F32, "mm_uq")
    kv = _matmul(ckvn, w["b_w_ukv"], "nn", F32, "mm_ukv")

    def body(rv, ev, pv):
        Q, K, V = _f_mlaprep(rv[0], rv[1], rv[2], rv[3], rv[4])
        return [Q, K, V], [], []
    (Q, K, V), _, _ = _rowcall(
        body, name="f_mlaprep", T=T, S=S, tr=tr,
        rows=[(q, 768, 0), (kv, 1536, 0), (kr, LANES, 0), (cb, LANES, 0), (sb, LANES, 0)],
        row_outs=[(768, BF16, 768, 0)] * 3)
    sv.update(Q=Q, K=K, V=V)
    ta, tb, band = _attn_tiles(S)
    oa, lse_a = _attn_fwd(qa, ka, va, w["sinks"], B=B, S=S, HQ=A_Q_HEADS, HK=A_KV_HEADS, window=A_WINDOW,
                          scale=HEAD_DIM ** -0.5, tq=ta, band=None, name="attn_a_fwd")
    ob, lse_b = _attn_fwd(Q, K, V, None, B=B, S=S, HQ=B_HEADS, HK=B_HEADS, window=None,
                          scale=(B_NOPE + B_ROPE) ** -0.5, tq=tb, band=band, name="attn_b_fwd")
    sv.update(oa=oa, lse_a=lse_a, ob=ob, lse_b=lse_b)

    def body(rv, ev, pv):
        return [_f_sgu(rv[0], rv[1], pv[0], pv[1], pv[2], pv[3])], [], []
    (yc,), _, _ = _rowcall(body, name="f_sgu", T=T, S=S, tr=C_CHUNK,
                           rows=[(proj, C_WIDTH, P_CU // C_WIDTH), (proj, C_WIDTH, P_CV // C_WIDTH)],
                           pars=[w["c_ln_w"], w["c_ln_b"], w["c_w_s"], w["c_b_col"]],
                           row_outs=[(C_WIDTH, F32, C_WIDTH, 0)])
    sv["yc"] = yc

    def body(rv, ev, pv):
        return [_f_outnorm(rv[0], rv[1], rv[2], pv[0])], [], []
    (y,), _, _ = _rowcall(body, name="f_outnorm", T=T, S=S, tr=tr,
                          rows=[(oa, 768, 0), (ob, 768, 0), (yc, C_WIDTH, 0)], pars=[w["out_norm_w"]],
                          row_outs=[(Y_END, BF16, Y_END, 0)])
    sv["y"] = y
    o = _mm(host, got, "gather", y, w["w_out"], "nn", F32, "mm_out")
    sv["o"] = o

    def body(rv, ev, pv):
        xn, hh = _f_resid_norm_mod(rv[0], rv[1], ev[0], pv[0], ev[1], ev[2])
        return [xn, hh], [], []
    (x1, h2), _, _ = _rowcall(body, name="f_resid_norm_mod2", T=T, S=S, tr=tr,
                              rows=[(x, D_MODEL, 0), (o, D_MODEL, 0)], exs=[g1, sc2, sh2], pars=[w["norm2_w"]],
                              row_outs=[(D_MODEL, F32, D_MODEL, 0), (D_MODEL, BF16, D_MODEL, 0)])
    sv["h2"] = h2
    gu = _mm(host, got, "gather", h2, w["w_gate_up"], "nn", BF16, "mm_gate_up", b_chunks=True)
    sv["gu"] = gu

    def body(rv, ev, pv):
        return [_f_swiglu(rv[0].astype(F32), rv[1].astype(F32))], [], []
    (act,), _, _ = _rowcall(body, name="f_swiglu", T=T, S=S, tr=tr,
                            rows=[(gu, FFN_HIDDEN, 0), (gu, FFN_HIDDEN, 1)], row_outs=[(FFN_HIDDEN, BF16, FFN_HIDDEN, 0)])
    sv["act"] = act
    d = _mm(host, got, "gather", act, w["w_down"], "nn", F32, "mm_down")
    sv["x1"], sv["d"] = x1, d
    return (x1, d, g2), sv, got


def _final(x1, d, g2, fw, target, B, S):
    T = B * S
    tr = _row_tile(S)

    def loss_fn(x1v, dv, gv, wv, tv):
        yf = _rms(x1v + gv * dv, wv)
        return 0.5 * jnp.sum(jnp.mean(jnp.square(yf - tv), axis=-1))

    def body(rv, ev, pv):
        x1v, dv, tv = rv
        val, vjp = jax.vjp(lambda a, b_, g, ww: loss_fn(a, b_, g, ww, tv), x1v, dv, ev[0], pv[0])
        dx1, dd, dg, dw = vjp(jnp.ones((), F32))
        return [dx1, dd], [dg], [dw, jnp.full((1, LANES), val, F32)]
    (dx1, dd), (dg2,), (dfw, loss) = _rowcall(
        body, name="final_loss", T=T, S=S, tr=tr, rows=[(x1, D_MODEL, 0), (d, D_MODEL, 0), (target, D_MODEL, 0)],
        exs=[g2], pars=[fw], row_outs=[(D_MODEL, F32, D_MODEL, 0), (D_MODEL, BF16, D_MODEL, 0)],
        ex_outs=[D_MODEL], par_outs=[(1, D_MODEL), (1, LANES)])
    return loss, dx1, dd, dg2, dfw


def _layer_bwd(sv, prev, mods, w, tabs, dx1, dd, B, S, host=None):
    T = B * S
    tr = _row_tile(S)
    sh1, sc1, g1, sh2, sc2, g2 = mods
    ca, sa, cb, sb = tabs
    gr = {}
    got = {}
    dact = _mm(host, got, "a2a", dd, w["w_down"], "nt", BF16, "mm_down_dx")
    gr["w_down"] = _matmul(sv["act"], dd, "tn", BF16, "mm_down_dw")

    def body(rv, ev, pv):
        _, vjp = jax.vjp(_f_swiglu, rv[0].astype(F32), rv[1].astype(F32))
        dgate, dup = vjp(rv[2].astype(F32))
        return [[dgate, dup]], [], []
    (dgu,), _, _ = _rowcall(body, name="b_swiglu", T=T, S=S, tr=tr,
                            rows=[(sv["gu"], FFN_HIDDEN, 0), (sv["gu"], FFN_HIDDEN, 1), (dact, FFN_HIDDEN, 0)],
                            row_outs=[(2 * FFN_HIDDEN, BF16, 2 * FFN_HIDDEN, 0)])
    dh2 = _mm(host, got, "a2a", dgu, w["w_gate_up"], "nt", F32, "mm_gate_up_dx", b_chunks=True)
    gr["w_gate_up"] = _mm(host, got, "a2a", sv["h2"], dgu, "tn", BF16, "mm_gate_up_dw", out_chunks=True)

    def body(rv, ev, pv):
        xa, delta, dh, dxn = rv
        _, vjp = jax.vjp(_f_resid_norm_mod, xa, delta, ev[0], pv[0], ev[1], ev[2])
        dxa, ddelta, dg, dw, dsc, dsh = vjp((dxn, dh))
        return [dxa, ddelta], [dg, dsc, dsh], [dw]
    (dx, do), (dg1, dsc2, dsh2), (gr["norm2_w"],) = _rowcall(
        body, name="b_resid_norm_mod2", T=T, S=S, tr=tr,
        rows=[(sv["x"], D_MODEL, 0), (sv["o"], D_MODEL, 0), (dh2, D_MODEL, 0), (dx1, D_MODEL, 0)],
        exs=[g1, sc2, sh2], pars=[w["norm2_w"]],
        row_outs=[(D_MODEL, F32, D_MODEL, 0), (D_MODEL, BF16, D_MODEL, 0)], ex_outs=[D_MODEL] * 3,
        par_outs=[(1, D_MODEL)])
    dy = _matmul(do, w["w_out"], "nt", F32, "mm_out_dx")
    gr["w_out"] = _matmul(sv["y"], do, "tn", BF16, "mm_out_dw")

    def body(rv, ev, pv):
        _, vjp = jax.vjp(_f_outnorm, rv[0], rv[1], rv[2], pv[0])
        doa, dob, dyc, dgw = vjp(rv[3])
        return [doa, dob, dyc], [], [dgw]
    (doa, dob, dyc), _, (gr["out_norm_w"],) = _rowcall(
        body, name="b_outnorm", T=T, S=S, tr=tr,
        rows=[(sv["oa"], 768, 0), (sv["ob"], 768, 0), (sv["yc"], C_WIDTH, 0), (dy, Y_END, 0)], pars=[w["out_norm_w"]],
        row_outs=[(768, F32, 768, 0), (768, F32, 768, 0), (C_WIDTH, F32, C_WIDTH, 0)], par_outs=[(1, Y_END)])

    ta, tb, band = _attn_tiles(S)
    dQ, dK, dV, _ = _attn_bwd(sv["Q"], sv["K"], sv["V"], sv["ob"], sv["lse_b"], dob, None, B=B, S=S, HQ=B_HEADS,
                              HK=B_HEADS, window=None, scale=(B_NOPE + B_ROPE) ** -0.5, tq=tb, band=band,
                              name="attn_b_bwd")
    dqa, dka, dva, dsink = _attn_bwd(sv["qa"], sv["ka"], sv["va"], sv["oa"], sv["lse_a"], doa, w["sinks"], B=B, S=S,
                                     HQ=A_Q_HEADS, HK=A_KV_HEADS, window=A_WINDOW, scale=HEAD_DIM ** -0.5, tq=ta,
                                     band=None, name="attn_a_bwd")
    gr["sinks"] = dsink

    def body(rv, ev, pv):
        dQv, dKv, dVv, cbv, sbv = rv
        dq = [_rope_bwd((cbv, sbv), p)[0] for p in _heads(dQv, B_HEADS)]
        dkr = None
        for p in _heads(dKv, B_HEADS):
            dkr = p if dkr is None else dkr + p
        return [dq, [dKv, dVv], dkr], [], []
    (dq, dkv, dkr), _, _ = _rowcall(
        body, name="b_mlaprep", T=T, S=S, tr=tr,
        rows=[(dQ, 768, 0), (dK, 768, 0), (dV, 768, 0), (cb, LANES, 0), (sb, LANES, 0)],
        row_outs=[(768, BF16, 768, 0), (1536, BF16, 1536, 0), (LANES, F32, LANES, 0)])
    dcqn = _matmul(dq, w["b_w_uq"], "nt", F32, "mm_uq_dx")
    gr["b_w_uq"] = _matmul(sv["cqn"], dq, "tn", BF16, "mm_uq_dw")
    dckvn = _matmul(dkv, w["b_w_ukv"], "nt", F32, "mm_ukv_dx")
    gr["b_w_ukv"] = _matmul(sv["ckvn"], dkv, "tn", BF16, "mm_ukv_dw")

    def body(rv, ev, pv):
        proj, cav, sav, cbv, sbv, dqa_, dka_, dva_, dcqn_, dckvn_, dkr_ = rv
        _, vjp = jax.vjp(lambda p, a, b_: _f_mixprep(p, cav, sav, cbv, sbv, a, b_), proj, pv[0], pv[1])
        dproj, dqn, dkvn = vjp((dqa_, dka_, dva_, dcqn_, dckvn_, dkr_))
        return [dproj], [], [dqn, dkvn]
    (dproj,), _, (gr["b_q_norm_w"], gr["b_kv_norm_w"]) = _rowcall(
        body, name="b_mixprep", T=T, S=S, tr=tr,
        rows=[(sv["proj"], P_CU, 0), (ca, LANES, 0), (sa, LANES, 0), (cb, LANES, 0), (sb, LANES, 0),
              (dqa, 768, 0), (dka, 256, 0), (dva, 256, 0), (dcqn, 384, 0), (dckvn, 256, 0), (dkr, LANES, 0)],
        pars=[w["b_q_norm_w"], w["b_kv_norm_w"]], row_outs=[(P_END, BF16, P_CU, 0)],
        par_outs=[(1, B_Q_RANK), (1, B_KV_RANK)])

    def body(rv, ev, pv):
        cu, cv, dycv, _ = rv
        _, vjp = jax.vjp(_f_sgu, cu, cv, pv[0], pv[1], pv[2], pv[3])
        dcu, dcv, dlw, dlb, dws, dbc = vjp(dycv)
        return [[dcu, dcv]], [], [dlw, dlb, dws, dbc]
    (dproj,), _, (gr["c_ln_w"], gr["c_ln_b"], gr["c_w_s"], gr["c_b_col"]) = _rowcall(
        body, name="b_sgu", T=T, S=S, tr=C_CHUNK,
        rows=[(sv["proj"], C_WIDTH, P_CU // C_WIDTH), (sv["proj"], C_WIDTH, P_CV // C_WIDTH), (dyc, C_WIDTH, 0),
              (dproj, 2 * C_WIDTH, P_CU // (2 * C_WIDTH))],
        pars=[w["c_ln_w"], w["c_ln_b"], w["c_w_s"], w["c_b_col"]],
        row_outs=[(P_END, BF16, 2 * C_WIDTH, P_CU // (2 * C_WIDTH))],
        par_outs=[(1, C_WIDTH), (1, C_WIDTH), (C_GROUPS, C_CHUNK, C_CHUNK), (C_GROUPS, C_CHUNK, 1)],
        aliases={3: 0})
    dh = _mm(host, got, "a2a", dproj, w["w_in"], "nt", F32, "mm_in_dx")
    gr["w_in"] = _matmul(sv["h"], dproj, "tn", BF16, "mm_in_dw")

    if prev is None:
        def body(rv, ev, pv):
            xv, dhv, dxd = rv
            _, vjp = jax.vjp(_f_norm_mod, xv, pv[0], ev[0], ev[1])
            dxa, dw, dsc, dsh = vjp(dhv)
            return [dxa + dxd], [dsc, dsh], [dw]
        (dxin,), (dsc1, dsh1), (gr["norm1_w"],) = _rowcall(
            body, name="b_norm_mod", T=T, S=S, tr=tr, rows=[(sv["x"], D_MODEL, 0), (dh, D_MODEL, 0), (dx, D_MODEL, 0)],
            exs=[sc1, sh1], pars=[w["norm1_w"]], row_outs=[(D_MODEL, F32, D_MODEL, 0)], ex_outs=[D_MODEL] * 2,
            par_outs=[(1, D_MODEL)])
        nxt = (dxin, None, None)
    else:
        x1p, dp, g2p = prev

        def body(rv, ev, pv):
            xa, delta, dhv, dxn = rv
            _, vjp = jax.vjp(_f_resid_norm_mod, xa, delta, ev[0], pv[0], ev[1], ev[2])
            dxa, ddelta, dg, dw, dsc, dsh = vjp((dxn, dhv))
            return [dxa, ddelta], [dg, dsc, dsh], [dw]
        (dx1p, ddp), (dg2p, dsc1, dsh1), (gr["norm1_w"],) = _rowcall(
            body, name="b_resid_norm_mod1", T=T, S=S, tr=tr,
            rows=[(x1p, D_MODEL, 0), (dp, D_MODEL, 0), (dh, D_MODEL, 0), (dx, D_MODEL, 0)],
            exs=[g2p, sc1, sh1], pars=[w["norm1_w"]],
            row_outs=[(D_MODEL, F32, D_MODEL, 0), (D_MODEL, BF16, D_MODEL, 0)], ex_outs=[D_MODEL] * 3,
            par_outs=[(1, D_MODEL)])
        nxt = (dx1p, ddp, dg2p)
    return gr, (dsh1, dsc1, dg1, dsh2, dsc2), nxt, got


def _lane_table(lanes_neg, lanes_pos, inv):
    freq = np.zeros((LANES,), np.int64) - 1
    sign = np.zeros((1, LANES), np.float32)
    n = len(lanes_neg)
    freq[lanes_neg] = np.arange(n)
    freq[lanes_pos] = np.arange(n)
    sign[0, lanes_neg] = -1.0
    sign[0, lanes_pos] = 1.0
    return _select_axis(inv, freq, 0).reshape(1, LANES), jnp.asarray(sign)


SHARDED = ("ada_w", "w_in", "b_w_uq", "b_w_ukv", "w_out", "w_gate_up", "w_down")
ROW_SHARDED = ("w_out", "w_down")
SMALL = ("ada_b", "norm1_w", "a_sinks", "b_q_norm_w", "b_kv_norm_w", "c_ln_w", "c_ln_b", "c_w_s", "c_b_s",
         "out_norm_w", "norm2_w", "final_norm_w")
FWD_HOST = {"mm_in": ("w_in", "w_out", "b_w_uq", "b_w_ukv"), "mm_out": ("w_down",), "mm_gate_up": ("ada_w",),
            "mm_down": ("w_gate_up",)}
BWD_HOST = {"mm_in_dx": ("w_in", "w_out", "b_w_uq", "b_w_ukv"), "mm_down_dx": ("w_down",),
            "mm_gate_up_dx": ("ada_w",), "mm_gate_up_dw": ("w_gate_up",)}


def _from_host(table, got):
    return {k: got[name][i] for name, ks in table.items() for i, k in enumerate(ks)}


def _to_host(table, arrays):
    return {name: [arrays[k] for k in ks] for name, ks in table.items()}


def _join_cols(g):
    return jnp.concatenate([g[j] for j in range(N_CHIPS)], axis=1)


def _split_cols(g):
    n = g.shape[1] // N_CHIPS
    return jnp.stack([g[:, j * n:(j + 1) * n] for j in range(N_CHIPS)])


def _layer_weights(G, small, l):
    D = D_MODEL
    return {
        "norm1_w": small["norm1_w"][l].reshape(1, D),
        "w_in": _pad_axis(_join_cols(G["w_in"]), _map_w_in(), 1),
        "sinks": jnp.broadcast_to(small["a_sinks"][l].reshape(A_Q_HEADS, 1, 1), (A_Q_HEADS, 1, LANES)),
        "b_q_norm_w": small["b_q_norm_w"][l].reshape(1, B_Q_RANK),
        "b_w_uq": _pad_axis(_join_cols(G["b_w_uq"]), _map_w_uq(), 1),
        "b_kv_norm_w": small["b_kv_norm_w"][l].reshape(1, B_KV_RANK),
        "b_w_ukv": _pad_axis(_join_cols(G["b_w_ukv"]), _map_w_ukv(), 1),
        "c_ln_w": small["c_ln_w"][l].reshape(1, C_WIDTH), "c_ln_b": small["c_ln_b"][l].reshape(1, C_WIDTH),
        "c_w_s": small["c_w_s"][l], "c_b_col": small["c_b_s"][l].reshape(C_GROUPS, C_CHUNK, 1),
        "out_norm_w": _pad_axis(small["out_norm_w"][l].reshape(1, D), _map_w_out(), 1),
        "w_out": _pad_axis(G["w_out"].reshape(D, D), _map_w_out(), 0),
        "norm2_w": small["norm2_w"][l].reshape(1, D),
        "w_gate_up": G["w_gate_up"], "w_down": G["w_down"].reshape(FFN_HIDDEN, D),
    }


def _send_buffers(gr, ada_gw):
    D = D_MODEL
    return {
        "ada_w": ada_gw, "w_gate_up": gr["w_gate_up"], "w_down": gr["w_down"].reshape(N_CHIPS, FFN_HIDDEN // N_CHIPS, D),
        "w_out": _unpad_axis(gr["w_out"], _map_w_out(), D, 0).reshape(N_CHIPS, D // N_CHIPS, D),
        "w_in": _split_cols(_unpad_axis(gr["w_in"], _map_w_in(), IN_COLS, 1)),
        "b_w_uq": _split_cols(_unpad_axis(gr["b_w_uq"], _map_w_uq(), B_HEADS * (B_NOPE + B_ROPE), 1)),
        "b_w_ukv": _split_cols(_unpad_axis(gr["b_w_ukv"], _map_w_ukv(), B_HEADS * (B_NOPE + B_V), 1)),
    }


def _small_grads(gr, ada_gb, B):
    D = D_MODEL
    return {
        "ada_b": ada_gb.reshape(N_MOD * D), "norm1_w": gr["norm1_w"].reshape(D),
        "a_sinks": gr["sinks"][:, 0, 0].reshape(B, A_Q_HEADS).sum(axis=0),
        "b_q_norm_w": gr["b_q_norm_w"].reshape(B_Q_RANK), "b_kv_norm_w": gr["b_kv_norm_w"].reshape(B_KV_RANK),
        "c_ln_w": gr["c_ln_w"].reshape(C_WIDTH), "c_ln_b": gr["c_ln_b"].reshape(C_WIDTH), "c_w_s": gr["c_w_s"],
        "c_b_s": gr["c_b_col"].reshape(C_GROUPS, C_CHUNK),
        "out_norm_w": _unpad_axis(gr["out_norm_w"], _map_w_out(), D, 1).reshape(D), "norm2_w": gr["norm2_w"].reshape(D),
    }


def _step(x, c, positions, target, small, shard_of, gathered=None):
    dist = gathered is None
    B, S, D = x.shape
    T = B * S
    xt = x.reshape(T, D)
    tgt = target.reshape(T, D)
    pos_col = positions.astype(F32).reshape(T, 1)
    inv_a = 1.0 / (ROPE_THETA ** (jnp.arange(0, HEAD_DIM, 2, dtype=F32) / HEAD_DIM))
    inv_b = 1.0 / (ROPE_THETA ** (jnp.arange(0, B_ROPE, 2, dtype=F32) / B_ROPE))
    fa, sga = _lane_table(np.arange(32), 64 + np.arange(32), inv_a)
    fb, sgb = _lane_table(48 + np.arange(16), 112 + np.arange(16), inv_b)
    ca, sa = _rope_tables(pos_col, fa, sga, "rope_a")
    cb, sb = _rope_tables(pos_col, fb, sgb, "rope_b")
    tabs = (ca, sa, cb, sb)
    c8 = jnp.zeros((8, D), F32).at[:B].set(c)

    if dist:
        first = shard_of(0)
        G = dict(zip(SHARDED, _xchg_call("gather", [first[k] for k in SHARDED], "gather_first")))
    else:
        G = gathered[0]
    saved, prevs, modss, ws = [], [], [], []
    prev = None
    for l in range(DEPTH):
        w = _layer_weights(G, small, l)
        mod = _ada_fwd(c8, G["ada_w"], small["ada_b"][l].reshape(1, N_MOD * D))
        mods = tuple(mod[:B, i * D:(i + 1) * D].reshape(B, 1, D) for i in range(N_MOD))
        more = l + 1 < DEPTH
        host = _to_host(FWD_HOST, shard_of(l + 1)) if dist and more else None
        prevs.append(prev)
        modss.append(mods)
        ws.append(w)
        prev, sv, got = _layer_fwd(xt, prev, mods, w, tabs, B, S, host)
        saved.append(sv)
        if more:
            G = _from_host(FWD_HOST, got) if dist else gathered[l + 1]
    x1, d, g2 = prev
    loss, dx1, dd, dg2, dfw = _final(x1, d, g2, small["final_norm_w"].reshape(1, D), tgt, B, S)

    landed = [None] * DEPTH
    smalls = [None] * DEPTH
    pending = None
    for l in reversed(range(DEPTH)):
        host = _to_host(BWD_HOST, pending) if dist and pending is not None else None
        gr, (dsh1, dsc1, dg1, dsh2, dsc2), nxt, got = _layer_bwd(saved[l], prevs[l], modss[l], ws[l], tabs, dx1, dd, B, S, host)
        if pending is not None:
            landed[l + 1] = _from_host(BWD_HOST, got) if dist else pending
        dmod = jnp.concatenate([dsh1, dsc1, dg1, dsh2, dsc2, dg2], axis=-1).reshape(B, N_MOD * D)
        ada_gw, ada_gb = _ada_bwd(c8, jnp.zeros((8, N_MOD * D), F32).at[:B].set(dmod))
        pending = _send_buffers(gr, ada_gw)
        smalls[l] = _small_grads(gr, ada_gb, B)
        dx1, dd, dg2 = nxt
    if dist:
        landed[0] = dict(zip(SHARDED, _xchg_call("a2a", [pending[k] for k in SHARDED], "grad_exchange_last")))
    else:
        landed[0] = pending
    small_g = {k: jnp.stack([smalls[l][k] for l in range(DEPTH)]) for k in SMALL if k != "final_norm_w"}
    small_g["final_norm_w"] = dfw.reshape(D)
    return loss[0, 0], dx1.reshape(B, S, D), landed, small_g


def _pack(arrs, cols, mult):
    flat = jnp.concatenate([a.reshape(-1) for a in arrs])
    n = flat.shape[0]
    rows = -(-n // cols)
    rows = -(-rows // mult) * mult
    return jnp.pad(flat, (0, rows * cols - n)).reshape(rows, cols)


def _unpack(blob, shapes):
    flat = blob.reshape(-1)
    out, off = [], 0
    for s in shapes:
        n = int(np.prod(s))
        out.append(flat[off:off + n].reshape(s))
        off += n
    return out


def kernel(x, c, positions, ada_w, ada_b, norm1_w, w_in, a_sinks, b_q_norm_w, b_w_uq, b_kv_norm_w, b_w_ukv, c_ln_w, c_ln_b, c_w_s, c_b_s, out_norm_w, w_out, norm2_w, w_gate_up, w_down, final_norm_w, loss_target, m_ada_w, m_ada_b, m_norm1_w, m_w_in, m_a_sinks, m_b_q_norm_w, m_b_w_uq, m_b_kv_norm_w, m_b_w_ukv, m_c_ln_w, m_c_ln_b, m_c_w_s, m_c_b_s, m_out_norm_w, m_w_out, m_norm2_w, m_w_gate_up, m_w_down, m_final_norm_w, v_ada_w, v_ada_b, v_norm1_w, v_w_in, v_a_sinks, v_b_q_norm_w, v_b_w_uq, v_b_kv_norm_w, v_b_w_ukv, v_c_ln_w, v_c_ln_b, v_c_w_s, v_c_b_s, v_out_norm_w, v_w_out, v_norm2_w, v_w_gate_up, v_w_down, v_final_norm_w):
    names = ("ada_w", "ada_b", "norm1_w", "w_in", "a_sinks", "b_q_norm_w", "b_w_uq", "b_kv_norm_w", "b_w_ukv", "c_ln_w",
             "c_ln_b", "c_w_s", "c_b_s", "out_norm_w", "w_out", "norm2_w", "w_gate_up", "w_down", "final_norm_w")
    ws = dict(zip(names, (ada_w, ada_b, norm1_w, w_in, a_sinks, b_q_norm_w, b_w_uq, b_kv_norm_w, b_w_ukv, c_ln_w, c_ln_b,
                          c_w_s, c_b_s, out_norm_w, w_out, norm2_w, w_gate_up, w_down, final_norm_w)))
    ms = dict(zip(names, (m_ada_w, m_ada_b, m_norm1_w, m_w_in, m_a_sinks, m_b_q_norm_w, m_b_w_uq, m_b_kv_norm_w, m_b_w_ukv,
                          m_c_ln_w, m_c_ln_b, m_c_w_s, m_c_b_s, m_out_norm_w, m_w_out, m_norm2_w, m_w_gate_up, m_w_down,
                          m_final_norm_w)))
    vs = dict(zip(names, (v_ada_w, v_ada_b, v_norm1_w, v_w_in, v_a_sinks, v_b_q_norm_w, v_b_w_uq, v_b_kv_norm_w, v_b_w_ukv,
                          v_c_ln_w, v_c_ln_b, v_c_w_s, v_c_b_s, v_out_norm_w, v_w_out, v_norm2_w, v_w_gate_up, v_w_down,
                          v_final_norm_w)))
    shards = {k: ws[k].astype(BF16) for k in SHARDED}
    loss_local, grad_x, landed, gsmall = _step(x, c, positions, loss_target, {k: ws[k] for k in SMALL},
                                               lambda l: {k: shards[k][l] for k in SHARDED})

    mine = {k: jnp.stack([_sum_slots(landed[l][k], "grad_sum_" + k) for l in range(DEPTH)]) for k in SHARDED}
    theirs = dict(zip(SHARDED, _xchg_call("swap", [mine[k] for k in SHARDED], "grad_sibling_swap")))
    grads, delta, new_m, new_v = {}, {}, {}, {}
    for k in SHARDED:
        shp = ws[k].shape
        two = (shp[0] * shp[1], shp[2])
        g, dlt, nm, nv = _adamw(ws[k].reshape(two), mine[k].reshape(two), theirs[k].reshape(two), ms[k].reshape(two),
                                vs[k].reshape(two), "adamw_" + k)
        grads[k], delta[k], new_m[k], new_v[k] = g.reshape(shp), dlt.reshape(shp), nm.reshape(shp), nv.reshape(shp)

    small_shapes = [ws[k].shape for k in SMALL]
    sblob = _pack([gsmall[k] for k in SMALL] + [loss_local.reshape(1)], LANES, 8)
    svals = _unpack(_all_reduce_small(sblob, "small_all_reduce"), small_shapes + [(1,)])
    loss = svals[-1].reshape(())
    pw = _pack([ws[k] for k in SMALL], LANES, 8)
    pg = _pack(svals[:-1], LANES, 8)
    pm = _pack([ms[k] for k in SMALL], LANES, 8)
    pv = _pack([vs[k] for k in SMALL], LANES, 8)
    g, dlt, nm, nv = _adamw(pw, pg, None, pm, pv, "adamw_small")
    for k, a, b_, c_, d_ in zip(SMALL, _unpack(g, small_shapes), _unpack(dlt, small_shapes), _unpack(nm, small_shapes),
                                _unpack(nv, small_shapes)):
        grads[k], delta[k], new_m[k], new_v[k] = a, b_, c_, d_

    return (loss, grad_x, *[grads[k] for k in names], *[delta[k] for k in names], *[new_m[k] for k in names],
            *[new_v[k] for k in names])
```

```python
import functools
import math

import numpy as np
import jax
import jax.numpy as jnp
from jax import lax
from jax.experimental import pallas as pl
from jax.experimental.pallas import tpu as pltpu

F32 = jnp.float32
BF16 = jnp.bfloat16

D_MODEL = 1024
DEPTH = 4
HEAD_DIM = 64
ROPE_THETA = 10000.0
NORM_EPS = 1e-6
NEG_INF = -1e30
A_Q_HEADS = 6
A_KV_HEADS = 2
A_WINDOW = 128
B_HEADS = 6
B_Q_RANK = 384
B_KV_RANK = 256
B_NOPE = 64
B_ROPE = 32
B_V = 64
C_GROUPS = 4
C_GROUP_DIM = 64
C_WIDTH = 256
C_CHUNK = 128
IN_COLS = 1824
FFN_HIDDEN = 2816
N_MOD = 6
ADAM_LR = 0.001
ADAM_B1 = 0.9
ADAM_B2 = 0.999
ADAM_EPS = 1e-08
ADAM_WD = 0.01
ADAM_STEP = 10

LANES = 128
VMEM_LIMIT = 56 * 1024 * 1024
N_CHIPS = 4

P_AQ, P_AK, P_AV, P_CQ, P_CKV, P_KR, P_CU, P_CV, P_END = 0, 768, 1024, 1280, 1664, 1920, 2048, 2304, 2560
Y_A, Y_B, Y_C, Y_END = 0, 768, 1536, 1792


def _map_w_in():
    idx = -np.ones(P_END, np.int64)
    half = HEAD_DIM // 2
    for h in range(A_Q_HEADS):
        idx[P_AQ + h * LANES + np.arange(half)] = h * HEAD_DIM + np.arange(half)
        idx[P_AQ + h * LANES + 64 + np.arange(half)] = h * HEAD_DIM + half + np.arange(half)
    for h in range(A_KV_HEADS):
        idx[P_AK + h * LANES + np.arange(half)] = 384 + h * HEAD_DIM + np.arange(half)
        idx[P_AK + h * LANES + 64 + np.arange(half)] = 384 + h * HEAD_DIM + half + np.arange(half)
        idx[P_AV + h * LANES + np.arange(HEAD_DIM)] = 512 + h * HEAD_DIM + np.arange(HEAD_DIM)
    idx[P_CQ:P_CQ + 384] = 640 + np.arange(384)
    idx[P_CKV:P_CKV + 256] = 1024 + np.arange(256)
    idx[P_KR + 48 + np.arange(16)] = 1280 + np.arange(16)
    idx[P_KR + 112 + np.arange(16)] = 1296 + np.arange(16)
    idx[P_CU:P_CU + 256] = 1312 + np.arange(256)
    idx[P_CV:P_CV + 256] = 1568 + np.arange(256)
    return idx


def _map_w_uq():
    idx = -np.ones(B_HEADS * LANES, np.int64)
    for h in range(B_HEADS):
        b = h * (B_NOPE + B_ROPE)
        idx[h * LANES + np.arange(48)] = b + np.arange(48)
        idx[h * LANES + 48 + np.arange(16)] = b + 64 + np.arange(16)
        idx[h * LANES + 64 + np.arange(16)] = b + 48 + np.arange(16)
        idx[h * LANES + 112 + np.arange(16)] = b + 80 + np.arange(16)
    return idx


def _map_w_ukv():
    idx = -np.ones(2 * B_HEADS * LANES, np.int64)
    for h in range(B_HEADS):
        b = h * (B_NOPE + B_V)
        idx[h * LANES + np.arange(48)] = b + np.arange(48)
        idx[h * LANES + 64 + np.arange(16)] = b + 48 + np.arange(16)
        idx[B_HEADS * LANES + h * LANES + np.arange(B_V)] = b + B_NOPE + np.arange(B_V)
    return idx


def _map_w_out():
    idx = -np.ones(Y_END, np.int64)
    for h in range(A_Q_HEADS):
        idx[Y_A + h * LANES + np.arange(64)] = h * 64 + np.arange(64)
    for h in range(B_HEADS):
        idx[Y_B + h * LANES + np.arange(64)] = 384 + h * 64 + np.arange(64)
    idx[Y_C:Y_C + 256] = 768 + np.arange(256)
    return idx


def _inverse(idx, n):
    inv = np.zeros(n, np.int64)
    pos = np.nonzero(idx >= 0)[0]
    inv[idx[pos]] = pos
    return inv


def _runs(idx):
    runs, i, n = [], 0, len(idx)
    while i < n:
        j = i + 1
        while j < n and ((idx[i] < 0 and idx[j] < 0) or (idx[i] >= 0 and idx[j] == idx[i] + (j - i))):
            j += 1
        runs.append((int(idx[i]), j - i))
        i = j
    return runs


def _select_axis(w, idx, axis):
    pieces = []
    for start, length in _runs(idx):
        if start < 0:
            shape = list(w.shape)
            shape[axis] = length
            pieces.append(jnp.zeros(shape, w.dtype))
        else:
            pieces.append(lax.slice_in_dim(w, start, start + length, axis=axis))
    return jnp.concatenate(pieces, axis=axis)


def _pad_axis(w, idx, axis):
    return _select_axis(w, idx, axis)


def _unpad_axis(g, idx, n, axis):
    return _select_axis(g, _inverse(idx, n), axis)


def _params(sem):
    return pltpu.CompilerParams(dimension_semantics=sem, vmem_limit_bytes=VMEM_LIMIT)


def _tile(dim, target):
    if dim <= target:
        return dim
    best = None
    for t in range(LANES, target + 1, LANES):
        if dim % t == 0:
            best = t
    assert best is not None, dim
    return best


def _row_div(rows, target):
    if rows <= target:
        return rows
    best = None
    for t in range(8, target + 1, 8):
        if rows % t == 0:
            best = t
    assert best is not None, rows
    return best


_ANY = pl.BlockSpec(memory_space=pl.ANY)
_MESH = pl.DeviceIdType.MESH


def _xchg_out_shapes(kind, srcs):
    if kind == "gather":
        return [jax.ShapeDtypeStruct((N_CHIPS,) + s.shape, s.dtype) for s in srcs]
    return [jax.ShapeDtypeStruct(s.shape, s.dtype) for s in srcs]


def _xchg_scratch(kind, n):
    per = 1 if kind == "swap" else N_CHIPS - 1
    return [pltpu.SemaphoreType.DMA((per * n,)), pltpu.SemaphoreType.DMA((per * n,)), pltpu.SemaphoreType.DMA((n,))]


def _xchg_copies(kind, srcs, dsts, send_sems, recv_sems, local_sems, arrivals):
    x, y, c = lax.axis_index("x"), lax.axis_index("y"), lax.axis_index("c")
    me = 2 * x + y
    peers = [(1 - x, y), (x, 1 - y), (1 - x, 1 - y)]
    local, out, back = [], [], []
    for i, (s, d) in enumerate(zip(srcs, dsts)):
        if kind == "swap":
            cp = pltpu.make_async_remote_copy(src_ref=s, dst_ref=d, send_sem=send_sems.at[i], recv_sem=recv_sems.at[i],
                                              device_id=(x, y, 1 - c), device_id_type=_MESH)
            out.append(cp)
            back.append(cp)
            continue
        local.append(pltpu.make_async_copy(s if kind == "gather" else s.at[me], d.at[me], local_sems.at[i]))
        for kk, (px, py) in enumerate(peers):
            j = (N_CHIPS - 1) * i + kk
            theirs = 2 * px + py
            out.append(pltpu.make_async_remote_copy(
                src_ref=s if kind == "gather" else s.at[theirs], dst_ref=d.at[me], send_sem=send_sems.at[j],
                recv_sem=recv_sems.at[j], device_id=(px, py, c), device_id_type=_MESH))
            if arrivals:
                back.append(pltpu.make_async_remote_copy(
                    src_ref=s if kind == "gather" else s.at[me], dst_ref=d.at[theirs], send_sem=send_sems.at[j],
                    recv_sem=recv_sems.at[j], device_id=(px, py, c), device_id_type=_MESH))
    return local, out, back


def _xchg_start(kind, srcs, dsts, sems):
    local, out, _ = _xchg_copies(kind, srcs, dsts, *sems, arrivals=False)
    for cp in local + out:
        cp.start()


def _xchg_wait(kind, srcs, dsts, sems):
    local, out, back = _xchg_copies(kind, srcs, dsts, *sems, arrivals=True)
    for cp in back:
        cp.wait_recv()
    for cp in out:
        cp.wait_send()
    for cp in local:
        cp.wait()


def _xchg_at_ends(kind, srcs, dsts, sems, grid, first):
    ids = [pl.program_id(a) for a in range(len(grid))]
    cond = None
    for i, n in zip(ids, grid):
        c = (i == 0) if first else (i == n - 1)
        cond = c if cond is None else jnp.logical_and(cond, c)

    @pl.when(cond)
    def _():
        (_xchg_start if first else _xchg_wait)(kind, srcs, dsts, sems)


def _xchg_call(kind, srcs, name):
    n = len(srcs)

    def kern(*refs):
        s, d, sems = refs[:n], refs[n:2 * n], refs[2 * n:]
        _xchg_start(kind, s, d, sems)
        _xchg_wait(kind, s, d, sems)

    return pl.pallas_call(
        kern, name=name, in_specs=[_ANY] * n, out_specs=[_ANY] * n, out_shape=_xchg_out_shapes(kind, srcs),
        scratch_shapes=_xchg_scratch(kind, n),
    )(*srcs)


_DIMS = {"nn": (((1,), (0,)), ((), ())), "nt": (((1,), (1,)), ((), ())), "tn": (((0,), (0,)), ((), ()))}


def _matmul(a, b, mode, out_dtype, name, *, b_chunks=False, out_chunks=False, xchg=None):
    if b_chunks:
        nchunk, brows, bcols = b.shape
        bshape = (brows, nchunk * bcols)
    else:
        bshape = b.shape
    if mode == "nn":
        (m, k), (_, n) = a.shape, bshape
    elif mode == "nt":
        (m, k), (n, _) = a.shape, bshape
    else:
        (k, m), (_, n) = a.shape, bshape
    tm, tk = (1408, 1024) if mode == "tn" else (1024, 1408)
    tm, tn, tk = _tile(m, tm), _tile(n, 1408), _tile(k, tk)
    if b_chunks:
        if mode == "nn":
            tn = bcols
        else:
            assert mode == "nt"
            tk = bcols
    if out_chunks:
        assert n % N_CHIPS == 0
        tn = n // N_CHIPS
    ni, nj, nk = m // tm, n // tn, k // tk
    dims = _DIMS[mode]
    n_x = 0 if xchg is None else len(xchg[1])

    def kern(*refs):
        a_ref, b_ref = refs[0], refs[1]
        xs = refs[2:2 + n_x]
        o_ref = refs[2 + n_x]
        xd = refs[3 + n_x:3 + 2 * n_x]
        acc_ref = refs[3 + 2 * n_x]
        sems = refs[4 + 2 * n_x:]
        kk = pl.program_id(2)
        if n_x:
            _xchg_at_ends(xchg[0], xs, xd, sems, (ni, nj, nk), True)

        @pl.when(kk == 0)
        def _():
            acc_ref[...] = jnp.zeros_like(acc_ref)

        acc_ref[...] += lax.dot_general(a_ref[...], b_ref[...], dims, preferred_element_type=F32)

        @pl.when(kk == nk - 1)
        def _():
            o_ref[...] = acc_ref[...].astype(o_ref.dtype)

        if n_x:
            _xchg_at_ends(xchg[0], xs, xd, sems, (ni, nj, nk), False)

    if mode == "tn":
        a_spec = pl.BlockSpec((tk, tm), lambda i, j, kk: (kk, i))
    else:
        a_spec = pl.BlockSpec((tm, tk), lambda i, j, kk: (i, kk))
    if b_chunks and mode == "nn":
        b_spec = pl.BlockSpec((None, tk, tn), lambda i, j, kk: (j, kk, 0))
    elif b_chunks:
        b_spec = pl.BlockSpec((None, tn, tk), lambda i, j, kk: (kk, j, 0))
    elif mode == "nt":
        b_spec = pl.BlockSpec((tn, tk), lambda i, j, kk: (j, kk))
    else:
        b_spec = pl.BlockSpec((tk, tn), lambda i, j, kk: (kk, j))
    if out_chunks:
        o_spec = pl.BlockSpec((None, tm, tn), lambda i, j, kk: (j, i, 0))
        o_shape = jax.ShapeDtypeStruct((N_CHIPS, m, tn), out_dtype)
    else:
        o_spec = pl.BlockSpec((tm, tn), lambda i, j, kk: (i, j))
        o_shape = jax.ShapeDtypeStruct((m, n), out_dtype)
    xs = [] if xchg is None else list(xchg[1])
    res = pl.pallas_call(
        kern, name=name, grid=(ni, nj, nk),
        in_specs=[a_spec, b_spec] + [_ANY] * n_x, out_specs=[o_spec] + [_ANY] * n_x,
        out_shape=[o_shape] + (_xchg_out_shapes(xchg[0], xs) if n_x else []),
        scratch_shapes=[pltpu.VMEM((tm, tn), F32)] + (_xchg_scratch(xchg[0], n_x) if n_x else []),
        compiler_params=_params(("arbitrary", "arbitrary", "arbitrary") if n_x else ("parallel", "parallel", "arbitrary")),
    )(a, b, *xs)
    return (res[0], list(res[1:])) if n_x else res[0]


def _rowcall(body, *, name, T, S, tr, rows, exs=(), pars=(), row_outs=(), ex_outs=(), par_outs=(), aliases=None):
    assert S % tr == 0 and T % S == 0
    per_ex = S // tr
    nb = T // S
    n_rows, n_exs, n_pars = len(rows), len(exs), len(pars)
    n_ro, n_eo, n_po = len(row_outs), len(ex_outs), len(par_outs)

    def kern(*refs):
        ins = refs[:n_rows + n_exs + n_pars]
        outs = refs[n_rows + n_exs + n_pars:]
        rv = [r[...].astype(F32) for r in ins[:n_rows]]
        ev = [r[0] for r in ins[n_rows:n_rows + n_exs]]
        pv = [r[...] for r in ins[n_rows + n_exs:]]
        ro, eo, po = body(rv, ev, pv)
        i = pl.program_id(0)
        for ref, val in zip(outs[:n_ro], ro):
            if isinstance(val, (list, tuple)):
                off = 0
                for piece in val:
                    w = piece.shape[-1]
                    ref[:, off:off + w] = piece.astype(ref.dtype)
                    off += w
            else:
                ref[...] = val.astype(ref.dtype)
        first_of_ex = (i % per_ex) == 0
        for ref, val in zip(outs[n_ro:n_ro + n_eo], eo):
            @pl.when(first_of_ex)
            def _(ref=ref, val=val):
                ref[0] = val

            @pl.when(jnp.logical_not(first_of_ex))
            def _(ref=ref, val=val):
                ref[0] += val
        for ref, val in zip(outs[n_ro + n_eo:], po):
            @pl.when(i == 0)
            def _(ref=ref, val=val):
                ref[...] = val

            @pl.when(i != 0)
            def _(ref=ref, val=val):
                ref[...] += val

    in_specs = [pl.BlockSpec((tr, w), functools.partial(lambda i, cb: (i, cb), cb=cb)) for (_, w, cb) in rows]
    in_specs += [pl.BlockSpec((1, 1, e.shape[-1]), lambda i: (i // per_ex, 0, 0)) for e in exs]
    in_specs += [pl.BlockSpec(p.shape, functools.partial(lambda i, nd: (0,) * nd, nd=p.ndim)) for p in pars]
    out_specs = [pl.BlockSpec((tr, w), functools.partial(lambda i, cb: (i, cb), cb=cb)) for (_, _, w, cb) in row_outs]
    out_specs += [pl.BlockSpec((1, 1, f), lambda i: (i // per_ex, 0, 0)) for f in ex_outs]
    out_specs += [pl.BlockSpec(tuple(s), functools.partial(lambda i, nd: (0,) * nd, nd=len(s))) for s in par_outs]
    out_shape = [jax.ShapeDtypeStruct((T, tw), dt) for (tw, dt, _, _) in row_outs]
    out_shape += [jax.ShapeDtypeStruct((nb, 1, f), F32) for f in ex_outs]
    out_shape += [jax.ShapeDtypeStruct(tuple(s), F32) for s in par_outs]
    res = pl.pallas_call(
        kern, name=name, grid=(T // tr,), in_specs=in_specs, out_specs=out_specs, out_shape=out_shape,
        input_output_aliases=aliases or {}, compiler_params=_params(("arbitrary",)),
    )(*[r[0] for r in rows], *exs, *pars)
    return res[:n_ro], res[n_ro:n_ro + n_eo], res[n_ro + n_eo:]


def _rms(x, w, n=None):
    n = x.shape[-1] if n is None else n
    ms = jnp.sum(x * x, axis=-1, keepdims=True) * (1.0 / n)
    return x * lax.rsqrt(ms + NORM_EPS) * w


def _gelu(x):
    return 0.5 * x * (1.0 + lax.erf(x * np.float32(1.0 / math.sqrt(2.0))))


def _silu(x):
    return x * jax.nn.sigmoid(x)


@jax.custom_vjp
def _rope(x, cos, sin):
    return x * cos + pltpu.roll(x, 64, 1) * sin


def _rope_fwd(x, cos, sin):
    return _rope(x, cos, sin), (cos, sin)


def _rope_bwd(res, dy):
    cos, sin = res
    return dy * cos + pltpu.roll(dy * sin, 64, 1), None, None


_rope.defvjp(_rope_fwd, _rope_bwd)


def _heads(x, n):
    return [x[:, h * LANES:(h + 1) * LANES] for h in range(n)]


def _f_norm_mod(x, w, sc, sh):
    return _rms(x, w) * (1.0 + sc) + sh


def _f_resid_norm_mod(xa, delta, g, w, sc, sh):
    xn = xa + g * delta
    return xn, _f_norm_mod(xn, w, sc, sh)


def _f_mixprep(proj, ca, sa, cb, sb, qnw, kvnw):
    qa = [_rope(p, ca, sa) for p in _heads(proj[:, P_AQ:P_AK], A_Q_HEADS)]
    ka = [_rope(p, ca, sa) for p in _heads(proj[:, P_AK:P_AV], A_KV_HEADS)]
    va = proj[:, P_AV:P_CQ]
    cqn = _rms(proj[:, P_CQ:P_CKV], qnw)
    ckvn = _rms(proj[:, P_CKV:P_KR], kvnw)
    kr = _rope(proj[:, P_KR:P_CU], cb, sb)
    return jnp.concatenate(qa, -1), jnp.concatenate(ka, -1), va, cqn, ckvn, kr


def _f_mlaprep(q, kv, kr, cb, sb):
    qs = [_rope(p, cb, sb) for p in _heads(q, B_HEADS)]
    ks = [p + kr for p in _heads(kv[:, :B_HEADS * LANES], B_HEADS)]
    return jnp.concatenate(qs, -1), jnp.concatenate(ks, -1), kv[:, B_HEADS * LANES:]


def _f_sgu(cu, cv, ln_w, ln_b, w_s, b_col):
    u = _gelu(cu)
    v = _gelu(cv)
    mu = jnp.mean(v, axis=-1, keepdims=True)
    var = jnp.mean(jnp.square(v - mu), axis=-1, keepdims=True)
    vn = (v - mu) * lax.rsqrt(var + NORM_EPS) * ln_w + ln_b
    r = lax.broadcasted_iota(jnp.int32, (C_CHUNK, C_CHUNK), 0)
    c = lax.broadcasted_iota(jnp.int32, (C_CHUNK, C_CHUNK), 1)
    lane = lax.broadcasted_iota(jnp.int32, (1, C_WIDTH), 1)
    mixed = jnp.zeros(cu.shape, F32)
    for g in range(C_GROUPS):
        gm = (lane // C_GROUP_DIM == g).astype(F32)
        wg = jnp.where(r >= c, w_s[g], 0.0).astype(BF16)
        mixed = mixed + jnp.dot(wg, (vn * gm).astype(BF16), preferred_element_type=F32) + b_col[g] * gm
    return u * mixed


def _f_outnorm(oa, ob, yc, gw):
    ya = _rms(oa, gw[:, Y_A:Y_B], A_Q_HEADS * HEAD_DIM)
    yb = _rms(ob, gw[:, Y_B:Y_C], B_HEADS * B_V)
    ycn = _rms(yc, gw[:, Y_C:Y_END])
    return jnp.concatenate([ya, yb, ycn], -1)


def _f_swiglu(gate, up):
    return _silu(gate) * up


def _mask(q_start, k_start, tq, tk, window):
    qpos = q_start + lax.broadcasted_iota(jnp.int32, (tq, tk), 0)
    kpos = k_start + lax.broadcasted_iota(jnp.int32, (tq, tk), 1)
    m = kpos <= qpos
    if window is not None:
        m = jnp.logical_and(m, qpos - kpos < window)
    return m


def _tile_fwd(qv, kk, vv, q_start, k_start, n_free, scale, window, m0, l0):
    tq = qv.shape[0]
    W = kk.shape[0]
    parts = []
    if n_free > 0:
        parts.append((lax.dot_general(qv, kk[:n_free], _DIMS["nt"], preferred_element_type=F32) * scale, vv[:n_free]))
    if W > n_free:
        s = lax.dot_general(qv, kk[n_free:], _DIMS["nt"], preferred_element_type=F32) * scale
        s = jnp.where(_mask(q_start, k_start + n_free, tq, W - n_free, window), s, NEG_INF)
        parts.append((s, vv[n_free:]))
    m = m0
    for s, _ in parts:
        mx = jnp.max(s, axis=-1, keepdims=True)
        m = mx if m is None else jnp.maximum(m, mx)
    l = None if l0 is None else l0 * jnp.exp(m0 - m)
    o = None
    for s, vpart in parts:
        p = jnp.exp(s - m)
        ps = jnp.sum(p, axis=-1, keepdims=True)
        l = ps if l is None else l + ps
        po = jnp.dot(p.astype(BF16), vpart, preferred_element_type=F32)
        o = po if o is None else o + po
    return o / l, m + jnp.log(l)


def _tile_bwd(qv, kk, vv, dof, ov, lse, q_start, k_start, n_free, scale, window):
    tq = qv.shape[0]
    W = kk.shape[0]
    dob = dof.astype(BF16)
    delta = jnp.sum(dof * ov, axis=-1, keepdims=True)
    dq = None
    outs = []
    for (a, b, masked) in ((0, n_free, False), (n_free, W, True)):
        if b <= a:
            continue
        kp, vp = kk[a:b], vv[a:b]
        s = lax.dot_general(qv, kp, _DIMS["nt"], preferred_element_type=F32) * scale
        if masked:
            s = jnp.where(_mask(q_start, k_start + a, tq, b - a, window), s, NEG_INF)
        p = jnp.exp(s - lse)
        dp = lax.dot_general(dob, vp, _DIMS["nt"], preferred_element_type=F32)
        ds = (p * (dp - delta) * scale).astype(BF16)
        d = jnp.dot(ds, kp, preferred_element_type=F32)
        dq = d if dq is None else dq + d
        dkp = lax.dot_general(ds, qv, _DIMS["tn"], preferred_element_type=F32)
        dvp = lax.dot_general(p.astype(BF16), dob, _DIMS["tn"], preferred_element_type=F32)
        outs.append((a, dkp, dvp))
    return dq, outs


def _attn_fwd(q, k, v, sinks, *, B, S, HQ, HK, window, scale, tq, band, name, xchg=None):
    G = HQ // HK
    nq = S // tq
    T = B * S
    has_sink = sinks is not None
    n_x = 0 if xchg is None else len(xchg[1])
    n_in = 4 if has_sink else 3

    def kern(*refs):
        xs, xd, sems = refs[n_in:n_in + n_x], refs[n_in + n_x + 2:n_in + 2 * n_x + 2], refs[n_in + 2 * n_x + 2:]
        refs = refs[:n_in] + refs[n_in + n_x:n_in + n_x + 2]
        if n_x:
            _xchg_at_ends(xchg[0], xs, xd, sems, (B, HQ, nq), True)
        if has_sink:
            q_ref, k_ref, v_ref, s_ref, o_ref, lse_ref = refs
        else:
            q_ref, k_ref, v_ref, o_ref, lse_ref = refs
        q_start = pl.program_id(2) * tq
        qv = q_ref[...]
        if has_sink:
            m0 = jnp.broadcast_to(s_ref[0][:, :1], (tq, 1))
            l0 = jnp.ones((tq, 1), F32)
        else:
            m0 = l0 = None

        def finish(o, lse):
            o_ref[...] = o
            lse_ref[...] = jnp.broadcast_to(lse, (tq, LANES))

        if window is None:
            bidx = q_start // band
            for bb in range(S // band):
                @pl.when(bidx == bb)
                def _(bb=bb):
                    W = (bb + 1) * band
                    finish(*_tile_fwd(qv, k_ref[0:W, :], v_ref[0:W, :], q_start, 0, bb * band, scale, None, m0, l0))
        else:
            W = min(S, tq + window)
            k_start = pl.multiple_of(jnp.maximum(q_start - window, 0), window)
            finish(*_tile_fwd(qv, k_ref[pl.ds(k_start, W), :], v_ref[pl.ds(k_start, W), :], q_start, k_start, 0, scale,
                              window, m0, l0))
        if n_x:
            _xchg_at_ends(xchg[0], xs, xd, sems, (B, HQ, nq), False)

    q_spec = pl.BlockSpec((tq, LANES), lambda b, h, i: (b * nq + i, h))
    kv_spec = pl.BlockSpec((S, LANES), lambda b, h, i: (b, h // G))
    in_specs = [q_spec, kv_spec, kv_spec]
    args = [q, k, v]
    if has_sink:
        in_specs.append(pl.BlockSpec((1, 1, LANES), lambda b, h, i: (h, 0, 0)))
        args.append(sinks)
    xs = [] if xchg is None else list(xchg[1])
    res = pl.pallas_call(
        kern, name=name, grid=(B, HQ, nq), in_specs=in_specs + [_ANY] * n_x, out_specs=[q_spec, q_spec] + [_ANY] * n_x,
        out_shape=[jax.ShapeDtypeStruct((T, HQ * LANES), F32), jax.ShapeDtypeStruct((T, HQ * LANES), F32)]
        + (_xchg_out_shapes(xchg[0], xs) if n_x else []),
        scratch_shapes=_xchg_scratch(xchg[0], n_x) if n_x else [],
        compiler_params=_params(("arbitrary",) * 3 if n_x else ("parallel", "parallel", "arbitrary")),
    )(*args, *xs)
    return ((res[0], res[1]), list(res[2:])) if n_x else res


def _attn_bwd(q, k, v, o, lse, do, sinks, *, B, S, HQ, HK, window, scale, tq, band, name, xchg=None):
    G = HQ // HK
    nq = S // tq
    T = B * S
    has_sink = sinks is not None
    n_x = 0 if xchg is None else len(xchg[1])
    n_in, n_out = (7, 4) if has_sink else (6, 3)

    def kern(*refs):
        xs, xd = refs[n_in:n_in + n_x], refs[n_in + n_x + n_out:n_in + 2 * n_x + n_out]
        sems = refs[n_in + 2 * n_x + n_out:]
        refs = refs[:n_in] + refs[n_in + n_x:n_in + n_x + n_out]
        if n_x:
            _xchg_at_ends(xchg[0], xs, xd, sems, (B, HK, G, nq), True)
        if has_sink:
            q_ref, k_ref, v_ref, o_ref, lse_ref, do_ref, s_ref, dq_ref, dk_ref, dv_ref, ds_ref = refs
        else:
            q_ref, k_ref, v_ref, o_ref, lse_ref, do_ref, dq_ref, dk_ref, dv_ref = refs
        gi = pl.program_id(2)
        qi = pl.program_id(3)
        q_start = qi * tq

        @pl.when(jnp.logical_and(gi == 0, qi == 0))
        def _():
            dk_ref[...] = jnp.zeros_like(dk_ref)
            dv_ref[...] = jnp.zeros_like(dv_ref)

        qv = q_ref[...]
        dof = do_ref[...]
        ov = o_ref[...]
        lse_v = lse_ref[...][:, :1]
        if window is None:
            bidx = q_start // band
            for bb in range(S // band):
                @pl.when(bidx == bb)
                def _(bb=bb):
                    W = (bb + 1) * band
                    dq, outs = _tile_bwd(qv, k_ref[0:W, :], v_ref[0:W, :], dof, ov, lse_v, q_start, 0, bb * band, scale, None)
                    dq_ref[...] = dq
                    for a, dkp, dvp in outs:
                        dk_ref[a:a + dkp.shape[0], :] += dkp
                        dv_ref[a:a + dvp.shape[0], :] += dvp
        else:
            W = min(S, tq + window)
            k_start = pl.multiple_of(jnp.maximum(q_start - window, 0), window)
            dq, outs = _tile_bwd(qv, k_ref[pl.ds(k_start, W), :], v_ref[pl.ds(k_start, W), :], dof, ov, lse_v, q_start,
                                 k_start, 0, scale, window)
            dq_ref[...] = dq
            (_, dkp, dvp), = outs
            dk_ref[pl.ds(k_start, W), :] += dkp
            dv_ref[pl.ds(k_start, W), :] += dvp
        if has_sink:
            delta = jnp.sum(dof * ov, axis=-1, keepdims=True)
            sink = s_ref[0][:, :1]
            part = -jnp.sum(jnp.exp(sink - lse_v) * delta, axis=0, keepdims=True)
            part = jnp.broadcast_to(part, (1, LANES))

            @pl.when(qi == 0)
            def _():
                ds_ref[0] = part

            @pl.when(qi != 0)
            def _():
                ds_ref[0] += part
        if n_x:
            _xchg_at_ends(xchg[0], xs, xd, sems, (B, HK, G, nq), False)

    q_spec = pl.BlockSpec((tq, LANES), lambda b, hk, g, i: (b * nq + i, hk * G + g))
    kv_spec = pl.BlockSpec((S, LANES), lambda b, hk, g, i: (b, hk))
    in_specs = [q_spec, kv_spec, kv_spec, q_spec, q_spec, q_spec]
    args = [q, k, v, o, lse, do]
    out_specs = [q_spec, kv_spec, kv_spec]
    out_shape = [jax.ShapeDtypeStruct((T, HQ * LANES), F32), jax.ShapeDtypeStruct((T, HK * LANES), F32),
                 jax.ShapeDtypeStruct((T, HK * LANES), F32)]
    if has_sink:
        in_specs.append(pl.BlockSpec((1, 1, LANES), lambda b, hk, g, i: (hk * G + g, 0, 0)))
        args.append(sinks)
        out_specs.append(pl.BlockSpec((1, 1, LANES), lambda b, hk, g, i: (b * HQ + hk * G + g, 0, 0)))
        out_shape.append(jax.ShapeDtypeStruct((B * HQ, 1, LANES), F32))
    xs = [] if xchg is None else list(xchg[1])
    res = pl.pallas_call(
        kern, name=name, grid=(B, HK, G, nq), in_specs=in_specs + [_ANY] * n_x, out_specs=out_specs + [_ANY] * n_x,
        out_shape=out_shape + (_xchg_out_shapes(xchg[0], xs) if n_x else []),
        scratch_shapes=_xchg_scratch(xchg[0], n_x) if n_x else [],
        compiler_params=_params(("arbitrary",) * 4 if n_x else ("parallel", "parallel", "arbitrary", "arbitrary")),
    )(*args, *xs)
    main = tuple(res[:n_out]) if has_sink else (*res[:n_out], None)
    return (main, list(res[n_out:])) if n_x else main


def _rope_tables(pos_col, freq, sign, name):
    T = pos_col.shape[0]
    tr = _tile(T, 1024)

    def kern(p_ref, f_ref, s_ref, c_out, s_out):
        ang = p_ref[...] * f_ref[...]
        c_out[...] = jnp.cos(ang)
        s_out[...] = jnp.sin(ang) * s_ref[...]

    spec = pl.BlockSpec((tr, LANES), lambda i: (i, 0))
    par = pl.BlockSpec((1, LANES), lambda i: (0, 0))
    return pl.pallas_call(
        kern, name=name, grid=(T // tr,), in_specs=[pl.BlockSpec((tr, 1), lambda i: (i, 0)), par, par],
        out_specs=[spec, spec], out_shape=[jax.ShapeDtypeStruct((T, LANES), F32)] * 2,
        compiler_params=_params(("parallel",)),
    )(pos_col, freq, sign)


def _ada_fwd(c8, ada_w, ada_b):
    nch, D, tn = ada_w.shape

    def kern(c_ref, w_ref, b_ref, o_ref):
        act = _silu(c_ref[...]).astype(BF16)
        o_ref[...] = jnp.dot(act, w_ref[...], preferred_element_type=F32) + b_ref[...]

    return pl.pallas_call(
        kern, name="ada_fwd", grid=(nch,),
        in_specs=[pl.BlockSpec((8, D), lambda j: (0, 0)), pl.BlockSpec((None, D, tn), lambda j: (j, 0, 0)),
                  pl.BlockSpec((1, tn), lambda j: (0, j))],
        out_specs=pl.BlockSpec((8, tn), lambda j: (0, j)),
        out_shape=jax.ShapeDtypeStruct((8, nch * tn), F32), compiler_params=_params(("parallel",)),
    )(c8, ada_w, ada_b)


def _ada_bwd(c8, dmod):
    _, N = dmod.shape
    D = c8.shape[1]
    tn = N // N_CHIPS

    def kern(c_ref, d_ref, gw_ref, gb_ref):
        act = _silu(c_ref[...]).astype(BF16)
        d = d_ref[...]
        gw_ref[...] = lax.dot_general(act, d.astype(BF16), _DIMS["tn"], preferred_element_type=F32).astype(BF16)
        gb_ref[...] = jnp.sum(d, axis=0, keepdims=True)

    return pl.pallas_call(
        kern, name="ada_bwd", grid=(N_CHIPS,),
        in_specs=[pl.BlockSpec((8, D), lambda j: (0, 0)), pl.BlockSpec((8, tn), lambda j: (0, j))],
        out_specs=[pl.BlockSpec((None, D, tn), lambda j: (j, 0, 0)), pl.BlockSpec((1, tn), lambda j: (0, j))],
        out_shape=[jax.ShapeDtypeStruct((N_CHIPS, D, tn), BF16), jax.ShapeDtypeStruct((1, N), F32)],
        compiler_params=_params(("parallel",)),
    )(c8, dmod)


def _adamw(w, ga, gb, m, v, name):
    rows, cols = w.shape
    tr = _row_div(rows, 256)
    two = gb is not None

    def kern(*refs):
        if two:
            w_ref, ga_ref, gb_ref, m_ref, v_ref, g_out, d_out, m_out, v_out = refs
            gv = ga_ref[...] + gb_ref[...]
        else:
            w_ref, ga_ref, m_ref, v_ref, g_out, d_out, m_out, v_out = refs
            gv = ga_ref[...]
        mn = ADAM_B1 * m_ref[...] + (1.0 - ADAM_B1) * gv
        vn = ADAM_B2 * v_ref[...] + (1.0 - ADAM_B2) * jnp.square(gv)
        m_hat = mn / (1.0 - ADAM_B1 ** ADAM_STEP)
        v_hat = vn / (1.0 - ADAM_B2 ** ADAM_STEP)
        g_out[...] = gv
        d_out[...] = -ADAM_LR * (m_hat / (jnp.sqrt(v_hat) + ADAM_EPS) + ADAM_WD * w_ref[...])
        m_out[...] = mn
        v_out[...] = vn

    spec = pl.BlockSpec((tr, cols), lambda i: (i, 0))
    args = [w, ga, gb, m, v] if two else [w, ga, m, v]
    return pl.pallas_call(
        kern, name=name, grid=(rows // tr,), in_specs=[spec] * len(args), out_specs=[spec] * 4,
        out_shape=[jax.ShapeDtypeStruct((rows, cols), F32)] * 4, compiler_params=_params(("parallel",)),
    )(*args)


def _sum_slots(x, name):
    n, rows, cols = x.shape
    tr = _row_div(rows, 256)

    def kern(x_ref, o_ref):
        acc = x_ref[0].astype(F32)
        for j in range(1, n):
            acc = acc + x_ref[j].astype(F32)
        o_ref[...] = acc

    return pl.pallas_call(
        kern, name=name, grid=(rows // tr,), in_specs=[pl.BlockSpec((n, tr, cols), lambda i: (0, i, 0))],
        out_specs=pl.BlockSpec((tr, cols), lambda i: (i, 0)), out_shape=jax.ShapeDtypeStruct((rows, cols), F32),
        compiler_params=_params(("parallel",)),
    )(x)


def _all_reduce_small(blob, name):
    R, C = blob.shape

    def kern(src, out, pair, chips, send_sems, recv_sems):
        x, y, c = lax.axis_index("x"), lax.axis_index("y"), lax.axis_index("c")
        me = 2 * x + y
        to_sibling = pltpu.make_async_remote_copy(
            src_ref=src, dst_ref=pair, send_sem=send_sems.at[0], recv_sem=recv_sems.at[0],
            device_id=(x, y, 1 - c), device_id_type=_MESH)
        to_sibling.start()
        to_sibling.wait()
        chips[me] = src[...] + pair[...]
        peers = [(1 - x, y), (x, 1 - y), (1 - x, 1 - y)]
        copies = [pltpu.make_async_remote_copy(
            src_ref=chips.at[me], dst_ref=chips.at[me], send_sem=send_sems.at[1 + kk], recv_sem=recv_sems.at[1 + kk],
            device_id=(px, py, c), device_id_type=_MESH) for kk, (px, py) in enumerate(peers)]
        for cp in copies:
            cp.start()
        for kk, (px, py) in enumerate(peers):
            pltpu.make_async_remote_copy(
                src_ref=chips.at[me], dst_ref=chips.at[2 * px + py], send_sem=send_sems.at[1 + kk],
                recv_sem=recv_sems.at[1 + kk], device_id=(px, py, c), device_id_type=_MESH).wait_recv()
        for cp in copies:
            cp.wait_send()
        acc = chips[0]
        for j in range(1, N_CHIPS):
            acc = acc + chips[j]
        out[...] = acc

    vm = pl.BlockSpec(memory_space=pltpu.VMEM)
    return pl.pallas_call(
        kern, name=name, in_specs=[vm], out_specs=vm, out_shape=jax.ShapeDtypeStruct((R, C), F32),
        scratch_shapes=[pltpu.VMEM((R, C), F32), pltpu.VMEM((N_CHIPS, R, C), F32), pltpu.SemaphoreType.DMA((4,)),
                        pltpu.SemaphoreType.DMA((4,))],
        compiler_params=pltpu.CompilerParams(vmem_limit_bytes=VMEM_LIMIT),
    )(blob)


def _row_tile(S):
    return 256 if S % 256 == 0 else 128


def _attn_tiles(S):
    return min(S, 512), min(S, 256), min(S, 512)


def _hosted(fn, host, got, kind, name, *args, **kw):
    if host is not None and host.get(name):
        res, xs = fn(*args, name=name, xchg=(kind, host[name]), **kw)
        got[name] = xs
        return res
    return fn(*args, name=name, **kw)


def _mm(host, got, kind, a, b, mode, out_dtype, name, **kw):
    return _hosted(_matmul, host, got, kind, name, a, b, mode, out_dtype, **kw)


def _sgu_rows(S):
    return 4 * C_CHUNK if S % (4 * C_CHUNK) == 0 else C_CHUNK


def _layer_fwd(xin, prev, mods, w, tabs, B, S, host=None, late=None):
    T = B * S
    tr = _row_tile(S)
    sh1, sc1, g1, sh2, sc2, g2 = mods
    ca, sa, cb, sb = tabs
    sv = {}
    got = {}
    if prev is None:
        def body(rv, ev, pv):
            return [_f_norm_mod(rv[0], pv[0], ev[0], ev[1])], [], []
        (h,), _, _ = _rowcall(body, name="f_norm_mod", T=T, S=S, tr=tr, rows=[(xin, D_MODEL, 0)], exs=[sc1, sh1],
                              pars=[w["norm1_w"]], row_outs=[(D_MODEL, BF16, D_MODEL, 0)])
        x = xin
    else:
        x1p, dp, g2p = prev

        def body(rv, ev, pv):
            xn, hh = _f_resid_norm_mod(rv[0], rv[1], ev[0], pv[0], ev[1], ev[2])
            return [xn, hh], [], []
        (x, h), _, _ = _rowcall(body, name="f_resid_norm_mod1", T=T, S=S, tr=tr,
                                rows=[(x1p, D_MODEL, 0), (dp, D_MODEL, 0)], exs=[g2p, sc1, sh1], pars=[w["norm1_w"]],
                                row_outs=[(D_MODEL, F32, D_MODEL, 0), (D_MODEL, BF16, D_MODEL, 0)])
    sv["x"], sv["h"] = x, h
    proj = _mm(host, got, "gather", h, w["w_in"], "nn", BF16, "mm_in")
    sv["proj"] = proj

    def body(rv, ev, pv):
        outs = _f_mixprep(rv[0], rv[1], rv[2], rv[3], rv[4], pv[0], pv[1])
        return list(outs), [], []
    (qa, ka, va, cqn, ckvn, kr), _, _ = _rowcall(
        body, name="f_mixprep", T=T, S=S, tr=tr,
        rows=[(proj, P_CU, 0), (ca, LANES, 0), (sa, LANES, 0), (cb, LANES, 0), (sb, LANES, 0)],
        pars=[w["b_q_norm_w"], w["b_kv_norm_w"]],
        row_outs=[(768, BF16, 768, 0), (256, BF16, 256, 0), (256, BF16, 256, 0), (384, BF16, 384, 0),
                  (256, BF16, 256, 0), (LANES, F32, LANES, 0)])
    sv.update(qa=qa, ka=ka, va=va, cqn=cqn, ckvn=ckvn)
    q = _matmul(cqn, w["b_w_uq"], "nn", F32, "mm_uq")
    kv = _matmul(ckvn, w["b_w_ukv"], "nn", F32, "mm_ukv")

    def body(rv, ev, pv):
        Q, K, V = _f_mlaprep(rv[0], rv[1], rv[2], rv[3], rv[4])
        return [Q, K, V], [], []
    (Q, K, V), _, _ = _rowcall(
        body, name="f_mlaprep", T=T, S=S, tr=tr,
        rows=[(q, 768, 0), (kv, 1536, 0), (kr, LANES, 0), (cb, LANES, 0), (sb, LANES, 0)],
        row_outs=[(768, BF16, 768, 0)] * 3)
    sv.update(Q=Q, K=K, V=V)
    ta, tb, band = _attn_tiles(S)
    oa, lse_a = _hosted(_attn_fwd, host, got, "gather", "attn_a_fwd", qa, ka, va, w["sinks"], B=B, S=S, HQ=A_Q_HEADS,
                        HK=A_KV_HEADS, window=A_WINDOW, scale=HEAD_DIM ** -0.5, tq=ta, band=None)
    ob, lse_b = _hosted(_attn_fwd, host, got, "gather", "attn_b_fwd", Q, K, V, None, B=B, S=S, HQ=B_HEADS, HK=B_HEADS,
                        window=None, scale=(B_NOPE + B_ROPE) ** -0.5, tq=tb, band=band)
    sv.update(oa=oa, lse_a=lse_a, ob=ob, lse_b=lse_b)
    if late is not None:
        w = {**w, **late(got)}
    ts = _sgu_rows(S)

    def body(rv, ev, pv):
        outs = [_f_sgu(rv[0][r:r + C_CHUNK], rv[1][r:r + C_CHUNK], pv[0], pv[1], pv[2], pv[3])
                for r in range(0, ts, C_CHUNK)]
        return [jnp.concatenate(outs, axis=0)], [], []
    (yc,), _, _ = _rowcall(body, name="f_sgu", T=T, S=S, tr=ts,
                           rows=[(proj, C_WIDTH, P_CU // C_WIDTH), (proj, C_WIDTH, P_CV // C_WIDTH)],
                           pars=[w["c_ln_w"], w["c_ln_b"], w["c_w_s"], w["c_b_col"]],
                           row_outs=[(C_WIDTH, F32, C_WIDTH, 0)])
    sv["yc"] = yc

    def body(rv, ev, pv):
        return [_f_outnorm(rv[0], rv[1], rv[2], pv[0])], [], []
    (y,), _, _ = _rowcall(body, name="f_outnorm", T=T, S=S, tr=tr,
                          rows=[(oa, 768, 0), (ob, 768, 0), (yc, C_WIDTH, 0)], pars=[w["out_norm_w"]],
                          row_outs=[(Y_END, BF16, Y_END, 0)])
    sv["y"] = y
    o = _mm(host, got, "gather", y, w["w_out"], "nn", F32, "mm_out")
    sv["o"] = o

    def body(rv, ev, pv):
        xn, hh = _f_resid_norm_mod(rv[0], rv[1], ev[0], pv[0], ev[1], ev[2])
        return [xn, hh], [], []
    (x1, h2), _, _ = _rowcall(body, name="f_resid_norm_mod2", T=T, S=S, tr=tr,
                              rows=[(x, D_MODEL, 0), (o, D_MODEL, 0)], exs=[g1, sc2, sh2], pars=[w["norm2_w"]],
                              row_outs=[(D_MODEL, F32, D_MODEL, 0), (D_MODEL, BF16, D_MODEL, 0)])
    sv["h2"] = h2
    gu = _mm(host, got, "gather", h2, w["w_gate_up"], "nn", BF16, "mm_gate_up", b_chunks=True)
    sv["gu"] = gu

    def body(rv, ev, pv):
        return [_f_swiglu(rv[0], rv[1])], [], []
    (act,), _, _ = _rowcall(body, name="f_swiglu", T=T, S=S, tr=tr,
                            rows=[(gu, FFN_HIDDEN, 0), (gu, FFN_HIDDEN, 1)], row_outs=[(FFN_HIDDEN, BF16, FFN_HIDDEN, 0)])
    sv["act"] = act
    d = _mm(host, got, "gather", act, w["w_down"], "nn", F32, "mm_down")
    sv["x1"], sv["d"], sv["w"] = x1, d, w
    return (x1, d, g2), sv, got


def _final(x1, d, g2, fw, target, B, S):
    T = B * S
    tr = _row_tile(S)

    def loss_fn(x1v, dv, gv, wv, tv):
        yf = _rms(x1v + gv * dv, wv)
        return 0.5 * jnp.sum(jnp.mean(jnp.square(yf - tv), axis=-1))

    def body(rv, ev, pv):
        x1v, dv, tv = rv
        val, vjp = jax.vjp(lambda a, b_, g, ww: loss_fn(a, b_, g, ww, tv), x1v, dv, ev[0], pv[0])
        dx1, dd, dg, dw = vjp(jnp.ones((), F32))
        return [dx1, dd], [dg], [dw, jnp.full((1, LANES), val, F32)]
    (dx1, dd), (dg2,), (dfw, loss) = _rowcall(
        body, name="final_loss", T=T, S=S, tr=tr, rows=[(x1, D_MODEL, 0), (d, D_MODEL, 0), (target, D_MODEL, 0)],
        exs=[g2], pars=[fw], row_outs=[(D_MODEL, F32, D_MODEL, 0), (D_MODEL, BF16, D_MODEL, 0)],
        ex_outs=[D_MODEL], par_outs=[(1, D_MODEL), (1, LANES)])
    return loss, dx1, dd, dg2, dfw


def _layer_bwd(sv, prev, mods, tabs, dx1, dd, B, S, host=None, own_early=False):
    w = sv["w"]
    T = B * S
    tr = _row_tile(S)
    sh1, sc1, g1, sh2, sc2, g2 = mods
    ca, sa, cb, sb = tabs
    gr = {}
    got = {}
    dact = _mm(host, got, "a2a", dd, w["w_down"], "nt", BF16, "mm_down_dx")
    gr["w_down"] = _matmul(sv["act"], dd, "tn", BF16, "mm_down_dw")

    def body(rv, ev, pv):
        _, vjp = jax.vjp(_f_swiglu, rv[0], rv[1])
        dgate, dup = vjp(rv[2])
        return [[dgate, dup]], [], []
    (dgu,), _, _ = _rowcall(body, name="b_swiglu", T=T, S=S, tr=tr,
                            rows=[(sv["gu"], FFN_HIDDEN, 0), (sv["gu"], FFN_HIDDEN, 1), (dact, FFN_HIDDEN, 0)],
                            row_outs=[(2 * FFN_HIDDEN, BF16, 2 * FFN_HIDDEN, 0)])
    dh2 = _mm(host, got, "a2a", dgu, w["w_gate_up"], "nt", BF16, "mm_gate_up_dx", b_chunks=True)
    gr["w_gate_up"] = _mm(host, got, "a2a", sv["h2"], dgu, "tn", BF16, "mm_gate_up_dw", out_chunks=True)

    def body(rv, ev, pv):
        xa, delta, dh, dxn = rv
        _, vjp = jax.vjp(_f_resid_norm_mod, xa, delta, ev[0], pv[0], ev[1], ev[2])
        dxa, ddelta, dg, dw, dsc, dsh = vjp((dxn, dh))
        return [dxa, ddelta], [dg, dsc, dsh], [dw]
    (dx, do), (dg1, dsc2, dsh2), (gr["norm2_w"],) = _rowcall(
        body, name="b_resid_norm_mod2", T=T, S=S, tr=tr,
        rows=[(sv["x"], D_MODEL, 0), (sv["o"], D_MODEL, 0), (dh2, D_MODEL, 0), (dx1, D_MODEL, 0)],
        exs=[g1, sc2, sh2], pars=[w["norm2_w"]],
        row_outs=[(D_MODEL, F32, D_MODEL, 0), (D_MODEL, BF16, D_MODEL, 0)], ex_outs=[D_MODEL] * 3,
        par_outs=[(1, D_MODEL)])
    dy = _matmul(do, w["w_out"], "nt", BF16, "mm_out_dx")
    gr["w_out"] = _matmul(sv["y"], do, "tn", BF16, "mm_out_dw")

    def body(rv, ev, pv):
        _, vjp = jax.vjp(_f_outnorm, rv[0], rv[1], rv[2], pv[0])
        doa, dob, dyc, dgw = vjp(rv[3])
        return [doa, dob, dyc], [], [dgw]
    (doa, dob, dyc), _, (gr["out_norm_w"],) = _rowcall(
        body, name="b_outnorm", T=T, S=S, tr=tr,
        rows=[(sv["oa"], 768, 0), (sv["ob"], 768, 0), (sv["yc"], C_WIDTH, 0), (dy, Y_END, 0)], pars=[w["out_norm_w"]],
        row_outs=[(768, F32, 768, 0), (768, F32, 768, 0), (C_WIDTH, F32, C_WIDTH, 0)], par_outs=[(1, Y_END)])

    ta, tb, band = _attn_tiles(S)
    if own_early:
        host = dict(host)
        host["attn_b_bwd"] = list(host.get("attn_b_bwd", ())) + [
            gr["w_gate_up"], gr["w_down"].reshape(N_CHIPS, FFN_HIDDEN // N_CHIPS, D_MODEL)]
    dQ, dK, dV, _ = _hosted(_attn_bwd, host, got, "a2a", "attn_b_bwd", sv["Q"], sv["K"], sv["V"], sv["ob"], sv["lse_b"],
                            dob, None, B=B, S=S, HQ=B_HEADS, HK=B_HEADS, window=None,
                            scale=(B_NOPE + B_ROPE) ** -0.5, tq=tb, band=band)
    dqa, dka, dva, dsink = _hosted(_attn_bwd, host, got, "a2a", "attn_a_bwd", sv["qa"], sv["ka"], sv["va"], sv["oa"],
                                   sv["lse_a"], doa, w["sinks"], B=B, S=S, HQ=A_Q_HEADS, HK=A_KV_HEADS,
                                   window=A_WINDOW, scale=HEAD_DIM ** -0.5, tq=ta, band=None)
    gr["sinks"] = dsink

    def body(rv, ev, pv):
        dQv, dKv, dVv, cbv, sbv = rv
        dq = [_rope_bwd((cbv, sbv), p)[0] for p in _heads(dQv, B_HEADS)]
        dkr = None
        for p in _heads(dKv, B_HEADS):
            dkr = p if dkr is None else dkr + p
        return [dq, [dKv, dVv], dkr], [], []
    (dq, dkv, dkr), _, _ = _rowcall(
        body, name="b_mlaprep", T=T, S=S, tr=tr,
        rows=[(dQ, 768, 0), (dK, 768, 0), (dV, 768, 0), (cb, LANES, 0), (sb, LANES, 0)],
        row_outs=[(768, BF16, 768, 0), (1536, BF16, 1536, 0), (LANES, F32, LANES, 0)])
    dcqn = _matmul(dq, w["b_w_uq"], "nt", BF16, "mm_uq_dx")
    gr["b_w_uq"] = _matmul(sv["cqn"], dq, "tn", BF16, "mm_uq_dw")
    dckvn = _matmul(dkv, w["b_w_ukv"], "nt", BF16, "mm_ukv_dx")
    gr["b_w_ukv"] = _matmul(sv["ckvn"], dkv, "tn", BF16, "mm_ukv_dw")

    def body(rv, ev, pv):
        proj, cav, sav, cbv, sbv, dqa_, dka_, dva_, dcqn_, dckvn_, dkr_ = rv
        _, vjp = jax.vjp(lambda p, a, b_: _f_mixprep(p, cav, sav, cbv, sbv, a, b_), proj, pv[0], pv[1])
        dproj, dqn, dkvn = vjp((dqa_, dka_, dva_, dcqn_, dckvn_, dkr_))
        return [dproj], [], [dqn, dkvn]
    (dproj,), _, (gr["b_q_norm_w"], gr["b_kv_norm_w"]) = _rowcall(
        body, name="b_mixprep", T=T, S=S, tr=tr,
        rows=[(sv["proj"], P_CU, 0), (ca, LANES, 0), (sa, LANES, 0), (cb, LANES, 0), (sb, LANES, 0),
              (dqa, 768, 0), (dka, 256, 0), (dva, 256, 0), (dcqn, 384, 0), (dckvn, 256, 0), (dkr, LANES, 0)],
        pars=[w["b_q_norm_w"], w["b_kv_norm_w"]], row_outs=[(P_END, BF16, P_CU, 0)],
        par_outs=[(1, B_Q_RANK), (1, B_KV_RANK)])

    ts = _sgu_rows(S)

    def body(rv, ev, pv):
        cu, cv, dycv, _ = rv
        dcus, dcvs, acc = [], [], None
        for r in range(0, ts, C_CHUNK):
            _, vjp = jax.vjp(_f_sgu, cu[r:r + C_CHUNK], cv[r:r + C_CHUNK], pv[0], pv[1], pv[2], pv[3])
            dcu, dcv, *dpar = vjp(dycv[r:r + C_CHUNK])
            dcus.append(dcu)
            dcvs.append(dcv)
            acc = dpar if acc is None else [a + b_ for a, b_ in zip(acc, dpar)]
        return [[jnp.concatenate(dcus, axis=0), jnp.concatenate(dcvs, axis=0)]], [], acc
    (dproj,), _, (gr["c_ln_w"], gr["c_ln_b"], gr["c_w_s"], gr["c_b_col"]) = _rowcall(
        body, name="b_sgu", T=T, S=S, tr=ts,
        rows=[(sv["proj"], C_WIDTH, P_CU // C_WIDTH), (sv["proj"], C_WIDTH, P_CV // C_WIDTH), (dyc, C_WIDTH, 0),
              (dproj, 2 * C_WIDTH, P_CU // (2 * C_WIDTH))],
        pars=[w["c_ln_w"], w["c_ln_b"], w["c_w_s"], w["c_b_col"]],
        row_outs=[(P_END, BF16, 2 * C_WIDTH, P_CU // (2 * C_WIDTH))],
        par_outs=[(1, C_WIDTH), (1, C_WIDTH), (C_GROUPS, C_CHUNK, C_CHUNK), (C_GROUPS, C_CHUNK, 1)],
        aliases={3: 0})
    dh = _mm(host, got, "a2a", dproj, w["w_in"], "nt", BF16, "mm_in_dx")
    gr["w_in"] = _matmul(sv["h"], dproj, "tn", BF16, "mm_in_dw")

    if prev is None:
        def body(rv, ev, pv):
            xv, dhv, dxd = rv
            _, vjp = jax.vjp(_f_norm_mod, xv, pv[0], ev[0], ev[1])
            dxa, dw, dsc, dsh = vjp(dhv)
            return [dxa + dxd], [dsc, dsh], [dw]
        (dxin,), (dsc1, dsh1), (gr["norm1_w"],) = _rowcall(
            body, name="b_norm_mod", T=T, S=S, tr=tr, rows=[(sv["x"], D_MODEL, 0), (dh, D_MODEL, 0), (dx, D_MODEL, 0)],
            exs=[sc1, sh1], pars=[w["norm1_w"]], row_outs=[(D_MODEL, F32, D_MODEL, 0)], ex_outs=[D_MODEL] * 2,
            par_outs=[(1, D_MODEL)])
        nxt = (dxin, None, None)
    else:
        x1p, dp, g2p = prev

        def body(rv, ev, pv):
            xa, delta, dhv, dxn = rv
            _, vjp = jax.vjp(_f_resid_norm_mod, xa, delta, ev[0], pv[0], ev[1], ev[2])
            dxa, ddelta, dg, dw, dsc, dsh = vjp((dxn, dhv))
            return [dxa, ddelta], [dg, dsc, dsh], [dw]
        (dx1p, ddp), (dg2p, dsc1, dsh1), (gr["norm1_w"],) = _rowcall(
            body, name="b_resid_norm_mod1", T=T, S=S, tr=tr,
            rows=[(x1p, D_MODEL, 0), (dp, D_MODEL, 0), (dh, D_MODEL, 0), (dx, D_MODEL, 0)],
            exs=[g2p, sc1, sh1], pars=[w["norm1_w"]],
            row_outs=[(D_MODEL, F32, D_MODEL, 0), (D_MODEL, BF16, D_MODEL, 0)], ex_outs=[D_MODEL] * 3,
            par_outs=[(1, D_MODEL)])
        nxt = (dx1p, ddp, dg2p)
    return gr, (dsh1, dsc1, dg1, dsh2, dsc2), nxt, got


def _lane_table(lanes_neg, lanes_pos, inv):
    freq = np.zeros((LANES,), np.int64) - 1
    sign = np.zeros((1, LANES), np.float32)
    n = len(lanes_neg)
    freq[lanes_neg] = np.arange(n)
    freq[lanes_pos] = np.arange(n)
    sign[0, lanes_neg] = -1.0
    sign[0, lanes_pos] = 1.0
    return _select_axis(inv, freq, 0).reshape(1, LANES), jnp.asarray(sign)


SHARDED = ("ada_w", "w_in", "b_w_uq", "b_w_ukv", "w_out", "w_gate_up", "w_down")
ROW_SHARDED = ("w_out", "w_down")
SMALL = ("ada_b", "norm1_w", "a_sinks", "b_q_norm_w", "b_kv_norm_w", "c_ln_w", "c_ln_b", "c_w_s", "c_b_s",
         "out_norm_w", "norm2_w", "final_norm_w")
FWD_HOST = {"mm_in": ("w_in", "w_out", "b_w_uq", "b_w_ukv"), "attn_b_fwd": ("ada_w", "w_gate_up", "w_down")}
BWD_HOST = {"attn_a_bwd": ("w_in", "w_out", "b_w_uq", "b_w_ukv"), "attn_b_bwd": ("ada_w", "w_gate_up", "w_down")}
EXPOSED = ("ada_w", "w_in", "b_w_uq", "b_w_ukv", "w_out")
OWN_LAYER = ("w_gate_up", "w_down")


def _from_host(table, got):
    return {k: got[name][i] for name, ks in table.items() for i, k in enumerate(ks)}


def _to_host(table, arrays):
    return {name: [arrays[k] for k in ks] for name, ks in table.items()}


def _join_cols(g):
    return jnp.concatenate([g[j] for j in range(N_CHIPS)], axis=1)


def _split_cols(g):
    n = g.shape[1] // N_CHIPS
    return jnp.stack([g[:, j * n:(j + 1) * n] for j in range(N_CHIPS)])


def _layer_weights(G, small, l):
    D = D_MODEL
    w = {} if "w_gate_up" not in G else _late_weights(G["w_gate_up"], G["w_down"])
    return {
        **w,
        "norm1_w": small["norm1_w"][l].reshape(1, D),
        "w_in": _pad_axis(_join_cols(G["w_in"]), _map_w_in(), 1),
        "sinks": jnp.broadcast_to(small["a_sinks"][l].reshape(A_Q_HEADS, 1, 1), (A_Q_HEADS, 1, LANES)),
        "b_q_norm_w": small["b_q_norm_w"][l].reshape(1, B_Q_RANK),
        "b_w_uq": _pad_axis(_join_cols(G["b_w_uq"]), _map_w_uq(), 1),
        "b_kv_norm_w": small["b_kv_norm_w"][l].reshape(1, B_KV_RANK),
        "b_w_ukv": _pad_axis(_join_cols(G["b_w_ukv"]), _map_w_ukv(), 1),
        "c_ln_w": small["c_ln_w"][l].reshape(1, C_WIDTH), "c_ln_b": small["c_ln_b"][l].reshape(1, C_WIDTH),
        "c_w_s": small["c_w_s"][l], "c_b_col": small["c_b_s"][l].reshape(C_GROUPS, C_CHUNK, 1),
        "out_norm_w": _pad_axis(small["out_norm_w"][l].reshape(1, D), _map_w_out(), 1),
        "w_out": _pad_axis(G["w_out"].reshape(D, D), _map_w_out(), 0),
        "norm2_w": small["norm2_w"][l].reshape(1, D),
    }


def _late_weights(gate_up, down):
    return {"w_gate_up": gate_up, "w_down": down.reshape(FFN_HIDDEN, D_MODEL)}


def _send_buffers(gr, ada_gw):
    D = D_MODEL
    return {
        "ada_w": ada_gw, "w_gate_up": gr["w_gate_up"], "w_down": gr["w_down"].reshape(N_CHIPS, FFN_HIDDEN // N_CHIPS, D),
        "w_out": _unpad_axis(gr["w_out"], _map_w_out(), D, 0).reshape(N_CHIPS, D // N_CHIPS, D),
        "w_in": _split_cols(_unpad_axis(gr["w_in"], _map_w_in(), IN_COLS, 1)),
        "b_w_uq": _split_cols(_unpad_axis(gr["b_w_uq"], _map_w_uq(), B_HEADS * (B_NOPE + B_ROPE), 1)),
        "b_w_ukv": _split_cols(_unpad_axis(gr["b_w_ukv"], _map_w_ukv(), B_HEADS * (B_NOPE + B_V), 1)),
    }


def _small_grads(gr, ada_gb, B):
    D = D_MODEL
    return {
        "ada_b": ada_gb.reshape(N_MOD * D), "norm1_w": gr["norm1_w"].reshape(D),
        "a_sinks": gr["sinks"][:, 0, 0].reshape(B, A_Q_HEADS).sum(axis=0),
        "b_q_norm_w": gr["b_q_norm_w"].reshape(B_Q_RANK), "b_kv_norm_w": gr["b_kv_norm_w"].reshape(B_KV_RANK),
        "c_ln_w": gr["c_ln_w"].reshape(C_WIDTH), "c_ln_b": gr["c_ln_b"].reshape(C_WIDTH), "c_w_s": gr["c_w_s"],
        "c_b_s": gr["c_b_col"].reshape(C_GROUPS, C_CHUNK),
        "out_norm_w": _unpad_axis(gr["out_norm_w"], _map_w_out(), D, 1).reshape(D), "norm2_w": gr["norm2_w"].reshape(D),
    }


def _step(x, c, positions, target, small, shard_of, gathered=None):
    dist = gathered is None
    B, S, D = x.shape
    T = B * S
    xt = x.reshape(T, D)
    tgt = target.reshape(T, D)
    pos_col = positions.astype(F32).reshape(T, 1)
    inv_a = 1.0 / (ROPE_THETA ** (jnp.arange(0, HEAD_DIM, 2, dtype=F32) / HEAD_DIM))
    inv_b = 1.0 / (ROPE_THETA ** (jnp.arange(0, B_ROPE, 2, dtype=F32) / B_ROPE))
    fa, sga = _lane_table(np.arange(32), 64 + np.arange(32), inv_a)
    fb, sgb = _lane_table(48 + np.arange(16), 112 + np.arange(16), inv_b)
    ca, sa = _rope_tables(pos_col, fa, sga, "rope_a")
    cb, sb = _rope_tables(pos_col, fb, sgb, "rope_b")
    tabs = (ca, sa, cb, sb)
    c8 = jnp.zeros((8, D), F32).at[:B].set(c)

    if dist:
        first = shard_of(0)
        G = dict(zip(EXPOSED, _xchg_call("gather", [first[k] for k in EXPOSED], "gather_first")))
    else:
        G = gathered[0]
    saved, prevs, modss = [], [], []
    prev = None
    for l in range(DEPTH):
        w = _layer_weights(G, small, l)
        mod = _ada_fwd(c8, G["ada_w"], small["ada_b"][l].reshape(1, N_MOD * D))
        mods = tuple(mod[:B, i * D:(i + 1) * D].reshape(B, 1, D) for i in range(N_MOD))
        more = l + 1 < DEPTH
        host = _to_host(FWD_HOST, shard_of(l + 1)) if dist and more else {}
        late = None
        if dist and l == 0:
            host["attn_a_fwd"] = [first[k] for k in OWN_LAYER]
            late = lambda got: _late_weights(*got["attn_a_fwd"])
        prevs.append(prev)
        modss.append(mods)
        prev, sv, got = _layer_fwd(xt, prev, mods, w, tabs, B, S, host, late)
        saved.append(sv)
        if more:
            G = _from_host(FWD_HOST, got) if dist else gathered[l + 1]
    x1, d, g2 = prev
    loss, dx1, dd, dg2, dfw = _final(x1, d, g2, small["final_norm_w"].reshape(1, D), tgt, B, S)

    landed = [None] * DEPTH
    smalls = [None] * DEPTH
    pending = None
    for l in reversed(range(DEPTH)):
        host = _to_host(BWD_HOST, pending) if dist and pending is not None else None
        early = dist and l == 0 and host is not None
        gr, (dsh1, dsc1, dg1, dsh2, dsc2), nxt, got = _layer_bwd(saved[l], prevs[l], modss[l], tabs, dx1, dd, B, S, host, early)
        if pending is not None:
            landed[l + 1] = _from_host(BWD_HOST, got) if dist else pending
        dmod = jnp.concatenate([dsh1, dsc1, dg1, dsh2, dsc2, dg2], axis=-1).reshape(B, N_MOD * D)
        ada_gw, ada_gb = _ada_bwd(c8, jnp.zeros((8, N_MOD * D), F32).at[:B].set(dmod))
        pending = _send_buffers(gr, ada_gw)
        smalls[l] = _small_grads(gr, ada_gb, B)
        dx1, dd, dg2 = nxt
    if dist:
        last = EXPOSED if early else SHARDED
        landed[0] = dict(zip(last, _xchg_call("a2a", [pending[k] for k in last], "grad_exchange_last")))
        if early:
            landed[0].update(zip(OWN_LAYER, got["attn_b_bwd"][-len(OWN_LAYER):]))
    else:
        landed[0] = pending
    small_g = {k: jnp.stack([smalls[l][k] for l in range(DEPTH)]) for k in SMALL if k != "final_norm_w"}
    small_g["final_norm_w"] = dfw.reshape(D)
    return loss[0, 0], dx1.reshape(B, S, D), landed, small_g


def _pack(arrs, cols, mult):
    flat = jnp.concatenate([a.reshape(-1) for a in arrs])
    n = flat.shape[0]
    rows = -(-n // cols)
    rows = -(-rows // mult) * mult
    return jnp.pad(flat, (0, rows * cols - n)).reshape(rows, cols)


def _unpack(blob, shapes):
    flat = blob.reshape(-1)
    out, off = [], 0
    for s in shapes:
        n = int(np.prod(s))
        out.append(flat[off:off + n].reshape(s))
        off += n
    return out


def kernel(x, c, positions, ada_w, ada_b, norm1_w, w_in, a_sinks, b_q_norm_w, b_w_uq, b_kv_norm_w, b_w_ukv, c_ln_w, c_ln_b, c_w_s, c_b_s, out_norm_w, w_out, norm2_w, w_gate_up, w_down, final_norm_w, loss_target, m_ada_w, m_ada_b, m_norm1_w, m_w_in, m_a_sinks, m_b_q_norm_w, m_b_w_uq, m_b_kv_norm_w, m_b_w_ukv, m_c_ln_w, m_c_ln_b, m_c_w_s, m_c_b_s, m_out_norm_w, m_w_out, m_norm2_w, m_w_gate_up, m_w_down, m_final_norm_w, v_ada_w, v_ada_b, v_norm1_w, v_w_in, v_a_sinks, v_b_q_norm_w, v_b_w_uq, v_b_kv_norm_w, v_b_w_ukv, v_c_ln_w, v_c_ln_b, v_c_w_s, v_c_b_s, v_out_norm_w, v_w_out, v_norm2_w, v_w_gate_up, v_w_down, v_final_norm_w):
    names = ("ada_w", "ada_b", "norm1_w", "w_in", "a_sinks", "b_q_norm_w", "b_w_uq", "b_kv_norm_w", "b_w_ukv", "c_ln_w",
             "c_ln_b", "c_w_s", "c_b_s", "out_norm_w", "w_out", "norm2_w", "w_gate_up", "w_down", "final_norm_w")
    ws = dict(zip(names, (ada_w, ada_b, norm1_w, w_in, a_sinks, b_q_norm_w, b_w_uq, b_kv_norm_w, b_w_ukv, c_ln_w, c_ln_b,
                          c_w_s, c_b_s, out_norm_w, w_out, norm2_w, w_gate_up, w_down, final_norm_w)))
    ms = dict(zip(names, (m_ada_w, m_ada_b, m_norm1_w, m_w_in, m_a_sinks, m_b_q_norm_w, m_b_w_uq, m_b_kv_norm_w, m_b_w_ukv,
                          m_c_ln_w, m_c_ln_b, m_c_w_s, m_c_b_s, m_out_norm_w, m_w_out, m_norm2_w, m_w_gate_up, m_w_down,
                          m_final_norm_w)))
    vs = dict(zip(names, (v_ada_w, v_ada_b, v_norm1_w, v_w_in, v_a_sinks, v_b_q_norm_w, v_b_w_uq, v_b_kv_norm_w, v_b_w_ukv,
                          v_c_ln_w, v_c_ln_b, v_c_w_s, v_c_b_s, v_out_norm_w, v_w_out, v_norm2_w, v_w_gate_up, v_w_down,
                          v_final_norm_w)))
    shards = {k: ws[k].astype(BF16) for k in SHARDED}
    loss_local, grad_x, landed, gsmall = _step(x, c, positions, loss_target, {k: ws[k] for k in SMALL},
                                               lambda l: {k: shards[k][l] for k in SHARDED})

    mine = {k: jnp.stack([_sum_slots(landed[l][k], "grad_sum_" + k) for l in range(DEPTH)]) for k in SHARDED}
    theirs = dict(zip(SHARDED, _xchg_call("swap", [mine[k] for k in SHARDED], "grad_sibling_swap")))
    grads, delta, new_m, new_v = {}, {}, {}, {}
    for k in SHARDED:
        shp = ws[k].shape
        two = (shp[0] * shp[1], shp[2])
        g, dlt, nm, nv = _adamw(ws[k].reshape(two), mine[k].reshape(two), theirs[k].reshape(two), ms[k].reshape(two),
                                vs[k].reshape(two), "adamw_" + k)
        grads[k], delta[k], new_m[k], new_v[k] = g.reshape(shp), dlt.reshape(shp), nm.reshape(shp), nv.reshape(shp)

    small_shapes = [ws[k].shape for k in SMALL]
    sblob = _pack([gsmall[k] for k in SMALL] + [loss_local.reshape(1)], LANES, 8)
    svals = _unpack(_all_reduce_small(sblob, "small_all_reduce"), small_shapes + [(1,)])
    loss = svals[-1].reshape(())
    pw = _pack([ws[k] for k in SMALL], LANES, 8)
    pg = _pack(svals[:-1], LANES, 8)
    pm = _pack([ms[k] for k in SMALL], LANES, 8)
    pv = _pack([vs[k] for k in SMALL], LANES, 8)
    g, dlt, nm, nv = _adamw(pw, pg, None, pm, pv, "adamw_small")
    for k, a, b_, c_, d_ in zip(SMALL, _unpack(g, small_shapes), _unpack(dlt, small_shapes), _unpack(nm, small_shapes),
                                _unpack(nv, small_shapes)):
        grads[k], delta[k], new_m[k], new_v[k] = a, b_, c_, d_

    return (loss, grad_x, *[grads[k] for k in names], *[delta[k] for k in names], *[new_m[k] for k in names],
            *[new_v[k] for k in names])
```

```python
import functools
import math

import numpy as np
import jax
import jax.numpy as jnp
from jax import lax
from jax.experimental import pallas as pl
from jax.experimental.pallas import tpu as pltpu

F32 = jnp.float32
BF16 = jnp.bfloat16

D_MODEL = 1024
DEPTH = 4
HEAD_DIM = 64
ROPE_THETA = 10000.0
NORM_EPS = 1e-6
NEG_INF = -1e30
A_Q_HEADS = 6
A_KV_HEADS = 2
A_WINDOW = 128
B_HEADS = 6
B_Q_RANK = 384
B_KV_RANK = 256
B_NOPE = 64
B_ROPE = 32
B_V = 64
C_GROUPS = 4
C_GROUP_DIM = 64
C_WIDTH = 256
C_CHUNK = 128
IN_COLS = 1824
FFN_HIDDEN = 2816
N_MOD = 6
ADAM_LR = 0.001
ADAM_B1 = 0.9
ADAM_B2 = 0.999
ADAM_EPS = 1e-08
ADAM_WD = 0.01
ADAM_STEP = 10

LANES = 128
VMEM_LIMIT = 56 * 1024 * 1024
N_CHIPS = 4
WINDOW_SUB = 256

P_AQ, P_AK, P_AV, P_CQ, P_CKV, P_KR, P_CU, P_CV, P_END = 0, 768, 1024, 1280, 1664, 1920, 2048, 2304, 2560
Y_A, Y_B, Y_C, Y_END = 0, 768, 1536, 1792


def _map_w_in():
    idx = -np.ones(P_END, np.int64)
    half = HEAD_DIM // 2
    for h in range(A_Q_HEADS):
        idx[P_AQ + h * LANES + np.arange(half)] = h * HEAD_DIM + np.arange(half)
        idx[P_AQ + h * LANES + 64 + np.arange(half)] = h * HEAD_DIM + half + np.arange(half)
    for h in range(A_KV_HEADS):
        idx[P_AK + h * LANES + np.arange(half)] = 384 + h * HEAD_DIM + np.arange(half)
        idx[P_AK + h * LANES + 64 + np.arange(half)] = 384 + h * HEAD_DIM + half + np.arange(half)
        idx[P_AV + h * LANES + np.arange(HEAD_DIM)] = 512 + h * HEAD_DIM + np.arange(HEAD_DIM)
    idx[P_CQ:P_CQ + 384] = 640 + np.arange(384)
    idx[P_CKV:P_CKV + 256] = 1024 + np.arange(256)
    idx[P_KR + 48 + np.arange(16)] = 1280 + np.arange(16)
    idx[P_KR + 112 + np.arange(16)] = 1296 + np.arange(16)
    idx[P_CU:P_CU + 256] = 1312 + np.arange(256)
    idx[P_CV:P_CV + 256] = 1568 + np.arange(256)
    return idx


def _map_w_uq():
    idx = -np.ones(B_HEADS * LANES, np.int64)
    for h in range(B_HEADS):
        b = h * (B_NOPE + B_ROPE)
        idx[h * LANES + np.arange(48)] = b + np.arange(48)
        idx[h * LANES + 48 + np.arange(16)] = b + 64 + np.arange(16)
        idx[h * LANES + 64 + np.arange(16)] = b + 48 + np.arange(16)
        idx[h * LANES + 112 + np.arange(16)] = b + 80 + np.arange(16)
    return idx


def _map_w_ukv():
    idx = -np.ones(2 * B_HEADS * LANES, np.int64)
    for h in range(B_HEADS):
        b = h * (B_NOPE + B_V)
        idx[h * LANES + np.arange(48)] = b + np.arange(48)
        idx[h * LANES + 64 + np.arange(16)] = b + 48 + np.arange(16)
        idx[B_HEADS * LANES + h * LANES + np.arange(B_V)] = b + B_NOPE + np.arange(B_V)
    return idx


def _map_w_out():
    idx = -np.ones(Y_END, np.int64)
    for h in range(A_Q_HEADS):
        idx[Y_A + h * LANES + np.arange(64)] = h * 64 + np.arange(64)
    for h in range(B_HEADS):
        idx[Y_B + h * LANES + np.arange(64)] = 384 + h * 64 + np.arange(64)
    idx[Y_C:Y_C + 256] = 768 + np.arange(256)
    return idx


def _inverse(idx, n):
    inv = np.zeros(n, np.int64)
    pos = np.nonzero(idx >= 0)[0]
    inv[idx[pos]] = pos
    return inv


def _runs(idx):
    runs, i, n = [], 0, len(idx)
    while i < n:
        j = i + 1
        while j < n and ((idx[i] < 0 and idx[j] < 0) or (idx[i] >= 0 and idx[j] == idx[i] + (j - i))):
            j += 1
        runs.append((int(idx[i]), j - i))
        i = j
    return runs


def _select_axis(w, idx, axis):
    pieces = []
    for start, length in _runs(idx):
        if start < 0:
            shape = list(w.shape)
            shape[axis] = length
            pieces.append(jnp.zeros(shape, w.dtype))
        else:
            pieces.append(lax.slice_in_dim(w, start, start + length, axis=axis))
    return jnp.concatenate(pieces, axis=axis)


def _pad_axis(w, idx, axis):
    return _select_axis(w, idx, axis)


def _unpad_axis(g, idx, n, axis):
    return _select_axis(g, _inverse(idx, n), axis)


def _params(sem):
    return pltpu.CompilerParams(dimension_semantics=sem, vmem_limit_bytes=VMEM_LIMIT)


def _tile(dim, target):
    if dim <= target:
        return dim
    best = None
    for t in range(LANES, target + 1, LANES):
        if dim % t == 0:
            best = t
    assert best is not None, dim
    return best


def _row_div(rows, target):
    if rows <= target:
        return rows
    best = None
    for t in range(8, target + 1, 8):
        if rows % t == 0:
            best = t
    assert best is not None, rows
    return best


_ANY = pl.BlockSpec(memory_space=pl.ANY)
_MESH = pl.DeviceIdType.MESH


def _xchg_out_shapes(kind, srcs):
    if kind == "gather":
        return [jax.ShapeDtypeStruct((N_CHIPS,) + s.shape, s.dtype) for s in srcs]
    return [jax.ShapeDtypeStruct(s.shape, s.dtype) for s in srcs]


def _xchg_scratch(kind, n):
    per = 1 if kind == "swap" else N_CHIPS - 1
    return [pltpu.SemaphoreType.DMA((per * n,)), pltpu.SemaphoreType.DMA((per * n,)), pltpu.SemaphoreType.DMA((n,))]


def _xchg_copies(kind, srcs, dsts, send_sems, recv_sems, local_sems, arrivals):
    x, y, c = lax.axis_index("x"), lax.axis_index("y"), lax.axis_index("c")
    me = 2 * x + y
    peers = [(1 - x, y), (x, 1 - y), (1 - x, 1 - y)]
    local, out, back = [], [], []
    for i, (s, d) in enumerate(zip(srcs, dsts)):
        if kind == "swap":
            cp = pltpu.make_async_remote_copy(src_ref=s, dst_ref=d, send_sem=send_sems.at[i], recv_sem=recv_sems.at[i],
                                              device_id=(x, y, 1 - c), device_id_type=_MESH)
            out.append(cp)
            back.append(cp)
            continue
        local.append(pltpu.make_async_copy(s if kind == "gather" else s.at[me], d.at[me], local_sems.at[i]))
        for kk, (px, py) in enumerate(peers):
            j = (N_CHIPS - 1) * i + kk
            theirs = 2 * px + py
            out.append(pltpu.make_async_remote_copy(
                src_ref=s if kind == "gather" else s.at[theirs], dst_ref=d.at[me], send_sem=send_sems.at[j],
                recv_sem=recv_sems.at[j], device_id=(px, py, c), device_id_type=_MESH))
            if arrivals:
                back.append(pltpu.make_async_remote_copy(
                    src_ref=s if kind == "gather" else s.at[me], dst_ref=d.at[theirs], send_sem=send_sems.at[j],
                    recv_sem=recv_sems.at[j], device_id=(px, py, c), device_id_type=_MESH))
    return local, out, back


def _xchg_start(kind, srcs, dsts, sems):
    local, out, _ = _xchg_copies(kind, srcs, dsts, *sems, arrivals=False)
    for cp in local + out:
        cp.start()


def _xchg_wait(kind, srcs, dsts, sems):
    local, out, back = _xchg_copies(kind, srcs, dsts, *sems, arrivals=True)
    for cp in back:
        cp.wait_recv()
    for cp in out:
        cp.wait_send()
    for cp in local:
        cp.wait()


def _xchg_at_ends(kind, srcs, dsts, sems, grid, first):
    ids = [pl.program_id(a) for a in range(len(grid))]
    cond = None
    for i, n in zip(ids, grid):
        c = (i == 0) if first else (i == n - 1)
        cond = c if cond is None else jnp.logical_and(cond, c)

    @pl.when(cond)
    def _():
        (_xchg_start if first else _xchg_wait)(kind, srcs, dsts, sems)


def _xchg_call(kind, srcs, name):
    n = len(srcs)

    def kern(*refs):
        s, d, sems = refs[:n], refs[n:2 * n], refs[2 * n:]
        _xchg_start(kind, s, d, sems)
        _xchg_wait(kind, s, d, sems)

    return pl.pallas_call(
        kern, name=name, in_specs=[_ANY] * n, out_specs=[_ANY] * n, out_shape=_xchg_out_shapes(kind, srcs),
        scratch_shapes=_xchg_scratch(kind, n),
    )(*srcs)


_DIMS = {"nn": (((1,), (0,)), ((), ())), "nt": (((1,), (1,)), ((), ())), "tn": (((0,), (0,)), ((), ()))}


def _matmul(a, b, mode, out_dtype, name, *, b_chunks=False, out_chunks=False, xchg=None):
    if b_chunks:
        nchunk, brows, bcols = b.shape
        bshape = (brows, nchunk * bcols)
    else:
        bshape = b.shape
    if mode == "nn":
        (m, k), (_, n) = a.shape, bshape
    elif mode == "nt":
        (m, k), (n, _) = a.shape, bshape
    else:
        (k, m), (_, n) = a.shape, bshape
    tm, tk = (1408, 1024) if mode == "tn" else (1024, 1408)
    tm, tn, tk = _tile(m, tm), _tile(n, 1408), _tile(k, tk)
    if b_chunks:
        if mode == "nn":
            tn = bcols
        else:
            assert mode == "nt"
            tk = bcols
    if out_chunks:
        assert n % N_CHIPS == 0
        tn = n // N_CHIPS
    ni, nj, nk = m // tm, n // tn, k // tk
    dims = _DIMS[mode]
    n_x = 0 if xchg is None else len(xchg[1])

    def kern(*refs):
        a_ref, b_ref = refs[0], refs[1]
        xs = refs[2:2 + n_x]
        o_ref = refs[2 + n_x]
        xd = refs[3 + n_x:3 + 2 * n_x]
        acc_ref = refs[3 + 2 * n_x]
        sems = refs[4 + 2 * n_x:]
        kk = pl.program_id(2)
        if n_x:
            _xchg_at_ends(xchg[0], xs, xd, sems, (ni, nj, nk), True)

        @pl.when(kk == 0)
        def _():
            acc_ref[...] = jnp.zeros_like(acc_ref)

        acc_ref[...] += lax.dot_general(a_ref[...], b_ref[...], dims, preferred_element_type=F32)

        @pl.when(kk == nk - 1)
        def _():
            o_ref[...] = acc_ref[...].astype(o_ref.dtype)

        if n_x:
            _xchg_at_ends(xchg[0], xs, xd, sems, (ni, nj, nk), False)

    if mode == "tn":
        a_spec = pl.BlockSpec((tk, tm), lambda i, j, kk: (kk, i))
    else:
        a_spec = pl.BlockSpec((tm, tk), lambda i, j, kk: (i, kk))
    if b_chunks and mode == "nn":
        b_spec = pl.BlockSpec((None, tk, tn), lambda i, j, kk: (j, kk, 0))
    elif b_chunks:
        b_spec = pl.BlockSpec((None, tn, tk), lambda i, j, kk: (kk, j, 0))
    elif mode == "nt":
        b_spec = pl.BlockSpec((tn, tk), lambda i, j, kk: (j, kk))
    else:
        b_spec = pl.BlockSpec((tk, tn), lambda i, j, kk: (kk, j))
    if out_chunks:
        o_spec = pl.BlockSpec((None, tm, tn), lambda i, j, kk: (j, i, 0))
        o_shape = jax.ShapeDtypeStruct((N_CHIPS, m, tn), out_dtype)
    else:
        o_spec = pl.BlockSpec((tm, tn), lambda i, j, kk: (i, j))
        o_shape = jax.ShapeDtypeStruct((m, n), out_dtype)
    xs = [] if xchg is None else list(xchg[1])
    res = pl.pallas_call(
        kern, name=name, grid=(ni, nj, nk),
        in_specs=[a_spec, b_spec] + [_ANY] * n_x, out_specs=[o_spec] + [_ANY] * n_x,
        out_shape=[o_shape] + (_xchg_out_shapes(xchg[0], xs) if n_x else []),
        scratch_shapes=[pltpu.VMEM((tm, tn), F32)] + (_xchg_scratch(xchg[0], n_x) if n_x else []),
        compiler_params=_params(("arbitrary", "arbitrary", "arbitrary") if n_x else ("parallel", "parallel", "arbitrary")),
    )(a, b, *xs)
    return (res[0], list(res[1:])) if n_x else res[0]


def _rowcall(body, *, name, T, S, tr, rows, exs=(), pars=(), row_outs=(), ex_outs=(), par_outs=(), aliases=None):
    assert S % tr == 0 and T % S == 0
    per_ex = S // tr
    nb = T // S
    n_rows, n_exs, n_pars = len(rows), len(exs), len(pars)
    n_ro, n_eo, n_po = len(row_outs), len(ex_outs), len(par_outs)

    def kern(*refs):
        ins = refs[:n_rows + n_exs + n_pars]
        outs = refs[n_rows + n_exs + n_pars:]
        rv = [r[...].astype(F32) for r in ins[:n_rows]]
        ev = [r[0] for r in ins[n_rows:n_rows + n_exs]]
        pv = [r[...] for r in ins[n_rows + n_exs:]]
        ro, eo, po = body(rv, ev, pv)
        i = pl.program_id(0)
        for ref, val in zip(outs[:n_ro], ro):
            if isinstance(val, (list, tuple)):
                off = 0
                for piece in val:
                    w = piece.shape[-1]
                    ref[:, off:off + w] = piece.astype(ref.dtype)
                    off += w
            else:
                ref[...] = val.astype(ref.dtype)
        first_of_ex = (i % per_ex) == 0
        for ref, val in zip(outs[n_ro:n_ro + n_eo], eo):
            @pl.when(first_of_ex)
            def _(ref=ref, val=val):
                ref[0] = val

            @pl.when(jnp.logical_not(first_of_ex))
            def _(ref=ref, val=val):
                ref[0] += val
        for ref, val in zip(outs[n_ro + n_eo:], po):
            @pl.when(i == 0)
            def _(ref=ref, val=val):
                ref[...] = val

            @pl.when(i != 0)
            def _(ref=ref, val=val):
                ref[...] += val

    in_specs = [pl.BlockSpec((tr, w), functools.partial(lambda i, cb: (i, cb), cb=cb)) for (_, w, cb) in rows]
    in_specs += [pl.BlockSpec((1, 1, e.shape[-1]), lambda i: (i // per_ex, 0, 0)) for e in exs]
    in_specs += [pl.BlockSpec(p.shape, functools.partial(lambda i, nd: (0,) * nd, nd=p.ndim)) for p in pars]
    out_specs = [pl.BlockSpec((tr, w), functools.partial(lambda i, cb: (i, cb), cb=cb)) for (_, _, w, cb) in row_outs]
    out_specs += [pl.BlockSpec((1, 1, f), lambda i: (i // per_ex, 0, 0)) for f in ex_outs]
    out_specs += [pl.BlockSpec(tuple(s), functools.partial(lambda i, nd: (0,) * nd, nd=len(s))) for s in par_outs]
    out_shape = [jax.ShapeDtypeStruct((T, tw), dt) for (tw, dt, _, _) in row_outs]
    out_shape += [jax.ShapeDtypeStruct((nb, 1, f), F32) for f in ex_outs]
    out_shape += [jax.ShapeDtypeStruct(tuple(s), F32) for s in par_outs]
    res = pl.pallas_call(
        kern, name=name, grid=(T // tr,), in_specs=in_specs, out_specs=out_specs, out_shape=out_shape,
        input_output_aliases=aliases or {}, compiler_params=_params(("arbitrary",)),
    )(*[r[0] for r in rows], *exs, *pars)
    return res[:n_ro], res[n_ro:n_ro + n_eo], res[n_ro + n_eo:]


def _rms(x, w, n=None):
    n = x.shape[-1] if n is None else n
    ms = jnp.sum(x * x, axis=-1, keepdims=True) * (1.0 / n)
    return x * lax.rsqrt(ms + NORM_EPS) * w


def _gelu(x):
    return 0.5 * x * (1.0 + lax.erf(x * np.float32(1.0 / math.sqrt(2.0))))


def _silu(x):
    return x * jax.nn.sigmoid(x)


@jax.custom_vjp
def _rope(x, cos, sin):
    return x * cos + pltpu.roll(x, 64, 1) * sin


def _rope_fwd(x, cos, sin):
    return _rope(x, cos, sin), (cos, sin)


def _rope_bwd(res, dy):
    cos, sin = res
    return dy * cos + pltpu.roll(dy * sin, 64, 1), None, None


_rope.defvjp(_rope_fwd, _rope_bwd)


def _heads(x, n):
    return [x[:, h * LANES:(h + 1) * LANES] for h in range(n)]


def _f_norm_mod(x, w, sc, sh):
    return _rms(x, w) * (1.0 + sc) + sh


def _f_resid_norm_mod(xa, delta, g, w, sc, sh):
    xn = xa + g * delta
    return xn, _f_norm_mod(xn, w, sc, sh)


def _f_mixprep(proj, ca, sa, cb, sb, qnw, kvnw):
    qa = [_rope(p, ca, sa) for p in _heads(proj[:, P_AQ:P_AK], A_Q_HEADS)]
    ka = [_rope(p, ca, sa) for p in _heads(proj[:, P_AK:P_AV], A_KV_HEADS)]
    va = proj[:, P_AV:P_CQ]
    cqn = _rms(proj[:, P_CQ:P_CKV], qnw)
    ckvn = _rms(proj[:, P_CKV:P_KR], kvnw)
    kr = _rope(proj[:, P_KR:P_CU], cb, sb)
    return jnp.concatenate(qa, -1), jnp.concatenate(ka, -1), va, cqn, ckvn, kr


def _f_mlaprep(q, kv, kr, cb, sb):
    qs = [_rope(p, cb, sb) for p in _heads(q, B_HEADS)]
    ks = [p + kr for p in _heads(kv[:, :B_HEADS * LANES], B_HEADS)]
    return jnp.concatenate(qs, -1), jnp.concatenate(ks, -1), kv[:, B_HEADS * LANES:]


def _f_sgu(cu, cv, ln_w, ln_b, w_s, b_col):
    u = _gelu(cu)
    v = _gelu(cv)
    mu = jnp.mean(v, axis=-1, keepdims=True)
    var = jnp.mean(jnp.square(v - mu), axis=-1, keepdims=True)
    vn = (v - mu) * lax.rsqrt(var + NORM_EPS) * ln_w + ln_b
    r = lax.broadcasted_iota(jnp.int32, (C_CHUNK, C_CHUNK), 0)
    c = lax.broadcasted_iota(jnp.int32, (C_CHUNK, C_CHUNK), 1)
    lane = lax.broadcasted_iota(jnp.int32, (1, C_WIDTH), 1)
    mixed = jnp.zeros(cu.shape, F32)
    for g in range(C_GROUPS):
        gm = (lane // C_GROUP_DIM == g).astype(F32)
        wg = jnp.where(r >= c, w_s[g], 0.0).astype(BF16)
        mixed = mixed + jnp.dot(wg, (vn * gm).astype(BF16), preferred_element_type=F32) + b_col[g] * gm
    return u * mixed


def _f_outnorm(oa, ob, yc, gw):
    ya = _rms(oa, gw[:, Y_A:Y_B], A_Q_HEADS * HEAD_DIM)
    yb = _rms(ob, gw[:, Y_B:Y_C], B_HEADS * B_V)
    ycn = _rms(yc, gw[:, Y_C:Y_END])
    return jnp.concatenate([ya, yb, ycn], -1)


def _f_swiglu(gate, up):
    return _silu(gate) * up


def _mask(q_start, k_start, tq, tk, window):
    qpos = q_start + lax.broadcasted_iota(jnp.int32, (tq, tk), 0)
    kpos = k_start + lax.broadcasted_iota(jnp.int32, (tq, tk), 1)
    m = kpos <= qpos
    if window is not None:
        m = jnp.logical_and(m, qpos - kpos < window)
    return m


def _tile_fwd(qv, kk, vv, q_start, k_start, n_free, scale, window, m0, l0):
    tq = qv.shape[0]
    W = kk.shape[0]
    parts = []
    if n_free > 0:
        parts.append((lax.dot_general(qv, kk[:n_free], _DIMS["nt"], preferred_element_type=F32) * scale, vv[:n_free]))
    if W > n_free:
        s = lax.dot_general(qv, kk[n_free:], _DIMS["nt"], preferred_element_type=F32) * scale
        s = jnp.where(_mask(q_start, k_start + n_free, tq, W - n_free, window), s, NEG_INF)
        parts.append((s, vv[n_free:]))
    m = m0
    for s, _ in parts:
        mx = jnp.max(s, axis=-1, keepdims=True)
        m = mx if m is None else jnp.maximum(m, mx)
    l = None if l0 is None else l0 * jnp.exp(m0 - m)
    o = None
    for s, vpart in parts:
        p = jnp.exp(s - m)
        ps = jnp.sum(p, axis=-1, keepdims=True)
        l = ps if l is None else l + ps
        po = jnp.dot(p.astype(BF16), vpart, preferred_element_type=F32)
        o = po if o is None else o + po
    return o / l, m + jnp.log(l)


def _tile_bwd(qv, kk, vv, dof, ov, lse, q_start, k_start, n_free, scale, window):
    tq = qv.shape[0]
    W = kk.shape[0]
    dob = dof.astype(BF16)
    delta = jnp.sum(dof * ov, axis=-1, keepdims=True)
    dq = None
    outs = []
    for (a, b, masked) in ((0, n_free, False), (n_free, W, True)):
        if b <= a:
            continue
        kp, vp = kk[a:b], vv[a:b]
        s = lax.dot_general(qv, kp, _DIMS["nt"], preferred_element_type=F32) * scale
        if masked:
            s = jnp.where(_mask(q_start, k_start + a, tq, b - a, window), s, NEG_INF)
        p = jnp.exp(s - lse)
        dp = lax.dot_general(dob, vp, _DIMS["nt"], preferred_element_type=F32)
        ds = (p * (dp - delta) * scale).astype(BF16)
        d = jnp.dot(ds, kp, preferred_element_type=F32)
        dq = d if dq is None else dq + d
        dkp = lax.dot_general(ds, qv, _DIMS["tn"], preferred_element_type=F32)
        dvp = lax.dot_general(p.astype(BF16), dob, _DIMS["tn"], preferred_element_type=F32)
        outs.append((a, dkp, dvp))
    return dq, outs


def _attn_fwd(q, k, v, sinks, *, B, S, HQ, HK, window, scale, tq, band, name, xchg=None):
    G = HQ // HK
    nq = S // tq
    T = B * S
    has_sink = sinks is not None
    n_x = 0 if xchg is None else len(xchg[1])
    n_in = 4 if has_sink else 3

    def kern(*refs):
        xs, xd, sems = refs[n_in:n_in + n_x], refs[n_in + n_x + 2:n_in + 2 * n_x + 2], refs[n_in + 2 * n_x + 2:]
        refs = refs[:n_in] + refs[n_in + n_x:n_in + n_x + 2]
        if n_x:
            _xchg_at_ends(xchg[0], xs, xd, sems, (B, HQ, nq), True)
        if has_sink:
            q_ref, k_ref, v_ref, s_ref, o_ref, lse_ref = refs
        else:
            q_ref, k_ref, v_ref, o_ref, lse_ref = refs
        q_start = pl.program_id(2) * tq
        qv = q_ref[...]
        if has_sink:
            m0 = jnp.broadcast_to(s_ref[0][:, :1], (tq, 1))
            l0 = jnp.ones((tq, 1), F32)
        else:
            m0 = l0 = None

        def finish(o, lse):
            o_ref[...] = o
            lse_ref[...] = jnp.broadcast_to(lse, (tq, LANES))

        if window is None:
            bidx = q_start // band
            for bb in range(S // band):
                @pl.when(bidx == bb)
                def _(bb=bb):
                    W = (bb + 1) * band
                    finish(*_tile_fwd(qv, k_ref[0:W, :], v_ref[0:W, :], q_start, 0, bb * band, scale, None, m0, l0))
        else:
            sub = min(tq, WINDOW_SUB)
            W = min(S, sub + window)
            for r in range(0, tq, sub):
                k_start = pl.multiple_of(jnp.maximum(q_start + r - window, 0), window)
                o, lse = _tile_fwd(qv[r:r + sub], k_ref[pl.ds(k_start, W), :], v_ref[pl.ds(k_start, W), :], q_start + r,
                                   k_start, 0, scale, window, None if m0 is None else m0[:sub], None if l0 is None else l0[:sub])
                o_ref[r:r + sub, :] = o
                lse_ref[r:r + sub, :] = jnp.broadcast_to(lse, (sub, LANES))
        if n_x:
            _xchg_at_ends(xchg[0], xs, xd, sems, (B, HQ, nq), False)

    q_spec = pl.BlockSpec((tq, LANES), lambda b, h, i: (b * nq + i, h))
    kv_spec = pl.BlockSpec((S, LANES), lambda b, h, i: (b, h // G))
    in_specs = [q_spec, kv_spec, kv_spec]
    args = [q, k, v]
    if has_sink:
        in_specs.append(pl.BlockSpec((1, 1, LANES), lambda b, h, i: (h, 0, 0)))
        args.append(sinks)
    xs = [] if xchg is None else list(xchg[1])
    res = pl.pallas_call(
        kern, name=name, grid=(B, HQ, nq), in_specs=in_specs + [_ANY] * n_x, out_specs=[q_spec, q_spec] + [_ANY] * n_x,
        out_shape=[jax.ShapeDtypeStruct((T, HQ * LANES), F32), jax.ShapeDtypeStruct((T, HQ * LANES), F32)]
        + (_xchg_out_shapes(xchg[0], xs) if n_x else []),
        scratch_shapes=_xchg_scratch(xchg[0], n_x) if n_x else [],
        compiler_params=_params(("arbitrary",) * 3 if n_x else ("parallel", "parallel", "arbitrary")),
    )(*args, *xs)
    return ((res[0], res[1]), list(res[2:])) if n_x else res


def _attn_bwd(q, k, v, o, lse, do, sinks, *, B, S, HQ, HK, window, scale, tq, band, name, xchg=None):
    G = HQ // HK
    nq = S // tq
    T = B * S
    has_sink = sinks is not None
    n_x = 0 if xchg is None else len(xchg[1])
    n_in, n_out = (7, 4) if has_sink else (6, 3)

    def kern(*refs):
        xs, xd = refs[n_in:n_in + n_x], refs[n_in + n_x + n_out:n_in + 2 * n_x + n_out]
        sems = refs[n_in + 2 * n_x + n_out:]
        refs = refs[:n_in] + refs[n_in + n_x:n_in + n_x + n_out]
        if n_x:
            _xchg_at_ends(xchg[0], xs, xd, sems, (B, HK, G, nq), True)
        if has_sink:
            q_ref, k_ref, v_ref, o_ref, lse_ref, do_ref, s_ref, dq_ref, dk_ref, dv_ref, ds_ref = refs
        else:
            q_ref, k_ref, v_ref, o_ref, lse_ref, do_ref, dq_ref, dk_ref, dv_ref = refs
        gi = pl.program_id(2)
        qi = pl.program_id(3)
        q_start = qi * tq

        @pl.when(jnp.logical_and(gi == 0, qi == 0))
        def _():
            dk_ref[...] = jnp.zeros_like(dk_ref)
            dv_ref[...] = jnp.zeros_like(dv_ref)

        qv = q_ref[...]
        dof = do_ref[...]
        ov = o_ref[...]
        lse_v = lse_ref[...][:, :1]
        if window is None:
            bidx = q_start // band
            for bb in range(S // band):
                @pl.when(bidx == bb)
                def _(bb=bb):
                    W = (bb + 1) * band
                    dq, outs = _tile_bwd(qv, k_ref[0:W, :], v_ref[0:W, :], dof, ov, lse_v, q_start, 0, bb * band, scale, None)
                    dq_ref[...] = dq
                    for a, dkp, dvp in outs:
                        dk_ref[a:a + dkp.shape[0], :] += dkp
                        dv_ref[a:a + dvp.shape[0], :] += dvp
        else:
            sub = min(tq, WINDOW_SUB)
            W = min(S, sub + window)
            parts = []
            for r in range(0, tq, sub):
                k_start = pl.multiple_of(jnp.maximum(q_start + r - window, 0), window)
                dq, outs = _tile_bwd(qv[r:r + sub], k_ref[pl.ds(k_start, W), :], v_ref[pl.ds(k_start, W), :],
                                     dof[r:r + sub], ov[r:r + sub], lse_v[r:r + sub], q_start + r, k_start, 0, scale, window)
                dq_ref[r:r + sub, :] = dq
                parts.append((k_start, outs[0][1], outs[0][2]))
            for k_start, dkp, dvp in parts:
                dk_ref[pl.ds(k_start, W), :] += dkp
                dv_ref[pl.ds(k_start, W), :] += dvp
        if has_sink:
            delta = jnp.sum(dof * ov, axis=-1, keepdims=True)
            sink = s_ref[0][:, :1]
            part = -jnp.sum(jnp.exp(sink - lse_v) * delta, axis=0, keepdims=True)
            part = jnp.broadcast_to(part, (1, LANES))

            @pl.when(qi == 0)
            def _():
                ds_ref[0] = part

            @pl.when(qi != 0)
            def _():
                ds_ref[0] += part
        if n_x:
            _xchg_at_ends(xchg[0], xs, xd, sems, (B, HK, G, nq), False)

    q_spec = pl.BlockSpec((tq, LANES), lambda b, hk, g, i: (b * nq + i, hk * G + g))
    kv_spec = pl.BlockSpec((S, LANES), lambda b, hk, g, i: (b, hk))
    in_specs = [q_spec, kv_spec, kv_spec, q_spec, q_spec, q_spec]
    args = [q, k, v, o, lse, do]
    out_specs = [q_spec, kv_spec, kv_spec]
    out_shape = [jax.ShapeDtypeStruct((T, HQ * LANES), F32), jax.ShapeDtypeStruct((T, HK * LANES), F32),
                 jax.ShapeDtypeStruct((T, HK * LANES), F32)]
    if has_sink:
        in_specs.append(pl.BlockSpec((1, 1, LANES), lambda b, hk, g, i: (hk * G + g, 0, 0)))
        args.append(sinks)
        out_specs.append(pl.BlockSpec((1, 1, LANES), lambda b, hk, g, i: (b * HQ + hk * G + g, 0, 0)))
        out_shape.append(jax.ShapeDtypeStruct((B * HQ, 1, LANES), F32))
    xs = [] if xchg is None else list(xchg[1])
    res = pl.pallas_call(
        kern, name=name, grid=(B, HK, G, nq), in_specs=in_specs + [_ANY] * n_x, out_specs=out_specs + [_ANY] * n_x,
        out_shape=out_shape + (_xchg_out_shapes(xchg[0], xs) if n_x else []),
        scratch_shapes=_xchg_scratch(xchg[0], n_x) if n_x else [],
        compiler_params=_params(("arbitrary",) * 4 if n_x else ("parallel", "parallel", "arbitrary", "arbitrary")),
    )(*args, *xs)
    main = tuple(res[:n_out]) if has_sink else (*res[:n_out], None)
    return (main, list(res[n_out:])) if n_x else main


def _rope_tables(pos_col, freq, sign, name):
    T = pos_col.shape[0]
    tr = _tile(T, 1024)

    def kern(p_ref, f_ref, s_ref, c_out, s_out):
        ang = p_ref[...] * f_ref[...]
        c_out[...] = jnp.cos(ang)
        s_out[...] = jnp.sin(ang) * s_ref[...]

    spec = pl.BlockSpec((tr, LANES), lambda i: (i, 0))
    par = pl.BlockSpec((1, LANES), lambda i: (0, 0))
    return pl.pallas_call(
        kern, name=name, grid=(T // tr,), in_specs=[pl.BlockSpec((tr, 1), lambda i: (i, 0)), par, par],
        out_specs=[spec, spec], out_shape=[jax.ShapeDtypeStruct((T, LANES), F32)] * 2,
        compiler_params=_params(("parallel",)),
    )(pos_col, freq, sign)


def _ada_fwd(c8, ada_w, ada_b):
    nch, D, tn = ada_w.shape

    def kern(c_ref, w_ref, b_ref, o_ref):
        act = _silu(c_ref[...]).astype(BF16)
        o_ref[...] = jnp.dot(act, w_ref[...], preferred_element_type=F32) + b_ref[...]

    return pl.pallas_call(
        kern, name="ada_fwd", grid=(nch,),
        in_specs=[pl.BlockSpec((8, D), lambda j: (0, 0)), pl.BlockSpec((None, D, tn), lambda j: (j, 0, 0)),
                  pl.BlockSpec((1, tn), lambda j: (0, j))],
        out_specs=pl.BlockSpec((8, tn), lambda j: (0, j)),
        out_shape=jax.ShapeDtypeStruct((8, nch * tn), F32), compiler_params=_params(("parallel",)),
    )(c8, ada_w, ada_b)


def _ada_bwd(c8, dmod):
    _, N = dmod.shape
    D = c8.shape[1]
    tn = N // N_CHIPS

    def kern(c_ref, d_ref, gw_ref, gb_ref):
        act = _silu(c_ref[...]).astype(BF16)
        d = d_ref[...]
        gw_ref[...] = lax.dot_general(act, d.astype(BF16), _DIMS["tn"], preferred_element_type=F32).astype(BF16)
        gb_ref[...] = jnp.sum(d, axis=0, keepdims=True)

    return pl.pallas_call(
        kern, name="ada_bwd", grid=(N_CHIPS,),
        in_specs=[pl.BlockSpec((8, D), lambda j: (0, 0)), pl.BlockSpec((8, tn), lambda j: (0, j))],
        out_specs=[pl.BlockSpec((None, D, tn), lambda j: (j, 0, 0)), pl.BlockSpec((1, tn), lambda j: (0, j))],
        out_shape=[jax.ShapeDtypeStruct((N_CHIPS, D, tn), BF16), jax.ShapeDtypeStruct((1, N), F32)],
        compiler_params=_params(("parallel",)),
    )(c8, dmod)


def _adamw(w, ga, gb, m, v, name):
    rows, cols = w.shape
    tr = _row_div(rows, 256)
    two = gb is not None

    def kern(*refs):
        if two:
            w_ref, ga_ref, gb_ref, m_ref, v_ref, g_out, d_out, m_out, v_out = refs
            gv = ga_ref[...] + gb_ref[...]
        else:
            w_ref, ga_ref, m_ref, v_ref, g_out, d_out, m_out, v_out = refs
            gv = ga_ref[...]
        mn = ADAM_B1 * m_ref[...] + (1.0 - ADAM_B1) * gv
        vn = ADAM_B2 * v_ref[...] + (1.0 - ADAM_B2) * jnp.square(gv)
        m_hat = mn / (1.0 - ADAM_B1 ** ADAM_STEP)
        v_hat = vn / (1.0 - ADAM_B2 ** ADAM_STEP)
        g_out[...] = gv
        d_out[...] = -ADAM_LR * (m_hat / (jnp.sqrt(v_hat) + ADAM_EPS) + ADAM_WD * w_ref[...])
        m_out[...] = mn
        v_out[...] = vn

    spec = pl.BlockSpec((tr, cols), lambda i: (i, 0))
    args = [w, ga, gb, m, v] if two else [w, ga, m, v]
    return pl.pallas_call(
        kern, name=name, grid=(rows // tr,), in_specs=[spec] * len(args), out_specs=[spec] * 4,
        out_shape=[jax.ShapeDtypeStruct((rows, cols), F32)] * 4, compiler_params=_params(("parallel",)),
    )(*args)


def _sum_slots(x, name):
    n, rows, cols = x.shape
    tr = _row_div(rows, 256)

    def kern(x_ref, o_ref):
        acc = x_ref[0].astype(F32)
        for j in range(1, n):
            acc = acc + x_ref[j].astype(F32)
        o_ref[...] = acc

    return pl.pallas_call(
        kern, name=name, grid=(rows // tr,), in_specs=[pl.BlockSpec((n, tr, cols), lambda i: (0, i, 0))],
        out_specs=pl.BlockSpec((tr, cols), lambda i: (i, 0)), out_shape=jax.ShapeDtypeStruct((rows, cols), F32),
        compiler_params=_params(("parallel",)),
    )(x)


def _all_reduce_small(blob, name):
    R, C = blob.shape

    def kern(src, out, pair, chips, send_sems, recv_sems):
        x, y, c = lax.axis_index("x"), lax.axis_index("y"), lax.axis_index("c")
        me = 2 * x + y
        to_sibling = pltpu.make_async_remote_copy(
            src_ref=src, dst_ref=pair, send_sem=send_sems.at[0], recv_sem=recv_sems.at[0],
            device_id=(x, y, 1 - c), device_id_type=_MESH)
        to_sibling.start()
        to_sibling.wait()
        chips[me] = src[...] + pair[...]
        peers = [(1 - x, y), (x, 1 - y), (1 - x, 1 - y)]
        copies = [pltpu.make_async_remote_copy(
            src_ref=chips.at[me], dst_ref=chips.at[me], send_sem=send_sems.at[1 + kk], recv_sem=recv_sems.at[1 + kk],
            device_id=(px, py, c), device_id_type=_MESH) for kk, (px, py) in enumerate(peers)]
        for cp in copies:
            cp.start()
        for kk, (px, py) in enumerate(peers):
            pltpu.make_async_remote_copy(
                src_ref=chips.at[me], dst_ref=chips.at[2 * px + py], send_sem=send_sems.at[1 + kk],
                recv_sem=recv_sems.at[1 + kk], device_id=(px, py, c), device_id_type=_MESH).wait_recv()
        for cp in copies:
            cp.wait_send()
        acc = chips[0]
        for j in range(1, N_CHIPS):
            acc = acc + chips[j]
        out[...] = acc

    vm = pl.BlockSpec(memory_space=pltpu.VMEM)
    return pl.pallas_call(
        kern, name=name, in_specs=[vm], out_specs=vm, out_shape=jax.ShapeDtypeStruct((R, C), F32),
        scratch_shapes=[pltpu.VMEM((R, C), F32), pltpu.VMEM((N_CHIPS, R, C), F32), pltpu.SemaphoreType.DMA((4,)),
                        pltpu.SemaphoreType.DMA((4,))],
        compiler_params=pltpu.CompilerParams(vmem_limit_bytes=VMEM_LIMIT),
    )(blob)


def _row_tile(S):
    return 256 if S % 256 == 0 else 128


def _attn_tiles(S):
    return min(S, 512), min(S, 256), min(S, 512)


def _hosted(fn, host, got, kind, name, *args, **kw):
    if host is not None and host.get(name):
        res, xs = fn(*args, name=name, xchg=(kind, host[name]), **kw)
        got[name] = xs
        return res
    return fn(*args, name=name, **kw)


def _mm(host, got, kind, a, b, mode, out_dtype, name, **kw):
    return _hosted(_matmul, host, got, kind, name, a, b, mode, out_dtype, **kw)


def _sgu_rows(S):
    return 4 * C_CHUNK if S % (4 * C_CHUNK) == 0 else C_CHUNK


def _layer_fwd(xin, prev, mods, w, tabs, B, S, host=None, late=None):
    T = B * S
    tr = _row_tile(S)
    sh1, sc1, g1, sh2, sc2, g2 = mods
    ca, sa, cb, sb = tabs
    sv = {}
    got = {}
    if prev is None:
        def body(rv, ev, pv):
            return [_f_norm_mod(rv[0], pv[0], ev[0], ev[1])], [], []
        (h,), _, _ = _rowcall(body, name="f_norm_mod", T=T, S=S, tr=tr, rows=[(xin, D_MODEL, 0)], exs=[sc1, sh1],
                              pars=[w["norm1_w"]], row_outs=[(D_MODEL, BF16, D_MODEL, 0)])
        x = xin
    else:
        x1p, dp, g2p = prev

        def body(rv, ev, pv):
            xn, hh = _f_resid_norm_mod(rv[0], rv[1], ev[0], pv[0], ev[1], ev[2])
            return [xn, hh], [], []
        (x, h), _, _ = _rowcall(body, name="f_resid_norm_mod1", T=T, S=S, tr=tr,
                                rows=[(x1p, D_MODEL, 0), (dp, D_MODEL, 0)], exs=[g2p, sc1, sh1], pars=[w["norm1_w"]],
                                row_outs=[(D_MODEL, F32, D_MODEL, 0), (D_MODEL, BF16, D_MODEL, 0)])
    sv["x"], sv["h"] = x, h
    proj = _mm(host, got, "gather", h, w["w_in"], "nn", BF16, "mm_in")
    sv["proj"] = proj

    def body(rv, ev, pv):
        outs = _f_mixprep(rv[0], rv[1], rv[2], rv[3], rv[4], pv[0], pv[1])
        return list(outs), [], []
    (qa, ka, va, cqn, ckvn, kr), _, _ = _rowcall(
        body, name="f_mixprep", T=T, S=S, tr=tr,
        rows=[(proj, P_CU, 0), (ca, LANES, 0), (sa, LANES, 0), (cb, LANES, 0), (sb, LANES, 0)],
        pars=[w["b_q_norm_w"], w["b_kv_norm_w"]],
        row_outs=[(768, BF16, 768, 0), (256, BF16, 256, 0), (256, BF16, 256, 0), (384, BF16, 384, 0),
                  (256, BF16, 256, 0), (LANES, F32, LANES, 0)])
    sv.update(qa=qa, ka=ka, va=va, cqn=cqn, ckvn=ckvn)
    q = _matmul(cqn, w["b_w_uq"], "nn", BF16, "mm_uq")
    kv = _matmul(ckvn, w["b_w_ukv"], "nn", BF16, "mm_ukv")

    def body(rv, ev, pv):
        Q, K, V = _f_mlaprep(rv[0], rv[1], rv[2], rv[3], rv[4])
        return [Q, K, V], [], []
    (Q, K, V), _, _ = _rowcall(
        body, name="f_mlaprep", T=T, S=S, tr=tr,
        rows=[(q, 768, 0), (kv, 1536, 0), (kr, LANES, 0), (cb, LANES, 0), (sb, LANES, 0)],
        row_outs=[(768, BF16, 768, 0)] * 3)
    sv.update(Q=Q, K=K, V=V)
    ta, tb, band = _attn_tiles(S)
    oa, lse_a = _hosted(_attn_fwd, host, got, "gather", "attn_a_fwd", qa, ka, va, w["sinks"], B=B, S=S, HQ=A_Q_HEADS,
                        HK=A_KV_HEADS, window=A_WINDOW, scale=HEAD_DIM ** -0.5, tq=ta, band=None)
    ob, lse_b = _hosted(_attn_fwd, host, got, "gather", "attn_b_fwd", Q, K, V, None, B=B, S=S, HQ=B_HEADS, HK=B_HEADS,
                        window=None, scale=(B_NOPE + B_ROPE) ** -0.5, tq=tb, band=band)
    sv.update(oa=oa, lse_a=lse_a, ob=ob, lse_b=lse_b)
    if late is not None:
        w = {**w, **late(got)}
    ts = _sgu_rows(S)

    def body(rv, ev, pv):
        outs = [_f_sgu(rv[0][r:r + C_CHUNK], rv[1][r:r + C_CHUNK], pv[0], pv[1], pv[2], pv[3])
                for r in range(0, ts, C_CHUNK)]
        return [jnp.concatenate(outs, axis=0)], [], []
    (yc,), _, _ = _rowcall(body, name="f_sgu", T=T, S=S, tr=ts,
                           rows=[(proj, C_WIDTH, P_CU // C_WIDTH), (proj, C_WIDTH, P_CV // C_WIDTH)],
                           pars=[w["c_ln_w"], w["c_ln_b"], w["c_w_s"], w["c_b_col"]],
                           row_outs=[(C_WIDTH, F32, C_WIDTH, 0)])
    sv["yc"] = yc

    def body(rv, ev, pv):
        return [_f_outnorm(rv[0], rv[1], rv[2], pv[0])], [], []
    (y,), _, _ = _rowcall(body, name="f_outnorm", T=T, S=S, tr=tr,
                          rows=[(oa, 768, 0), (ob, 768, 0), (yc, C_WIDTH, 0)], pars=[w["out_norm_w"]],
                          row_outs=[(Y_END, BF16, Y_END, 0)])
    sv["y"] = y
    o = _mm(host, got, "gather", y, w["w_out"], "nn", F32, "mm_out")
    sv["o"] = o

    def body(rv, ev, pv):
        xn, hh = _f_resid_norm_mod(rv[0], rv[1], ev[0], pv[0], ev[1], ev[2])
        return [xn, hh], [], []
    (x1, h2), _, _ = _rowcall(body, name="f_resid_norm_mod2", T=T, S=S, tr=tr,
                              rows=[(x, D_MODEL, 0), (o, D_MODEL, 0)], exs=[g1, sc2, sh2], pars=[w["norm2_w"]],
                              row_outs=[(D_MODEL, F32, D_MODEL, 0), (D_MODEL, BF16, D_MODEL, 0)])
    sv["h2"] = h2
    gu = _mm(host, got, "gather", h2, w["w_gate_up"], "nn", BF16, "mm_gate_up", b_chunks=True)
    sv["gu"] = gu

    def body(rv, ev, pv):
        return [_f_swiglu(rv[0], rv[1])], [], []
    (act,), _, _ = _rowcall(body, name="f_swiglu", T=T, S=S, tr=tr,
                            rows=[(gu, FFN_HIDDEN, 0), (gu, FFN_HIDDEN, 1)], row_outs=[(FFN_HIDDEN, BF16, FFN_HIDDEN, 0)])
    sv["act"] = act
    d = _mm(host, got, "gather", act, w["w_down"], "nn", F32, "mm_down")
    sv["x1"], sv["d"], sv["w"] = x1, d, w
    return (x1, d, g2), sv, got


def _final(x1, d, g2, fw, target, B, S):
    T = B * S
    tr = _row_tile(S)

    def loss_fn(x1v, dv, gv, wv, tv):
        yf = _rms(x1v + gv * dv, wv)
        return 0.5 * jnp.sum(jnp.mean(jnp.square(yf - tv), axis=-1))

    def body(rv, ev, pv):
        x1v, dv, tv = rv
        val, vjp = jax.vjp(lambda a, b_, g, ww: loss_fn(a, b_, g, ww, tv), x1v, dv, ev[0], pv[0])
        dx1, dd, dg, dw = vjp(jnp.ones((), F32))
        return [dx1, dd], [dg], [dw, jnp.full((1, LANES), val, F32)]
    (dx1, dd), (dg2,), (dfw, loss) = _rowcall(
        body, name="final_loss", T=T, S=S, tr=tr, rows=[(x1, D_MODEL, 0), (d, D_MODEL, 0), (target, D_MODEL, 0)],
        exs=[g2], pars=[fw], row_outs=[(D_MODEL, F32, D_MODEL, 0), (D_MODEL, BF16, D_MODEL, 0)],
        ex_outs=[D_MODEL], par_outs=[(1, D_MODEL), (1, LANES)])
    return loss, dx1, dd, dg2, dfw


def _layer_bwd(sv, prev, mods, tabs, dx1, dd, B, S, host=None, own_early=False):
    w = sv["w"]
    T = B * S
    tr = _row_tile(S)
    sh1, sc1, g1, sh2, sc2, g2 = mods
    ca, sa, cb, sb = tabs
    gr = {}
    got = {}
    dact = _mm(host, got, "a2a", dd, w["w_down"], "nt", BF16, "mm_down_dx")
    gr["w_down"] = _matmul(sv["act"], dd, "tn", BF16, "mm_down_dw")

    def body(rv, ev, pv):
        _, vjp = jax.vjp(_f_swiglu, rv[0], rv[1])
        dgate, dup = vjp(rv[2])
        return [[dgate, dup]], [], []
    (dgu,), _, _ = _rowcall(body, name="b_swiglu", T=T, S=S, tr=tr,
                            rows=[(sv["gu"], FFN_HIDDEN, 0), (sv["gu"], FFN_HIDDEN, 1), (dact, FFN_HIDDEN, 0)],
                            row_outs=[(2 * FFN_HIDDEN, BF16, 2 * FFN_HIDDEN, 0)])
    dh2 = _mm(host, got, "a2a", dgu, w["w_gate_up"], "nt", BF16, "mm_gate_up_dx", b_chunks=True)
    gr["w_gate_up"] = _mm(host, got, "a2a", sv["h2"], dgu, "tn", BF16, "mm_gate_up_dw", out_chunks=True)

    def body(rv, ev, pv):
        xa, delta, dh, dxn = rv
        _, vjp = jax.vjp(_f_resid_norm_mod, xa, delta, ev[0], pv[0], ev[1], ev[2])
        dxa, ddelta, dg, dw, dsc, dsh = vjp((dxn, dh))
        return [dxa, ddelta], [dg, dsc, dsh], [dw]
    (dx, do), (dg1, dsc2, dsh2), (gr["norm2_w"],) = _rowcall(
        body, name="b_resid_norm_mod2", T=T, S=S, tr=tr,
        rows=[(sv["x"], D_MODEL, 0), (sv["o"], D_MODEL, 0), (dh2, D_MODEL, 0), (dx1, D_MODEL, 0)],
        exs=[g1, sc2, sh2], pars=[w["norm2_w"]],
        row_outs=[(D_MODEL, F32, D_MODEL, 0), (D_MODEL, BF16, D_MODEL, 0)], ex_outs=[D_MODEL] * 3,
        par_outs=[(1, D_MODEL)])
    dy = _matmul(do, w["w_out"], "nt", BF16, "mm_out_dx")
    gr["w_out"] = _matmul(sv["y"], do, "tn", BF16, "mm_out_dw")

    def body(rv, ev, pv):
        _, vjp = jax.vjp(_f_outnorm, rv[0], rv[1], rv[2], pv[0])
        doa, dob, dyc, dgw = vjp(rv[3])
        return [doa, dob, dyc], [], [dgw]
    (doa, dob, dyc), _, (gr["out_norm_w"],) = _rowcall(
        body, name="b_outnorm", T=T, S=S, tr=tr,
        rows=[(sv["oa"], 768, 0), (sv["ob"], 768, 0), (sv["yc"], C_WIDTH, 0), (dy, Y_END, 0)], pars=[w["out_norm_w"]],
        row_outs=[(768, F32, 768, 0), (768, F32, 768, 0), (C_WIDTH, F32, C_WIDTH, 0)], par_outs=[(1, Y_END)])

    ta, tb, band = _attn_tiles(S)
    if own_early:
        host = dict(host)
        host["attn_b_bwd"] = list(host.get("attn_b_bwd", ())) + [
            gr["w_gate_up"], gr["w_down"].reshape(N_CHIPS, FFN_HIDDEN // N_CHIPS, D_MODEL)]
    dQ, dK, dV, _ = _hosted(_attn_bwd, host, got, "a2a", "attn_b_bwd", sv["Q"], sv["K"], sv["V"], sv["ob"], sv["lse_b"],
                            dob, None, B=B, S=S, HQ=B_HEADS, HK=B_HEADS, window=None,
                            scale=(B_NOPE + B_ROPE) ** -0.5, tq=tb, band=band)
    dqa, dka, dva, dsink = _hosted(_attn_bwd, host, got, "a2a", "attn_a_bwd", sv["qa"], sv["ka"], sv["va"], sv["oa"],
                                   sv["lse_a"], doa, w["sinks"], B=B, S=S, HQ=A_Q_HEADS, HK=A_KV_HEADS,
                                   window=A_WINDOW, scale=HEAD_DIM ** -0.5, tq=ta, band=None)
    gr["sinks"] = dsink

    def body(rv, ev, pv):
        dQv, dKv, dVv, cbv, sbv = rv
        dq = [_rope_bwd((cbv, sbv), p)[0] for p in _heads(dQv, B_HEADS)]
        dkr = None
        for p in _heads(dKv, B_HEADS):
            dkr = p if dkr is None else dkr + p
        return [dq, [dKv, dVv], dkr], [], []
    (dq, dkv, dkr), _, _ = _rowcall(
        body, name="b_mlaprep", T=T, S=S, tr=tr,
        rows=[(dQ, 768, 0), (dK, 768, 0), (dV, 768, 0), (cb, LANES, 0), (sb, LANES, 0)],
        row_outs=[(768, BF16, 768, 0), (1536, BF16, 1536, 0), (LANES, F32, LANES, 0)])
    dcqn = _matmul(dq, w["b_w_uq"], "nt", BF16, "mm_uq_dx")
    gr["b_w_uq"] = _matmul(sv["cqn"], dq, "tn", BF16, "mm_uq_dw")
    dckvn = _matmul(dkv, w["b_w_ukv"], "nt", BF16, "mm_ukv_dx")
    gr["b_w_ukv"] = _matmul(sv["ckvn"], dkv, "tn", BF16, "mm_ukv_dw")

    def body(rv, ev, pv):
        proj, cav, sav, cbv, sbv, dqa_, dka_, dva_, dcqn_, dckvn_, dkr_ = rv
        _, vjp = jax.vjp(lambda p, a, b_: _f_mixprep(p, cav, sav, cbv, sbv, a, b_), proj, pv[0], pv[1])
        dproj, dqn, dkvn = vjp((dqa_, dka_, dva_, dcqn_, dckvn_, dkr_))
        return [dproj], [], [dqn, dkvn]
    (dproj,), _, (gr["b_q_norm_w"], gr["b_kv_norm_w"]) = _rowcall(
        body, name="b_mixprep", T=T, S=S, tr=tr,
        rows=[(sv["proj"], P_CU, 0), (ca, LANES, 0), (sa, LANES, 0), (cb, LANES, 0), (sb, LANES, 0),
              (dqa, 768, 0), (dka, 256, 0), (dva, 256, 0), (dcqn, 384, 0), (dckvn, 256, 0), (dkr, LANES, 0)],
        pars=[w["b_q_norm_w"], w["b_kv_norm_w"]], row_outs=[(P_END, BF16, P_CU, 0)],
        par_outs=[(1, B_Q_RANK), (1, B_KV_RANK)])

    ts = _sgu_rows(S)

    def body(rv, ev, pv):
        cu, cv, dycv, _ = rv
        dcus, dcvs, acc = [], [], None
        for r in range(0, ts, C_CHUNK):
            _, vjp = jax.vjp(_f_sgu, cu[r:r + C_CHUNK], cv[r:r + C_CHUNK], pv[0], pv[1], pv[2], pv[3])
            dcu, dcv, *dpar = vjp(dycv[r:r + C_CHUNK])
            dcus.append(dcu)
            dcvs.append(dcv)
            acc = dpar if acc is None else [a + b_ for a, b_ in zip(acc, dpar)]
        return [[jnp.concatenate(dcus, axis=0), jnp.concatenate(dcvs, axis=0)]], [], acc
    (dproj,), _, (gr["c_ln_w"], gr["c_ln_b"], gr["c_w_s"], gr["c_b_col"]) = _rowcall(
        body, name="b_sgu", T=T, S=S, tr=ts,
        rows=[(sv["proj"], C_WIDTH, P_CU // C_WIDTH), (sv["proj"], C_WIDTH, P_CV // C_WIDTH), (dyc, C_WIDTH, 0),
              (dproj, 2 * C_WIDTH, P_CU // (2 * C_WIDTH))],
        pars=[w["c_ln_w"], w["c_ln_b"], w["c_w_s"], w["c_b_col"]],
        row_outs=[(P_END, BF16, 2 * C_WIDTH, P_CU // (2 * C_WIDTH))],
        par_outs=[(1, C_WIDTH), (1, C_WIDTH), (C_GROUPS, C_CHUNK, C_CHUNK), (C_GROUPS, C_CHUNK, 1)],
        aliases={3: 0})
    dh = _mm(host, got, "a2a", dproj, w["w_in"], "nt", BF16, "mm_in_dx")
    gr["w_in"] = _matmul(sv["h"], dproj, "tn", BF16, "mm_in_dw")

    if prev is None:
        def body(rv, ev, pv):
            xv, dhv, dxd = rv
            _, vjp = jax.vjp(_f_norm_mod, xv, pv[0], ev[0], ev[1])
            dxa, dw, dsc, dsh = vjp(dhv)
            return [dxa + dxd], [dsc, dsh], [dw]
        (dxin,), (dsc1, dsh1), (gr["norm1_w"],) = _rowcall(
            body, name="b_norm_mod", T=T, S=S, tr=tr, rows=[(sv["x"], D_MODEL, 0), (dh, D_MODEL, 0), (dx, D_MODEL, 0)],
            exs=[sc1, sh1], pars=[w["norm1_w"]], row_outs=[(D_MODEL, F32, D_MODEL, 0)], ex_outs=[D_MODEL] * 2,
            par_outs=[(1, D_MODEL)])
        nxt = (dxin, None, None)
    else:
        x1p, dp, g2p = prev

        def body(rv, ev, pv):
            xa, delta, dhv, dxn = rv
            _, vjp = jax.vjp(_f_resid_norm_mod, xa, delta, ev[0], pv[0], ev[1], ev[2])
            dxa, ddelta, dg, dw, dsc, dsh = vjp((dxn, dhv))
            return [dxa, ddelta], [dg, dsc, dsh], [dw]
        (dx1p, ddp), (dg2p, dsc1, dsh1), (gr["norm1_w"],) = _rowcall(
            body, name="b_resid_norm_mod1", T=T, S=S, tr=tr,
            rows=[(x1p, D_MODEL, 0), (dp, D_MODEL, 0), (dh, D_MODEL, 0), (dx, D_MODEL, 0)],
            exs=[g2p, sc1, sh1], pars=[w["norm1_w"]],
            row_outs=[(D_MODEL, F32, D_MODEL, 0), (D_MODEL, BF16, D_MODEL, 0)], ex_outs=[D_MODEL] * 3,
            par_outs=[(1, D_MODEL)])
        nxt = (dx1p, ddp, dg2p)
    return gr, (dsh1, dsc1, dg1, dsh2, dsc2), nxt, got


def _lane_table(lanes_neg, lanes_pos, inv):
    freq = np.zeros((LANES,), np.int64) - 1
    sign = np.zeros((1, LANES), np.float32)
    n = len(lanes_neg)
    freq[lanes_neg] = np.arange(n)
    freq[lanes_pos] = np.arange(n)
    sign[0, lanes_neg] = -1.0
    sign[0, lanes_pos] = 1.0
    return _select_axis(inv, freq, 0).reshape(1, LANES), jnp.asarray(sign)


SHARDED = ("ada_w", "w_in", "b_w_uq", "b_w_ukv", "w_out", "w_gate_up", "w_down")
ROW_SHARDED = ("w_out", "w_down")
SMALL = ("ada_b", "norm1_w", "a_sinks", "b_q_norm_w", "b_kv_norm_w", "c_ln_w", "c_ln_b", "c_w_s", "c_b_s",
         "out_norm_w", "norm2_w", "final_norm_w")
FWD_HOST = {"attn_b_fwd": ("ada_w", "w_gate_up"), "attn_a_fwd": ("w_down", "w_in"),
            "mm_gate_up": ("w_out", "b_w_uq", "b_w_ukv")}
FWD_HOST_FIRST = {"attn_b_fwd": ("ada_w", "w_gate_up"), "mm_gate_up": ("w_down", "w_out", "b_w_uq", "b_w_ukv"),
                  "mm_down": ("w_in",)}
BWD_HOST = {"attn_a_bwd": ("w_in", "w_out", "b_w_uq", "b_w_ukv"), "attn_b_bwd": ("ada_w", "w_gate_up", "w_down")}
EXPOSED = ("ada_w", "w_in", "b_w_uq", "b_w_ukv", "w_out")
OWN_LAYER = ("w_gate_up", "w_down")


def _from_host(table, got):
    return {k: got[name][i] for name, ks in table.items() for i, k in enumerate(ks)}


def _to_host(table, arrays):
    return {name: [arrays[k] for k in ks] for name, ks in table.items()}


def _join_cols(g):
    return jnp.concatenate([g[j] for j in range(N_CHIPS)], axis=1)


def _split_cols(g):
    n = g.shape[1] // N_CHIPS
    return jnp.stack([g[:, j * n:(j + 1) * n] for j in range(N_CHIPS)])


def _layer_weights(G, small, l):
    D = D_MODEL
    w = {} if "w_gate_up" not in G else _late_weights(G["w_gate_up"], G["w_down"])
    return {
        **w,
        "norm1_w": small["norm1_w"][l].reshape(1, D),
        "w_in": _pad_axis(_join_cols(G["w_in"]), _map_w_in(), 1),
        "sinks": jnp.broadcast_to(small["a_sinks"][l].reshape(A_Q_HEADS, 1, 1), (A_Q_HEADS, 1, LANES)),
        "b_q_norm_w": small["b_q_norm_w"][l].reshape(1, B_Q_RANK),
        "b_w_uq": _pad_axis(_join_cols(G["b_w_uq"]), _map_w_uq(), 1),
        "b_kv_norm_w": small["b_kv_norm_w"][l].reshape(1, B_KV_RANK),
        "b_w_ukv": _pad_axis(_join_cols(G["b_w_ukv"]), _map_w_ukv(), 1),
        "c_ln_w": small["c_ln_w"][l].reshape(1, C_WIDTH), "c_ln_b": small["c_ln_b"][l].reshape(1, C_WIDTH),
        "c_w_s": small["c_w_s"][l], "c_b_col": small["c_b_s"][l].reshape(C_GROUPS, C_CHUNK, 1),
        "out_norm_w": _pad_axis(small["out_norm_w"][l].reshape(1, D), _map_w_out(), 1),
        "w_out": _pad_axis(G["w_out"].reshape(D, D), _map_w_out(), 0),
        "norm2_w": small["norm2_w"][l].reshape(1, D),
    }


def _late_weights(gate_up, down):
    return {"w_gate_up": gate_up, "w_down": down.reshape(FFN_HIDDEN, D_MODEL)}


def _send_buffers(gr, ada_gw):
    D = D_MODEL
    return {
        "ada_w": ada_gw, "w_gate_up": gr["w_gate_up"], "w_down": gr["w_down"].reshape(N_CHIPS, FFN_HIDDEN // N_CHIPS, D),
        "w_out": _unpad_axis(gr["w_out"], _map_w_out(), D, 0).reshape(N_CHIPS, D // N_CHIPS, D),
        "w_in": _split_cols(_unpad_axis(gr["w_in"], _map_w_in(), IN_COLS, 1)),
        "b_w_uq": _split_cols(_unpad_axis(gr["b_w_uq"], _map_w_uq(), B_HEADS * (B_NOPE + B_ROPE), 1)),
        "b_w_ukv": _split_cols(_unpad_axis(gr["b_w_ukv"], _map_w_ukv(), B_HEADS * (B_NOPE + B_V), 1)),
    }


def _small_grads(gr, ada_gb, B):
    D = D_MODEL
    return {
        "ada_b": ada_gb.reshape(N_MOD * D), "norm1_w": gr["norm1_w"].reshape(D),
        "a_sinks": gr["sinks"][:, 0, 0].reshape(B, A_Q_HEADS).sum(axis=0),
        "b_q_norm_w": gr["b_q_norm_w"].reshape(B_Q_RANK), "b_kv_norm_w": gr["b_kv_norm_w"].reshape(B_KV_RANK),
        "c_ln_w": gr["c_ln_w"].reshape(C_WIDTH), "c_ln_b": gr["c_ln_b"].reshape(C_WIDTH), "c_w_s": gr["c_w_s"],
        "c_b_s": gr["c_b_col"].reshape(C_GROUPS, C_CHUNK),
        "out_norm_w": _unpad_axis(gr["out_norm_w"], _map_w_out(), D, 1).reshape(D), "norm2_w": gr["norm2_w"].reshape(D),
    }


def _step(x, c, positions, target, small, shard_of, gathered=None):
    dist = gathered is None
    B, S, D = x.shape
    T = B * S
    xt = x.reshape(T, D)
    tgt = target.reshape(T, D)
    pos_col = positions.astype(F32).reshape(T, 1)
    inv_a = 1.0 / (ROPE_THETA ** (jnp.arange(0, HEAD_DIM, 2, dtype=F32) / HEAD_DIM))
    inv_b = 1.0 / (ROPE_THETA ** (jnp.arange(0, B_ROPE, 2, dtype=F32) / B_ROPE))
    fa, sga = _lane_table(np.arange(32), 64 + np.arange(32), inv_a)
    fb, sgb = _lane_table(48 + np.arange(16), 112 + np.arange(16), inv_b)
    ca, sa = _rope_tables(pos_col, fa, sga, "rope_a")
    cb, sb = _rope_tables(pos_col, fb, sgb, "rope_b")
    tabs = (ca, sa, cb, sb)
    c8 = jnp.zeros((8, D), F32).at[:B].set(c)

    if dist:
        first = shard_of(0)
        G = dict(zip(EXPOSED, _xchg_call("gather", [first[k] for k in EXPOSED], "gather_first")))
    else:
        G = gathered[0]
    saved, prevs, modss = [], [], []
    prev = None
    for l in range(DEPTH):
        w = _layer_weights(G, small, l)
        mod = _ada_fwd(c8, G["ada_w"], small["ada_b"][l].reshape(1, N_MOD * D))
        mods = tuple(mod[:B, i * D:(i + 1) * D].reshape(B, 1, D) for i in range(N_MOD))
        more = l + 1 < DEPTH
        table = FWD_HOST_FIRST if l == 0 else FWD_HOST
        host = _to_host(table, shard_of(l + 1)) if dist and more else {}
        late = None
        if dist and l == 0:
            host["attn_a_fwd"] = [first[k] for k in OWN_LAYER]
            late = lambda got: _late_weights(*got["attn_a_fwd"])
        prevs.append(prev)
        modss.append(mods)
        prev, sv, got = _layer_fwd(xt, prev, mods, w, tabs, B, S, host, late)
        saved.append(sv)
        if more:
            G = _from_host(table, got) if dist else gathered[l + 1]
    x1, d, g2 = prev
    loss, dx1, dd, dg2, dfw = _final(x1, d, g2, small["final_norm_w"].reshape(1, D), tgt, B, S)

    landed = [None] * DEPTH
    smalls = [None] * DEPTH
    pending = None
    for l in reversed(range(DEPTH)):
        host = _to_host(BWD_HOST, pending) if dist and pending is not None else None
        early = dist and l == 0 and host is not None
        gr, (dsh1, dsc1, dg1, dsh2, dsc2), nxt, got = _layer_bwd(saved[l], prevs[l], modss[l], tabs, dx1, dd, B, S, host, early)
        if pending is not None:
            landed[l + 1] = _from_host(BWD_HOST, got) if dist else pending
        dmod = jnp.concatenate([dsh1, dsc1, dg1, dsh2, dsc2, dg2], axis=-1).reshape(B, N_MOD * D)
        ada_gw, ada_gb = _ada_bwd(c8, jnp.zeros((8, N_MOD * D), F32).at[:B].set(dmod))
        pending = _send_buffers(gr, ada_gw)
        smalls[l] = _small_grads(gr, ada_gb, B)
        dx1, dd, dg2 = nxt
    if dist:
        last = EXPOSED if early else SHARDED
        landed[0] = dict(zip(last, _xchg_call("a2a", [pending[k] for k in last], "grad_exchange_last")))
        if early:
            landed[0].update(zip(OWN_LAYER, got["attn_b_bwd"][-len(OWN_LAYER):]))
    else:
        landed[0] = pending
    small_g = {k: jnp.stack([smalls[l][k] for l in range(DEPTH)]) for k in SMALL if k != "final_norm_w"}
    small_g["final_norm_w"] = dfw.reshape(D)
    return loss[0, 0], dx1.reshape(B, S, D), landed, small_g


def _pack(arrs, cols, mult):
    flat = jnp.concatenate([a.reshape(-1) for a in arrs])
    n = flat.shape[0]
    rows = -(-n // cols)
    rows = -(-rows // mult) * mult
    return jnp.pad(flat, (0, rows * cols - n)).reshape(rows, cols)


def _unpack(blob, shapes):
    flat = blob.reshape(-1)
    out, off = [], 0
    for s in shapes:
        n = int(np.prod(s))
        out.append(flat[off:off + n].reshape(s))
        off += n
    return out


def kernel(x, c, positions, ada_w, ada_b, norm1_w, w_in, a_sinks, b_q_norm_w, b_w_uq, b_kv_norm_w, b_w_ukv, c_ln_w, c_ln_b, c_w_s, c_b_s, out_norm_w, w_out, norm2_w, w_gate_up, w_down, final_norm_w, loss_target, m_ada_w, m_ada_b, m_norm1_w, m_w_in, m_a_sinks, m_b_q_norm_w, m_b_w_uq, m_b_kv_norm_w, m_b_w_ukv, m_c_ln_w, m_c_ln_b, m_c_w_s, m_c_b_s, m_out_norm_w, m_w_out, m_norm2_w, m_w_gate_up, m_w_down, m_final_norm_w, v_ada_w, v_ada_b, v_norm1_w, v_w_in, v_a_sinks, v_b_q_norm_w, v_b_w_uq, v_b_kv_norm_w, v_b_w_ukv, v_c_ln_w, v_c_ln_b, v_c_w_s, v_c_b_s, v_out_norm_w, v_w_out, v_norm2_w, v_w_gate_up, v_w_down, v_final_norm_w):
    names = ("ada_w", "ada_b", "norm1_w", "w_in", "a_sinks", "b_q_norm_w", "b_w_uq", "b_kv_norm_w", "b_w_ukv", "c_ln_w",
             "c_ln_b", "c_w_s", "c_b_s", "out_norm_w", "w_out", "norm2_w", "w_gate_up", "w_down", "final_norm_w")
    ws = dict(zip(names, (ada_w, ada_b, norm1_w, w_in, a_sinks, b_q_norm_w, b_w_uq, b_kv_norm_w, b_w_ukv, c_ln_w, c_ln_b,
                          c_w_s, c_b_s, out_norm_w, w_out, norm2_w, w_gate_up, w_down, final_norm_w)))
    ms = dict(zip(names, (m_ada_w, m_ada_b, m_norm1_w, m_w_in, m_a_sinks, m_b_q_norm_w, m_b_w_uq, m_b_kv_norm_w, m_b_w_ukv,
                          m_c_ln_w, m_c_ln_b, m_c_w_s, m_c_b_s, m_out_norm_w, m_w_out, m_norm2_w, m_w_gate_up, m_w_down,
                          m_final_norm_w)))
    vs = dict(zip(names, (v_ada_w, v_ada_b, v_norm1_w, v_w_in, v_a_sinks, v_b_q_norm_w, v_b_w_uq, v_b_kv_norm_w, v_b_w_ukv,
                          v_c_ln_w, v_c_ln_b, v_c_w_s, v_c_b_s, v_out_norm_w, v_w_out, v_norm2_w, v_w_gate_up, v_w_down,
                          v_final_norm_w)))
    shards = {k: ws[k].astype(BF16) for k in SHARDED}
    loss_local, grad_x, landed, gsmall = _step(x, c, positions, loss_target, {k: ws[k] for k in SMALL},
                                               lambda l: {k: shards[k][l] for k in SHARDED})

    mine = {k: jnp.stack([_sum_slots(landed[l][k], "grad_sum_" + k) for l in range(DEPTH)]) for k in SHARDED}
    theirs = dict(zip(SHARDED, _xchg_call("swap", [mine[k] for k in SHARDED], "grad_sibling_swap")))
    grads, delta, new_m, new_v = {}, {}, {}, {}
    for k in SHARDED:
        shp = ws[k].shape
        two = (shp[0] * shp[1], shp[2])
        g, dlt, nm, nv = _adamw(ws[k].reshape(two), mine[k].reshape(two), theirs[k].reshape(two), ms[k].reshape(two),
                                vs[k].reshape(two), "adamw_" + k)
        grads[k], delta[k], new_m[k], new_v[k] = g.reshape(shp), dlt.reshape(shp), nm.reshape(shp), nv.reshape(shp)

    small_shapes = [ws[k].shape for k in SMALL]
    sblob = _pack([gsmall[k] for k in SMALL] + [loss_local.reshape(1)], LANES, 8)
    svals = _unpack(_all_reduce_small(sblob, "small_all_reduce"), small_shapes + [(1,)])
    loss = svals[-1].reshape(())
    pw = _pack([ws[k] for k in SMALL], LANES, 8)
    pg = _pack(svals[:-1], LANES, 8)
    pm = _pack([ms[k] for k in SMALL], LANES, 8)
    pv = _pack([vs[k] for k in SMALL], LANES, 8)
    g, dlt, nm, nv = _adamw(pw, pg, None, pm, pv, "adamw_small")
    for k, a, b_, c_, d_ in zip(SMALL, _unpack(g, small_shapes), _unpack(dlt, small_shapes), _unpack(nm, small_shapes),
                                _unpack(nv, small_shapes)):
        grads[k], delta[k], new_m[k], new_v[k] = a, b_, c_, d_

    return (loss, grad_x, *[grads[k] for k in names], *[delta[k] for k in names], *[new_m[k] for k in names],
            *[new_v[k] for k in names])
```

```python
import functools
import math

import numpy as np
import jax
import jax.numpy as jnp
from jax import lax
from jax.experimental import pallas as pl
from jax.experimental.pallas import tpu as pltpu

F32 = jnp.float32
BF16 = jnp.bfloat16

D_MODEL = 1024
DEPTH = 4
HEAD_DIM = 64
ROPE_THETA = 10000.0
NORM_EPS = 1e-6
NEG_INF = -1e30
A_Q_HEADS = 6
A_KV_HEADS = 2
A_WINDOW = 128
B_HEADS = 6
B_Q_RANK = 384
B_KV_RANK = 256
B_NOPE = 64
B_ROPE = 32
B_V = 64
C_GROUPS = 4
C_GROUP_DIM = 64
C_WIDTH = 256
C_CHUNK = 128
IN_COLS = 1824
FFN_HIDDEN = 2816
N_MOD = 6
ADAM_LR = 0.001
ADAM_B1 = 0.9
ADAM_B2 = 0.999
ADAM_EPS = 1e-08
ADAM_WD = 0.01
ADAM_STEP = 10

LANES = 128
VMEM_LIMIT = 56 * 1024 * 1024
N_CHIPS = 4
WINDOW_SUB = 256

P_AQ, P_AK, P_AV, P_CQ, P_CKV, P_KR, P_CU, P_CV, P_END = 0, 768, 1024, 1280, 1664, 1920, 2048, 2304, 2560


def _map_w_in():
    idx = -np.ones(P_END, np.int64)
    half = HEAD_DIM // 2
    for h in range(A_Q_HEADS):
        idx[P_AQ + h * LANES + np.arange(half)] = h * HEAD_DIM + np.arange(half)
        idx[P_AQ + h * LANES + 64 + np.arange(half)] = h * HEAD_DIM + half + np.arange(half)
    for h in range(A_KV_HEADS):
        idx[P_AK + h * LANES + np.arange(half)] = 384 + h * HEAD_DIM + np.arange(half)
        idx[P_AK + h * LANES + 64 + np.arange(half)] = 384 + h * HEAD_DIM + half + np.arange(half)
        idx[P_AV + h * LANES + np.arange(HEAD_DIM)] = 512 + h * HEAD_DIM + np.arange(HEAD_DIM)
    idx[P_CQ:P_CQ + 384] = 640 + np.arange(384)
    idx[P_CKV:P_CKV + 256] = 1024 + np.arange(256)
    idx[P_KR + 48 + np.arange(16)] = 1280 + np.arange(16)
    idx[P_KR + 112 + np.arange(16)] = 1296 + np.arange(16)
    idx[P_CU:P_CU + 256] = 1312 + np.arange(256)
    idx[P_CV:P_CV + 256] = 1568 + np.arange(256)
    return idx


def _map_w_uq():
    idx = -np.ones(B_HEADS * LANES, np.int64)
    for h in range(B_HEADS):
        b = h * (B_NOPE + B_ROPE)
        idx[h * LANES + np.arange(48)] = b + np.arange(48)
        idx[h * LANES + 48 + np.arange(16)] = b + 64 + np.arange(16)
        idx[h * LANES + 64 + np.arange(16)] = b + 48 + np.arange(16)
        idx[h * LANES + 112 + np.arange(16)] = b + 80 + np.arange(16)
    return idx


def _map_w_ukv():
    idx = -np.ones(2 * B_HEADS * LANES, np.int64)
    for h in range(B_HEADS):
        b = h * (B_NOPE + B_V)
        idx[h * LANES + np.arange(48)] = b + np.arange(48)
        idx[h * LANES + 64 + np.arange(16)] = b + 48 + np.arange(16)
        idx[B_HEADS * LANES + h * LANES + np.arange(B_V)] = b + B_NOPE + np.arange(B_V)
    return idx


def _inverse(idx, n):
    inv = np.zeros(n, np.int64)
    pos = np.nonzero(idx >= 0)[0]
    inv[idx[pos]] = pos
    return inv


def _runs(idx):
    runs, i, n = [], 0, len(idx)
    while i < n:
        j = i + 1
        while j < n and ((idx[i] < 0 and idx[j] < 0) or (idx[i] >= 0 and idx[j] == idx[i] + (j - i))):
            j += 1
        runs.append((int(idx[i]), j - i))
        i = j
    return runs


def _select_axis(w, idx, axis):
    pieces = []
    for start, length in _runs(idx):
        if start < 0:
            shape = list(w.shape)
            shape[axis] = length
            pieces.append(jnp.zeros(shape, w.dtype))
        else:
            pieces.append(lax.slice_in_dim(w, start, start + length, axis=axis))
    return jnp.concatenate(pieces, axis=axis)


def _pad_axis(w, idx, axis):
    return _select_axis(w, idx, axis)


def _unpad_axis(g, idx, n, axis):
    return _select_axis(g, _inverse(idx, n), axis)


def _params(sem):
    return pltpu.CompilerParams(dimension_semantics=sem, vmem_limit_bytes=VMEM_LIMIT)


def _tile(dim, target):
    if dim <= target:
        return dim
    best = None
    for t in range(LANES, target + 1, LANES):
        if dim % t == 0:
            best = t
    assert best is not None, dim
    return best


def _row_div(rows, target):
    if rows <= target:
        return rows
    best = None
    for t in range(8, target + 1, 8):
        if rows % t == 0:
            best = t
    assert best is not None, rows
    return best


_ANY = pl.BlockSpec(memory_space=pl.ANY)
_MESH = pl.DeviceIdType.MESH


def _xchg_out_shapes(kind, srcs):
    if kind == "gather":
        return [jax.ShapeDtypeStruct((N_CHIPS,) + s.shape, s.dtype) for s in srcs]
    return [jax.ShapeDtypeStruct(s.shape, s.dtype) for s in srcs]


def _xchg_scratch(kind, n):
    per = 1 if kind == "swap" else N_CHIPS - 1
    return [pltpu.SemaphoreType.DMA((per * n,)), pltpu.SemaphoreType.DMA((per * n,)), pltpu.SemaphoreType.DMA((n,))]


def _xchg_copies(kind, srcs, dsts, send_sems, recv_sems, local_sems, arrivals):
    x, y, c = lax.axis_index("x"), lax.axis_index("y"), lax.axis_index("c")
    me = 2 * x + y
    peers = [(1 - x, y), (x, 1 - y), (1 - x, 1 - y)]
    local, out, back = [], [], []
    for i, (s, d) in enumerate(zip(srcs, dsts)):
        if kind == "swap":
            cp = pltpu.make_async_remote_copy(src_ref=s, dst_ref=d, send_sem=send_sems.at[i], recv_sem=recv_sems.at[i],
                                              device_id=(x, y, 1 - c), device_id_type=_MESH)
            out.append(cp)
            back.append(cp)
            continue
        local.append(pltpu.make_async_copy(s if kind == "gather" else s.at[me], d.at[me], local_sems.at[i]))
        for kk, (px, py) in enumerate(peers):
            j = (N_CHIPS - 1) * i + kk
            theirs = 2 * px + py
            out.append(pltpu.make_async_remote_copy(
                src_ref=s if kind == "gather" else s.at[theirs], dst_ref=d.at[me], send_sem=send_sems.at[j],
                recv_sem=recv_sems.at[j], device_id=(px, py, c), device_id_type=_MESH))
            if arrivals:
                back.append(pltpu.make_async_remote_copy(
                    src_ref=s if kind == "gather" else s.at[me], dst_ref=d.at[theirs], send_sem=send_sems.at[j],
                    recv_sem=recv_sems.at[j], device_id=(px, py, c), device_id_type=_MESH))
    return local, out, back


def _xchg_start(kind, srcs, dsts, sems):
    local, out, _ = _xchg_copies(kind, srcs, dsts, *sems, arrivals=False)
    for cp in local + out:
        cp.start()


def _xchg_wait(kind, srcs, dsts, sems):
    local, out, back = _xchg_copies(kind, srcs, dsts, *sems, arrivals=True)
    for cp in back:
        cp.wait_recv()
    for cp in out:
        cp.wait_send()
    for cp in local:
        cp.wait()


def _xchg_at_ends(kind, srcs, dsts, sems, grid, first):
    ids = [pl.program_id(a) for a in range(len(grid))]
    cond = None
    for i, n in zip(ids, grid):
        c = (i == 0) if first else (i == n - 1)
        cond = c if cond is None else jnp.logical_and(cond, c)

    @pl.when(cond)
    def _():
        (_xchg_start if first else _xchg_wait)(kind, srcs, dsts, sems)


def _xchg_call(kind, srcs, name):
    n = len(srcs)

    def kern(*refs):
        s, d, sems = refs[:n], refs[n:2 * n], refs[2 * n:]
        _xchg_start(kind, s, d, sems)
        _xchg_wait(kind, s, d, sems)

    return pl.pallas_call(
        kern, name=name, in_specs=[_ANY] * n, out_specs=[_ANY] * n, out_shape=_xchg_out_shapes(kind, srcs),
        scratch_shapes=_xchg_scratch(kind, n),
    )(*srcs)


_DIMS = {"nn": (((1,), (0,)), ((), ())), "nt": (((1,), (1,)), ((), ())), "tn": (((0,), (0,)), ((), ()))}


def _matmul(a, b, mode, out_dtype, name, *, b_chunks=False, out_chunks=False, xchg=None):
    if b_chunks:
        nchunk, brows, bcols = b.shape
        bshape = (brows, nchunk * bcols)
    else:
        bshape = b.shape
    if mode == "nn":
        (m, k), (_, n) = a.shape, bshape
    elif mode == "nt":
        (m, k), (n, _) = a.shape, bshape
    else:
        (k, m), (_, n) = a.shape, bshape
    tm, tk = (1408, 1024) if mode == "tn" else (1024, 1408)
    tm, tn, tk = _tile(m, tm), _tile(n, 1408), _tile(k, tk)
    if b_chunks:
        if mode == "nn":
            tn = bcols
        else:
            assert mode == "nt"
            tk = bcols
    if out_chunks:
        assert n % N_CHIPS == 0
        tn = n // N_CHIPS
    ni, nj, nk = m // tm, n // tn, k // tk
    dims = _DIMS[mode]
    n_x = 0 if xchg is None else len(xchg[1])

    def kern(*refs):
        a_ref, b_ref = refs[0], refs[1]
        xs = refs[2:2 + n_x]
        o_ref = refs[2 + n_x]
        xd = refs[3 + n_x:3 + 2 * n_x]
        acc_ref = refs[3 + 2 * n_x]
        sems = refs[4 + 2 * n_x:]
        kk = pl.program_id(2)
        if n_x:
            _xchg_at_ends(xchg[0], xs, xd, sems, (ni, nj, nk), True)

        @pl.when(kk == 0)
        def _():
            acc_ref[...] = jnp.zeros_like(acc_ref)

        acc_ref[...] += lax.dot_general(a_ref[...], b_ref[...], dims, preferred_element_type=F32)

        @pl.when(kk == nk - 1)
        def _():
            o_ref[...] = acc_ref[...].astype(o_ref.dtype)

        if n_x:
            _xchg_at_ends(xchg[0], xs, xd, sems, (ni, nj, nk), False)

    if mode == "tn":
        a_spec = pl.BlockSpec((tk, tm), lambda i, j, kk: (kk, i))
    else:
        a_spec = pl.BlockSpec((tm, tk), lambda i, j, kk: (i, kk))
    if b_chunks and mode == "nn":
        b_spec = pl.BlockSpec((None, tk, tn), lambda i, j, kk: (j, kk, 0))
    elif b_chunks:
        b_spec = pl.BlockSpec((None, tn, tk), lambda i, j, kk: (kk, j, 0))
    elif mode == "nt":
        b_spec = pl.BlockSpec((tn, tk), lambda i, j, kk: (j, kk))
    else:
        b_spec = pl.BlockSpec((tk, tn), lambda i, j, kk: (kk, j))
    if out_chunks:
        o_spec = pl.BlockSpec((None, tm, tn), lambda i, j, kk: (j, i, 0))
        o_shape = jax.ShapeDtypeStruct((N_CHIPS, m, tn), out_dtype)
    else:
        o_spec = pl.BlockSpec((tm, tn), lambda i, j, kk: (i, j))
        o_shape = jax.ShapeDtypeStruct((m, n), out_dtype)
    xs = [] if xchg is None else list(xchg[1])
    res = pl.pallas_call(
        kern, name=name, grid=(ni, nj, nk),
        in_specs=[a_spec, b_spec] + [_ANY] * n_x, out_specs=[o_spec] + [_ANY] * n_x,
        out_shape=[o_shape] + (_xchg_out_shapes(xchg[0], xs) if n_x else []),
        scratch_shapes=[pltpu.VMEM((tm, tn), F32)] + (_xchg_scratch(xchg[0], n_x) if n_x else []),
        compiler_params=_params(("arbitrary", "arbitrary", "arbitrary") if n_x else ("parallel", "parallel", "arbitrary")),
    )(a, b, *xs)
    return (res[0], list(res[1:])) if n_x else res[0]


def _rowcall(body, *, name, T, S, tr, rows, exs=(), pars=(), row_outs=(), ex_outs=(), par_outs=(), aliases=None):
    assert S % tr == 0 and T % S == 0
    per_ex = S // tr
    nb = T // S
    n_rows, n_exs, n_pars = len(rows), len(exs), len(pars)
    n_ro, n_eo, n_po = len(row_outs), len(ex_outs), len(par_outs)

    def kern(*refs):
        ins = refs[:n_rows + n_exs + n_pars]
        outs = refs[n_rows + n_exs + n_pars:]
        rv = [r[...].astype(F32) for r in ins[:n_rows]]
        ev = [r[0] for r in ins[n_rows:n_rows + n_exs]]
        pv = [r[...] for r in ins[n_rows + n_exs:]]
        ro, eo, po = body(rv, ev, pv)
        i = pl.program_id(0)
        for ref, val in zip(outs[:n_ro], ro):
            if isinstance(val, (list, tuple)):
                off = 0
                for piece in val:
                    w = piece.shape[-1]
                    ref[:, off:off + w] = piece.astype(ref.dtype)
                    off += w
            else:
                ref[...] = val.astype(ref.dtype)
        first_of_ex = (i % per_ex) == 0
        for ref, val in zip(outs[n_ro:n_ro + n_eo], eo):
            @pl.when(first_of_ex)
            def _(ref=ref, val=val):
                ref[0] = val

            @pl.when(jnp.logical_not(first_of_ex))
            def _(ref=ref, val=val):
                ref[0] += val
        for ref, val in zip(outs[n_ro + n_eo:], po):
            @pl.when(i == 0)
            def _(ref=ref, val=val):
                ref[...] = val

            @pl.when(i != 0)
            def _(ref=ref, val=val):
                ref[...] += val

    in_specs = [pl.BlockSpec((tr, w), functools.partial(lambda i, cb: (i, cb), cb=cb)) for (_, w, cb) in rows]
    in_specs += [pl.BlockSpec((1, 1, e.shape[-1]), lambda i: (i // per_ex, 0, 0)) for e in exs]
    in_specs += [pl.BlockSpec(p.shape, functools.partial(lambda i, nd: (0,) * nd, nd=p.ndim)) for p in pars]
    out_specs = [pl.BlockSpec((tr, w), functools.partial(lambda i, cb: (i, cb), cb=cb)) for (_, _, w, cb) in row_outs]
    out_specs += [pl.BlockSpec((1, 1, f), lambda i: (i // per_ex, 0, 0)) for f in ex_outs]
    out_specs += [pl.BlockSpec(tuple(s), functools.partial(lambda i, nd: (0,) * nd, nd=len(s))) for s in par_outs]
    out_shape = [jax.ShapeDtypeStruct((T, tw), dt) for (tw, dt, _, _) in row_outs]
    out_shape += [jax.ShapeDtypeStruct((nb, 1, f), F32) for f in ex_outs]
    out_shape += [jax.ShapeDtypeStruct(tuple(s), F32) for s in par_outs]
    res = pl.pallas_call(
        kern, name=name, grid=(T // tr,), in_specs=in_specs, out_specs=out_specs, out_shape=out_shape,
        input_output_aliases=aliases or {}, compiler_params=_params(("arbitrary",)),
    )(*[r[0] for r in rows], *exs, *pars)
    return res[:n_ro], res[n_ro:n_ro + n_eo], res[n_ro + n_eo:]


def _rms(x, w, n=None):
    n = x.shape[-1] if n is None else n
    ms = jnp.sum(x * x, axis=-1, keepdims=True) * (1.0 / n)
    return x * lax.rsqrt(ms + NORM_EPS) * w


def _gelu(x):
    return 0.5 * x * (1.0 + lax.erf(x * np.float32(1.0 / math.sqrt(2.0))))


def _silu(x):
    return x * jax.nn.sigmoid(x)


@jax.custom_vjp
def _rope(x, cos, sin):
    return x * cos + pltpu.roll(x, 64, 1) * sin


def _rope_fwd(x, cos, sin):
    return _rope(x, cos, sin), (cos, sin)


def _rope_bwd(res, dy):
    cos, sin = res
    return dy * cos + pltpu.roll(dy * sin, 64, 1), None, None


_rope.defvjp(_rope_fwd, _rope_bwd)


def _heads(x, n):
    return [x[:, h * LANES:(h + 1) * LANES] for h in range(n)]


def _f_norm_mod(x, w, sc, sh):
    return _rms(x, w) * (1.0 + sc) + sh


def _f_resid_norm_mod(xa, delta, g, w, sc, sh):
    xn = xa + g * delta
    return xn, _f_norm_mod(xn, w, sc, sh)


def _f_mixprep(proj, ca, sa, cb, sb, qnw, kvnw):
    qa = [_rope(p, ca, sa) for p in _heads(proj[:, P_AQ:P_AK], A_Q_HEADS)]
    ka = [_rope(p, ca, sa) for p in _heads(proj[:, P_AK:P_AV], A_KV_HEADS)]
    va = proj[:, P_AV:P_CQ]
    cqn = _rms(proj[:, P_CQ:P_CKV], qnw)
    ckvn = _rms(proj[:, P_CKV:P_KR], kvnw)
    kr = _rope(proj[:, P_KR:P_CU], cb, sb)
    return jnp.concatenate(qa, -1), jnp.concatenate(ka, -1), va, cqn, ckvn, kr


def _f_mlaprep(q, kv, kr, cb, sb):
    qs = [_rope(p, cb, sb) for p in _heads(q, B_HEADS)]
    ks = [p + kr for p in _heads(kv[:, :B_HEADS * LANES], B_HEADS)]
    return jnp.concatenate(qs, -1), jnp.concatenate(ks, -1), kv[:, B_HEADS * LANES:]


def _f_sgu(cu, cv, ln_w, ln_b, w_s, b_col):
    u = _gelu(cu)
    v = _gelu(cv)
    mu = jnp.mean(v, axis=-1, keepdims=True)
    var = jnp.mean(jnp.square(v - mu), axis=-1, keepdims=True)
    vn = (v - mu) * lax.rsqrt(var + NORM_EPS) * ln_w + ln_b
    r = lax.broadcasted_iota(jnp.int32, (C_CHUNK, C_CHUNK), 0)
    c = lax.broadcasted_iota(jnp.int32, (C_CHUNK, C_CHUNK), 1)
    lane = lax.broadcasted_iota(jnp.int32, (1, C_WIDTH), 1)
    mixed = jnp.zeros(cu.shape, F32)
    for g in range(C_GROUPS):
        gm = (lane // C_GROUP_DIM == g).astype(F32)
        wg = jnp.where(r >= c, w_s[g], 0.0).astype(BF16)
        mixed = mixed + jnp.dot(wg, (vn * gm).astype(BF16), preferred_element_type=F32) + b_col[g] * gm
    return u * mixed


@jax.custom_vjp
def _pack_pairs(x):
    heads = _heads(x, x.shape[-1] // LANES)
    return jnp.concatenate([heads[i] + pltpu.roll(heads[i + 1], 64, 1) for i in range(0, len(heads), 2)], -1)


def _pack_pairs_fwd(x):
    return _pack_pairs(x), None


def _pack_pairs_bwd(_, dy):
    out = []
    for p in _heads(dy, dy.shape[-1] // LANES):
        out += [p, pltpu.roll(p, 64, 1)]
    return (jnp.concatenate(out, -1),)


_pack_pairs.defvjp(_pack_pairs_fwd, _pack_pairs_bwd)


def _f_outnorm(oa, ob, yc, gw):
    na, nb = A_Q_HEADS * HEAD_DIM, B_HEADS * B_V
    ya = _rms(_pack_pairs(oa), gw[:, :na])
    yb = _rms(_pack_pairs(ob), gw[:, na:na + nb])
    ycn = _rms(yc, gw[:, na + nb:])
    return jnp.concatenate([ya, yb, ycn], -1)


def _f_swiglu(gate, up):
    return _silu(gate) * up


def _mask(q_start, k_start, tq, tk, window):
    qpos = q_start + lax.broadcasted_iota(jnp.int32, (tq, tk), 0)
    kpos = k_start + lax.broadcasted_iota(jnp.int32, (tq, tk), 1)
    m = kpos <= qpos
    if window is not None:
        m = jnp.logical_and(m, qpos - kpos < window)
    return m


def _tile_fwd(qv, kk, vv, q_start, k_start, n_free, scale, window, m0, l0):
    tq = qv.shape[0]
    W = kk.shape[0]
    parts = []
    if n_free > 0:
        parts.append((lax.dot_general(qv, kk[:n_free], _DIMS["nt"], preferred_element_type=F32) * scale, vv[:n_free]))
    if W > n_free:
        s = lax.dot_general(qv, kk[n_free:], _DIMS["nt"], preferred_element_type=F32) * scale
        s = jnp.where(_mask(q_start, k_start + n_free, tq, W - n_free, window), s, NEG_INF)
        parts.append((s, vv[n_free:]))
    m = m0
    for s, _ in parts:
        mx = jnp.max(s, axis=-1, keepdims=True)
        m = mx if m is None else jnp.maximum(m, mx)
    l = None if l0 is None else l0 * jnp.exp(m0 - m)
    o = None
    for s, vpart in parts:
        p = jnp.exp(s - m)
        ps = jnp.sum(p, axis=-1, keepdims=True)
        l = ps if l is None else l + ps
        po = jnp.dot(p.astype(BF16), vpart, preferred_element_type=F32)
        o = po if o is None else o + po
    return o / l, m + jnp.log(l)


def _tile_bwd(qv, kk, vv, dof, ov, lse, q_start, k_start, n_free, scale, window):
    tq = qv.shape[0]
    W = kk.shape[0]
    dob = dof.astype(BF16)
    delta = jnp.sum(dof * ov, axis=-1, keepdims=True)
    dq = None
    outs = []
    for (a, b, masked) in ((0, n_free, False), (n_free, W, True)):
        if b <= a:
            continue
        kp, vp = kk[a:b], vv[a:b]
        s = lax.dot_general(qv, kp, _DIMS["nt"], preferred_element_type=F32) * scale
        if masked:
            s = jnp.where(_mask(q_start, k_start + a, tq, b - a, window), s, NEG_INF)
        p = jnp.exp(s - lse)
        dp = lax.dot_general(dob, vp, _DIMS["nt"], preferred_element_type=F32)
        ds = (p * (dp - delta) * scale).astype(BF16)
        d = jnp.dot(ds, kp, preferred_element_type=F32)
        dq = d if dq is None else dq + d
        dkp = lax.dot_general(ds, qv, _DIMS["tn"], preferred_element_type=F32)
        dvp = lax.dot_general(p.astype(BF16), dob, _DIMS["tn"], preferred_element_type=F32)
        outs.append((a, dkp, dvp))
    return dq, outs


def _attn_fwd(q, k, v, sinks, *, B, S, HQ, HK, window, scale, tq, band, name, xchg=None):
    G = HQ // HK
    nq = S // tq
    T = B * S
    has_sink = sinks is not None
    n_x = 0 if xchg is None else len(xchg[1])
    n_in = 4 if has_sink else 3

    def kern(*refs):
        xs, xd, sems = refs[n_in:n_in + n_x], refs[n_in + n_x + 2:n_in + 2 * n_x + 2], refs[n_in + 2 * n_x + 2:]
        refs = refs[:n_in] + refs[n_in + n_x:n_in + n_x + 2]
        if n_x:
            _xchg_at_ends(xchg[0], xs, xd, sems, (B, HQ, nq), True)
        if has_sink:
            q_ref, k_ref, v_ref, s_ref, o_ref, lse_ref = refs
        else:
            q_ref, k_ref, v_ref, o_ref, lse_ref = refs
        q_start = pl.program_id(2) * tq
        qv = q_ref[...]
        if has_sink:
            m0 = jnp.broadcast_to(s_ref[0][:, :1], (tq, 1))
            l0 = jnp.ones((tq, 1), F32)
        else:
            m0 = l0 = None

        def finish(o, lse):
            o_ref[...] = o.astype(o_ref.dtype)
            lse_ref[...] = jnp.broadcast_to(lse, (tq, LANES))

        if window is None:
            bidx = q_start // band
            for bb in range(S // band):
                @pl.when(bidx == bb)
                def _(bb=bb):
                    W = (bb + 1) * band
                    finish(*_tile_fwd(qv, k_ref[0:W, :], v_ref[0:W, :], q_start, 0, bb * band, scale, None, m0, l0))
        else:
            sub = min(tq, WINDOW_SUB)
            W = min(S, sub + window)
            for r in range(0, tq, sub):
                k_start = pl.multiple_of(jnp.maximum(q_start + r - window, 0), window)
                o, lse = _tile_fwd(qv[r:r + sub], k_ref[pl.ds(k_start, W), :], v_ref[pl.ds(k_start, W), :], q_start + r,
                                   k_start, 0, scale, window, None if m0 is None else m0[:sub], None if l0 is None else l0[:sub])
                o_ref[r:r + sub, :] = o.astype(o_ref.dtype)
                lse_ref[r:r + sub, :] = jnp.broadcast_to(lse, (sub, LANES))
        if n_x:
            _xchg_at_ends(xchg[0], xs, xd, sems, (B, HQ, nq), False)

    q_spec = pl.BlockSpec((tq, LANES), lambda b, h, i: (b * nq + i, h))
    kv_spec = pl.BlockSpec((S, LANES), lambda b, h, i: (b, h // G))
    in_specs = [q_spec, kv_spec, kv_spec]
    args = [q, k, v]
    if has_sink:
        in_specs.append(pl.BlockSpec((1, 1, LANES), lambda b, h, i: (h, 0, 0)))
        args.append(sinks)
    xs = [] if xchg is None else list(xchg[1])
    res = pl.pallas_call(
        kern, name=name, grid=(B, HQ, nq), in_specs=in_specs + [_ANY] * n_x, out_specs=[q_spec, q_spec] + [_ANY] * n_x,
        out_shape=[jax.ShapeDtypeStruct((T, HQ * LANES), BF16), jax.ShapeDtypeStruct((T, HQ * LANES), F32)]
        + (_xchg_out_shapes(xchg[0], xs) if n_x else []),
        scratch_shapes=_xchg_scratch(xchg[0], n_x) if n_x else [],
        compiler_params=_params(("arbitrary",) * 3 if n_x else ("parallel", "parallel", "arbitrary")),
    )(*args, *xs)
    return ((res[0], res[1]), list(res[2:])) if n_x else res


def _attn_bwd(q, k, v, o, lse, do, sinks, *, B, S, HQ, HK, window, scale, tq, band, name, xchg=None):
    G = HQ // HK
    nq = S // tq
    T = B * S
    has_sink = sinks is not None
    n_x = 0 if xchg is None else len(xchg[1])
    n_in, n_out = (7, 4) if has_sink else (6, 3)

    def kern(*refs):
        xs, xd = refs[n_in:n_in + n_x], refs[n_in + n_x + n_out:n_in + 2 * n_x + n_out]
        sems = refs[n_in + 2 * n_x + n_out:]
        refs = refs[:n_in] + refs[n_in + n_x:n_in + n_x + n_out]
        if n_x:
            _xchg_at_ends(xchg[0], xs, xd, sems, (B, HK, G, nq), True)
        if has_sink:
            q_ref, k_ref, v_ref, o_ref, lse_ref, do_ref, s_ref, dq_ref, dk_ref, dv_ref, ds_ref = refs
        else:
            q_ref, k_ref, v_ref, o_ref, lse_ref, do_ref, dq_ref, dk_ref, dv_ref = refs
        gi = pl.program_id(2)
        qi = pl.program_id(3)
        q_start = qi * tq

        @pl.when(jnp.logical_and(gi == 0, qi == 0))
        def _():
            dk_ref[...] = jnp.zeros_like(dk_ref)
            dv_ref[...] = jnp.zeros_like(dv_ref)

        qv = q_ref[...]
        dof = do_ref[...].astype(F32)
        ov = o_ref[...].astype(F32)
        lse_v = lse_ref[...][:, :1]
        if window is None:
            bidx = q_start // band
            for bb in range(S // band):
                @pl.when(bidx == bb)
                def _(bb=bb):
                    W = (bb + 1) * band
                    dq, outs = _tile_bwd(qv, k_ref[0:W, :], v_ref[0:W, :], dof, ov, lse_v, q_start, 0, bb * band, scale, None)
                    dq_ref[...] = dq.astype(dq_ref.dtype)
                    for a, dkp, dvp in outs:
                        dk_ref[a:a + dkp.shape[0], :] += dkp
                        dv_ref[a:a + dvp.shape[0], :] += dvp
        else:
            sub = min(tq, WINDOW_SUB)
            W = min(S, sub + window)
            parts = []
            for r in range(0, tq, sub):
                k_start = pl.multiple_of(jnp.maximum(q_start + r - window, 0), window)
                dq, outs = _tile_bwd(qv[r:r + sub], k_ref[pl.ds(k_start, W), :], v_ref[pl.ds(k_start, W), :],
                                     dof[r:r + sub], ov[r:r + sub], lse_v[r:r + sub], q_start + r, k_start, 0, scale, window)
                dq_ref[r:r + sub, :] = dq.astype(dq_ref.dtype)
                parts.append((k_start, outs[0][1], outs[0][2]))
            for k_start, dkp, dvp in parts:
                dk_ref[pl.ds(k_start, W), :] += dkp
                dv_ref[pl.ds(k_start, W), :] += dvp
        if has_sink:
            delta = jnp.sum(dof * ov, axis=-1, keepdims=True)
            sink = s_ref[0][:, :1]
            part = -jnp.sum(jnp.exp(sink - lse_v) * delta, axis=0, keepdims=True)
            part = jnp.broadcast_to(part, (1, LANES))

            @pl.when(qi == 0)
            def _():
                ds_ref[0] = part

            @pl.when(qi != 0)
            def _():
                ds_ref[0] += part
        if n_x:
            _xchg_at_ends(xchg[0], xs, xd, sems, (B, HK, G, nq), False)

    q_spec = pl.BlockSpec((tq, LANES), lambda b, hk, g, i: (b * nq + i, hk * G + g))
    kv_spec = pl.BlockSpec((S, LANES), lambda b, hk, g, i: (b, hk))
    in_specs = [q_spec, kv_spec, kv_spec, q_spec, q_spec, q_spec]
    args = [q, k, v, o, lse, do]
    out_specs = [q_spec, kv_spec, kv_spec]
    out_shape = [jax.ShapeDtypeStruct((T, HQ * LANES), BF16), jax.ShapeDtypeStruct((T, HK * LANES), F32),
                 jax.ShapeDtypeStruct((T, HK * LANES), F32)]
    if has_sink:
        in_specs.append(pl.BlockSpec((1, 1, LANES), lambda b, hk, g, i: (hk * G + g, 0, 0)))
        args.append(sinks)
        out_specs.append(pl.BlockSpec((1, 1, LANES), lambda b, hk, g, i: (b * HQ + hk * G + g, 0, 0)))
        out_shape.append(jax.ShapeDtypeStruct((B * HQ, 1, LANES), F32))
    xs = [] if xchg is None else list(xchg[1])
    res = pl.pallas_call(
        kern, name=name, grid=(B, HK, G, nq), in_specs=in_specs + [_ANY] * n_x, out_specs=out_specs + [_ANY] * n_x,
        out_shape=out_shape + (_xchg_out_shapes(xchg[0], xs) if n_x else []),
        scratch_shapes=_xchg_scratch(xchg[0], n_x) if n_x else [],
        compiler_params=_params(("arbitrary",) * 4 if n_x else ("parallel", "parallel", "arbitrary", "arbitrary")),
    )(*args, *xs)
    main = tuple(res[:n_out]) if has_sink else (*res[:n_out], None)
    return (main, list(res[n_out:])) if n_x else main


def _rope_tables(pos_col, freq, sign, name):
    T = pos_col.shape[0]
    tr = _tile(T, 1024)

    def kern(p_ref, f_ref, s_ref, c_out, s_out):
        ang = p_ref[...] * f_ref[...]
        c_out[...] = jnp.cos(ang)
        s_out[...] = jnp.sin(ang) * s_ref[...]

    spec = pl.BlockSpec((tr, LANES), lambda i: (i, 0))
    par = pl.BlockSpec((1, LANES), lambda i: (0, 0))
    return pl.pallas_call(
        kern, name=name, grid=(T // tr,), in_specs=[pl.BlockSpec((tr, 1), lambda i: (i, 0)), par, par],
        out_specs=[spec, spec], out_shape=[jax.ShapeDtypeStruct((T, LANES), F32)] * 2,
        compiler_params=_params(("parallel",)),
    )(pos_col, freq, sign)


def _ada_fwd(c8, ada_w, ada_b):
    nch, D, tn = ada_w.shape

    def kern(c_ref, w_ref, b_ref, o_ref):
        act = _silu(c_ref[...]).astype(BF16)
        o_ref[...] = jnp.dot(act, w_ref[...], preferred_element_type=F32) + b_ref[...]

    return pl.pallas_call(
        kern, name="ada_fwd", grid=(nch,),
        in_specs=[pl.BlockSpec((8, D), lambda j: (0, 0)), pl.BlockSpec((None, D, tn), lambda j: (j, 0, 0)),
                  pl.BlockSpec((1, tn), lambda j: (0, j))],
        out_specs=pl.BlockSpec((8, tn), lambda j: (0, j)),
        out_shape=jax.ShapeDtypeStruct((8, nch * tn), F32), compiler_params=_params(("parallel",)),
    )(c8, ada_w, ada_b)


def _ada_bwd(c8, dmod):
    _, N = dmod.shape
    D = c8.shape[1]
    tn = N // N_CHIPS

    def kern(c_ref, d_ref, gw_ref, gb_ref):
        act = _silu(c_ref[...]).astype(BF16)
        d = d_ref[...]
        gw_ref[...] = lax.dot_general(act, d.astype(BF16), _DIMS["tn"], preferred_element_type=F32).astype(BF16)
        gb_ref[...] = jnp.sum(d, axis=0, keepdims=True)

    return pl.pallas_call(
        kern, name="ada_bwd", grid=(N_CHIPS,),
        in_specs=[pl.BlockSpec((8, D), lambda j: (0, 0)), pl.BlockSpec((8, tn), lambda j: (0, j))],
        out_specs=[pl.BlockSpec((None, D, tn), lambda j: (j, 0, 0)), pl.BlockSpec((1, tn), lambda j: (0, j))],
        out_shape=[jax.ShapeDtypeStruct((N_CHIPS, D, tn), BF16), jax.ShapeDtypeStruct((1, N), F32)],
        compiler_params=_params(("parallel",)),
    )(c8, dmod)


def _adamw(w, ga, gb, m, v, name):
    rows, cols = w.shape
    tr = _row_div(rows, 256)
    two = gb is not None

    def kern(*refs):
        if two:
            w_ref, ga_ref, gb_ref, m_ref, v_ref, g_out, d_out, m_out, v_out = refs
            gv = ga_ref[...] + gb_ref[...]
        else:
            w_ref, ga_ref, m_ref, v_ref, g_out, d_out, m_out, v_out = refs
            gv = ga_ref[...]
        mn = ADAM_B1 * m_ref[...] + (1.0 - ADAM_B1) * gv
        vn = ADAM_B2 * v_ref[...] + (1.0 - ADAM_B2) * jnp.square(gv)
        m_hat = mn / (1.0 - ADAM_B1 ** ADAM_STEP)
        v_hat = vn / (1.0 - ADAM_B2 ** ADAM_STEP)
        g_out[...] = gv
        d_out[...] = -ADAM_LR * (m_hat / (jnp.sqrt(v_hat) + ADAM_EPS) + ADAM_WD * w_ref[...])
        m_out[...] = mn
        v_out[...] = vn

    spec = pl.BlockSpec((tr, cols), lambda i: (i, 0))
    args = [w, ga, gb, m, v] if two else [w, ga, m, v]
    return pl.pallas_call(
        kern, name=name, grid=(rows // tr,), in_specs=[spec] * len(args), out_specs=[spec] * 4,
        out_shape=[jax.ShapeDtypeStruct((rows, cols), F32)] * 4, compiler_params=_params(("parallel",)),
    )(*args)


def _sum_slots(x, name):
    n, rows, cols = x.shape
    tr = _row_div(rows, 256)

    def kern(x_ref, o_ref):
        acc = x_ref[0].astype(F32)
        for j in range(1, n):
            acc = acc + x_ref[j].astype(F32)
        o_ref[...] = acc

    return pl.pallas_call(
        kern, name=name, grid=(rows // tr,), in_specs=[pl.BlockSpec((n, tr, cols), lambda i: (0, i, 0))],
        out_specs=pl.BlockSpec((tr, cols), lambda i: (i, 0)), out_shape=jax.ShapeDtypeStruct((rows, cols), F32),
        compiler_params=_params(("parallel",)),
    )(x)


def _all_reduce_small(blob, name):
    R, C = blob.shape

    def kern(src, out, pair, chips, send_sems, recv_sems):
        x, y, c = lax.axis_index("x"), lax.axis_index("y"), lax.axis_index("c")
        me = 2 * x + y
        to_sibling = pltpu.make_async_remote_copy(
            src_ref=src, dst_ref=pair, send_sem=send_sems.at[0], recv_sem=recv_sems.at[0],
            device_id=(x, y, 1 - c), device_id_type=_MESH)
        to_sibling.start()
        to_sibling.wait()
        chips[me] = src[...] + pair[...]
        peers = [(1 - x, y), (x, 1 - y), (1 - x, 1 - y)]
        copies = [pltpu.make_async_remote_copy(
            src_ref=chips.at[me], dst_ref=chips.at[me], send_sem=send_sems.at[1 + kk], recv_sem=recv_sems.at[1 + kk],
            device_id=(px, py, c), device_id_type=_MESH) for kk, (px, py) in enumerate(peers)]
        for cp in copies:
            cp.start()
        for kk, (px, py) in enumerate(peers):
            pltpu.make_async_remote_copy(
                src_ref=chips.at[me], dst_ref=chips.at[2 * px + py], send_sem=send_sems.at[1 + kk],
                recv_sem=recv_sems.at[1 + kk], device_id=(px, py, c), device_id_type=_MESH).wait_recv()
        for cp in copies:
            cp.wait_send()
        acc = chips[0]
        for j in range(1, N_CHIPS):
            acc = acc + chips[j]
        out[...] = acc

    vm = pl.BlockSpec(memory_space=pltpu.VMEM)
    return pl.pallas_call(
        kern, name=name, in_specs=[vm], out_specs=vm, out_shape=jax.ShapeDtypeStruct((R, C), F32),
        scratch_shapes=[pltpu.VMEM((R, C), F32), pltpu.VMEM((N_CHIPS, R, C), F32), pltpu.SemaphoreType.DMA((4,)),
                        pltpu.SemaphoreType.DMA((4,))],
        compiler_params=pltpu.CompilerParams(vmem_limit_bytes=VMEM_LIMIT),
    )(blob)


def _row_tile(S, wide=False):
    for t in ((256,) if wide else (512, 256)):
        if S % t == 0:
            return t
    return 128


def _attn_tiles(S):
    return min(S, 512), min(S, 256), min(S, 512)


def _hosted(fn, host, got, kind, name, *args, **kw):
    if host is not None and host.get(name):
        res, xs = fn(*args, name=name, xchg=(kind, host[name]), **kw)
        got[name] = xs
        return res
    return fn(*args, name=name, **kw)


def _mm(host, got, kind, a, b, mode, out_dtype, name, **kw):
    return _hosted(_matmul, host, got, kind, name, a, b, mode, out_dtype, **kw)


def _sgu_rows(S):
    return 4 * C_CHUNK if S % (4 * C_CHUNK) == 0 else C_CHUNK


def _layer_fwd(xin, prev, mods, w, tabs, B, S, host=None, late=None):
    T = B * S
    tr = _row_tile(S)
    sh1, sc1, g1, sh2, sc2, g2 = mods
    ca, sa, cb, sb = tabs
    sv = {}
    got = {}
    if prev is None:
        def body(rv, ev, pv):
            return [_f_norm_mod(rv[0], pv[0], ev[0], ev[1])], [], []
        (h,), _, _ = _rowcall(body, name="f_norm_mod", T=T, S=S, tr=tr, rows=[(xin, D_MODEL, 0)], exs=[sc1, sh1],
                              pars=[w["norm1_w"]], row_outs=[(D_MODEL, BF16, D_MODEL, 0)])
        x = xin
    else:
        x1p, dp, g2p = prev

        def body(rv, ev, pv):
            xn, hh = _f_resid_norm_mod(rv[0], rv[1], ev[0], pv[0], ev[1], ev[2])
            return [xn, hh], [], []
        (x, h), _, _ = _rowcall(body, name="f_resid_norm_mod1", T=T, S=S, tr=tr,
                                rows=[(x1p, D_MODEL, 0), (dp, D_MODEL, 0)], exs=[g2p, sc1, sh1], pars=[w["norm1_w"]],
                                row_outs=[(D_MODEL, F32, D_MODEL, 0), (D_MODEL, BF16, D_MODEL, 0)])
    sv["x"], sv["h"] = x, h
    proj = _mm(host, got, "gather", h, w["w_in"], "nn", BF16, "mm_in")
    sv["proj"] = proj

    def body(rv, ev, pv):
        outs = _f_mixprep(rv[0], rv[1], rv[2], rv[3], rv[4], pv[0], pv[1])
        return list(outs), [], []
    (qa, ka, va, cqn, ckvn, kr), _, _ = _rowcall(
        body, name="f_mixprep", T=T, S=S, tr=tr,
        rows=[(proj, P_CU, 0), (ca, LANES, 0), (sa, LANES, 0), (cb, LANES, 0), (sb, LANES, 0)],
        pars=[w["b_q_norm_w"], w["b_kv_norm_w"]],
        row_outs=[(768, BF16, 768, 0), (256, BF16, 256, 0), (256, BF16, 256, 0), (384, BF16, 384, 0),
                  (256, BF16, 256, 0), (LANES, F32, LANES, 0)])
    sv.update(qa=qa, ka=ka, va=va, cqn=cqn, ckvn=ckvn)
    q = _matmul(cqn, w["b_w_uq"], "nn", BF16, "mm_uq")
    kv = _matmul(ckvn, w["b_w_ukv"], "nn", BF16, "mm_ukv")

    def body(rv, ev, pv):
        Q, K, V = _f_mlaprep(rv[0], rv[1], rv[2], rv[3], rv[4])
        return [Q, K, V], [], []
    (Q, K, V), _, _ = _rowcall(
        body, name="f_mlaprep", T=T, S=S, tr=tr,
        rows=[(q, 768, 0), (kv, 1536, 0), (kr, LANES, 0), (cb, LANES, 0), (sb, LANES, 0)],
        row_outs=[(768, BF16, 768, 0)] * 3)
    sv.update(Q=Q, K=K, V=V)
    ta, tb, band = _attn_tiles(S)
    oa, lse_a = _hosted(_attn_fwd, host, got, "gather", "attn_a_fwd", qa, ka, va, w["sinks"], B=B, S=S, HQ=A_Q_HEADS,
                        HK=A_KV_HEADS, window=A_WINDOW, scale=HEAD_DIM ** -0.5, tq=ta, band=None)
    ob, lse_b = _hosted(_attn_fwd, host, got, "gather", "attn_b_fwd", Q, K, V, None, B=B, S=S, HQ=B_HEADS, HK=B_HEADS,
                        window=None, scale=(B_NOPE + B_ROPE) ** -0.5, tq=tb, band=band)
    sv.update(oa=oa, lse_a=lse_a, ob=ob, lse_b=lse_b)
    if late is not None:
        w = {**w, **late(got)}
    ts = _sgu_rows(S)

    def body(rv, ev, pv):
        outs = [_f_sgu(rv[0][r:r + C_CHUNK], rv[1][r:r + C_CHUNK], pv[0], pv[1], pv[2], pv[3])
                for r in range(0, ts, C_CHUNK)]
        return [jnp.concatenate(outs, axis=0)], [], []
    (yc,), _, _ = _rowcall(body, name="f_sgu", T=T, S=S, tr=ts,
                           rows=[(proj, C_WIDTH, P_CU // C_WIDTH), (proj, C_WIDTH, P_CV // C_WIDTH)],
                           pars=[w["c_ln_w"], w["c_ln_b"], w["c_w_s"], w["c_b_col"]],
                           row_outs=[(C_WIDTH, F32, C_WIDTH, 0)])
    sv["yc"] = yc

    def body(rv, ev, pv):
        return [_f_outnorm(rv[0], rv[1], rv[2], pv[0])], [], []
    (y,), _, _ = _rowcall(body, name="f_outnorm", T=T, S=S, tr=tr,
                          rows=[(oa, 768, 0), (ob, 768, 0), (yc, C_WIDTH, 0)], pars=[w["out_norm_w"]],
                          row_outs=[(D_MODEL, BF16, D_MODEL, 0)])
    sv["y"] = y
    o = _mm(host, got, "gather", y, w["w_out"], "nn", F32, "mm_out")
    sv["o"] = o

    def body(rv, ev, pv):
        xn, hh = _f_resid_norm_mod(rv[0], rv[1], ev[0], pv[0], ev[1], ev[2])
        return [xn, hh], [], []
    (x1, h2), _, _ = _rowcall(body, name="f_resid_norm_mod2", T=T, S=S, tr=tr,
                              rows=[(x, D_MODEL, 0), (o, D_MODEL, 0)], exs=[g1, sc2, sh2], pars=[w["norm2_w"]],
                              row_outs=[(D_MODEL, F32, D_MODEL, 0), (D_MODEL, BF16, D_MODEL, 0)])
    sv["h2"] = h2
    gu = _mm(host, got, "gather", h2, w["w_gate_up"], "nn", BF16, "mm_gate_up", b_chunks=True)
    sv["gu"] = gu

    def body(rv, ev, pv):
        return [_f_swiglu(rv[0], rv[1])], [], []
    (act,), _, _ = _rowcall(body, name="f_swiglu", T=T, S=S, tr=_row_tile(S, wide=True),
                            rows=[(gu, FFN_HIDDEN, 0), (gu, FFN_HIDDEN, 1)], row_outs=[(FFN_HIDDEN, BF16, FFN_HIDDEN, 0)])
    sv["act"] = act
    d = _mm(host, got, "gather", act, w["w_down"], "nn", F32, "mm_down")
    sv["x1"], sv["d"], sv["w"] = x1, d, w
    return (x1, d, g2), sv, got


def _final(x1, d, g2, fw, target, B, S):
    T = B * S
    tr = _row_tile(S)

    def loss_fn(x1v, dv, gv, wv, tv):
        yf = _rms(x1v + gv * dv, wv)
        return 0.5 * jnp.sum(jnp.mean(jnp.square(yf - tv), axis=-1))

    def body(rv, ev, pv):
        x1v, dv, tv = rv
        val, vjp = jax.vjp(lambda a, b_, g, ww: loss_fn(a, b_, g, ww, tv), x1v, dv, ev[0], pv[0])
        dx1, dd, dg, dw = vjp(jnp.ones((), F32))
        return [dx1, dd], [dg], [dw, jnp.full((1, LANES), val, F32)]
    (dx1, dd), (dg2,), (dfw, loss) = _rowcall(
        body, name="final_loss", T=T, S=S, tr=tr, rows=[(x1, D_MODEL, 0), (d, D_MODEL, 0), (target, D_MODEL, 0)],
        exs=[g2], pars=[fw], row_outs=[(D_MODEL, F32, D_MODEL, 0), (D_MODEL, BF16, D_MODEL, 0)],
        ex_outs=[D_MODEL], par_outs=[(1, D_MODEL), (1, LANES)])
    return loss, dx1, dd, dg2, dfw


def _layer_bwd(sv, prev, mods, tabs, dx1, dd, B, S, host=None, own_early=False):
    w = sv["w"]
    T = B * S
    tr = _row_tile(S)
    sh1, sc1, g1, sh2, sc2, g2 = mods
    ca, sa, cb, sb = tabs
    gr = {}
    got = {}
    dact = _mm(host, got, "a2a", dd, w["w_down"], "nt", BF16, "mm_down_dx")
    gr["w_down"] = _matmul(sv["act"], dd, "tn", BF16, "mm_down_dw")

    def body(rv, ev, pv):
        _, vjp = jax.vjp(_f_swiglu, rv[0], rv[1])
        dgate, dup = vjp(rv[2])
        return [[dgate, dup]], [], []
    (dgu,), _, _ = _rowcall(body, name="b_swiglu", T=T, S=S, tr=_row_tile(S, wide=True),
                            rows=[(sv["gu"], FFN_HIDDEN, 0), (sv["gu"], FFN_HIDDEN, 1), (dact, FFN_HIDDEN, 0)],
                            row_outs=[(2 * FFN_HIDDEN, BF16, 2 * FFN_HIDDEN, 0)])
    dh2 = _mm(host, got, "a2a", dgu, w["w_gate_up"], "nt", BF16, "mm_gate_up_dx", b_chunks=True)
    gr["w_gate_up"] = _mm(host, got, "a2a", sv["h2"], dgu, "tn", BF16, "mm_gate_up_dw", out_chunks=True)

    def body(rv, ev, pv):
        xa, delta, dh, dxn = rv
        _, vjp = jax.vjp(_f_resid_norm_mod, xa, delta, ev[0], pv[0], ev[1], ev[2])
        dxa, ddelta, dg, dw, dsc, dsh = vjp((dxn, dh))
        return [dxa, ddelta], [dg, dsc, dsh], [dw]
    (dx, do), (dg1, dsc2, dsh2), (gr["norm2_w"],) = _rowcall(
        body, name="b_resid_norm_mod2", T=T, S=S, tr=tr,
        rows=[(sv["x"], D_MODEL, 0), (sv["o"], D_MODEL, 0), (dh2, D_MODEL, 0), (dx1, D_MODEL, 0)],
        exs=[g1, sc2, sh2], pars=[w["norm2_w"]],
        row_outs=[(D_MODEL, F32, D_MODEL, 0), (D_MODEL, BF16, D_MODEL, 0)], ex_outs=[D_MODEL] * 3,
        par_outs=[(1, D_MODEL)])
    dy = _matmul(do, w["w_out"], "nt", BF16, "mm_out_dx")
    gr["w_out"] = _matmul(sv["y"], do, "tn", BF16, "mm_out_dw")

    def body(rv, ev, pv):
        _, vjp = jax.vjp(_f_outnorm, rv[0], rv[1], rv[2], pv[0])
        doa, dob, dyc, dgw = vjp(rv[3])
        return [doa, dob, dyc], [], [dgw]
    (doa, dob, dyc), _, (gr["out_norm_w"],) = _rowcall(
        body, name="b_outnorm", T=T, S=S, tr=tr,
        rows=[(sv["oa"], 768, 0), (sv["ob"], 768, 0), (sv["yc"], C_WIDTH, 0), (dy, D_MODEL, 0)], pars=[w["out_norm_w"]],
        row_outs=[(768, BF16, 768, 0), (768, BF16, 768, 0), (C_WIDTH, F32, C_WIDTH, 0)], par_outs=[(1, D_MODEL)])

    ta, tb, band = _attn_tiles(S)
    if own_early:
        host = dict(host)
        host["attn_b_bwd"] = list(host.get("attn_b_bwd", ())) + [
            gr["w_gate_up"], gr["w_down"].reshape(N_CHIPS, FFN_HIDDEN // N_CHIPS, D_MODEL)]
    dQ, dK, dV, _ = _hosted(_attn_bwd, host, got, "a2a", "attn_b_bwd", sv["Q"], sv["K"], sv["V"], sv["ob"], sv["lse_b"],
                            dob, None, B=B, S=S, HQ=B_HEADS, HK=B_HEADS, window=None,
                            scale=(B_NOPE + B_ROPE) ** -0.5, tq=tb, band=band)
    dqa, dka, dva, dsink = _hosted(_attn_bwd, host, got, "a2a", "attn_a_bwd", sv["qa"], sv["ka"], sv["va"], sv["oa"],
                                   sv["lse_a"], doa, w["sinks"], B=B, S=S, HQ=A_Q_HEADS, HK=A_KV_HEADS,
                                   window=A_WINDOW, scale=HEAD_DIM ** -0.5, tq=ta, band=None)
    gr["sinks"] = dsink

    def body(rv, ev, pv):
        dQv, dKv, dVv, cbv, sbv = rv
        dq = [_rope_bwd((cbv, sbv), p)[0] for p in _heads(dQv, B_HEADS)]
        dkr = None
        for p in _heads(dKv, B_HEADS):
            dkr = p if dkr is None else dkr + p
        return [dq, [dKv, dVv], dkr], [], []
    (dq, dkv, dkr), _, _ = _rowcall(
        body, name="b_mlaprep", T=T, S=S, tr=tr,
        rows=[(dQ, 768, 0), (dK, 768, 0), (dV, 768, 0), (cb, LANES, 0), (sb, LANES, 0)],
        row_outs=[(768, BF16, 768, 0), (1536, BF16, 1536, 0), (LANES, F32, LANES, 0)])
    dcqn = _matmul(dq, w["b_w_uq"], "nt", BF16, "mm_uq_dx")
    gr["b_w_uq"] = _matmul(sv["cqn"], dq, "tn", BF16, "mm_uq_dw")
    dckvn = _matmul(dkv, w["b_w_ukv"], "nt", BF16, "mm_ukv_dx")
    gr["b_w_ukv"] = _matmul(sv["ckvn"], dkv, "tn", BF16, "mm_ukv_dw")

    def body(rv, ev, pv):
        proj, cav, sav, cbv, sbv, dqa_, dka_, dva_, dcqn_, dckvn_, dkr_ = rv
        _, vjp = jax.vjp(lambda p, a, b_: _f_mixprep(p, cav, sav, cbv, sbv, a, b_), proj, pv[0], pv[1])
        dproj, dqn, dkvn = vjp((dqa_, dka_, dva_, dcqn_, dckvn_, dkr_))
        return [dproj], [], [dqn, dkvn]
    (dproj,), _, (gr["b_q_norm_w"], gr["b_kv_norm_w"]) = _rowcall(
        body, name="b_mixprep", T=T, S=S, tr=tr,
        rows=[(sv["proj"], P_CU, 0), (ca, LANES, 0), (sa, LANES, 0), (cb, LANES, 0), (sb, LANES, 0),
              (dqa, 768, 0), (dka, 256, 0), (dva, 256, 0), (dcqn, 384, 0), (dckvn, 256, 0), (dkr, LANES, 0)],
        pars=[w["b_q_norm_w"], w["b_kv_norm_w"]], row_outs=[(P_END, BF16, P_CU, 0)],
        par_outs=[(1, B_Q_RANK), (1, B_KV_RANK)])

    ts = _sgu_rows(S)

    def body(rv, ev, pv):
        cu, cv, dycv, _ = rv
        dcus, dcvs, acc = [], [], None
        for r in range(0, ts, C_CHUNK):
            _, vjp = jax.vjp(_f_sgu, cu[r:r + C_CHUNK], cv[r:r + C_CHUNK], pv[0], pv[1], pv[2], pv[3])
            dcu, dcv, *dpar = vjp(dycv[r:r + C_CHUNK])
            dcus.append(dcu)
            dcvs.append(dcv)
            acc = dpar if acc is None else [a + b_ for a, b_ in zip(acc, dpar)]
        return [[jnp.concatenate(dcus, axis=0), jnp.concatenate(dcvs, axis=0)]], [], acc
    (dproj,), _, (gr["c_ln_w"], gr["c_ln_b"], gr["c_w_s"], gr["c_b_col"]) = _rowcall(
        body, name="b_sgu", T=T, S=S, tr=ts,
        rows=[(sv["proj"], C_WIDTH, P_CU // C_WIDTH), (sv["proj"], C_WIDTH, P_CV // C_WIDTH), (dyc, C_WIDTH, 0),
              (dproj, 2 * C_WIDTH, P_CU // (2 * C_WIDTH))],
        pars=[w["c_ln_w"], w["c_ln_b"], w["c_w_s"], w["c_b_col"]],
        row_outs=[(P_END, BF16, 2 * C_WIDTH, P_CU // (2 * C_WIDTH))],
        par_outs=[(1, C_WIDTH), (1, C_WIDTH), (C_GROUPS, C_CHUNK, C_CHUNK), (C_GROUPS, C_CHUNK, 1)],
        aliases={3: 0})
    dh = _mm(host, got, "a2a", dproj, w["w_in"], "nt", BF16, "mm_in_dx")
    gr["w_in"] = _matmul(sv["h"], dproj, "tn", BF16, "mm_in_dw")

    if prev is None:
        def body(rv, ev, pv):
            xv, dhv, dxd = rv
            _, vjp = jax.vjp(_f_norm_mod, xv, pv[0], ev[0], ev[1])
            dxa, dw, dsc, dsh = vjp(dhv)
            return [dxa + dxd], [dsc, dsh], [dw]
        (dxin,), (dsc1, dsh1), (gr["norm1_w"],) = _rowcall(
            body, name="b_norm_mod", T=T, S=S, tr=tr, rows=[(sv["x"], D_MODEL, 0), (dh, D_MODEL, 0), (dx, D_MODEL, 0)],
            exs=[sc1, sh1], pars=[w["norm1_w"]], row_outs=[(D_MODEL, F32, D_MODEL, 0)], ex_outs=[D_MODEL] * 2,
            par_outs=[(1, D_MODEL)])
        nxt = (dxin, None, None)
    else:
        x1p, dp, g2p = prev

        def body(rv, ev, pv):
            xa, delta, dhv, dxn = rv
            _, vjp = jax.vjp(_f_resid_norm_mod, xa, delta, ev[0], pv[0], ev[1], ev[2])
            dxa, ddelta, dg, dw, dsc, dsh = vjp((dxn, dhv))
            return [dxa, ddelta], [dg, dsc, dsh], [dw]
        (dx1p, ddp), (dg2p, dsc1, dsh1), (gr["norm1_w"],) = _rowcall(
            body, name="b_resid_norm_mod1", T=T, S=S, tr=tr,
            rows=[(x1p, D_MODEL, 0), (dp, D_MODEL, 0), (dh, D_MODEL, 0), (dx, D_MODEL, 0)],
            exs=[g2p, sc1, sh1], pars=[w["norm1_w"]],
            row_outs=[(D_MODEL, F32, D_MODEL, 0), (D_MODEL, BF16, D_MODEL, 0)], ex_outs=[D_MODEL] * 3,
            par_outs=[(1, D_MODEL)])
        nxt = (dx1p, ddp, dg2p)
    return gr, (dsh1, dsc1, dg1, dsh2, dsc2), nxt, got


def _lane_table(lanes_neg, lanes_pos, inv):
    freq = np.zeros((LANES,), np.int64) - 1
    sign = np.zeros((1, LANES), np.float32)
    n = len(lanes_neg)
    freq[lanes_neg] = np.arange(n)
    freq[lanes_pos] = np.arange(n)
    sign[0, lanes_neg] = -1.0
    sign[0, lanes_pos] = 1.0
    return _select_axis(inv, freq, 0).reshape(1, LANES), jnp.asarray(sign)


SHARDED = ("ada_w", "w_in", "b_w_uq", "b_w_ukv", "w_out", "w_gate_up", "w_down")
ROW_SHARDED = ("w_out", "w_down")
SMALL = ("ada_b", "norm1_w", "a_sinks", "b_q_norm_w", "b_kv_norm_w", "c_ln_w", "c_ln_b", "c_w_s", "c_b_s",
         "out_norm_w", "norm2_w", "final_norm_w")
FWD_HOST = {"attn_b_fwd": ("ada_w", "w_gate_up"), "attn_a_fwd": ("w_down", "w_in"),
            "mm_gate_up": ("w_out", "b_w_uq", "b_w_ukv")}
FWD_HOST_FIRST = {"attn_b_fwd": ("ada_w", "w_gate_up"), "mm_gate_up": ("w_down", "w_out", "b_w_uq", "b_w_ukv"),
                  "mm_down": ("w_in",)}
BWD_HOST = {"attn_a_bwd": ("w_in", "w_out", "b_w_uq", "b_w_ukv"), "attn_b_bwd": ("ada_w", "w_gate_up", "w_down")}
EXPOSED = ("ada_w", "w_in", "b_w_uq", "b_w_ukv", "w_out")
OWN_LAYER = ("w_gate_up", "w_down")


def _from_host(table, got):
    return {k: got[name][i] for name, ks in table.items() for i, k in enumerate(ks)}


def _to_host(table, arrays):
    return {name: [arrays[k] for k in ks] for name, ks in table.items()}


def _join_cols(g):
    return jnp.concatenate([g[j] for j in range(N_CHIPS)], axis=1)


def _split_cols(g):
    n = g.shape[1] // N_CHIPS
    return jnp.stack([g[:, j * n:(j + 1) * n] for j in range(N_CHIPS)])


def _layer_weights(G, small, l):
    D = D_MODEL
    w = {} if "w_gate_up" not in G else _late_weights(G["w_gate_up"], G["w_down"])
    return {
        **w,
        "norm1_w": small["norm1_w"][l].reshape(1, D),
        "w_in": _pad_axis(_join_cols(G["w_in"]), _map_w_in(), 1),
        "sinks": jnp.broadcast_to(small["a_sinks"][l].reshape(A_Q_HEADS, 1, 1), (A_Q_HEADS, 1, LANES)),
        "b_q_norm_w": small["b_q_norm_w"][l].reshape(1, B_Q_RANK),
        "b_w_uq": _pad_axis(_join_cols(G["b_w_uq"]), _map_w_uq(), 1),
        "b_kv_norm_w": small["b_kv_norm_w"][l].reshape(1, B_KV_RANK),
        "b_w_ukv": _pad_axis(_join_cols(G["b_w_ukv"]), _map_w_ukv(), 1),
        "c_ln_w": small["c_ln_w"][l].reshape(1, C_WIDTH), "c_ln_b": small["c_ln_b"][l].reshape(1, C_WIDTH),
        "c_w_s": small["c_w_s"][l], "c_b_col": small["c_b_s"][l].reshape(C_GROUPS, C_CHUNK, 1),
        "out_norm_w": small["out_norm_w"][l].reshape(1, D), "w_out": G["w_out"].reshape(D, D),
        "norm2_w": small["norm2_w"][l].reshape(1, D),
    }


def _late_weights(gate_up, down):
    return {"w_gate_up": gate_up, "w_down": down.reshape(FFN_HIDDEN, D_MODEL)}


def _send_buffers(gr, ada_gw):
    D = D_MODEL
    return {
        "ada_w": ada_gw, "w_gate_up": gr["w_gate_up"], "w_down": gr["w_down"].reshape(N_CHIPS, FFN_HIDDEN // N_CHIPS, D),
        "w_out": gr["w_out"].reshape(N_CHIPS, D // N_CHIPS, D),
        "w_in": _split_cols(_unpad_axis(gr["w_in"], _map_w_in(), IN_COLS, 1)),
        "b_w_uq": _split_cols(_unpad_axis(gr["b_w_uq"], _map_w_uq(), B_HEADS * (B_NOPE + B_ROPE), 1)),
        "b_w_ukv": _split_cols(_unpad_axis(gr["b_w_ukv"], _map_w_ukv(), B_HEADS * (B_NOPE + B_V), 1)),
    }


def _small_grads(gr, ada_gb, B):
    D = D_MODEL
    return {
        "ada_b": ada_gb.reshape(N_MOD * D), "norm1_w": gr["norm1_w"].reshape(D),
        "a_sinks": gr["sinks"][:, 0, 0].reshape(B, A_Q_HEADS).sum(axis=0),
        "b_q_norm_w": gr["b_q_norm_w"].reshape(B_Q_RANK), "b_kv_norm_w": gr["b_kv_norm_w"].reshape(B_KV_RANK),
        "c_ln_w": gr["c_ln_w"].reshape(C_WIDTH), "c_ln_b": gr["c_ln_b"].reshape(C_WIDTH), "c_w_s": gr["c_w_s"],
        "c_b_s": gr["c_b_col"].reshape(C_GROUPS, C_CHUNK),
        "out_norm_w": gr["out_norm_w"].reshape(D), "norm2_w": gr["norm2_w"].reshape(D),
    }


def _step(x, c, positions, target, small, shard_of, gathered=None):
    dist = gathered is None
    B, S, D = x.shape
    T = B * S
    xt = x.reshape(T, D)
    tgt = target.reshape(T, D)
    pos_col = positions.astype(F32).reshape(T, 1)
    inv_a = 1.0 / (ROPE_THETA ** (jnp.arange(0, HEAD_DIM, 2, dtype=F32) / HEAD_DIM))
    inv_b = 1.0 / (ROPE_THETA ** (jnp.arange(0, B_ROPE, 2, dtype=F32) / B_ROPE))
    fa, sga = _lane_table(np.arange(32), 64 + np.arange(32), inv_a)
    fb, sgb = _lane_table(48 + np.arange(16), 112 + np.arange(16), inv_b)
    ca, sa = _rope_tables(pos_col, fa, sga, "rope_a")
    cb, sb = _rope_tables(pos_col, fb, sgb, "rope_b")
    tabs = (ca, sa, cb, sb)
    c8 = jnp.zeros((8, D), F32).at[:B].set(c)

    if dist:
        first = shard_of(0)
        G = dict(zip(EXPOSED, _xchg_call("gather", [first[k] for k in EXPOSED], "gather_first")))
    else:
        G = gathered[0]
    saved, prevs, modss = [], [], []
    prev = None
    for l in range(DEPTH):
        w = _layer_weights(G, small, l)
        mod = _ada_fwd(c8, G["ada_w"], small["ada_b"][l].reshape(1, N_MOD * D))
        mods = tuple(mod[:B, i * D:(i + 1) * D].reshape(B, 1, D) for i in range(N_MOD))
        more = l + 1 < DEPTH
        table = FWD_HOST_FIRST if l == 0 else FWD_HOST
        host = _to_host(table, shard_of(l + 1)) if dist and more else {}
        late = None
        if dist and l == 0:
            host["attn_a_fwd"] = [first[k] for k in OWN_LAYER]
            late = lambda got: _late_weights(*got["attn_a_fwd"])
        prevs.append(prev)
        modss.append(mods)
        prev, sv, got = _layer_fwd(xt, prev, mods, w, tabs, B, S, host, late)
        saved.append(sv)
        if more:
            G = _from_host(table, got) if dist else gathered[l + 1]
    x1, d, g2 = prev
    loss, dx1, dd, dg2, dfw = _final(x1, d, g2, small["final_norm_w"].reshape(1, D), tgt, B, S)

    landed = [None] * DEPTH
    smalls = [None] * DEPTH
    pending = None
    for l in reversed(range(DEPTH)):
        host = _to_host(BWD_HOST, pending) if dist and pending is not None else None
        early = dist and l == 0 and host is not None
        gr, (dsh1, dsc1, dg1, dsh2, dsc2), nxt, got = _layer_bwd(saved[l], prevs[l], modss[l], tabs, dx1, dd, B, S, host, early)
        if pending is not None:
            landed[l + 1] = _from_host(BWD_HOST, got) if dist else pending
        dmod = jnp.concatenate([dsh1, dsc1, dg1, dsh2, dsc2, dg2], axis=-1).reshape(B, N_MOD * D)
        ada_gw, ada_gb = _ada_bwd(c8, jnp.zeros((8, N_MOD * D), F32).at[:B].set(dmod))
        pending = _send_buffers(gr, ada_gw)
        smalls[l] = _small_grads(gr, ada_gb, B)
        dx1, dd, dg2 = nxt
    if dist:
        last = EXPOSED if early else SHARDED
        landed[0] = dict(zip(last, _xchg_call("a2a", [pending[k] for k in last], "grad_exchange_last")))
        if early:
            landed[0].update(zip(OWN_LAYER, got["attn_b_bwd"][-len(OWN_LAYER):]))
    else:
        landed[0] = pending
    small_g = {k: jnp.stack([smalls[l][k] for l in range(DEPTH)]) for k in SMALL if k != "final_norm_w"}
    small_g["final_norm_w"] = dfw.reshape(D)
    return loss[0, 0], dx1.reshape(B, S, D), landed, small_g


def _pack(arrs, cols, mult):
    flat = jnp.concatenate([a.reshape(-1) for a in arrs])
    n = flat.shape[0]
    rows = -(-n // cols)
    rows = -(-rows // mult) * mult
    return jnp.pad(flat, (0, rows * cols - n)).reshape(rows, cols)


def _unpack(blob, shapes):
    flat = blob.reshape(-1)
    out, off = [], 0
    for s in shapes:
        n = int(np.prod(s))
        out.append(flat[off:off + n].reshape(s))
        off += n
    return out


def kernel(x, c, positions, ada_w, ada_b, norm1_w, w_in, a_sinks, b_q_norm_w, b_w_uq, b_kv_norm_w, b_w_ukv, c_ln_w, c_ln_b, c_w_s, c_b_s, out_norm_w, w_out, norm2_w, w_gate_up, w_down, final_norm_w, loss_target, m_ada_w, m_ada_b, m_norm1_w, m_w_in, m_a_sinks, m_b_q_norm_w, m_b_w_uq, m_b_kv_norm_w, m_b_w_ukv, m_c_ln_w, m_c_ln_b, m_c_w_s, m_c_b_s, m_out_norm_w, m_w_out, m_norm2_w, m_w_gate_up, m_w_down, m_final_norm_w, v_ada_w, v_ada_b, v_norm1_w, v_w_in, v_a_sinks, v_b_q_norm_w, v_b_w_uq, v_b_kv_norm_w, v_b_w_ukv, v_c_ln_w, v_c_ln_b, v_c_w_s, v_c_b_s, v_out_norm_w, v_w_out, v_norm2_w, v_w_gate_up, v_w_down, v_final_norm_w):
    names = ("ada_w", "ada_b", "norm1_w", "w_in", "a_sinks", "b_q_norm_w", "b_w_uq", "b_kv_norm_w", "b_w_ukv", "c_ln_w",
             "c_ln_b", "c_w_s", "c_b_s", "out_norm_w", "w_out", "norm2_w", "w_gate_up", "w_down", "final_norm_w")
    ws = dict(zip(names, (ada_w, ada_b, norm1_w, w_in, a_sinks, b_q_norm_w, b_w_uq, b_kv_norm_w, b_w_ukv, c_ln_w, c_ln_b,
                          c_w_s, c_b_s, out_norm_w, w_out, norm2_w, w_gate_up, w_down, final_norm_w)))
    ms = dict(zip(names, (m_ada_w, m_ada_b, m_norm1_w, m_w_in, m_a_sinks, m_b_q_norm_w, m_b_w_uq, m_b_kv_norm_w, m_b_w_ukv,
                          m_c_ln_w, m_c_ln_b, m_c_w_s, m_c_b_s, m_out_norm_w, m_w_out, m_norm2_w, m_w_gate_up, m_w_down,
                          m_final_norm_w)))
    vs = dict(zip(names, (v_ada_w, v_ada_b, v_norm1_w, v_w_in, v_a_sinks, v_b_q_norm_w, v_b_w_uq, v_b_kv_norm_w, v_b_w_ukv,
                          v_c_ln_w, v_c_ln_b, v_c_w_s, v_c_b_s, v_out_norm_w, v_w_out, v_norm2_w, v_w_gate_up, v_w_down,
                          v_final_norm_w)))
    shards = {k: ws[k].astype(BF16) for k in SHARDED}
    loss_local, grad_x, landed, gsmall = _step(x, c, positions, loss_target, {k: ws[k] for k in SMALL},
                                               lambda l: {k: shards[k][l] for k in SHARDED})

    mine = {k: jnp.stack([_sum_slots(landed[l][k], "grad_sum_" + k) for l in range(DEPTH)]) for k in SHARDED}
    theirs = dict(zip(SHARDED, _xchg_call("swap", [mine[k] for k in SHARDED], "grad_sibling_swap")))
    grads, delta, new_m, new_v = {}, {}, {}, {}
    for k in SHARDED:
        shp = ws[k].shape
        two = (shp[0] * shp[1], shp[2])
        g, dlt, nm, nv = _adamw(ws[k].reshape(two), mine[k].reshape(two), theirs[k].reshape(two), ms[k].reshape(two),
                                vs[k].reshape(two), "adamw_" + k)
        grads[k], delta[k], new_m[k], new_v[k] = g.reshape(shp), dlt.reshape(shp), nm.reshape(shp), nv.reshape(shp)

    small_shapes = [ws[k].shape for k in SMALL]
    sblob = _pack([gsmall[k] for k in SMALL] + [loss_local.reshape(1)], LANES, 8)
    svals = _unpack(_all_reduce_small(sblob, "small_all_reduce"), small_shapes + [(1,)])
    loss = svals[-1].reshape(())
    pw = _pack([ws[k] for k in SMALL], LANES, 8)
    pg = _pack(svals[:-1], LANES, 8)
    pm = _pack([ms[k] for k in SMALL], LANES, 8)
    pv = _pack([vs[k] for k in SMALL], LANES, 8)
    g, dlt, nm, nv = _adamw(pw, pg, None, pm, pv, "adamw_small")
    for k, a, b_, c_, d_ in zip(SMALL, _unpack(g, small_shapes), _unpack(dlt, small_shapes), _unpack(nm, small_shapes),
                                _unpack(nv, small_shapes)):
        grads[k], delta[k], new_m[k], new_v[k] = a, b_, c_, d_

    return (loss, grad_x, *[grads[k] for k in names], *[delta[k] for k in names], *[new_m[k] for k in names],
            *[new_v[k] for k in names])
```

```python
import functools
import math

import numpy as np
import jax
import jax.numpy as jnp
from jax import lax
from jax.experimental import pallas as pl
from jax.experimental.pallas import tpu as pltpu

F32 = jnp.float32
BF16 = jnp.bfloat16

D_MODEL = 1024
DEPTH = 4
HEAD_DIM = 64
ROPE_THETA = 10000.0
NORM_EPS = 1e-6
NEG_INF = -1e30
LOG2_E = math.log2(math.e)
A_Q_HEADS = 6
A_KV_HEADS = 2
A_WINDOW = 128
B_HEADS = 6
B_Q_RANK = 384
B_KV_RANK = 256
B_NOPE = 64
B_ROPE = 32
B_V = 64
C_GROUPS = 4
C_GROUP_DIM = 64
C_WIDTH = 256
C_CHUNK = 128
IN_COLS = 1824
FFN_HIDDEN = 2816
N_MOD = 6
ADAM_LR = 0.001
ADAM_B1 = 0.9
ADAM_B2 = 0.999
ADAM_EPS = 1e-08
ADAM_WD = 0.01
ADAM_STEP = 10

LANES = 128
VMEM_LIMIT = 56 * 1024 * 1024
N_CHIPS = 4
WINDOW_SUB = 256

P_AQ, P_AK, P_AV, P_CQ, P_CKV, P_KR, P_CU, P_CV, P_END = 0, 768, 1024, 1280, 1664, 1920, 2048, 2304, 2560


def _map_w_in():
    idx = -np.ones(P_END, np.int64)
    half = HEAD_DIM // 2
    for h in range(A_Q_HEADS):
        idx[P_AQ + h * LANES + np.arange(half)] = h * HEAD_DIM + np.arange(half)
        idx[P_AQ + h * LANES + 64 + np.arange(half)] = h * HEAD_DIM + half + np.arange(half)
    for h in range(A_KV_HEADS):
        idx[P_AK + h * LANES + np.arange(half)] = 384 + h * HEAD_DIM + np.arange(half)
        idx[P_AK + h * LANES + 64 + np.arange(half)] = 384 + h * HEAD_DIM + half + np.arange(half)
        idx[P_AV + h * LANES + np.arange(HEAD_DIM)] = 512 + h * HEAD_DIM + np.arange(HEAD_DIM)
    idx[P_CQ:P_CQ + 384] = 640 + np.arange(384)
    idx[P_CKV:P_CKV + 256] = 1024 + np.arange(256)
    idx[P_KR + 48 + np.arange(16)] = 1280 + np.arange(16)
    idx[P_KR + 112 + np.arange(16)] = 1296 + np.arange(16)
    idx[P_CU:P_CU + 256] = 1312 + np.arange(256)
    idx[P_CV:P_CV + 256] = 1568 + np.arange(256)
    return idx


def _map_w_uq():
    idx = -np.ones(B_HEADS * LANES, np.int64)
    for h in range(B_HEADS):
        b = h * (B_NOPE + B_ROPE)
        idx[h * LANES + np.arange(48)] = b + np.arange(48)
        idx[h * LANES + 48 + np.arange(16)] = b + 64 + np.arange(16)
        idx[h * LANES + 64 + np.arange(16)] = b + 48 + np.arange(16)
        idx[h * LANES + 112 + np.arange(16)] = b + 80 + np.arange(16)
    return idx


def _map_w_ukv():
    idx = -np.ones(2 * B_HEADS * LANES, np.int64)
    for h in range(B_HEADS):
        b = h * (B_NOPE + B_V)
        idx[h * LANES + np.arange(48)] = b + np.arange(48)
        idx[h * LANES + 64 + np.arange(16)] = b + 48 + np.arange(16)
        idx[B_HEADS * LANES + h * LANES + np.arange(B_V)] = b + B_NOPE + np.arange(B_V)
    return idx


def _inverse(idx, n):
    inv = np.zeros(n, np.int64)
    pos = np.nonzero(idx >= 0)[0]
    inv[idx[pos]] = pos
    return inv


def _runs(idx):
    runs, i, n = [], 0, len(idx)
    while i < n:
        j = i + 1
        while j < n and ((idx[i] < 0 and idx[j] < 0) or (idx[i] >= 0 and idx[j] == idx[i] + (j - i))):
            j += 1
        runs.append((int(idx[i]), j - i))
        i = j
    return runs


def _select_axis(w, idx, axis):
    pieces = []
    for start, length in _runs(idx):
        if start < 0:
            shape = list(w.shape)
            shape[axis] = length
            pieces.append(jnp.zeros(shape, w.dtype))
        else:
            pieces.append(lax.slice_in_dim(w, start, start + length, axis=axis))
    return jnp.concatenate(pieces, axis=axis)


def _pad_axis(w, idx, axis):
    return _select_axis(w, idx, axis)


def _unpad_axis(g, idx, n, axis):
    return _select_axis(g, _inverse(idx, n), axis)


def _params(sem):
    return pltpu.CompilerParams(dimension_semantics=sem, vmem_limit_bytes=VMEM_LIMIT)


def _tile(dim, target):
    if dim <= target:
        return dim
    best = None
    for t in range(LANES, target + 1, LANES):
        if dim % t == 0:
            best = t
    assert best is not None, dim
    return best


def _row_div(rows, target):
    if rows <= target:
        return rows
    best = None
    for t in range(8, target + 1, 8):
        if rows % t == 0:
            best = t
    assert best is not None, rows
    return best


_ANY = pl.BlockSpec(memory_space=pl.ANY)
_MESH = pl.DeviceIdType.MESH


def _xchg_out_shapes(kind, srcs):
    if kind == "gather":
        return [jax.ShapeDtypeStruct((N_CHIPS,) + s.shape, s.dtype) for s in srcs]
    return [jax.ShapeDtypeStruct(s.shape, s.dtype) for s in srcs]


def _xchg_scratch(kind, n):
    per = 1 if kind == "swap" else N_CHIPS - 1
    return [pltpu.SemaphoreType.DMA((per * n,)), pltpu.SemaphoreType.DMA((per * n,)), pltpu.SemaphoreType.DMA((n,))]


def _xchg_copies(kind, srcs, dsts, send_sems, recv_sems, local_sems, arrivals):
    x, y, c = lax.axis_index("x"), lax.axis_index("y"), lax.axis_index("c")
    me = 2 * x + y
    peers = [(1 - x, y), (x, 1 - y), (1 - x, 1 - y)]
    local, out, back = [], [], []
    for i, (s, d) in enumerate(zip(srcs, dsts)):
        if kind == "swap":
            cp = pltpu.make_async_remote_copy(src_ref=s, dst_ref=d, send_sem=send_sems.at[i], recv_sem=recv_sems.at[i],
                                              device_id=(x, y, 1 - c), device_id_type=_MESH)
            out.append(cp)
            back.append(cp)
            continue
        local.append(pltpu.make_async_copy(s if kind == "gather" else s.at[me], d.at[me], local_sems.at[i]))
        for kk, (px, py) in enumerate(peers):
            j = (N_CHIPS - 1) * i + kk
            theirs = 2 * px + py
            out.append(pltpu.make_async_remote_copy(
                src_ref=s if kind == "gather" else s.at[theirs], dst_ref=d.at[me], send_sem=send_sems.at[j],
                recv_sem=recv_sems.at[j], device_id=(px, py, c), device_id_type=_MESH))
            if arrivals:
                back.append(pltpu.make_async_remote_copy(
                    src_ref=s if kind == "gather" else s.at[me], dst_ref=d.at[theirs], send_sem=send_sems.at[j],
                    recv_sem=recv_sems.at[j], device_id=(px, py, c), device_id_type=_MESH))
    return local, out, back


def _xchg_start(kind, srcs, dsts, sems):
    local, out, _ = _xchg_copies(kind, srcs, dsts, *sems, arrivals=False)
    for cp in local + out:
        cp.start()


def _xchg_wait(kind, srcs, dsts, sems):
    local, out, back = _xchg_copies(kind, srcs, dsts, *sems, arrivals=True)
    for cp in back:
        cp.wait_recv()
    for cp in out:
        cp.wait_send()
    for cp in local:
        cp.wait()


def _xchg_at_ends(kind, srcs, dsts, sems, grid, first):
    ids = [pl.program_id(a) for a in range(len(grid))]
    cond = None
    for i, n in zip(ids, grid):
        c = (i == 0) if first else (i == n - 1)
        cond = c if cond is None else jnp.logical_and(cond, c)

    @pl.when(cond)
    def _():
        (_xchg_start if first else _xchg_wait)(kind, srcs, dsts, sems)


def _xchg_call(kind, srcs, name):
    n = len(srcs)

    def kern(*refs):
        s, d, sems = refs[:n], refs[n:2 * n], refs[2 * n:]
        _xchg_start(kind, s, d, sems)
        _xchg_wait(kind, s, d, sems)

    return pl.pallas_call(
        kern, name=name, in_specs=[_ANY] * n, out_specs=[_ANY] * n, out_shape=_xchg_out_shapes(kind, srcs),
        scratch_shapes=_xchg_scratch(kind, n),
    )(*srcs)


_DIMS = {"nn": (((1,), (0,)), ((), ())), "nt": (((1,), (1,)), ((), ())), "tn": (((0,), (0,)), ((), ()))}


def _matmul(a, b, mode, out_dtype, name, *, b_chunks=False, out_chunks=False, xchg=None):
    if b_chunks:
        nchunk, brows, bcols = b.shape
        bshape = (brows, nchunk * bcols)
    else:
        bshape = b.shape
    if mode == "nn":
        (m, k), (_, n) = a.shape, bshape
    elif mode == "nt":
        (m, k), (n, _) = a.shape, bshape
    else:
        (k, m), (_, n) = a.shape, bshape
    tm, tk = (1408, 1024) if mode == "tn" else (1024, 1408)
    tm, tn, tk = _tile(m, tm), _tile(n, 1408), _tile(k, tk)
    if b_chunks:
        if mode == "nn":
            tn = bcols
        else:
            assert mode == "nt"
            tk = bcols
    if out_chunks:
        assert n % N_CHIPS == 0
        tn = n // N_CHIPS
    ni, nj, nk = m // tm, n // tn, k // tk
    dims = _DIMS[mode]
    n_x = 0 if xchg is None else len(xchg[1])

    def kern(*refs):
        a_ref, b_ref = refs[0], refs[1]
        xs = refs[2:2 + n_x]
        o_ref = refs[2 + n_x]
        xd = refs[3 + n_x:3 + 2 * n_x]
        acc_ref = refs[3 + 2 * n_x]
        sems = refs[4 + 2 * n_x:]
        kk = pl.program_id(2)
        if n_x:
            _xchg_at_ends(xchg[0], xs, xd, sems, (ni, nj, nk), True)

        @pl.when(kk == 0)
        def _():
            acc_ref[...] = jnp.zeros_like(acc_ref)

        acc_ref[...] += lax.dot_general(a_ref[...], b_ref[...], dims, preferred_element_type=F32)

        @pl.when(kk == nk - 1)
        def _():
            o_ref[...] = acc_ref[...].astype(o_ref.dtype)

        if n_x:
            _xchg_at_ends(xchg[0], xs, xd, sems, (ni, nj, nk), False)

    if mode == "tn":
        a_spec = pl.BlockSpec((tk, tm), lambda i, j, kk: (kk, i))
    else:
        a_spec = pl.BlockSpec((tm, tk), lambda i, j, kk: (i, kk))
    if b_chunks and mode == "nn":
        b_spec = pl.BlockSpec((None, tk, tn), lambda i, j, kk: (j, kk, 0))
    elif b_chunks:
        b_spec = pl.BlockSpec((None, tn, tk), lambda i, j, kk: (kk, j, 0))
    elif mode == "nt":
        b_spec = pl.BlockSpec((tn, tk), lambda i, j, kk: (j, kk))
    else:
        b_spec = pl.BlockSpec((tk, tn), lambda i, j, kk: (kk, j))
    if out_chunks:
        o_spec = pl.BlockSpec((None, tm, tn), lambda i, j, kk: (j, i, 0))
        o_shape = jax.ShapeDtypeStruct((N_CHIPS, m, tn), out_dtype)
    else:
        o_spec = pl.BlockSpec((tm, tn), lambda i, j, kk: (i, j))
        o_shape = jax.ShapeDtypeStruct((m, n), out_dtype)
    xs = [] if xchg is None else list(xchg[1])
    res = pl.pallas_call(
        kern, name=name, grid=(ni, nj, nk),
        in_specs=[a_spec, b_spec] + [_ANY] * n_x, out_specs=[o_spec] + [_ANY] * n_x,
        out_shape=[o_shape] + (_xchg_out_shapes(xchg[0], xs) if n_x else []),
        scratch_shapes=[pltpu.VMEM((tm, tn), F32)] + (_xchg_scratch(xchg[0], n_x) if n_x else []),
        compiler_params=_params(("arbitrary", "arbitrary", "arbitrary") if n_x else ("parallel", "parallel", "arbitrary")),
    )(a, b, *xs)
    return (res[0], list(res[1:])) if n_x else res[0]


def _rowcall(body, *, name, T, S, tr, rows, exs=(), pars=(), row_outs=(), ex_outs=(), par_outs=(), aliases=None):
    assert S % tr == 0 and T % S == 0
    per_ex = S // tr
    nb = T // S
    n_rows, n_exs, n_pars = len(rows), len(exs), len(pars)
    n_ro, n_eo, n_po = len(row_outs), len(ex_outs), len(par_outs)

    def kern(*refs):
        ins = refs[:n_rows + n_exs + n_pars]
        outs = refs[n_rows + n_exs + n_pars:]
        rv = [r[...].astype(F32) for r in ins[:n_rows]]
        ev = [r[0] for r in ins[n_rows:n_rows + n_exs]]
        pv = [r[...] for r in ins[n_rows + n_exs:]]
        ro, eo, po = body(rv, ev, pv)
        i = pl.program_id(0)
        for ref, val in zip(outs[:n_ro], ro):
            if isinstance(val, (list, tuple)):
                off = 0
                for piece in val:
                    w = piece.shape[-1]
                    ref[:, off:off + w] = piece.astype(ref.dtype)
                    off += w
            else:
                ref[...] = val.astype(ref.dtype)
        first_of_ex = (i % per_ex) == 0
        for ref, val in zip(outs[n_ro:n_ro + n_eo], eo):
            @pl.when(first_of_ex)
            def _(ref=ref, val=val):
                ref[0] = val

            @pl.when(jnp.logical_not(first_of_ex))
            def _(ref=ref, val=val):
                ref[0] += val
        for ref, val in zip(outs[n_ro + n_eo:], po):
            @pl.when(i == 0)
            def _(ref=ref, val=val):
                ref[...] = val

            @pl.when(i != 0)
            def _(ref=ref, val=val):
                ref[...] += val

    in_specs = [pl.BlockSpec((tr, w), functools.partial(lambda i, cb: (i, cb), cb=cb)) for (_, w, cb) in rows]
    in_specs += [pl.BlockSpec((1, 1, e.shape[-1]), lambda i: (i // per_ex, 0, 0)) for e in exs]
    in_specs += [pl.BlockSpec(p.shape, functools.partial(lambda i, nd: (0,) * nd, nd=p.ndim)) for p in pars]
    out_specs = [pl.BlockSpec((tr, w), functools.partial(lambda i, cb: (i, cb), cb=cb)) for (_, _, w, cb) in row_outs]
    out_specs += [pl.BlockSpec((1, 1, f), lambda i: (i // per_ex, 0, 0)) for f in ex_outs]
    out_specs += [pl.BlockSpec(tuple(s), functools.partial(lambda i, nd: (0,) * nd, nd=len(s))) for s in par_outs]
    out_shape = [jax.ShapeDtypeStruct((T, tw), dt) for (tw, dt, _, _) in row_outs]
    out_shape += [jax.ShapeDtypeStruct((nb, 1, f), F32) for f in ex_outs]
    out_shape += [jax.ShapeDtypeStruct(tuple(s), F32) for s in par_outs]
    res = pl.pallas_call(
        kern, name=name, grid=(T // tr,), in_specs=in_specs, out_specs=out_specs, out_shape=out_shape,
        input_output_aliases=aliases or {}, compiler_params=_params(("arbitrary",)),
    )(*[r[0] for r in rows], *exs, *pars)
    return res[:n_ro], res[n_ro:n_ro + n_eo], res[n_ro + n_eo:]


def _rms(x, w, n=None):
    n = x.shape[-1] if n is None else n
    ms = jnp.sum(x * x, axis=-1, keepdims=True) * (1.0 / n)
    return x * lax.rsqrt(ms + NORM_EPS) * w


def _gelu(x):
    return 0.5 * x * (1.0 + lax.erf(x * np.float32(1.0 / math.sqrt(2.0))))


def _silu(x):
    return x * jax.nn.sigmoid(x)


@jax.custom_vjp
def _rope(x, cos, sin):
    return x * cos + pltpu.roll(x, 64, 1) * sin


def _rope_fwd(x, cos, sin):
    return _rope(x, cos, sin), (cos, sin)


def _rope_bwd(res, dy):
    cos, sin = res
    return dy * cos + pltpu.roll(dy * sin, 64, 1), None, None


_rope.defvjp(_rope_fwd, _rope_bwd)


def _heads(x, n):
    return [x[:, h * LANES:(h + 1) * LANES] for h in range(n)]


def _f_norm_mod(x, w, sc, sh):
    return _rms(x, w) * (1.0 + sc) + sh


def _f_resid_norm_mod(xa, delta, g, w, sc, sh):
    xn = xa + g * delta
    return xn, _f_norm_mod(xn, w, sc, sh)


def _f_mixprep(proj, ca, sa, cb, sb, qnw, kvnw):
    qa = [_rope(p, ca, sa) for p in _heads(proj[:, P_AQ:P_AK], A_Q_HEADS)]
    ka = [_rope(p, ca, sa) for p in _heads(proj[:, P_AK:P_AV], A_KV_HEADS)]
    va = proj[:, P_AV:P_CQ]
    cqn = _rms(proj[:, P_CQ:P_CKV], qnw)
    ckvn = _rms(proj[:, P_CKV:P_KR], kvnw)
    kr = _rope(proj[:, P_KR:P_CU], cb, sb)
    return jnp.concatenate(qa, -1), jnp.concatenate(ka, -1), va, cqn, ckvn, kr


def _f_mlaprep(q, kv, kr, cb, sb):
    qs = [_rope(p, cb, sb) for p in _heads(q, B_HEADS)]
    ks = [p + kr for p in _heads(kv[:, :B_HEADS * LANES], B_HEADS)]
    return jnp.concatenate(qs, -1), jnp.concatenate(ks, -1), kv[:, B_HEADS * LANES:]


def _f_sgu(cu, cv, ln_w, ln_b, w_s, b_col):
    u = _gelu(cu)
    v = _gelu(cv)
    mu = jnp.mean(v, axis=-1, keepdims=True)
    var = jnp.mean(jnp.square(v - mu), axis=-1, keepdims=True)
    vn = (v - mu) * lax.rsqrt(var + NORM_EPS) * ln_w + ln_b
    r = lax.broadcasted_iota(jnp.int32, (C_CHUNK, C_CHUNK), 0)
    c = lax.broadcasted_iota(jnp.int32, (C_CHUNK, C_CHUNK), 1)
    lane = lax.broadcasted_iota(jnp.int32, (1, LANES), 1)
    per_block = LANES // C_GROUP_DIM
    blocks = []
    for blk, vb in enumerate(_heads(vn, C_WIDTH // LANES)):
        mixed = jnp.zeros(vb.shape, F32)
        for j in range(per_block):
            g = blk * per_block + j
            gm = (lane // C_GROUP_DIM == j).astype(F32)
            wg = jnp.where(r >= c, w_s[g], 0.0).astype(BF16)
            mixed = mixed + jnp.dot(wg, (vb * gm).astype(BF16), preferred_element_type=F32) + b_col[g] * gm
        blocks.append(mixed)
    return u * jnp.concatenate(blocks, -1)


@jax.custom_vjp
def _pack_pairs(x):
    heads = _heads(x, x.shape[-1] // LANES)
    return jnp.concatenate([heads[i] + pltpu.roll(heads[i + 1], 64, 1) for i in range(0, len(heads), 2)], -1)


def _pack_pairs_fwd(x):
    return _pack_pairs(x), None


def _pack_pairs_bwd(_, dy):
    out = []
    for p in _heads(dy, dy.shape[-1] // LANES):
        out += [p, pltpu.roll(p, 64, 1)]
    return (jnp.concatenate(out, -1),)


_pack_pairs.defvjp(_pack_pairs_fwd, _pack_pairs_bwd)


def _f_outnorm(oa, ob, yc, gw):
    na, nb = A_Q_HEADS * HEAD_DIM, B_HEADS * B_V
    ya = _rms(_pack_pairs(oa), gw[:, :na])
    yb = _rms(_pack_pairs(ob), gw[:, na:na + nb])
    ycn = _rms(yc, gw[:, na + nb:])
    return jnp.concatenate([ya, yb, ycn], -1)


def _f_swiglu(gate, up):
    return _silu(gate) * up


def _mask(q_start, k_start, tq, tk, window):
    qpos = q_start + lax.broadcasted_iota(jnp.int32, (tq, tk), 0)
    kpos = k_start + lax.broadcasted_iota(jnp.int32, (tq, tk), 1)
    m = kpos <= qpos
    if window is not None:
        m = jnp.logical_and(m, qpos - kpos < window)
    return m


def _tile_fwd(qv, kk, vv, q_start, k_start, n_free, scale, window, m0, l0):
    tq = qv.shape[0]
    W = kk.shape[0]
    c = scale * LOG2_E
    parts = []
    if n_free > 0:
        parts.append((lax.dot_general(qv, kk[:n_free], _DIMS["nt"], preferred_element_type=F32), vv[:n_free]))
    if W > n_free:
        s = lax.dot_general(qv, kk[n_free:], _DIMS["nt"], preferred_element_type=F32)
        s = jnp.where(_mask(q_start, k_start + n_free, tq, W - n_free, window), s, NEG_INF)
        parts.append((s, vv[n_free:]))
    m = None if m0 is None else m0 * (1.0 / scale)
    for s, _ in parts:
        mx = jnp.max(s, axis=-1, keepdims=True)
        m = mx if m is None else jnp.maximum(m, mx)
    l = None if l0 is None else l0 * jnp.exp2((m0 * (1.0 / scale) - m) * c)
    o = None
    for s, vpart in parts:
        p = jnp.exp2((s - m) * c)
        ps = jnp.sum(p, axis=-1, keepdims=True)
        l = ps if l is None else l + ps
        po = jnp.dot(p.astype(BF16), vpart, preferred_element_type=F32)
        o = po if o is None else o + po
    return o / l, m * scale + jnp.log(l)


def _tile_bwd(qv, kk, vv, dof, ov, lse, q_start, k_start, n_free, scale, window):
    tq = qv.shape[0]
    W = kk.shape[0]
    dob = dof.astype(BF16)
    delta = jnp.sum(dof * ov, axis=-1, keepdims=True)
    c = scale * LOG2_E
    lse2 = lse * LOG2_E
    dq = None
    outs = []
    for (a, b, masked) in ((0, n_free, False), (n_free, W, True)):
        if b <= a:
            continue
        kp, vp = kk[a:b], vv[a:b]
        s = lax.dot_general(qv, kp, _DIMS["nt"], preferred_element_type=F32)
        if masked:
            s = jnp.where(_mask(q_start, k_start + a, tq, b - a, window), s, NEG_INF)
        p = jnp.exp2(s * c - lse2)
        dp = lax.dot_general(dob, vp, _DIMS["nt"], preferred_element_type=F32)
        ds = (p * ((dp - delta) * scale)).astype(BF16)
        d = jnp.dot(ds, kp, preferred_element_type=F32)
        dq = d if dq is None else dq + d
        dkp = lax.dot_general(ds, qv, _DIMS["tn"], preferred_element_type=F32)
        dvp = lax.dot_general(p.astype(BF16), dob, _DIMS["tn"], preferred_element_type=F32)
        outs.append((a, dkp, dvp))
    return dq, outs


def _attn_fwd(q, k, v, sinks, *, B, S, HQ, HK, window, scale, tq, band, name, xchg=None):
    G = HQ // HK
    nq = S // tq
    T = B * S
    has_sink = sinks is not None
    n_x = 0 if xchg is None else len(xchg[1])
    n_in = 4 if has_sink else 3

    def kern(*refs):
        xs, xd, sems = refs[n_in:n_in + n_x], refs[n_in + n_x + 2:n_in + 2 * n_x + 2], refs[n_in + 2 * n_x + 2:]
        refs = refs[:n_in] + refs[n_in + n_x:n_in + n_x + 2]
        if n_x:
            _xchg_at_ends(xchg[0], xs, xd, sems, (B, HQ, nq), True)
        if has_sink:
            q_ref, k_ref, v_ref, s_ref, o_ref, lse_ref = refs
        else:
            q_ref, k_ref, v_ref, o_ref, lse_ref = refs
        q_start = pl.program_id(2) * tq
        qv = q_ref[...]
        if has_sink:
            m0 = jnp.broadcast_to(s_ref[0][:, :1], (tq, 1))
            l0 = jnp.ones((tq, 1), F32)
        else:
            m0 = l0 = None

        def finish(o, lse):
            o_ref[...] = o.astype(o_ref.dtype)
            lse_ref[...] = jnp.broadcast_to(lse, (tq, LANES))

        if window is None:
            bidx = q_start // band
            for bb in range(S // band):
                @pl.when(bidx == bb)
                def _(bb=bb):
                    W = (bb + 1) * band
                    finish(*_tile_fwd(qv, k_ref[0:W, :], v_ref[0:W, :], q_start, 0, bb * band, scale, None, m0, l0))
        else:
            sub = min(tq, WINDOW_SUB)
            W = min(S, sub + window)
            for r in range(0, tq, sub):
                k_start = pl.multiple_of(jnp.maximum(q_start + r - window, 0), window)
                o, lse = _tile_fwd(qv[r:r + sub], k_ref[pl.ds(k_start, W), :], v_ref[pl.ds(k_start, W), :], q_start + r,
                                   k_start, 0, scale, window, None if m0 is None else m0[:sub], None if l0 is None else l0[:sub])
                o_ref[r:r + sub, :] = o.astype(o_ref.dtype)
                lse_ref[r:r + sub, :] = jnp.broadcast_to(lse, (sub, LANES))
        if n_x:
            _xchg_at_ends(xchg[0], xs, xd, sems, (B, HQ, nq), False)

    q_spec = pl.BlockSpec((tq, LANES), lambda b, h, i: (b * nq + i, h))
    kv_spec = pl.BlockSpec((S, LANES), lambda b, h, i: (b, h // G))
    in_specs = [q_spec, kv_spec, kv_spec]
    args = [q, k, v]
    if has_sink:
        in_specs.append(pl.BlockSpec((1, 1, LANES), lambda b, h, i: (h, 0, 0)))
        args.append(sinks)
    xs = [] if xchg is None else list(xchg[1])
    res = pl.pallas_call(
        kern, name=name, grid=(B, HQ, nq), in_specs=in_specs + [_ANY] * n_x, out_specs=[q_spec, q_spec] + [_ANY] * n_x,
        out_shape=[jax.ShapeDtypeStruct((T, HQ * LANES), BF16), jax.ShapeDtypeStruct((T, HQ * LANES), F32)]
        + (_xchg_out_shapes(xchg[0], xs) if n_x else []),
        scratch_shapes=_xchg_scratch(xchg[0], n_x) if n_x else [],
        compiler_params=_params(("arbitrary",) * 3 if n_x else ("parallel", "parallel", "arbitrary")),
    )(*args, *xs)
    return ((res[0], res[1]), list(res[2:])) if n_x else res


def _attn_bwd(q, k, v, o, lse, do, sinks, *, B, S, HQ, HK, window, scale, tq, band, name, xchg=None):
    G = HQ // HK
    nq = S // tq
    T = B * S
    has_sink = sinks is not None
    n_x = 0 if xchg is None else len(xchg[1])
    n_in, n_out = (7, 4) if has_sink else (6, 3)

    def kern(*refs):
        xs, xd = refs[n_in:n_in + n_x], refs[n_in + n_x + n_out:n_in + 2 * n_x + n_out]
        sems = refs[n_in + 2 * n_x + n_out:]
        refs = refs[:n_in] + refs[n_in + n_x:n_in + n_x + n_out]
        if n_x:
            _xchg_at_ends(xchg[0], xs, xd, sems, (B, HK, G, nq), True)
        if has_sink:
            q_ref, k_ref, v_ref, o_ref, lse_ref, do_ref, s_ref, dq_ref, dk_ref, dv_ref, ds_ref = refs
        else:
            q_ref, k_ref, v_ref, o_ref, lse_ref, do_ref, dq_ref, dk_ref, dv_ref = refs
        gi = pl.program_id(2)
        qi = pl.program_id(3)
        q_start = qi * tq

        @pl.when(jnp.logical_and(gi == 0, qi == 0))
        def _():
            dk_ref[...] = jnp.zeros_like(dk_ref)
            dv_ref[...] = jnp.zeros_like(dv_ref)

        qv = q_ref[...]
        dof = do_ref[...].astype(F32)
        ov = o_ref[...].astype(F32)
        lse_v = lse_ref[...][:, :1]
        if window is None:
            bidx = q_start // band
            for bb in range(S // band):
                @pl.when(bidx == bb)
                def _(bb=bb):
                    W = (bb + 1) * band
                    dq, outs = _tile_bwd(qv, k_ref[0:W, :], v_ref[0:W, :], dof, ov, lse_v, q_start, 0, bb * band, scale, None)
                    dq_ref[...] = dq.astype(dq_ref.dtype)
                    for a, dkp, dvp in outs:
                        dk_ref[a:a + dkp.shape[0], :] += dkp
                        dv_ref[a:a + dvp.shape[0], :] += dvp
        else:
            sub = min(tq, WINDOW_SUB)
            W = min(S, sub + window)
            parts = []
            for r in range(0, tq, sub):
                k_start = pl.multiple_of(jnp.maximum(q_start + r - window, 0), window)
                dq, outs = _tile_bwd(qv[r:r + sub], k_ref[pl.ds(k_start, W), :], v_ref[pl.ds(k_start, W), :],
                                     dof[r:r + sub], ov[r:r + sub], lse_v[r:r + sub], q_start + r, k_start, 0, scale, window)
                dq_ref[r:r + sub, :] = dq.astype(dq_ref.dtype)
                parts.append((k_start, outs[0][1], outs[0][2]))
            for k_start, dkp, dvp in parts:
                dk_ref[pl.ds(k_start, W), :] += dkp
                dv_ref[pl.ds(k_start, W), :] += dvp
        if has_sink:
            delta = jnp.sum(dof * ov, axis=-1, keepdims=True)
            sink = s_ref[0][:, :1]
            part = -jnp.sum(jnp.exp(sink - lse_v) * delta, axis=0, keepdims=True)
            part = jnp.broadcast_to(part, (1, LANES))

            @pl.when(qi == 0)
            def _():
                ds_ref[0] = part

            @pl.when(qi != 0)
            def _():
                ds_ref[0] += part
        if n_x:
            _xchg_at_ends(xchg[0], xs, xd, sems, (B, HK, G, nq), False)

    q_spec = pl.BlockSpec((tq, LANES), lambda b, hk, g, i: (b * nq + i, hk * G + g))
    kv_spec = pl.BlockSpec((S, LANES), lambda b, hk, g, i: (b, hk))
    in_specs = [q_spec, kv_spec, kv_spec, q_spec, q_spec, q_spec]
    args = [q, k, v, o, lse, do]
    out_specs = [q_spec, kv_spec, kv_spec]
    out_shape = [jax.ShapeDtypeStruct((T, HQ * LANES), BF16), jax.ShapeDtypeStruct((T, HK * LANES), F32),
                 jax.ShapeDtypeStruct((T, HK * LANES), F32)]
    if has_sink:
        in_specs.append(pl.BlockSpec((1, 1, LANES), lambda b, hk, g, i: (hk * G + g, 0, 0)))
        args.append(sinks)
        out_specs.append(pl.BlockSpec((1, 1, LANES), lambda b, hk, g, i: (b * HQ + hk * G + g, 0, 0)))
        out_shape.append(jax.ShapeDtypeStruct((B * HQ, 1, LANES), F32))
    xs = [] if xchg is None else list(xchg[1])
    res = pl.pallas_call(
        kern, name=name, grid=(B, HK, G, nq), in_specs=in_specs + [_ANY] * n_x, out_specs=out_specs + [_ANY] * n_x,
        out_shape=out_shape + (_xchg_out_shapes(xchg[0], xs) if n_x else []),
        scratch_shapes=_xchg_scratch(xchg[0], n_x) if n_x else [],
        compiler_params=_params(("arbitrary",) * 4 if n_x else ("parallel", "parallel", "arbitrary", "arbitrary")),
    )(*args, *xs)
    main = tuple(res[:n_out]) if has_sink else (*res[:n_out], None)
    return (main, list(res[n_out:])) if n_x else main


def _rope_tables(pos_col, freq, sign, name):
    T = pos_col.shape[0]
    tr = _tile(T, 1024)

    def kern(p_ref, f_ref, s_ref, c_out, s_out):
        ang = p_ref[...] * f_ref[...]
        c_out[...] = jnp.cos(ang)
        s_out[...] = jnp.sin(ang) * s_ref[...]

    spec = pl.BlockSpec((tr, LANES), lambda i: (i, 0))
    par = pl.BlockSpec((1, LANES), lambda i: (0, 0))
    return pl.pallas_call(
        kern, name=name, grid=(T // tr,), in_specs=[pl.BlockSpec((tr, 1), lambda i: (i, 0)), par, par],
        out_specs=[spec, spec], out_shape=[jax.ShapeDtypeStruct((T, LANES), F32)] * 2,
        compiler_params=_params(("parallel",)),
    )(pos_col, freq, sign)


def _ada_fwd(c8, ada_w, ada_b):
    nch, D, tn = ada_w.shape

    def kern(c_ref, w_ref, b_ref, o_ref):
        act = _silu(c_ref[...]).astype(BF16)
        o_ref[...] = jnp.dot(act, w_ref[...], preferred_element_type=F32) + b_ref[...]

    return pl.pallas_call(
        kern, name="ada_fwd", grid=(nch,),
        in_specs=[pl.BlockSpec((8, D), lambda j: (0, 0)), pl.BlockSpec((None, D, tn), lambda j: (j, 0, 0)),
                  pl.BlockSpec((1, tn), lambda j: (0, j))],
        out_specs=pl.BlockSpec((8, tn), lambda j: (0, j)),
        out_shape=jax.ShapeDtypeStruct((8, nch * tn), F32), compiler_params=_params(("parallel",)),
    )(c8, ada_w, ada_b)


def _ada_bwd(c8, dmod):
    _, N = dmod.shape
    D = c8.shape[1]
    tn = N // N_CHIPS

    def kern(c_ref, d_ref, gw_ref, gb_ref):
        act = _silu(c_ref[...]).astype(BF16)
        d = d_ref[...]
        gw_ref[...] = lax.dot_general(act, d.astype(BF16), _DIMS["tn"], preferred_element_type=F32).astype(BF16)
        gb_ref[...] = jnp.sum(d, axis=0, keepdims=True)

    return pl.pallas_call(
        kern, name="ada_bwd", grid=(N_CHIPS,),
        in_specs=[pl.BlockSpec((8, D), lambda j: (0, 0)), pl.BlockSpec((8, tn), lambda j: (0, j))],
        out_specs=[pl.BlockSpec((None, D, tn), lambda j: (j, 0, 0)), pl.BlockSpec((1, tn), lambda j: (0, j))],
        out_shape=[jax.ShapeDtypeStruct((N_CHIPS, D, tn), BF16), jax.ShapeDtypeStruct((1, N), F32)],
        compiler_params=_params(("parallel",)),
    )(c8, dmod)


def _adamw(w, ga, gb, m, v, name):
    rows, cols = w.shape
    tr = _row_div(rows, 256)
    two = gb is not None

    def kern(*refs):
        if two:
            w_ref, ga_ref, gb_ref, m_ref, v_ref, g_out, d_out, m_out, v_out = refs
            gv = ga_ref[...] + gb_ref[...]
        else:
            w_ref, ga_ref, m_ref, v_ref, g_out, d_out, m_out, v_out = refs
            gv = ga_ref[...]
        mn = ADAM_B1 * m_ref[...] + (1.0 - ADAM_B1) * gv
        vn = ADAM_B2 * v_ref[...] + (1.0 - ADAM_B2) * jnp.square(gv)
        m_hat = mn / (1.0 - ADAM_B1 ** ADAM_STEP)
        v_hat = vn / (1.0 - ADAM_B2 ** ADAM_STEP)
        g_out[...] = gv
        d_out[...] = -ADAM_LR * (m_hat / (jnp.sqrt(v_hat) + ADAM_EPS) + ADAM_WD * w_ref[...])
        m_out[...] = mn
        v_out[...] = vn

    spec = pl.BlockSpec((tr, cols), lambda i: (i, 0))
    args = [w, ga, gb, m, v] if two else [w, ga, m, v]
    return pl.pallas_call(
        kern, name=name, grid=(rows // tr,), in_specs=[spec] * len(args), out_specs=[spec] * 4,
        out_shape=[jax.ShapeDtypeStruct((rows, cols), F32)] * 4, compiler_params=_params(("parallel",)),
    )(*args)


def _sum_slots(x, name):
    n, rows, cols = x.shape
    tr = _row_div(rows, 256)

    def kern(x_ref, o_ref):
        acc = x_ref[0].astype(F32)
        for j in range(1, n):
            acc = acc + x_ref[j].astype(F32)
        o_ref[...] = acc

    return pl.pallas_call(
        kern, name=name, grid=(rows // tr,), in_specs=[pl.BlockSpec((n, tr, cols), lambda i: (0, i, 0))],
        out_specs=pl.BlockSpec((tr, cols), lambda i: (i, 0)), out_shape=jax.ShapeDtypeStruct((rows, cols), F32),
        compiler_params=_params(("parallel",)),
    )(x)


def _all_reduce_small(blob, name):
    R, C = blob.shape

    def kern(src, out, pair, chips, send_sems, recv_sems):
        x, y, c = lax.axis_index("x"), lax.axis_index("y"), lax.axis_index("c")
        me = 2 * x + y
        to_sibling = pltpu.make_async_remote_copy(
            src_ref=src, dst_ref=pair, send_sem=send_sems.at[0], recv_sem=recv_sems.at[0],
            device_id=(x, y, 1 - c), device_id_type=_MESH)
        to_sibling.start()
        to_sibling.wait()
        chips[me] = src[...] + pair[...]
        peers = [(1 - x, y), (x, 1 - y), (1 - x, 1 - y)]
        copies = [pltpu.make_async_remote_copy(
            src_ref=chips.at[me], dst_ref=chips.at[me], send_sem=send_sems.at[1 + kk], recv_sem=recv_sems.at[1 + kk],
            device_id=(px, py, c), device_id_type=_MESH) for kk, (px, py) in enumerate(peers)]
        for cp in copies:
            cp.start()
        for kk, (px, py) in enumerate(peers):
            pltpu.make_async_remote_copy(
                src_ref=chips.at[me], dst_ref=chips.at[2 * px + py], send_sem=send_sems.at[1 + kk],
                recv_sem=recv_sems.at[1 + kk], device_id=(px, py, c), device_id_type=_MESH).wait_recv()
        for cp in copies:
            cp.wait_send()
        acc = chips[0]
        for j in range(1, N_CHIPS):
            acc = acc + chips[j]
        out[...] = acc

    vm = pl.BlockSpec(memory_space=pltpu.VMEM)
    return pl.pallas_call(
        kern, name=name, in_specs=[vm], out_specs=vm, out_shape=jax.ShapeDtypeStruct((R, C), F32),
        scratch_shapes=[pltpu.VMEM((R, C), F32), pltpu.VMEM((N_CHIPS, R, C), F32), pltpu.SemaphoreType.DMA((4,)),
                        pltpu.SemaphoreType.DMA((4,))],
        compiler_params=pltpu.CompilerParams(vmem_limit_bytes=VMEM_LIMIT),
    )(blob)


def _row_tile(S, wide=False):
    for t in ((256,) if wide else (512, 256)):
        if S % t == 0:
            return t
    return 128


def _attn_tiles(S):
    return min(S, 512), min(S, 256), min(S, 512)


def _hosted(fn, host, got, kind, name, *args, **kw):
    if host is not None and host.get(name):
        res, xs = fn(*args, name=name, xchg=(kind, host[name]), **kw)
        got[name] = xs
        return res
    return fn(*args, name=name, **kw)


def _mm(host, got, kind, a, b, mode, out_dtype, name, **kw):
    return _hosted(_matmul, host, got, kind, name, a, b, mode, out_dtype, **kw)


def _merge_late(w, late, got, name):
    if not late or name not in late:
        return w
    return {**w, **_large_operands(dict(zip(late[name], got[name])))}


def _sgu_rows(S):
    return 4 * C_CHUNK if S % (4 * C_CHUNK) == 0 else C_CHUNK


def _layer_fwd(xin, prev, mods, w, tabs, B, S, host=None, late=None):
    T = B * S
    tr = _row_tile(S)
    sh1, sc1, g1, sh2, sc2, g2 = mods
    ca, sa, cb, sb = tabs
    sv = {}
    got = {}
    if prev is None:
        def body(rv, ev, pv):
            return [_f_norm_mod(rv[0], pv[0], ev[0], ev[1])], [], []
        (h,), _, _ = _rowcall(body, name="f_norm_mod", T=T, S=S, tr=tr, rows=[(xin, D_MODEL, 0)], exs=[sc1, sh1],
                              pars=[w["norm1_w"]], row_outs=[(D_MODEL, BF16, D_MODEL, 0)])
        x = xin
    else:
        x1p, dp, g2p = prev

        def body(rv, ev, pv):
            xn, hh = _f_resid_norm_mod(rv[0], rv[1], ev[0], pv[0], ev[1], ev[2])
            return [xn, hh], [], []
        (x, h), _, _ = _rowcall(body, name="f_resid_norm_mod1", T=T, S=S, tr=tr,
                                rows=[(x1p, D_MODEL, 0), (dp, D_MODEL, 0)], exs=[g2p, sc1, sh1], pars=[w["norm1_w"]],
                                row_outs=[(D_MODEL, F32, D_MODEL, 0), (D_MODEL, BF16, D_MODEL, 0)])
    sv["x"], sv["h"] = x, h
    proj = _mm(host, got, "gather", h, w["w_in"], "nn", BF16, "mm_in")
    sv["proj"] = proj
    w = _merge_late(w, late, got, "mm_in")

    def body(rv, ev, pv):
        outs = _f_mixprep(rv[0], rv[1], rv[2], rv[3], rv[4], pv[0], pv[1])
        return list(outs), [], []
    (qa, ka, va, cqn, ckvn, kr), _, _ = _rowcall(
        body, name="f_mixprep", T=T, S=S, tr=tr,
        rows=[(proj, P_CU, 0), (ca, LANES, 0), (sa, LANES, 0), (cb, LANES, 0), (sb, LANES, 0)],
        pars=[w["b_q_norm_w"], w["b_kv_norm_w"]],
        row_outs=[(768, BF16, 768, 0), (256, BF16, 256, 0), (256, BF16, 256, 0), (384, BF16, 384, 0),
                  (256, BF16, 256, 0), (LANES, F32, LANES, 0)])
    sv.update(qa=qa, ka=ka, va=va, cqn=cqn, ckvn=ckvn)
    q = _matmul(cqn, w["b_w_uq"], "nn", BF16, "mm_uq")
    kv = _matmul(ckvn, w["b_w_ukv"], "nn", BF16, "mm_ukv")

    def body(rv, ev, pv):
        Q, K, V = _f_mlaprep(rv[0], rv[1], rv[2], rv[3], rv[4])
        return [Q, K, V], [], []
    (Q, K, V), _, _ = _rowcall(
        body, name="f_mlaprep", T=T, S=S, tr=tr,
        rows=[(q, 768, 0), (kv, 1536, 0), (kr, LANES, 0), (cb, LANES, 0), (sb, LANES, 0)],
        row_outs=[(768, BF16, 768, 0)] * 3)
    sv.update(Q=Q, K=K, V=V)
    ta, tb, band = _attn_tiles(S)
    oa, lse_a = _hosted(_attn_fwd, host, got, "gather", "attn_a_fwd", qa, ka, va, w["sinks"], B=B, S=S, HQ=A_Q_HEADS,
                        HK=A_KV_HEADS, window=A_WINDOW, scale=HEAD_DIM ** -0.5, tq=ta, band=None)
    ob, lse_b = _hosted(_attn_fwd, host, got, "gather", "attn_b_fwd", Q, K, V, None, B=B, S=S, HQ=B_HEADS, HK=B_HEADS,
                        window=None, scale=(B_NOPE + B_ROPE) ** -0.5, tq=tb, band=band)
    sv.update(oa=oa, lse_a=lse_a, ob=ob, lse_b=lse_b)
    w = _merge_late(w, late, got, "attn_a_fwd")
    ts = _sgu_rows(S)

    def body(rv, ev, pv):
        outs = [_f_sgu(rv[0][r:r + C_CHUNK], rv[1][r:r + C_CHUNK], pv[0], pv[1], pv[2], pv[3])
                for r in range(0, ts, C_CHUNK)]
        return [jnp.concatenate(outs, axis=0)], [], []
    (yc,), _, _ = _rowcall(body, name="f_sgu", T=T, S=S, tr=ts,
                           rows=[(proj, C_WIDTH, P_CU // C_WIDTH), (proj, C_WIDTH, P_CV // C_WIDTH)],
                           pars=[w["c_ln_w"], w["c_ln_b"], w["c_w_s"], w["c_b_col"]],
                           row_outs=[(C_WIDTH, F32, C_WIDTH, 0)])
    sv["yc"] = yc

    def body(rv, ev, pv):
        return [_f_outnorm(rv[0], rv[1], rv[2], pv[0])], [], []
    (y,), _, _ = _rowcall(body, name="f_outnorm", T=T, S=S, tr=tr,
                          rows=[(oa, 768, 0), (ob, 768, 0), (yc, C_WIDTH, 0)], pars=[w["out_norm_w"]],
                          row_outs=[(D_MODEL, BF16, D_MODEL, 0)])
    sv["y"] = y
    o = _mm(host, got, "gather", y, w["w_out"], "nn", F32, "mm_out")
    sv["o"] = o

    def body(rv, ev, pv):
        xn, hh = _f_resid_norm_mod(rv[0], rv[1], ev[0], pv[0], ev[1], ev[2])
        return [xn, hh], [], []
    (x1, h2), _, _ = _rowcall(body, name="f_resid_norm_mod2", T=T, S=S, tr=tr,
                              rows=[(x, D_MODEL, 0), (o, D_MODEL, 0)], exs=[g1, sc2, sh2], pars=[w["norm2_w"]],
                              row_outs=[(D_MODEL, F32, D_MODEL, 0), (D_MODEL, BF16, D_MODEL, 0)])
    sv["h2"] = h2
    gu = _mm(host, got, "gather", h2, w["w_gate_up"], "nn", BF16, "mm_gate_up", b_chunks=True)
    sv["gu"] = gu

    def body(rv, ev, pv):
        return [_f_swiglu(rv[0], rv[1])], [], []
    (act,), _, _ = _rowcall(body, name="f_swiglu", T=T, S=S, tr=_row_tile(S, wide=True),
                            rows=[(gu, FFN_HIDDEN, 0), (gu, FFN_HIDDEN, 1)], row_outs=[(FFN_HIDDEN, BF16, FFN_HIDDEN, 0)])
    sv["act"] = act
    d = _mm(host, got, "gather", act, w["w_down"], "nn", F32, "mm_down")
    sv["x1"], sv["d"], sv["w"] = x1, d, w
    return (x1, d, g2), sv, got


def _final(x1, d, g2, fw, target, B, S):
    T = B * S
    tr = _row_tile(S)

    def loss_fn(x1v, dv, gv, wv, tv):
        yf = _rms(x1v + gv * dv, wv)
        return 0.5 * jnp.sum(jnp.mean(jnp.square(yf - tv), axis=-1))

    def body(rv, ev, pv):
        x1v, dv, tv = rv
        val, vjp = jax.vjp(lambda a, b_, g, ww: loss_fn(a, b_, g, ww, tv), x1v, dv, ev[0], pv[0])
        dx1, dd, dg, dw = vjp(jnp.ones((), F32))
        return [dx1, dd], [dg], [dw, jnp.full((1, LANES), val, F32)]
    (dx1, dd), (dg2,), (dfw, loss) = _rowcall(
        body, name="final_loss", T=T, S=S, tr=tr, rows=[(x1, D_MODEL, 0), (d, D_MODEL, 0), (target, D_MODEL, 0)],
        exs=[g2], pars=[fw], row_outs=[(D_MODEL, F32, D_MODEL, 0), (D_MODEL, BF16, D_MODEL, 0)],
        ex_outs=[D_MODEL], par_outs=[(1, D_MODEL), (1, LANES)])
    return loss, dx1, dd, dg2, dfw


def _layer_bwd(sv, prev, mods, tabs, dx1, dd, B, S, host=None, own_early=False):
    w = sv["w"]
    T = B * S
    tr = _row_tile(S)
    sh1, sc1, g1, sh2, sc2, g2 = mods
    ca, sa, cb, sb = tabs
    gr = {}
    got = {}
    dact = _mm(host, got, "a2a", dd, w["w_down"], "nt", BF16, "mm_down_dx")
    gr["w_down"] = _matmul(sv["act"], dd, "tn", BF16, "mm_down_dw")

    def body(rv, ev, pv):
        _, vjp = jax.vjp(_f_swiglu, rv[0], rv[1])
        dgate, dup = vjp(rv[2])
        return [[dgate, dup]], [], []
    (dgu,), _, _ = _rowcall(body, name="b_swiglu", T=T, S=S, tr=_row_tile(S, wide=True),
                            rows=[(sv["gu"], FFN_HIDDEN, 0), (sv["gu"], FFN_HIDDEN, 1), (dact, FFN_HIDDEN, 0)],
                            row_outs=[(2 * FFN_HIDDEN, BF16, 2 * FFN_HIDDEN, 0)])
    dh2 = _mm(host, got, "a2a", dgu, w["w_gate_up"], "nt", BF16, "mm_gate_up_dx", b_chunks=True)
    gr["w_gate_up"] = _mm(host, got, "a2a", sv["h2"], dgu, "tn", BF16, "mm_gate_up_dw", out_chunks=True)

    def body(rv, ev, pv):
        xa, delta, dh, dxn = rv
        _, vjp = jax.vjp(_f_resid_norm_mod, xa, delta, ev[0], pv[0], ev[1], ev[2])
        dxa, ddelta, dg, dw, dsc, dsh = vjp((dxn, dh))
        return [dxa, ddelta], [dg, dsc, dsh], [dw]
    (dx, do), (dg1, dsc2, dsh2), (gr["norm2_w"],) = _rowcall(
        body, name="b_resid_norm_mod2", T=T, S=S, tr=tr,
        rows=[(sv["x"], D_MODEL, 0), (sv["o"], D_MODEL, 0), (dh2, D_MODEL, 0), (dx1, D_MODEL, 0)],
        exs=[g1, sc2, sh2], pars=[w["norm2_w"]],
        row_outs=[(D_MODEL, F32, D_MODEL, 0), (D_MODEL, BF16, D_MODEL, 0)], ex_outs=[D_MODEL] * 3,
        par_outs=[(1, D_MODEL)])
    dy = _matmul(do, w["w_out"], "nt", BF16, "mm_out_dx")
    gr["w_out"] = _matmul(sv["y"], do, "tn", BF16, "mm_out_dw")

    def body(rv, ev, pv):
        _, vjp = jax.vjp(_f_outnorm, rv[0], rv[1], rv[2], pv[0])
        doa, dob, dyc, dgw = vjp(rv[3])
        return [doa, dob, dyc], [], [dgw]
    (doa, dob, dyc), _, (gr["out_norm_w"],) = _rowcall(
        body, name="b_outnorm", T=T, S=S, tr=tr,
        rows=[(sv["oa"], 768, 0), (sv["ob"], 768, 0), (sv["yc"], C_WIDTH, 0), (dy, D_MODEL, 0)], pars=[w["out_norm_w"]],
        row_outs=[(768, BF16, 768, 0), (768, BF16, 768, 0), (C_WIDTH, F32, C_WIDTH, 0)], par_outs=[(1, D_MODEL)])

    ta, tb, band = _attn_tiles(S)
    if own_early:
        host = dict(host)
        host["attn_b_bwd"] = list(host.get("attn_b_bwd", ())) + [
            gr["w_gate_up"], gr["w_down"].reshape(N_CHIPS, FFN_HIDDEN // N_CHIPS, D_MODEL)]
    dQ, dK, dV, _ = _hosted(_attn_bwd, host, got, "a2a", "attn_b_bwd", sv["Q"], sv["K"], sv["V"], sv["ob"], sv["lse_b"],
                            dob, None, B=B, S=S, HQ=B_HEADS, HK=B_HEADS, window=None,
                            scale=(B_NOPE + B_ROPE) ** -0.5, tq=tb, band=band)
    dqa, dka, dva, dsink = _hosted(_attn_bwd, host, got, "a2a", "attn_a_bwd", sv["qa"], sv["ka"], sv["va"], sv["oa"],
                                   sv["lse_a"], doa, w["sinks"], B=B, S=S, HQ=A_Q_HEADS, HK=A_KV_HEADS,
                                   window=A_WINDOW, scale=HEAD_DIM ** -0.5, tq=ta, band=None)
    gr["sinks"] = dsink

    def body(rv, ev, pv):
        dQv, dKv, dVv, cbv, sbv = rv
        dq = [_rope_bwd((cbv, sbv), p)[0] for p in _heads(dQv, B_HEADS)]
        dkr = None
        for p in _heads(dKv, B_HEADS):
            dkr = p if dkr is None else dkr + p
        return [dq, [dKv, dVv], dkr], [], []
    (dq, dkv, dkr), _, _ = _rowcall(
        body, name="b_mlaprep", T=T, S=S, tr=tr,
        rows=[(dQ, 768, 0), (dK, 768, 0), (dV, 768, 0), (cb, LANES, 0), (sb, LANES, 0)],
        row_outs=[(768, BF16, 768, 0), (1536, BF16, 1536, 0), (LANES, F32, LANES, 0)])
    dcqn = _matmul(dq, w["b_w_uq"], "nt", BF16, "mm_uq_dx")
    gr["b_w_uq"] = _matmul(sv["cqn"], dq, "tn", BF16, "mm_uq_dw")
    dckvn = _matmul(dkv, w["b_w_ukv"], "nt", BF16, "mm_ukv_dx")
    gr["b_w_ukv"] = _matmul(sv["ckvn"], dkv, "tn", BF16, "mm_ukv_dw")

    def body(rv, ev, pv):
        proj, cav, sav, cbv, sbv, dqa_, dka_, dva_, dcqn_, dckvn_, dkr_ = rv
        _, vjp = jax.vjp(lambda p, a, b_: _f_mixprep(p, cav, sav, cbv, sbv, a, b_), proj, pv[0], pv[1])
        dproj, dqn, dkvn = vjp((dqa_, dka_, dva_, dcqn_, dckvn_, dkr_))
        return [dproj], [], [dqn, dkvn]
    (dproj,), _, (gr["b_q_norm_w"], gr["b_kv_norm_w"]) = _rowcall(
        body, name="b_mixprep", T=T, S=S, tr=tr,
        rows=[(sv["proj"], P_CU, 0), (ca, LANES, 0), (sa, LANES, 0), (cb, LANES, 0), (sb, LANES, 0),
              (dqa, 768, 0), (dka, 256, 0), (dva, 256, 0), (dcqn, 384, 0), (dckvn, 256, 0), (dkr, LANES, 0)],
        pars=[w["b_q_norm_w"], w["b_kv_norm_w"]], row_outs=[(P_END, BF16, P_CU, 0)],
        par_outs=[(1, B_Q_RANK), (1, B_KV_RANK)])

    ts = _sgu_rows(S)

    def body(rv, ev, pv):
        cu, cv, dycv, _ = rv
        dcus, dcvs, acc = [], [], None
        for r in range(0, ts, C_CHUNK):
            _, vjp = jax.vjp(_f_sgu, cu[r:r + C_CHUNK], cv[r:r + C_CHUNK], pv[0], pv[1], pv[2], pv[3])
            dcu, dcv, *dpar = vjp(dycv[r:r + C_CHUNK])
            dcus.append(dcu)
            dcvs.append(dcv)
            acc = dpar if acc is None else [a + b_ for a, b_ in zip(acc, dpar)]
        return [[jnp.concatenate(dcus, axis=0), jnp.concatenate(dcvs, axis=0)]], [], acc
    (dproj,), _, (gr["c_ln_w"], gr["c_ln_b"], gr["c_w_s"], gr["c_b_col"]) = _rowcall(
        body, name="b_sgu", T=T, S=S, tr=ts,
        rows=[(sv["proj"], C_WIDTH, P_CU // C_WIDTH), (sv["proj"], C_WIDTH, P_CV // C_WIDTH), (dyc, C_WIDTH, 0),
              (dproj, 2 * C_WIDTH, P_CU // (2 * C_WIDTH))],
        pars=[w["c_ln_w"], w["c_ln_b"], w["c_w_s"], w["c_b_col"]],
        row_outs=[(P_END, BF16, 2 * C_WIDTH, P_CU // (2 * C_WIDTH))],
        par_outs=[(1, C_WIDTH), (1, C_WIDTH), (C_GROUPS, C_CHUNK, C_CHUNK), (C_GROUPS, C_CHUNK, 1)],
        aliases={3: 0})
    dh = _mm(host, got, "a2a", dproj, w["w_in"], "nt", BF16, "mm_in_dx")
    gr["w_in"] = _matmul(sv["h"], dproj, "tn", BF16, "mm_in_dw")

    if prev is None:
        def body(rv, ev, pv):
            xv, dhv, dxd = rv
            _, vjp = jax.vjp(_f_norm_mod, xv, pv[0], ev[0], ev[1])
            dxa, dw, dsc, dsh = vjp(dhv)
            return [dxa + dxd], [dsc, dsh], [dw]
        (dxin,), (dsc1, dsh1), (gr["norm1_w"],) = _rowcall(
            body, name="b_norm_mod", T=T, S=S, tr=tr, rows=[(sv["x"], D_MODEL, 0), (dh, D_MODEL, 0), (dx, D_MODEL, 0)],
            exs=[sc1, sh1], pars=[w["norm1_w"]], row_outs=[(D_MODEL, F32, D_MODEL, 0)], ex_outs=[D_MODEL] * 2,
            par_outs=[(1, D_MODEL)])
        nxt = (dxin, None, None)
    else:
        x1p, dp, g2p = prev

        def body(rv, ev, pv):
            xa, delta, dhv, dxn = rv
            _, vjp = jax.vjp(_f_resid_norm_mod, xa, delta, ev[0], pv[0], ev[1], ev[2])
            dxa, ddelta, dg, dw, dsc, dsh = vjp((dxn, dhv))
            return [dxa, ddelta], [dg, dsc, dsh], [dw]
        (dx1p, ddp), (dg2p, dsc1, dsh1), (gr["norm1_w"],) = _rowcall(
            body, name="b_resid_norm_mod1", T=T, S=S, tr=tr,
            rows=[(x1p, D_MODEL, 0), (dp, D_MODEL, 0), (dh, D_MODEL, 0), (dx, D_MODEL, 0)],
            exs=[g2p, sc1, sh1], pars=[w["norm1_w"]],
            row_outs=[(D_MODEL, F32, D_MODEL, 0), (D_MODEL, BF16, D_MODEL, 0)], ex_outs=[D_MODEL] * 3,
            par_outs=[(1, D_MODEL)])
        nxt = (dx1p, ddp, dg2p)
    return gr, (dsh1, dsc1, dg1, dsh2, dsc2), nxt, got


def _lane_table(lanes_neg, lanes_pos, inv):
    freq = np.zeros((LANES,), np.int64) - 1
    sign = np.zeros((1, LANES), np.float32)
    n = len(lanes_neg)
    freq[lanes_neg] = np.arange(n)
    freq[lanes_pos] = np.arange(n)
    sign[0, lanes_neg] = -1.0
    sign[0, lanes_pos] = 1.0
    return _select_axis(inv, freq, 0).reshape(1, LANES), jnp.asarray(sign)


SHARDED = ("ada_w", "w_in", "b_w_uq", "b_w_ukv", "w_out", "w_gate_up", "w_down")
ROW_SHARDED = ("w_out", "w_down")
SMALL = ("ada_b", "norm1_w", "a_sinks", "b_q_norm_w", "b_kv_norm_w", "c_ln_w", "c_ln_b", "c_w_s", "c_b_s",
         "out_norm_w", "norm2_w", "final_norm_w")
FWD_HOST = {"attn_b_fwd": ("ada_w", "w_gate_up"), "attn_a_fwd": ("w_down", "w_in"),
            "mm_gate_up": ("w_out", "b_w_uq", "b_w_ukv")}
FWD_HOST_FIRST = {"attn_b_fwd": ("ada_w", "w_gate_up"), "mm_gate_up": ("w_down", "w_out", "b_w_uq", "b_w_ukv"),
                  "mm_down": ("w_in",)}
BWD_HOST = {"attn_a_bwd": ("w_in", "w_out", "b_w_uq", "b_w_ukv"), "attn_b_bwd": ("ada_w", "w_gate_up", "w_down")}
FIRST_GATHER = ("ada_w", "w_in")
FIRST_LATE = {"mm_in": ("b_w_uq", "b_w_ukv", "w_out"), "attn_a_fwd": ("w_gate_up", "w_down")}
EXPOSED = ("ada_w", "w_in", "b_w_uq", "b_w_ukv", "w_out")
OWN_LAYER = ("w_gate_up", "w_down")
assert not set(FIRST_LATE) & set(FWD_HOST_FIRST)


def _from_host(table, got):
    return {k: got[name][i] for name, ks in table.items() for i, k in enumerate(ks)}


def _to_host(table, arrays):
    return {name: [arrays[k] for k in ks] for name, ks in table.items()}


def _join_cols(g):
    return jnp.concatenate([g[j] for j in range(N_CHIPS)], axis=1)


def _split_cols(g):
    n = g.shape[1] // N_CHIPS
    return jnp.stack([g[:, j * n:(j + 1) * n] for j in range(N_CHIPS)])


def _layer_weights(G, small, l):
    D = D_MODEL
    return {
        **_large_operands({k: v for k, v in G.items() if k != "ada_w"}),
        "norm1_w": small["norm1_w"][l].reshape(1, D),
        "sinks": jnp.broadcast_to(small["a_sinks"][l].reshape(A_Q_HEADS, 1, 1), (A_Q_HEADS, 1, LANES)),
        "b_q_norm_w": small["b_q_norm_w"][l].reshape(1, B_Q_RANK),
        "b_kv_norm_w": small["b_kv_norm_w"][l].reshape(1, B_KV_RANK),
        "c_ln_w": small["c_ln_w"][l].reshape(1, C_WIDTH), "c_ln_b": small["c_ln_b"][l].reshape(1, C_WIDTH),
        "c_w_s": small["c_w_s"][l], "c_b_col": small["c_b_s"][l].reshape(C_GROUPS, C_CHUNK, 1),
        "out_norm_w": small["out_norm_w"][l].reshape(1, D), "norm2_w": small["norm2_w"][l].reshape(1, D),
    }


def _large_operands(G):
    D = D_MODEL
    make = {
        "w_in": lambda g: _pad_axis(_join_cols(g), _map_w_in(), 1),
        "b_w_uq": lambda g: _pad_axis(_join_cols(g), _map_w_uq(), 1),
        "b_w_ukv": lambda g: _pad_axis(_join_cols(g), _map_w_ukv(), 1),
        "w_out": lambda g: g.reshape(D, D), "w_gate_up": lambda g: g, "w_down": lambda g: g.reshape(FFN_HIDDEN, D),
    }
    return {k: make[k](g) for k, g in G.items()}


def _send_buffers(gr, ada_gw):
    D = D_MODEL
    return {
        "ada_w": ada_gw, "w_gate_up": gr["w_gate_up"], "w_down": gr["w_down"].reshape(N_CHIPS, FFN_HIDDEN // N_CHIPS, D),
        "w_out": gr["w_out"].reshape(N_CHIPS, D // N_CHIPS, D),
        "w_in": _split_cols(_unpad_axis(gr["w_in"], _map_w_in(), IN_COLS, 1)),
        "b_w_uq": _split_cols(_unpad_axis(gr["b_w_uq"], _map_w_uq(), B_HEADS * (B_NOPE + B_ROPE), 1)),
        "b_w_ukv": _split_cols(_unpad_axis(gr["b_w_ukv"], _map_w_ukv(), B_HEADS * (B_NOPE + B_V), 1)),
    }


def _small_grads(gr, ada_gb, B):
    D = D_MODEL
    return {
        "ada_b": ada_gb.reshape(N_MOD * D), "norm1_w": gr["norm1_w"].reshape(D),
        "a_sinks": gr["sinks"][:, 0, 0].reshape(B, A_Q_HEADS).sum(axis=0),
        "b_q_norm_w": gr["b_q_norm_w"].reshape(B_Q_RANK), "b_kv_norm_w": gr["b_kv_norm_w"].reshape(B_KV_RANK),
        "c_ln_w": gr["c_ln_w"].reshape(C_WIDTH), "c_ln_b": gr["c_ln_b"].reshape(C_WIDTH), "c_w_s": gr["c_w_s"],
        "c_b_s": gr["c_b_col"].reshape(C_GROUPS, C_CHUNK),
        "out_norm_w": gr["out_norm_w"].reshape(D), "norm2_w": gr["norm2_w"].reshape(D),
    }


def _step(x, c, positions, target, small, shard_of, gathered=None):
    dist = gathered is None
    B, S, D = x.shape
    T = B * S
    xt = x.reshape(T, D)
    tgt = target.reshape(T, D)
    pos_col = positions.astype(F32).reshape(T, 1)
    inv_a = 1.0 / (ROPE_THETA ** (jnp.arange(0, HEAD_DIM, 2, dtype=F32) / HEAD_DIM))
    inv_b = 1.0 / (ROPE_THETA ** (jnp.arange(0, B_ROPE, 2, dtype=F32) / B_ROPE))
    fa, sga = _lane_table(np.arange(32), 64 + np.arange(32), inv_a)
    fb, sgb = _lane_table(48 + np.arange(16), 112 + np.arange(16), inv_b)
    ca, sa = _rope_tables(pos_col, fa, sga, "rope_a")
    cb, sb = _rope_tables(pos_col, fb, sgb, "rope_b")
    tabs = (ca, sa, cb, sb)
    c8 = jnp.zeros((8, D), F32).at[:B].set(c)

    if dist:
        first = shard_of(0)
        G = dict(zip(FIRST_GATHER, _xchg_call("gather", [first[k] for k in FIRST_GATHER], "gather_first")))
    else:
        G = gathered[0]
    saved, prevs, modss = [], [], []
    prev = None
    for l in range(DEPTH):
        w = _layer_weights(G, small, l)
        mod = _ada_fwd(c8, G["ada_w"], small["ada_b"][l].reshape(1, N_MOD * D))
        mods = tuple(mod[:B, i * D:(i + 1) * D].reshape(B, 1, D) for i in range(N_MOD))
        more = l + 1 < DEPTH
        table = FWD_HOST_FIRST if l == 0 else FWD_HOST
        host = _to_host(table, shard_of(l + 1)) if dist and more else {}
        late = None
        if dist and l == 0:
            late = FIRST_LATE
            for name, ks in late.items():
                host[name] = [first[k] for k in ks] + host.get(name, [])
        prevs.append(prev)
        modss.append(mods)
        prev, sv, got = _layer_fwd(xt, prev, mods, w, tabs, B, S, host, late)
        saved.append(sv)
        if more:
            G = _from_host(table, got) if dist else gathered[l + 1]
    x1, d, g2 = prev
    loss, dx1, dd, dg2, dfw = _final(x1, d, g2, small["final_norm_w"].reshape(1, D), tgt, B, S)

    landed = [None] * DEPTH
    smalls = [None] * DEPTH
    pending = None
    for l in reversed(range(DEPTH)):
        host = _to_host(BWD_HOST, pending) if dist and pending is not None else None
        early = dist and l == 0 and host is not None
        gr, (dsh1, dsc1, dg1, dsh2, dsc2), nxt, got = _layer_bwd(saved[l], prevs[l], modss[l], tabs, dx1, dd, B, S, host, early)
        if pending is not None:
            landed[l + 1] = _from_host(BWD_HOST, got) if dist else pending
        dmod = jnp.concatenate([dsh1, dsc1, dg1, dsh2, dsc2, dg2], axis=-1).reshape(B, N_MOD * D)
        ada_gw, ada_gb = _ada_bwd(c8, jnp.zeros((8, N_MOD * D), F32).at[:B].set(dmod))
        pending = _send_buffers(gr, ada_gw)
        smalls[l] = _small_grads(gr, ada_gb, B)
        dx1, dd, dg2 = nxt
    if dist:
        last = EXPOSED if early else SHARDED
        landed[0] = dict(zip(last, _xchg_call("a2a", [pending[k] for k in last], "grad_exchange_last")))
        if early:
            landed[0].update(zip(OWN_LAYER, got["attn_b_bwd"][-len(OWN_LAYER):]))
    else:
        landed[0] = pending
    small_g = {k: jnp.stack([smalls[l][k] for l in range(DEPTH)]) for k in SMALL if k != "final_norm_w"}
    small_g["final_norm_w"] = dfw.reshape(D)
    return loss[0, 0], dx1.reshape(B, S, D), landed, small_g


def _pack(arrs, cols, mult):
    flat = jnp.concatenate([a.reshape(-1) for a in arrs])
    n = flat.shape[0]
    rows = -(-n // cols)
    rows = -(-rows // mult) * mult
    return jnp.pad(flat, (0, rows * cols - n)).reshape(rows, cols)


def _unpack(blob, shapes):
    flat = blob.reshape(-1)
    out, off = [], 0
    for s in shapes:
        n = int(np.prod(s))
        out.append(flat[off:off + n].reshape(s))
        off += n
    return out


def kernel(x, c, positions, ada_w, ada_b, norm1_w, w_in, a_sinks, b_q_norm_w, b_w_uq, b_kv_norm_w, b_w_ukv, c_ln_w, c_ln_b, c_w_s, c_b_s, out_norm_w, w_out, norm2_w, w_gate_up, w_down, final_norm_w, loss_target, m_ada_w, m_ada_b, m_norm1_w, m_w_in, m_a_sinks, m_b_q_norm_w, m_b_w_uq, m_b_kv_norm_w, m_b_w_ukv, m_c_ln_w, m_c_ln_b, m_c_w_s, m_c_b_s, m_out_norm_w, m_w_out, m_norm2_w, m_w_gate_up, m_w_down, m_final_norm_w, v_ada_w, v_ada_b, v_norm1_w, v_w_in, v_a_sinks, v_b_q_norm_w, v_b_w_uq, v_b_kv_norm_w, v_b_w_ukv, v_c_ln_w, v_c_ln_b, v_c_w_s, v_c_b_s, v_out_norm_w, v_w_out, v_norm2_w, v_w_gate_up, v_w_down, v_final_norm_w):
    names = ("ada_w", "ada_b", "norm1_w", "w_in", "a_sinks", "b_q_norm_w", "b_w_uq", "b_kv_norm_w", "b_w_ukv", "c_ln_w",
             "c_ln_b", "c_w_s", "c_b_s", "out_norm_w", "w_out", "norm2_w", "w_gate_up", "w_down", "final_norm_w")
    ws = dict(zip(names, (ada_w, ada_b, norm1_w, w_in, a_sinks, b_q_norm_w, b_w_uq, b_kv_norm_w, b_w_ukv, c_ln_w, c_ln_b,
                          c_w_s, c_b_s, out_norm_w, w_out, norm2_w, w_gate_up, w_down, final_norm_w)))
    ms = dict(zip(names, (m_ada_w, m_ada_b, m_norm1_w, m_w_in, m_a_sinks, m_b_q_norm_w, m_b_w_uq, m_b_kv_norm_w, m_b_w_ukv,
                          m_c_ln_w, m_c_ln_b, m_c_w_s, m_c_b_s, m_out_norm_w, m_w_out, m_norm2_w, m_w_gate_up, m_w_down,
                          m_final_norm_w)))
    vs = dict(zip(names, (v_ada_w, v_ada_b, v_norm1_w, v_w_in, v_a_sinks, v_b_q_norm_w, v_b_w_uq, v_b_kv_norm_w, v_b_w_ukv,
                          v_c_ln_w, v_c_ln_b, v_c_w_s, v_c_b_s, v_out_norm_w, v_w_out, v_norm2_w, v_w_gate_up, v_w_down,
                          v_final_norm_w)))
    shards = {k: ws[k].astype(BF16) for k in SHARDED}
    loss_local, grad_x, landed, gsmall = _step(x, c, positions, loss_target, {k: ws[k] for k in SMALL},
                                               lambda l: {k: shards[k][l] for k in SHARDED})

    mine = {k: jnp.stack([_sum_slots(landed[l][k], "grad_sum_" + k) for l in range(DEPTH)]) for k in SHARDED}
    theirs = dict(zip(SHARDED, _xchg_call("swap", [mine[k] for k in SHARDED], "grad_sibling_swap")))
    grads, delta, new_m, new_v = {}, {}, {}, {}
    for k in SHARDED:
        shp = ws[k].shape
        two = (shp[0] * shp[1], shp[2])
        g, dlt, nm, nv = _adamw(ws[k].reshape(two), mine[k].reshape(two), theirs[k].reshape(two), ms[k].reshape(two),
                                vs[k].reshape(two), "adamw_" + k)
        grads[k], delta[k], new_m[k], new_v[k] = g.reshape(shp), dlt.reshape(shp), nm.reshape(shp), nv.reshape(shp)

    small_shapes = [ws[k].shape for k in SMALL]
    sblob = _pack([gsmall[k] for k in SMALL] + [loss_local.reshape(1)], LANES, 8)
    svals = _unpack(_all_reduce_small(sblob, "small_all_reduce"), small_shapes + [(1,)])
    loss = svals[-1].reshape(())
    pw = _pack([ws[k] for k in SMALL], LANES, 8)
    pg = _pack(svals[:-1], LANES, 8)
    pm = _pack([ms[k] for k in SMALL], LANES, 8)
    pv = _pack([vs[k] for k in SMALL], LANES, 8)
    g, dlt, nm, nv = _adamw(pw, pg, None, pm, pv, "adamw_small")
    for k, a, b_, c_, d_ in zip(SMALL, _unpack(g, small_shapes), _unpack(dlt, small_shapes), _unpack(nm, small_shapes),
                                _unpack(nv, small_shapes)):
        grads[k], delta[k], new_m[k], new_v[k] = a, b_, c_, d_

    return (loss, grad_x, *[grads[k] for k in names], *[delta[k] for k in names], *[new_m[k] for k in names],
            *[new_v[k] for k in names])
```

```python
import functools
import math

import numpy as np
import jax
import jax.numpy as jnp
from jax import lax
from jax.experimental import pallas as pl
from jax.experimental.pallas import tpu as pltpu

F32 = jnp.float32
BF16 = jnp.bfloat16

D_MODEL = 1024
DEPTH = 4
HEAD_DIM = 64
ROPE_THETA = 10000.0
NORM_EPS = 1e-6
NEG_INF = -1e30
LOG2_E = math.log2(math.e)
A_Q_HEADS = 6
A_KV_HEADS = 2
A_WINDOW = 128
B_HEADS = 6
B_Q_RANK = 384
B_KV_RANK = 256
B_NOPE = 64
B_ROPE = 32
B_V = 64
C_GROUPS = 4
C_GROUP_DIM = 64
C_WIDTH = 256
C_CHUNK = 128
IN_COLS = 1824
FFN_HIDDEN = 2816
N_MOD = 6
ADAM_LR = 0.001
ADAM_B1 = 0.9
ADAM_B2 = 0.999
ADAM_EPS = 1e-08
ADAM_WD = 0.01
ADAM_STEP = 10

LANES = 128
VMEM_LIMIT = 56 * 1024 * 1024
N_CHIPS = 4
WINDOW_SUB = 256

P_AQ, P_AK, P_AV, P_CQ, P_CKV, P_KR, P_CU, P_CV, P_END = 0, 768, 1024, 1280, 1664, 1920, 2048, 2304, 2560


def _map_w_in():
    idx = -np.ones(P_END, np.int64)
    half = HEAD_DIM // 2
    for h in range(A_Q_HEADS):
        idx[P_AQ + h * LANES + np.arange(half)] = h * HEAD_DIM + np.arange(half)
        idx[P_AQ + h * LANES + 64 + np.arange(half)] = h * HEAD_DIM + half + np.arange(half)
    for h in range(A_KV_HEADS):
        idx[P_AK + h * LANES + np.arange(half)] = 384 + h * HEAD_DIM + np.arange(half)
        idx[P_AK + h * LANES + 64 + np.arange(half)] = 384 + h * HEAD_DIM + half + np.arange(half)
        idx[P_AV + h * LANES + np.arange(HEAD_DIM)] = 512 + h * HEAD_DIM + np.arange(HEAD_DIM)
    idx[P_CQ:P_CQ + 384] = 640 + np.arange(384)
    idx[P_CKV:P_CKV + 256] = 1024 + np.arange(256)
    idx[P_KR + 48 + np.arange(16)] = 1280 + np.arange(16)
    idx[P_KR + 112 + np.arange(16)] = 1296 + np.arange(16)
    idx[P_CU:P_CU + 256] = 1312 + np.arange(256)
    idx[P_CV:P_CV + 256] = 1568 + np.arange(256)
    return idx


def _map_w_uq():
    idx = -np.ones(B_HEADS * LANES, np.int64)
    for h in range(B_HEADS):
        b = h * (B_NOPE + B_ROPE)
        idx[h * LANES + np.arange(48)] = b + np.arange(48)
        idx[h * LANES + 48 + np.arange(16)] = b + 64 + np.arange(16)
        idx[h * LANES + 64 + np.arange(16)] = b + 48 + np.arange(16)
        idx[h * LANES + 112 + np.arange(16)] = b + 80 + np.arange(16)
    return idx


def _map_w_ukv():
    idx = -np.ones(2 * B_HEADS * LANES, np.int64)
    for h in range(B_HEADS):
        b = h * (B_NOPE + B_V)
        idx[h * LANES + np.arange(48)] = b + np.arange(48)
        idx[h * LANES + 64 + np.arange(16)] = b + 48 + np.arange(16)
        idx[B_HEADS * LANES + h * LANES + np.arange(B_V)] = b + B_NOPE + np.arange(B_V)
    return idx


def _inverse(idx, n):
    inv = np.zeros(n, np.int64)
    pos = np.nonzero(idx >= 0)[0]
    inv[idx[pos]] = pos
    return inv


def _runs(idx):
    runs, i, n = [], 0, len(idx)
    while i < n:
        j = i + 1
        while j < n and ((idx[i] < 0 and idx[j] < 0) or (idx[i] >= 0 and idx[j] == idx[i] + (j - i))):
            j += 1
        runs.append((int(idx[i]), j - i))
        i = j
    return runs


def _select_axis(w, idx, axis):
    pieces = []
    for start, length in _runs(idx):
        if start < 0:
            shape = list(w.shape)
            shape[axis] = length
            pieces.append(jnp.zeros(shape, w.dtype))
        else:
            pieces.append(lax.slice_in_dim(w, start, start + length, axis=axis))
    return jnp.concatenate(pieces, axis=axis)


def _pad_axis(w, idx, axis):
    return _select_axis(w, idx, axis)


def _unpad_axis(g, idx, n, axis):
    return _select_axis(g, _inverse(idx, n), axis)


def _params(sem):
    return pltpu.CompilerParams(dimension_semantics=sem, vmem_limit_bytes=VMEM_LIMIT)


def _tile(dim, target):
    if dim <= target:
        return dim
    best = None
    for t in range(LANES, target + 1, LANES):
        if dim % t == 0:
            best = t
    assert best is not None, dim
    return best


def _row_div(rows, target):
    if rows <= target:
        return rows
    best = None
    for t in range(8, target + 1, 8):
        if rows % t == 0:
            best = t
    assert best is not None, rows
    return best


_ANY = pl.BlockSpec(memory_space=pl.ANY)
_MESH = pl.DeviceIdType.MESH


def _xchg_out_shapes(kind, srcs):
    if kind == "gather":
        return [jax.ShapeDtypeStruct((N_CHIPS,) + s.shape, s.dtype) for s in srcs]
    return [jax.ShapeDtypeStruct(s.shape, s.dtype) for s in srcs]


def _xchg_scratch(kind, n):
    per = 1 if kind == "swap" else N_CHIPS - 1
    return [pltpu.SemaphoreType.DMA((per * n,)), pltpu.SemaphoreType.DMA((per * n,)), pltpu.SemaphoreType.DMA((n,))]


def _xchg_copies(kind, srcs, dsts, send_sems, recv_sems, local_sems, arrivals):
    x, y, c = lax.axis_index("x"), lax.axis_index("y"), lax.axis_index("c")
    me = 2 * x + y
    peers = [(1 - x, y), (x, 1 - y), (1 - x, 1 - y)]
    local, out, back = [], [], []
    for i, (s, d) in enumerate(zip(srcs, dsts)):
        if kind == "swap":
            cp = pltpu.make_async_remote_copy(src_ref=s, dst_ref=d, send_sem=send_sems.at[i], recv_sem=recv_sems.at[i],
                                              device_id=(x, y, 1 - c), device_id_type=_MESH)
            out.append(cp)
            back.append(cp)
            continue
        local.append(pltpu.make_async_copy(s if kind == "gather" else s.at[me], d.at[me], local_sems.at[i]))
        for kk, (px, py) in enumerate(peers):
            j = (N_CHIPS - 1) * i + kk
            theirs = 2 * px + py
            out.append(pltpu.make_async_remote_copy(
                src_ref=s if kind == "gather" else s.at[theirs], dst_ref=d.at[me], send_sem=send_sems.at[j],
                recv_sem=recv_sems.at[j], device_id=(px, py, c), device_id_type=_MESH))
            if arrivals:
                back.append(pltpu.make_async_remote_copy(
                    src_ref=s if kind == "gather" else s.at[me], dst_ref=d.at[theirs], send_sem=send_sems.at[j],
                    recv_sem=recv_sems.at[j], device_id=(px, py, c), device_id_type=_MESH))
    return local, out, back


def _xchg_start(kind, srcs, dsts, sems):
    local, out, _ = _xchg_copies(kind, srcs, dsts, *sems, arrivals=False)
    for cp in local + out:
        cp.start()


def _xchg_wait(kind, srcs, dsts, sems):
    local, out, back = _xchg_copies(kind, srcs, dsts, *sems, arrivals=True)
    for cp in back:
        cp.wait_recv()
    for cp in out:
        cp.wait_send()
    for cp in local:
        cp.wait()


def _xchg_at_ends(kind, srcs, dsts, sems, grid, first):
    ids = [pl.program_id(a) for a in range(len(grid))]
    cond = None
    for i, n in zip(ids, grid):
        c = (i == 0) if first else (i == n - 1)
        cond = c if cond is None else jnp.logical_and(cond, c)

    @pl.when(cond)
    def _():
        (_xchg_start if first else _xchg_wait)(kind, srcs, dsts, sems)


def _xchg_call(kind, srcs, name):
    n = len(srcs)

    def kern(*refs):
        s, d, sems = refs[:n], refs[n:2 * n], refs[2 * n:]
        _xchg_start(kind, s, d, sems)
        _xchg_wait(kind, s, d, sems)

    return pl.pallas_call(
        kern, name=name, in_specs=[_ANY] * n, out_specs=[_ANY] * n, out_shape=_xchg_out_shapes(kind, srcs),
        scratch_shapes=_xchg_scratch(kind, n),
    )(*srcs)


_DIMS = {"nn": (((1,), (0,)), ((), ())), "nt": (((1,), (1,)), ((), ())), "tn": (((0,), (0,)), ((), ()))}


def _matmul(a, b, mode, out_dtype, name, *, b_chunks=False, out_chunks=False, xchg=None):
    if b_chunks:
        nchunk, brows, bcols = b.shape
        bshape = (brows, nchunk * bcols)
    else:
        bshape = b.shape
    if mode == "nn":
        (m, k), (_, n) = a.shape, bshape
    elif mode == "nt":
        (m, k), (n, _) = a.shape, bshape
    else:
        (k, m), (_, n) = a.shape, bshape
    tm, tk = (1408, 1024) if mode == "tn" else (1024, 1408)
    tm, tn, tk = _tile(m, tm), _tile(n, 1408), _tile(k, tk)
    if b_chunks:
        if mode == "nn":
            tn = bcols
        else:
            assert mode == "nt"
            tk = bcols
    if out_chunks:
        assert n % N_CHIPS == 0
        tn = n // N_CHIPS
    ni, nj, nk = m // tm, n // tn, k // tk
    dims = _DIMS[mode]
    n_x = 0 if xchg is None else len(xchg[1])

    def kern(*refs):
        a_ref, b_ref = refs[0], refs[1]
        xs = refs[2:2 + n_x]
        o_ref = refs[2 + n_x]
        xd = refs[3 + n_x:3 + 2 * n_x]
        acc_ref = refs[3 + 2 * n_x]
        sems = refs[4 + 2 * n_x:]
        kk = pl.program_id(2)
        if n_x:
            _xchg_at_ends(xchg[0], xs, xd, sems, (ni, nj, nk), True)

        @pl.when(kk == 0)
        def _():
            acc_ref[...] = jnp.zeros_like(acc_ref)

        acc_ref[...] += lax.dot_general(a_ref[...], b_ref[...], dims, preferred_element_type=F32)

        @pl.when(kk == nk - 1)
        def _():
            o_ref[...] = acc_ref[...].astype(o_ref.dtype)

        if n_x:
            _xchg_at_ends(xchg[0], xs, xd, sems, (ni, nj, nk), False)

    if mode == "tn":
        a_spec = pl.BlockSpec((tk, tm), lambda i, j, kk: (kk, i))
    else:
        a_spec = pl.BlockSpec((tm, tk), lambda i, j, kk: (i, kk))
    if b_chunks and mode == "nn":
        b_spec = pl.BlockSpec((None, tk, tn), lambda i, j, kk: (j, kk, 0))
    elif b_chunks:
        b_spec = pl.BlockSpec((None, tn, tk), lambda i, j, kk: (kk, j, 0))
    elif mode == "nt":
        b_spec = pl.BlockSpec((tn, tk), lambda i, j, kk: (j, kk))
    else:
        b_spec = pl.BlockSpec((tk, tn), lambda i, j, kk: (kk, j))
    if out_chunks:
        o_spec = pl.BlockSpec((None, tm, tn), lambda i, j, kk: (j, i, 0))
        o_shape = jax.ShapeDtypeStruct((N_CHIPS, m, tn), out_dtype)
    else:
        o_spec = pl.BlockSpec((tm, tn), lambda i, j, kk: (i, j))
        o_shape = jax.ShapeDtypeStruct((m, n), out_dtype)
    xs = [] if xchg is None else list(xchg[1])
    res = pl.pallas_call(
        kern, name=name, grid=(ni, nj, nk),
        in_specs=[a_spec, b_spec] + [_ANY] * n_x, out_specs=[o_spec] + [_ANY] * n_x,
        out_shape=[o_shape] + (_xchg_out_shapes(xchg[0], xs) if n_x else []),
        scratch_shapes=[pltpu.VMEM((tm, tn), F32)] + (_xchg_scratch(xchg[0], n_x) if n_x else []),
        compiler_params=_params(("arbitrary", "arbitrary", "arbitrary") if n_x else ("parallel", "parallel", "arbitrary")),
    )(a, b, *xs)
    return (res[0], list(res[1:])) if n_x else res[0]


def _rowcall(body, *, name, T, S, tr, rows, exs=(), pars=(), row_outs=(), ex_outs=(), par_outs=(), aliases=None):
    assert S % tr == 0 and T % S == 0
    per_ex = S // tr
    nb = T // S
    n_rows, n_exs, n_pars = len(rows), len(exs), len(pars)
    n_ro, n_eo, n_po = len(row_outs), len(ex_outs), len(par_outs)

    def kern(*refs):
        ins = refs[:n_rows + n_exs + n_pars]
        outs = refs[n_rows + n_exs + n_pars:]
        rv = [r[...].astype(F32) for r in ins[:n_rows]]
        ev = [r[0] for r in ins[n_rows:n_rows + n_exs]]
        pv = [r[...] for r in ins[n_rows + n_exs:]]
        ro, eo, po = body(rv, ev, pv)
        i = pl.program_id(0)
        for ref, val in zip(outs[:n_ro], ro):
            if isinstance(val, (list, tuple)):
                off = 0
                for piece in val:
                    w = piece.shape[-1]
                    ref[:, off:off + w] = piece.astype(ref.dtype)
                    off += w
            else:
                ref[...] = val.astype(ref.dtype)
        first_of_ex = (i % per_ex) == 0
        for ref, val in zip(outs[n_ro:n_ro + n_eo], eo):
            @pl.when(first_of_ex)
            def _(ref=ref, val=val):
                ref[0] = val

            @pl.when(jnp.logical_not(first_of_ex))
            def _(ref=ref, val=val):
                ref[0] += val
        for ref, val in zip(outs[n_ro + n_eo:], po):
            @pl.when(i == 0)
            def _(ref=ref, val=val):
                ref[...] = val

            @pl.when(i != 0)
            def _(ref=ref, val=val):
                ref[...] += val

    in_specs = [pl.BlockSpec((tr, w), functools.partial(lambda i, cb: (i, cb), cb=cb)) for (_, w, cb) in rows]
    in_specs += [pl.BlockSpec((1, 1, e.shape[-1]), lambda i: (i // per_ex, 0, 0)) for e in exs]
    in_specs += [pl.BlockSpec(p.shape, functools.partial(lambda i, nd: (0,) * nd, nd=p.ndim)) for p in pars]
    out_specs = [pl.BlockSpec((tr, w), functools.partial(lambda i, cb: (i, cb), cb=cb)) for (_, _, w, cb) in row_outs]
    out_specs += [pl.BlockSpec((1, 1, f), lambda i: (i // per_ex, 0, 0)) for f in ex_outs]
    out_specs += [pl.BlockSpec(tuple(s), functools.partial(lambda i, nd: (0,) * nd, nd=len(s))) for s in par_outs]
    out_shape = [jax.ShapeDtypeStruct((T, tw), dt) for (tw, dt, _, _) in row_outs]
    out_shape += [jax.ShapeDtypeStruct((nb, 1, f), F32) for f in ex_outs]
    out_shape += [jax.ShapeDtypeStruct(tuple(s), F32) for s in par_outs]
    res = pl.pallas_call(
        kern, name=name, grid=(T // tr,), in_specs=in_specs, out_specs=out_specs, out_shape=out_shape,
        input_output_aliases=aliases or {}, compiler_params=_params(("arbitrary",)),
    )(*[r[0] for r in rows], *exs, *pars)
    return res[:n_ro], res[n_ro:n_ro + n_eo], res[n_ro + n_eo:]


def _rms(x, w, n=None):
    n = x.shape[-1] if n is None else n
    ms = jnp.sum(x * x, axis=-1, keepdims=True) * (1.0 / n)
    return x * lax.rsqrt(ms + NORM_EPS) * w


def _gelu(x):
    return 0.5 * x * (1.0 + lax.erf(x * np.float32(1.0 / math.sqrt(2.0))))


def _silu(x):
    return x * jax.nn.sigmoid(x)


@jax.custom_vjp
def _rope(x, cos, sin):
    return x * cos + pltpu.roll(x, 64, 1) * sin


def _rope_fwd(x, cos, sin):
    return _rope(x, cos, sin), (cos, sin)


def _rope_bwd(res, dy):
    cos, sin = res
    return dy * cos + pltpu.roll(dy * sin, 64, 1), None, None


_rope.defvjp(_rope_fwd, _rope_bwd)


def _heads(x, n):
    return [x[:, h * LANES:(h + 1) * LANES] for h in range(n)]


def _f_norm_mod(x, w, sc, sh):
    return _rms(x, w) * (1.0 + sc) + sh


def _f_resid_norm_mod(xa, delta, g, w, sc, sh):
    xn = xa + g * delta
    return xn, _f_norm_mod(xn, w, sc, sh)


def _f_mixprep(proj, ca, sa, cb, sb, qnw, kvnw):
    qa = [_rope(p, ca, sa) for p in _heads(proj[:, P_AQ:P_AK], A_Q_HEADS)]
    ka = [_rope(p, ca, sa) for p in _heads(proj[:, P_AK:P_AV], A_KV_HEADS)]
    va = proj[:, P_AV:P_CQ]
    cqn = _rms(proj[:, P_CQ:P_CKV], qnw)
    ckvn = _rms(proj[:, P_CKV:P_KR], kvnw)
    kr = _rope(proj[:, P_KR:P_CU], cb, sb)
    return jnp.concatenate(qa, -1), jnp.concatenate(ka, -1), va, cqn, ckvn, kr


def _f_mlaprep(q, kv, kr, cb, sb):
    qs = [_rope(p, cb, sb) for p in _heads(q, B_HEADS)]
    ks = [p + kr for p in _heads(kv[:, :B_HEADS * LANES], B_HEADS)]
    return jnp.concatenate(qs, -1), jnp.concatenate(ks, -1), kv[:, B_HEADS * LANES:]


def _f_sgu(cu, cv, ln_w, ln_b, w_s, b_col):
    u = _gelu(cu)
    v = _gelu(cv)
    mu = jnp.mean(v, axis=-1, keepdims=True)
    var = jnp.mean(jnp.square(v - mu), axis=-1, keepdims=True)
    vn = (v - mu) * lax.rsqrt(var + NORM_EPS) * ln_w + ln_b
    r = lax.broadcasted_iota(jnp.int32, (C_CHUNK, C_CHUNK), 0)
    c = lax.broadcasted_iota(jnp.int32, (C_CHUNK, C_CHUNK), 1)
    lane = lax.broadcasted_iota(jnp.int32, (1, LANES), 1)
    per_block = LANES // C_GROUP_DIM
    blocks = []
    for blk, vb in enumerate(_heads(vn, C_WIDTH // LANES)):
        mixed = jnp.zeros(vb.shape, F32)
        for j in range(per_block):
            g = blk * per_block + j
            gm = (lane // C_GROUP_DIM == j).astype(F32)
            wg = jnp.where(r >= c, w_s[g], 0.0).astype(BF16)
            mixed = mixed + jnp.dot(wg, (vb * gm).astype(BF16), preferred_element_type=F32) + b_col[g] * gm
        blocks.append(mixed)
    return u * jnp.concatenate(blocks, -1)


@jax.custom_vjp
def _pack_pairs(x):
    heads = _heads(x, x.shape[-1] // LANES)
    return jnp.concatenate([heads[i] + pltpu.roll(heads[i + 1], 64, 1) for i in range(0, len(heads), 2)], -1)


def _pack_pairs_fwd(x):
    return _pack_pairs(x), None


def _pack_pairs_bwd(_, dy):
    out = []
    for p in _heads(dy, dy.shape[-1] // LANES):
        out += [p, pltpu.roll(p, 64, 1)]
    return (jnp.concatenate(out, -1),)


_pack_pairs.defvjp(_pack_pairs_fwd, _pack_pairs_bwd)


def _f_outnorm(oa, ob, yc, gw):
    na, nb = A_Q_HEADS * HEAD_DIM, B_HEADS * B_V
    ya = _rms(_pack_pairs(oa), gw[:, :na])
    yb = _rms(_pack_pairs(ob), gw[:, na:na + nb])
    ycn = _rms(yc, gw[:, na + nb:])
    return jnp.concatenate([ya, yb, ycn], -1)


def _f_swiglu(gate, up):
    return _silu(gate) * up


def _mask(q_start, k_start, tq, tk, window):
    qpos = q_start + lax.broadcasted_iota(jnp.int32, (tq, tk), 0)
    kpos = k_start + lax.broadcasted_iota(jnp.int32, (tq, tk), 1)
    m = kpos <= qpos
    if window is not None:
        m = jnp.logical_and(m, qpos - kpos < window)
    return m


def _tile_fwd(qv, kk, vv, q_start, k_start, n_free, scale, window, m0, l0):
    tq = qv.shape[0]
    W = kk.shape[0]
    c = scale * LOG2_E
    parts = []
    if n_free > 0:
        parts.append((lax.dot_general(qv, kk[:n_free], _DIMS["nt"], preferred_element_type=F32), vv[:n_free]))
    if W > n_free:
        s = lax.dot_general(qv, kk[n_free:], _DIMS["nt"], preferred_element_type=F32)
        s = jnp.where(_mask(q_start, k_start + n_free, tq, W - n_free, window), s, NEG_INF)
        parts.append((s, vv[n_free:]))
    m = None if m0 is None else m0 * (1.0 / scale)
    for s, _ in parts:
        mx = jnp.max(s, axis=-1, keepdims=True)
        m = mx if m is None else jnp.maximum(m, mx)
    l = None if l0 is None else l0 * jnp.exp2((m0 * (1.0 / scale) - m) * c)
    o = None
    for s, vpart in parts:
        p = jnp.exp2((s - m) * c)
        ps = jnp.sum(p, axis=-1, keepdims=True)
        l = ps if l is None else l + ps
        po = jnp.dot(p.astype(BF16), vpart, preferred_element_type=F32)
        o = po if o is None else o + po
    return o / l, m * scale + jnp.log(l)


def _tile_bwd(qv, kk, vv, dof, ov, lse, q_start, k_start, n_free, scale, window):
    tq = qv.shape[0]
    W = kk.shape[0]
    dob = dof.astype(BF16)
    delta = jnp.sum(dof * ov, axis=-1, keepdims=True)
    c = scale * LOG2_E
    lse2 = lse * LOG2_E
    dq = None
    outs = []
    for (a, b, masked) in ((0, n_free, False), (n_free, W, True)):
        if b <= a:
            continue
        kp, vp = kk[a:b], vv[a:b]
        s = lax.dot_general(qv, kp, _DIMS["nt"], preferred_element_type=F32)
        if masked:
            s = jnp.where(_mask(q_start, k_start + a, tq, b - a, window), s, NEG_INF)
        p = jnp.exp2(s * c - lse2)
        dp = lax.dot_general(dob, vp, _DIMS["nt"], preferred_element_type=F32)
        ds = (p * ((dp - delta) * scale)).astype(BF16)
        d = jnp.dot(ds, kp, preferred_element_type=F32)
        dq = d if dq is None else dq + d
        dkp = lax.dot_general(ds, qv, _DIMS["tn"], preferred_element_type=F32)
        dvp = lax.dot_general(p.astype(BF16), dob, _DIMS["tn"], preferred_element_type=F32)
        outs.append((a, dkp, dvp))
    return dq, outs


def _attn_fwd(q, k, v, sinks, *, B, S, HQ, HK, window, scale, tq, band, name, xchg=None):
    G = HQ // HK
    nq = S // tq
    T = B * S
    has_sink = sinks is not None
    n_x = 0 if xchg is None else len(xchg[1])
    n_in = 4 if has_sink else 3

    def kern(*refs):
        xs, xd, sems = refs[n_in:n_in + n_x], refs[n_in + n_x + 2:n_in + 2 * n_x + 2], refs[n_in + 2 * n_x + 2:]
        refs = refs[:n_in] + refs[n_in + n_x:n_in + n_x + 2]
        if n_x:
            _xchg_at_ends(xchg[0], xs, xd, sems, (B, HQ, nq), True)
        if has_sink:
            q_ref, k_ref, v_ref, s_ref, o_ref, lse_ref = refs
        else:
            q_ref, k_ref, v_ref, o_ref, lse_ref = refs
        q_start = pl.program_id(2) * tq
        qv = q_ref[...]
        if has_sink:
            m0 = jnp.broadcast_to(s_ref[0][:, :1], (tq, 1))
            l0 = jnp.ones((tq, 1), F32)
        else:
            m0 = l0 = None

        def finish(o, lse):
            o_ref[...] = o.astype(o_ref.dtype)
            lse_ref[...] = jnp.broadcast_to(lse, (tq, LANES))

        if window is None:
            bidx = q_start // band
            for bb in range(S // band):
                @pl.when(bidx == bb)
                def _(bb=bb):
                    W = (bb + 1) * band
                    finish(*_tile_fwd(qv, k_ref[0:W, :], v_ref[0:W, :], q_start, 0, bb * band, scale, None, m0, l0))
        else:
            sub = min(tq, WINDOW_SUB)
            W = min(S, sub + window)
            for r in range(0, tq, sub):
                k_start = pl.multiple_of(jnp.maximum(q_start + r - window, 0), window)
                o, lse = _tile_fwd(qv[r:r + sub], k_ref[pl.ds(k_start, W), :], v_ref[pl.ds(k_start, W), :], q_start + r,
                                   k_start, 0, scale, window, None if m0 is None else m0[:sub], None if l0 is None else l0[:sub])
                o_ref[r:r + sub, :] = o.astype(o_ref.dtype)
                lse_ref[r:r + sub, :] = jnp.broadcast_to(lse, (sub, LANES))
        if n_x:
            _xchg_at_ends(xchg[0], xs, xd, sems, (B, HQ, nq), False)

    q_spec = pl.BlockSpec((tq, LANES), lambda b, h, i: (b * nq + i, h))
    kv_spec = pl.BlockSpec((S, LANES), lambda b, h, i: (b, h // G))
    in_specs = [q_spec, kv_spec, kv_spec]
    args = [q, k, v]
    if has_sink:
        in_specs.append(pl.BlockSpec((1, 1, LANES), lambda b, h, i: (h, 0, 0)))
        args.append(sinks)
    xs = [] if xchg is None else list(xchg[1])
    res = pl.pallas_call(
        kern, name=name, grid=(B, HQ, nq), in_specs=in_specs + [_ANY] * n_x, out_specs=[q_spec, q_spec] + [_ANY] * n_x,
        out_shape=[jax.ShapeDtypeStruct((T, HQ * LANES), BF16), jax.ShapeDtypeStruct((T, HQ * LANES), F32)]
        + (_xchg_out_shapes(xchg[0], xs) if n_x else []),
        scratch_shapes=_xchg_scratch(xchg[0], n_x) if n_x else [],
        compiler_params=_params(("arbitrary",) * 3 if n_x else ("parallel", "parallel", "arbitrary")),
    )(*args, *xs)
    return ((res[0], res[1]), list(res[2:])) if n_x else res


def _attn_bwd(q, k, v, o, lse, do, sinks, *, B, S, HQ, HK, window, scale, tq, band, name, xchg=None):
    G = HQ // HK
    nq = S // tq
    T = B * S
    has_sink = sinks is not None
    n_x = 0 if xchg is None else len(xchg[1])
    n_in, n_out = (7, 4) if has_sink else (6, 3)

    def kern(*refs):
        xs, xd = refs[n_in:n_in + n_x], refs[n_in + n_x + n_out:n_in + 2 * n_x + n_out]
        sems = refs[n_in + 2 * n_x + n_out:]
        refs = refs[:n_in] + refs[n_in + n_x:n_in + n_x + n_out]
        if n_x:
            _xchg_at_ends(xchg[0], xs, xd, sems, (B, HK, G, nq), True)
        if has_sink:
            q_ref, k_ref, v_ref, o_ref, lse_ref, do_ref, s_ref, dq_ref, dk_ref, dv_ref, ds_ref = refs
        else:
            q_ref, k_ref, v_ref, o_ref, lse_ref, do_ref, dq_ref, dk_ref, dv_ref = refs
        gi = pl.program_id(2)
        qi = pl.program_id(3)
        q_start = qi * tq

        @pl.when(jnp.logical_and(gi == 0, qi == 0))
        def _():
            dk_ref[...] = jnp.zeros_like(dk_ref)
            dv_ref[...] = jnp.zeros_like(dv_ref)

        qv = q_ref[...]
        dof = do_ref[...].astype(F32)
        ov = o_ref[...].astype(F32)
        lse_v = lse_ref[...][:, :1]
        if window is None:
            bidx = q_start // band
            for bb in range(S // band):
                @pl.when(bidx == bb)
                def _(bb=bb):
                    W = (bb + 1) * band
                    dq, outs = _tile_bwd(qv, k_ref[0:W, :], v_ref[0:W, :], dof, ov, lse_v, q_start, 0, bb * band, scale, None)
                    dq_ref[...] = dq.astype(dq_ref.dtype)
                    for a, dkp, dvp in outs:
                        dk_ref[a:a + dkp.shape[0], :] += dkp
                        dv_ref[a:a + dvp.shape[0], :] += dvp
        else:
            sub = min(tq, WINDOW_SUB)
            W = min(S, sub + window)
            parts = []
            for r in range(0, tq, sub):
                k_start = pl.multiple_of(jnp.maximum(q_start + r - window, 0), window)
                dq, outs = _tile_bwd(qv[r:r + sub], k_ref[pl.ds(k_start, W), :], v_ref[pl.ds(k_start, W), :],
                                     dof[r:r + sub], ov[r:r + sub], lse_v[r:r + sub], q_start + r, k_start, 0, scale, window)
                dq_ref[r:r + sub, :] = dq.astype(dq_ref.dtype)
                parts.append((k_start, outs[0][1], outs[0][2]))
            for k_start, dkp, dvp in parts:
                dk_ref[pl.ds(k_start, W), :] += dkp
                dv_ref[pl.ds(k_start, W), :] += dvp
        if has_sink:
            delta = jnp.sum(dof * ov, axis=-1, keepdims=True)
            sink = s_ref[0][:, :1]
            part = -jnp.sum(jnp.exp(sink - lse_v) * delta, axis=0, keepdims=True)
            part = jnp.broadcast_to(part, (1, LANES))

            @pl.when(qi == 0)
            def _():
                ds_ref[0] = part

            @pl.when(qi != 0)
            def _():
                ds_ref[0] += part
        if n_x:
            _xchg_at_ends(xchg[0], xs, xd, sems, (B, HK, G, nq), False)

    q_spec = pl.BlockSpec((tq, LANES), lambda b, hk, g, i: (b * nq + i, hk * G + g))
    kv_spec = pl.BlockSpec((S, LANES), lambda b, hk, g, i: (b, hk))
    in_specs = [q_spec, kv_spec, kv_spec, q_spec, q_spec, q_spec]
    args = [q, k, v, o, lse, do]
    out_specs = [q_spec, kv_spec, kv_spec]
    out_shape = [jax.ShapeDtypeStruct((T, HQ * LANES), BF16), jax.ShapeDtypeStruct((T, HK * LANES), F32),
                 jax.ShapeDtypeStruct((T, HK * LANES), F32)]
    if has_sink:
        in_specs.append(pl.BlockSpec((1, 1, LANES), lambda b, hk, g, i: (hk * G + g, 0, 0)))
        args.append(sinks)
        out_specs.append(pl.BlockSpec((1, 1, LANES), lambda b, hk, g, i: (b * HQ + hk * G + g, 0, 0)))
        out_shape.append(jax.ShapeDtypeStruct((B * HQ, 1, LANES), F32))
    xs = [] if xchg is None else list(xchg[1])
    res = pl.pallas_call(
        kern, name=name, grid=(B, HK, G, nq), in_specs=in_specs + [_ANY] * n_x, out_specs=out_specs + [_ANY] * n_x,
        out_shape=out_shape + (_xchg_out_shapes(xchg[0], xs) if n_x else []),
        scratch_shapes=_xchg_scratch(xchg[0], n_x) if n_x else [],
        compiler_params=_params(("arbitrary",) * 4 if n_x else ("parallel", "parallel", "arbitrary", "arbitrary")),
    )(*args, *xs)
    main = tuple(res[:n_out]) if has_sink else (*res[:n_out], None)
    return (main, list(res[n_out:])) if n_x else main


def _rope_tables(pos_col, freq, sign, name):
    T = pos_col.shape[0]
    tr = _tile(T, 1024)

    def kern(p_ref, f_ref, s_ref, c_out, s_out):
        ang = p_ref[...] * f_ref[...]
        c_out[...] = jnp.cos(ang)
        s_out[...] = jnp.sin(ang) * s_ref[...]

    spec = pl.BlockSpec((tr, LANES), lambda i: (i, 0))
    par = pl.BlockSpec((1, LANES), lambda i: (0, 0))
    return pl.pallas_call(
        kern, name=name, grid=(T // tr,), in_specs=[pl.BlockSpec((tr, 1), lambda i: (i, 0)), par, par],
        out_specs=[spec, spec], out_shape=[jax.ShapeDtypeStruct((T, LANES), F32)] * 2,
        compiler_params=_params(("parallel",)),
    )(pos_col, freq, sign)


def _ada_fwd(c8, ada_w, ada_b):
    L, D, N = ada_w.shape

    def kern(c_ref, w_ref, b_ref, o_ref):
        act = _silu(c_ref[...]).astype(BF16)
        o_ref[0] = jnp.dot(act, w_ref[0], preferred_element_type=F32) + b_ref[0]

    return pl.pallas_call(
        kern, name="ada_fwd", grid=(L,),
        in_specs=[pl.BlockSpec((8, D), lambda l: (0, 0)), pl.BlockSpec((1, D, N), lambda l: (l, 0, 0)),
                  pl.BlockSpec((1, 1, N), lambda l: (l, 0, 0))],
        out_specs=pl.BlockSpec((1, 8, N), lambda l: (l, 0, 0)),
        out_shape=jax.ShapeDtypeStruct((L, 8, N), F32), compiler_params=_params(("parallel",)),
    )(c8, ada_w, ada_b)


def _ada_bwd(c8, dmod):
    L, _, N = dmod.shape
    D = c8.shape[1]

    def kern(c_ref, d_ref, gw_ref):
        act = _silu(c_ref[...]).astype(BF16)
        gw_ref[0] = lax.dot_general(act, d_ref[0].astype(BF16), _DIMS["tn"], preferred_element_type=F32)

    return pl.pallas_call(
        kern, name="ada_bwd", grid=(L,),
        in_specs=[pl.BlockSpec((8, D), lambda l: (0, 0)), pl.BlockSpec((1, 8, N), lambda l: (l, 0, 0))],
        out_specs=pl.BlockSpec((1, D, N), lambda l: (l, 0, 0)),
        out_shape=jax.ShapeDtypeStruct((L, D, N), F32), compiler_params=_params(("parallel",)),
    )(c8, dmod)


def _sum_examples(d):
    L, _, N = d.shape

    def kern(d_ref, o_ref):
        o_ref[...] = jnp.sum(d_ref[...], axis=1, keepdims=True)

    vm = pl.BlockSpec(memory_space=pltpu.VMEM)
    return pl.pallas_call(kern, name="sum_examples", in_specs=[vm], out_specs=vm,
                          out_shape=jax.ShapeDtypeStruct((L, 1, N), F32))(d)


def _adamw(w, ga, gb, m, v, name):
    rows, cols = w.shape
    tr = _row_div(rows, 256)
    two = gb is not None

    def kern(*refs):
        if two:
            w_ref, ga_ref, gb_ref, m_ref, v_ref, g_out, d_out, m_out, v_out = refs
            gv = ga_ref[...] + gb_ref[...]
        else:
            w_ref, ga_ref, m_ref, v_ref, g_out, d_out, m_out, v_out = refs
            gv = ga_ref[...]
        mn = ADAM_B1 * m_ref[...] + (1.0 - ADAM_B1) * gv
        vn = ADAM_B2 * v_ref[...] + (1.0 - ADAM_B2) * jnp.square(gv)
        m_hat = mn / (1.0 - ADAM_B1 ** ADAM_STEP)
        v_hat = vn / (1.0 - ADAM_B2 ** ADAM_STEP)
        g_out[...] = gv
        d_out[...] = -ADAM_LR * (m_hat / (jnp.sqrt(v_hat) + ADAM_EPS) + ADAM_WD * w_ref[...])
        m_out[...] = mn
        v_out[...] = vn

    spec = pl.BlockSpec((tr, cols), lambda i: (i, 0))
    args = [w, ga, gb, m, v] if two else [w, ga, m, v]
    return pl.pallas_call(
        kern, name=name, grid=(rows // tr,), in_specs=[spec] * len(args), out_specs=[spec] * 4,
        out_shape=[jax.ShapeDtypeStruct((rows, cols), F32)] * 4, compiler_params=_params(("parallel",)),
    )(*args)


def _sum_slots(x, name):
    n, rows, cols = x.shape
    tr = _row_div(rows, 256)

    def kern(x_ref, o_ref):
        acc = x_ref[0].astype(F32)
        for j in range(1, n):
            acc = acc + x_ref[j].astype(F32)
        o_ref[...] = acc

    return pl.pallas_call(
        kern, name=name, grid=(rows // tr,), in_specs=[pl.BlockSpec((n, tr, cols), lambda i: (0, i, 0))],
        out_specs=pl.BlockSpec((tr, cols), lambda i: (i, 0)), out_shape=jax.ShapeDtypeStruct((rows, cols), F32),
        compiler_params=_params(("parallel",)),
    )(x)


def _all_reduce_small(blob, name):
    R, C = blob.shape

    def kern(src, out, pair, chips, send_sems, recv_sems):
        x, y, c = lax.axis_index("x"), lax.axis_index("y"), lax.axis_index("c")
        me = 2 * x + y
        to_sibling = pltpu.make_async_remote_copy(
            src_ref=src, dst_ref=pair, send_sem=send_sems.at[0], recv_sem=recv_sems.at[0],
            device_id=(x, y, 1 - c), device_id_type=_MESH)
        to_sibling.start()
        to_sibling.wait()
        chips[me] = src[...] + pair[...]
        peers = [(1 - x, y), (x, 1 - y), (1 - x, 1 - y)]
        copies = [pltpu.make_async_remote_copy(
            src_ref=chips.at[me], dst_ref=chips.at[me], send_sem=send_sems.at[1 + kk], recv_sem=recv_sems.at[1 + kk],
            device_id=(px, py, c), device_id_type=_MESH) for kk, (px, py) in enumerate(peers)]
        for cp in copies:
            cp.start()
        for kk, (px, py) in enumerate(peers):
            pltpu.make_async_remote_copy(
                src_ref=chips.at[me], dst_ref=chips.at[2 * px + py], send_sem=send_sems.at[1 + kk],
                recv_sem=recv_sems.at[1 + kk], device_id=(px, py, c), device_id_type=_MESH).wait_recv()
        for cp in copies:
            cp.wait_send()
        acc = chips[0]
        for j in range(1, N_CHIPS):
            acc = acc + chips[j]
        out[...] = acc

    vm = pl.BlockSpec(memory_space=pltpu.VMEM)
    return pl.pallas_call(
        kern, name=name, in_specs=[vm], out_specs=vm, out_shape=jax.ShapeDtypeStruct((R, C), F32),
        scratch_shapes=[pltpu.VMEM((R, C), F32), pltpu.VMEM((N_CHIPS, R, C), F32), pltpu.SemaphoreType.DMA((4,)),
                        pltpu.SemaphoreType.DMA((4,))],
        compiler_params=pltpu.CompilerParams(vmem_limit_bytes=VMEM_LIMIT),
    )(blob)


def _row_tile(S, wide=False):
    for t in ((256,) if wide else (512, 256)):
        if S % t == 0:
            return t
    return 128


def _attn_tiles(S):
    return min(S, 512), min(S, 256), min(S, 512)


def _hosted(fn, host, got, kind, name, *args, **kw):
    if host is not None and host.get(name):
        res, xs = fn(*args, name=name, xchg=(kind, host[name]), **kw)
        got[name] = xs
        return res
    return fn(*args, name=name, **kw)


def _mm(host, got, kind, a, b, mode, out_dtype, name, **kw):
    return _hosted(_matmul, host, got, kind, name, a, b, mode, out_dtype, **kw)


def _merge_late(w, late, got, name):
    if not late or name not in late:
        return w
    return {**w, **_large_operands(dict(zip(late[name], got[name])))}


def _sgu_rows(S):
    return 4 * C_CHUNK if S % (4 * C_CHUNK) == 0 else C_CHUNK


def _layer_fwd(xin, prev, mods, w, tabs, B, S, host=None, late=None):
    T = B * S
    tr = _row_tile(S)
    sh1, sc1, g1, sh2, sc2, g2 = mods
    ca, sa, cb, sb = tabs
    sv = {}
    got = {}
    if prev is None:
        def body(rv, ev, pv):
            return [_f_norm_mod(rv[0], pv[0], ev[0], ev[1])], [], []
        (h,), _, _ = _rowcall(body, name="f_norm_mod", T=T, S=S, tr=tr, rows=[(xin, D_MODEL, 0)], exs=[sc1, sh1],
                              pars=[w["norm1_w"]], row_outs=[(D_MODEL, BF16, D_MODEL, 0)])
        x = xin
    else:
        x1p, dp, g2p = prev

        def body(rv, ev, pv):
            xn, hh = _f_resid_norm_mod(rv[0], rv[1], ev[0], pv[0], ev[1], ev[2])
            return [xn, hh], [], []
        (x, h), _, _ = _rowcall(body, name="f_resid_norm_mod1", T=T, S=S, tr=tr,
                                rows=[(x1p, D_MODEL, 0), (dp, D_MODEL, 0)], exs=[g2p, sc1, sh1], pars=[w["norm1_w"]],
                                row_outs=[(D_MODEL, F32, D_MODEL, 0), (D_MODEL, BF16, D_MODEL, 0)])
    sv["x"], sv["h"] = x, h
    proj = _mm(host, got, "gather", h, w["w_in"], "nn", BF16, "mm_in")
    sv["proj"] = proj
    w = _merge_late(w, late, got, "mm_in")

    def body(rv, ev, pv):
        outs = _f_mixprep(rv[0], rv[1], rv[2], rv[3], rv[4], pv[0], pv[1])
        return list(outs), [], []
    (qa, ka, va, cqn, ckvn, kr), _, _ = _rowcall(
        body, name="f_mixprep", T=T, S=S, tr=tr,
        rows=[(proj, P_CU, 0), (ca, LANES, 0), (sa, LANES, 0), (cb, LANES, 0), (sb, LANES, 0)],
        pars=[w["b_q_norm_w"], w["b_kv_norm_w"]],
        row_outs=[(768, BF16, 768, 0), (256, BF16, 256, 0), (256, BF16, 256, 0), (384, BF16, 384, 0),
                  (256, BF16, 256, 0), (LANES, F32, LANES, 0)])
    sv.update(qa=qa, ka=ka, va=va, cqn=cqn, ckvn=ckvn)
    q = _matmul(cqn, w["b_w_uq"], "nn", BF16, "mm_uq")
    kv = _matmul(ckvn, w["b_w_ukv"], "nn", BF16, "mm_ukv")

    def body(rv, ev, pv):
        Q, K, V = _f_mlaprep(rv[0], rv[1], rv[2], rv[3], rv[4])
        return [Q, K, V], [], []
    (Q, K, V), _, _ = _rowcall(
        body, name="f_mlaprep", T=T, S=S, tr=tr,
        rows=[(q, 768, 0), (kv, 1536, 0), (kr, LANES, 0), (cb, LANES, 0), (sb, LANES, 0)],
        row_outs=[(768, BF16, 768, 0)] * 3)
    sv.update(Q=Q, K=K, V=V)
    ta, tb, band = _attn_tiles(S)
    oa, lse_a = _hosted(_attn_fwd, host, got, "gather", "attn_a_fwd", qa, ka, va, w["sinks"], B=B, S=S, HQ=A_Q_HEADS,
                        HK=A_KV_HEADS, window=A_WINDOW, scale=HEAD_DIM ** -0.5, tq=ta, band=None)
    ob, lse_b = _hosted(_attn_fwd, host, got, "gather", "attn_b_fwd", Q, K, V, None, B=B, S=S, HQ=B_HEADS, HK=B_HEADS,
                        window=None, scale=(B_NOPE + B_ROPE) ** -0.5, tq=tb, band=band)
    sv.update(oa=oa, lse_a=lse_a, ob=ob, lse_b=lse_b)
    w = _merge_late(w, late, got, "attn_a_fwd")
    ts = _sgu_rows(S)

    def body(rv, ev, pv):
        outs = [_f_sgu(rv[0][r:r + C_CHUNK], rv[1][r:r + C_CHUNK], pv[0], pv[1], pv[2], pv[3])
                for r in range(0, ts, C_CHUNK)]
        return [jnp.concatenate(outs, axis=0)], [], []
    (yc,), _, _ = _rowcall(body, name="f_sgu", T=T, S=S, tr=ts,
                           rows=[(proj, C_WIDTH, P_CU // C_WIDTH), (proj, C_WIDTH, P_CV // C_WIDTH)],
                           pars=[w["c_ln_w"], w["c_ln_b"], w["c_w_s"], w["c_b_col"]],
                           row_outs=[(C_WIDTH, F32, C_WIDTH, 0)])
    sv["yc"] = yc

    def body(rv, ev, pv):
        return [_f_outnorm(rv[0], rv[1], rv[2], pv[0])], [], []
    (y,), _, _ = _rowcall(body, name="f_outnorm", T=T, S=S, tr=tr,
                          rows=[(oa, 768, 0), (ob, 768, 0), (yc, C_WIDTH, 0)], pars=[w["out_norm_w"]],
                          row_outs=[(D_MODEL, BF16, D_MODEL, 0)])
    sv["y"] = y
    o = _mm(host, got, "gather", y, w["w_out"], "nn", F32, "mm_out")
    sv["o"] = o

    def body(rv, ev, pv):
        xn, hh = _f_resid_norm_mod(rv[0], rv[1], ev[0], pv[0], ev[1], ev[2])
        return [xn, hh], [], []
    (x1, h2), _, _ = _rowcall(body, name="f_resid_norm_mod2", T=T, S=S, tr=tr,
                              rows=[(x, D_MODEL, 0), (o, D_MODEL, 0)], exs=[g1, sc2, sh2], pars=[w["norm2_w"]],
                              row_outs=[(D_MODEL, F32, D_MODEL, 0), (D_MODEL, BF16, D_MODEL, 0)])
    sv["h2"] = h2
    gu = _mm(host, got, "gather", h2, w["w_gate_up"], "nn", BF16, "mm_gate_up", b_chunks=True)
    sv["gu"] = gu

    def body(rv, ev, pv):
        return [_f_swiglu(rv[0], rv[1])], [], []
    (act,), _, _ = _rowcall(body, name="f_swiglu", T=T, S=S, tr=_row_tile(S, wide=True),
                            rows=[(gu, FFN_HIDDEN, 0), (gu, FFN_HIDDEN, 1)], row_outs=[(FFN_HIDDEN, BF16, FFN_HIDDEN, 0)])
    sv["act"] = act
    d = _mm(host, got, "gather", act, w["w_down"], "nn", F32, "mm_down")
    sv["x1"], sv["d"], sv["w"] = x1, d, w
    return (x1, d, g2), sv, got


def _final(x1, d, g2, fw, target, B, S):
    T = B * S
    tr = _row_tile(S)

    def loss_fn(x1v, dv, gv, wv, tv):
        yf = _rms(x1v + gv * dv, wv)
        return 0.5 * jnp.sum(jnp.mean(jnp.square(yf - tv), axis=-1))

    def body(rv, ev, pv):
        x1v, dv, tv = rv
        val, vjp = jax.vjp(lambda a, b_, g, ww: loss_fn(a, b_, g, ww, tv), x1v, dv, ev[0], pv[0])
        dx1, dd, dg, dw = vjp(jnp.ones((), F32))
        return [dx1, dd], [dg], [dw, jnp.full((1, LANES), val, F32)]
    (dx1, dd), (dg2,), (dfw, loss) = _rowcall(
        body, name="final_loss", T=T, S=S, tr=tr, rows=[(x1, D_MODEL, 0), (d, D_MODEL, 0), (target, D_MODEL, 0)],
        exs=[g2], pars=[fw], row_outs=[(D_MODEL, F32, D_MODEL, 0), (D_MODEL, BF16, D_MODEL, 0)],
        ex_outs=[D_MODEL], par_outs=[(1, D_MODEL), (1, LANES)])
    return loss, dx1, dd, dg2, dfw


def _layer_bwd(sv, prev, mods, tabs, dx1, dd, B, S, host=None, own_early=False):
    w = sv["w"]
    T = B * S
    tr = _row_tile(S)
    sh1, sc1, g1, sh2, sc2, g2 = mods
    ca, sa, cb, sb = tabs
    gr = {}
    got = {}
    dact = _mm(host, got, "a2a", dd, w["w_down"], "nt", BF16, "mm_down_dx")
    gr["w_down"] = _matmul(sv["act"], dd, "tn", BF16, "mm_down_dw")

    def body(rv, ev, pv):
        _, vjp = jax.vjp(_f_swiglu, rv[0], rv[1])
        dgate, dup = vjp(rv[2])
        return [[dgate, dup]], [], []
    (dgu,), _, _ = _rowcall(body, name="b_swiglu", T=T, S=S, tr=_row_tile(S, wide=True),
                            rows=[(sv["gu"], FFN_HIDDEN, 0), (sv["gu"], FFN_HIDDEN, 1), (dact, FFN_HIDDEN, 0)],
                            row_outs=[(2 * FFN_HIDDEN, BF16, 2 * FFN_HIDDEN, 0)])
    dh2 = _mm(host, got, "a2a", dgu, w["w_gate_up"], "nt", BF16, "mm_gate_up_dx", b_chunks=True)
    gr["w_gate_up"] = _mm(host, got, "a2a", sv["h2"], dgu, "tn", BF16, "mm_gate_up_dw", out_chunks=True)

    def body(rv, ev, pv):
        xa, delta, dh, dxn = rv
        _, vjp = jax.vjp(_f_resid_norm_mod, xa, delta, ev[0], pv[0], ev[1], ev[2])
        dxa, ddelta, dg, dw, dsc, dsh = vjp((dxn, dh))
        return [dxa, ddelta], [dg, dsc, dsh], [dw]
    (dx, do), (dg1, dsc2, dsh2), (gr["norm2_w"],) = _rowcall(
        body, name="b_resid_norm_mod2", T=T, S=S, tr=tr,
        rows=[(sv["x"], D_MODEL, 0), (sv["o"], D_MODEL, 0), (dh2, D_MODEL, 0), (dx1, D_MODEL, 0)],
        exs=[g1, sc2, sh2], pars=[w["norm2_w"]],
        row_outs=[(D_MODEL, F32, D_MODEL, 0), (D_MODEL, BF16, D_MODEL, 0)], ex_outs=[D_MODEL] * 3,
        par_outs=[(1, D_MODEL)])
    dy = _matmul(do, w["w_out"], "nt", BF16, "mm_out_dx")
    gr["w_out"] = _matmul(sv["y"], do, "tn", BF16, "mm_out_dw")

    def body(rv, ev, pv):
        _, vjp = jax.vjp(_f_outnorm, rv[0], rv[1], rv[2], pv[0])
        doa, dob, dyc, dgw = vjp(rv[3])
        return [doa, dob, dyc], [], [dgw]
    (doa, dob, dyc), _, (gr["out_norm_w"],) = _rowcall(
        body, name="b_outnorm", T=T, S=S, tr=tr,
        rows=[(sv["oa"], 768, 0), (sv["ob"], 768, 0), (sv["yc"], C_WIDTH, 0), (dy, D_MODEL, 0)], pars=[w["out_norm_w"]],
        row_outs=[(768, BF16, 768, 0), (768, BF16, 768, 0), (C_WIDTH, F32, C_WIDTH, 0)], par_outs=[(1, D_MODEL)])

    ta, tb, band = _attn_tiles(S)
    if own_early:
        host = dict(host)
        host["attn_b_bwd"] = list(host.get("attn_b_bwd", ())) + [
            gr["w_gate_up"], gr["w_down"].reshape(N_CHIPS, FFN_HIDDEN // N_CHIPS, D_MODEL)]
    dQ, dK, dV, _ = _hosted(_attn_bwd, host, got, "a2a", "attn_b_bwd", sv["Q"], sv["K"], sv["V"], sv["ob"], sv["lse_b"],
                            dob, None, B=B, S=S, HQ=B_HEADS, HK=B_HEADS, window=None,
                            scale=(B_NOPE + B_ROPE) ** -0.5, tq=tb, band=band)
    dqa, dka, dva, dsink = _hosted(_attn_bwd, host, got, "a2a", "attn_a_bwd", sv["qa"], sv["ka"], sv["va"], sv["oa"],
                                   sv["lse_a"], doa, w["sinks"], B=B, S=S, HQ=A_Q_HEADS, HK=A_KV_HEADS,
                                   window=A_WINDOW, scale=HEAD_DIM ** -0.5, tq=ta, band=None)
    gr["sinks"] = dsink

    def body(rv, ev, pv):
        dQv, dKv, dVv, cbv, sbv = rv
        dq = [_rope_bwd((cbv, sbv), p)[0] for p in _heads(dQv, B_HEADS)]
        dkr = None
        for p in _heads(dKv, B_HEADS):
            dkr = p if dkr is None else dkr + p
        return [dq, [dKv, dVv], dkr], [], []
    (dq, dkv, dkr), _, _ = _rowcall(
        body, name="b_mlaprep", T=T, S=S, tr=tr,
        rows=[(dQ, 768, 0), (dK, 768, 0), (dV, 768, 0), (cb, LANES, 0), (sb, LANES, 0)],
        row_outs=[(768, BF16, 768, 0), (1536, BF16, 1536, 0), (LANES, F32, LANES, 0)])
    dcqn = _matmul(dq, w["b_w_uq"], "nt", BF16, "mm_uq_dx")
    gr["b_w_uq"] = _matmul(sv["cqn"], dq, "tn", BF16, "mm_uq_dw")
    dckvn = _matmul(dkv, w["b_w_ukv"], "nt", BF16, "mm_ukv_dx")
    gr["b_w_ukv"] = _matmul(sv["ckvn"], dkv, "tn", BF16, "mm_ukv_dw")

    def body(rv, ev, pv):
        proj, cav, sav, cbv, sbv, dqa_, dka_, dva_, dcqn_, dckvn_, dkr_ = rv
        _, vjp = jax.vjp(lambda p, a, b_: _f_mixprep(p, cav, sav, cbv, sbv, a, b_), proj, pv[0], pv[1])
        dproj, dqn, dkvn = vjp((dqa_, dka_, dva_, dcqn_, dckvn_, dkr_))
        return [dproj], [], [dqn, dkvn]
    (dproj,), _, (gr["b_q_norm_w"], gr["b_kv_norm_w"]) = _rowcall(
        body, name="b_mixprep", T=T, S=S, tr=tr,
        rows=[(sv["proj"], P_CU, 0), (ca, LANES, 0), (sa, LANES, 0), (cb, LANES, 0), (sb, LANES, 0),
              (dqa, 768, 0), (dka, 256, 0), (dva, 256, 0), (dcqn, 384, 0), (dckvn, 256, 0), (dkr, LANES, 0)],
        pars=[w["b_q_norm_w"], w["b_kv_norm_w"]], row_outs=[(P_END, BF16, P_CU, 0)],
        par_outs=[(1, B_Q_RANK), (1, B_KV_RANK)])

    ts = _sgu_rows(S)

    def body(rv, ev, pv):
        cu, cv, dycv, _ = rv
        dcus, dcvs, acc = [], [], None
        for r in range(0, ts, C_CHUNK):
            _, vjp = jax.vjp(_f_sgu, cu[r:r + C_CHUNK], cv[r:r + C_CHUNK], pv[0], pv[1], pv[2], pv[3])
            dcu, dcv, *dpar = vjp(dycv[r:r + C_CHUNK])
            dcus.append(dcu)
            dcvs.append(dcv)
            acc = dpar if acc is None else [a + b_ for a, b_ in zip(acc, dpar)]
        return [[jnp.concatenate(dcus, axis=0), jnp.concatenate(dcvs, axis=0)]], [], acc
    (dproj,), _, (gr["c_ln_w"], gr["c_ln_b"], gr["c_w_s"], gr["c_b_col"]) = _rowcall(
        body, name="b_sgu", T=T, S=S, tr=ts,
        rows=[(sv["proj"], C_WIDTH, P_CU // C_WIDTH), (sv["proj"], C_WIDTH, P_CV // C_WIDTH), (dyc, C_WIDTH, 0),
              (dproj, 2 * C_WIDTH, P_CU // (2 * C_WIDTH))],
        pars=[w["c_ln_w"], w["c_ln_b"], w["c_w_s"], w["c_b_col"]],
        row_outs=[(P_END, BF16, 2 * C_WIDTH, P_CU // (2 * C_WIDTH))],
        par_outs=[(1, C_WIDTH), (1, C_WIDTH), (C_GROUPS, C_CHUNK, C_CHUNK), (C_GROUPS, C_CHUNK, 1)],
        aliases={3: 0})
    dh = _mm(host, got, "a2a", dproj, w["w_in"], "nt", BF16, "mm_in_dx")
    gr["w_in"] = _matmul(sv["h"], dproj, "tn", BF16, "mm_in_dw")

    if prev is None:
        def body(rv, ev, pv):
            xv, dhv, dxd = rv
            _, vjp = jax.vjp(_f_norm_mod, xv, pv[0], ev[0], ev[1])
            dxa, dw, dsc, dsh = vjp(dhv)
            return [dxa + dxd], [dsc, dsh], [dw]
        (dxin,), (dsc1, dsh1), (gr["norm1_w"],) = _rowcall(
            body, name="b_norm_mod", T=T, S=S, tr=tr, rows=[(sv["x"], D_MODEL, 0), (dh, D_MODEL, 0), (dx, D_MODEL, 0)],
            exs=[sc1, sh1], pars=[w["norm1_w"]], row_outs=[(D_MODEL, F32, D_MODEL, 0)], ex_outs=[D_MODEL] * 2,
            par_outs=[(1, D_MODEL)])
        nxt = (dxin, None, None)
    else:
        x1p, dp, g2p = prev

        def body(rv, ev, pv):
            xa, delta, dhv, dxn = rv
            _, vjp = jax.vjp(_f_resid_norm_mod, xa, delta, ev[0], pv[0], ev[1], ev[2])
            dxa, ddelta, dg, dw, dsc, dsh = vjp((dxn, dhv))
            return [dxa, ddelta], [dg, dsc, dsh], [dw]
        (dx1p, ddp), (dg2p, dsc1, dsh1), (gr["norm1_w"],) = _rowcall(
            body, name="b_resid_norm_mod1", T=T, S=S, tr=tr,
            rows=[(x1p, D_MODEL, 0), (dp, D_MODEL, 0), (dh, D_MODEL, 0), (dx, D_MODEL, 0)],
            exs=[g2p, sc1, sh1], pars=[w["norm1_w"]],
            row_outs=[(D_MODEL, F32, D_MODEL, 0), (D_MODEL, BF16, D_MODEL, 0)], ex_outs=[D_MODEL] * 3,
            par_outs=[(1, D_MODEL)])
        nxt = (dx1p, ddp, dg2p)
    return gr, (dsh1, dsc1, dg1, dsh2, dsc2), nxt, got


def _lane_table(lanes_neg, lanes_pos, inv):
    freq = np.zeros((LANES,), np.int64) - 1
    sign = np.zeros((1, LANES), np.float32)
    n = len(lanes_neg)
    freq[lanes_neg] = np.arange(n)
    freq[lanes_pos] = np.arange(n)
    sign[0, lanes_neg] = -1.0
    sign[0, lanes_pos] = 1.0
    return _select_axis(inv, freq, 0).reshape(1, LANES), jnp.asarray(sign)


SHARDED = ("ada_w", "w_in", "b_w_uq", "b_w_ukv", "w_out", "w_gate_up", "w_down")
ROW_SHARDED = ("w_out", "w_down")
SMALL = ("ada_b", "norm1_w", "a_sinks", "b_q_norm_w", "b_kv_norm_w", "c_ln_w", "c_ln_b", "c_w_s", "c_b_s",
         "out_norm_w", "norm2_w", "final_norm_w")
FWD_HOST = {"attn_b_fwd": ("w_gate_up", "w_down"), "attn_a_fwd": ("w_in", "w_out", "b_w_uq", "b_w_ukv")}
FWD_HOST_FIRST = {"attn_b_fwd": ("w_gate_up", "w_down"), "mm_gate_up": ("w_in", "w_out", "b_w_uq", "b_w_ukv")}
BWD_HOST = {"attn_a_bwd": ("w_in", "w_out", "b_w_uq", "b_w_ukv"), "attn_b_bwd": ("w_gate_up", "w_down")}
FIRST_LATE = {"mm_in": ("b_w_uq", "b_w_ukv", "w_out"), "attn_a_fwd": ("w_gate_up", "w_down")}
EXPOSED = ("w_in", "b_w_uq", "b_w_ukv", "w_out")
OWN_LAYER = ("w_gate_up", "w_down")
EXCHANGED = SHARDED[1:]
N_ADA = N_MOD * D_MODEL // N_CHIPS
assert not set(FIRST_LATE) & set(FWD_HOST_FIRST)


def _from_host(table, got):
    return {k: got[name][i] for name, ks in table.items() for i, k in enumerate(ks)}


def _to_host(table, arrays):
    return {name: [arrays[k] for k in ks] for name, ks in table.items()}


def _join_cols(g):
    return jnp.concatenate([g[j] for j in range(N_CHIPS)], axis=1)


def _split_cols(g):
    n = g.shape[1] // N_CHIPS
    return jnp.stack([g[:, j * n:(j + 1) * n] for j in range(N_CHIPS)])


def _layer_weights(G, small, l):
    D = D_MODEL
    return {
        **_large_operands(G),
        "norm1_w": small["norm1_w"][l].reshape(1, D),
        "sinks": jnp.broadcast_to(small["a_sinks"][l].reshape(A_Q_HEADS, 1, 1), (A_Q_HEADS, 1, LANES)),
        "b_q_norm_w": small["b_q_norm_w"][l].reshape(1, B_Q_RANK),
        "b_kv_norm_w": small["b_kv_norm_w"][l].reshape(1, B_KV_RANK),
        "c_ln_w": small["c_ln_w"][l].reshape(1, C_WIDTH), "c_ln_b": small["c_ln_b"][l].reshape(1, C_WIDTH),
        "c_w_s": small["c_w_s"][l], "c_b_col": small["c_b_s"][l].reshape(C_GROUPS, C_CHUNK, 1),
        "out_norm_w": small["out_norm_w"][l].reshape(1, D), "norm2_w": small["norm2_w"][l].reshape(1, D),
    }


def _large_operands(G):
    D = D_MODEL
    make = {
        "w_in": lambda g: _pad_axis(_join_cols(g), _map_w_in(), 1),
        "b_w_uq": lambda g: _pad_axis(_join_cols(g), _map_w_uq(), 1),
        "b_w_ukv": lambda g: _pad_axis(_join_cols(g), _map_w_ukv(), 1),
        "w_out": lambda g: g.reshape(D, D), "w_gate_up": lambda g: g, "w_down": lambda g: g.reshape(FFN_HIDDEN, D),
    }
    return {k: make[k](g) for k, g in G.items()}


def _send_buffers(gr):
    D = D_MODEL
    return {
        "w_gate_up": gr["w_gate_up"], "w_down": gr["w_down"].reshape(N_CHIPS, FFN_HIDDEN // N_CHIPS, D),
        "w_out": gr["w_out"].reshape(N_CHIPS, D // N_CHIPS, D),
        "w_in": _split_cols(_unpad_axis(gr["w_in"], _map_w_in(), IN_COLS, 1)),
        "b_w_uq": _split_cols(_unpad_axis(gr["b_w_uq"], _map_w_uq(), B_HEADS * (B_NOPE + B_ROPE), 1)),
        "b_w_ukv": _split_cols(_unpad_axis(gr["b_w_ukv"], _map_w_ukv(), B_HEADS * (B_NOPE + B_V), 1)),
    }


def _small_grads(gr, B):
    D = D_MODEL
    return {
        "norm1_w": gr["norm1_w"].reshape(D),
        "a_sinks": gr["sinks"][:, 0, 0].reshape(B, A_Q_HEADS).sum(axis=0),
        "b_q_norm_w": gr["b_q_norm_w"].reshape(B_Q_RANK), "b_kv_norm_w": gr["b_kv_norm_w"].reshape(B_KV_RANK),
        "c_ln_w": gr["c_ln_w"].reshape(C_WIDTH), "c_ln_b": gr["c_ln_b"].reshape(C_WIDTH), "c_w_s": gr["c_w_s"],
        "c_b_s": gr["c_b_col"].reshape(C_GROUPS, C_CHUNK),
        "out_norm_w": gr["out_norm_w"].reshape(D), "norm2_w": gr["norm2_w"].reshape(D),
    }


def _step(x, c, positions, target, small, shard_of, ada=None, gathered=None):
    dist = gathered is None
    B, S, D = x.shape
    T = B * S
    xt = x.reshape(T, D)
    tgt = target.reshape(T, D)
    pos_col = positions.astype(F32).reshape(T, 1)
    inv_a = 1.0 / (ROPE_THETA ** (jnp.arange(0, HEAD_DIM, 2, dtype=F32) / HEAD_DIM))
    inv_b = 1.0 / (ROPE_THETA ** (jnp.arange(0, B_ROPE, 2, dtype=F32) / B_ROPE))
    fa, sga = _lane_table(np.arange(32), 64 + np.arange(32), inv_a)
    fb, sgb = _lane_table(48 + np.arange(16), 112 + np.arange(16), inv_b)
    ca, sa = _rope_tables(pos_col, fa, sga, "rope_a")
    cb, sb = _rope_tables(pos_col, fb, sgb, "rope_b")
    tabs = (ca, sa, cb, sb)
    ada_b = small["ada_b"]
    if dist:
        assert N_CHIPS * B == 8
        me = 2 * lax.axis_index("x") + lax.axis_index("y")
        first = shard_of(0)
        c_all, w_in_first = _xchg_call("gather", [c, first["w_in"]], "gather_first")
        c8 = c_all.reshape(N_CHIPS * B, D)
        mine = lax.dynamic_slice_in_dim(ada_b, me * N_ADA, N_ADA, axis=1).reshape(DEPTH, 1, N_ADA)
        part = _ada_fwd(c8, ada, mine)
        part = part.reshape(DEPTH, N_CHIPS, B, N_ADA).transpose(1, 0, 2, 3).reshape(N_CHIPS, DEPTH * B, N_ADA)
        (back,) = _xchg_call("a2a", [part], "mod_exchange")
        mod_all = jnp.concatenate([back[j].reshape(DEPTH, B, N_ADA) for j in range(N_CHIPS)], axis=-1)
        G = {"w_in": w_in_first}
    else:
        c8 = jnp.zeros((8, D), F32).at[:B].set(c)
        cols = [(jnp.stack([gathered[l]["ada_w"][j] for l in range(DEPTH)]),
                 ada_b[:, j * N_ADA:(j + 1) * N_ADA].reshape(DEPTH, 1, N_ADA)) for j in range(N_CHIPS)]
        mod_all = jnp.concatenate([_ada_fwd(c8, wj, bj)[:, :B] for wj, bj in cols], axis=-1)
        G = {k: v for k, v in gathered[0].items() if k != "ada_w"}
    saved, prevs, modss = [], [], []
    prev = None
    for l in range(DEPTH):
        w = _layer_weights(G, small, l)
        mods = tuple(mod_all[l, :, i * D:(i + 1) * D].reshape(B, 1, D) for i in range(N_MOD))
        more = l + 1 < DEPTH
        table = FWD_HOST_FIRST if l == 0 else FWD_HOST
        host = _to_host(table, shard_of(l + 1)) if dist and more else {}
        late = None
        if dist and l == 0:
            late = FIRST_LATE
            for name, ks in late.items():
                host[name] = [first[k] for k in ks] + host.get(name, [])
        prevs.append(prev)
        modss.append(mods)
        prev, sv, got = _layer_fwd(xt, prev, mods, w, tabs, B, S, host, late)
        saved.append(sv)
        if more:
            G = _from_host(table, got) if dist else {k: v for k, v in gathered[l + 1].items() if k != "ada_w"}
    x1, d, g2 = prev
    loss, dx1, dd, dg2, dfw = _final(x1, d, g2, small["final_norm_w"].reshape(1, D), tgt, B, S)

    landed = [None] * DEPTH
    smalls = [None] * DEPTH
    dmods = [None] * DEPTH
    pending = None
    for l in reversed(range(DEPTH)):
        host = _to_host(BWD_HOST, pending) if dist and pending is not None else None
        early = dist and l == 0 and host is not None
        gr, (dsh1, dsc1, dg1, dsh2, dsc2), nxt, got = _layer_bwd(saved[l], prevs[l], modss[l], tabs, dx1, dd, B, S, host, early)
        if pending is not None:
            landed[l + 1] = _from_host(BWD_HOST, got) if dist else pending
        dmods[l] = jnp.concatenate([dsh1, dsc1, dg1, dsh2, dsc2, dg2], axis=-1).reshape(B, N_MOD * D)
        pending = _send_buffers(gr)
        smalls[l] = _small_grads(gr, B)
        dx1, dd, dg2 = nxt
    dmod_all = jnp.stack(dmods)
    by_chip = [dmod_all[:, :, j * N_ADA:(j + 1) * N_ADA] for j in range(N_CHIPS)]
    if dist:
        last = EXPOSED if early else EXCHANGED
        send = jnp.stack([p.reshape(DEPTH * B, N_ADA) for p in by_chip])
        *res, back = _xchg_call("a2a", [pending[k] for k in last] + [send], "grad_exchange_last")
        landed[0] = dict(zip(last, res))
        if early:
            landed[0].update(zip(OWN_LAYER, got["attn_b_bwd"][-len(OWN_LAYER):]))
        dmod8 = back.reshape(N_CHIPS, DEPTH, B, N_ADA).transpose(1, 0, 2, 3).reshape(DEPTH, N_CHIPS * B, N_ADA)
        ada_g = _ada_bwd(c8, dmod8)
    else:
        landed[0] = pending
        ada_g = jnp.stack([_ada_bwd(c8, jnp.zeros((DEPTH, 8, N_ADA), F32).at[:, :B].set(p)) for p in by_chip])
    small_g = {k: jnp.stack([smalls[l][k] for l in range(DEPTH)]) for k in SMALL if k not in ("final_norm_w", "ada_b")}
    small_g["ada_b"] = _sum_examples(dmod_all).reshape(DEPTH, N_MOD * D)
    small_g["final_norm_w"] = dfw.reshape(D)
    return loss[0, 0], dx1.reshape(B, S, D), landed, small_g, ada_g


def _pack(arrs, cols, mult):
    flat = jnp.concatenate([a.reshape(-1) for a in arrs])
    n = flat.shape[0]
    rows = -(-n // cols)
    rows = -(-rows // mult) * mult
    return jnp.pad(flat, (0, rows * cols - n)).reshape(rows, cols)


def _unpack(blob, shapes):
    flat = blob.reshape(-1)
    out, off = [], 0
    for s in shapes:
        n = int(np.prod(s))
        out.append(flat[off:off + n].reshape(s))
        off += n
    return out


def kernel(x, c, positions, ada_w, ada_b, norm1_w, w_in, a_sinks, b_q_norm_w, b_w_uq, b_kv_norm_w, b_w_ukv, c_ln_w, c_ln_b, c_w_s, c_b_s, out_norm_w, w_out, norm2_w, w_gate_up, w_down, final_norm_w, loss_target, m_ada_w, m_ada_b, m_norm1_w, m_w_in, m_a_sinks, m_b_q_norm_w, m_b_w_uq, m_b_kv_norm_w, m_b_w_ukv, m_c_ln_w, m_c_ln_b, m_c_w_s, m_c_b_s, m_out_norm_w, m_w_out, m_norm2_w, m_w_gate_up, m_w_down, m_final_norm_w, v_ada_w, v_ada_b, v_norm1_w, v_w_in, v_a_sinks, v_b_q_norm_w, v_b_w_uq, v_b_kv_norm_w, v_b_w_ukv, v_c_ln_w, v_c_ln_b, v_c_w_s, v_c_b_s, v_out_norm_w, v_w_out, v_norm2_w, v_w_gate_up, v_w_down, v_final_norm_w):
    names = ("ada_w", "ada_b", "norm1_w", "w_in", "a_sinks", "b_q_norm_w", "b_w_uq", "b_kv_norm_w", "b_w_ukv", "c_ln_w",
             "c_ln_b", "c_w_s", "c_b_s", "out_norm_w", "w_out", "norm2_w", "w_gate_up", "w_down", "final_norm_w")
    ws = dict(zip(names, (ada_w, ada_b, norm1_w, w_in, a_sinks, b_q_norm_w, b_w_uq, b_kv_norm_w, b_w_ukv, c_ln_w, c_ln_b,
                          c_w_s, c_b_s, out_norm_w, w_out, norm2_w, w_gate_up, w_down, final_norm_w)))
    ms = dict(zip(names, (m_ada_w, m_ada_b, m_norm1_w, m_w_in, m_a_sinks, m_b_q_norm_w, m_b_w_uq, m_b_kv_norm_w, m_b_w_ukv,
                          m_c_ln_w, m_c_ln_b, m_c_w_s, m_c_b_s, m_out_norm_w, m_w_out, m_norm2_w, m_w_gate_up, m_w_down,
                          m_final_norm_w)))
    vs = dict(zip(names, (v_ada_w, v_ada_b, v_norm1_w, v_w_in, v_a_sinks, v_b_q_norm_w, v_b_w_uq, v_b_kv_norm_w, v_b_w_ukv,
                          v_c_ln_w, v_c_ln_b, v_c_w_s, v_c_b_s, v_out_norm_w, v_w_out, v_norm2_w, v_w_gate_up, v_w_down,
                          v_final_norm_w)))
    shards = {k: ws[k].astype(BF16) for k in SHARDED}
    loss_local, grad_x, landed, gsmall, ada_g = _step(
        x, c, positions, loss_target, {k: ws[k] for k in SMALL}, lambda l: {k: shards[k][l] for k in EXCHANGED},
        ada=shards["ada_w"])

    mine = {k: jnp.stack([_sum_slots(landed[l][k], "grad_sum_" + k) for l in range(DEPTH)]) for k in EXCHANGED}
    mine["ada_w"] = ada_g
    theirs = dict(zip(SHARDED, _xchg_call("swap", [mine[k] for k in SHARDED], "grad_sibling_swap")))
    grads, delta, new_m, new_v = {}, {}, {}, {}
    for k in SHARDED:
        shp = ws[k].shape
        two = (shp[0] * shp[1], shp[2])
        g, dlt, nm, nv = _adamw(ws[k].reshape(two), mine[k].reshape(two), theirs[k].reshape(two), ms[k].reshape(two),
                                vs[k].reshape(two), "adamw_" + k)
        grads[k], delta[k], new_m[k], new_v[k] = g.reshape(shp), dlt.reshape(shp), nm.reshape(shp), nv.reshape(shp)

    small_shapes = [ws[k].shape for k in SMALL]
    sblob = _pack([gsmall[k] for k in SMALL] + [loss_local.reshape(1)], LANES, 8)
    svals = _unpack(_all_reduce_small(sblob, "small_all_reduce"), small_shapes + [(1,)])
    loss = svals[-1].reshape(())
    pw = _pack([ws[k] for k in SMALL], LANES, 8)
    pg = _pack(svals[:-1], LANES, 8)
    pm = _pack([ms[k] for k in SMALL], LANES, 8)
    pv = _pack([vs[k] for k in SMALL], LANES, 8)
    g, dlt, nm, nv = _adamw(pw, pg, None, pm, pv, "adamw_small")
    for k, a, b_, c_, d_ in zip(SMALL, _unpack(g, small_shapes), _unpack(dlt, small_shapes), _unpack(nm, small_shapes),
                                _unpack(nv, small_shapes)):
        grads[k], delta[k], new_m[k], new_v[k] = a, b_, c_, d_

    return (loss, grad_x, *[grads[k] for k in names], *[delta[k] for k in names], *[new_m[k] for k in names],
            *[new_v[k] for k in names])
```

```python
import functools
import math

import numpy as np
import jax
import jax.numpy as jnp
from jax import lax
from jax.experimental import pallas as pl
from jax.experimental.pallas import tpu as pltpu

F32 = jnp.float32
BF16 = jnp.bfloat16

D_MODEL = 1024
DEPTH = 4
HEAD_DIM = 64
ROPE_THETA = 10000.0
NORM_EPS = 1e-6
NEG_INF = -1e30
LOG2_E = math.log2(math.e)
A_Q_HEADS = 6
A_KV_HEADS = 2
A_WINDOW = 128
B_HEADS = 6
B_Q_RANK = 384
B_KV_RANK = 256
B_NOPE = 64
B_ROPE = 32
B_V = 64
C_GROUPS = 4
C_GROUP_DIM = 64
C_WIDTH = 256
C_CHUNK = 128
IN_COLS = 1824
FFN_HIDDEN = 2816
N_MOD = 6
ADAM_LR = 0.001
ADAM_B1 = 0.9
ADAM_B2 = 0.999
ADAM_EPS = 1e-08
ADAM_WD = 0.01
ADAM_STEP = 10

LANES = 128
VMEM_LIMIT = 56 * 1024 * 1024
N_CHIPS = 4
WINDOW_SUB = 256

P_KR, P_AQ, P_AK, P_AV, P_CQ, P_CKV, P_MIX, P_CU, P_CV, P_END = 0, 128, 512, 640, 768, 1152, 1408, 1536, 1792, 2048


def _map_w_in():
    idx = -np.ones(P_END, np.int64)
    idx[P_AQ:P_AQ + 384] = np.arange(384)
    idx[P_AK:P_AK + 128] = 384 + np.arange(128)
    idx[P_AV:P_AV + 128] = 512 + np.arange(128)
    idx[P_CQ:P_CQ + 384] = 640 + np.arange(384)
    idx[P_CKV:P_CKV + 256] = 1024 + np.arange(256)
    idx[P_KR + 48 + np.arange(16)] = 1280 + np.arange(16)
    idx[P_KR + 112 + np.arange(16)] = 1296 + np.arange(16)
    idx[P_CU:P_CU + 256] = 1312 + np.arange(256)
    idx[P_CV:P_CV + 256] = 1568 + np.arange(256)
    return idx


def _map_w_uq():
    idx = -np.ones(B_HEADS * LANES, np.int64)
    for h in range(B_HEADS):
        b = h * (B_NOPE + B_ROPE)
        idx[h * LANES + np.arange(48)] = b + np.arange(48)
        idx[h * LANES + 48 + np.arange(16)] = b + 64 + np.arange(16)
        idx[h * LANES + 64 + np.arange(16)] = b + 48 + np.arange(16)
        idx[h * LANES + 112 + np.arange(16)] = b + 80 + np.arange(16)
    return idx


def _map_w_ukv():
    idx = -np.ones(2 * B_HEADS * LANES, np.int64)
    for h in range(B_HEADS):
        b = h * (B_NOPE + B_V)
        idx[h * LANES + np.arange(48)] = b + np.arange(48)
        idx[h * LANES + 64 + np.arange(16)] = b + 48 + np.arange(16)
        idx[B_HEADS * LANES + h * LANES + np.arange(B_V)] = b + B_NOPE + np.arange(B_V)
    return idx


def _inverse(idx, n):
    inv = np.zeros(n, np.int64)
    pos = np.nonzero(idx >= 0)[0]
    inv[idx[pos]] = pos
    return inv


def _runs(idx):
    runs, i, n = [], 0, len(idx)
    while i < n:
        j = i + 1
        while j < n and ((idx[i] < 0 and idx[j] < 0) or (idx[i] >= 0 and idx[j] == idx[i] + (j - i))):
            j += 1
        runs.append((int(idx[i]), j - i))
        i = j
    return runs


def _select_axis(w, idx, axis):
    pieces = []
    for start, length in _runs(idx):
        if start < 0:
            shape = list(w.shape)
            shape[axis] = length
            pieces.append(jnp.zeros(shape, w.dtype))
        else:
            pieces.append(lax.slice_in_dim(w, start, start + length, axis=axis))
    return jnp.concatenate(pieces, axis=axis)


def _pad_axis(w, idx, axis):
    return _select_axis(w, idx, axis)


def _unpad_axis(g, idx, n, axis):
    return _select_axis(g, _inverse(idx, n), axis)


def _params(sem):
    return pltpu.CompilerParams(dimension_semantics=sem, vmem_limit_bytes=VMEM_LIMIT)


def _tile(dim, target):
    if dim <= target:
        return dim
    best = None
    for t in range(LANES, target + 1, LANES):
        if dim % t == 0:
            best = t
    assert best is not None, dim
    return best


def _row_div(rows, target):
    if rows <= target:
        return rows
    best = None
    for t in range(8, target + 1, 8):
        if rows % t == 0:
            best = t
    assert best is not None, rows
    return best


_ANY = pl.BlockSpec(memory_space=pl.ANY)
_MESH = pl.DeviceIdType.MESH


def _xchg_out_shapes(kind, srcs):
    if kind == "gather":
        return [jax.ShapeDtypeStruct((N_CHIPS,) + s.shape, s.dtype) for s in srcs]
    return [jax.ShapeDtypeStruct(s.shape, s.dtype) for s in srcs]


def _xchg_scratch(kind, n):
    per = 1 if kind == "swap" else N_CHIPS - 1
    return [pltpu.SemaphoreType.DMA((per * n,)), pltpu.SemaphoreType.DMA((per * n,)), pltpu.SemaphoreType.DMA((n,))]


def _xchg_copies(kind, srcs, dsts, send_sems, recv_sems, local_sems, arrivals):
    x, y, c = lax.axis_index("x"), lax.axis_index("y"), lax.axis_index("c")
    me = 2 * x + y
    peers = [(1 - x, y), (x, 1 - y), (1 - x, 1 - y)]
    local, out, back = [], [], []
    for i, (s, d) in enumerate(zip(srcs, dsts)):
        if kind == "swap":
            cp = pltpu.make_async_remote_copy(src_ref=s, dst_ref=d, send_sem=send_sems.at[i], recv_sem=recv_sems.at[i],
                                              device_id=(x, y, 1 - c), device_id_type=_MESH)
            out.append(cp)
            back.append(cp)
            continue
        local.append(pltpu.make_async_copy(s if kind == "gather" else s.at[me], d.at[me], local_sems.at[i]))
        for kk, (px, py) in enumerate(peers):
            j = (N_CHIPS - 1) * i + kk
            theirs = 2 * px + py
            out.append(pltpu.make_async_remote_copy(
                src_ref=s if kind == "gather" else s.at[theirs], dst_ref=d.at[me], send_sem=send_sems.at[j],
                recv_sem=recv_sems.at[j], device_id=(px, py, c), device_id_type=_MESH))
            if arrivals:
                back.append(pltpu.make_async_remote_copy(
                    src_ref=s if kind == "gather" else s.at[me], dst_ref=d.at[theirs], send_sem=send_sems.at[j],
                    recv_sem=recv_sems.at[j], device_id=(px, py, c), device_id_type=_MESH))
    return local, out, back


def _xchg_start(kind, srcs, dsts, sems):
    local, out, _ = _xchg_copies(kind, srcs, dsts, *sems, arrivals=False)
    for cp in local + out:
        cp.start()


def _xchg_wait(kind, srcs, dsts, sems):
    local, out, back = _xchg_copies(kind, srcs, dsts, *sems, arrivals=True)
    for cp in back:
        cp.wait_recv()
    for cp in out:
        cp.wait_send()
    for cp in local:
        cp.wait()


def _xchg_at_ends(kind, srcs, dsts, sems, grid, first):
    ids = [pl.program_id(a) for a in range(len(grid))]
    cond = None
    for i, n in zip(ids, grid):
        c = (i == 0) if first else (i == n - 1)
        cond = c if cond is None else jnp.logical_and(cond, c)

    @pl.when(cond)
    def _():
        (_xchg_start if first else _xchg_wait)(kind, srcs, dsts, sems)


def _xchg_call(kind, srcs, name):
    n = len(srcs)

    def kern(*refs):
        s, d, sems = refs[:n], refs[n:2 * n], refs[2 * n:]
        _xchg_start(kind, s, d, sems)
        _xchg_wait(kind, s, d, sems)

    return pl.pallas_call(
        kern, name=name, in_specs=[_ANY] * n, out_specs=[_ANY] * n, out_shape=_xchg_out_shapes(kind, srcs),
        scratch_shapes=_xchg_scratch(kind, n),
    )(*srcs)


_DIMS = {"nn": (((1,), (0,)), ((), ())), "nt": (((1,), (1,)), ((), ())), "tn": (((0,), (0,)), ((), ()))}


def _matmul(a, b, mode, out_dtype, name, *, b_chunks=False, out_chunks=False, xchg=None):
    if b_chunks:
        nchunk, brows, bcols = b.shape
        bshape = (brows, nchunk * bcols)
    else:
        bshape = b.shape
    if mode == "nn":
        (m, k), (_, n) = a.shape, bshape
    elif mode == "nt":
        (m, k), (n, _) = a.shape, bshape
    else:
        (k, m), (_, n) = a.shape, bshape
    tm, tk = (1408, 1024) if mode == "tn" else (1024, 1408)
    tm, tn, tk = _tile(m, tm), _tile(n, 1408), _tile(k, tk)
    if b_chunks:
        if mode == "nn":
            tn = bcols
        else:
            assert mode == "nt"
            tk = bcols
    if out_chunks:
        assert n % N_CHIPS == 0
        tn = n // N_CHIPS
    ni, nj, nk = m // tm, n // tn, k // tk
    dims = _DIMS[mode]
    n_x = 0 if xchg is None else len(xchg[1])

    def kern(*refs):
        a_ref, b_ref = refs[0], refs[1]
        xs = refs[2:2 + n_x]
        o_ref = refs[2 + n_x]
        xd = refs[3 + n_x:3 + 2 * n_x]
        acc_ref = refs[3 + 2 * n_x]
        sems = refs[4 + 2 * n_x:]
        kk = pl.program_id(2)
        if n_x:
            _xchg_at_ends(xchg[0], xs, xd, sems, (ni, nj, nk), True)

        @pl.when(kk == 0)
        def _():
            acc_ref[...] = jnp.zeros_like(acc_ref)

        acc_ref[...] += lax.dot_general(a_ref[...], b_ref[...], dims, preferred_element_type=F32)

        @pl.when(kk == nk - 1)
        def _():
            o_ref[...] = acc_ref[...].astype(o_ref.dtype)

        if n_x:
            _xchg_at_ends(xchg[0], xs, xd, sems, (ni, nj, nk), False)

    if mode == "tn":
        a_spec = pl.BlockSpec((tk, tm), lambda i, j, kk: (kk, i))
    else:
        a_spec = pl.BlockSpec((tm, tk), lambda i, j, kk: (i, kk))
    if b_chunks and mode == "nn":
        b_spec = pl.BlockSpec((None, tk, tn), lambda i, j, kk: (j, kk, 0))
    elif b_chunks:
        b_spec = pl.BlockSpec((None, tn, tk), lambda i, j, kk: (kk, j, 0))
    elif mode == "nt":
        b_spec = pl.BlockSpec((tn, tk), lambda i, j, kk: (j, kk))
    else:
        b_spec = pl.BlockSpec((tk, tn), lambda i, j, kk: (kk, j))
    if out_chunks:
        o_spec = pl.BlockSpec((None, tm, tn), lambda i, j, kk: (j, i, 0))
        o_shape = jax.ShapeDtypeStruct((N_CHIPS, m, tn), out_dtype)
    else:
        o_spec = pl.BlockSpec((tm, tn), lambda i, j, kk: (i, j))
        o_shape = jax.ShapeDtypeStruct((m, n), out_dtype)
    xs = [] if xchg is None else list(xchg[1])
    res = pl.pallas_call(
        kern, name=name, grid=(ni, nj, nk),
        in_specs=[a_spec, b_spec] + [_ANY] * n_x, out_specs=[o_spec] + [_ANY] * n_x,
        out_shape=[o_shape] + (_xchg_out_shapes(xchg[0], xs) if n_x else []),
        scratch_shapes=[pltpu.VMEM((tm, tn), F32)] + (_xchg_scratch(xchg[0], n_x) if n_x else []),
        compiler_params=_params(("arbitrary", "arbitrary", "arbitrary") if n_x else ("parallel", "parallel", "arbitrary")),
    )(a, b, *xs)
    return (res[0], list(res[1:])) if n_x else res[0]


def _rowcall(body, *, name, T, S, tr, rows, exs=(), pars=(), row_outs=(), ex_outs=(), par_outs=(), aliases=None):
    assert S % tr == 0 and T % S == 0
    per_ex = S // tr
    nb = T // S
    n_rows, n_exs, n_pars = len(rows), len(exs), len(pars)
    n_ro, n_eo, n_po = len(row_outs), len(ex_outs), len(par_outs)

    def kern(*refs):
        ins = refs[:n_rows + n_exs + n_pars]
        outs = refs[n_rows + n_exs + n_pars:]
        rv = [r[...].astype(F32) for r in ins[:n_rows]]
        ev = [r[0] for r in ins[n_rows:n_rows + n_exs]]
        pv = [r[...] for r in ins[n_rows + n_exs:]]
        ro, eo, po = body(rv, ev, pv)
        i = pl.program_id(0)
        for ref, val in zip(outs[:n_ro], ro):
            if isinstance(val, (list, tuple)):
                off = 0
                for piece in val:
                    w = piece.shape[-1]
                    ref[:, off:off + w] = piece.astype(ref.dtype)
                    off += w
            else:
                ref[...] = val.astype(ref.dtype)
        first_of_ex = (i % per_ex) == 0
        for ref, val in zip(outs[n_ro:n_ro + n_eo], eo):
            @pl.when(first_of_ex)
            def _(ref=ref, val=val):
                ref[0] = val

            @pl.when(jnp.logical_not(first_of_ex))
            def _(ref=ref, val=val):
                ref[0] += val
        for ref, val in zip(outs[n_ro + n_eo:], po):
            @pl.when(i == 0)
            def _(ref=ref, val=val):
                ref[...] = val

            @pl.when(i != 0)
            def _(ref=ref, val=val):
                ref[...] += val

    in_specs = [pl.BlockSpec((tr, w), functools.partial(lambda i, cb: (i, cb), cb=cb)) for (_, w, cb) in rows]
    in_specs += [pl.BlockSpec((1, 1, e.shape[-1]), lambda i: (i // per_ex, 0, 0)) for e in exs]
    in_specs += [pl.BlockSpec(p.shape, functools.partial(lambda i, nd: (0,) * nd, nd=p.ndim)) for p in pars]
    out_specs = [pl.BlockSpec((tr, w), functools.partial(lambda i, cb: (i, cb), cb=cb)) for (_, _, w, cb) in row_outs]
    out_specs += [pl.BlockSpec((1, 1, f), lambda i: (i // per_ex, 0, 0)) for f in ex_outs]
    out_specs += [pl.BlockSpec(tuple(s), functools.partial(lambda i, nd: (0,) * nd, nd=len(s))) for s in par_outs]
    out_shape = [jax.ShapeDtypeStruct((T, tw), dt) for (tw, dt, _, _) in row_outs]
    out_shape += [jax.ShapeDtypeStruct((nb, 1, f), F32) for f in ex_outs]
    out_shape += [jax.ShapeDtypeStruct(tuple(s), F32) for s in par_outs]
    res = pl.pallas_call(
        kern, name=name, grid=(T // tr,), in_specs=in_specs, out_specs=out_specs, out_shape=out_shape,
        input_output_aliases=aliases or {}, compiler_params=_params(("arbitrary",)),
    )(*[r[0] for r in rows], *exs, *pars)
    return res[:n_ro], res[n_ro:n_ro + n_eo], res[n_ro + n_eo:]


def _rms(x, w, n=None):
    n = x.shape[-1] if n is None else n
    ms = jnp.sum(x * x, axis=-1, keepdims=True) * (1.0 / n)
    return x * lax.rsqrt(ms + NORM_EPS) * w


def _gelu(x):
    return 0.5 * x * (1.0 + lax.erf(x * np.float32(1.0 / math.sqrt(2.0))))


def _silu(x):
    return x * jax.nn.sigmoid(x)


@jax.custom_vjp
def _rope(x, cos, sin):
    return x * cos + pltpu.roll(x, 64, 1) * sin


def _rope_fwd(x, cos, sin):
    return _rope(x, cos, sin), (cos, sin)


def _rope_bwd(res, dy):
    cos, sin = res
    return dy * cos + pltpu.roll(dy * sin, 64, 1), None, None


_rope.defvjp(_rope_fwd, _rope_bwd)


def _heads(x, n):
    return [x[:, h * LANES:(h + 1) * LANES] for h in range(n)]


@functools.partial(jax.custom_vjp, nondiff_argnums=(1,))
def _lroll(x, shift):
    return pltpu.roll(x, shift, 1)


def _lroll_fwd(x, shift):
    return _lroll(x, shift), None


def _lroll_bwd(shift, _, dy):
    return (pltpu.roll(dy, (LANES - shift) % LANES, 1),)


_lroll.defvjp(_lroll_fwd, _lroll_bwd)


def _spread_rotary(v):
    low = lax.broadcasted_iota(jnp.int32, (1, LANES), 1) < 64
    return [jnp.where(low, v, _lroll(v, 32)), jnp.where(low, _lroll(v, 64), _lroll(v, 96))]


def _spread_values(v):
    low = lax.broadcasted_iota(jnp.int32, (1, LANES), 1) < 64
    return [jnp.where(low, v, 0.0), jnp.where(low, _lroll(v, 64), 0.0)]


def _f_norm_mod(x, w, sc, sh):
    return _rms(x, w) * (1.0 + sc) + sh


def _f_resid_norm_mod(xa, delta, g, w, sc, sh):
    xn = xa + g * delta
    return xn, _f_norm_mod(xn, w, sc, sh)


def _f_mixprep(proj, ca, sa, cb, sb, qnw, kvnw):
    qa = [_rope(p, ca, sa) for pair in _heads(proj[:, P_AQ:P_AK], A_Q_HEADS // 2) for p in _spread_rotary(pair)]
    ka = [_rope(p, ca, sa) for p in _spread_rotary(proj[:, P_AK:P_AV])]
    va = _spread_values(proj[:, P_AV:P_CQ])
    cqn = _rms(proj[:, P_CQ:P_CKV], qnw)
    ckvn = _rms(proj[:, P_CKV:P_MIX], kvnw)
    kr = _rope(proj[:, P_KR:P_AQ], cb, sb)
    return jnp.concatenate(qa, -1), jnp.concatenate(ka, -1), jnp.concatenate(va, -1), cqn, ckvn, kr


def _f_mlaprep(q, kv, kr, cb, sb):
    qs = [_rope(p, cb, sb) for p in _heads(q, B_HEADS)]
    ks = [p + kr for p in _heads(kv[:, :B_HEADS * LANES], B_HEADS)]
    return jnp.concatenate(qs, -1), jnp.concatenate(ks, -1), kv[:, B_HEADS * LANES:]


def _f_sgu(cu, cv, ln_w, ln_b, w_s, b_col):
    u = _gelu(cu)
    v = _gelu(cv)
    mu = jnp.mean(v, axis=-1, keepdims=True)
    var = jnp.mean(jnp.square(v - mu), axis=-1, keepdims=True)
    vn = (v - mu) * lax.rsqrt(var + NORM_EPS) * ln_w + ln_b
    r = lax.broadcasted_iota(jnp.int32, (C_CHUNK, C_CHUNK), 0)
    c = lax.broadcasted_iota(jnp.int32, (C_CHUNK, C_CHUNK), 1)
    lane = lax.broadcasted_iota(jnp.int32, (1, LANES), 1)
    per_block = LANES // C_GROUP_DIM
    blocks = []
    for blk, vb in enumerate(_heads(vn, C_WIDTH // LANES)):
        mixed = jnp.zeros(vb.shape, F32)
        for j in range(per_block):
            g = blk * per_block + j
            gm = (lane // C_GROUP_DIM == j).astype(F32)
            wg = jnp.where(r >= c, w_s[g], 0.0).astype(BF16)
            mixed = mixed + jnp.dot(wg, (vb * gm).astype(BF16), preferred_element_type=F32) + b_col[g] * gm
        blocks.append(mixed)
    return u * jnp.concatenate(blocks, -1)


@jax.custom_vjp
def _pack_pairs(x):
    heads = _heads(x, x.shape[-1] // LANES)
    return jnp.concatenate([heads[i] + pltpu.roll(heads[i + 1], 64, 1) for i in range(0, len(heads), 2)], -1)


def _pack_pairs_fwd(x):
    return _pack_pairs(x), None


def _pack_pairs_bwd(_, dy):
    out = []
    for p in _heads(dy, dy.shape[-1] // LANES):
        out += [p, pltpu.roll(p, 64, 1)]
    return (jnp.concatenate(out, -1),)


_pack_pairs.defvjp(_pack_pairs_fwd, _pack_pairs_bwd)


def _f_outnorm(oa, ob, yc, gw):
    na, nb = A_Q_HEADS * HEAD_DIM, B_HEADS * B_V
    ya = _rms(_pack_pairs(oa), gw[:, :na])
    yb = _rms(_pack_pairs(ob), gw[:, na:na + nb])
    ycn = _rms(yc, gw[:, na + nb:])
    return jnp.concatenate([ya, yb, ycn], -1)


def _f_swiglu(gate, up):
    return _silu(gate) * up


def _mask(q_start, k_start, tq, tk, window):
    qpos = q_start + lax.broadcasted_iota(jnp.int32, (tq, tk), 0)
    kpos = k_start + lax.broadcasted_iota(jnp.int32, (tq, tk), 1)
    m = kpos <= qpos
    if window is not None:
        m = jnp.logical_and(m, qpos - kpos < window)
    return m


def _tile_fwd(qv, kk, vv, q_start, k_start, n_free, scale, window, m0, l0):
    tq = qv.shape[0]
    W = kk.shape[0]
    c = scale * LOG2_E
    parts = []
    if n_free > 0:
        parts.append((lax.dot_general(qv, kk[:n_free], _DIMS["nt"], preferred_element_type=F32), vv[:n_free]))
    if W > n_free:
        s = lax.dot_general(qv, kk[n_free:], _DIMS["nt"], preferred_element_type=F32)
        s = jnp.where(_mask(q_start, k_start + n_free, tq, W - n_free, window), s, NEG_INF)
        parts.append((s, vv[n_free:]))
    m = None if m0 is None else m0 * (1.0 / scale)
    for s, _ in parts:
        mx = jnp.max(s, axis=-1, keepdims=True)
        m = mx if m is None else jnp.maximum(m, mx)
    l = None if l0 is None else l0 * jnp.exp2((m0 * (1.0 / scale) - m) * c)
    o = None
    for s, vpart in parts:
        p = jnp.exp2((s - m) * c)
        ps = jnp.sum(p, axis=-1, keepdims=True)
        l = ps if l is None else l + ps
        po = jnp.dot(p.astype(BF16), vpart, preferred_element_type=F32)
        o = po if o is None else o + po
    return o / l, m * scale + jnp.log(l)


def _tile_bwd(qv, kk, vv, dof, ov, lse, q_start, k_start, n_free, scale, window):
    tq = qv.shape[0]
    W = kk.shape[0]
    dob = dof.astype(BF16)
    delta = jnp.sum(dof * ov, axis=-1, keepdims=True)
    c = scale * LOG2_E
    lse2 = lse * LOG2_E
    dq = None
    outs = []
    for (a, b, masked) in ((0, n_free, False), (n_free, W, True)):
        if b <= a:
            continue
        kp, vp = kk[a:b], vv[a:b]
        s = lax.dot_general(qv, kp, _DIMS["nt"], preferred_element_type=F32)
        if masked:
            s = jnp.where(_mask(q_start, k_start + a, tq, b - a, window), s, NEG_INF)
        p = jnp.exp2(s * c - lse2)
        dp = lax.dot_general(dob, vp, _DIMS["nt"], preferred_element_type=F32)
        ds = (p * ((dp - delta) * scale)).astype(BF16)
        d = jnp.dot(ds, kp, preferred_element_type=F32)
        dq = d if dq is None else dq + d
        dkp = lax.dot_general(ds, qv, _DIMS["tn"], preferred_element_type=F32)
        dvp = lax.dot_general(p.astype(BF16), dob, _DIMS["tn"], preferred_element_type=F32)
        outs.append((a, dkp, dvp))
    return dq, outs


def _attn_fwd(q, k, v, sinks, *, B, S, HQ, HK, window, scale, tq, band, name, xchg=None):
    G = HQ // HK
    nq = S // tq
    T = B * S
    has_sink = sinks is not None
    n_x = 0 if xchg is None else len(xchg[1])
    n_in = 4 if has_sink else 3

    def kern(*refs):
        xs, xd, sems = refs[n_in:n_in + n_x], refs[n_in + n_x + 2:n_in + 2 * n_x + 2], refs[n_in + 2 * n_x + 2:]
        refs = refs[:n_in] + refs[n_in + n_x:n_in + n_x + 2]
        if n_x:
            _xchg_at_ends(xchg[0], xs, xd, sems, (B, HQ, nq), True)
        if has_sink:
            q_ref, k_ref, v_ref, s_ref, o_ref, lse_ref = refs
        else:
            q_ref, k_ref, v_ref, o_ref, lse_ref = refs
        q_start = pl.program_id(2) * tq
        qv = q_ref[...]
        if has_sink:
            m0 = jnp.broadcast_to(s_ref[0][:, :1], (tq, 1))
            l0 = jnp.ones((tq, 1), F32)
        else:
            m0 = l0 = None

        def finish(o, lse):
            o_ref[...] = o.astype(o_ref.dtype)
            lse_ref[...] = jnp.broadcast_to(lse, (tq, LANES))

        if window is None:
            bidx = q_start // band
            for bb in range(S // band):
                @pl.when(bidx == bb)
                def _(bb=bb):
                    W = (bb + 1) * band
                    finish(*_tile_fwd(qv, k_ref[0:W, :], v_ref[0:W, :], q_start, 0, bb * band, scale, None, m0, l0))
        else:
            sub = min(tq, WINDOW_SUB)
            W = min(S, sub + window)
            for r in range(0, tq, sub):
                k_start = pl.multiple_of(jnp.maximum(q_start + r - window, 0), window)
                o, lse = _tile_fwd(qv[r:r + sub], k_ref[pl.ds(k_start, W), :], v_ref[pl.ds(k_start, W), :], q_start + r,
                                   k_start, 0, scale, window, None if m0 is None else m0[:sub], None if l0 is None else l0[:sub])
                o_ref[r:r + sub, :] = o.astype(o_ref.dtype)
                lse_ref[r:r + sub, :] = jnp.broadcast_to(lse, (sub, LANES))
        if n_x:
            _xchg_at_ends(xchg[0], xs, xd, sems, (B, HQ, nq), False)

    q_spec = pl.BlockSpec((tq, LANES), lambda b, h, i: (b * nq + i, h))
    kv_spec = pl.BlockSpec((S, LANES), lambda b, h, i: (b, h // G))
    in_specs = [q_spec, kv_spec, kv_spec]
    args = [q, k, v]
    if has_sink:
        in_specs.append(pl.BlockSpec((1, 1, LANES), lambda b, h, i: (h, 0, 0)))
        args.append(sinks)
    xs = [] if xchg is None else list(xchg[1])
    res = pl.pallas_call(
        kern, name=name, grid=(B, HQ, nq), in_specs=in_specs + [_ANY] * n_x, out_specs=[q_spec, q_spec] + [_ANY] * n_x,
        out_shape=[jax.ShapeDtypeStruct((T, HQ * LANES), BF16), jax.ShapeDtypeStruct((T, HQ * LANES), F32)]
        + (_xchg_out_shapes(xchg[0], xs) if n_x else []),
        scratch_shapes=_xchg_scratch(xchg[0], n_x) if n_x else [],
        compiler_params=_params(("arbitrary",) * 3 if n_x else ("parallel", "parallel", "arbitrary")),
    )(*args, *xs)
    return ((res[0], res[1]), list(res[2:])) if n_x else res


def _attn_bwd(q, k, v, o, lse, do, sinks, *, B, S, HQ, HK, window, scale, tq, band, name, xchg=None):
    G = HQ // HK
    nq = S // tq
    T = B * S
    has_sink = sinks is not None
    n_x = 0 if xchg is None else len(xchg[1])
    n_in, n_out = (7, 4) if has_sink else (6, 3)

    def kern(*refs):
        xs, xd = refs[n_in:n_in + n_x], refs[n_in + n_x + n_out:n_in + 2 * n_x + n_out]
        sems = refs[n_in + 2 * n_x + n_out:]
        refs = refs[:n_in] + refs[n_in + n_x:n_in + n_x + n_out]
        if n_x:
            _xchg_at_ends(xchg[0], xs, xd, sems, (B, HK, G, nq), True)
        if has_sink:
            q_ref, k_ref, v_ref, o_ref, lse_ref, do_ref, s_ref, dq_ref, dk_ref, dv_ref, ds_ref = refs
        else:
            q_ref, k_ref, v_ref, o_ref, lse_ref, do_ref, dq_ref, dk_ref, dv_ref = refs
        gi = pl.program_id(2)
        qi = pl.program_id(3)
        q_start = qi * tq

        @pl.when(jnp.logical_and(gi == 0, qi == 0))
        def _():
            dk_ref[...] = jnp.zeros_like(dk_ref)
            dv_ref[...] = jnp.zeros_like(dv_ref)

        qv = q_ref[...]
        dof = do_ref[...].astype(F32)
        ov = o_ref[...].astype(F32)
        lse_v = lse_ref[...][:, :1]
        if window is None:
            bidx = q_start // band
            for bb in range(S // band):
                @pl.when(bidx == bb)
                def _(bb=bb):
                    W = (bb + 1) * band
                    dq, outs = _tile_bwd(qv, k_ref[0:W, :], v_ref[0:W, :], dof, ov, lse_v, q_start, 0, bb * band, scale, None)
                    dq_ref[...] = dq.astype(dq_ref.dtype)
                    for a, dkp, dvp in outs:
                        dk_ref[a:a + dkp.shape[0], :] += dkp
                        dv_ref[a:a + dvp.shape[0], :] += dvp
        else:
            sub = min(tq, WINDOW_SUB)
            W = min(S, sub + window)
            parts = []
            for r in range(0, tq, sub):
                k_start = pl.multiple_of(jnp.maximum(q_start + r - window, 0), window)
                dq, outs = _tile_bwd(qv[r:r + sub], k_ref[pl.ds(k_start, W), :], v_ref[pl.ds(k_start, W), :],
                                     dof[r:r + sub], ov[r:r + sub], lse_v[r:r + sub], q_start + r, k_start, 0, scale, window)
                dq_ref[r:r + sub, :] = dq.astype(dq_ref.dtype)
                parts.append((k_start, outs[0][1], outs[0][2]))
            for k_start, dkp, dvp in parts:
                dk_ref[pl.ds(k_start, W), :] += dkp
                dv_ref[pl.ds(k_start, W), :] += dvp
        if has_sink:
            delta = jnp.sum(dof * ov, axis=-1, keepdims=True)
            sink = s_ref[0][:, :1]
            part = -jnp.sum(jnp.exp(sink - lse_v) * delta, axis=0, keepdims=True)
            part = jnp.broadcast_to(part, (1, LANES))

            @pl.when(qi == 0)
            def _():
                ds_ref[0] = part

            @pl.when(qi != 0)
            def _():
                ds_ref[0] += part
        if n_x:
            _xchg_at_ends(xchg[0], xs, xd, sems, (B, HK, G, nq), False)

    q_spec = pl.BlockSpec((tq, LANES), lambda b, hk, g, i: (b * nq + i, hk * G + g))
    kv_spec = pl.BlockSpec((S, LANES), lambda b, hk, g, i: (b, hk))
    in_specs = [q_spec, kv_spec, kv_spec, q_spec, q_spec, q_spec]
    args = [q, k, v, o, lse, do]
    out_specs = [q_spec, kv_spec, kv_spec]
    out_shape = [jax.ShapeDtypeStruct((T, HQ * LANES), BF16), jax.ShapeDtypeStruct((T, HK * LANES), F32),
                 jax.ShapeDtypeStruct((T, HK * LANES), F32)]
    if has_sink:
        in_specs.append(pl.BlockSpec((1, 1, LANES), lambda b, hk, g, i: (hk * G + g, 0, 0)))
        args.append(sinks)
        out_specs.append(pl.BlockSpec((1, 1, LANES), lambda b, hk, g, i: (b * HQ + hk * G + g, 0, 0)))
        out_shape.append(jax.ShapeDtypeStruct((B * HQ, 1, LANES), F32))
    xs = [] if xchg is None else list(xchg[1])
    res = pl.pallas_call(
        kern, name=name, grid=(B, HK, G, nq), in_specs=in_specs + [_ANY] * n_x, out_specs=out_specs + [_ANY] * n_x,
        out_shape=out_shape + (_xchg_out_shapes(xchg[0], xs) if n_x else []),
        scratch_shapes=_xchg_scratch(xchg[0], n_x) if n_x else [],
        compiler_params=_params(("arbitrary",) * 4 if n_x else ("parallel", "parallel", "arbitrary", "arbitrary")),
    )(*args, *xs)
    main = tuple(res[:n_out]) if has_sink else (*res[:n_out], None)
    return (main, list(res[n_out:])) if n_x else main


def _rope_tables(pos_col, freq, sign, keep, name):
    T = pos_col.shape[0]
    tr = _tile(T, 1024)

    def kern(p_ref, f_ref, s_ref, k_ref, c_out, s_out):
        ang = p_ref[...] * f_ref[...]
        c_out[...] = jnp.cos(ang) * k_ref[...]
        s_out[...] = jnp.sin(ang) * s_ref[...]

    spec = pl.BlockSpec((tr, LANES), lambda i: (i, 0))
    par = pl.BlockSpec((1, LANES), lambda i: (0, 0))
    return pl.pallas_call(
        kern, name=name, grid=(T // tr,), in_specs=[pl.BlockSpec((tr, 1), lambda i: (i, 0)), par, par, par],
        out_specs=[spec, spec], out_shape=[jax.ShapeDtypeStruct((T, LANES), F32)] * 2,
        compiler_params=_params(("parallel",)),
    )(pos_col, freq, sign, keep)


def _ada_fwd(c8, ada_w, ada_b):
    L, D, N = ada_w.shape

    def kern(c_ref, w_ref, b_ref, o_ref):
        act = _silu(c_ref[...]).astype(BF16)
        o_ref[0] = jnp.dot(act, w_ref[0], preferred_element_type=F32) + b_ref[0]

    return pl.pallas_call(
        kern, name="ada_fwd", grid=(L,),
        in_specs=[pl.BlockSpec((8, D), lambda l: (0, 0)), pl.BlockSpec((1, D, N), lambda l: (l, 0, 0)),
                  pl.BlockSpec((1, 1, N), lambda l: (l, 0, 0))],
        out_specs=pl.BlockSpec((1, 8, N), lambda l: (l, 0, 0)),
        out_shape=jax.ShapeDtypeStruct((L, 8, N), F32), compiler_params=_params(("parallel",)),
    )(c8, ada_w, ada_b)


def _ada_bwd(c8, dmod):
    L, _, N = dmod.shape
    D = c8.shape[1]

    def kern(c_ref, d_ref, gw_ref):
        act = _silu(c_ref[...]).astype(BF16)
        gw_ref[0] = lax.dot_general(act, d_ref[0].astype(BF16), _DIMS["tn"], preferred_element_type=F32)

    return pl.pallas_call(
        kern, name="ada_bwd", grid=(L,),
        in_specs=[pl.BlockSpec((8, D), lambda l: (0, 0)), pl.BlockSpec((1, 8, N), lambda l: (l, 0, 0))],
        out_specs=pl.BlockSpec((1, D, N), lambda l: (l, 0, 0)),
        out_shape=jax.ShapeDtypeStruct((L, D, N), F32), compiler_params=_params(("parallel",)),
    )(c8, dmod)


def _sum_examples(d):
    L, _, N = d.shape

    def kern(d_ref, o_ref):
        o_ref[...] = jnp.sum(d_ref[...], axis=1, keepdims=True)

    vm = pl.BlockSpec(memory_space=pltpu.VMEM)
    return pl.pallas_call(kern, name="sum_examples", in_specs=[vm], out_specs=vm,
                          out_shape=jax.ShapeDtypeStruct((L, 1, N), F32))(d)


def _adamw(w, ga, gb, m, v, name, xchg=None):
    rows, cols = w.shape
    tr = _row_div(rows, 256)
    two = gb is not None
    n_x = 0 if xchg is None else len(xchg[1])
    n_in = 5 if two else 4

    def kern(*refs):
        xs, xd, sems = refs[n_in:n_in + n_x], refs[n_in + n_x + 4:n_in + 2 * n_x + 4], refs[n_in + 2 * n_x + 4:]
        refs = refs[:n_in] + refs[n_in + n_x:n_in + n_x + 4]
        if n_x:
            _xchg_at_ends(xchg[0], xs, xd, sems, (rows // tr,), True)
        if two:
            w_ref, ga_ref, gb_ref, m_ref, v_ref, g_out, d_out, m_out, v_out = refs
            gv = ga_ref[...] + gb_ref[...]
        else:
            w_ref, ga_ref, m_ref, v_ref, g_out, d_out, m_out, v_out = refs
            gv = ga_ref[...]
        mn = ADAM_B1 * m_ref[...] + (1.0 - ADAM_B1) * gv
        vn = ADAM_B2 * v_ref[...] + (1.0 - ADAM_B2) * jnp.square(gv)
        m_hat = mn / (1.0 - ADAM_B1 ** ADAM_STEP)
        v_hat = vn / (1.0 - ADAM_B2 ** ADAM_STEP)
        g_out[...] = gv
        d_out[...] = -ADAM_LR * (m_hat / (jnp.sqrt(v_hat) + ADAM_EPS) + ADAM_WD * w_ref[...])
        m_out[...] = mn
        v_out[...] = vn
        if n_x:
            _xchg_at_ends(xchg[0], xs, xd, sems, (rows // tr,), False)

    spec = pl.BlockSpec((tr, cols), lambda i: (i, 0))
    args = [w, ga, gb, m, v] if two else [w, ga, m, v]
    xs = [] if xchg is None else list(xchg[1])
    res = pl.pallas_call(
        kern, name=name, grid=(rows // tr,), in_specs=[spec] * len(args) + [_ANY] * n_x,
        out_specs=[spec] * 4 + [_ANY] * n_x,
        out_shape=[jax.ShapeDtypeStruct((rows, cols), F32)] * 4 + (_xchg_out_shapes(xchg[0], xs) if n_x else []),
        scratch_shapes=_xchg_scratch(xchg[0], n_x) if n_x else [],
        compiler_params=_params(("arbitrary",) if n_x else ("parallel",)),
    )(*args, *xs)
    return (tuple(res[:4]), list(res[4:])) if n_x else res


def _sum_slots(x, name):
    n, rows, cols = x.shape
    tr = _row_div(rows, 256)

    def kern(x_ref, o_ref):
        acc = x_ref[0].astype(F32)
        for j in range(1, n):
            acc = acc + x_ref[j].astype(F32)
        o_ref[...] = acc

    return pl.pallas_call(
        kern, name=name, grid=(rows // tr,), in_specs=[pl.BlockSpec((n, tr, cols), lambda i: (0, i, 0))],
        out_specs=pl.BlockSpec((tr, cols), lambda i: (i, 0)), out_shape=jax.ShapeDtypeStruct((rows, cols), F32),
        compiler_params=_params(("parallel",)),
    )(x)


def _all_reduce_small(blob, name):
    R, C = blob.shape

    def kern(src, out, pair, chips, send_sems, recv_sems):
        x, y, c = lax.axis_index("x"), lax.axis_index("y"), lax.axis_index("c")
        me = 2 * x + y
        to_sibling = pltpu.make_async_remote_copy(
            src_ref=src, dst_ref=pair, send_sem=send_sems.at[0], recv_sem=recv_sems.at[0],
            device_id=(x, y, 1 - c), device_id_type=_MESH)
        to_sibling.start()
        to_sibling.wait()
        chips[me] = src[...] + pair[...]
        peers = [(1 - x, y), (x, 1 - y), (1 - x, 1 - y)]
        copies = [pltpu.make_async_remote_copy(
            src_ref=chips.at[me], dst_ref=chips.at[me], send_sem=send_sems.at[1 + kk], recv_sem=recv_sems.at[1 + kk],
            device_id=(px, py, c), device_id_type=_MESH) for kk, (px, py) in enumerate(peers)]
        for cp in copies:
            cp.start()
        for kk, (px, py) in enumerate(peers):
            pltpu.make_async_remote_copy(
                src_ref=chips.at[me], dst_ref=chips.at[2 * px + py], send_sem=send_sems.at[1 + kk],
                recv_sem=recv_sems.at[1 + kk], device_id=(px, py, c), device_id_type=_MESH).wait_recv()
        for cp in copies:
            cp.wait_send()
        acc = chips[0]
        for j in range(1, N_CHIPS):
            acc = acc + chips[j]
        out[...] = acc

    vm = pl.BlockSpec(memory_space=pltpu.VMEM)
    return pl.pallas_call(
        kern, name=name, in_specs=[vm], out_specs=vm, out_shape=jax.ShapeDtypeStruct((R, C), F32),
        scratch_shapes=[pltpu.VMEM((R, C), F32), pltpu.VMEM((N_CHIPS, R, C), F32), pltpu.SemaphoreType.DMA((4,)),
                        pltpu.SemaphoreType.DMA((4,))],
        compiler_params=pltpu.CompilerParams(vmem_limit_bytes=VMEM_LIMIT),
    )(blob)


def _row_tile(S, wide=False):
    for t in ((256,) if wide else (512, 256)):
        if S % t == 0:
            return t
    return 128


def _attn_tiles(S):
    return min(S, 512), min(S, 256), min(S, 512)


def _hosted(fn, host, got, kind, name, *args, **kw):
    if host is not None and host.get(name):
        res, xs = fn(*args, name=name, xchg=(kind, host[name]), **kw)
        got[name] = xs
        return res
    return fn(*args, name=name, **kw)


def _mm(host, got, kind, a, b, mode, out_dtype, name, **kw):
    return _hosted(_matmul, host, got, kind, name, a, b, mode, out_dtype, **kw)


def _merge_late(w, late, got, name):
    if not late or name not in late:
        return w
    return {**w, **_large_operands(dict(zip(late[name], got[name])))}


def _sgu_rows(S):
    return 4 * C_CHUNK if S % (4 * C_CHUNK) == 0 else C_CHUNK


def _layer_fwd(xin, prev, mods, w, tabs, B, S, host=None, late=None):
    T = B * S
    tr = _row_tile(S)
    sh1, sc1, g1, sh2, sc2, g2 = mods
    ca, sa, cb, sb = tabs
    sv = {}
    got = {}
    if prev is None:
        def body(rv, ev, pv):
            return [_f_norm_mod(rv[0], pv[0], ev[0], ev[1])], [], []
        (h,), _, _ = _rowcall(body, name="f_norm_mod", T=T, S=S, tr=tr, rows=[(xin, D_MODEL, 0)], exs=[sc1, sh1],
                              pars=[w["norm1_w"]], row_outs=[(D_MODEL, BF16, D_MODEL, 0)])
        x = xin
    else:
        x1p, dp, g2p = prev

        def body(rv, ev, pv):
            xn, hh = _f_resid_norm_mod(rv[0], rv[1], ev[0], pv[0], ev[1], ev[2])
            return [xn, hh], [], []
        (x, h), _, _ = _rowcall(body, name="f_resid_norm_mod1", T=T, S=S, tr=tr,
                                rows=[(x1p, D_MODEL, 0), (dp, D_MODEL, 0)], exs=[g2p, sc1, sh1], pars=[w["norm1_w"]],
                                row_outs=[(D_MODEL, F32, D_MODEL, 0), (D_MODEL, BF16, D_MODEL, 0)])
    sv["x"], sv["h"] = x, h
    proj = _mm(host, got, "gather", h, w["w_in"], "nn", BF16, "mm_in")
    sv["proj"] = proj
    w = _merge_late(w, late, got, "mm_in")

    def body(rv, ev, pv):
        outs = _f_mixprep(rv[0], rv[1], rv[2], rv[3], rv[4], pv[0], pv[1])
        return list(outs), [], []
    (qa, ka, va, cqn, ckvn, kr), _, _ = _rowcall(
        body, name="f_mixprep", T=T, S=S, tr=tr,
        rows=[(proj, P_MIX, 0), (ca, LANES, 0), (sa, LANES, 0), (cb, LANES, 0), (sb, LANES, 0)],
        pars=[w["b_q_norm_w"], w["b_kv_norm_w"]],
        row_outs=[(768, BF16, 768, 0), (256, BF16, 256, 0), (256, BF16, 256, 0), (384, BF16, 384, 0),
                  (256, BF16, 256, 0), (LANES, F32, LANES, 0)])
    sv.update(qa=qa, ka=ka, va=va, cqn=cqn, ckvn=ckvn)
    q = _matmul(cqn, w["b_w_uq"], "nn", BF16, "mm_uq")
    kv = _matmul(ckvn, w["b_w_ukv"], "nn", BF16, "mm_ukv")

    def body(rv, ev, pv):
        Q, K, V = _f_mlaprep(rv[0], rv[1], rv[2], rv[3], rv[4])
        return [Q, K, V], [], []
    (Q, K, V), _, _ = _rowcall(
        body, name="f_mlaprep", T=T, S=S, tr=tr,
        rows=[(q, 768, 0), (kv, 1536, 0), (kr, LANES, 0), (cb, LANES, 0), (sb, LANES, 0)],
        row_outs=[(768, BF16, 768, 0)] * 3)
    sv.update(Q=Q, K=K, V=V)
    ta, tb, band = _attn_tiles(S)
    oa, lse_a = _hosted(_attn_fwd, host, got, "gather", "attn_a_fwd", qa, ka, va, w["sinks"], B=B, S=S, HQ=A_Q_HEADS,
                        HK=A_KV_HEADS, window=A_WINDOW, scale=HEAD_DIM ** -0.5, tq=ta, band=None)
    ob, lse_b = _hosted(_attn_fwd, host, got, "gather", "attn_b_fwd", Q, K, V, None, B=B, S=S, HQ=B_HEADS, HK=B_HEADS,
                        window=None, scale=(B_NOPE + B_ROPE) ** -0.5, tq=tb, band=band)
    sv.update(oa=oa, lse_a=lse_a, ob=ob, lse_b=lse_b)
    w = _merge_late(w, late, got, "attn_a_fwd")
    ts = _sgu_rows(S)

    def body(rv, ev, pv):
        outs = [_f_sgu(rv[0][r:r + C_CHUNK], rv[1][r:r + C_CHUNK], pv[0], pv[1], pv[2], pv[3])
                for r in range(0, ts, C_CHUNK)]
        return [jnp.concatenate(outs, axis=0)], [], []
    (yc,), _, _ = _rowcall(body, name="f_sgu", T=T, S=S, tr=ts,
                           rows=[(proj, C_WIDTH, P_CU // C_WIDTH), (proj, C_WIDTH, P_CV // C_WIDTH)],
                           pars=[w["c_ln_w"], w["c_ln_b"], w["c_w_s"], w["c_b_col"]],
                           row_outs=[(C_WIDTH, F32, C_WIDTH, 0)])
    sv["yc"] = yc

    def body(rv, ev, pv):
        return [_f_outnorm(rv[0], rv[1], rv[2], pv[0])], [], []
    (y,), _, _ = _rowcall(body, name="f_outnorm", T=T, S=S, tr=tr,
                          rows=[(oa, 768, 0), (ob, 768, 0), (yc, C_WIDTH, 0)], pars=[w["out_norm_w"]],
                          row_outs=[(D_MODEL, BF16, D_MODEL, 0)])
    sv["y"] = y
    o = _mm(host, got, "gather", y, w["w_out"], "nn", BF16, "mm_out")
    sv["o"] = o

    def body(rv, ev, pv):
        xn, hh = _f_resid_norm_mod(rv[0], rv[1], ev[0], pv[0], ev[1], ev[2])
        return [xn, hh], [], []
    (x1, h2), _, _ = _rowcall(body, name="f_resid_norm_mod2", T=T, S=S, tr=tr,
                              rows=[(x, D_MODEL, 0), (o, D_MODEL, 0)], exs=[g1, sc2, sh2], pars=[w["norm2_w"]],
                              row_outs=[(D_MODEL, F32, D_MODEL, 0), (D_MODEL, BF16, D_MODEL, 0)])
    sv["h2"] = h2
    gu = _mm(host, got, "gather", h2, w["w_gate_up"], "nn", BF16, "mm_gate_up", b_chunks=True)
    sv["gu"] = gu

    def body(rv, ev, pv):
        return [_f_swiglu(rv[0], rv[1])], [], []
    (act,), _, _ = _rowcall(body, name="f_swiglu", T=T, S=S, tr=_row_tile(S, wide=True),
                            rows=[(gu, FFN_HIDDEN, 0), (gu, FFN_HIDDEN, 1)], row_outs=[(FFN_HIDDEN, BF16, FFN_HIDDEN, 0)])
    sv["act"] = act
    d = _mm(host, got, "gather", act, w["w_down"], "nn", BF16, "mm_down")
    sv["x1"], sv["d"], sv["w"] = x1, d, w
    return (x1, d, g2), sv, got


def _final(x1, d, g2, fw, target, B, S):
    T = B * S
    tr = _row_tile(S)

    def loss_fn(x1v, dv, gv, wv, tv):
        yf = _rms(x1v + gv * dv, wv)
        return 0.5 * jnp.sum(jnp.mean(jnp.square(yf - tv), axis=-1))

    def body(rv, ev, pv):
        x1v, dv, tv = rv
        val, vjp = jax.vjp(lambda a, b_, g, ww: loss_fn(a, b_, g, ww, tv), x1v, dv, ev[0], pv[0])
        dx1, dd, dg, dw = vjp(jnp.ones((), F32))
        return [dx1, dd], [dg], [dw, jnp.full((1, LANES), val, F32)]
    (dx1, dd), (dg2,), (dfw, loss) = _rowcall(
        body, name="final_loss", T=T, S=S, tr=tr, rows=[(x1, D_MODEL, 0), (d, D_MODEL, 0), (target, D_MODEL, 0)],
        exs=[g2], pars=[fw], row_outs=[(D_MODEL, F32, D_MODEL, 0), (D_MODEL, BF16, D_MODEL, 0)],
        ex_outs=[D_MODEL], par_outs=[(1, D_MODEL), (1, LANES)])
    return loss, dx1, dd, dg2, dfw


def _layer_bwd(sv, prev, mods, tabs, dx1, dd, B, S, host=None, own_early=False):
    w = sv["w"]
    T = B * S
    tr = _row_tile(S)
    sh1, sc1, g1, sh2, sc2, g2 = mods
    ca, sa, cb, sb = tabs
    gr = {}
    got = {}
    dact = _mm(host, got, "a2a", dd, w["w_down"], "nt", BF16, "mm_down_dx")
    gr["w_down"] = _matmul(sv["act"], dd, "tn", BF16, "mm_down_dw")

    def body(rv, ev, pv):
        _, vjp = jax.vjp(_f_swiglu, rv[0], rv[1])
        dgate, dup = vjp(rv[2])
        return [[dgate, dup]], [], []
    (dgu,), _, _ = _rowcall(body, name="b_swiglu", T=T, S=S, tr=_row_tile(S, wide=True),
                            rows=[(sv["gu"], FFN_HIDDEN, 0), (sv["gu"], FFN_HIDDEN, 1), (dact, FFN_HIDDEN, 0)],
                            row_outs=[(2 * FFN_HIDDEN, BF16, 2 * FFN_HIDDEN, 0)])
    dh2 = _mm(host, got, "a2a", dgu, w["w_gate_up"], "nt", BF16, "mm_gate_up_dx", b_chunks=True)
    gr["w_gate_up"] = _mm(host, got, "a2a", sv["h2"], dgu, "tn", BF16, "mm_gate_up_dw", out_chunks=True)

    def body(rv, ev, pv):
        xa, delta, dh, dxn = rv
        _, vjp = jax.vjp(_f_resid_norm_mod, xa, delta, ev[0], pv[0], ev[1], ev[2])
        dxa, ddelta, dg, dw, dsc, dsh = vjp((dxn, dh))
        return [dxa, ddelta], [dg, dsc, dsh], [dw]
    (dx, do), (dg1, dsc2, dsh2), (gr["norm2_w"],) = _rowcall(
        body, name="b_resid_norm_mod2", T=T, S=S, tr=tr,
        rows=[(sv["x"], D_MODEL, 0), (sv["o"], D_MODEL, 0), (dh2, D_MODEL, 0), (dx1, D_MODEL, 0)],
        exs=[g1, sc2, sh2], pars=[w["norm2_w"]],
        row_outs=[(D_MODEL, F32, D_MODEL, 0), (D_MODEL, BF16, D_MODEL, 0)], ex_outs=[D_MODEL] * 3,
        par_outs=[(1, D_MODEL)])
    dy = _matmul(do, w["w_out"], "nt", BF16, "mm_out_dx")
    gr["w_out"] = _matmul(sv["y"], do, "tn", BF16, "mm_out_dw")

    def body(rv, ev, pv):
        _, vjp = jax.vjp(_f_outnorm, rv[0], rv[1], rv[2], pv[0])
        doa, dob, dyc, dgw = vjp(rv[3])
        return [doa, dob, dyc], [], [dgw]
    (doa, dob, dyc), _, (gr["out_norm_w"],) = _rowcall(
        body, name="b_outnorm", T=T, S=S, tr=tr,
        rows=[(sv["oa"], 768, 0), (sv["ob"], 768, 0), (sv["yc"], C_WIDTH, 0), (dy, D_MODEL, 0)], pars=[w["out_norm_w"]],
        row_outs=[(768, BF16, 768, 0), (768, BF16, 768, 0), (C_WIDTH, F32, C_WIDTH, 0)], par_outs=[(1, D_MODEL)])

    ta, tb, band = _attn_tiles(S)
    if own_early:
        host = dict(host)
        host["attn_b_bwd"] = list(host.get("attn_b_bwd", ())) + [
            gr["w_gate_up"], gr["w_down"].reshape(N_CHIPS, FFN_HIDDEN // N_CHIPS, D_MODEL)]
    dQ, dK, dV, _ = _hosted(_attn_bwd, host, got, "a2a", "attn_b_bwd", sv["Q"], sv["K"], sv["V"], sv["ob"], sv["lse_b"],
                            dob, None, B=B, S=S, HQ=B_HEADS, HK=B_HEADS, window=None,
                            scale=(B_NOPE + B_ROPE) ** -0.5, tq=tb, band=band)
    dqa, dka, dva, dsink = _hosted(_attn_bwd, host, got, "a2a", "attn_a_bwd", sv["qa"], sv["ka"], sv["va"], sv["oa"],
                                   sv["lse_a"], doa, w["sinks"], B=B, S=S, HQ=A_Q_HEADS, HK=A_KV_HEADS,
                                   window=A_WINDOW, scale=HEAD_DIM ** -0.5, tq=ta, band=None)
    gr["sinks"] = dsink

    def body(rv, ev, pv):
        dQv, dKv, dVv, cbv, sbv = rv
        dq = [_rope_bwd((cbv, sbv), p)[0] for p in _heads(dQv, B_HEADS)]
        dkr = None
        for p in _heads(dKv, B_HEADS):
            dkr = p if dkr is None else dkr + p
        return [dq, [dKv, dVv], dkr], [], []
    (dq, dkv, dkr), _, _ = _rowcall(
        body, name="b_mlaprep", T=T, S=S, tr=tr,
        rows=[(dQ, 768, 0), (dK, 768, 0), (dV, 768, 0), (cb, LANES, 0), (sb, LANES, 0)],
        row_outs=[(768, BF16, 768, 0), (1536, BF16, 1536, 0), (LANES, F32, LANES, 0)])
    dcqn = _matmul(dq, w["b_w_uq"], "nt", BF16, "mm_uq_dx")
    gr["b_w_uq"] = _matmul(sv["cqn"], dq, "tn", BF16, "mm_uq_dw")
    dckvn = _matmul(dkv, w["b_w_ukv"], "nt", BF16, "mm_ukv_dx")
    gr["b_w_ukv"] = _matmul(sv["ckvn"], dkv, "tn", BF16, "mm_ukv_dw")

    def body(rv, ev, pv):
        proj, cav, sav, cbv, sbv, dqa_, dka_, dva_, dcqn_, dckvn_, dkr_ = rv
        _, vjp = jax.vjp(lambda p, a, b_: _f_mixprep(p, cav, sav, cbv, sbv, a, b_), proj, pv[0], pv[1])
        dproj, dqn, dkvn = vjp((dqa_, dka_, dva_, dcqn_, dckvn_, dkr_))
        return [[dproj, jnp.zeros((dproj.shape[0], P_CU - P_MIX), F32)]], [], [dqn, dkvn]
    (dproj,), _, (gr["b_q_norm_w"], gr["b_kv_norm_w"]) = _rowcall(
        body, name="b_mixprep", T=T, S=S, tr=tr,
        rows=[(sv["proj"], P_MIX, 0), (ca, LANES, 0), (sa, LANES, 0), (cb, LANES, 0), (sb, LANES, 0),
              (dqa, 768, 0), (dka, 256, 0), (dva, 256, 0), (dcqn, 384, 0), (dckvn, 256, 0), (dkr, LANES, 0)],
        pars=[w["b_q_norm_w"], w["b_kv_norm_w"]], row_outs=[(P_END, BF16, P_CU, 0)],
        par_outs=[(1, B_Q_RANK), (1, B_KV_RANK)])

    ts = _sgu_rows(S)

    def body(rv, ev, pv):
        cu, cv, dycv, _ = rv
        dcus, dcvs, acc = [], [], None
        for r in range(0, ts, C_CHUNK):
            _, vjp = jax.vjp(_f_sgu, cu[r:r + C_CHUNK], cv[r:r + C_CHUNK], pv[0], pv[1], pv[2], pv[3])
            dcu, dcv, *dpar = vjp(dycv[r:r + C_CHUNK])
            dcus.append(dcu)
            dcvs.append(dcv)
            acc = dpar if acc is None else [a + b_ for a, b_ in zip(acc, dpar)]
        return [[jnp.concatenate(dcus, axis=0), jnp.concatenate(dcvs, axis=0)]], [], acc
    (dproj,), _, (gr["c_ln_w"], gr["c_ln_b"], gr["c_w_s"], gr["c_b_col"]) = _rowcall(
        body, name="b_sgu", T=T, S=S, tr=ts,
        rows=[(sv["proj"], C_WIDTH, P_CU // C_WIDTH), (sv["proj"], C_WIDTH, P_CV // C_WIDTH), (dyc, C_WIDTH, 0),
              (dproj, 2 * C_WIDTH, P_CU // (2 * C_WIDTH))],
        pars=[w["c_ln_w"], w["c_ln_b"], w["c_w_s"], w["c_b_col"]],
        row_outs=[(P_END, BF16, 2 * C_WIDTH, P_CU // (2 * C_WIDTH))],
        par_outs=[(1, C_WIDTH), (1, C_WIDTH), (C_GROUPS, C_CHUNK, C_CHUNK), (C_GROUPS, C_CHUNK, 1)],
        aliases={3: 0})
    dh = _mm(host, got, "a2a", dproj, w["w_in"], "nt", BF16, "mm_in_dx")
    gr["w_in"] = _matmul(sv["h"], dproj, "tn", BF16, "mm_in_dw")

    if prev is None:
        def body(rv, ev, pv):
            xv, dhv, dxd = rv
            _, vjp = jax.vjp(_f_norm_mod, xv, pv[0], ev[0], ev[1])
            dxa, dw, dsc, dsh = vjp(dhv)
            return [dxa + dxd], [dsc, dsh], [dw]
        (dxin,), (dsc1, dsh1), (gr["norm1_w"],) = _rowcall(
            body, name="b_norm_mod", T=T, S=S, tr=tr, rows=[(sv["x"], D_MODEL, 0), (dh, D_MODEL, 0), (dx, D_MODEL, 0)],
            exs=[sc1, sh1], pars=[w["norm1_w"]], row_outs=[(D_MODEL, F32, D_MODEL, 0)], ex_outs=[D_MODEL] * 2,
            par_outs=[(1, D_MODEL)])
        nxt = (dxin, None, None)
    else:
        x1p, dp, g2p = prev

        def body(rv, ev, pv):
            xa, delta, dhv, dxn = rv
            _, vjp = jax.vjp(_f_resid_norm_mod, xa, delta, ev[0], pv[0], ev[1], ev[2])
            dxa, ddelta, dg, dw, dsc, dsh = vjp((dxn, dhv))
            return [dxa, ddelta], [dg, dsc, dsh], [dw]
        (dx1p, ddp), (dg2p, dsc1, dsh1), (gr["norm1_w"],) = _rowcall(
            body, name="b_resid_norm_mod1", T=T, S=S, tr=tr,
            rows=[(x1p, D_MODEL, 0), (dp, D_MODEL, 0), (dh, D_MODEL, 0), (dx, D_MODEL, 0)],
            exs=[g2p, sc1, sh1], pars=[w["norm1_w"]],
            row_outs=[(D_MODEL, F32, D_MODEL, 0), (D_MODEL, BF16, D_MODEL, 0)], ex_outs=[D_MODEL] * 3,
            par_outs=[(1, D_MODEL)])
        nxt = (dx1p, ddp, dg2p)
    return gr, (dsh1, dsc1, dg1, dsh2, dsc2), nxt, got


def _lane_table(lanes_neg, lanes_pos, inv):
    freq = np.zeros((LANES,), np.int64) - 1
    sign = np.zeros((1, LANES), np.float32)
    n = len(lanes_neg)
    freq[lanes_neg] = np.arange(n)
    freq[lanes_pos] = np.arange(n)
    sign[0, lanes_neg] = -1.0
    sign[0, lanes_pos] = 1.0
    return _select_axis(inv, freq, 0).reshape(1, LANES), jnp.asarray(sign)


SHARDED = ("ada_w", "w_in", "b_w_uq", "b_w_ukv", "w_out", "w_gate_up", "w_down")
ROW_SHARDED = ("w_out", "w_down")
SMALL = ("ada_b", "norm1_w", "a_sinks", "b_q_norm_w", "b_kv_norm_w", "c_ln_w", "c_ln_b", "c_w_s", "c_b_s",
         "out_norm_w", "norm2_w", "final_norm_w")
FWD_HOST = {"attn_b_fwd": ("w_gate_up", "w_down"), "attn_a_fwd": ("w_in", "w_out", "b_w_uq", "b_w_ukv")}
FWD_HOST_FIRST = {"attn_b_fwd": ("w_gate_up", "w_down"), "mm_gate_up": ("w_in", "w_out", "b_w_uq", "b_w_ukv")}
BWD_HOST = {"attn_a_bwd": ("w_in", "w_out", "b_w_uq", "b_w_ukv"), "attn_b_bwd": ("w_gate_up", "w_down")}
FIRST_LATE = {"mm_in": ("b_w_uq", "b_w_ukv", "w_out"), "attn_a_fwd": ("w_gate_up", "w_down")}
EXPOSED = ("w_in", "b_w_uq", "b_w_ukv", "w_out")
OWN_LAYER = ("w_gate_up", "w_down")
EXCHANGED = SHARDED[1:]
SWAP_ORDER = (("ada_w",), ("w_gate_up", "w_down"), ("w_in", "w_out", "b_w_uq", "b_w_ukv"))
N_ADA = N_MOD * D_MODEL // N_CHIPS
assert not set(FIRST_LATE) & set(FWD_HOST_FIRST)


def _from_host(table, got):
    return {k: got[name][i] for name, ks in table.items() for i, k in enumerate(ks)}


def _to_host(table, arrays):
    return {name: [arrays[k] for k in ks] for name, ks in table.items()}


def _join_cols(g):
    return jnp.concatenate([g[j] for j in range(N_CHIPS)], axis=1)


def _split_cols(g):
    n = g.shape[1] // N_CHIPS
    return jnp.stack([g[:, j * n:(j + 1) * n] for j in range(N_CHIPS)])


def _layer_weights(G, small, l):
    D = D_MODEL
    return {
        **_large_operands(G),
        "norm1_w": small["norm1_w"][l].reshape(1, D),
        "sinks": jnp.broadcast_to(small["a_sinks"][l].reshape(A_Q_HEADS, 1, 1), (A_Q_HEADS, 1, LANES)),
        "b_q_norm_w": small["b_q_norm_w"][l].reshape(1, B_Q_RANK),
        "b_kv_norm_w": small["b_kv_norm_w"][l].reshape(1, B_KV_RANK),
        "c_ln_w": small["c_ln_w"][l].reshape(1, C_WIDTH), "c_ln_b": small["c_ln_b"][l].reshape(1, C_WIDTH),
        "c_w_s": small["c_w_s"][l], "c_b_col": small["c_b_s"][l].reshape(C_GROUPS, C_CHUNK, 1),
        "out_norm_w": small["out_norm_w"][l].reshape(1, D), "norm2_w": small["norm2_w"][l].reshape(1, D),
    }


def _large_operands(G):
    D = D_MODEL
    make = {
        "w_in": lambda g: _pad_axis(_join_cols(g), _map_w_in(), 1),
        "b_w_uq": lambda g: _pad_axis(_join_cols(g), _map_w_uq(), 1),
        "b_w_ukv": lambda g: _pad_axis(_join_cols(g), _map_w_ukv(), 1),
        "w_out": lambda g: g.reshape(D, D), "w_gate_up": lambda g: g, "w_down": lambda g: g.reshape(FFN_HIDDEN, D),
    }
    return {k: make[k](g) for k, g in G.items()}


def _send_buffers(gr):
    D = D_MODEL
    return {
        "w_gate_up": gr["w_gate_up"], "w_down": gr["w_down"].reshape(N_CHIPS, FFN_HIDDEN // N_CHIPS, D),
        "w_out": gr["w_out"].reshape(N_CHIPS, D // N_CHIPS, D),
        "w_in": _split_cols(_unpad_axis(gr["w_in"], _map_w_in(), IN_COLS, 1)),
        "b_w_uq": _split_cols(_unpad_axis(gr["b_w_uq"], _map_w_uq(), B_HEADS * (B_NOPE + B_ROPE), 1)),
        "b_w_ukv": _split_cols(_unpad_axis(gr["b_w_ukv"], _map_w_ukv(), B_HEADS * (B_NOPE + B_V), 1)),
    }


def _small_grads(gr, B):
    D = D_MODEL
    return {
        "norm1_w": gr["norm1_w"].reshape(D),
        "a_sinks": gr["sinks"][:, 0, 0].reshape(B, A_Q_HEADS).sum(axis=0),
        "b_q_norm_w": gr["b_q_norm_w"].reshape(B_Q_RANK), "b_kv_norm_w": gr["b_kv_norm_w"].reshape(B_KV_RANK),
        "c_ln_w": gr["c_ln_w"].reshape(C_WIDTH), "c_ln_b": gr["c_ln_b"].reshape(C_WIDTH), "c_w_s": gr["c_w_s"],
        "c_b_s": gr["c_b_col"].reshape(C_GROUPS, C_CHUNK),
        "out_norm_w": gr["out_norm_w"].reshape(D), "norm2_w": gr["norm2_w"].reshape(D),
    }


def _step(x, c, positions, target, small, shard_of, ada=None, gathered=None):
    dist = gathered is None
    B, S, D = x.shape
    T = B * S
    xt = x.reshape(T, D)
    tgt = target.reshape(T, D)
    pos_col = positions.astype(F32).reshape(T, 1)
    inv_a = 1.0 / (ROPE_THETA ** (jnp.arange(0, HEAD_DIM, 2, dtype=F32) / HEAD_DIM))
    inv_b = 1.0 / (ROPE_THETA ** (jnp.arange(0, B_ROPE, 2, dtype=F32) / B_ROPE))
    fa, sga = _lane_table(np.arange(32), 64 + np.arange(32), inv_a)
    fb, sgb = _lane_table(48 + np.arange(16), 112 + np.arange(16), inv_b)
    ca, sa = _rope_tables(pos_col, fa, sga, jnp.abs(sga), "rope_a")
    cb, sb = _rope_tables(pos_col, fb, sgb, jnp.ones_like(sgb), "rope_b")
    tabs = (ca, sa, cb, sb)
    ada_b = small["ada_b"]
    if dist:
        assert N_CHIPS * B == 8
        me = 2 * lax.axis_index("x") + lax.axis_index("y")
        first = shard_of(0)
        c_all, w_in_first = _xchg_call("gather", [c, first["w_in"]], "gather_first")
        c8 = c_all.reshape(N_CHIPS * B, D)
        mine = lax.dynamic_slice_in_dim(ada_b, me * N_ADA, N_ADA, axis=1).reshape(DEPTH, 1, N_ADA)
        part = _ada_fwd(c8, ada, mine)
        part = part.reshape(DEPTH, N_CHIPS, B, N_ADA).transpose(1, 0, 2, 3).reshape(N_CHIPS, DEPTH * B, N_ADA)
        (back,) = _xchg_call("a2a", [part], "mod_exchange")
        mod_all = jnp.concatenate([back[j].reshape(DEPTH, B, N_ADA) for j in range(N_CHIPS)], axis=-1)
        G = {"w_in": w_in_first}
    else:
        c8 = jnp.zeros((8, D), F32).at[:B].set(c)
        cols = [(jnp.stack([gathered[l]["ada_w"][j] for l in range(DEPTH)]),
                 ada_b[:, j * N_ADA:(j + 1) * N_ADA].reshape(DEPTH, 1, N_ADA)) for j in range(N_CHIPS)]
        mod_all = jnp.concatenate([_ada_fwd(c8, wj, bj)[:, :B] for wj, bj in cols], axis=-1)
        G = {k: v for k, v in gathered[0].items() if k != "ada_w"}
    saved, prevs, modss = [], [], []
    prev = None
    for l in range(DEPTH):
        w = _layer_weights(G, small, l)
        mods = tuple(mod_all[l, :, i * D:(i + 1) * D].reshape(B, 1, D) for i in range(N_MOD))
        more = l + 1 < DEPTH
        table = FWD_HOST_FIRST if l == 0 else FWD_HOST
        host = _to_host(table, shard_of(l + 1)) if dist and more else {}
        late = None
        if dist and l == 0:
            late = FIRST_LATE
            for name, ks in late.items():
                host[name] = [first[k] for k in ks] + host.get(name, [])
        prevs.append(prev)
        modss.append(mods)
        prev, sv, got = _layer_fwd(xt, prev, mods, w, tabs, B, S, host, late)
        saved.append(sv)
        if more:
            G = _from_host(table, got) if dist else {k: v for k, v in gathered[l + 1].items() if k != "ada_w"}
    x1, d, g2 = prev
    loss, dx1, dd, dg2, dfw = _final(x1, d, g2, small["final_norm_w"].reshape(1, D), tgt, B, S)

    landed = [None] * DEPTH
    smalls = [None] * DEPTH
    dmods = [None] * DEPTH
    pending = None
    for l in reversed(range(DEPTH)):
        host = _to_host(BWD_HOST, pending) if dist and pending is not None else None
        early = dist and l == 0 and host is not None
        gr, (dsh1, dsc1, dg1, dsh2, dsc2), nxt, got = _layer_bwd(saved[l], prevs[l], modss[l], tabs, dx1, dd, B, S, host, early)
        if pending is not None:
            landed[l + 1] = _from_host(BWD_HOST, got) if dist else pending
        dmods[l] = jnp.concatenate([dsh1, dsc1, dg1, dsh2, dsc2, dg2], axis=-1).reshape(B, N_MOD * D)
        pending = _send_buffers(gr)
        smalls[l] = _small_grads(gr, B)
        dx1, dd, dg2 = nxt
    dmod_all = jnp.stack(dmods)
    by_chip = [dmod_all[:, :, j * N_ADA:(j + 1) * N_ADA] for j in range(N_CHIPS)]
    if dist:
        last = EXPOSED if early else EXCHANGED
        send = jnp.stack([p.reshape(DEPTH * B, N_ADA) for p in by_chip])
        *res, back = _xchg_call("a2a", [pending[k] for k in last] + [send], "grad_exchange_last")
        landed[0] = dict(zip(last, res))
        if early:
            landed[0].update(zip(OWN_LAYER, got["attn_b_bwd"][-len(OWN_LAYER):]))
        dmod8 = back.reshape(N_CHIPS, DEPTH, B, N_ADA).transpose(1, 0, 2, 3).reshape(DEPTH, N_CHIPS * B, N_ADA)
        ada_g = _ada_bwd(c8, dmod8)
    else:
        landed[0] = pending
        ada_g = jnp.stack([_ada_bwd(c8, jnp.zeros((DEPTH, 8, N_ADA), F32).at[:, :B].set(p)) for p in by_chip])
    small_g = {k: jnp.stack([smalls[l][k] for l in range(DEPTH)]) for k in SMALL if k not in ("final_norm_w", "ada_b")}
    small_g["ada_b"] = _sum_examples(dmod_all).reshape(DEPTH, N_MOD * D)
    small_g["final_norm_w"] = dfw.reshape(D)
    return loss[0, 0], dx1.reshape(B, S, D), landed, small_g, ada_g


def _pack(arrs, cols, mult):
    flat = jnp.concatenate([a.reshape(-1) for a in arrs])
    n = flat.shape[0]
    rows = -(-n // cols)
    rows = -(-rows // mult) * mult
    return jnp.pad(flat, (0, rows * cols - n)).reshape(rows, cols)


def _unpack(blob, shapes):
    flat = blob.reshape(-1)
    out, off = [], 0
    for s in shapes:
        n = int(np.prod(s))
        out.append(flat[off:off + n].reshape(s))
        off += n
    return out


def kernel(x, c, positions, ada_w, ada_b, norm1_w, w_in, a_sinks, b_q_norm_w, b_w_uq, b_kv_norm_w, b_w_ukv, c_ln_w, c_ln_b, c_w_s, c_b_s, out_norm_w, w_out, norm2_w, w_gate_up, w_down, final_norm_w, loss_target, m_ada_w, m_ada_b, m_norm1_w, m_w_in, m_a_sinks, m_b_q_norm_w, m_b_w_uq, m_b_kv_norm_w, m_b_w_ukv, m_c_ln_w, m_c_ln_b, m_c_w_s, m_c_b_s, m_out_norm_w, m_w_out, m_norm2_w, m_w_gate_up, m_w_down, m_final_norm_w, v_ada_w, v_ada_b, v_norm1_w, v_w_in, v_a_sinks, v_b_q_norm_w, v_b_w_uq, v_b_kv_norm_w, v_b_w_ukv, v_c_ln_w, v_c_ln_b, v_c_w_s, v_c_b_s, v_out_norm_w, v_w_out, v_norm2_w, v_w_gate_up, v_w_down, v_final_norm_w):
    names = ("ada_w", "ada_b", "norm1_w", "w_in", "a_sinks", "b_q_norm_w", "b_w_uq", "b_kv_norm_w", "b_w_ukv", "c_ln_w",
             "c_ln_b", "c_w_s", "c_b_s", "out_norm_w", "w_out", "norm2_w", "w_gate_up", "w_down", "final_norm_w")
    ws = dict(zip(names, (ada_w, ada_b, norm1_w, w_in, a_sinks, b_q_norm_w, b_w_uq, b_kv_norm_w, b_w_ukv, c_ln_w, c_ln_b,
                          c_w_s, c_b_s, out_norm_w, w_out, norm2_w, w_gate_up, w_down, final_norm_w)))
    ms = dict(zip(names, (m_ada_w, m_ada_b, m_norm1_w, m_w_in, m_a_sinks, m_b_q_norm_w, m_b_w_uq, m_b_kv_norm_w, m_b_w_ukv,
                          m_c_ln_w, m_c_ln_b, m_c_w_s, m_c_b_s, m_out_norm_w, m_w_out, m_norm2_w, m_w_gate_up, m_w_down,
                          m_final_norm_w)))
    vs = dict(zip(names, (v_ada_w, v_ada_b, v_norm1_w, v_w_in, v_a_sinks, v_b_q_norm_w, v_b_w_uq, v_b_kv_norm_w, v_b_w_ukv,
                          v_c_ln_w, v_c_ln_b, v_c_w_s, v_c_b_s, v_out_norm_w, v_w_out, v_norm2_w, v_w_gate_up, v_w_down,
                          v_final_norm_w)))
    shards = {k: ws[k].astype(BF16) for k in SHARDED}
    loss_local, grad_x, landed, gsmall, ada_g = _step(
        x, c, positions, loss_target, {k: ws[k] for k in SMALL}, lambda l: {k: shards[k][l] for k in EXCHANGED},
        ada=shards["ada_w"])

    mine = {k: jnp.stack([_sum_slots(landed[l][k], "grad_sum_" + k) for l in range(DEPTH)]) for k in EXCHANGED}
    mine["ada_w"] = ada_g
    theirs = dict(zip(SWAP_ORDER[0], _xchg_call("swap", [mine[k] for k in SWAP_ORDER[0]], "grad_sibling_swap")))
    hosted = {group[0]: nxt for group, nxt in zip(SWAP_ORDER, SWAP_ORDER[1:])}
    grads, delta, new_m, new_v = {}, {}, {}, {}
    for k in [k for group in SWAP_ORDER for k in group]:
        shp = ws[k].shape
        two = (shp[0] * shp[1], shp[2])
        args = (ws[k].reshape(two), mine[k].reshape(two), theirs[k].reshape(two), ms[k].reshape(two), vs[k].reshape(two))
        if k in hosted:
            (g, dlt, nm, nv), got = _adamw(*args, "adamw_" + k, xchg=("swap", [mine[j] for j in hosted[k]]))
            theirs.update(zip(hosted[k], got))
        else:
            g, dlt, nm, nv = _adamw(*args, "adamw_" + k)
        grads[k], delta[k], new_m[k], new_v[k] = g.reshape(shp), dlt.reshape(shp), nm.reshape(shp), nv.reshape(shp)

    small_shapes = [ws[k].shape for k in SMALL]
    sblob = _pack([gsmall[k] for k in SMALL] + [loss_local.reshape(1)], LANES, 8)
    svals = _unpack(_all_reduce_small(sblob, "small_all_reduce"), small_shapes + [(1,)])
    loss = svals[-1].reshape(())
    pw = _pack([ws[k] for k in SMALL], LANES, 8)
    pg = _pack(svals[:-1], LANES, 8)
    pm = _pack([ms[k] for k in SMALL], LANES, 8)
    pv = _pack([vs[k] for k in SMALL], LANES, 8)
    g, dlt, nm, nv = _adamw(pw, pg, None, pm, pv, "adamw_small")
    for k, a, b_, c_, d_ in zip(SMALL, _unpack(g, small_shapes), _unpack(dlt, small_shapes), _unpack(nm, small_shapes),
                                _unpack(nv, small_shapes)):
        grads[k], delta[k], new_m[k], new_v[k] = a, b_, c_, d_

    return (loss, grad_x, *[grads[k] for k in names], *[delta[k] for k in names], *[new_m[k] for k in names],
            *[new_v[k] for k in names])
```

```python
import functools
import math

import numpy as np
import jax
import jax.numpy as jnp
from jax import lax
from jax.experimental import pallas as pl
from jax.experimental.pallas import tpu as pltpu

F32 = jnp.float32
BF16 = jnp.bfloat16

D_MODEL = 1024
DEPTH = 4
HEAD_DIM = 64
ROPE_THETA = 10000.0
NORM_EPS = 1e-6
NEG_INF = -1e30
LOG2_E = math.log2(math.e)
A_Q_HEADS = 6
A_KV_HEADS = 2
A_WINDOW = 128
B_HEADS = 6
B_Q_RANK = 384
B_KV_RANK = 256
B_NOPE = 64
B_ROPE = 32
B_V = 64
C_GROUPS = 4
C_GROUP_DIM = 64
C_WIDTH = 256
C_CHUNK = 128
IN_COLS = 1824
FFN_HIDDEN = 2816
N_MOD = 6
ADAM_LR = 0.001
ADAM_B1 = 0.9
ADAM_B2 = 0.999
ADAM_EPS = 1e-08
ADAM_WD = 0.01
ADAM_STEP = 10

LANES = 128
VMEM_LIMIT = 56 * 1024 * 1024
N_CHIPS = 4
WINDOW_SUB = 256

P_KR, P_AQ, P_AK, P_AV, P_CQ, P_CKV, P_MIX, P_CU, P_CV, P_END = 0, 128, 512, 640, 768, 1152, 1408, 1536, 1792, 2048


def _map_w_in():
    idx = -np.ones(P_END, np.int64)
    idx[P_AQ:P_AQ + 384] = np.arange(384)
    idx[P_AK:P_AK + 128] = 384 + np.arange(128)
    idx[P_AV:P_AV + 128] = 512 + np.arange(128)
    idx[P_CQ:P_CQ + 384] = 640 + np.arange(384)
    idx[P_CKV:P_CKV + 256] = 1024 + np.arange(256)
    idx[P_KR + 48 + np.arange(16)] = 1280 + np.arange(16)
    idx[P_KR + 112 + np.arange(16)] = 1296 + np.arange(16)
    idx[P_CU:P_CU + 256] = 1312 + np.arange(256)
    idx[P_CV:P_CV + 256] = 1568 + np.arange(256)
    return idx


def _map_w_uq():
    idx = -np.ones(B_HEADS * LANES, np.int64)
    for h in range(B_HEADS):
        b = h * (B_NOPE + B_ROPE)
        idx[h * LANES + np.arange(48)] = b + np.arange(48)
        idx[h * LANES + 48 + np.arange(16)] = b + 64 + np.arange(16)
        idx[h * LANES + 64 + np.arange(16)] = b + 48 + np.arange(16)
        idx[h * LANES + 112 + np.arange(16)] = b + 80 + np.arange(16)
    return idx


def _map_w_ukv():
    idx = -np.ones(2 * B_HEADS * LANES, np.int64)
    for h in range(B_HEADS):
        b = h * (B_NOPE + B_V)
        idx[h * LANES + np.arange(48)] = b + np.arange(48)
        idx[h * LANES + 64 + np.arange(16)] = b + 48 + np.arange(16)
        idx[B_HEADS * LANES + h * LANES + np.arange(B_V)] = b + B_NOPE + np.arange(B_V)
    return idx


def _inverse(idx, n):
    inv = np.zeros(n, np.int64)
    pos = np.nonzero(idx >= 0)[0]
    inv[idx[pos]] = pos
    return inv


def _runs(idx):
    runs, i, n = [], 0, len(idx)
    while i < n:
        j = i + 1
        while j < n and ((idx[i] < 0 and idx[j] < 0) or (idx[i] >= 0 and idx[j] == idx[i] + (j - i))):
            j += 1
        runs.append((int(idx[i]), j - i))
        i = j
    return runs


def _select_axis(w, idx, axis):
    pieces = []
    for start, length in _runs(idx):
        if start < 0:
            shape = list(w.shape)
            shape[axis] = length
            pieces.append(jnp.zeros(shape, w.dtype))
        else:
            pieces.append(lax.slice_in_dim(w, start, start + length, axis=axis))
    return jnp.concatenate(pieces, axis=axis)


def _pad_axis(w, idx, axis):
    return _select_axis(w, idx, axis)


def _unpad_axis(g, idx, n, axis):
    return _select_axis(g, _inverse(idx, n), axis)


def _params(sem):
    return pltpu.CompilerParams(dimension_semantics=sem, vmem_limit_bytes=VMEM_LIMIT)


def _tile(dim, target):
    if dim <= target:
        return dim
    best = None
    for t in range(LANES, target + 1, LANES):
        if dim % t == 0:
            best = t
    assert best is not None, dim
    return best


def _row_div(rows, target):
    if rows <= target:
        return rows
    best = None
    for t in range(8, target + 1, 8):
        if rows % t == 0:
            best = t
    assert best is not None, rows
    return best


_ANY = pl.BlockSpec(memory_space=pl.ANY)
_MESH = pl.DeviceIdType.MESH


def _xchg_out_shapes(kind, srcs):
    if kind == "gather":
        return [jax.ShapeDtypeStruct((N_CHIPS,) + s.shape, s.dtype) for s in srcs]
    return [jax.ShapeDtypeStruct(s.shape, s.dtype) for s in srcs]


def _xchg_scratch(kind, n):
    per = 1 if kind == "swap" else N_CHIPS - 1
    return [pltpu.SemaphoreType.DMA((per * n,)), pltpu.SemaphoreType.DMA((per * n,)), pltpu.SemaphoreType.DMA((n,))]


def _xchg_copies(kind, srcs, dsts, send_sems, recv_sems, local_sems, arrivals):
    x, y, c = lax.axis_index("x"), lax.axis_index("y"), lax.axis_index("c")
    me = 2 * x + y
    peers = [(1 - x, y), (x, 1 - y), (1 - x, 1 - y)]
    local, out, back = [], [], []
    for i, (s, d) in enumerate(zip(srcs, dsts)):
        if kind == "swap":
            cp = pltpu.make_async_remote_copy(src_ref=s, dst_ref=d, send_sem=send_sems.at[i], recv_sem=recv_sems.at[i],
                                              device_id=(x, y, 1 - c), device_id_type=_MESH)
            out.append(cp)
            back.append(cp)
            continue
        local.append(pltpu.make_async_copy(s if kind == "gather" else s.at[me], d.at[me], local_sems.at[i]))
        for kk, (px, py) in enumerate(peers):
            j = (N_CHIPS - 1) * i + kk
            theirs = 2 * px + py
            out.append(pltpu.make_async_remote_copy(
                src_ref=s if kind == "gather" else s.at[theirs], dst_ref=d.at[me], send_sem=send_sems.at[j],
                recv_sem=recv_sems.at[j], device_id=(px, py, c), device_id_type=_MESH))
            if arrivals:
                back.append(pltpu.make_async_remote_copy(
                    src_ref=s if kind == "gather" else s.at[me], dst_ref=d.at[theirs], send_sem=send_sems.at[j],
                    recv_sem=recv_sems.at[j], device_id=(px, py, c), device_id_type=_MESH))
    return local, out, back


def _xchg_start(kind, srcs, dsts, sems):
    local, out, _ = _xchg_copies(kind, srcs, dsts, *sems, arrivals=False)
    for cp in local + out:
        cp.start()


def _xchg_wait(kind, srcs, dsts, sems):
    local, out, back = _xchg_copies(kind, srcs, dsts, *sems, arrivals=True)
    for cp in back:
        cp.wait_recv()
    for cp in out:
        cp.wait_send()
    for cp in local:
        cp.wait()


def _xchg_at_ends(kind, srcs, dsts, sems, grid, first):
    ids = [pl.program_id(a) for a in range(len(grid))]
    cond = None
    for i, n in zip(ids, grid):
        c = (i == 0) if first else (i == n - 1)
        cond = c if cond is None else jnp.logical_and(cond, c)

    @pl.when(cond)
    def _():
        (_xchg_start if first else _xchg_wait)(kind, srcs, dsts, sems)


def _xchg_call(kind, srcs, name):
    n = len(srcs)

    def kern(*refs):
        s, d, sems = refs[:n], refs[n:2 * n], refs[2 * n:]
        _xchg_start(kind, s, d, sems)
        _xchg_wait(kind, s, d, sems)

    return pl.pallas_call(
        kern, name=name, in_specs=[_ANY] * n, out_specs=[_ANY] * n, out_shape=_xchg_out_shapes(kind, srcs),
        scratch_shapes=_xchg_scratch(kind, n),
    )(*srcs)


_DIMS = {"nn": (((1,), (0,)), ((), ())), "nt": (((1,), (1,)), ((), ())), "tn": (((0,), (0,)), ((), ()))}


def _matmul(a, b, mode, out_dtype, name, *, b_chunks=False, out_chunks=False, xchg=None):
    if b_chunks:
        nchunk, brows, bcols = b.shape
        bshape = (brows, nchunk * bcols)
    else:
        bshape = b.shape
    if mode == "nn":
        (m, k), (_, n) = a.shape, bshape
    elif mode == "nt":
        (m, k), (n, _) = a.shape, bshape
    else:
        (k, m), (_, n) = a.shape, bshape
    tm, tk = (1408, 1024) if mode == "tn" else (1024, 1408)
    tm, tn, tk = _tile(m, tm), _tile(n, 1408), _tile(k, tk)
    if b_chunks:
        if mode == "nn":
            tn = bcols
        else:
            assert mode == "nt"
            tk = bcols
    if out_chunks:
        assert n % N_CHIPS == 0
        tn = n // N_CHIPS
    ni, nj, nk = m // tm, n // tn, k // tk
    dims = _DIMS[mode]
    n_x = 0 if xchg is None else len(xchg[1])

    def kern(*refs):
        a_ref, b_ref = refs[0], refs[1]
        xs = refs[2:2 + n_x]
        o_ref = refs[2 + n_x]
        xd = refs[3 + n_x:3 + 2 * n_x]
        acc_ref = refs[3 + 2 * n_x]
        sems = refs[4 + 2 * n_x:]
        kk = pl.program_id(2)
        if n_x:
            _xchg_at_ends(xchg[0], xs, xd, sems, (ni, nj, nk), True)

        @pl.when(kk == 0)
        def _():
            acc_ref[...] = jnp.zeros_like(acc_ref)

        acc_ref[...] += lax.dot_general(a_ref[...], b_ref[...], dims, preferred_element_type=F32)

        @pl.when(kk == nk - 1)
        def _():
            o_ref[...] = acc_ref[...].astype(o_ref.dtype)

        if n_x:
            _xchg_at_ends(xchg[0], xs, xd, sems, (ni, nj, nk), False)

    if mode == "tn":
        a_spec = pl.BlockSpec((tk, tm), lambda i, j, kk: (kk, i))
    else:
        a_spec = pl.BlockSpec((tm, tk), lambda i, j, kk: (i, kk))
    if b_chunks and mode == "nn":
        b_spec = pl.BlockSpec((None, tk, tn), lambda i, j, kk: (j, kk, 0))
    elif b_chunks:
        b_spec = pl.BlockSpec((None, tn, tk), lambda i, j, kk: (kk, j, 0))
    elif mode == "nt":
        b_spec = pl.BlockSpec((tn, tk), lambda i, j, kk: (j, kk))
    else:
        b_spec = pl.BlockSpec((tk, tn), lambda i, j, kk: (kk, j))
    if out_chunks:
        o_spec = pl.BlockSpec((None, tm, tn), lambda i, j, kk: (j, i, 0))
        o_shape = jax.ShapeDtypeStruct((N_CHIPS, m, tn), out_dtype)
    else:
        o_spec = pl.BlockSpec((tm, tn), lambda i, j, kk: (i, j))
        o_shape = jax.ShapeDtypeStruct((m, n), out_dtype)
    xs = [] if xchg is None else list(xchg[1])
    res = pl.pallas_call(
        kern, name=name, grid=(ni, nj, nk),
        in_specs=[a_spec, b_spec] + [_ANY] * n_x, out_specs=[o_spec] + [_ANY] * n_x,
        out_shape=[o_shape] + (_xchg_out_shapes(xchg[0], xs) if n_x else []),
        scratch_shapes=[pltpu.VMEM((tm, tn), F32)] + (_xchg_scratch(xchg[0], n_x) if n_x else []),
        compiler_params=_params(("arbitrary", "arbitrary", "arbitrary") if n_x else ("parallel", "parallel", "arbitrary")),
    )(a, b, *xs)
    return (res[0], list(res[1:])) if n_x else res[0]


def _rowcall(body, *, name, T, S, tr, rows, exs=(), pars=(), row_outs=(), ex_outs=(), par_outs=(), aliases=None):
    assert S % tr == 0 and T % S == 0
    per_ex = S // tr
    nb = T // S
    n_rows, n_exs, n_pars = len(rows), len(exs), len(pars)
    n_ro, n_eo, n_po = len(row_outs), len(ex_outs), len(par_outs)

    def kern(*refs):
        ins = refs[:n_rows + n_exs + n_pars]
        outs = refs[n_rows + n_exs + n_pars:]
        rv = [r[...].astype(F32) for r in ins[:n_rows]]
        ev = [r[0] for r in ins[n_rows:n_rows + n_exs]]
        pv = [r[...] for r in ins[n_rows + n_exs:]]
        ro, eo, po = body(rv, ev, pv)
        i = pl.program_id(0)
        for ref, val in zip(outs[:n_ro], ro):
            if isinstance(val, (list, tuple)):
                off = 0
                for piece in val:
                    w = piece.shape[-1]
                    ref[:, off:off + w] = piece.astype(ref.dtype)
                    off += w
            else:
                ref[...] = val.astype(ref.dtype)
        first_of_ex = (i % per_ex) == 0
        for ref, val in zip(outs[n_ro:n_ro + n_eo], eo):
            @pl.when(first_of_ex)
            def _(ref=ref, val=val):
                ref[0] = val

            @pl.when(jnp.logical_not(first_of_ex))
            def _(ref=ref, val=val):
                ref[0] += val
        for ref, val in zip(outs[n_ro + n_eo:], po):
            @pl.when(i == 0)
            def _(ref=ref, val=val):
                ref[...] = val

            @pl.when(i != 0)
            def _(ref=ref, val=val):
                ref[...] += val

    in_specs = [pl.BlockSpec((tr, w), functools.partial(lambda i, cb: (i, cb), cb=cb)) for (_, w, cb) in rows]
    in_specs += [pl.BlockSpec((1, 1, e.shape[-1]), lambda i: (i // per_ex, 0, 0)) for e in exs]
    in_specs += [pl.BlockSpec(p.shape, functools.partial(lambda i, nd: (0,) * nd, nd=p.ndim)) for p in pars]
    out_specs = [pl.BlockSpec((tr, w), functools.partial(lambda i, cb: (i, cb), cb=cb)) for (_, _, w, cb) in row_outs]
    out_specs += [pl.BlockSpec((1, 1, f), lambda i: (i // per_ex, 0, 0)) for f in ex_outs]
    out_specs += [pl.BlockSpec(tuple(s), functools.partial(lambda i, nd: (0,) * nd, nd=len(s))) for s in par_outs]
    out_shape = [jax.ShapeDtypeStruct((T, tw), dt) for (tw, dt, _, _) in row_outs]
    out_shape += [jax.ShapeDtypeStruct((nb, 1, f), F32) for f in ex_outs]
    out_shape += [jax.ShapeDtypeStruct(tuple(s), F32) for s in par_outs]
    res = pl.pallas_call(
        kern, name=name, grid=(T // tr,), in_specs=in_specs, out_specs=out_specs, out_shape=out_shape,
        input_output_aliases=aliases or {}, compiler_params=_params(("arbitrary",)),
    )(*[r[0] for r in rows], *exs, *pars)
    return res[:n_ro], res[n_ro:n_ro + n_eo], res[n_ro + n_eo:]


def _rms(x, w, n=None):
    n = x.shape[-1] if n is None else n
    ms = jnp.sum(x * x, axis=-1, keepdims=True) * (1.0 / n)
    return x * lax.rsqrt(ms + NORM_EPS) * w


def _gelu(x):
    return 0.5 * x * (1.0 + lax.erf(x * np.float32(1.0 / math.sqrt(2.0))))


def _silu(x):
    return x * jax.nn.sigmoid(x)


@jax.custom_vjp
def _rope(x, cos, sin):
    return x * cos + pltpu.roll(x, 64, 1) * sin


def _rope_fwd(x, cos, sin):
    return _rope(x, cos, sin), (cos, sin)


def _rope_bwd(res, dy):
    cos, sin = res
    return dy * cos + pltpu.roll(dy * sin, 64, 1), None, None


_rope.defvjp(_rope_fwd, _rope_bwd)


def _heads(x, n):
    return [x[:, h * LANES:(h + 1) * LANES] for h in range(n)]


@functools.partial(jax.custom_vjp, nondiff_argnums=(1,))
def _lroll(x, shift):
    return pltpu.roll(x, shift, 1)


def _lroll_fwd(x, shift):
    return _lroll(x, shift), None


def _lroll_bwd(shift, _, dy):
    return (pltpu.roll(dy, (LANES - shift) % LANES, 1),)


_lroll.defvjp(_lroll_fwd, _lroll_bwd)


def _spread_rotary(v):
    low = lax.broadcasted_iota(jnp.int32, (1, LANES), 1) < 64
    return [jnp.where(low, v, _lroll(v, 32)), jnp.where(low, _lroll(v, 64), _lroll(v, 96))]


def _spread_values(v):
    low = lax.broadcasted_iota(jnp.int32, (1, LANES), 1) < 64
    return [jnp.where(low, v, 0.0), jnp.where(low, _lroll(v, 64), 0.0)]


def _f_norm_mod(x, w, sc, sh):
    return _rms(x, w) * (1.0 + sc) + sh


def _f_resid_norm_mod(xa, delta, g, w, sc, sh):
    xn = xa + g * delta
    return xn, _f_norm_mod(xn, w, sc, sh)


def _f_mixprep(proj, ca, sa, cb, sb, qnw, kvnw):
    qa = [_rope(p, ca, sa) for pair in _heads(proj[:, P_AQ:P_AK], A_Q_HEADS // 2) for p in _spread_rotary(pair)]
    ka = [_rope(p, ca, sa) for p in _spread_rotary(proj[:, P_AK:P_AV])]
    va = _spread_values(proj[:, P_AV:P_CQ])
    cqn = _rms(proj[:, P_CQ:P_CKV], qnw)
    ckvn = _rms(proj[:, P_CKV:P_MIX], kvnw)
    kr = _rope(proj[:, P_KR:P_AQ], cb, sb)
    return jnp.concatenate(qa, -1), jnp.concatenate(ka, -1), jnp.concatenate(va, -1), cqn, ckvn, kr


def _f_mlaprep(q, kv, kr, cb, sb):
    qs = [_rope(p, cb, sb) for p in _heads(q, B_HEADS)]
    ks = [p + kr for p in _heads(kv[:, :B_HEADS * LANES], B_HEADS)]
    return jnp.concatenate(qs, -1), jnp.concatenate(ks, -1), kv[:, B_HEADS * LANES:]


def _f_sgu(cu, cv, ln_w, ln_b, w_s, b_col):
    u = _gelu(cu)
    v = _gelu(cv)
    mu = jnp.mean(v, axis=-1, keepdims=True)
    var = jnp.mean(jnp.square(v - mu), axis=-1, keepdims=True)
    vn = (v - mu) * lax.rsqrt(var + NORM_EPS) * ln_w + ln_b
    r = lax.broadcasted_iota(jnp.int32, (C_CHUNK, C_CHUNK), 0)
    c = lax.broadcasted_iota(jnp.int32, (C_CHUNK, C_CHUNK), 1)
    lane = lax.broadcasted_iota(jnp.int32, (1, LANES), 1)
    per_block = LANES // C_GROUP_DIM
    blocks = []
    for blk, vb in enumerate(_heads(vn, C_WIDTH // LANES)):
        mixed = jnp.zeros(vb.shape, F32)
        for j in range(per_block):
            g = blk * per_block + j
            gm = (lane // C_GROUP_DIM == j).astype(F32)
            wg = jnp.where(r >= c, w_s[g], 0.0).astype(BF16)
            mixed = mixed + jnp.dot(wg, (vb * gm).astype(BF16), preferred_element_type=F32) + b_col[g] * gm
        blocks.append(mixed)
    return u * jnp.concatenate(blocks, -1)


@jax.custom_vjp
def _pack_pairs(x):
    heads = _heads(x, x.shape[-1] // LANES)
    return jnp.concatenate([heads[i] + pltpu.roll(heads[i + 1], 64, 1) for i in range(0, len(heads), 2)], -1)


def _pack_pairs_fwd(x):
    return _pack_pairs(x), None


def _pack_pairs_bwd(_, dy):
    out = []
    for p in _heads(dy, dy.shape[-1] // LANES):
        out += [p, pltpu.roll(p, 64, 1)]
    return (jnp.concatenate(out, -1),)


_pack_pairs.defvjp(_pack_pairs_fwd, _pack_pairs_bwd)


def _f_outnorm(oa, ob, yc, gw):
    na, nb = A_Q_HEADS * HEAD_DIM, B_HEADS * B_V
    ya = _rms(_pack_pairs(oa), gw[:, :na])
    yb = _rms(_pack_pairs(ob), gw[:, na:na + nb])
    ycn = _rms(yc, gw[:, na + nb:])
    return jnp.concatenate([ya, yb, ycn], -1)


def _f_swiglu(gate, up):
    return _silu(gate) * up


def _mask(q_start, k_start, tq, tk, window):
    qpos = q_start + lax.broadcasted_iota(jnp.int32, (tq, tk), 0)
    kpos = k_start + lax.broadcasted_iota(jnp.int32, (tq, tk), 1)
    m = kpos <= qpos
    if window is not None:
        m = jnp.logical_and(m, qpos - kpos < window)
    return m


def _tile_fwd(qv, kk, vv, q_start, k_start, n_free, scale, window, m0, l0):
    tq = qv.shape[0]
    W = kk.shape[0]
    c = scale * LOG2_E
    parts = []
    if n_free > 0:
        parts.append((lax.dot_general(qv, kk[:n_free], _DIMS["nt"], preferred_element_type=F32), vv[:n_free]))
    if W > n_free:
        s = lax.dot_general(qv, kk[n_free:], _DIMS["nt"], preferred_element_type=F32)
        s = jnp.where(_mask(q_start, k_start + n_free, tq, W - n_free, window), s, NEG_INF)
        parts.append((s, vv[n_free:]))
    m = None if m0 is None else m0 * (1.0 / scale)
    for s, _ in parts:
        mx = jnp.max(s, axis=-1, keepdims=True)
        m = mx if m is None else jnp.maximum(m, mx)
    l = None if l0 is None else l0 * jnp.exp2((m0 * (1.0 / scale) - m) * c)
    o = None
    for s, vpart in parts:
        p = jnp.exp2((s - m) * c)
        ps = jnp.sum(p, axis=-1, keepdims=True)
        l = ps if l is None else l + ps
        po = jnp.dot(p.astype(BF16), vpart, preferred_element_type=F32)
        o = po if o is None else o + po
    return o / l, m * scale + jnp.log(l)


def _tile_bwd(qv, kk, vv, dof, ov, lse, q_start, k_start, n_free, scale, window):
    tq = qv.shape[0]
    W = kk.shape[0]
    dob = dof.astype(BF16)
    delta = jnp.sum(dof * ov, axis=-1, keepdims=True)
    c = scale * LOG2_E
    lse2 = lse * LOG2_E
    dq = None
    outs = []
    for (a, b, masked) in ((0, n_free, False), (n_free, W, True)):
        if b <= a:
            continue
        kp, vp = kk[a:b], vv[a:b]
        s = lax.dot_general(qv, kp, _DIMS["nt"], preferred_element_type=F32)
        if masked:
            s = jnp.where(_mask(q_start, k_start + a, tq, b - a, window), s, NEG_INF)
        p = jnp.exp2(s * c - lse2)
        dp = lax.dot_general(dob, vp, _DIMS["nt"], preferred_element_type=F32)
        ds = (p * ((dp - delta) * scale)).astype(BF16)
        d = jnp.dot(ds, kp, preferred_element_type=F32)
        dq = d if dq is None else dq + d
        dkp = lax.dot_general(ds, qv, _DIMS["tn"], preferred_element_type=F32)
        dvp = lax.dot_general(p.astype(BF16), dob, _DIMS["tn"], preferred_element_type=F32)
        outs.append((a, dkp, dvp))
    return dq, outs


def _attn_fwd(q, k, v, sinks, *, B, S, HQ, HK, window, scale, tq, band, name, xchg=None):
    G = HQ // HK
    nq = S // tq
    T = B * S
    has_sink = sinks is not None
    n_x = 0 if xchg is None else len(xchg[1])
    n_in = 4 if has_sink else 3

    def kern(*refs):
        xs, xd, sems = refs[n_in:n_in + n_x], refs[n_in + n_x + 2:n_in + 2 * n_x + 2], refs[n_in + 2 * n_x + 2:]
        refs = refs[:n_in] + refs[n_in + n_x:n_in + n_x + 2]
        if n_x:
            _xchg_at_ends(xchg[0], xs, xd, sems, (B, HQ, nq), True)
        if has_sink:
            q_ref, k_ref, v_ref, s_ref, o_ref, lse_ref = refs
        else:
            q_ref, k_ref, v_ref, o_ref, lse_ref = refs
        q_start = pl.program_id(2) * tq
        qv = q_ref[...]
        if has_sink:
            m0 = jnp.broadcast_to(s_ref[0][:, :1], (tq, 1))
            l0 = jnp.ones((tq, 1), F32)
        else:
            m0 = l0 = None

        def finish(o, lse):
            o_ref[...] = o.astype(o_ref.dtype)
            lse_ref[...] = jnp.broadcast_to(lse, (tq, LANES))

        if window is None:
            bidx = q_start // band
            for bb in range(S // band):
                @pl.when(bidx == bb)
                def _(bb=bb):
                    W = (bb + 1) * band
                    finish(*_tile_fwd(qv, k_ref[0:W, :], v_ref[0:W, :], q_start, 0, bb * band, scale, None, m0, l0))
        else:
            sub = min(tq, WINDOW_SUB)
            W = min(S, sub + window)
            for r in range(0, tq, sub):
                k_start = pl.multiple_of(jnp.maximum(q_start + r - window, 0), window)
                o, lse = _tile_fwd(qv[r:r + sub], k_ref[pl.ds(k_start, W), :], v_ref[pl.ds(k_start, W), :], q_start + r,
                                   k_start, 0, scale, window, None if m0 is None else m0[:sub], None if l0 is None else l0[:sub])
                o_ref[r:r + sub, :] = o.astype(o_ref.dtype)
                lse_ref[r:r + sub, :] = jnp.broadcast_to(lse, (sub, LANES))
        if n_x:
            _xchg_at_ends(xchg[0], xs, xd, sems, (B, HQ, nq), False)

    q_spec = pl.BlockSpec((tq, LANES), lambda b, h, i: (b * nq + i, h))
    kv_spec = pl.BlockSpec((S, LANES), lambda b, h, i: (b, h // G))
    in_specs = [q_spec, kv_spec, kv_spec]
    args = [q, k, v]
    if has_sink:
        in_specs.append(pl.BlockSpec((1, 1, LANES), lambda b, h, i: (h, 0, 0)))
        args.append(sinks)
    xs = [] if xchg is None else list(xchg[1])
    res = pl.pallas_call(
        kern, name=name, grid=(B, HQ, nq), in_specs=in_specs + [_ANY] * n_x, out_specs=[q_spec, q_spec] + [_ANY] * n_x,
        out_shape=[jax.ShapeDtypeStruct((T, HQ * LANES), BF16), jax.ShapeDtypeStruct((T, HQ * LANES), F32)]
        + (_xchg_out_shapes(xchg[0], xs) if n_x else []),
        scratch_shapes=_xchg_scratch(xchg[0], n_x) if n_x else [],
        compiler_params=_params(("arbitrary",) * 3 if n_x else ("parallel", "parallel", "arbitrary")),
    )(*args, *xs)
    return ((res[0], res[1]), list(res[2:])) if n_x else res


def _attn_bwd(q, k, v, o, lse, do, sinks, *, B, S, HQ, HK, window, scale, tq, band, name, xchg=None):
    G = HQ // HK
    nq = S // tq
    T = B * S
    has_sink = sinks is not None
    n_x = 0 if xchg is None else len(xchg[1])
    n_in, n_out = (7, 4) if has_sink else (6, 3)

    def kern(*refs):
        xs, xd = refs[n_in:n_in + n_x], refs[n_in + n_x + n_out:n_in + 2 * n_x + n_out]
        dk_acc, dv_acc = refs[n_in + 2 * n_x + n_out:n_in + 2 * n_x + n_out + 2]
        sems = refs[n_in + 2 * n_x + n_out + 2:]
        refs = refs[:n_in] + refs[n_in + n_x:n_in + n_x + n_out]
        if n_x:
            _xchg_at_ends(xchg[0], xs, xd, sems, (B, HK, G, nq), True)
        if has_sink:
            q_ref, k_ref, v_ref, o_ref, lse_ref, do_ref, s_ref, dq_ref, dk_ref, dv_ref, ds_ref = refs
        else:
            q_ref, k_ref, v_ref, o_ref, lse_ref, do_ref, dq_ref, dk_ref, dv_ref = refs
        gi = pl.program_id(2)
        qi = pl.program_id(3)
        q_start = qi * tq

        @pl.when(jnp.logical_and(gi == 0, qi == 0))
        def _():
            dk_acc[...] = jnp.zeros_like(dk_acc)
            dv_acc[...] = jnp.zeros_like(dv_acc)

        qv = q_ref[...]
        dof = do_ref[...].astype(F32)
        ov = o_ref[...].astype(F32)
        lse_v = lse_ref[...][:, :1]
        if window is None:
            bidx = q_start // band
            for bb in range(S // band):
                @pl.when(bidx == bb)
                def _(bb=bb):
                    W = (bb + 1) * band
                    dq, outs = _tile_bwd(qv, k_ref[0:W, :], v_ref[0:W, :], dof, ov, lse_v, q_start, 0, bb * band, scale, None)
                    dq_ref[...] = dq.astype(dq_ref.dtype)
                    for a, dkp, dvp in outs:
                        dk_acc[a:a + dkp.shape[0], :] += dkp
                        dv_acc[a:a + dvp.shape[0], :] += dvp
        else:
            sub = min(tq, WINDOW_SUB)
            W = min(S, sub + window)
            parts = []
            for r in range(0, tq, sub):
                k_start = pl.multiple_of(jnp.maximum(q_start + r - window, 0), window)
                dq, outs = _tile_bwd(qv[r:r + sub], k_ref[pl.ds(k_start, W), :], v_ref[pl.ds(k_start, W), :],
                                     dof[r:r + sub], ov[r:r + sub], lse_v[r:r + sub], q_start + r, k_start, 0, scale, window)
                dq_ref[r:r + sub, :] = dq.astype(dq_ref.dtype)
                parts.append((k_start, outs[0][1], outs[0][2]))
            for k_start, dkp, dvp in parts:
                dk_acc[pl.ds(k_start, W), :] += dkp
                dv_acc[pl.ds(k_start, W), :] += dvp

        @pl.when(jnp.logical_and(gi == G - 1, qi == nq - 1))
        def _():
            dk_ref[...] = dk_acc[...].astype(dk_ref.dtype)
            dv_ref[...] = dv_acc[...].astype(dv_ref.dtype)
        if has_sink:
            delta = jnp.sum(dof * ov, axis=-1, keepdims=True)
            sink = s_ref[0][:, :1]
            part = -jnp.sum(jnp.exp(sink - lse_v) * delta, axis=0, keepdims=True)
            part = jnp.broadcast_to(part, (1, LANES))

            @pl.when(qi == 0)
            def _():
                ds_ref[0] = part

            @pl.when(qi != 0)
            def _():
                ds_ref[0] += part
        if n_x:
            _xchg_at_ends(xchg[0], xs, xd, sems, (B, HK, G, nq), False)

    q_spec = pl.BlockSpec((tq, LANES), lambda b, hk, g, i: (b * nq + i, hk * G + g))
    kv_spec = pl.BlockSpec((S, LANES), lambda b, hk, g, i: (b, hk))
    in_specs = [q_spec, kv_spec, kv_spec, q_spec, q_spec, q_spec]
    args = [q, k, v, o, lse, do]
    out_specs = [q_spec, kv_spec, kv_spec]
    out_shape = [jax.ShapeDtypeStruct((T, HQ * LANES), BF16), jax.ShapeDtypeStruct((T, HK * LANES), BF16),
                 jax.ShapeDtypeStruct((T, HK * LANES), BF16)]
    if has_sink:
        in_specs.append(pl.BlockSpec((1, 1, LANES), lambda b, hk, g, i: (hk * G + g, 0, 0)))
        args.append(sinks)
        out_specs.append(pl.BlockSpec((1, 1, LANES), lambda b, hk, g, i: (b * HQ + hk * G + g, 0, 0)))
        out_shape.append(jax.ShapeDtypeStruct((B * HQ, 1, LANES), F32))
    xs = [] if xchg is None else list(xchg[1])
    res = pl.pallas_call(
        kern, name=name, grid=(B, HK, G, nq), in_specs=in_specs + [_ANY] * n_x, out_specs=out_specs + [_ANY] * n_x,
        out_shape=out_shape + (_xchg_out_shapes(xchg[0], xs) if n_x else []),
        scratch_shapes=[pltpu.VMEM((S, LANES), F32)] * 2 + (_xchg_scratch(xchg[0], n_x) if n_x else []),
        compiler_params=_params(("arbitrary",) * 4 if n_x else ("parallel", "parallel", "arbitrary", "arbitrary")),
    )(*args, *xs)
    main = tuple(res[:n_out]) if has_sink else (*res[:n_out], None)
    return (main, list(res[n_out:])) if n_x else main


def _rope_tables(pos_col, freq, sign, keep, name):
    T = pos_col.shape[0]
    tr = _tile(T, 1024)

    def kern(p_ref, f_ref, s_ref, k_ref, c_out, s_out):
        ang = p_ref[...] * f_ref[...]
        c_out[...] = jnp.cos(ang) * k_ref[...]
        s_out[...] = jnp.sin(ang) * s_ref[...]

    spec = pl.BlockSpec((tr, LANES), lambda i: (i, 0))
    par = pl.BlockSpec((1, LANES), lambda i: (0, 0))
    return pl.pallas_call(
        kern, name=name, grid=(T // tr,), in_specs=[pl.BlockSpec((tr, 1), lambda i: (i, 0)), par, par, par],
        out_specs=[spec, spec], out_shape=[jax.ShapeDtypeStruct((T, LANES), F32)] * 2,
        compiler_params=_params(("parallel",)),
    )(pos_col, freq, sign, keep)


def _ada_fwd(c8, ada_w, ada_b):
    L, D, N = ada_w.shape

    def kern(c_ref, w_ref, b_ref, o_ref):
        act = _silu(c_ref[...]).astype(BF16)
        o_ref[0] = jnp.dot(act, w_ref[0], preferred_element_type=F32) + b_ref[0]

    return pl.pallas_call(
        kern, name="ada_fwd", grid=(L,),
        in_specs=[pl.BlockSpec((8, D), lambda l: (0, 0)), pl.BlockSpec((1, D, N), lambda l: (l, 0, 0)),
                  pl.BlockSpec((1, 1, N), lambda l: (l, 0, 0))],
        out_specs=pl.BlockSpec((1, 8, N), lambda l: (l, 0, 0)),
        out_shape=jax.ShapeDtypeStruct((L, 8, N), F32), compiler_params=_params(("parallel",)),
    )(c8, ada_w, ada_b)


def _ada_bwd(c8, dmod):
    L, _, N = dmod.shape
    D = c8.shape[1]

    def kern(c_ref, d_ref, gw_ref):
        act = _silu(c_ref[...]).astype(BF16)
        gw_ref[0] = lax.dot_general(act, d_ref[0].astype(BF16), _DIMS["tn"], preferred_element_type=F32)

    return pl.pallas_call(
        kern, name="ada_bwd", grid=(L,),
        in_specs=[pl.BlockSpec((8, D), lambda l: (0, 0)), pl.BlockSpec((1, 8, N), lambda l: (l, 0, 0))],
        out_specs=pl.BlockSpec((1, D, N), lambda l: (l, 0, 0)),
        out_shape=jax.ShapeDtypeStruct((L, D, N), F32), compiler_params=_params(("parallel",)),
    )(c8, dmod)


def _sum_examples(d):
    L, _, N = d.shape

    def kern(d_ref, o_ref):
        o_ref[...] = jnp.sum(d_ref[...], axis=1, keepdims=True)

    vm = pl.BlockSpec(memory_space=pltpu.VMEM)
    return pl.pallas_call(kern, name="sum_examples", in_specs=[vm], out_specs=vm,
                          out_shape=jax.ShapeDtypeStruct((L, 1, N), F32))(d)


def _adamw(w, ga, gb, m, v, name):
    rows, cols = w.shape
    tr = _row_div(rows, 256)
    two = gb is not None

    def kern(*refs):
        if two:
            w_ref, ga_ref, gb_ref, m_ref, v_ref, g_out, d_out, m_out, v_out = refs
            gv = ga_ref[...] + gb_ref[...]
        else:
            w_ref, ga_ref, m_ref, v_ref, g_out, d_out, m_out, v_out = refs
            gv = ga_ref[...]
        mn = ADAM_B1 * m_ref[...] + (1.0 - ADAM_B1) * gv
        vn = ADAM_B2 * v_ref[...] + (1.0 - ADAM_B2) * jnp.square(gv)
        m_hat = mn / (1.0 - ADAM_B1 ** ADAM_STEP)
        v_hat = vn / (1.0 - ADAM_B2 ** ADAM_STEP)
        g_out[...] = gv
        d_out[...] = -ADAM_LR * (m_hat / (jnp.sqrt(v_hat) + ADAM_EPS) + ADAM_WD * w_ref[...])
        m_out[...] = mn
        v_out[...] = vn

    spec = pl.BlockSpec((tr, cols), lambda i: (i, 0))
    args = [w, ga, gb, m, v] if two else [w, ga, m, v]
    return pl.pallas_call(
        kern, name=name, grid=(rows // tr,), in_specs=[spec] * len(args), out_specs=[spec] * 4,
        out_shape=[jax.ShapeDtypeStruct((rows, cols), F32)] * 4, compiler_params=_params(("parallel",)),
    )(*args)


def _sum_slots(x, name):
    n, rows, cols = x.shape
    tr = _row_div(rows, 256)

    def kern(x_ref, o_ref):
        acc = x_ref[0].astype(F32)
        for j in range(1, n):
            acc = acc + x_ref[j].astype(F32)
        o_ref[...] = acc

    return pl.pallas_call(
        kern, name=name, grid=(rows // tr,), in_specs=[pl.BlockSpec((n, tr, cols), lambda i: (0, i, 0))],
        out_specs=pl.BlockSpec((tr, cols), lambda i: (i, 0)), out_shape=jax.ShapeDtypeStruct((rows, cols), F32),
        compiler_params=_params(("parallel",)),
    )(x)


def _all_reduce_small(blob, name):
    R, C = blob.shape

    def kern(src, out, pair, chips, send_sems, recv_sems):
        x, y, c = lax.axis_index("x"), lax.axis_index("y"), lax.axis_index("c")
        me = 2 * x + y
        to_sibling = pltpu.make_async_remote_copy(
            src_ref=src, dst_ref=pair, send_sem=send_sems.at[0], recv_sem=recv_sems.at[0],
            device_id=(x, y, 1 - c), device_id_type=_MESH)
        to_sibling.start()
        to_sibling.wait()
        chips[me] = src[...] + pair[...]
        peers = [(1 - x, y), (x, 1 - y), (1 - x, 1 - y)]
        copies = [pltpu.make_async_remote_copy(
            src_ref=chips.at[me], dst_ref=chips.at[me], send_sem=send_sems.at[1 + kk], recv_sem=recv_sems.at[1 + kk],
            device_id=(px, py, c), device_id_type=_MESH) for kk, (px, py) in enumerate(peers)]
        for cp in copies:
            cp.start()
        for kk, (px, py) in enumerate(peers):
            pltpu.make_async_remote_copy(
                src_ref=chips.at[me], dst_ref=chips.at[2 * px + py], send_sem=send_sems.at[1 + kk],
                recv_sem=recv_sems.at[1 + kk], device_id=(px, py, c), device_id_type=_MESH).wait_recv()
        for cp in copies:
            cp.wait_send()
        acc = chips[0]
        for j in range(1, N_CHIPS):
            acc = acc + chips[j]
        out[...] = acc

    vm = pl.BlockSpec(memory_space=pltpu.VMEM)
    return pl.pallas_call(
        kern, name=name, in_specs=[vm], out_specs=vm, out_shape=jax.ShapeDtypeStruct((R, C), F32),
        scratch_shapes=[pltpu.VMEM((R, C), F32), pltpu.VMEM((N_CHIPS, R, C), F32), pltpu.SemaphoreType.DMA((4,)),
                        pltpu.SemaphoreType.DMA((4,))],
        compiler_params=pltpu.CompilerParams(vmem_limit_bytes=VMEM_LIMIT),
    )(blob)


def _row_tile(S, wide=False):
    for t in ((256,) if wide else (512, 256)):
        if S % t == 0:
            return t
    return 128


def _attn_tiles(S):
    return min(S, 512), min(S, 256), min(S, 512)


def _hosted(fn, host, got, kind, name, *args, **kw):
    if host is not None and host.get(name):
        res, xs = fn(*args, name=name, xchg=(kind, host[name]), **kw)
        got[name] = xs
        return res
    return fn(*args, name=name, **kw)


def _mm(host, got, kind, a, b, mode, out_dtype, name, **kw):
    return _hosted(_matmul, host, got, kind, name, a, b, mode, out_dtype, **kw)


def _merge_late(w, late, got, name):
    if not late or name not in late:
        return w
    return {**w, **_large_operands(dict(zip(late[name], got[name])))}


def _sgu_rows(S):
    return 4 * C_CHUNK if S % (4 * C_CHUNK) == 0 else C_CHUNK


def _layer_fwd(xin, prev, mods, w, tabs, B, S, host=None, late=None):
    T = B * S
    tr = _row_tile(S)
    sh1, sc1, g1, sh2, sc2, g2 = mods
    ca, sa, cb, sb = tabs
    sv = {}
    got = {}
    if prev is None:
        def body(rv, ev, pv):
            return [_f_norm_mod(rv[0], pv[0], ev[0], ev[1])], [], []
        (h,), _, _ = _rowcall(body, name="f_norm_mod", T=T, S=S, tr=tr, rows=[(xin, D_MODEL, 0)], exs=[sc1, sh1],
                              pars=[w["norm1_w"]], row_outs=[(D_MODEL, BF16, D_MODEL, 0)])
        x = xin
    else:
        x1p, dp, g2p = prev

        def body(rv, ev, pv):
            xn, hh = _f_resid_norm_mod(rv[0], rv[1], ev[0], pv[0], ev[1], ev[2])
            return [xn, hh], [], []
        (x, h), _, _ = _rowcall(body, name="f_resid_norm_mod1", T=T, S=S, tr=tr,
                                rows=[(x1p, D_MODEL, 0), (dp, D_MODEL, 0)], exs=[g2p, sc1, sh1], pars=[w["norm1_w"]],
                                row_outs=[(D_MODEL, F32, D_MODEL, 0), (D_MODEL, BF16, D_MODEL, 0)])
    sv["x"], sv["h"] = x, h
    proj = _mm(host, got, "gather", h, w["w_in"], "nn", BF16, "mm_in")
    sv["proj"] = proj
    w = _merge_late(w, late, got, "mm_in")

    def body(rv, ev, pv):
        outs = _f_mixprep(rv[0], rv[1], rv[2], rv[3], rv[4], pv[0], pv[1])
        return list(outs), [], []
    (qa, ka, va, cqn, ckvn, kr), _, _ = _rowcall(
        body, name="f_mixprep", T=T, S=S, tr=tr,
        rows=[(proj, P_MIX, 0), (ca, LANES, 0), (sa, LANES, 0), (cb, LANES, 0), (sb, LANES, 0)],
        pars=[w["b_q_norm_w"], w["b_kv_norm_w"]],
        row_outs=[(768, BF16, 768, 0), (256, BF16, 256, 0), (256, BF16, 256, 0), (384, BF16, 384, 0),
                  (256, BF16, 256, 0), (LANES, F32, LANES, 0)])
    sv.update(qa=qa, ka=ka, va=va, cqn=cqn, ckvn=ckvn)
    q = _matmul(cqn, w["b_w_uq"], "nn", BF16, "mm_uq")
    kv = _matmul(ckvn, w["b_w_ukv"], "nn", BF16, "mm_ukv")

    def body(rv, ev, pv):
        Q, K, V = _f_mlaprep(rv[0], rv[1], rv[2], rv[3], rv[4])
        return [Q, K, V], [], []
    (Q, K, V), _, _ = _rowcall(
        body, name="f_mlaprep", T=T, S=S, tr=tr,
        rows=[(q, 768, 0), (kv, 1536, 0), (kr, LANES, 0), (cb, LANES, 0), (sb, LANES, 0)],
        row_outs=[(768, BF16, 768, 0)] * 3)
    sv.update(Q=Q, K=K, V=V)
    ta, tb, band = _attn_tiles(S)
    oa, lse_a = _hosted(_attn_fwd, host, got, "gather", "attn_a_fwd", qa, ka, va, w["sinks"], B=B, S=S, HQ=A_Q_HEADS,
                        HK=A_KV_HEADS, window=A_WINDOW, scale=HEAD_DIM ** -0.5, tq=ta, band=None)
    ob, lse_b = _hosted(_attn_fwd, host, got, "gather", "attn_b_fwd", Q, K, V, None, B=B, S=S, HQ=B_HEADS, HK=B_HEADS,
                        window=None, scale=(B_NOPE + B_ROPE) ** -0.5, tq=tb, band=band)
    sv.update(oa=oa, lse_a=lse_a, ob=ob, lse_b=lse_b)
    w = _merge_late(w, late, got, "attn_a_fwd")
    ts = _sgu_rows(S)

    def body(rv, ev, pv):
        outs = [_f_sgu(rv[0][r:r + C_CHUNK], rv[1][r:r + C_CHUNK], pv[0], pv[1], pv[2], pv[3])
                for r in range(0, ts, C_CHUNK)]
        return [jnp.concatenate(outs, axis=0)], [], []
    (yc,), _, _ = _rowcall(body, name="f_sgu", T=T, S=S, tr=ts,
                           rows=[(proj, C_WIDTH, P_CU // C_WIDTH), (proj, C_WIDTH, P_CV // C_WIDTH)],
                           pars=[w["c_ln_w"], w["c_ln_b"], w["c_w_s"], w["c_b_col"]],
                           row_outs=[(C_WIDTH, F32, C_WIDTH, 0)])
    sv["yc"] = yc

    def body(rv, ev, pv):
        return [_f_outnorm(rv[0], rv[1], rv[2], pv[0])], [], []
    (y,), _, _ = _rowcall(body, name="f_outnorm", T=T, S=S, tr=tr,
                          rows=[(oa, 768, 0), (ob, 768, 0), (yc, C_WIDTH, 0)], pars=[w["out_norm_w"]],
                          row_outs=[(D_MODEL, BF16, D_MODEL, 0)])
    sv["y"] = y
    o = _mm(host, got, "gather", y, w["w_out"], "nn", BF16, "mm_out")
    sv["o"] = o

    def body(rv, ev, pv):
        xn, hh = _f_resid_norm_mod(rv[0], rv[1], ev[0], pv[0], ev[1], ev[2])
        return [xn, hh], [], []
    (x1, h2), _, _ = _rowcall(body, name="f_resid_norm_mod2", T=T, S=S, tr=tr,
                              rows=[(x, D_MODEL, 0), (o, D_MODEL, 0)], exs=[g1, sc2, sh2], pars=[w["norm2_w"]],
                              row_outs=[(D_MODEL, F32, D_MODEL, 0), (D_MODEL, BF16, D_MODEL, 0)])
    sv["h2"] = h2
    gu = _mm(host, got, "gather", h2, w["w_gate_up"], "nn", BF16, "mm_gate_up", b_chunks=True)
    sv["gu"] = gu

    def body(rv, ev, pv):
        return [_f_swiglu(rv[0], rv[1])], [], []
    (act,), _, _ = _rowcall(body, name="f_swiglu", T=T, S=S, tr=_row_tile(S, wide=True),
                            rows=[(gu, FFN_HIDDEN, 0), (gu, FFN_HIDDEN, 1)], row_outs=[(FFN_HIDDEN, BF16, FFN_HIDDEN, 0)])
    sv["act"] = act
    d = _mm(host, got, "gather", act, w["w_down"], "nn", BF16, "mm_down")
    sv["x1"], sv["d"], sv["w"] = x1, d, w
    return (x1, d, g2), sv, got


def _final(x1, d, g2, fw, target, B, S):
    T = B * S
    tr = _row_tile(S)

    def loss_fn(x1v, dv, gv, wv, tv):
        yf = _rms(x1v + gv * dv, wv)
        return 0.5 * jnp.sum(jnp.mean(jnp.square(yf - tv), axis=-1))

    def body(rv, ev, pv):
        x1v, dv, tv = rv
        val, vjp = jax.vjp(lambda a, b_, g, ww: loss_fn(a, b_, g, ww, tv), x1v, dv, ev[0], pv[0])
        dx1, dd, dg, dw = vjp(jnp.ones((), F32))
        return [dx1, dd], [dg], [dw, jnp.full((1, LANES), val, F32)]
    (dx1, dd), (dg2,), (dfw, loss) = _rowcall(
        body, name="final_loss", T=T, S=S, tr=tr, rows=[(x1, D_MODEL, 0), (d, D_MODEL, 0), (target, D_MODEL, 0)],
        exs=[g2], pars=[fw], row_outs=[(D_MODEL, F32, D_MODEL, 0), (D_MODEL, BF16, D_MODEL, 0)],
        ex_outs=[D_MODEL], par_outs=[(1, D_MODEL), (1, LANES)])
    return loss, dx1, dd, dg2, dfw


def _layer_bwd(sv, prev, mods, tabs, dx1, dd, B, S, host=None, own_early=False):
    w = sv["w"]
    T = B * S
    tr = _row_tile(S)
    sh1, sc1, g1, sh2, sc2, g2 = mods
    ca, sa, cb, sb = tabs
    gr = {}
    got = {}
    dact = _mm(host, got, "a2a", dd, w["w_down"], "nt", BF16, "mm_down_dx")
    gr["w_down"] = _matmul(sv["act"], dd, "tn", BF16, "mm_down_dw")

    def body(rv, ev, pv):
        _, vjp = jax.vjp(_f_swiglu, rv[0], rv[1])
        dgate, dup = vjp(rv[2])
        return [[dgate, dup]], [], []
    (dgu,), _, _ = _rowcall(body, name="b_swiglu", T=T, S=S, tr=_row_tile(S, wide=True),
                            rows=[(sv["gu"], FFN_HIDDEN, 0), (sv["gu"], FFN_HIDDEN, 1), (dact, FFN_HIDDEN, 0)],
                            row_outs=[(2 * FFN_HIDDEN, BF16, 2 * FFN_HIDDEN, 0)])
    dh2 = _mm(host, got, "a2a", dgu, w["w_gate_up"], "nt", BF16, "mm_gate_up_dx", b_chunks=True)
    gr["w_gate_up"] = _mm(host, got, "a2a", sv["h2"], dgu, "tn", BF16, "mm_gate_up_dw", out_chunks=True)

    def body(rv, ev, pv):
        xa, delta, dh, dxn = rv
        _, vjp = jax.vjp(_f_resid_norm_mod, xa, delta, ev[0], pv[0], ev[1], ev[2])
        dxa, ddelta, dg, dw, dsc, dsh = vjp((dxn, dh))
        return [dxa, ddelta], [dg, dsc, dsh], [dw]
    (dx, do), (dg1, dsc2, dsh2), (gr["norm2_w"],) = _rowcall(
        body, name="b_resid_norm_mod2", T=T, S=S, tr=tr,
        rows=[(sv["x"], D_MODEL, 0), (sv["o"], D_MODEL, 0), (dh2, D_MODEL, 0), (dx1, D_MODEL, 0)],
        exs=[g1, sc2, sh2], pars=[w["norm2_w"]],
        row_outs=[(D_MODEL, F32, D_MODEL, 0), (D_MODEL, BF16, D_MODEL, 0)], ex_outs=[D_MODEL] * 3,
        par_outs=[(1, D_MODEL)])
    dy = _matmul(do, w["w_out"], "nt", BF16, "mm_out_dx")
    gr["w_out"] = _matmul(sv["y"], do, "tn", BF16, "mm_out_dw")

    def body(rv, ev, pv):
        _, vjp = jax.vjp(_f_outnorm, rv[0], rv[1], rv[2], pv[0])
        doa, dob, dyc, dgw = vjp(rv[3])
        return [doa, dob, dyc], [], [dgw]
    (doa, dob, dyc), _, (gr["out_norm_w"],) = _rowcall(
        body, name="b_outnorm", T=T, S=S, tr=tr,
        rows=[(sv["oa"], 768, 0), (sv["ob"], 768, 0), (sv["yc"], C_WIDTH, 0), (dy, D_MODEL, 0)], pars=[w["out_norm_w"]],
        row_outs=[(768, BF16, 768, 0), (768, BF16, 768, 0), (C_WIDTH, F32, C_WIDTH, 0)], par_outs=[(1, D_MODEL)])

    ta, tb, band = _attn_tiles(S)
    if own_early:
        host = dict(host)
        host["attn_b_bwd"] = list(host.get("attn_b_bwd", ())) + [
            gr["w_gate_up"], gr["w_down"].reshape(N_CHIPS, FFN_HIDDEN // N_CHIPS, D_MODEL)]
    dQ, dK, dV, _ = _hosted(_attn_bwd, host, got, "a2a", "attn_b_bwd", sv["Q"], sv["K"], sv["V"], sv["ob"], sv["lse_b"],
                            dob, None, B=B, S=S, HQ=B_HEADS, HK=B_HEADS, window=None,
                            scale=(B_NOPE + B_ROPE) ** -0.5, tq=tb, band=band)
    dqa, dka, dva, dsink = _hosted(_attn_bwd, host, got, "a2a", "attn_a_bwd", sv["qa"], sv["ka"], sv["va"], sv["oa"],
                                   sv["lse_a"], doa, w["sinks"], B=B, S=S, HQ=A_Q_HEADS, HK=A_KV_HEADS,
                                   window=A_WINDOW, scale=HEAD_DIM ** -0.5, tq=ta, band=None)
    gr["sinks"] = dsink

    def body(rv, ev, pv):
        dQv, dKv, dVv, cbv, sbv = rv
        dq = [_rope_bwd((cbv, sbv), p)[0] for p in _heads(dQv, B_HEADS)]
        dkr = None
        for p in _heads(dKv, B_HEADS):
            dkr = p if dkr is None else dkr + p
        return [dq, [dKv, dVv], dkr], [], []
    (dq, dkv, dkr), _, _ = _rowcall(
        body, name="b_mlaprep", T=T, S=S, tr=tr,
        rows=[(dQ, 768, 0), (dK, 768, 0), (dV, 768, 0), (cb, LANES, 0), (sb, LANES, 0)],
        row_outs=[(768, BF16, 768, 0), (1536, BF16, 1536, 0), (LANES, F32, LANES, 0)])
    dcqn = _matmul(dq, w["b_w_uq"], "nt", BF16, "mm_uq_dx")
    gr["b_w_uq"] = _matmul(sv["cqn"], dq, "tn", BF16, "mm_uq_dw")
    dckvn = _matmul(dkv, w["b_w_ukv"], "nt", BF16, "mm_ukv_dx")
    gr["b_w_ukv"] = _matmul(sv["ckvn"], dkv, "tn", BF16, "mm_ukv_dw")

    def body(rv, ev, pv):
        proj, cav, sav, cbv, sbv, dqa_, dka_, dva_, dcqn_, dckvn_, dkr_ = rv
        _, vjp = jax.vjp(lambda p, a, b_: _f_mixprep(p, cav, sav, cbv, sbv, a, b_), proj, pv[0], pv[1])
        dproj, dqn, dkvn = vjp((dqa_, dka_, dva_, dcqn_, dckvn_, dkr_))
        return [[dproj, jnp.zeros((dproj.shape[0], P_CU - P_MIX), F32)]], [], [dqn, dkvn]
    (dproj,), _, (gr["b_q_norm_w"], gr["b_kv_norm_w"]) = _rowcall(
        body, name="b_mixprep", T=T, S=S, tr=tr,
        rows=[(sv["proj"], P_MIX, 0), (ca, LANES, 0), (sa, LANES, 0), (cb, LANES, 0), (sb, LANES, 0),
              (dqa, 768, 0), (dka, 256, 0), (dva, 256, 0), (dcqn, 384, 0), (dckvn, 256, 0), (dkr, LANES, 0)],
        pars=[w["b_q_norm_w"], w["b_kv_norm_w"]], row_outs=[(P_END, BF16, P_CU, 0)],
        par_outs=[(1, B_Q_RANK), (1, B_KV_RANK)])

    ts = _sgu_rows(S)

    def body(rv, ev, pv):
        cu, cv, dycv, _ = rv
        dcus, dcvs, acc = [], [], None
        for r in range(0, ts, C_CHUNK):
            _, vjp = jax.vjp(_f_sgu, cu[r:r + C_CHUNK], cv[r:r + C_CHUNK], pv[0], pv[1], pv[2], pv[3])
            dcu, dcv, *dpar = vjp(dycv[r:r + C_CHUNK])
            dcus.append(dcu)
            dcvs.append(dcv)
            acc = dpar if acc is None else [a + b_ for a, b_ in zip(acc, dpar)]
        return [[jnp.concatenate(dcus, axis=0), jnp.concatenate(dcvs, axis=0)]], [], acc
    (dproj,), _, (gr["c_ln_w"], gr["c_ln_b"], gr["c_w_s"], gr["c_b_col"]) = _rowcall(
        body, name="b_sgu", T=T, S=S, tr=ts,
        rows=[(sv["proj"], C_WIDTH, P_CU // C_WIDTH), (sv["proj"], C_WIDTH, P_CV // C_WIDTH), (dyc, C_WIDTH, 0),
              (dproj, 2 * C_WIDTH, P_CU // (2 * C_WIDTH))],
        pars=[w["c_ln_w"], w["c_ln_b"], w["c_w_s"], w["c_b_col"]],
        row_outs=[(P_END, BF16, 2 * C_WIDTH, P_CU // (2 * C_WIDTH))],
        par_outs=[(1, C_WIDTH), (1, C_WIDTH), (C_GROUPS, C_CHUNK, C_CHUNK), (C_GROUPS, C_CHUNK, 1)],
        aliases={3: 0})
    dh = _mm(host, got, "a2a", dproj, w["w_in"], "nt", BF16, "mm_in_dx")
    gr["w_in"] = _matmul(sv["h"], dproj, "tn", BF16, "mm_in_dw")

    if prev is None:
        def body(rv, ev, pv):
            xv, dhv, dxd = rv
            _, vjp = jax.vjp(_f_norm_mod, xv, pv[0], ev[0], ev[1])
            dxa, dw, dsc, dsh = vjp(dhv)
            return [dxa + dxd], [dsc, dsh], [dw]
        (dxin,), (dsc1, dsh1), (gr["norm1_w"],) = _rowcall(
            body, name="b_norm_mod", T=T, S=S, tr=tr, rows=[(sv["x"], D_MODEL, 0), (dh, D_MODEL, 0), (dx, D_MODEL, 0)],
            exs=[sc1, sh1], pars=[w["norm1_w"]], row_outs=[(D_MODEL, F32, D_MODEL, 0)], ex_outs=[D_MODEL] * 2,
            par_outs=[(1, D_MODEL)])
        nxt = (dxin, None, None)
    else:
        x1p, dp, g2p = prev

        def body(rv, ev, pv):
            xa, delta, dhv, dxn = rv
            _, vjp = jax.vjp(_f_resid_norm_mod, xa, delta, ev[0], pv[0], ev[1], ev[2])
            dxa, ddelta, dg, dw, dsc, dsh = vjp((dxn, dhv))
            return [dxa, ddelta], [dg, dsc, dsh], [dw]
        (dx1p, ddp), (dg2p, dsc1, dsh1), (gr["norm1_w"],) = _rowcall(
            body, name="b_resid_norm_mod1", T=T, S=S, tr=tr,
            rows=[(x1p, D_MODEL, 0), (dp, D_MODEL, 0), (dh, D_MODEL, 0), (dx, D_MODEL, 0)],
            exs=[g2p, sc1, sh1], pars=[w["norm1_w"]],
            row_outs=[(D_MODEL, F32, D_MODEL, 0), (D_MODEL, BF16, D_MODEL, 0)], ex_outs=[D_MODEL] * 3,
            par_outs=[(1, D_MODEL)])
        nxt = (dx1p, ddp, dg2p)
    return gr, (dsh1, dsc1, dg1, dsh2, dsc2), nxt, got


def _lane_table(lanes_neg, lanes_pos, inv):
    freq = np.zeros((LANES,), np.int64) - 1
    sign = np.zeros((1, LANES), np.float32)
    n = len(lanes_neg)
    freq[lanes_neg] = np.arange(n)
    freq[lanes_pos] = np.arange(n)
    sign[0, lanes_neg] = -1.0
    sign[0, lanes_pos] = 1.0
    return _select_axis(inv, freq, 0).reshape(1, LANES), jnp.asarray(sign)


SHARDED = ("ada_w", "w_in", "b_w_uq", "b_w_ukv", "w_out", "w_gate_up", "w_down")
ROW_SHARDED = ("w_out", "w_down")
SMALL = ("ada_b", "norm1_w", "a_sinks", "b_q_norm_w", "b_kv_norm_w", "c_ln_w", "c_ln_b", "c_w_s", "c_b_s",
         "out_norm_w", "norm2_w", "final_norm_w")
FWD_HOST = {"attn_b_fwd": ("w_gate_up", "w_down"), "attn_a_fwd": ("w_in", "w_out", "b_w_uq", "b_w_ukv")}
FWD_HOST_FIRST = {"attn_b_fwd": ("w_gate_up", "w_down"), "mm_gate_up": ("w_in", "w_out", "b_w_uq", "b_w_ukv")}
BWD_HOST = {"attn_a_bwd": ("w_in", "w_out", "b_w_uq", "b_w_ukv"), "attn_b_bwd": ("w_gate_up", "w_down")}
FIRST_LATE = {"mm_in": ("b_w_uq", "b_w_ukv", "w_out"), "attn_a_fwd": ("w_gate_up", "w_down")}
EXPOSED = ("w_in", "b_w_uq", "b_w_ukv", "w_out")
OWN_LAYER = ("w_gate_up", "w_down")
EXCHANGED = SHARDED[1:]
N_ADA = N_MOD * D_MODEL // N_CHIPS
assert not set(FIRST_LATE) & set(FWD_HOST_FIRST)


def _from_host(table, got):
    return {k: got[name][i] for name, ks in table.items() for i, k in enumerate(ks)}


def _to_host(table, arrays):
    return {name: [arrays[k] for k in ks] for name, ks in table.items()}


def _join_cols(g):
    return jnp.concatenate([g[j] for j in range(N_CHIPS)], axis=1)


def _split_cols(g):
    n = g.shape[1] // N_CHIPS
    return jnp.stack([g[:, j * n:(j + 1) * n] for j in range(N_CHIPS)])


def _layer_weights(G, small, l):
    D = D_MODEL
    return {
        **_large_operands(G),
        "norm1_w": small["norm1_w"][l].reshape(1, D),
        "sinks": jnp.broadcast_to(small["a_sinks"][l].reshape(A_Q_HEADS, 1, 1), (A_Q_HEADS, 1, LANES)),
        "b_q_norm_w": small["b_q_norm_w"][l].reshape(1, B_Q_RANK),
        "b_kv_norm_w": small["b_kv_norm_w"][l].reshape(1, B_KV_RANK),
        "c_ln_w": small["c_ln_w"][l].reshape(1, C_WIDTH), "c_ln_b": small["c_ln_b"][l].reshape(1, C_WIDTH),
        "c_w_s": small["c_w_s"][l], "c_b_col": small["c_b_s"][l].reshape(C_GROUPS, C_CHUNK, 1),
        "out_norm_w": small["out_norm_w"][l].reshape(1, D), "norm2_w": small["norm2_w"][l].reshape(1, D),
    }


def _large_operands(G):
    D = D_MODEL
    make = {
        "w_in": lambda g: _pad_axis(_join_cols(g), _map_w_in(), 1),
        "b_w_uq": lambda g: _pad_axis(_join_cols(g), _map_w_uq(), 1),
        "b_w_ukv": lambda g: _pad_axis(_join_cols(g), _map_w_ukv(), 1),
        "w_out": lambda g: g.reshape(D, D), "w_gate_up": lambda g: g, "w_down": lambda g: g.reshape(FFN_HIDDEN, D),
    }
    return {k: make[k](g) for k, g in G.items()}


def _send_buffers(gr):
    D = D_MODEL
    return {
        "w_gate_up": gr["w_gate_up"], "w_down": gr["w_down"].reshape(N_CHIPS, FFN_HIDDEN // N_CHIPS, D),
        "w_out": gr["w_out"].reshape(N_CHIPS, D // N_CHIPS, D),
        "w_in": _split_cols(_unpad_axis(gr["w_in"], _map_w_in(), IN_COLS, 1)),
        "b_w_uq": _split_cols(_unpad_axis(gr["b_w_uq"], _map_w_uq(), B_HEADS * (B_NOPE + B_ROPE), 1)),
        "b_w_ukv": _split_cols(_unpad_axis(gr["b_w_ukv"], _map_w_ukv(), B_HEADS * (B_NOPE + B_V), 1)),
    }


def _small_grads(gr, B):
    D = D_MODEL
    return {
        "norm1_w": gr["norm1_w"].reshape(D),
        "a_sinks": gr["sinks"][:, 0, 0].reshape(B, A_Q_HEADS).sum(axis=0),
        "b_q_norm_w": gr["b_q_norm_w"].reshape(B_Q_RANK), "b_kv_norm_w": gr["b_kv_norm_w"].reshape(B_KV_RANK),
        "c_ln_w": gr["c_ln_w"].reshape(C_WIDTH), "c_ln_b": gr["c_ln_b"].reshape(C_WIDTH), "c_w_s": gr["c_w_s"],
        "c_b_s": gr["c_b_col"].reshape(C_GROUPS, C_CHUNK),
        "out_norm_w": gr["out_norm_w"].reshape(D), "norm2_w": gr["norm2_w"].reshape(D),
    }


def _step(x, c, positions, target, small, shard_of, ada=None, gathered=None):
    dist = gathered is None
    B, S, D = x.shape
    T = B * S
    xt = x.reshape(T, D)
    tgt = target.reshape(T, D)
    pos_col = positions.astype(F32).reshape(T, 1)
    inv_a = 1.0 / (ROPE_THETA ** (jnp.arange(0, HEAD_DIM, 2, dtype=F32) / HEAD_DIM))
    inv_b = 1.0 / (ROPE_THETA ** (jnp.arange(0, B_ROPE, 2, dtype=F32) / B_ROPE))
    fa, sga = _lane_table(np.arange(32), 64 + np.arange(32), inv_a)
    fb, sgb = _lane_table(48 + np.arange(16), 112 + np.arange(16), inv_b)
    ca, sa = _rope_tables(pos_col, fa, sga, jnp.abs(sga), "rope_a")
    cb, sb = _rope_tables(pos_col, fb, sgb, jnp.ones_like(sgb), "rope_b")
    tabs = (ca, sa, cb, sb)
    ada_b = small["ada_b"]
    if dist:
        assert N_CHIPS * B == 8
        me = 2 * lax.axis_index("x") + lax.axis_index("y")
        first = shard_of(0)
        c_all, w_in_first = _xchg_call("gather", [c, first["w_in"]], "gather_first")
        c8 = c_all.reshape(N_CHIPS * B, D)
        mine = lax.dynamic_slice_in_dim(ada_b, me * N_ADA, N_ADA, axis=1).reshape(DEPTH, 1, N_ADA)
        part = _ada_fwd(c8, ada, mine)
        part = part.reshape(DEPTH, N_CHIPS, B, N_ADA).transpose(1, 0, 2, 3).reshape(N_CHIPS, DEPTH * B, N_ADA)
        (back,) = _xchg_call("a2a", [part], "mod_exchange")
        mod_all = jnp.concatenate([back[j].reshape(DEPTH, B, N_ADA) for j in range(N_CHIPS)], axis=-1)
        G = {"w_in": w_in_first}
    else:
        c8 = jnp.zeros((8, D), F32).at[:B].set(c)
        cols = [(jnp.stack([gathered[l]["ada_w"][j] for l in range(DEPTH)]),
                 ada_b[:, j * N_ADA:(j + 1) * N_ADA].reshape(DEPTH, 1, N_ADA)) for j in range(N_CHIPS)]
        mod_all = jnp.concatenate([_ada_fwd(c8, wj, bj)[:, :B] for wj, bj in cols], axis=-1)
        G = {k: v for k, v in gathered[0].items() if k != "ada_w"}
    saved, prevs, modss = [], [], []
    prev = None
    for l in range(DEPTH):
        w = _layer_weights(G, small, l)
        mods = tuple(mod_all[l, :, i * D:(i + 1) * D].reshape(B, 1, D) for i in range(N_MOD))
        more = l + 1 < DEPTH
        table = FWD_HOST_FIRST if l == 0 else FWD_HOST
        host = _to_host(table, shard_of(l + 1)) if dist and more else {}
        late = None
        if dist and l == 0:
            late = FIRST_LATE
            for name, ks in late.items():
                host[name] = [first[k] for k in ks] + host.get(name, [])
        prevs.append(prev)
        modss.append(mods)
        prev, sv, got = _layer_fwd(xt, prev, mods, w, tabs, B, S, host, late)
        saved.append(sv)
        if more:
            G = _from_host(table, got) if dist else {k: v for k, v in gathered[l + 1].items() if k != "ada_w"}
    x1, d, g2 = prev
    loss, dx1, dd, dg2, dfw = _final(x1, d, g2, small["final_norm_w"].reshape(1, D), tgt, B, S)

    landed = [None] * DEPTH
    smalls = [None] * DEPTH
    dmods = [None] * DEPTH
    pending = None
    for l in reversed(range(DEPTH)):
        host = _to_host(BWD_HOST, pending) if dist and pending is not None else None
        early = dist and l == 0 and host is not None
        gr, (dsh1, dsc1, dg1, dsh2, dsc2), nxt, got = _layer_bwd(saved[l], prevs[l], modss[l], tabs, dx1, dd, B, S, host, early)
        if pending is not None:
            landed[l + 1] = _from_host(BWD_HOST, got) if dist else pending
        dmods[l] = jnp.concatenate([dsh1, dsc1, dg1, dsh2, dsc2, dg2], axis=-1).reshape(B, N_MOD * D)
        pending = _send_buffers(gr)
        smalls[l] = _small_grads(gr, B)
        dx1, dd, dg2 = nxt
    dmod_all = jnp.stack(dmods)
    by_chip = [dmod_all[:, :, j * N_ADA:(j + 1) * N_ADA] for j in range(N_CHIPS)]
    if dist:
        last = EXPOSED if early else EXCHANGED
        send = jnp.stack([p.reshape(DEPTH * B, N_ADA) for p in by_chip])
        *res, back = _xchg_call("a2a", [pending[k] for k in last] + [send], "grad_exchange_last")
        landed[0] = dict(zip(last, res))
        if early:
            landed[0].update(zip(OWN_LAYER, got["attn_b_bwd"][-len(OWN_LAYER):]))
        dmod8 = back.reshape(N_CHIPS, DEPTH, B, N_ADA).transpose(1, 0, 2, 3).reshape(DEPTH, N_CHIPS * B, N_ADA)
        ada_g = _ada_bwd(c8, dmod8)
    else:
        landed[0] = pending
        ada_g = jnp.stack([_ada_bwd(c8, jnp.zeros((DEPTH, 8, N_ADA), F32).at[:, :B].set(p)) for p in by_chip])
    small_g = {k: jnp.stack([smalls[l][k] for l in range(DEPTH)]) for k in SMALL if k not in ("final_norm_w", "ada_b")}
    small_g["ada_b"] = _sum_examples(dmod_all).reshape(DEPTH, N_MOD * D)
    small_g["final_norm_w"] = dfw.reshape(D)
    return loss[0, 0], dx1.reshape(B, S, D), landed, small_g, ada_g


def _pack(arrs, cols, mult):
    flat = jnp.concatenate([a.reshape(-1) for a in arrs])
    n = flat.shape[0]
    rows = -(-n // cols)
    rows = -(-rows // mult) * mult
    return jnp.pad(flat, (0, rows * cols - n)).reshape(rows, cols)


def _unpack(blob, shapes):
    flat = blob.reshape(-1)
    out, off = [], 0
    for s in shapes:
        n = int(np.prod(s))
        out.append(flat[off:off + n].reshape(s))
        off += n
    return out


def kernel(x, c, positions, ada_w, ada_b, norm1_w, w_in, a_sinks, b_q_norm_w, b_w_uq, b_kv_norm_w, b_w_ukv, c_ln_w, c_ln_b, c_w_s, c_b_s, out_norm_w, w_out, norm2_w, w_gate_up, w_down, final_norm_w, loss_target, m_ada_w, m_ada_b, m_norm1_w, m_w_in, m_a_sinks, m_b_q_norm_w, m_b_w_uq, m_b_kv_norm_w, m_b_w_ukv, m_c_ln_w, m_c_ln_b, m_c_w_s, m_c_b_s, m_out_norm_w, m_w_out, m_norm2_w, m_w_gate_up, m_w_down, m_final_norm_w, v_ada_w, v_ada_b, v_norm1_w, v_w_in, v_a_sinks, v_b_q_norm_w, v_b_w_uq, v_b_kv_norm_w, v_b_w_ukv, v_c_ln_w, v_c_ln_b, v_c_w_s, v_c_b_s, v_out_norm_w, v_w_out, v_norm2_w, v_w_gate_up, v_w_down, v_final_norm_w):
    names = ("ada_w", "ada_b", "norm1_w", "w_in", "a_sinks", "b_q_norm_w", "b_w_uq", "b_kv_norm_w", "b_w_ukv", "c_ln_w",
             "c_ln_b", "c_w_s", "c_b_s", "out_norm_w", "w_out", "norm2_w", "w_gate_up", "w_down", "final_norm_w")
    ws = dict(zip(names, (ada_w, ada_b, norm1_w, w_in, a_sinks, b_q_norm_w, b_w_uq, b_kv_norm_w, b_w_ukv, c_ln_w, c_ln_b,
                          c_w_s, c_b_s, out_norm_w, w_out, norm2_w, w_gate_up, w_down, final_norm_w)))
    ms = dict(zip(names, (m_ada_w, m_ada_b, m_norm1_w, m_w_in, m_a_sinks, m_b_q_norm_w, m_b_w_uq, m_b_kv_norm_w, m_b_w_ukv,
                          m_c_ln_w, m_c_ln_b, m_c_w_s, m_c_b_s, m_out_norm_w, m_w_out, m_norm2_w, m_w_gate_up, m_w_down,
                          m_final_norm_w)))
    vs = dict(zip(names, (v_ada_w, v_ada_b, v_norm1_w, v_w_in, v_a_sinks, v_b_q_norm_w, v_b_w_uq, v_b_kv_norm_w, v_b_w_ukv,
                          v_c_ln_w, v_c_ln_b, v_c_w_s, v_c_b_s, v_out_norm_w, v_w_out, v_norm2_w, v_w_gate_up, v_w_down,
                          v_final_norm_w)))
    shards = {k: ws[k].astype(BF16) for k in SHARDED}
    loss_local, grad_x, landed, gsmall, ada_g = _step(
        x, c, positions, loss_target, {k: ws[k] for k in SMALL}, lambda l: {k: shards[k][l] for k in EXCHANGED},
        ada=shards["ada_w"])

    mine = {k: jnp.stack([_sum_slots(landed[l][k], "grad_sum_" + k) for l in range(DEPTH)]) for k in EXCHANGED}
    mine["ada_w"] = ada_g
    theirs = dict(zip(SHARDED, _xchg_call("swap", [mine[k] for k in SHARDED], "grad_sibling_swap")))
    grads, delta, new_m, new_v = {}, {}, {}, {}
    for k in SHARDED:
        shp = ws[k].shape
        two = (shp[0] * shp[1], shp[2])
        g, dlt, nm, nv = _adamw(ws[k].reshape(two), mine[k].reshape(two), theirs[k].reshape(two), ms[k].reshape(two),
                                vs[k].reshape(two), "adamw_" + k)
        grads[k], delta[k], new_m[k], new_v[k] = g.reshape(shp), dlt.reshape(shp), nm.reshape(shp), nv.reshape(shp)

    small_shapes = [ws[k].shape for k in SMALL]
    sblob = _pack([gsmall[k] for k in SMALL] + [loss_local.reshape(1)], LANES, 8)
    svals = _unpack(_all_reduce_small(sblob, "small_all_reduce"), small_shapes + [(1,)])
    loss = svals[-1].reshape(())
    pw = _pack([ws[k] for k in SMALL], LANES, 8)
    pg = _pack(svals[:-1], LANES, 8)
    pm = _pack([ms[k] for k in SMALL], LANES, 8)
    pv = _pack([vs[k] for k in SMALL], LANES, 8)
    g, dlt, nm, nv = _adamw(pw, pg, None, pm, pv, "adamw_small")
    for k, a, b_, c_, d_ in zip(SMALL, _unpack(g, small_shapes), _unpack(dlt, small_shapes), _unpack(nm, small_shapes),
                                _unpack(nv, small_shapes)):
        grads[k], delta[k], new_m[k], new_v[k] = a, b_, c_, d_

    return (loss, grad_x, *[grads[k] for k in names], *[delta[k] for k in names], *[new_m[k] for k in names],
            *[new_v[k] for k in names])
```

```python
import functools
import math

import numpy as np
import jax
import jax.numpy as jnp
from jax import lax
from jax.experimental import pallas as pl
from jax.experimental.pallas import tpu as pltpu

F32 = jnp.float32
BF16 = jnp.bfloat16

D_MODEL = 1024
DEPTH = 4
HEAD_DIM = 64
ROPE_THETA = 10000.0
NORM_EPS = 1e-6
NEG_INF = -1e30
LOG2_E = math.log2(math.e)
A_Q_HEADS = 6
A_KV_HEADS = 2
A_WINDOW = 128
B_HEADS = 6
B_Q_RANK = 384
B_KV_RANK = 256
B_NOPE = 64
B_ROPE = 32
B_V = 64
C_GROUPS = 4
C_GROUP_DIM = 64
C_WIDTH = 256
C_CHUNK = 128
IN_COLS = 1824
FFN_HIDDEN = 2816
N_MOD = 6
ADAM_LR = 0.001
ADAM_B1 = 0.9
ADAM_B2 = 0.999
ADAM_EPS = 1e-08
ADAM_WD = 0.01
ADAM_STEP = 10

LANES = 128
VMEM_LIMIT = 56 * 1024 * 1024
N_CHIPS = 4
WINDOW_SUB = 256

P_KR, P_AQ, P_AK, P_AV, P_CQ, P_CKV, P_MIX, P_CU, P_CV, P_END = 0, 128, 512, 640, 768, 1152, 1408, 1536, 1792, 2048


def _map_w_in():
    idx = -np.ones(P_END, np.int64)
    idx[P_AQ:P_AQ + 384] = np.arange(384)
    idx[P_AK:P_AK + 128] = 384 + np.arange(128)
    idx[P_AV:P_AV + 128] = 512 + np.arange(128)
    idx[P_CQ:P_CQ + 384] = 640 + np.arange(384)
    idx[P_CKV:P_CKV + 256] = 1024 + np.arange(256)
    idx[P_KR + 48 + np.arange(16)] = 1280 + np.arange(16)
    idx[P_KR + 112 + np.arange(16)] = 1296 + np.arange(16)
    idx[P_CU:P_CU + 256] = 1312 + np.arange(256)
    idx[P_CV:P_CV + 256] = 1568 + np.arange(256)
    return idx


def _map_w_uq():
    idx = -np.ones(B_HEADS * LANES, np.int64)
    for h in range(B_HEADS):
        b = h * (B_NOPE + B_ROPE)
        idx[h * LANES + np.arange(48)] = b + np.arange(48)
        idx[h * LANES + 48 + np.arange(16)] = b + 64 + np.arange(16)
        idx[h * LANES + 64 + np.arange(16)] = b + 48 + np.arange(16)
        idx[h * LANES + 112 + np.arange(16)] = b + 80 + np.arange(16)
    return idx


def _map_w_ukv():
    idx = -np.ones(2 * B_HEADS * LANES, np.int64)
    for h in range(B_HEADS):
        b = h * (B_NOPE + B_V)
        idx[h * LANES + np.arange(48)] = b + np.arange(48)
        idx[h * LANES + 64 + np.arange(16)] = b + 48 + np.arange(16)
        idx[B_HEADS * LANES + h * LANES + np.arange(B_V)] = b + B_NOPE + np.arange(B_V)
    return idx


def _inverse(idx, n):
    inv = np.zeros(n, np.int64)
    pos = np.nonzero(idx >= 0)[0]
    inv[idx[pos]] = pos
    return inv


def _runs(idx):
    runs, i, n = [], 0, len(idx)
    while i < n:
        j = i + 1
        while j < n and ((idx[i] < 0 and idx[j] < 0) or (idx[i] >= 0 and idx[j] == idx[i] + (j - i))):
            j += 1
        runs.append((int(idx[i]), j - i))
        i = j
    return runs


def _select_axis(w, idx, axis):
    pieces = []
    for start, length in _runs(idx):
        if start < 0:
            shape = list(w.shape)
            shape[axis] = length
            pieces.append(jnp.zeros(shape, w.dtype))
        else:
            pieces.append(lax.slice_in_dim(w, start, start + length, axis=axis))
    return jnp.concatenate(pieces, axis=axis)


def _pad_axis(w, idx, axis):
    return _select_axis(w, idx, axis)


def _unpad_axis(g, idx, n, axis):
    return _select_axis(g, _inverse(idx, n), axis)


def _params(sem):
    return pltpu.CompilerParams(dimension_semantics=sem, vmem_limit_bytes=VMEM_LIMIT)


def _tile(dim, target):
    if dim <= target:
        return dim
    best = None
    for t in range(LANES, target + 1, LANES):
        if dim % t == 0:
            best = t
    assert best is not None, dim
    return best


def _row_div(rows, target):
    if rows <= target:
        return rows
    best = None
    for t in range(8, target + 1, 8):
        if rows % t == 0:
            best = t
    assert best is not None, rows
    return best


_ANY = pl.BlockSpec(memory_space=pl.ANY)
_MESH = pl.DeviceIdType.MESH


def _xchg_out_shapes(kind, srcs):
    if kind == "gather":
        return [jax.ShapeDtypeStruct((N_CHIPS,) + s.shape, s.dtype) for s in srcs]
    return [jax.ShapeDtypeStruct(s.shape, s.dtype) for s in srcs]


def _xchg_scratch(kind, n):
    per = 1 if kind == "swap" else N_CHIPS - 1
    return [pltpu.SemaphoreType.DMA((per * n,)), pltpu.SemaphoreType.DMA((per * n,)), pltpu.SemaphoreType.DMA((n,))]


def _xchg_copies(kind, srcs, dsts, send_sems, recv_sems, local_sems, arrivals):
    x, y, c = lax.axis_index("x"), lax.axis_index("y"), lax.axis_index("c")
    me = 2 * x + y
    peers = [(1 - x, y), (x, 1 - y), (1 - x, 1 - y)]
    local, out, back = [], [], []
    for i, (s, d) in enumerate(zip(srcs, dsts)):
        if kind == "swap":
            cp = pltpu.make_async_remote_copy(src_ref=s, dst_ref=d, send_sem=send_sems.at[i], recv_sem=recv_sems.at[i],
                                              device_id=(x, y, 1 - c), device_id_type=_MESH)
            out.append(cp)
            back.append(cp)
            continue
        local.append(pltpu.make_async_copy(s if kind == "gather" else s.at[me], d.at[me], local_sems.at[i]))
        for kk, (px, py) in enumerate(peers):
            j = (N_CHIPS - 1) * i + kk
            theirs = 2 * px + py
            out.append(pltpu.make_async_remote_copy(
                src_ref=s if kind == "gather" else s.at[theirs], dst_ref=d.at[me], send_sem=send_sems.at[j],
                recv_sem=recv_sems.at[j], device_id=(px, py, c), device_id_type=_MESH))
            if arrivals:
                back.append(pltpu.make_async_remote_copy(
                    src_ref=s if kind == "gather" else s.at[me], dst_ref=d.at[theirs], send_sem=send_sems.at[j],
                    recv_sem=recv_sems.at[j], device_id=(px, py, c), device_id_type=_MESH))
    return local, out, back


def _xchg_start(kind, srcs, dsts, sems):
    local, out, _ = _xchg_copies(kind, srcs, dsts, *sems, arrivals=False)
    for cp in local + out:
        cp.start()


def _xchg_wait(kind, srcs, dsts, sems):
    local, out, back = _xchg_copies(kind, srcs, dsts, *sems, arrivals=True)
    for cp in back:
        cp.wait_recv()
    for cp in out:
        cp.wait_send()
    for cp in local:
        cp.wait()


def _xchg_at_ends(kind, srcs, dsts, sems, grid, first):
    ids = [pl.program_id(a) for a in range(len(grid))]
    cond = None
    for i, n in zip(ids, grid):
        c = (i == 0) if first else (i == n - 1)
        cond = c if cond is None else jnp.logical_and(cond, c)

    @pl.when(cond)
    def _():
        (_xchg_start if first else _xchg_wait)(kind, srcs, dsts, sems)


def _xchg_call(kind, srcs, name):
    n = len(srcs)

    def kern(*refs):
        s, d, sems = refs[:n], refs[n:2 * n], refs[2 * n:]
        _xchg_start(kind, s, d, sems)
        _xchg_wait(kind, s, d, sems)

    return pl.pallas_call(
        kern, name=name, in_specs=[_ANY] * n, out_specs=[_ANY] * n, out_shape=_xchg_out_shapes(kind, srcs),
        scratch_shapes=_xchg_scratch(kind, n),
    )(*srcs)


_DIMS = {"nn": (((1,), (0,)), ((), ())), "nt": (((1,), (1,)), ((), ())), "tn": (((0,), (0,)), ((), ()))}


def _matmul(a, b, mode, out_dtype, name, *, b_chunks=False, out_chunks=False, xchg=None):
    if b_chunks:
        nchunk, brows, bcols = b.shape
        bshape = (brows, nchunk * bcols)
    else:
        bshape = b.shape
    if mode == "nn":
        (m, k), (_, n) = a.shape, bshape
    elif mode == "nt":
        (m, k), (n, _) = a.shape, bshape
    else:
        (k, m), (_, n) = a.shape, bshape
    tm, tk = (1408, 1024) if mode == "tn" else (1024, 1408)
    tm, tn, tk = _tile(m, tm), _tile(n, 1408), _tile(k, tk)
    if b_chunks:
        if mode == "nn":
            tn = bcols
        else:
            assert mode == "nt"
            tk = bcols
    if out_chunks:
        assert n % N_CHIPS == 0
        tn = n // N_CHIPS
    ni, nj, nk = m // tm, n // tn, k // tk
    dims = _DIMS[mode]
    n_x = 0 if xchg is None else len(xchg[1])

    def kern(*refs):
        a_ref, b_ref = refs[0], refs[1]
        xs = refs[2:2 + n_x]
        o_ref = refs[2 + n_x]
        xd = refs[3 + n_x:3 + 2 * n_x]
        acc_ref = refs[3 + 2 * n_x]
        sems = refs[4 + 2 * n_x:]
        kk = pl.program_id(2)
        if n_x:
            _xchg_at_ends(xchg[0], xs, xd, sems, (ni, nj, nk), True)

        @pl.when(kk == 0)
        def _():
            acc_ref[...] = jnp.zeros_like(acc_ref)

        acc_ref[...] += lax.dot_general(a_ref[...], b_ref[...], dims, preferred_element_type=F32)

        @pl.when(kk == nk - 1)
        def _():
            o_ref[...] = acc_ref[...].astype(o_ref.dtype)

        if n_x:
            _xchg_at_ends(xchg[0], xs, xd, sems, (ni, nj, nk), False)

    if mode == "tn":
        a_spec = pl.BlockSpec((tk, tm), lambda i, j, kk: (kk, i))
    else:
        a_spec = pl.BlockSpec((tm, tk), lambda i, j, kk: (i, kk))
    if b_chunks and mode == "nn":
        b_spec = pl.BlockSpec((None, tk, tn), lambda i, j, kk: (j, kk, 0))
    elif b_chunks:
        b_spec = pl.BlockSpec((None, tn, tk), lambda i, j, kk: (kk, j, 0))
    elif mode == "nt":
        b_spec = pl.BlockSpec((tn, tk), lambda i, j, kk: (j, kk))
    else:
        b_spec = pl.BlockSpec((tk, tn), lambda i, j, kk: (kk, j))
    if out_chunks:
        o_spec = pl.BlockSpec((None, tm, tn), lambda i, j, kk: (j, i, 0))
        o_shape = jax.ShapeDtypeStruct((N_CHIPS, m, tn), out_dtype)
    else:
        o_spec = pl.BlockSpec((tm, tn), lambda i, j, kk: (i, j))
        o_shape = jax.ShapeDtypeStruct((m, n), out_dtype)
    xs = [] if xchg is None else list(xchg[1])
    res = pl.pallas_call(
        kern, name=name, grid=(ni, nj, nk),
        in_specs=[a_spec, b_spec] + [_ANY] * n_x, out_specs=[o_spec] + [_ANY] * n_x,
        out_shape=[o_shape] + (_xchg_out_shapes(xchg[0], xs) if n_x else []),
        scratch_shapes=[pltpu.VMEM((tm, tn), F32)] + (_xchg_scratch(xchg[0], n_x) if n_x else []),
        compiler_params=_params(("arbitrary", "arbitrary", "arbitrary") if n_x else ("parallel", "parallel", "arbitrary")),
    )(a, b, *xs)
    return (res[0], list(res[1:])) if n_x else res[0]


def _rowcall(body, *, name, T, S, tr, rows, exs=(), pars=(), row_outs=(), ex_outs=(), par_outs=(), aliases=None):
    assert S % tr == 0 and T % S == 0
    per_ex = S // tr
    nb = T // S
    n_rows, n_exs, n_pars = len(rows), len(exs), len(pars)
    n_ro, n_eo, n_po = len(row_outs), len(ex_outs), len(par_outs)

    def kern(*refs):
        ins = refs[:n_rows + n_exs + n_pars]
        outs = refs[n_rows + n_exs + n_pars:]
        rv = [r[...].astype(F32) for r in ins[:n_rows]]
        ev = [r[0] for r in ins[n_rows:n_rows + n_exs]]
        pv = [r[...] for r in ins[n_rows + n_exs:]]
        ro, eo, po = body(rv, ev, pv)
        i = pl.program_id(0)
        for ref, val in zip(outs[:n_ro], ro):
            if isinstance(val, (list, tuple)):
                off = 0
                for piece in val:
                    w = piece.shape[-1]
                    ref[:, off:off + w] = piece.astype(ref.dtype)
                    off += w
            else:
                ref[...] = val.astype(ref.dtype)
        first_of_ex = (i % per_ex) == 0
        for ref, val in zip(outs[n_ro:n_ro + n_eo], eo):
            @pl.when(first_of_ex)
            def _(ref=ref, val=val):
                ref[0] = val

            @pl.when(jnp.logical_not(first_of_ex))
            def _(ref=ref, val=val):
                ref[0] += val
        for ref, val in zip(outs[n_ro + n_eo:], po):
            @pl.when(i == 0)
            def _(ref=ref, val=val):
                ref[...] = val

            @pl.when(i != 0)
            def _(ref=ref, val=val):
                ref[...] += val

    in_specs = [pl.BlockSpec((tr, w), functools.partial(lambda i, cb: (i, cb), cb=cb)) for (_, w, cb) in rows]
    in_specs += [pl.BlockSpec((1, 1, e.shape[-1]), lambda i: (i // per_ex, 0, 0)) for e in exs]
    in_specs += [pl.BlockSpec(p.shape, functools.partial(lambda i, nd: (0,) * nd, nd=p.ndim)) for p in pars]
    out_specs = [pl.BlockSpec((tr, w), functools.partial(lambda i, cb: (i, cb), cb=cb)) for (_, _, w, cb) in row_outs]
    out_specs += [pl.BlockSpec((1, 1, f), lambda i: (i // per_ex, 0, 0)) for f in ex_outs]
    out_specs += [pl.BlockSpec(tuple(s), functools.partial(lambda i, nd: (0,) * nd, nd=len(s))) for s in par_outs]
    out_shape = [jax.ShapeDtypeStruct((T, tw), dt) for (tw, dt, _, _) in row_outs]
    out_shape += [jax.ShapeDtypeStruct((nb, 1, f), F32) for f in ex_outs]
    out_shape += [jax.ShapeDtypeStruct(tuple(s), F32) for s in par_outs]
    res = pl.pallas_call(
        kern, name=name, grid=(T // tr,), in_specs=in_specs, out_specs=out_specs, out_shape=out_shape,
        input_output_aliases=aliases or {}, compiler_params=_params(("arbitrary",)),
    )(*[r[0] for r in rows], *exs, *pars)
    return res[:n_ro], res[n_ro:n_ro + n_eo], res[n_ro + n_eo:]


def _rms(x, w, n=None):
    n = x.shape[-1] if n is None else n
    ms = jnp.sum(x * x, axis=-1, keepdims=True) * (1.0 / n)
    return x * lax.rsqrt(ms + NORM_EPS) * w


def _gelu(x):
    return 0.5 * x * (1.0 + lax.erf(x * np.float32(1.0 / math.sqrt(2.0))))


def _silu(x):
    return x * jax.nn.sigmoid(x)


@jax.custom_vjp
def _rope(x, cos, sin):
    return x * cos + pltpu.roll(x, 64, 1) * sin


def _rope_fwd(x, cos, sin):
    return _rope(x, cos, sin), (cos, sin)


def _rope_bwd(res, dy):
    cos, sin = res
    return dy * cos + pltpu.roll(dy * sin, 64, 1), None, None


_rope.defvjp(_rope_fwd, _rope_bwd)


def _heads(x, n):
    return [x[:, h * LANES:(h + 1) * LANES] for h in range(n)]


@functools.partial(jax.custom_vjp, nondiff_argnums=(1,))
def _lroll(x, shift):
    return pltpu.roll(x, shift, 1)


def _lroll_fwd(x, shift):
    return _lroll(x, shift), None


def _lroll_bwd(shift, _, dy):
    return (pltpu.roll(dy, (LANES - shift) % LANES, 1),)


_lroll.defvjp(_lroll_fwd, _lroll_bwd)


def _spread_rotary(v):
    low = lax.broadcasted_iota(jnp.int32, (1, LANES), 1) < 64
    return [jnp.where(low, v, _lroll(v, 32)), jnp.where(low, _lroll(v, 64), _lroll(v, 96))]


def _spread_values(v):
    low = lax.broadcasted_iota(jnp.int32, (1, LANES), 1) < 64
    return [jnp.where(low, v, 0.0), jnp.where(low, _lroll(v, 64), 0.0)]


def _f_norm_mod(x, w, sc, sh):
    return _rms(x, w) * (1.0 + sc) + sh


def _f_resid_norm_mod(xa, delta, g, w, sc, sh):
    xn = xa + g * delta
    return xn, _f_norm_mod(xn, w, sc, sh)


def _f_mixprep(proj, ca, sa, cb, sb, qnw, kvnw):
    qa = [_rope(p, ca, sa) for pair in _heads(proj[:, P_AQ:P_AK], A_Q_HEADS // 2) for p in _spread_rotary(pair)]
    ka = [_rope(p, ca, sa) for p in _spread_rotary(proj[:, P_AK:P_AV])]
    va = _spread_values(proj[:, P_AV:P_CQ])
    cqn = _rms(proj[:, P_CQ:P_CKV], qnw)
    ckvn = _rms(proj[:, P_CKV:P_MIX], kvnw)
    kr = _rope(proj[:, P_KR:P_AQ], cb, sb)
    return jnp.concatenate(qa, -1), jnp.concatenate(ka, -1), jnp.concatenate(va, -1), cqn, ckvn, kr


def _f_mlaprep(q, kv, kr, cb, sb):
    qs = [_rope(p, cb, sb) for p in _heads(q, B_HEADS)]
    ks = [p + kr for p in _heads(kv[:, :B_HEADS * LANES], B_HEADS)]
    return jnp.concatenate(qs, -1), jnp.concatenate(ks, -1), kv[:, B_HEADS * LANES:]


def _f_sgu(cu, cv, ln_w, ln_b, w_s, b_col):
    u = _gelu(cu)
    v = _gelu(cv)
    mu = jnp.mean(v, axis=-1, keepdims=True)
    var = jnp.mean(jnp.square(v - mu), axis=-1, keepdims=True)
    vn = (v - mu) * lax.rsqrt(var + NORM_EPS) * ln_w + ln_b
    r = lax.broadcasted_iota(jnp.int32, (C_CHUNK, C_CHUNK), 0)
    c = lax.broadcasted_iota(jnp.int32, (C_CHUNK, C_CHUNK), 1)
    lane = lax.broadcasted_iota(jnp.int32, (1, LANES), 1)
    per_block = LANES // C_GROUP_DIM
    blocks = []
    for blk, vb in enumerate(_heads(vn, C_WIDTH // LANES)):
        mixed = jnp.zeros(vb.shape, F32)
        for j in range(per_block):
            g = blk * per_block + j
            gm = (lane // C_GROUP_DIM == j).astype(F32)
            wg = jnp.where(r >= c, w_s[g], 0.0).astype(BF16)
            mixed = mixed + jnp.dot(wg, (vb * gm).astype(BF16), preferred_element_type=F32) + b_col[g] * gm
        blocks.append(mixed)
    return u * jnp.concatenate(blocks, -1)


@jax.custom_vjp
def _pack_pairs(x):
    heads = _heads(x, x.shape[-1] // LANES)
    return jnp.concatenate([heads[i] + pltpu.roll(heads[i + 1], 64, 1) for i in range(0, len(heads), 2)], -1)


def _pack_pairs_fwd(x):
    return _pack_pairs(x), None


def _pack_pairs_bwd(_, dy):
    out = []
    for p in _heads(dy, dy.shape[-1] // LANES):
        out += [p, pltpu.roll(p, 64, 1)]
    return (jnp.concatenate(out, -1),)


_pack_pairs.defvjp(_pack_pairs_fwd, _pack_pairs_bwd)


def _f_outnorm(oa, ob, yc, gw):
    na, nb = A_Q_HEADS * HEAD_DIM, B_HEADS * B_V
    ya = _rms(_pack_pairs(oa), gw[:, :na])
    yb = _rms(_pack_pairs(ob), gw[:, na:na + nb])
    ycn = _rms(yc, gw[:, na + nb:])
    return jnp.concatenate([ya, yb, ycn], -1)


def _f_swiglu(gate, up):
    return _silu(gate) * up


def _mask(q_start, k_start, tq, tk, window):
    qpos = q_start + lax.broadcasted_iota(jnp.int32, (tq, tk), 0)
    kpos = k_start + lax.broadcasted_iota(jnp.int32, (tq, tk), 1)
    m = kpos <= qpos
    if window is not None:
        m = jnp.logical_and(m, qpos - kpos < window)
    return m


def _tile_fwd(qv, kk, vv, q_start, k_start, n_free, scale, window, m0, l0):
    tq = qv.shape[0]
    W = kk.shape[0]
    c = scale * LOG2_E
    parts = []
    if n_free > 0:
        parts.append((lax.dot_general(qv, kk[:n_free], _DIMS["nt"], preferred_element_type=F32), vv[:n_free]))
    if W > n_free:
        s = lax.dot_general(qv, kk[n_free:], _DIMS["nt"], preferred_element_type=F32)
        s = jnp.where(_mask(q_start, k_start + n_free, tq, W - n_free, window), s, NEG_INF)
        parts.append((s, vv[n_free:]))
    m = None if m0 is None else m0 * (1.0 / scale)
    for s, _ in parts:
        mx = jnp.max(s, axis=-1, keepdims=True)
        m = mx if m is None else jnp.maximum(m, mx)
    l = None if l0 is None else l0 * jnp.exp2((m0 * (1.0 / scale) - m) * c)
    o = None
    for s, vpart in parts:
        p = jnp.exp2((s - m) * c)
        ps = jnp.sum(p, axis=-1, keepdims=True)
        l = ps if l is None else l + ps
        po = jnp.dot(p.astype(BF16), vpart, preferred_element_type=F32)
        o = po if o is None else o + po
    return o / l, m * scale + jnp.log(l)


def _tile_bwd(qv, kk, vv, dof, ov, lse, q_start, k_start, n_free, scale, window):
    tq = qv.shape[0]
    W = kk.shape[0]
    dob = dof.astype(BF16)
    delta = jnp.sum(dof * ov, axis=-1, keepdims=True)
    c = scale * LOG2_E
    lse2 = lse * LOG2_E
    dq = None
    outs = []
    for (a, b, masked) in ((0, n_free, False), (n_free, W, True)):
        if b <= a:
            continue
        kp, vp = kk[a:b], vv[a:b]
        s = lax.dot_general(qv, kp, _DIMS["nt"], preferred_element_type=F32)
        if masked:
            s = jnp.where(_mask(q_start, k_start + a, tq, b - a, window), s, NEG_INF)
        p = jnp.exp2(s * c - lse2)
        dp = lax.dot_general(dob, vp, _DIMS["nt"], preferred_element_type=F32)
        ds = (p * ((dp - delta) * scale)).astype(BF16)
        d = jnp.dot(ds, kp, preferred_element_type=F32)
        dq = d if dq is None else dq + d
        dkp = lax.dot_general(ds, qv, _DIMS["tn"], preferred_element_type=F32)
        dvp = lax.dot_general(p.astype(BF16), dob, _DIMS["tn"], preferred_element_type=F32)
        outs.append((a, dkp, dvp))
    return dq, outs


def _attn_fwd(q, k, v, sinks, *, B, S, HQ, HK, window, scale, tq, band, name, xchg=None):
    G = HQ // HK
    nq = S // tq
    T = B * S
    has_sink = sinks is not None
    n_x = 0 if xchg is None else len(xchg[1])
    n_in = 4 if has_sink else 3

    def kern(*refs):
        xs, xd, sems = refs[n_in:n_in + n_x], refs[n_in + n_x + 2:n_in + 2 * n_x + 2], refs[n_in + 2 * n_x + 2:]
        refs = refs[:n_in] + refs[n_in + n_x:n_in + n_x + 2]
        if n_x:
            _xchg_at_ends(xchg[0], xs, xd, sems, (B, HQ, nq), True)
        if has_sink:
            q_ref, k_ref, v_ref, s_ref, o_ref, lse_ref = refs
        else:
            q_ref, k_ref, v_ref, o_ref, lse_ref = refs
        q_start = pl.program_id(2) * tq
        qv = q_ref[...]
        if has_sink:
            m0 = jnp.broadcast_to(s_ref[0][:, :1], (tq, 1))
            l0 = jnp.ones((tq, 1), F32)
        else:
            m0 = l0 = None

        def finish(o, lse):
            o_ref[...] = o.astype(o_ref.dtype)
            lse_ref[...] = jnp.broadcast_to(lse, (tq, LANES))

        if window is None:
            bidx = q_start // band
            for bb in range(S // band):
                @pl.when(bidx == bb)
                def _(bb=bb):
                    W = (bb + 1) * band
                    finish(*_tile_fwd(qv, k_ref[0:W, :], v_ref[0:W, :], q_start, 0, bb * band, scale, None, m0, l0))
        else:
            sub = min(tq, WINDOW_SUB)
            W = min(S, sub + window)
            for r in range(0, tq, sub):
                k_start = pl.multiple_of(jnp.maximum(q_start + r - window, 0), window)
                o, lse = _tile_fwd(qv[r:r + sub], k_ref[pl.ds(k_start, W), :], v_ref[pl.ds(k_start, W), :], q_start + r,
                                   k_start, 0, scale, window, None if m0 is None else m0[:sub], None if l0 is None else l0[:sub])
                o_ref[r:r + sub, :] = o.astype(o_ref.dtype)
                lse_ref[r:r + sub, :] = jnp.broadcast_to(lse, (sub, LANES))
        if n_x:
            _xchg_at_ends(xchg[0], xs, xd, sems, (B, HQ, nq), False)

    q_spec = pl.BlockSpec((tq, LANES), lambda b, h, i: (b * nq + i, h))
    kv_spec = pl.BlockSpec((S, LANES), lambda b, h, i: (b, h // G))
    in_specs = [q_spec, kv_spec, kv_spec]
    args = [q, k, v]
    if has_sink:
        in_specs.append(pl.BlockSpec((1, 1, LANES), lambda b, h, i: (h, 0, 0)))
        args.append(sinks)
    xs = [] if xchg is None else list(xchg[1])
    res = pl.pallas_call(
        kern, name=name, grid=(B, HQ, nq), in_specs=in_specs + [_ANY] * n_x, out_specs=[q_spec, q_spec] + [_ANY] * n_x,
        out_shape=[jax.ShapeDtypeStruct((T, HQ * LANES), BF16), jax.ShapeDtypeStruct((T, HQ * LANES), F32)]
        + (_xchg_out_shapes(xchg[0], xs) if n_x else []),
        scratch_shapes=_xchg_scratch(xchg[0], n_x) if n_x else [],
        compiler_params=_params(("arbitrary",) * 3 if n_x else ("parallel", "parallel", "arbitrary")),
    )(*args, *xs)
    return ((res[0], res[1]), list(res[2:])) if n_x else res


def _attn_bwd(q, k, v, o, lse, do, sinks, *, B, S, HQ, HK, window, scale, tq, band, name, xchg=None):
    G = HQ // HK
    nq = S // tq
    T = B * S
    has_sink = sinks is not None
    n_x = 0 if xchg is None else len(xchg[1])
    n_in, n_out = (7, 4) if has_sink else (6, 3)

    def kern(*refs):
        xs, xd = refs[n_in:n_in + n_x], refs[n_in + n_x + n_out:n_in + 2 * n_x + n_out]
        dk_acc, dv_acc = refs[n_in + 2 * n_x + n_out:n_in + 2 * n_x + n_out + 2]
        sems = refs[n_in + 2 * n_x + n_out + 2:]
        refs = refs[:n_in] + refs[n_in + n_x:n_in + n_x + n_out]
        if n_x:
            _xchg_at_ends(xchg[0], xs, xd, sems, (B, HK, G, nq), True)
        if has_sink:
            q_ref, k_ref, v_ref, o_ref, lse_ref, do_ref, s_ref, dq_ref, dk_ref, dv_ref, ds_ref = refs
        else:
            q_ref, k_ref, v_ref, o_ref, lse_ref, do_ref, dq_ref, dk_ref, dv_ref = refs
        gi = pl.program_id(2)
        qi = pl.program_id(3)
        q_start = qi * tq

        @pl.when(jnp.logical_and(gi == 0, qi == 0))
        def _():
            dk_acc[...] = jnp.zeros_like(dk_acc)
            dv_acc[...] = jnp.zeros_like(dv_acc)

        qv = q_ref[...]
        dof = do_ref[...].astype(F32)
        ov = o_ref[...].astype(F32)
        lse_v = lse_ref[...][:, :1]
        if window is None:
            bidx = q_start // band
            for bb in range(S // band):
                @pl.when(bidx == bb)
                def _(bb=bb):
                    W = (bb + 1) * band
                    dq, outs = _tile_bwd(qv, k_ref[0:W, :], v_ref[0:W, :], dof, ov, lse_v, q_start, 0, bb * band, scale, None)
                    dq_ref[...] = dq.astype(dq_ref.dtype)
                    for a, dkp, dvp in outs:
                        dk_acc[a:a + dkp.shape[0], :] += dkp
                        dv_acc[a:a + dvp.shape[0], :] += dvp
        else:
            sub = min(tq, WINDOW_SUB)
            W = min(S, sub + window)
            parts = []
            for r in range(0, tq, sub):
                k_start = pl.multiple_of(jnp.maximum(q_start + r - window, 0), window)
                dq, outs = _tile_bwd(qv[r:r + sub], k_ref[pl.ds(k_start, W), :], v_ref[pl.ds(k_start, W), :],
                                     dof[r:r + sub], ov[r:r + sub], lse_v[r:r + sub], q_start + r, k_start, 0, scale, window)
                dq_ref[r:r + sub, :] = dq.astype(dq_ref.dtype)
                parts.append((k_start, outs[0][1], outs[0][2]))
            for k_start, dkp, dvp in parts:
                dk_acc[pl.ds(k_start, W), :] += dkp
                dv_acc[pl.ds(k_start, W), :] += dvp

        @pl.when(jnp.logical_and(gi == G - 1, qi == nq - 1))
        def _():
            dk_ref[...] = dk_acc[...].astype(dk_ref.dtype)
            dv_ref[...] = dv_acc[...].astype(dv_ref.dtype)
        if has_sink:
            delta = jnp.sum(dof * ov, axis=-1, keepdims=True)
            sink = s_ref[0][:, :1]
            part = -jnp.sum(jnp.exp(sink - lse_v) * delta, axis=0, keepdims=True)
            part = jnp.broadcast_to(part, (1, LANES))

            @pl.when(qi == 0)
            def _():
                ds_ref[0] = part

            @pl.when(qi != 0)
            def _():
                ds_ref[0] += part
        if n_x:
            _xchg_at_ends(xchg[0], xs, xd, sems, (B, HK, G, nq), False)

    q_spec = pl.BlockSpec((tq, LANES), lambda b, hk, g, i: (b * nq + i, hk * G + g))
    kv_spec = pl.BlockSpec((S, LANES), lambda b, hk, g, i: (b, hk))
    in_specs = [q_spec, kv_spec, kv_spec, q_spec, q_spec, q_spec]
    args = [q, k, v, o, lse, do]
    out_specs = [q_spec, kv_spec, kv_spec]
    out_shape = [jax.ShapeDtypeStruct((T, HQ * LANES), BF16), jax.ShapeDtypeStruct((T, HK * LANES), BF16),
                 jax.ShapeDtypeStruct((T, HK * LANES), BF16)]
    if has_sink:
        in_specs.append(pl.BlockSpec((1, 1, LANES), lambda b, hk, g, i: (hk * G + g, 0, 0)))
        args.append(sinks)
        out_specs.append(pl.BlockSpec((1, 1, LANES), lambda b, hk, g, i: (b * HQ + hk * G + g, 0, 0)))
        out_shape.append(jax.ShapeDtypeStruct((B * HQ, 1, LANES), F32))
    xs = [] if xchg is None else list(xchg[1])
    res = pl.pallas_call(
        kern, name=name, grid=(B, HK, G, nq), in_specs=in_specs + [_ANY] * n_x, out_specs=out_specs + [_ANY] * n_x,
        out_shape=out_shape + (_xchg_out_shapes(xchg[0], xs) if n_x else []),
        scratch_shapes=[pltpu.VMEM((S, LANES), F32)] * 2 + (_xchg_scratch(xchg[0], n_x) if n_x else []),
        compiler_params=_params(("arbitrary",) * 4 if n_x else ("parallel", "parallel", "arbitrary", "arbitrary")),
    )(*args, *xs)
    main = tuple(res[:n_out]) if has_sink else (*res[:n_out], None)
    return (main, list(res[n_out:])) if n_x else main


def _rope_tables(pos_col, freq, sign, keep, name):
    T = pos_col.shape[0]
    tr = _tile(T, 1024)

    def kern(p_ref, f_ref, s_ref, k_ref, c_out, s_out):
        ang = p_ref[...] * f_ref[...]
        c_out[...] = jnp.cos(ang) * k_ref[...]
        s_out[...] = jnp.sin(ang) * s_ref[...]

    spec = pl.BlockSpec((tr, LANES), lambda i: (i, 0))
    par = pl.BlockSpec((1, LANES), lambda i: (0, 0))
    return pl.pallas_call(
        kern, name=name, grid=(T // tr,), in_specs=[pl.BlockSpec((tr, 1), lambda i: (i, 0)), par, par, par],
        out_specs=[spec, spec], out_shape=[jax.ShapeDtypeStruct((T, LANES), F32)] * 2,
        compiler_params=_params(("parallel",)),
    )(pos_col, freq, sign, keep)


def _ada_fwd(c8, ada_w, ada_b):
    L, D, N = ada_w.shape

    def kern(c_ref, w_ref, b_ref, o_ref):
        act = _silu(c_ref[...]).astype(BF16)
        o_ref[0] = jnp.dot(act, w_ref[0], preferred_element_type=F32) + b_ref[0]

    return pl.pallas_call(
        kern, name="ada_fwd", grid=(L,),
        in_specs=[pl.BlockSpec((8, D), lambda l: (0, 0)), pl.BlockSpec((1, D, N), lambda l: (l, 0, 0)),
                  pl.BlockSpec((1, 1, N), lambda l: (l, 0, 0))],
        out_specs=pl.BlockSpec((1, 8, N), lambda l: (l, 0, 0)),
        out_shape=jax.ShapeDtypeStruct((L, 8, N), F32), compiler_params=_params(("parallel",)),
    )(c8, ada_w, ada_b)


def _ada_bwd(c8, dmod):
    L, _, N = dmod.shape
    D = c8.shape[1]

    def kern(c_ref, d_ref, gw_ref):
        act = _silu(c_ref[...]).astype(BF16)
        gw_ref[0] = lax.dot_general(act, d_ref[0].astype(BF16), _DIMS["tn"], preferred_element_type=F32)

    return pl.pallas_call(
        kern, name="ada_bwd", grid=(L,),
        in_specs=[pl.BlockSpec((8, D), lambda l: (0, 0)), pl.BlockSpec((1, 8, N), lambda l: (l, 0, 0))],
        out_specs=pl.BlockSpec((1, D, N), lambda l: (l, 0, 0)),
        out_shape=jax.ShapeDtypeStruct((L, D, N), F32), compiler_params=_params(("parallel",)),
    )(c8, dmod)


def _sum_examples(d):
    L, _, N = d.shape

    def kern(d_ref, o_ref):
        o_ref[...] = jnp.sum(d_ref[...], axis=1, keepdims=True)

    vm = pl.BlockSpec(memory_space=pltpu.VMEM)
    return pl.pallas_call(kern, name="sum_examples", in_specs=[vm], out_specs=vm,
                          out_shape=jax.ShapeDtypeStruct((L, 1, N), F32))(d)


def _adamw(w, ga, gb, m, v, name):
    rows, cols = w.shape
    tr = _row_div(rows, 256)
    two = gb is not None

    def kern(*refs):
        if two:
            w_ref, ga_ref, gb_ref, m_ref, v_ref, g_out, d_out, m_out, v_out = refs
            gv = ga_ref[...] + gb_ref[...]
        else:
            w_ref, ga_ref, m_ref, v_ref, g_out, d_out, m_out, v_out = refs
            gv = ga_ref[...]
        mn = ADAM_B1 * m_ref[...] + (1.0 - ADAM_B1) * gv
        vn = ADAM_B2 * v_ref[...] + (1.0 - ADAM_B2) * jnp.square(gv)
        m_hat = mn / (1.0 - ADAM_B1 ** ADAM_STEP)
        v_hat = vn / (1.0 - ADAM_B2 ** ADAM_STEP)
        g_out[...] = gv
        d_out[...] = -ADAM_LR * (m_hat / (jnp.sqrt(v_hat) + ADAM_EPS) + ADAM_WD * w_ref[...])
        m_out[...] = mn
        v_out[...] = vn

    spec = pl.BlockSpec((tr, cols), lambda i: (i, 0))
    args = [w, ga, gb, m, v] if two else [w, ga, m, v]
    return pl.pallas_call(
        kern, name=name, grid=(rows // tr,), in_specs=[spec] * len(args), out_specs=[spec] * 4,
        out_shape=[jax.ShapeDtypeStruct((rows, cols), F32)] * 4, compiler_params=_params(("parallel",)),
    )(*args)


def _sum_slots(x, name):
    n, rows, cols = x.shape
    tr = _row_div(rows, 256)

    def kern(x_ref, o_ref):
        acc = x_ref[0].astype(F32)
        for j in range(1, n):
            acc = acc + x_ref[j].astype(F32)
        o_ref[...] = acc

    return pl.pallas_call(
        kern, name=name, grid=(rows // tr,), in_specs=[pl.BlockSpec((n, tr, cols), lambda i: (0, i, 0))],
        out_specs=pl.BlockSpec((tr, cols), lambda i: (i, 0)), out_shape=jax.ShapeDtypeStruct((rows, cols), F32),
        compiler_params=_params(("parallel",)),
    )(x)


def _all_reduce_small(blob, name):
    R, C = blob.shape

    def kern(src, out, pair, chips, send_sems, recv_sems):
        x, y, c = lax.axis_index("x"), lax.axis_index("y"), lax.axis_index("c")
        me = 2 * x + y
        to_sibling = pltpu.make_async_remote_copy(
            src_ref=src, dst_ref=pair, send_sem=send_sems.at[0], recv_sem=recv_sems.at[0],
            device_id=(x, y, 1 - c), device_id_type=_MESH)
        to_sibling.start()
        to_sibling.wait()
        chips[me] = src[...] + pair[...]
        peers = [(1 - x, y), (x, 1 - y), (1 - x, 1 - y)]
        copies = [pltpu.make_async_remote_copy(
            src_ref=chips.at[me], dst_ref=chips.at[me], send_sem=send_sems.at[1 + kk], recv_sem=recv_sems.at[1 + kk],
            device_id=(px, py, c), device_id_type=_MESH) for kk, (px, py) in enumerate(peers)]
        for cp in copies:
            cp.start()
        for kk, (px, py) in enumerate(peers):
            pltpu.make_async_remote_copy(
                src_ref=chips.at[me], dst_ref=chips.at[2 * px + py], send_sem=send_sems.at[1 + kk],
                recv_sem=recv_sems.at[1 + kk], device_id=(px, py, c), device_id_type=_MESH).wait_recv()
        for cp in copies:
            cp.wait_send()
        acc = chips[0]
        for j in range(1, N_CHIPS):
            acc = acc + chips[j]
        out[...] = acc

    vm = pl.BlockSpec(memory_space=pltpu.VMEM)
    return pl.pallas_call(
        kern, name=name, in_specs=[vm], out_specs=vm, out_shape=jax.ShapeDtypeStruct((R, C), F32),
        scratch_shapes=[pltpu.VMEM((R, C), F32), pltpu.VMEM((N_CHIPS, R, C), F32), pltpu.SemaphoreType.DMA((4,)),
                        pltpu.SemaphoreType.DMA((4,))],
        compiler_params=pltpu.CompilerParams(vmem_limit_bytes=VMEM_LIMIT),
    )(blob)


def _row_tile(S, wide=False):
    for t in ((256,) if wide else (512, 256)):
        if S % t == 0:
            return t
    return 128


def _attn_tiles(S):
    return min(S, 512), min(S, 256), min(S, 256)


def _hosted(fn, host, got, kind, name, *args, **kw):
    if host is not None and host.get(name):
        res, xs = fn(*args, name=name, xchg=(kind, host[name]), **kw)
        got[name] = xs
        return res
    return fn(*args, name=name, **kw)


def _mm(host, got, kind, a, b, mode, out_dtype, name, **kw):
    return _hosted(_matmul, host, got, kind, name, a, b, mode, out_dtype, **kw)


def _merge_late(w, late, got, name):
    if not late or name not in late:
        return w
    return {**w, **_large_operands(dict(zip(late[name], got[name])))}


def _sgu_rows(S):
    return 4 * C_CHUNK if S % (4 * C_CHUNK) == 0 else C_CHUNK


def _layer_fwd(xin, prev, mods, w, tabs, B, S, host=None, late=None):
    T = B * S
    tr = _row_tile(S)
    sh1, sc1, g1, sh2, sc2, g2 = mods
    ca, sa, cb, sb = tabs
    sv = {}
    got = {}
    if prev is None:
        def body(rv, ev, pv):
            return [_f_norm_mod(rv[0], pv[0], ev[0], ev[1])], [], []
        (h,), _, _ = _rowcall(body, name="f_norm_mod", T=T, S=S, tr=tr, rows=[(xin, D_MODEL, 0)], exs=[sc1, sh1],
                              pars=[w["norm1_w"]], row_outs=[(D_MODEL, BF16, D_MODEL, 0)])
        x = xin
    else:
        x1p, dp, g2p = prev

        def body(rv, ev, pv):
            xn, hh = _f_resid_norm_mod(rv[0], rv[1], ev[0], pv[0], ev[1], ev[2])
            return [xn, hh], [], []
        (x, h), _, _ = _rowcall(body, name="f_resid_norm_mod1", T=T, S=S, tr=tr,
                                rows=[(x1p, D_MODEL, 0), (dp, D_MODEL, 0)], exs=[g2p, sc1, sh1], pars=[w["norm1_w"]],
                                row_outs=[(D_MODEL, F32, D_MODEL, 0), (D_MODEL, BF16, D_MODEL, 0)])
    sv["x"], sv["h"] = x, h
    proj = _mm(host, got, "gather", h, w["w_in"], "nn", BF16, "mm_in")
    sv["proj"] = proj
    w = _merge_late(w, late, got, "mm_in")

    def body(rv, ev, pv):
        outs = _f_mixprep(rv[0], rv[1], rv[2], rv[3], rv[4], pv[0], pv[1])
        return list(outs), [], []
    (qa, ka, va, cqn, ckvn, kr), _, _ = _rowcall(
        body, name="f_mixprep", T=T, S=S, tr=tr,
        rows=[(proj, P_MIX, 0), (ca, LANES, 0), (sa, LANES, 0), (cb, LANES, 0), (sb, LANES, 0)],
        pars=[w["b_q_norm_w"], w["b_kv_norm_w"]],
        row_outs=[(768, BF16, 768, 0), (256, BF16, 256, 0), (256, BF16, 256, 0), (384, BF16, 384, 0),
                  (256, BF16, 256, 0), (LANES, F32, LANES, 0)])
    sv.update(qa=qa, ka=ka, va=va, cqn=cqn, ckvn=ckvn)
    q = _matmul(cqn, w["b_w_uq"], "nn", BF16, "mm_uq")
    kv = _matmul(ckvn, w["b_w_ukv"], "nn", BF16, "mm_ukv")

    def body(rv, ev, pv):
        Q, K, V = _f_mlaprep(rv[0], rv[1], rv[2], rv[3], rv[4])
        return [Q, K, V], [], []
    (Q, K, V), _, _ = _rowcall(
        body, name="f_mlaprep", T=T, S=S, tr=tr,
        rows=[(q, 768, 0), (kv, 1536, 0), (kr, LANES, 0), (cb, LANES, 0), (sb, LANES, 0)],
        row_outs=[(768, BF16, 768, 0)] * 3)
    sv.update(Q=Q, K=K, V=V)
    ta, tb, band = _attn_tiles(S)
    oa, lse_a = _hosted(_attn_fwd, host, got, "gather", "attn_a_fwd", qa, ka, va, w["sinks"], B=B, S=S, HQ=A_Q_HEADS,
                        HK=A_KV_HEADS, window=A_WINDOW, scale=HEAD_DIM ** -0.5, tq=ta, band=None)
    ob, lse_b = _hosted(_attn_fwd, host, got, "gather", "attn_b_fwd", Q, K, V, None, B=B, S=S, HQ=B_HEADS, HK=B_HEADS,
                        window=None, scale=(B_NOPE + B_ROPE) ** -0.5, tq=tb, band=band)
    sv.update(oa=oa, lse_a=lse_a, ob=ob, lse_b=lse_b)
    w = _merge_late(w, late, got, "attn_a_fwd")
    ts = _sgu_rows(S)

    def body(rv, ev, pv):
        outs = [_f_sgu(rv[0][r:r + C_CHUNK], rv[1][r:r + C_CHUNK], pv[0], pv[1], pv[2], pv[3])
                for r in range(0, ts, C_CHUNK)]
        return [jnp.concatenate(outs, axis=0)], [], []
    (yc,), _, _ = _rowcall(body, name="f_sgu", T=T, S=S, tr=ts,
                           rows=[(proj, C_WIDTH, P_CU // C_WIDTH), (proj, C_WIDTH, P_CV // C_WIDTH)],
                           pars=[w["c_ln_w"], w["c_ln_b"], w["c_w_s"], w["c_b_col"]],
                           row_outs=[(C_WIDTH, F32, C_WIDTH, 0)])
    sv["yc"] = yc

    def body(rv, ev, pv):
        return [_f_outnorm(rv[0], rv[1], rv[2], pv[0])], [], []
    (y,), _, _ = _rowcall(body, name="f_outnorm", T=T, S=S, tr=tr,
                          rows=[(oa, 768, 0), (ob, 768, 0), (yc, C_WIDTH, 0)], pars=[w["out_norm_w"]],
                          row_outs=[(D_MODEL, BF16, D_MODEL, 0)])
    sv["y"] = y
    o = _mm(host, got, "gather", y, w["w_out"], "nn", BF16, "mm_out")
    sv["o"] = o

    def body(rv, ev, pv):
        xn, hh = _f_resid_norm_mod(rv[0], rv[1], ev[0], pv[0], ev[1], ev[2])
        return [xn, hh], [], []
    (x1, h2), _, _ = _rowcall(body, name="f_resid_norm_mod2", T=T, S=S, tr=tr,
                              rows=[(x, D_MODEL, 0), (o, D_MODEL, 0)], exs=[g1, sc2, sh2], pars=[w["norm2_w"]],
                              row_outs=[(D_MODEL, F32, D_MODEL, 0), (D_MODEL, BF16, D_MODEL, 0)])
    sv["h2"] = h2
    gu = _mm(host, got, "gather", h2, w["w_gate_up"], "nn", BF16, "mm_gate_up", b_chunks=True)
    sv["gu"] = gu

    def body(rv, ev, pv):
        return [_f_swiglu(rv[0], rv[1])], [], []
    (act,), _, _ = _rowcall(body, name="f_swiglu", T=T, S=S, tr=_row_tile(S, wide=True),
                            rows=[(gu, FFN_HIDDEN, 0), (gu, FFN_HIDDEN, 1)], row_outs=[(FFN_HIDDEN, BF16, FFN_HIDDEN, 0)])
    sv["act"] = act
    d = _mm(host, got, "gather", act, w["w_down"], "nn", BF16, "mm_down")
    sv["x1"], sv["d"], sv["w"] = x1, d, w
    return (x1, d, g2), sv, got


def _final(x1, d, g2, fw, target, B, S):
    T = B * S
    tr = _row_tile(S)

    def loss_fn(x1v, dv, gv, wv, tv):
        yf = _rms(x1v + gv * dv, wv)
        return 0.5 * jnp.sum(jnp.mean(jnp.square(yf - tv), axis=-1))

    def body(rv, ev, pv):
        x1v, dv, tv = rv
        val, vjp = jax.vjp(lambda a, b_, g, ww: loss_fn(a, b_, g, ww, tv), x1v, dv, ev[0], pv[0])
        dx1, dd, dg, dw = vjp(jnp.ones((), F32))
        return [dx1, dd], [dg], [dw, jnp.full((1, LANES), val, F32)]
    (dx1, dd), (dg2,), (dfw, loss) = _rowcall(
        body, name="final_loss", T=T, S=S, tr=tr, rows=[(x1, D_MODEL, 0), (d, D_MODEL, 0), (target, D_MODEL, 0)],
        exs=[g2], pars=[fw], row_outs=[(D_MODEL, F32, D_MODEL, 0), (D_MODEL, BF16, D_MODEL, 0)],
        ex_outs=[D_MODEL], par_outs=[(1, D_MODEL), (1, LANES)])
    return loss, dx1, dd, dg2, dfw


def _layer_bwd(sv, prev, mods, tabs, dx1, dd, B, S, host=None, own_early=False):
    w = sv["w"]
    T = B * S
    tr = _row_tile(S)
    sh1, sc1, g1, sh2, sc2, g2 = mods
    ca, sa, cb, sb = tabs
    gr = {}
    got = {}
    dact = _mm(host, got, "a2a", dd, w["w_down"], "nt", BF16, "mm_down_dx")
    gr["w_down"] = _matmul(sv["act"], dd, "tn", BF16, "mm_down_dw")

    def body(rv, ev, pv):
        _, vjp = jax.vjp(_f_swiglu, rv[0], rv[1])
        dgate, dup = vjp(rv[2])
        return [[dgate, dup]], [], []
    (dgu,), _, _ = _rowcall(body, name="b_swiglu", T=T, S=S, tr=_row_tile(S, wide=True),
                            rows=[(sv["gu"], FFN_HIDDEN, 0), (sv["gu"], FFN_HIDDEN, 1), (dact, FFN_HIDDEN, 0)],
                            row_outs=[(2 * FFN_HIDDEN, BF16, 2 * FFN_HIDDEN, 0)])
    dh2 = _mm(host, got, "a2a", dgu, w["w_gate_up"], "nt", BF16, "mm_gate_up_dx", b_chunks=True)
    gr["w_gate_up"] = _mm(host, got, "a2a", sv["h2"], dgu, "tn", BF16, "mm_gate_up_dw", out_chunks=True)

    def body(rv, ev, pv):
        xa, delta, dh, dxn = rv
        _, vjp = jax.vjp(_f_resid_norm_mod, xa, delta, ev[0], pv[0], ev[1], ev[2])
        dxa, ddelta, dg, dw, dsc, dsh = vjp((dxn, dh))
        return [dxa, ddelta], [dg, dsc, dsh], [dw]
    (dx, do), (dg1, dsc2, dsh2), (gr["norm2_w"],) = _rowcall(
        body, name="b_resid_norm_mod2", T=T, S=S, tr=tr,
        rows=[(sv["x"], D_MODEL, 0), (sv["o"], D_MODEL, 0), (dh2, D_MODEL, 0), (dx1, D_MODEL, 0)],
        exs=[g1, sc2, sh2], pars=[w["norm2_w"]],
        row_outs=[(D_MODEL, F32, D_MODEL, 0), (D_MODEL, BF16, D_MODEL, 0)], ex_outs=[D_MODEL] * 3,
        par_outs=[(1, D_MODEL)])
    dy = _matmul(do, w["w_out"], "nt", BF16, "mm_out_dx")
    gr["w_out"] = _matmul(sv["y"], do, "tn", BF16, "mm_out_dw")

    def body(rv, ev, pv):
        _, vjp = jax.vjp(_f_outnorm, rv[0], rv[1], rv[2], pv[0])
        doa, dob, dyc, dgw = vjp(rv[3])
        return [doa, dob, dyc], [], [dgw]
    (doa, dob, dyc), _, (gr["out_norm_w"],) = _rowcall(
        body, name="b_outnorm", T=T, S=S, tr=tr,
        rows=[(sv["oa"], 768, 0), (sv["ob"], 768, 0), (sv["yc"], C_WIDTH, 0), (dy, D_MODEL, 0)], pars=[w["out_norm_w"]],
        row_outs=[(768, BF16, 768, 0), (768, BF16, 768, 0), (C_WIDTH, F32, C_WIDTH, 0)], par_outs=[(1, D_MODEL)])

    ta, tb, band = _attn_tiles(S)
    if own_early:
        host = dict(host)
        host["attn_b_bwd"] = list(host.get("attn_b_bwd", ())) + [
            gr["w_gate_up"], gr["w_down"].reshape(N_CHIPS, FFN_HIDDEN // N_CHIPS, D_MODEL)]
    dQ, dK, dV, _ = _hosted(_attn_bwd, host, got, "a2a", "attn_b_bwd", sv["Q"], sv["K"], sv["V"], sv["ob"], sv["lse_b"],
                            dob, None, B=B, S=S, HQ=B_HEADS, HK=B_HEADS, window=None,
                            scale=(B_NOPE + B_ROPE) ** -0.5, tq=tb, band=band)
    dqa, dka, dva, dsink = _hosted(_attn_bwd, host, got, "a2a", "attn_a_bwd", sv["qa"], sv["ka"], sv["va"], sv["oa"],
                                   sv["lse_a"], doa, w["sinks"], B=B, S=S, HQ=A_Q_HEADS, HK=A_KV_HEADS,
                                   window=A_WINDOW, scale=HEAD_DIM ** -0.5, tq=ta, band=None)
    gr["sinks"] = dsink

    def body(rv, ev, pv):
        dQv, dKv, dVv, cbv, sbv = rv
        dq = [_rope_bwd((cbv, sbv), p)[0] for p in _heads(dQv, B_HEADS)]
        dkr = None
        for p in _heads(dKv, B_HEADS):
            dkr = p if dkr is None else dkr + p
        return [dq, [dKv, dVv], dkr], [], []
    (dq, dkv, dkr), _, _ = _rowcall(
        body, name="b_mlaprep", T=T, S=S, tr=tr,
        rows=[(dQ, 768, 0), (dK, 768, 0), (dV, 768, 0), (cb, LANES, 0), (sb, LANES, 0)],
        row_outs=[(768, BF16, 768, 0), (1536, BF16, 1536, 0), (LANES, F32, LANES, 0)])
    dcqn = _matmul(dq, w["b_w_uq"], "nt", BF16, "mm_uq_dx")
    gr["b_w_uq"] = _matmul(sv["cqn"], dq, "tn", BF16, "mm_uq_dw")
    dckvn = _matmul(dkv, w["b_w_ukv"], "nt", BF16, "mm_ukv_dx")
    gr["b_w_ukv"] = _matmul(sv["ckvn"], dkv, "tn", BF16, "mm_ukv_dw")

    def body(rv, ev, pv):
        proj, cav, sav, cbv, sbv, dqa_, dka_, dva_, dcqn_, dckvn_, dkr_ = rv
        _, vjp = jax.vjp(lambda p, a, b_: _f_mixprep(p, cav, sav, cbv, sbv, a, b_), proj, pv[0], pv[1])
        dproj, dqn, dkvn = vjp((dqa_, dka_, dva_, dcqn_, dckvn_, dkr_))
        return [[dproj, jnp.zeros((dproj.shape[0], P_CU - P_MIX), F32)]], [], [dqn, dkvn]
    (dproj,), _, (gr["b_q_norm_w"], gr["b_kv_norm_w"]) = _rowcall(
        body, name="b_mixprep", T=T, S=S, tr=tr,
        rows=[(sv["proj"], P_MIX, 0), (ca, LANES, 0), (sa, LANES, 0), (cb, LANES, 0), (sb, LANES, 0),
              (dqa, 768, 0), (dka, 256, 0), (dva, 256, 0), (dcqn, 384, 0), (dckvn, 256, 0), (dkr, LANES, 0)],
        pars=[w["b_q_norm_w"], w["b_kv_norm_w"]], row_outs=[(P_END, BF16, P_CU, 0)],
        par_outs=[(1, B_Q_RANK), (1, B_KV_RANK)])

    ts = _sgu_rows(S)

    def body(rv, ev, pv):
        cu, cv, dycv, _ = rv
        dcus, dcvs, acc = [], [], None
        for r in range(0, ts, C_CHUNK):
            _, vjp = jax.vjp(_f_sgu, cu[r:r + C_CHUNK], cv[r:r + C_CHUNK], pv[0], pv[1], pv[2], pv[3])
            dcu, dcv, *dpar = vjp(dycv[r:r + C_CHUNK])
            dcus.append(dcu)
            dcvs.append(dcv)
            acc = dpar if acc is None else [a + b_ for a, b_ in zip(acc, dpar)]
        return [[jnp.concatenate(dcus, axis=0), jnp.concatenate(dcvs, axis=0)]], [], acc
    (dproj,), _, (gr["c_ln_w"], gr["c_ln_b"], gr["c_w_s"], gr["c_b_col"]) = _rowcall(
        body, name="b_sgu", T=T, S=S, tr=ts,
        rows=[(sv["proj"], C_WIDTH, P_CU // C_WIDTH), (sv["proj"], C_WIDTH, P_CV // C_WIDTH), (dyc, C_WIDTH, 0),
              (dproj, 2 * C_WIDTH, P_CU // (2 * C_WIDTH))],
        pars=[w["c_ln_w"], w["c_ln_b"], w["c_w_s"], w["c_b_col"]],
        row_outs=[(P_END, BF16, 2 * C_WIDTH, P_CU // (2 * C_WIDTH))],
        par_outs=[(1, C_WIDTH), (1, C_WIDTH), (C_GROUPS, C_CHUNK, C_CHUNK), (C_GROUPS, C_CHUNK, 1)],
        aliases={3: 0})
    dh = _mm(host, got, "a2a", dproj, w["w_in"], "nt", BF16, "mm_in_dx")
    gr["w_in"] = _matmul(sv["h"], dproj, "tn", BF16, "mm_in_dw")

    if prev is None:
        def body(rv, ev, pv):
            xv, dhv, dxd = rv
            _, vjp = jax.vjp(_f_norm_mod, xv, pv[0], ev[0], ev[1])
            dxa, dw, dsc, dsh = vjp(dhv)
            return [dxa + dxd], [dsc, dsh], [dw]
        (dxin,), (dsc1, dsh1), (gr["norm1_w"],) = _rowcall(
            body, name="b_norm_mod", T=T, S=S, tr=tr, rows=[(sv["x"], D_MODEL, 0), (dh, D_MODEL, 0), (dx, D_MODEL, 0)],
            exs=[sc1, sh1], pars=[w["norm1_w"]], row_outs=[(D_MODEL, F32, D_MODEL, 0)], ex_outs=[D_MODEL] * 2,
            par_outs=[(1, D_MODEL)])
        nxt = (dxin, None, None)
    else:
        x1p, dp, g2p = prev

        def body(rv, ev, pv):
            xa, delta, dhv, dxn = rv
            _, vjp = jax.vjp(_f_resid_norm_mod, xa, delta, ev[0], pv[0], ev[1], ev[2])
            dxa, ddelta, dg, dw, dsc, dsh = vjp((dxn, dhv))
            return [dxa, ddelta], [dg, dsc, dsh], [dw]
        (dx1p, ddp), (dg2p, dsc1, dsh1), (gr["norm1_w"],) = _rowcall(
            body, name="b_resid_norm_mod1", T=T, S=S, tr=tr,
            rows=[(x1p, D_MODEL, 0), (dp, D_MODEL, 0), (dh, D_MODEL, 0), (dx, D_MODEL, 0)],
            exs=[g2p, sc1, sh1], pars=[w["norm1_w"]],
            row_outs=[(D_MODEL, F32, D_MODEL, 0), (D_MODEL, BF16, D_MODEL, 0)], ex_outs=[D_MODEL] * 3,
            par_outs=[(1, D_MODEL)])
        nxt = (dx1p, ddp, dg2p)
    return gr, (dsh1, dsc1, dg1, dsh2, dsc2), nxt, got


def _lane_table(lanes_neg, lanes_pos, inv):
    freq = np.zeros((LANES,), np.int64) - 1
    sign = np.zeros((1, LANES), np.float32)
    n = len(lanes_neg)
    freq[lanes_neg] = np.arange(n)
    freq[lanes_pos] = np.arange(n)
    sign[0, lanes_neg] = -1.0
    sign[0, lanes_pos] = 1.0
    return _select_axis(inv, freq, 0).reshape(1, LANES), jnp.asarray(sign)


SHARDED = ("ada_w", "w_in", "b_w_uq", "b_w_ukv", "w_out", "w_gate_up", "w_down")
ROW_SHARDED = ("w_out", "w_down")
SMALL = ("ada_b", "norm1_w", "a_sinks", "b_q_norm_w", "b_kv_norm_w", "c_ln_w", "c_ln_b", "c_w_s", "c_b_s",
         "out_norm_w", "norm2_w", "final_norm_w")
FWD_HOST = {"attn_b_fwd": ("w_gate_up", "w_down"), "attn_a_fwd": ("w_in", "w_out", "b_w_uq", "b_w_ukv")}
FWD_HOST_FIRST = {"attn_b_fwd": ("w_gate_up", "w_down"), "mm_gate_up": ("w_in", "w_out", "b_w_uq", "b_w_ukv")}
BWD_HOST = {"attn_a_bwd": ("w_in", "w_out", "b_w_uq", "b_w_ukv"), "attn_b_bwd": ("w_gate_up", "w_down")}
FIRST_LATE = {"mm_in": ("b_w_uq", "b_w_ukv", "w_out"), "attn_a_fwd": ("w_gate_up", "w_down")}
EXPOSED = ("w_in", "b_w_uq", "b_w_ukv", "w_out")
OWN_LAYER = ("w_gate_up", "w_down")
EXCHANGED = SHARDED[1:]
N_ADA = N_MOD * D_MODEL // N_CHIPS
assert not set(FIRST_LATE) & set(FWD_HOST_FIRST)


def _from_host(table, got):
    return {k: got[name][i] for name, ks in table.items() for i, k in enumerate(ks)}


def _to_host(table, arrays):
    return {name: [arrays[k] for k in ks] for name, ks in table.items()}


def _join_cols(g):
    return jnp.concatenate([g[j] for j in range(N_CHIPS)], axis=1)


def _split_cols(g):
    n = g.shape[1] // N_CHIPS
    return jnp.stack([g[:, j * n:(j + 1) * n] for j in range(N_CHIPS)])


def _layer_weights(G, small, l):
    D = D_MODEL
    return {
        **_large_operands(G),
        "norm1_w": small["norm1_w"][l].reshape(1, D),
        "sinks": jnp.broadcast_to(small["a_sinks"][l].reshape(A_Q_HEADS, 1, 1), (A_Q_HEADS, 1, LANES)),
        "b_q_norm_w": small["b_q_norm_w"][l].reshape(1, B_Q_RANK),
        "b_kv_norm_w": small["b_kv_norm_w"][l].reshape(1, B_KV_RANK),
        "c_ln_w": small["c_ln_w"][l].reshape(1, C_WIDTH), "c_ln_b": small["c_ln_b"][l].reshape(1, C_WIDTH),
        "c_w_s": small["c_w_s"][l], "c_b_col": small["c_b_s"][l].reshape(C_GROUPS, C_CHUNK, 1),
        "out_norm_w": small["out_norm_w"][l].reshape(1, D), "norm2_w": small["norm2_w"][l].reshape(1, D),
    }


def _large_operands(G):
    D = D_MODEL
    make = {
        "w_in": lambda g: _pad_axis(_join_cols(g), _map_w_in(), 1),
        "b_w_uq": lambda g: _pad_axis(_join_cols(g), _map_w_uq(), 1),
        "b_w_ukv": lambda g: _pad_axis(_join_cols(g), _map_w_ukv(), 1),
        "w_out": lambda g: g.reshape(D, D), "w_gate_up": lambda g: g, "w_down": lambda g: g.reshape(FFN_HIDDEN, D),
    }
    return {k: make[k](g) for k, g in G.items()}


def _send_buffers(gr):
    D = D_MODEL
    return {
        "w_gate_up": gr["w_gate_up"], "w_down": gr["w_down"].reshape(N_CHIPS, FFN_HIDDEN // N_CHIPS, D),
        "w_out": gr["w_out"].reshape(N_CHIPS, D // N_CHIPS, D),
        "w_in": _split_cols(_unpad_axis(gr["w_in"], _map_w_in(), IN_COLS, 1)),
        "b_w_uq": _split_cols(_unpad_axis(gr["b_w_uq"], _map_w_uq(), B_HEADS * (B_NOPE + B_ROPE), 1)),
        "b_w_ukv": _split_cols(_unpad_axis(gr["b_w_ukv"], _map_w_ukv(), B_HEADS * (B_NOPE + B_V), 1)),
    }


def _small_grads(gr, B):
    D = D_MODEL
    return {
        "norm1_w": gr["norm1_w"].reshape(D),
        "a_sinks": gr["sinks"][:, 0, 0].reshape(B, A_Q_HEADS).sum(axis=0),
        "b_q_norm_w": gr["b_q_norm_w"].reshape(B_Q_RANK), "b_kv_norm_w": gr["b_kv_norm_w"].reshape(B_KV_RANK),
        "c_ln_w": gr["c_ln_w"].reshape(C_WIDTH), "c_ln_b": gr["c_ln_b"].reshape(C_WIDTH), "c_w_s": gr["c_w_s"],
        "c_b_s": gr["c_b_col"].reshape(C_GROUPS, C_CHUNK),
        "out_norm_w": gr["out_norm_w"].reshape(D), "norm2_w": gr["norm2_w"].reshape(D),
    }


def _step(x, c, positions, target, small, shard_of, ada=None, gathered=None):
    dist = gathered is None
    B, S, D = x.shape
    T = B * S
    xt = x.reshape(T, D)
    tgt = target.reshape(T, D)
    pos_col = positions.astype(F32).reshape(T, 1)
    inv_a = 1.0 / (ROPE_THETA ** (jnp.arange(0, HEAD_DIM, 2, dtype=F32) / HEAD_DIM))
    inv_b = 1.0 / (ROPE_THETA ** (jnp.arange(0, B_ROPE, 2, dtype=F32) / B_ROPE))
    fa, sga = _lane_table(np.arange(32), 64 + np.arange(32), inv_a)
    fb, sgb = _lane_table(48 + np.arange(16), 112 + np.arange(16), inv_b)
    ca, sa = _rope_tables(pos_col, fa, sga, jnp.abs(sga), "rope_a")
    cb, sb = _rope_tables(pos_col, fb, sgb, jnp.ones_like(sgb), "rope_b")
    tabs = (ca, sa, cb, sb)
    ada_b = small["ada_b"]
    if dist:
        assert N_CHIPS * B == 8
        me = 2 * lax.axis_index("x") + lax.axis_index("y")
        first = shard_of(0)
        c_all, w_in_first = _xchg_call("gather", [c, first["w_in"]], "gather_first")
        c8 = c_all.reshape(N_CHIPS * B, D)
        mine = lax.dynamic_slice_in_dim(ada_b, me * N_ADA, N_ADA, axis=1).reshape(DEPTH, 1, N_ADA)
        part = _ada_fwd(c8, ada, mine)
        part = part.reshape(DEPTH, N_CHIPS, B, N_ADA).transpose(1, 0, 2, 3).reshape(N_CHIPS, DEPTH * B, N_ADA)
        (back,) = _xchg_call("a2a", [part], "mod_exchange")
        mod_all = jnp.concatenate([back[j].reshape(DEPTH, B, N_ADA) for j in range(N_CHIPS)], axis=-1)
        G = {"w_in": w_in_first}
    else:
        c8 = jnp.zeros((8, D), F32).at[:B].set(c)
        cols = [(jnp.stack([gathered[l]["ada_w"][j] for l in range(DEPTH)]),
                 ada_b[:, j * N_ADA:(j + 1) * N_ADA].reshape(DEPTH, 1, N_ADA)) for j in range(N_CHIPS)]
        mod_all = jnp.concatenate([_ada_fwd(c8, wj, bj)[:, :B] for wj, bj in cols], axis=-1)
        G = {k: v for k, v in gathered[0].items() if k != "ada_w"}
    saved, prevs, modss = [], [], []
    prev = None
    for l in range(DEPTH):
        w = _layer_weights(G, small, l)
        mods = tuple(mod_all[l, :, i * D:(i + 1) * D].reshape(B, 1, D) for i in range(N_MOD))
        more = l + 1 < DEPTH
        table = FWD_HOST_FIRST if l == 0 else FWD_HOST
        host = _to_host(table, shard_of(l + 1)) if dist and more else {}
        late = None
        if dist and l == 0:
            late = FIRST_LATE
            for name, ks in late.items():
                host[name] = [first[k] for k in ks] + host.get(name, [])
        prevs.append(prev)
        modss.append(mods)
        prev, sv, got = _layer_fwd(xt, prev, mods, w, tabs, B, S, host, late)
        saved.append(sv)
        if more:
            G = _from_host(table, got) if dist else {k: v for k, v in gathered[l + 1].items() if k != "ada_w"}
    x1, d, g2 = prev
    loss, dx1, dd, dg2, dfw = _final(x1, d, g2, small["final_norm_w"].reshape(1, D), tgt, B, S)

    landed = [None] * DEPTH
    smalls = [None] * DEPTH
    dmods = [None] * DEPTH
    pending = None
    for l in reversed(range(DEPTH)):
        host = _to_host(BWD_HOST, pending) if dist and pending is not None else None
        early = dist and l == 0 and host is not None
        gr, (dsh1, dsc1, dg1, dsh2, dsc2), nxt, got = _layer_bwd(saved[l], prevs[l], modss[l], tabs, dx1, dd, B, S, host, early)
        if pending is not None:
            landed[l + 1] = _from_host(BWD_HOST, got) if dist else pending
        dmods[l] = jnp.concatenate([dsh1, dsc1, dg1, dsh2, dsc2, dg2], axis=-1).reshape(B, N_MOD * D)
        pending = _send_buffers(gr)
        smalls[l] = _small_grads(gr, B)
        dx1, dd, dg2 = nxt
    dmod_all = jnp.stack(dmods)
    by_chip = [dmod_all[:, :, j * N_ADA:(j + 1) * N_ADA] for j in range(N_CHIPS)]
    if dist:
        last = EXPOSED if early else EXCHANGED
        send = jnp.stack([p.reshape(DEPTH * B, N_ADA) for p in by_chip])
        *res, back = _xchg_call("a2a", [pending[k] for k in last] + [send], "grad_exchange_last")
        landed[0] = dict(zip(last, res))
        if early:
            landed[0].update(zip(OWN_LAYER, got["attn_b_bwd"][-len(OWN_LAYER):]))
        dmod8 = back.reshape(N_CHIPS, DEPTH, B, N_ADA).transpose(1, 0, 2, 3).reshape(DEPTH, N_CHIPS * B, N_ADA)
        ada_g = _ada_bwd(c8, dmod8)
    else:
        landed[0] = pending
        ada_g = jnp.stack([_ada_bwd(c8, jnp.zeros((DEPTH, 8, N_ADA), F32).at[:, :B].set(p)) for p in by_chip])
    small_g = {k: jnp.stack([smalls[l][k] for l in range(DEPTH)]) for k in SMALL if k not in ("final_norm_w", "ada_b")}
    small_g["ada_b"] = _sum_examples(dmod_all).reshape(DEPTH, N_MOD * D)
    small_g["final_norm_w"] = dfw.reshape(D)
    return loss[0, 0], dx1.reshape(B, S, D), landed, small_g, ada_g


def _pack(arrs, cols, mult):
    flat = jnp.concatenate([a.reshape(-1) for a in arrs])
    n = flat.shape[0]
    rows = -(-n // cols)
    rows = -(-rows // mult) * mult
    return jnp.pad(flat, (0, rows * cols - n)).reshape(rows, cols)


def _unpack(blob, shapes):
    flat = blob.reshape(-1)
    out, off = [], 0
    for s in shapes:
        n = int(np.prod(s))
        out.append(flat[off:off + n].reshape(s))
        off += n
    return out


def kernel(x, c, positions, ada_w, ada_b, norm1_w, w_in, a_sinks, b_q_norm_w, b_w_uq, b_kv_norm_w, b_w_ukv, c_ln_w, c_ln_b, c_w_s, c_b_s, out_norm_w, w_out, norm2_w, w_gate_up, w_down, final_norm_w, loss_target, m_ada_w, m_ada_b, m_norm1_w, m_w_in, m_a_sinks, m_b_q_norm_w, m_b_w_uq, m_b_kv_norm_w, m_b_w_ukv, m_c_ln_w, m_c_ln_b, m_c_w_s, m_c_b_s, m_out_norm_w, m_w_out, m_norm2_w, m_w_gate_up, m_w_down, m_final_norm_w, v_ada_w, v_ada_b, v_norm1_w, v_w_in, v_a_sinks, v_b_q_norm_w, v_b_w_uq, v_b_kv_norm_w, v_b_w_ukv, v_c_ln_w, v_c_ln_b, v_c_w_s, v_c_b_s, v_out_norm_w, v_w_out, v_norm2_w, v_w_gate_up, v_w_down, v_final_norm_w):
    names = ("ada_w", "ada_b", "norm1_w", "w_in", "a_sinks", "b_q_norm_w", "b_w_uq", "b_kv_norm_w", "b_w_ukv", "c_ln_w",
             "c_ln_b", "c_w_s", "c_b_s", "out_norm_w", "w_out", "norm2_w", "w_gate_up", "w_down", "final_norm_w")
    ws = dict(zip(names, (ada_w, ada_b, norm1_w, w_in, a_sinks, b_q_norm_w, b_w_uq, b_kv_norm_w, b_w_ukv, c_ln_w, c_ln_b,
                          c_w_s, c_b_s, out_norm_w, w_out, norm2_w, w_gate_up, w_down, final_norm_w)))
    ms = dict(zip(names, (m_ada_w, m_ada_b, m_norm1_w, m_w_in, m_a_sinks, m_b_q_norm_w, m_b_w_uq, m_b_kv_norm_w, m_b_w_ukv,
                          m_c_ln_w, m_c_ln_b, m_c_w_s, m_c_b_s, m_out_norm_w, m_w_out, m_norm2_w, m_w_gate_up, m_w_down,
                          m_final_norm_w)))
    vs = dict(zip(names, (v_ada_w, v_ada_b, v_norm1_w, v_w_in, v_a_sinks, v_b_q_norm_w, v_b_w_uq, v_b_kv_norm_w, v_b_w_ukv,
                          v_c_ln_w, v_c_ln_b, v_c_w_s, v_c_b_s, v_out_norm_w, v_w_out, v_norm2_w, v_w_gate_up, v_w_down,
                          v_final_norm_w)))
    shards = {k: ws[k].astype(BF16) for k in SHARDED}
    loss_local, grad_x, landed, gsmall, ada_g = _step(
        x, c, positions, loss_target, {k: ws[k] for k in SMALL}, lambda l: {k: shards[k][l] for k in EXCHANGED},
        ada=shards["ada_w"])

    mine = {k: jnp.stack([_sum_slots(landed[l][k], "grad_sum_" + k) for l in range(DEPTH)]) for k in EXCHANGED}
    mine["ada_w"] = ada_g
    theirs = dict(zip(SHARDED, _xchg_call("swap", [mine[k] for k in SHARDED], "grad_sibling_swap")))
    grads, delta, new_m, new_v = {}, {}, {}, {}
    for k in SHARDED:
        shp = ws[k].shape
        two = (shp[0] * shp[1], shp[2])
        g, dlt, nm, nv = _adamw(ws[k].reshape(two), mine[k].reshape(two), theirs[k].reshape(two), ms[k].reshape(two),
                                vs[k].reshape(two), "adamw_" + k)
        grads[k], delta[k], new_m[k], new_v[k] = g.reshape(shp), dlt.reshape(shp), nm.reshape(shp), nv.reshape(shp)

    small_shapes = [ws[k].shape for k in SMALL]
    sblob = _pack([gsmall[k] for k in SMALL] + [loss_local.reshape(1)], LANES, 8)
    svals = _unpack(_all_reduce_small(sblob, "small_all_reduce"), small_shapes + [(1,)])
    loss = svals[-1].reshape(())
    pw = _pack([ws[k] for k in SMALL], LANES, 8)
    pg = _pack(svals[:-1], LANES, 8)
    pm = _pack([ms[k] for k in SMALL], LANES, 8)
    pv = _pack([vs[k] for k in SMALL], LANES, 8)
    g, dlt, nm, nv = _adamw(pw, pg, None, pm, pv, "adamw_small")
    for k, a, b_, c_, d_ in zip(SMALL, _unpack(g, small_shapes), _unpack(dlt, small_shapes), _unpack(nm, small_shapes),
                                _unpack(nv, small_shapes)):
        grads[k], delta[k], new_m[k], new_v[k] = a, b_, c_, d_

    return (loss, grad_x, *[grads[k] for k in names], *[delta[k] for k in names], *[new_m[k] for k in names],
            *[new_v[k] for k in names])
```

```python
import functools
import math

import numpy as np
import jax
import jax.numpy as jnp
from jax import lax
from jax.experimental import pallas as pl
from jax.experimental.pallas import tpu as pltpu

F32 = jnp.float32
BF16 = jnp.bfloat16

D_MODEL = 1024
DEPTH = 4
HEAD_DIM = 64
ROPE_THETA = 10000.0
NORM_EPS = 1e-6
NEG_INF = -1e30
LOG2_E = math.log2(math.e)
A_Q_HEADS = 6
A_KV_HEADS = 2
A_WINDOW = 128
B_HEADS = 6
B_Q_RANK = 384
B_KV_RANK = 256
B_NOPE = 64
B_ROPE = 32
B_V = 64
C_GROUPS = 4
C_GROUP_DIM = 64
C_WIDTH = 256
C_CHUNK = 128
IN_COLS = 1824
FFN_HIDDEN = 2816
N_MOD = 6
ADAM_LR = 0.001
ADAM_B1 = 0.9
ADAM_B2 = 0.999
ADAM_EPS = 1e-08
ADAM_WD = 0.01
ADAM_STEP = 10

LANES = 128
VMEM_LIMIT = 56 * 1024 * 1024
N_CHIPS = 4
WINDOW_SUB = 256

P_KR, P_AQ, P_AK, P_AV, P_CQ, P_CKV, P_MIX, P_CU, P_CV, P_END = 0, 128, 512, 640, 768, 1152, 1408, 1536, 1792, 2048


def _map_w_in():
    idx = -np.ones(P_END, np.int64)
    idx[P_AQ:P_AQ + 384] = np.arange(384)
    idx[P_AK:P_AK + 128] = 384 + np.arange(128)
    idx[P_AV:P_AV + 128] = 512 + np.arange(128)
    idx[P_CQ:P_CQ + 384] = 640 + np.arange(384)
    idx[P_CKV:P_CKV + 256] = 1024 + np.arange(256)
    idx[P_KR + 48 + np.arange(16)] = 1280 + np.arange(16)
    idx[P_KR + 112 + np.arange(16)] = 1296 + np.arange(16)
    idx[P_CU:P_CU + 256] = 1312 + np.arange(256)
    idx[P_CV:P_CV + 256] = 1568 + np.arange(256)
    return idx


def _map_w_uq():
    idx = -np.ones(B_HEADS * LANES, np.int64)
    for h in range(B_HEADS):
        b = h * (B_NOPE + B_ROPE)
        idx[h * LANES + np.arange(48)] = b + np.arange(48)
        idx[h * LANES + 48 + np.arange(16)] = b + 64 + np.arange(16)
        idx[h * LANES + 64 + np.arange(16)] = b + 48 + np.arange(16)
        idx[h * LANES + 112 + np.arange(16)] = b + 80 + np.arange(16)
    return idx


def _map_w_ukv():
    idx = -np.ones(2 * B_HEADS * LANES, np.int64)
    for h in range(B_HEADS):
        b = h * (B_NOPE + B_V)
        idx[h * LANES + np.arange(48)] = b + np.arange(48)
        idx[h * LANES + 64 + np.arange(16)] = b + 48 + np.arange(16)
        idx[B_HEADS * LANES + h * LANES + np.arange(B_V)] = b + B_NOPE + np.arange(B_V)
    return idx


def _inverse(idx, n):
    inv = np.zeros(n, np.int64)
    pos = np.nonzero(idx >= 0)[0]
    inv[idx[pos]] = pos
    return inv


def _runs(idx):
    runs, i, n = [], 0, len(idx)
    while i < n:
        j = i + 1
        while j < n and ((idx[i] < 0 and idx[j] < 0) or (idx[i] >= 0 and idx[j] == idx[i] + (j - i))):
            j += 1
        runs.append((int(idx[i]), j - i))
        i = j
    return runs


def _select_axis(w, idx, axis):
    pieces = []
    for start, length in _runs(idx):
        if start < 0:
            shape = list(w.shape)
            shape[axis] = length
            pieces.append(jnp.zeros(shape, w.dtype))
        else:
            pieces.append(lax.slice_in_dim(w, start, start + length, axis=axis))
    return jnp.concatenate(pieces, axis=axis)


def _pad_axis(w, idx, axis):
    return _select_axis(w, idx, axis)


def _unpad_axis(g, idx, n, axis):
    return _select_axis(g, _inverse(idx, n), axis)


def _params(sem):
    return pltpu.CompilerParams(dimension_semantics=sem, vmem_limit_bytes=VMEM_LIMIT)


def _tile(dim, target):
    if dim <= target:
        return dim
    best = None
    for t in range(LANES, target + 1, LANES):
        if dim % t == 0:
            best = t
    assert best is not None, dim
    return best


def _row_div(rows, target):
    if rows <= target:
        return rows
    best = None
    for t in range(8, target + 1, 8):
        if rows % t == 0:
            best = t
    assert best is not None, rows
    return best


_ANY = pl.BlockSpec(memory_space=pl.ANY)
_MESH = pl.DeviceIdType.MESH


def _xchg_out_shapes(kind, srcs):
    if kind == "gather":
        return [jax.ShapeDtypeStruct((N_CHIPS,) + s.shape, s.dtype) for s in srcs]
    return [jax.ShapeDtypeStruct(s.shape, s.dtype) for s in srcs]


def _xchg_scratch(kind, n):
    per = 1 if kind == "swap" else N_CHIPS - 1
    return [pltpu.SemaphoreType.DMA((per * n,)), pltpu.SemaphoreType.DMA((per * n,)), pltpu.SemaphoreType.DMA((n,))]


def _xchg_copies(kind, srcs, dsts, send_sems, recv_sems, local_sems, arrivals):
    x, y, c = lax.axis_index("x"), lax.axis_index("y"), lax.axis_index("c")
    me = 2 * x + y
    peers = [(1 - x, y), (x, 1 - y), (1 - x, 1 - y)]
    local, out, back = [], [], []
    for i, (s, d) in enumerate(zip(srcs, dsts)):
        if kind == "swap":
            cp = pltpu.make_async_remote_copy(src_ref=s, dst_ref=d, send_sem=send_sems.at[i], recv_sem=recv_sems.at[i],
                                              device_id=(x, y, 1 - c), device_id_type=_MESH)
            out.append(cp)
            back.append(cp)
            continue
        local.append(pltpu.make_async_copy(s if kind == "gather" else s.at[me], d.at[me], local_sems.at[i]))
        for kk, (px, py) in enumerate(peers):
            j = (N_CHIPS - 1) * i + kk
            theirs = 2 * px + py
            out.append(pltpu.make_async_remote_copy(
                src_ref=s if kind == "gather" else s.at[theirs], dst_ref=d.at[me], send_sem=send_sems.at[j],
                recv_sem=recv_sems.at[j], device_id=(px, py, c), device_id_type=_MESH))
            if arrivals:
                back.append(pltpu.make_async_remote_copy(
                    src_ref=s if kind == "gather" else s.at[me], dst_ref=d.at[theirs], send_sem=send_sems.at[j],
                    recv_sem=recv_sems.at[j], device_id=(px, py, c), device_id_type=_MESH))
    return local, out, back


def _xchg_start(kind, srcs, dsts, sems):
    local, out, _ = _xchg_copies(kind, srcs, dsts, *sems, arrivals=False)
    for cp in local + out:
        cp.start()


def _xchg_wait(kind, srcs, dsts, sems):
    local, out, back = _xchg_copies(kind, srcs, dsts, *sems, arrivals=True)
    for cp in back:
        cp.wait_recv()
    for cp in out:
        cp.wait_send()
    for cp in local:
        cp.wait()


def _xchg_at_ends(kind, srcs, dsts, sems, grid, first):
    ids = [pl.program_id(a) for a in range(len(grid))]
    cond = None
    for i, n in zip(ids, grid):
        c = (i == 0) if first else (i == n - 1)
        cond = c if cond is None else jnp.logical_and(cond, c)

    @pl.when(cond)
    def _():
        (_xchg_start if first else _xchg_wait)(kind, srcs, dsts, sems)


def _xchg_call(kind, srcs, name):
    n = len(srcs)

    def kern(*refs):
        s, d, sems = refs[:n], refs[n:2 * n], refs[2 * n:]
        _xchg_start(kind, s, d, sems)
        _xchg_wait(kind, s, d, sems)

    return pl.pallas_call(
        kern, name=name, in_specs=[_ANY] * n, out_specs=[_ANY] * n, out_shape=_xchg_out_shapes(kind, srcs),
        scratch_shapes=_xchg_scratch(kind, n),
    )(*srcs)


_DIMS = {"nn": (((1,), (0,)), ((), ())), "nt": (((1,), (1,)), ((), ())), "tn": (((0,), (0,)), ((), ()))}


def _matmul(a, b, mode, out_dtype, name, *, b_chunks=False, out_chunks=False, xchg=None):
    if b_chunks:
        nchunk, brows, bcols = b.shape
        bshape = (brows, nchunk * bcols)
    else:
        bshape = b.shape
    if mode == "nn":
        (m, k), (_, n) = a.shape, bshape
    elif mode == "nt":
        (m, k), (n, _) = a.shape, bshape
    else:
        (k, m), (_, n) = a.shape, bshape
    tm, tk = (1408, 1024) if mode == "tn" else (1024, 1408)
    tm, tn, tk = _tile(m, tm), _tile(n, 1408), _tile(k, tk)
    if b_chunks:
        if mode == "nn":
            tn = bcols
        else:
            assert mode == "nt"
            tk = bcols
    if out_chunks:
        assert n % N_CHIPS == 0
        tn = n // N_CHIPS
    ni, nj, nk = m // tm, n // tn, k // tk
    dims = _DIMS[mode]
    n_x = 0 if xchg is None else len(xchg[1])

    def kern(*refs):
        a_ref, b_ref = refs[0], refs[1]
        xs = refs[2:2 + n_x]
        o_ref = refs[2 + n_x]
        xd = refs[3 + n_x:3 + 2 * n_x]
        acc_ref = refs[3 + 2 * n_x]
        sems = refs[4 + 2 * n_x:]
        kk = pl.program_id(2)
        if n_x:
            _xchg_at_ends(xchg[0], xs, xd, sems, (ni, nj, nk), True)

        @pl.when(kk == 0)
        def _():
            acc_ref[...] = jnp.zeros_like(acc_ref)

        acc_ref[...] += lax.dot_general(a_ref[...], b_ref[...], dims, preferred_element_type=F32)

        @pl.when(kk == nk - 1)
        def _():
            o_ref[...] = acc_ref[...].astype(o_ref.dtype)

        if n_x:
            _xchg_at_ends(xchg[0], xs, xd, sems, (ni, nj, nk), False)

    if mode == "tn":
        a_spec = pl.BlockSpec((tk, tm), lambda i, j, kk: (kk, i))
    else:
        a_spec = pl.BlockSpec((tm, tk), lambda i, j, kk: (i, kk))
    if b_chunks and mode == "nn":
        b_spec = pl.BlockSpec((None, tk, tn), lambda i, j, kk: (j, kk, 0))
    elif b_chunks:
        b_spec = pl.BlockSpec((None, tn, tk), lambda i, j, kk: (kk, j, 0))
    elif mode == "nt":
        b_spec = pl.BlockSpec((tn, tk), lambda i, j, kk: (j, kk))
    else:
        b_spec = pl.BlockSpec((tk, tn), lambda i, j, kk: (kk, j))
    if out_chunks:
        o_spec = pl.BlockSpec((None, tm, tn), lambda i, j, kk: (j, i, 0))
        o_shape = jax.ShapeDtypeStruct((N_CHIPS, m, tn), out_dtype)
    else:
        o_spec = pl.BlockSpec((tm, tn), lambda i, j, kk: (i, j))
        o_shape = jax.ShapeDtypeStruct((m, n), out_dtype)
    xs = [] if xchg is None else list(xchg[1])
    res = pl.pallas_call(
        kern, name=name, grid=(ni, nj, nk),
        in_specs=[a_spec, b_spec] + [_ANY] * n_x, out_specs=[o_spec] + [_ANY] * n_x,
        out_shape=[o_shape] + (_xchg_out_shapes(xchg[0], xs) if n_x else []),
        scratch_shapes=[pltpu.VMEM((tm, tn), F32)] + (_xchg_scratch(xchg[0], n_x) if n_x else []),
        compiler_params=_params(("arbitrary", "arbitrary", "arbitrary") if n_x else ("parallel", "parallel", "arbitrary")),
    )(a, b, *xs)
    return (res[0], list(res[1:])) if n_x else res[0]


def _rowcall(body, *, name, T, S, tr, rows, exs=(), pars=(), row_outs=(), ex_outs=(), par_outs=(), aliases=None):
    assert S % tr == 0 and T % S == 0
    per_ex = S // tr
    nb = T // S
    n_rows, n_exs, n_pars = len(rows), len(exs), len(pars)
    n_ro, n_eo, n_po = len(row_outs), len(ex_outs), len(par_outs)

    def kern(*refs):
        ins = refs[:n_rows + n_exs + n_pars]
        outs = refs[n_rows + n_exs + n_pars:]
        rv = [r[...].astype(F32) for r in ins[:n_rows]]
        ev = [r[0] for r in ins[n_rows:n_rows + n_exs]]
        pv = [r[...] for r in ins[n_rows + n_exs:]]
        ro, eo, po = body(rv, ev, pv)
        i = pl.program_id(0)
        for ref, val in zip(outs[:n_ro], ro):
            if isinstance(val, (list, tuple)):
                off = 0
                for piece in val:
                    w = piece.shape[-1]
                    ref[:, off:off + w] = piece.astype(ref.dtype)
                    off += w
            else:
                ref[...] = val.astype(ref.dtype)
        first_of_ex = (i % per_ex) == 0
        for ref, val in zip(outs[n_ro:n_ro + n_eo], eo):
            @pl.when(first_of_ex)
            def _(ref=ref, val=val):
                ref[0] = val

            @pl.when(jnp.logical_not(first_of_ex))
            def _(ref=ref, val=val):
                ref[0] += val
        for ref, val in zip(outs[n_ro + n_eo:], po):
            @pl.when(i == 0)
            def _(ref=ref, val=val):
                ref[...] = val

            @pl.when(i != 0)
            def _(ref=ref, val=val):
                ref[...] += val

    in_specs = [pl.BlockSpec((tr, w), functools.partial(lambda i, cb: (i, cb), cb=cb)) for (_, w, cb) in rows]
    in_specs += [pl.BlockSpec((1, 1, e.shape[-1]), lambda i: (i // per_ex, 0, 0)) for e in exs]
    in_specs += [pl.BlockSpec(p.shape, functools.partial(lambda i, nd: (0,) * nd, nd=p.ndim)) for p in pars]
    out_specs = [pl.BlockSpec((tr, w), functools.partial(lambda i, cb: (i, cb), cb=cb)) for (_, _, w, cb) in row_outs]
    out_specs += [pl.BlockSpec((1, 1, f), lambda i: (i // per_ex, 0, 0)) for f in ex_outs]
    out_specs += [pl.BlockSpec(tuple(s), functools.partial(lambda i, nd: (0,) * nd, nd=len(s))) for s in par_outs]
    out_shape = [jax.ShapeDtypeStruct((T, tw), dt) for (tw, dt, _, _) in row_outs]
    out_shape += [jax.ShapeDtypeStruct((nb, 1, f), F32) for f in ex_outs]
    out_shape += [jax.ShapeDtypeStruct(tuple(s), F32) for s in par_outs]
    res = pl.pallas_call(
        kern, name=name, grid=(T // tr,), in_specs=in_specs, out_specs=out_specs, out_shape=out_shape,
        input_output_aliases=aliases or {}, compiler_params=_params(("arbitrary",)),
    )(*[r[0] for r in rows], *exs, *pars)
    return res[:n_ro], res[n_ro:n_ro + n_eo], res[n_ro + n_eo:]


def _rms(x, w, n=None):
    n = x.shape[-1] if n is None else n
    ms = jnp.sum(x * x, axis=-1, keepdims=True) * (1.0 / n)
    return x * lax.rsqrt(ms + NORM_EPS) * w


def _gelu(x):
    return 0.5 * x * (1.0 + lax.erf(x * np.float32(1.0 / math.sqrt(2.0))))


def _silu(x):
    return x * jax.nn.sigmoid(x)


@jax.custom_vjp
def _rope(x, cos, sin):
    return x * cos + pltpu.roll(x, 64, 1) * sin


def _rope_fwd(x, cos, sin):
    return _rope(x, cos, sin), (cos, sin)


def _rope_bwd(res, dy):
    cos, sin = res
    return dy * cos + pltpu.roll(dy * sin, 64, 1), None, None


_rope.defvjp(_rope_fwd, _rope_bwd)


def _heads(x, n):
    return [x[:, h * LANES:(h + 1) * LANES] for h in range(n)]


@functools.partial(jax.custom_vjp, nondiff_argnums=(1,))
def _lroll(x, shift):
    return pltpu.roll(x, shift, 1)


def _lroll_fwd(x, shift):
    return _lroll(x, shift), None


def _lroll_bwd(shift, _, dy):
    return (pltpu.roll(dy, (LANES - shift) % LANES, 1),)


_lroll.defvjp(_lroll_fwd, _lroll_bwd)


def _spread_rotary(v):
    low = lax.broadcasted_iota(jnp.int32, (1, LANES), 1) < 64
    return [jnp.where(low, v, _lroll(v, 32)), jnp.where(low, _lroll(v, 64), _lroll(v, 96))]


def _spread_values(v):
    low = lax.broadcasted_iota(jnp.int32, (1, LANES), 1) < 64
    return [jnp.where(low, v, 0.0), jnp.where(low, _lroll(v, 64), 0.0)]


def _f_norm_mod(x, w, sc, sh):
    return _rms(x, w) * (1.0 + sc) + sh


def _f_resid_norm_mod(xa, delta, g, w, sc, sh):
    xn = xa + g * delta
    return xn, _f_norm_mod(xn, w, sc, sh)


def _f_mixprep(proj, ca, sa, cb, sb, qnw, kvnw):
    qa = [_rope(p, ca, sa) for pair in _heads(proj[:, P_AQ:P_AK], A_Q_HEADS // 2) for p in _spread_rotary(pair)]
    ka = [_rope(p, ca, sa) for p in _spread_rotary(proj[:, P_AK:P_AV])]
    va = _spread_values(proj[:, P_AV:P_CQ])
    cqn = _rms(proj[:, P_CQ:P_CKV], qnw)
    ckvn = _rms(proj[:, P_CKV:P_MIX], kvnw)
    kr = _rope(proj[:, P_KR:P_AQ], cb, sb)
    return jnp.concatenate(qa, -1), jnp.concatenate(ka, -1), jnp.concatenate(va, -1), cqn, ckvn, kr


def _f_mlaprep(q, kv, kr, cb, sb):
    qs = [_rope(p, cb, sb) for p in _heads(q, B_HEADS)]
    ks = [p + kr for p in _heads(kv[:, :B_HEADS * LANES], B_HEADS)]
    return jnp.concatenate(qs, -1), jnp.concatenate(ks, -1), kv[:, B_HEADS * LANES:]


def _f_sgu(cu, cv, ln_w, ln_b, w_s, b_col):
    u = _gelu(cu)
    v = _gelu(cv)
    mu = jnp.mean(v, axis=-1, keepdims=True)
    var = jnp.mean(jnp.square(v - mu), axis=-1, keepdims=True)
    vn = (v - mu) * lax.rsqrt(var + NORM_EPS) * ln_w + ln_b
    r = lax.broadcasted_iota(jnp.int32, (C_CHUNK, C_CHUNK), 0)
    c = lax.broadcasted_iota(jnp.int32, (C_CHUNK, C_CHUNK), 1)
    lane = lax.broadcasted_iota(jnp.int32, (1, LANES), 1)
    per_block = LANES // C_GROUP_DIM
    blocks = []
    for blk, vb in enumerate(_heads(vn, C_WIDTH // LANES)):
        mixed = jnp.zeros(vb.shape, F32)
        for j in range(per_block):
            g = blk * per_block + j
            gm = (lane // C_GROUP_DIM == j).astype(F32)
            wg = jnp.where(r >= c, w_s[g], 0.0).astype(BF16)
            mixed = mixed + jnp.dot(wg, (vb * gm).astype(BF16), preferred_element_type=F32) + b_col[g] * gm
        blocks.append(mixed)
    return u * jnp.concatenate(blocks, -1)


@jax.custom_vjp
def _pack_pairs(x):
    heads = _heads(x, x.shape[-1] // LANES)
    return jnp.concatenate([heads[i] + pltpu.roll(heads[i + 1], 64, 1) for i in range(0, len(heads), 2)], -1)


def _pack_pairs_fwd(x):
    return _pack_pairs(x), None


def _pack_pairs_bwd(_, dy):
    out = []
    for p in _heads(dy, dy.shape[-1] // LANES):
        out += [p, pltpu.roll(p, 64, 1)]
    return (jnp.concatenate(out, -1),)


_pack_pairs.defvjp(_pack_pairs_fwd, _pack_pairs_bwd)


def _f_outnorm(oa, ob, yc, gw):
    na, nb = A_Q_HEADS * HEAD_DIM, B_HEADS * B_V
    ya = _rms(_pack_pairs(oa), gw[:, :na])
    yb = _rms(_pack_pairs(ob), gw[:, na:na + nb])
    ycn = _rms(yc, gw[:, na + nb:])
    return jnp.concatenate([ya, yb, ycn], -1)


def _f_swiglu(gate, up):
    return _silu(gate) * up


def _mask(q_start, k_start, tq, tk, window):
    qpos = q_start + lax.broadcasted_iota(jnp.int32, (tq, tk), 0)
    kpos = k_start + lax.broadcasted_iota(jnp.int32, (tq, tk), 1)
    m = kpos <= qpos
    if window is not None:
        m = jnp.logical_and(m, qpos - kpos < window)
    return m


def _tile_fwd(qv, kk, vv, q_start, k_start, n_free, scale, window, m0, l0):
    tq = qv.shape[0]
    W = kk.shape[0]
    c = scale * LOG2_E
    parts = []
    if n_free > 0:
        parts.append((lax.dot_general(qv, kk[:n_free], _DIMS["nt"], preferred_element_type=F32), vv[:n_free]))
    if W > n_free:
        s = lax.dot_general(qv, kk[n_free:], _DIMS["nt"], preferred_element_type=F32)
        s = jnp.where(_mask(q_start, k_start + n_free, tq, W - n_free, window), s, NEG_INF)
        parts.append((s, vv[n_free:]))
    m = None if m0 is None else m0 * (1.0 / scale)
    for s, _ in parts:
        mx = jnp.max(s, axis=-1, keepdims=True)
        m = mx if m is None else jnp.maximum(m, mx)
    l = None if l0 is None else l0 * jnp.exp2((m0 * (1.0 / scale) - m) * c)
    o = None
    for s, vpart in parts:
        p = jnp.exp2((s - m) * c)
        ps = jnp.sum(p, axis=-1, keepdims=True)
        l = ps if l is None else l + ps
        po = jnp.dot(p.astype(BF16), vpart, preferred_element_type=F32)
        o = po if o is None else o + po
    return o / l, m * scale + jnp.log(l)


def _tile_bwd(qv, kk, vv, dof, ov, lse, q_start, k_start, n_free, scale, window):
    tq = qv.shape[0]
    W = kk.shape[0]
    dob = dof.astype(BF16)
    delta = jnp.sum(dof * ov, axis=-1, keepdims=True)
    c = scale * LOG2_E
    lse2 = lse * LOG2_E
    dq = None
    outs = []
    for (a, b, masked) in ((0, n_free, False), (n_free, W, True)):
        if b <= a:
            continue
        kp, vp = kk[a:b], vv[a:b]
        s = lax.dot_general(qv, kp, _DIMS["nt"], preferred_element_type=F32)
        if masked:
            s = jnp.where(_mask(q_start, k_start + a, tq, b - a, window), s, NEG_INF)
        p = jnp.exp2(s * c - lse2)
        dp = lax.dot_general(dob, vp, _DIMS["nt"], preferred_element_type=F32)
        ds = (p * ((dp - delta) * scale)).astype(BF16)
        d = jnp.dot(ds, kp, preferred_element_type=F32)
        dq = d if dq is None else dq + d
        dkp = lax.dot_general(ds, qv, _DIMS["tn"], preferred_element_type=F32)
        dvp = lax.dot_general(p.astype(BF16), dob, _DIMS["tn"], preferred_element_type=F32)
        outs.append((a, dkp, dvp))
    return dq, outs


def _attn_fwd(q, k, v, sinks, *, B, S, HQ, HK, window, scale, tq, band, name, xchg=None):
    G = HQ // HK
    nq = S // tq
    T = B * S
    has_sink = sinks is not None
    n_x = 0 if xchg is None else len(xchg[1])
    n_in = 4 if has_sink else 3

    def kern(*refs):
        xs, xd, sems = refs[n_in:n_in + n_x], refs[n_in + n_x + 2:n_in + 2 * n_x + 2], refs[n_in + 2 * n_x + 2:]
        refs = refs[:n_in] + refs[n_in + n_x:n_in + n_x + 2]
        if n_x:
            _xchg_at_ends(xchg[0], xs, xd, sems, (B, HQ, nq), True)
        if has_sink:
            q_ref, k_ref, v_ref, s_ref, o_ref, lse_ref = refs
        else:
            q_ref, k_ref, v_ref, o_ref, lse_ref = refs
        q_start = pl.program_id(2) * tq
        qv = q_ref[...]
        if has_sink:
            m0 = jnp.broadcast_to(s_ref[0][:, :1], (tq, 1))
            l0 = jnp.ones((tq, 1), F32)
        else:
            m0 = l0 = None

        def finish(o, lse):
            o_ref[...] = o.astype(o_ref.dtype)
            lse_ref[...] = jnp.broadcast_to(lse, (tq, LANES))

        if window is None:
            bidx = q_start // band
            for bb in range(S // band):
                @pl.when(bidx == bb)
                def _(bb=bb):
                    W = (bb + 1) * band
                    finish(*_tile_fwd(qv, k_ref[0:W, :], v_ref[0:W, :], q_start, 0, bb * band, scale, None, m0, l0))
        else:
            sub = min(tq, WINDOW_SUB)
            W = min(S, sub + window)
            for r in range(0, tq, sub):
                k_start = pl.multiple_of(jnp.maximum(q_start + r - window, 0), window)
                o, lse = _tile_fwd(qv[r:r + sub], k_ref[pl.ds(k_start, W), :], v_ref[pl.ds(k_start, W), :], q_start + r,
                                   k_start, 0, scale, window, None if m0 is None else m0[:sub], None if l0 is None else l0[:sub])
                o_ref[r:r + sub, :] = o.astype(o_ref.dtype)
                lse_ref[r:r + sub, :] = jnp.broadcast_to(lse, (sub, LANES))
        if n_x:
            _xchg_at_ends(xchg[0], xs, xd, sems, (B, HQ, nq), False)

    q_spec = pl.BlockSpec((tq, LANES), lambda b, h, i: (b * nq + i, h))
    kv_spec = pl.BlockSpec((S, LANES), lambda b, h, i: (b, h // G))
    in_specs = [q_spec, kv_spec, kv_spec]
    args = [q, k, v]
    if has_sink:
        in_specs.append(pl.BlockSpec((1, 1, LANES), lambda b, h, i: (h, 0, 0)))
        args.append(sinks)
    xs = [] if xchg is None else list(xchg[1])
    res = pl.pallas_call(
        kern, name=name, grid=(B, HQ, nq), in_specs=in_specs + [_ANY] * n_x, out_specs=[q_spec, q_spec] + [_ANY] * n_x,
        out_shape=[jax.ShapeDtypeStruct((T, HQ * LANES), BF16), jax.ShapeDtypeStruct((T, HQ * LANES), F32)]
        + (_xchg_out_shapes(xchg[0], xs) if n_x else []),
        scratch_shapes=_xchg_scratch(xchg[0], n_x) if n_x else [],
        compiler_params=_params(("arbitrary",) * 3 if n_x else ("parallel", "parallel", "arbitrary")),
    )(*args, *xs)
    return ((res[0], res[1]), list(res[2:])) if n_x else res


def _attn_bwd(q, k, v, o, lse, do, sinks, *, B, S, HQ, HK, window, scale, tq, band, name, xchg=None):
    G = HQ // HK
    nq = S // tq
    T = B * S
    has_sink = sinks is not None
    n_x = 0 if xchg is None else len(xchg[1])
    n_in, n_out = (7, 4) if has_sink else (6, 3)

    def kern(*refs):
        xs, xd = refs[n_in:n_in + n_x], refs[n_in + n_x + n_out:n_in + 2 * n_x + n_out]
        dk_acc, dv_acc = refs[n_in + 2 * n_x + n_out:n_in + 2 * n_x + n_out + 2]
        sems = refs[n_in + 2 * n_x + n_out + 2:]
        refs = refs[:n_in] + refs[n_in + n_x:n_in + n_x + n_out]
        if n_x:
            _xchg_at_ends(xchg[0], xs, xd, sems, (B, HK, G, nq), True)
        if has_sink:
            q_ref, k_ref, v_ref, o_ref, lse_ref, do_ref, s_ref, dq_ref, dk_ref, dv_ref, ds_ref = refs
        else:
            q_ref, k_ref, v_ref, o_ref, lse_ref, do_ref, dq_ref, dk_ref, dv_ref = refs
        gi = pl.program_id(2)
        qi = pl.program_id(3)
        q_start = qi * tq

        @pl.when(jnp.logical_and(gi == 0, qi == 0))
        def _():
            dk_acc[...] = jnp.zeros_like(dk_acc)
            dv_acc[...] = jnp.zeros_like(dv_acc)

        qv = q_ref[...]
        dof = do_ref[...].astype(F32)
        ov = o_ref[...].astype(F32)
        lse_v = lse_ref[...][:, :1]
        if window is None:
            bidx = q_start // band
            for bb in range(S // band):
                @pl.when(bidx == bb)
                def _(bb=bb):
                    W = (bb + 1) * band
                    dq, outs = _tile_bwd(qv, k_ref[0:W, :], v_ref[0:W, :], dof, ov, lse_v, q_start, 0, bb * band, scale, None)
                    dq_ref[...] = dq.astype(dq_ref.dtype)
                    for a, dkp, dvp in outs:
                        dk_acc[a:a + dkp.shape[0], :] += dkp
                        dv_acc[a:a + dvp.shape[0], :] += dvp
        else:
            sub = min(tq, WINDOW_SUB)
            W = min(S, sub + window)
            parts = []
            for r in range(0, tq, sub):
                k_start = pl.multiple_of(jnp.maximum(q_start + r - window, 0), window)
                dq, outs = _tile_bwd(qv[r:r + sub], k_ref[pl.ds(k_start, W), :], v_ref[pl.ds(k_start, W), :],
                                     dof[r:r + sub], ov[r:r + sub], lse_v[r:r + sub], q_start + r, k_start, 0, scale, window)
                dq_ref[r:r + sub, :] = dq.astype(dq_ref.dtype)
                parts.append((k_start, outs[0][1], outs[0][2]))
            for k_start, dkp, dvp in parts:
                dk_acc[pl.ds(k_start, W), :] += dkp
                dv_acc[pl.ds(k_start, W), :] += dvp

        @pl.when(jnp.logical_and(gi == G - 1, qi == nq - 1))
        def _():
            dk_ref[...] = dk_acc[...].astype(dk_ref.dtype)
            dv_ref[...] = dv_acc[...].astype(dv_ref.dtype)
        if has_sink:
            delta = jnp.sum(dof * ov, axis=-1, keepdims=True)
            sink = s_ref[0][:, :1]
            part = -jnp.sum(jnp.exp(sink - lse_v) * delta, axis=0, keepdims=True)
            part = jnp.broadcast_to(part, (1, LANES))

            @pl.when(qi == 0)
            def _():
                ds_ref[0] = part

            @pl.when(qi != 0)
            def _():
                ds_ref[0] += part
        if n_x:
            _xchg_at_ends(xchg[0], xs, xd, sems, (B, HK, G, nq), False)

    q_spec = pl.BlockSpec((tq, LANES), lambda b, hk, g, i: (b * nq + i, hk * G + g))
    kv_spec = pl.BlockSpec((S, LANES), lambda b, hk, g, i: (b, hk))
    in_specs = [q_spec, kv_spec, kv_spec, q_spec, q_spec, q_spec]
    args = [q, k, v, o, lse, do]
    out_specs = [q_spec, kv_spec, kv_spec]
    out_shape = [jax.ShapeDtypeStruct((T, HQ * LANES), BF16), jax.ShapeDtypeStruct((T, HK * LANES), BF16),
                 jax.ShapeDtypeStruct((T, HK * LANES), BF16)]
    if has_sink:
        in_specs.append(pl.BlockSpec((1, 1, LANES), lambda b, hk, g, i: (hk * G + g, 0, 0)))
        args.append(sinks)
        out_specs.append(pl.BlockSpec((1, 1, LANES), lambda b, hk, g, i: (b * HQ + hk * G + g, 0, 0)))
        out_shape.append(jax.ShapeDtypeStruct((B * HQ, 1, LANES), F32))
    xs = [] if xchg is None else list(xchg[1])
    res = pl.pallas_call(
        kern, name=name, grid=(B, HK, G, nq), in_specs=in_specs + [_ANY] * n_x, out_specs=out_specs + [_ANY] * n_x,
        out_shape=out_shape + (_xchg_out_shapes(xchg[0], xs) if n_x else []),
        scratch_shapes=[pltpu.VMEM((S, LANES), F32)] * 2 + (_xchg_scratch(xchg[0], n_x) if n_x else []),
        compiler_params=_params(("arbitrary",) * 4 if n_x else ("parallel", "parallel", "arbitrary", "arbitrary")),
    )(*args, *xs)
    main = tuple(res[:n_out]) if has_sink else (*res[:n_out], None)
    return (main, list(res[n_out:])) if n_x else main


def _rope_tables(pos_col, freq, sign, keep, name):
    T = pos_col.shape[0]
    tr = _tile(T, 1024)

    def kern(p_ref, f_ref, s_ref, k_ref, c_out, s_out):
        ang = p_ref[...] * f_ref[...]
        c_out[...] = jnp.cos(ang) * k_ref[...]
        s_out[...] = jnp.sin(ang) * s_ref[...]

    spec = pl.BlockSpec((tr, LANES), lambda i: (i, 0))
    par = pl.BlockSpec((1, LANES), lambda i: (0, 0))
    return pl.pallas_call(
        kern, name=name, grid=(T // tr,), in_specs=[pl.BlockSpec((tr, 1), lambda i: (i, 0)), par, par, par],
        out_specs=[spec, spec], out_shape=[jax.ShapeDtypeStruct((T, LANES), F32)] * 2,
        compiler_params=_params(("parallel",)),
    )(pos_col, freq, sign, keep)


def _ada_fwd(c8, ada_w, ada_b):
    L, D, N = ada_w.shape

    def kern(c_ref, w_ref, b_ref, o_ref):
        act = _silu(c_ref[...]).astype(BF16)
        o_ref[0] = jnp.dot(act, w_ref[0], preferred_element_type=F32) + b_ref[0]

    return pl.pallas_call(
        kern, name="ada_fwd", grid=(L,),
        in_specs=[pl.BlockSpec((8, D), lambda l: (0, 0)), pl.BlockSpec((1, D, N), lambda l: (l, 0, 0)),
                  pl.BlockSpec((1, 1, N), lambda l: (l, 0, 0))],
        out_specs=pl.BlockSpec((1, 8, N), lambda l: (l, 0, 0)),
        out_shape=jax.ShapeDtypeStruct((L, 8, N), F32), compiler_params=_params(("parallel",)),
    )(c8, ada_w, ada_b)


def _ada_bwd(c8, dmod):
    L, _, N = dmod.shape
    D = c8.shape[1]

    def kern(c_ref, d_ref, gw_ref):
        act = _silu(c_ref[...]).astype(BF16)
        gw_ref[0] = lax.dot_general(act, d_ref[0].astype(BF16), _DIMS["tn"], preferred_element_type=F32)

    return pl.pallas_call(
        kern, name="ada_bwd", grid=(L,),
        in_specs=[pl.BlockSpec((8, D), lambda l: (0, 0)), pl.BlockSpec((1, 8, N), lambda l: (l, 0, 0))],
        out_specs=pl.BlockSpec((1, D, N), lambda l: (l, 0, 0)),
        out_shape=jax.ShapeDtypeStruct((L, D, N), F32), compiler_params=_params(("parallel",)),
    )(c8, dmod)


def _sum_examples(d):
    L, _, N = d.shape

    def kern(d_ref, o_ref):
        o_ref[...] = jnp.sum(d_ref[...], axis=1, keepdims=True)

    vm = pl.BlockSpec(memory_space=pltpu.VMEM)
    return pl.pallas_call(kern, name="sum_examples", in_specs=[vm], out_specs=vm,
                          out_shape=jax.ShapeDtypeStruct((L, 1, N), F32))(d)


def _adamw(w, ga, gb, m, v, name):
    rows, cols = w.shape
    tr = _row_div(rows, 256)
    two = gb is not None

    def kern(*refs):
        if two:
            w_ref, ga_ref, gb_ref, m_ref, v_ref, g_out, d_out, m_out, v_out = refs
            gv = ga_ref[...] + gb_ref[...]
        else:
            w_ref, ga_ref, m_ref, v_ref, g_out, d_out, m_out, v_out = refs
            gv = ga_ref[...]
        mn = ADAM_B1 * m_ref[...] + (1.0 - ADAM_B1) * gv
        vn = ADAM_B2 * v_ref[...] + (1.0 - ADAM_B2) * jnp.square(gv)
        m_hat = mn / (1.0 - ADAM_B1 ** ADAM_STEP)
        v_hat = vn / (1.0 - ADAM_B2 ** ADAM_STEP)
        g_out[...] = gv
        d_out[...] = -ADAM_LR * (m_hat / (jnp.sqrt(v_hat) + ADAM_EPS) + ADAM_WD * w_ref[...])
        m_out[...] = mn
        v_out[...] = vn

    spec = pl.BlockSpec((tr, cols), lambda i: (i, 0))
    args = [w, ga, gb, m, v] if two else [w, ga, m, v]
    return pl.pallas_call(
        kern, name=name, grid=(rows // tr,), in_specs=[spec] * len(args), out_specs=[spec] * 4,
        out_shape=[jax.ShapeDtypeStruct((rows, cols), F32)] * 4, compiler_params=_params(("parallel",)),
    )(*args)


def _sum_slots(x, name):
    n, rows, cols = x.shape
    tr = _row_div(rows, 256)

    def kern(x_ref, o_ref):
        acc = x_ref[0].astype(F32)
        for j in range(1, n):
            acc = acc + x_ref[j].astype(F32)
        o_ref[...] = acc

    return pl.pallas_call(
        kern, name=name, grid=(rows // tr,), in_specs=[pl.BlockSpec((n, tr, cols), lambda i: (0, i, 0))],
        out_specs=pl.BlockSpec((tr, cols), lambda i: (i, 0)), out_shape=jax.ShapeDtypeStruct((rows, cols), F32),
        compiler_params=_params(("parallel",)),
    )(x)


def _all_reduce_small(blob, name, swaps=()):
    R, C = blob.shape
    n = len(swaps)

    def kern(*refs):
        src, xs, out, xd = refs[0], refs[1:1 + n], refs[1 + n], refs[2 + n:2 + 2 * n]
        pair, chips, send_sems, recv_sems = refs[2 + 2 * n:6 + 2 * n]
        swap_sems = refs[6 + 2 * n:]
        if n:
            _xchg_start("swap", xs, xd, swap_sems)
        x, y, c = lax.axis_index("x"), lax.axis_index("y"), lax.axis_index("c")
        me = 2 * x + y
        to_sibling = pltpu.make_async_remote_copy(
            src_ref=src, dst_ref=pair, send_sem=send_sems.at[0], recv_sem=recv_sems.at[0],
            device_id=(x, y, 1 - c), device_id_type=_MESH)
        to_sibling.start()
        to_sibling.wait()
        chips[me] = src[...] + pair[...]
        peers = [(1 - x, y), (x, 1 - y), (1 - x, 1 - y)]
        copies = [pltpu.make_async_remote_copy(
            src_ref=chips.at[me], dst_ref=chips.at[me], send_sem=send_sems.at[1 + kk], recv_sem=recv_sems.at[1 + kk],
            device_id=(px, py, c), device_id_type=_MESH) for kk, (px, py) in enumerate(peers)]
        for cp in copies:
            cp.start()
        for kk, (px, py) in enumerate(peers):
            pltpu.make_async_remote_copy(
                src_ref=chips.at[me], dst_ref=chips.at[2 * px + py], send_sem=send_sems.at[1 + kk],
                recv_sem=recv_sems.at[1 + kk], device_id=(px, py, c), device_id_type=_MESH).wait_recv()
        for cp in copies:
            cp.wait_send()
        acc = chips[0]
        for j in range(1, N_CHIPS):
            acc = acc + chips[j]
        out[...] = acc
        if n:
            _xchg_wait("swap", xs, xd, swap_sems)

    vm = pl.BlockSpec(memory_space=pltpu.VMEM)
    swaps = list(swaps)
    res = pl.pallas_call(
        kern, name=name, in_specs=[vm] + [_ANY] * n, out_specs=[vm] + [_ANY] * n,
        out_shape=[jax.ShapeDtypeStruct((R, C), F32)] + _xchg_out_shapes("swap", swaps),
        scratch_shapes=[pltpu.VMEM((R, C), F32), pltpu.VMEM((N_CHIPS, R, C), F32), pltpu.SemaphoreType.DMA((4,)),
                        pltpu.SemaphoreType.DMA((4,))] + (_xchg_scratch("swap", n) if n else []),
        compiler_params=pltpu.CompilerParams(vmem_limit_bytes=VMEM_LIMIT),
    )(blob, *swaps)
    return res[0], list(res[1:])


def _row_tile(S, wide=False):
    for t in ((256,) if wide else (512, 256)):
        if S % t == 0:
            return t
    return 128


def _attn_tiles(S):
    return min(S, 512), min(S, 256), min(S, 256)


def _hosted(fn, host, got, kind, name, *args, **kw):
    if host is not None and host.get(name):
        res, xs = fn(*args, name=name, xchg=(kind, host[name]), **kw)
        got[name] = xs
        return res
    return fn(*args, name=name, **kw)


def _mm(host, got, kind, a, b, mode, out_dtype, name, **kw):
    return _hosted(_matmul, host, got, kind, name, a, b, mode, out_dtype, **kw)


def _merge_late(w, late, got, name):
    if not late or name not in late:
        return w
    return {**w, **_large_operands(dict(zip(late[name], got[name])))}


def _sgu_rows(S):
    return 4 * C_CHUNK if S % (4 * C_CHUNK) == 0 else C_CHUNK


def _layer_fwd(xin, prev, mods, w, tabs, B, S, host=None, late=None):
    T = B * S
    tr = _row_tile(S)
    sh1, sc1, g1, sh2, sc2, g2 = mods
    ca, sa, cb, sb = tabs
    sv = {}
    got = {}
    if prev is None:
        def body(rv, ev, pv):
            return [_f_norm_mod(rv[0], pv[0], ev[0], ev[1])], [], []
        (h,), _, _ = _rowcall(body, name="f_norm_mod", T=T, S=S, tr=tr, rows=[(xin, D_MODEL, 0)], exs=[sc1, sh1],
                              pars=[w["norm1_w"]], row_outs=[(D_MODEL, BF16, D_MODEL, 0)])
        x = xin
    else:
        x1p, dp, g2p = prev

        def body(rv, ev, pv):
            xn, hh = _f_resid_norm_mod(rv[0], rv[1], ev[0], pv[0], ev[1], ev[2])
            return [xn, hh], [], []
        (x, h), _, _ = _rowcall(body, name="f_resid_norm_mod1", T=T, S=S, tr=tr,
                                rows=[(x1p, D_MODEL, 0), (dp, D_MODEL, 0)], exs=[g2p, sc1, sh1], pars=[w["norm1_w"]],
                                row_outs=[(D_MODEL, F32, D_MODEL, 0), (D_MODEL, BF16, D_MODEL, 0)])
    sv["x"], sv["h"] = x, h
    proj = _mm(host, got, "gather", h, w["w_in"], "nn", BF16, "mm_in")
    sv["proj"] = proj
    w = _merge_late(w, late, got, "mm_in")

    def body(rv, ev, pv):
        outs = _f_mixprep(rv[0], rv[1], rv[2], rv[3], rv[4], pv[0], pv[1])
        return list(outs), [], []
    (qa, ka, va, cqn, ckvn, kr), _, _ = _rowcall(
        body, name="f_mixprep", T=T, S=S, tr=tr,
        rows=[(proj, P_MIX, 0), (ca, LANES, 0), (sa, LANES, 0), (cb, LANES, 0), (sb, LANES, 0)],
        pars=[w["b_q_norm_w"], w["b_kv_norm_w"]],
        row_outs=[(768, BF16, 768, 0), (256, BF16, 256, 0), (256, BF16, 256, 0), (384, BF16, 384, 0),
                  (256, BF16, 256, 0), (LANES, F32, LANES, 0)])
    sv.update(qa=qa, ka=ka, va=va, cqn=cqn, ckvn=ckvn)
    q = _matmul(cqn, w["b_w_uq"], "nn", BF16, "mm_uq")
    kv = _matmul(ckvn, w["b_w_ukv"], "nn", BF16, "mm_ukv")

    def body(rv, ev, pv):
        Q, K, V = _f_mlaprep(rv[0], rv[1], rv[2], rv[3], rv[4])
        return [Q, K, V], [], []
    (Q, K, V), _, _ = _rowcall(
        body, name="f_mlaprep", T=T, S=S, tr=tr,
        rows=[(q, 768, 0), (kv, 1536, 0), (kr, LANES, 0), (cb, LANES, 0), (sb, LANES, 0)],
        row_outs=[(768, BF16, 768, 0)] * 3)
    sv.update(Q=Q, K=K, V=V)
    ta, tb, band = _attn_tiles(S)
    oa, lse_a = _hosted(_attn_fwd, host, got, "gather", "attn_a_fwd", qa, ka, va, w["sinks"], B=B, S=S, HQ=A_Q_HEADS,
                        HK=A_KV_HEADS, window=A_WINDOW, scale=HEAD_DIM ** -0.5, tq=ta, band=None)
    ob, lse_b = _hosted(_attn_fwd, host, got, "gather", "attn_b_fwd", Q, K, V, None, B=B, S=S, HQ=B_HEADS, HK=B_HEADS,
                        window=None, scale=(B_NOPE + B_ROPE) ** -0.5, tq=tb, band=band)
    sv.update(oa=oa, lse_a=lse_a, ob=ob, lse_b=lse_b)
    w = _merge_late(w, late, got, "attn_a_fwd")
    ts = _sgu_rows(S)

    def body(rv, ev, pv):
        outs = [_f_sgu(rv[0][r:r + C_CHUNK], rv[1][r:r + C_CHUNK], pv[0], pv[1], pv[2], pv[3])
                for r in range(0, ts, C_CHUNK)]
        return [jnp.concatenate(outs, axis=0)], [], []
    (yc,), _, _ = _rowcall(body, name="f_sgu", T=T, S=S, tr=ts,
                           rows=[(proj, C_WIDTH, P_CU // C_WIDTH), (proj, C_WIDTH, P_CV // C_WIDTH)],
                           pars=[w["c_ln_w"], w["c_ln_b"], w["c_w_s"], w["c_b_col"]],
                           row_outs=[(C_WIDTH, F32, C_WIDTH, 0)])
    sv["yc"] = yc

    def body(rv, ev, pv):
        return [_f_outnorm(rv[0], rv[1], rv[2], pv[0])], [], []
    (y,), _, _ = _rowcall(body, name="f_outnorm", T=T, S=S, tr=tr,
                          rows=[(oa, 768, 0), (ob, 768, 0), (yc, C_WIDTH, 0)], pars=[w["out_norm_w"]],
                          row_outs=[(D_MODEL, BF16, D_MODEL, 0)])
    sv["y"] = y
    o = _mm(host, got, "gather", y, w["w_out"], "nn", BF16, "mm_out")
    sv["o"] = o

    def body(rv, ev, pv):
        xn, hh = _f_resid_norm_mod(rv[0], rv[1], ev[0], pv[0], ev[1], ev[2])
        return [xn, hh], [], []
    (x1, h2), _, _ = _rowcall(body, name="f_resid_norm_mod2", T=T, S=S, tr=tr,
                              rows=[(x, D_MODEL, 0), (o, D_MODEL, 0)], exs=[g1, sc2, sh2], pars=[w["norm2_w"]],
                              row_outs=[(D_MODEL, F32, D_MODEL, 0), (D_MODEL, BF16, D_MODEL, 0)])
    sv["h2"] = h2
    gu = _mm(host, got, "gather", h2, w["w_gate_up"], "nn", BF16, "mm_gate_up", b_chunks=True)
    sv["gu"] = gu

    def body(rv, ev, pv):
        return [_f_swiglu(rv[0], rv[1])], [], []
    (act,), _, _ = _rowcall(body, name="f_swiglu", T=T, S=S, tr=_row_tile(S, wide=True),
                            rows=[(gu, FFN_HIDDEN, 0), (gu, FFN_HIDDEN, 1)], row_outs=[(FFN_HIDDEN, BF16, FFN_HIDDEN, 0)])
    sv["act"] = act
    d = _mm(host, got, "gather", act, w["w_down"], "nn", BF16, "mm_down")
    sv["x1"], sv["d"], sv["w"] = x1, d, w
    return (x1, d, g2), sv, got


def _final(x1, d, g2, fw, target, B, S):
    T = B * S
    tr = _row_tile(S)

    def loss_fn(x1v, dv, gv, wv, tv):
        yf = _rms(x1v + gv * dv, wv)
        return 0.5 * jnp.sum(jnp.mean(jnp.square(yf - tv), axis=-1))

    def body(rv, ev, pv):
        x1v, dv, tv = rv
        val, vjp = jax.vjp(lambda a, b_, g, ww: loss_fn(a, b_, g, ww, tv), x1v, dv, ev[0], pv[0])
        dx1, dd, dg, dw = vjp(jnp.ones((), F32))
        return [dx1, dd], [dg], [dw, jnp.full((1, LANES), val, F32)]
    (dx1, dd), (dg2,), (dfw, loss) = _rowcall(
        body, name="final_loss", T=T, S=S, tr=tr, rows=[(x1, D_MODEL, 0), (d, D_MODEL, 0), (target, D_MODEL, 0)],
        exs=[g2], pars=[fw], row_outs=[(D_MODEL, F32, D_MODEL, 0), (D_MODEL, BF16, D_MODEL, 0)],
        ex_outs=[D_MODEL], par_outs=[(1, D_MODEL), (1, LANES)])
    return loss, dx1, dd, dg2, dfw


def _layer_bwd(sv, prev, mods, tabs, dx1, dd, B, S, host=None, own_early=False):
    w = sv["w"]
    T = B * S
    tr = _row_tile(S)
    sh1, sc1, g1, sh2, sc2, g2 = mods
    ca, sa, cb, sb = tabs
    gr = {}
    got = {}
    dact = _mm(host, got, "a2a", dd, w["w_down"], "nt", BF16, "mm_down_dx")
    gr["w_down"] = _matmul(sv["act"], dd, "tn", BF16, "mm_down_dw")

    def body(rv, ev, pv):
        _, vjp = jax.vjp(_f_swiglu, rv[0], rv[1])
        dgate, dup = vjp(rv[2])
        return [[dgate, dup]], [], []
    (dgu,), _, _ = _rowcall(body, name="b_swiglu", T=T, S=S, tr=_row_tile(S, wide=True),
                            rows=[(sv["gu"], FFN_HIDDEN, 0), (sv["gu"], FFN_HIDDEN, 1), (dact, FFN_HIDDEN, 0)],
                            row_outs=[(2 * FFN_HIDDEN, BF16, 2 * FFN_HIDDEN, 0)])
    dh2 = _mm(host, got, "a2a", dgu, w["w_gate_up"], "nt", BF16, "mm_gate_up_dx", b_chunks=True)
    gr["w_gate_up"] = _mm(host, got, "a2a", sv["h2"], dgu, "tn", BF16, "mm_gate_up_dw", out_chunks=True)

    def body(rv, ev, pv):
        xa, delta, dh, dxn = rv
        _, vjp = jax.vjp(_f_resid_norm_mod, xa, delta, ev[0], pv[0], ev[1], ev[2])
        dxa, ddelta, dg, dw, dsc, dsh = vjp((dxn, dh))
        return [dxa, ddelta], [dg, dsc, dsh], [dw]
    (dx, do), (dg1, dsc2, dsh2), (gr["norm2_w"],) = _rowcall(
        body, name="b_resid_norm_mod2", T=T, S=S, tr=tr,
        rows=[(sv["x"], D_MODEL, 0), (sv["o"], D_MODEL, 0), (dh2, D_MODEL, 0), (dx1, D_MODEL, 0)],
        exs=[g1, sc2, sh2], pars=[w["norm2_w"]],
        row_outs=[(D_MODEL, F32, D_MODEL, 0), (D_MODEL, BF16, D_MODEL, 0)], ex_outs=[D_MODEL] * 3,
        par_outs=[(1, D_MODEL)])
    dy = _matmul(do, w["w_out"], "nt", BF16, "mm_out_dx")
    gr["w_out"] = _matmul(sv["y"], do, "tn", BF16, "mm_out_dw")

    def body(rv, ev, pv):
        _, vjp = jax.vjp(_f_outnorm, rv[0], rv[1], rv[2], pv[0])
        doa, dob, dyc, dgw = vjp(rv[3])
        return [doa, dob, dyc], [], [dgw]
    (doa, dob, dyc), _, (gr["out_norm_w"],) = _rowcall(
        body, name="b_outnorm", T=T, S=S, tr=tr,
        rows=[(sv["oa"], 768, 0), (sv["ob"], 768, 0), (sv["yc"], C_WIDTH, 0), (dy, D_MODEL, 0)], pars=[w["out_norm_w"]],
        row_outs=[(768, BF16, 768, 0), (768, BF16, 768, 0), (C_WIDTH, F32, C_WIDTH, 0)], par_outs=[(1, D_MODEL)])

    ta, tb, band = _attn_tiles(S)
    if own_early:
        host = dict(host)
        host["attn_b_bwd"] = list(host.get("attn_b_bwd", ())) + [
            gr["w_gate_up"], gr["w_down"].reshape(N_CHIPS, FFN_HIDDEN // N_CHIPS, D_MODEL)]
        host["attn_a_bwd"] = list(host.get("attn_a_bwd", ())) + [
            gr["w_out"].reshape(N_CHIPS, D_MODEL // N_CHIPS, D_MODEL)]
    dQ, dK, dV, _ = _hosted(_attn_bwd, host, got, "a2a", "attn_b_bwd", sv["Q"], sv["K"], sv["V"], sv["ob"], sv["lse_b"],
                            dob, None, B=B, S=S, HQ=B_HEADS, HK=B_HEADS, window=None,
                            scale=(B_NOPE + B_ROPE) ** -0.5, tq=tb, band=band)
    dqa, dka, dva, dsink = _hosted(_attn_bwd, host, got, "a2a", "attn_a_bwd", sv["qa"], sv["ka"], sv["va"], sv["oa"],
                                   sv["lse_a"], doa, w["sinks"], B=B, S=S, HQ=A_Q_HEADS, HK=A_KV_HEADS,
                                   window=A_WINDOW, scale=HEAD_DIM ** -0.5, tq=ta, band=None)
    gr["sinks"] = dsink

    def body(rv, ev, pv):
        dQv, dKv, dVv, cbv, sbv = rv
        dq = [_rope_bwd((cbv, sbv), p)[0] for p in _heads(dQv, B_HEADS)]
        dkr = None
        for p in _heads(dKv, B_HEADS):
            dkr = p if dkr is None else dkr + p
        return [dq, [dKv, dVv], dkr], [], []
    (dq, dkv, dkr), _, _ = _rowcall(
        body, name="b_mlaprep", T=T, S=S, tr=tr,
        rows=[(dQ, 768, 0), (dK, 768, 0), (dV, 768, 0), (cb, LANES, 0), (sb, LANES, 0)],
        row_outs=[(768, BF16, 768, 0), (1536, BF16, 1536, 0), (LANES, F32, LANES, 0)])
    dcqn = _matmul(dq, w["b_w_uq"], "nt", BF16, "mm_uq_dx")
    gr["b_w_uq"] = _matmul(sv["cqn"], dq, "tn", BF16, "mm_uq_dw")
    dckvn = _matmul(dkv, w["b_w_ukv"], "nt", BF16, "mm_ukv_dx")
    gr["b_w_ukv"] = _matmul(sv["ckvn"], dkv, "tn", BF16, "mm_ukv_dw")

    def body(rv, ev, pv):
        proj, cav, sav, cbv, sbv, dqa_, dka_, dva_, dcqn_, dckvn_, dkr_ = rv
        _, vjp = jax.vjp(lambda p, a, b_: _f_mixprep(p, cav, sav, cbv, sbv, a, b_), proj, pv[0], pv[1])
        dproj, dqn, dkvn = vjp((dqa_, dka_, dva_, dcqn_, dckvn_, dkr_))
        return [[dproj, jnp.zeros((dproj.shape[0], P_CU - P_MIX), F32)]], [], [dqn, dkvn]
    (dproj,), _, (gr["b_q_norm_w"], gr["b_kv_norm_w"]) = _rowcall(
        body, name="b_mixprep", T=T, S=S, tr=tr,
        rows=[(sv["proj"], P_MIX, 0), (ca, LANES, 0), (sa, LANES, 0), (cb, LANES, 0), (sb, LANES, 0),
              (dqa, 768, 0), (dka, 256, 0), (dva, 256, 0), (dcqn, 384, 0), (dckvn, 256, 0), (dkr, LANES, 0)],
        pars=[w["b_q_norm_w"], w["b_kv_norm_w"]], row_outs=[(P_END, BF16, P_CU, 0)],
        par_outs=[(1, B_Q_RANK), (1, B_KV_RANK)])

    ts = _sgu_rows(S)

    def body(rv, ev, pv):
        cu, cv, dycv, _ = rv
        dcus, dcvs, acc = [], [], None
        for r in range(0, ts, C_CHUNK):
            _, vjp = jax.vjp(_f_sgu, cu[r:r + C_CHUNK], cv[r:r + C_CHUNK], pv[0], pv[1], pv[2], pv[3])
            dcu, dcv, *dpar = vjp(dycv[r:r + C_CHUNK])
            dcus.append(dcu)
            dcvs.append(dcv)
            acc = dpar if acc is None else [a + b_ for a, b_ in zip(acc, dpar)]
        return [[jnp.concatenate(dcus, axis=0), jnp.concatenate(dcvs, axis=0)]], [], acc
    (dproj,), _, (gr["c_ln_w"], gr["c_ln_b"], gr["c_w_s"], gr["c_b_col"]) = _rowcall(
        body, name="b_sgu", T=T, S=S, tr=ts,
        rows=[(sv["proj"], C_WIDTH, P_CU // C_WIDTH), (sv["proj"], C_WIDTH, P_CV // C_WIDTH), (dyc, C_WIDTH, 0),
              (dproj, 2 * C_WIDTH, P_CU // (2 * C_WIDTH))],
        pars=[w["c_ln_w"], w["c_ln_b"], w["c_w_s"], w["c_b_col"]],
        row_outs=[(P_END, BF16, 2 * C_WIDTH, P_CU // (2 * C_WIDTH))],
        par_outs=[(1, C_WIDTH), (1, C_WIDTH), (C_GROUPS, C_CHUNK, C_CHUNK), (C_GROUPS, C_CHUNK, 1)],
        aliases={3: 0})
    dh = _mm(host, got, "a2a", dproj, w["w_in"], "nt", BF16, "mm_in_dx")
    gr["w_in"] = _matmul(sv["h"], dproj, "tn", BF16, "mm_in_dw")

    if prev is None:
        def body(rv, ev, pv):
            xv, dhv, dxd = rv
            _, vjp = jax.vjp(_f_norm_mod, xv, pv[0], ev[0], ev[1])
            dxa, dw, dsc, dsh = vjp(dhv)
            return [dxa + dxd], [dsc, dsh], [dw]
        (dxin,), (dsc1, dsh1), (gr["norm1_w"],) = _rowcall(
            body, name="b_norm_mod", T=T, S=S, tr=tr, rows=[(sv["x"], D_MODEL, 0), (dh, D_MODEL, 0), (dx, D_MODEL, 0)],
            exs=[sc1, sh1], pars=[w["norm1_w"]], row_outs=[(D_MODEL, F32, D_MODEL, 0)], ex_outs=[D_MODEL] * 2,
            par_outs=[(1, D_MODEL)])
        nxt = (dxin, None, None)
    else:
        x1p, dp, g2p = prev

        def body(rv, ev, pv):
            xa, delta, dhv, dxn = rv
            _, vjp = jax.vjp(_f_resid_norm_mod, xa, delta, ev[0], pv[0], ev[1], ev[2])
            dxa, ddelta, dg, dw, dsc, dsh = vjp((dxn, dhv))
            return [dxa, ddelta], [dg, dsc, dsh], [dw]
        (dx1p, ddp), (dg2p, dsc1, dsh1), (gr["norm1_w"],) = _rowcall(
            body, name="b_resid_norm_mod1", T=T, S=S, tr=tr,
            rows=[(x1p, D_MODEL, 0), (dp, D_MODEL, 0), (dh, D_MODEL, 0), (dx, D_MODEL, 0)],
            exs=[g2p, sc1, sh1], pars=[w["norm1_w"]],
            row_outs=[(D_MODEL, F32, D_MODEL, 0), (D_MODEL, BF16, D_MODEL, 0)], ex_outs=[D_MODEL] * 3,
            par_outs=[(1, D_MODEL)])
        nxt = (dx1p, ddp, dg2p)
    return gr, (dsh1, dsc1, dg1, dsh2, dsc2), nxt, got


def _lane_table(lanes_neg, lanes_pos, inv):
    freq = np.zeros((LANES,), np.int64) - 1
    sign = np.zeros((1, LANES), np.float32)
    n = len(lanes_neg)
    freq[lanes_neg] = np.arange(n)
    freq[lanes_pos] = np.arange(n)
    sign[0, lanes_neg] = -1.0
    sign[0, lanes_pos] = 1.0
    return _select_axis(inv, freq, 0).reshape(1, LANES), jnp.asarray(sign)


SHARDED = ("ada_w", "w_in", "b_w_uq", "b_w_ukv", "w_out", "w_gate_up", "w_down")
ROW_SHARDED = ("w_out", "w_down")
SMALL = ("ada_b", "norm1_w", "a_sinks", "b_q_norm_w", "b_kv_norm_w", "c_ln_w", "c_ln_b", "c_w_s", "c_b_s",
         "out_norm_w", "norm2_w", "final_norm_w")
FWD_HOST = {"attn_b_fwd": ("w_gate_up", "w_down"), "attn_a_fwd": ("w_in", "w_out", "b_w_uq", "b_w_ukv")}
FWD_HOST_FIRST = {"attn_b_fwd": ("w_gate_up", "w_down"), "mm_gate_up": ("w_in", "w_out", "b_w_uq", "b_w_ukv")}
BWD_HOST = {"attn_a_bwd": ("w_in", "w_out", "b_w_uq", "b_w_ukv"), "attn_b_bwd": ("w_gate_up", "w_down")}
FIRST_LATE = {"mm_in": ("b_w_uq", "b_w_ukv", "w_out"), "attn_a_fwd": ("w_gate_up", "w_down")}
EXPOSED = ("w_in", "b_w_uq", "b_w_ukv")
OWN_LAYER = ("w_gate_up", "w_down")
EXCHANGED = SHARDED[1:]
N_ADA = N_MOD * D_MODEL // N_CHIPS
assert not set(FIRST_LATE) & set(FWD_HOST_FIRST)


def _from_host(table, got):
    return {k: got[name][i] for name, ks in table.items() for i, k in enumerate(ks)}


def _to_host(table, arrays):
    return {name: [arrays[k] for k in ks] for name, ks in table.items()}


def _join_cols(g):
    return jnp.concatenate([g[j] for j in range(N_CHIPS)], axis=1)


def _split_cols(g):
    n = g.shape[1] // N_CHIPS
    return jnp.stack([g[:, j * n:(j + 1) * n] for j in range(N_CHIPS)])


def _layer_weights(G, small, l):
    D = D_MODEL
    return {
        **_large_operands(G),
        "norm1_w": small["norm1_w"][l].reshape(1, D),
        "sinks": jnp.broadcast_to(small["a_sinks"][l].reshape(A_Q_HEADS, 1, 1), (A_Q_HEADS, 1, LANES)),
        "b_q_norm_w": small["b_q_norm_w"][l].reshape(1, B_Q_RANK),
        "b_kv_norm_w": small["b_kv_norm_w"][l].reshape(1, B_KV_RANK),
        "c_ln_w": small["c_ln_w"][l].reshape(1, C_WIDTH), "c_ln_b": small["c_ln_b"][l].reshape(1, C_WIDTH),
        "c_w_s": small["c_w_s"][l], "c_b_col": small["c_b_s"][l].reshape(C_GROUPS, C_CHUNK, 1),
        "out_norm_w": small["out_norm_w"][l].reshape(1, D), "norm2_w": small["norm2_w"][l].reshape(1, D),
    }


def _large_operands(G):
    D = D_MODEL
    make = {
        "w_in": lambda g: _pad_axis(_join_cols(g), _map_w_in(), 1),
        "b_w_uq": lambda g: _pad_axis(_join_cols(g), _map_w_uq(), 1),
        "b_w_ukv": lambda g: _pad_axis(_join_cols(g), _map_w_ukv(), 1),
        "w_out": lambda g: g.reshape(D, D), "w_gate_up": lambda g: g, "w_down": lambda g: g.reshape(FFN_HIDDEN, D),
    }
    return {k: make[k](g) for k, g in G.items()}


def _send_buffers(gr):
    D = D_MODEL
    return {
        "w_gate_up": gr["w_gate_up"], "w_down": gr["w_down"].reshape(N_CHIPS, FFN_HIDDEN // N_CHIPS, D),
        "w_out": gr["w_out"].reshape(N_CHIPS, D // N_CHIPS, D),
        "w_in": _split_cols(_unpad_axis(gr["w_in"], _map_w_in(), IN_COLS, 1)),
        "b_w_uq": _split_cols(_unpad_axis(gr["b_w_uq"], _map_w_uq(), B_HEADS * (B_NOPE + B_ROPE), 1)),
        "b_w_ukv": _split_cols(_unpad_axis(gr["b_w_ukv"], _map_w_ukv(), B_HEADS * (B_NOPE + B_V), 1)),
    }


def _small_grads(gr, B):
    D = D_MODEL
    return {
        "norm1_w": gr["norm1_w"].reshape(D),
        "a_sinks": gr["sinks"][:, 0, 0].reshape(B, A_Q_HEADS).sum(axis=0),
        "b_q_norm_w": gr["b_q_norm_w"].reshape(B_Q_RANK), "b_kv_norm_w": gr["b_kv_norm_w"].reshape(B_KV_RANK),
        "c_ln_w": gr["c_ln_w"].reshape(C_WIDTH), "c_ln_b": gr["c_ln_b"].reshape(C_WIDTH), "c_w_s": gr["c_w_s"],
        "c_b_s": gr["c_b_col"].reshape(C_GROUPS, C_CHUNK),
        "out_norm_w": gr["out_norm_w"].reshape(D), "norm2_w": gr["norm2_w"].reshape(D),
    }


def _step(x, c, positions, target, small, shard_of, ada=None, gathered=None):
    dist = gathered is None
    B, S, D = x.shape
    T = B * S
    xt = x.reshape(T, D)
    tgt = target.reshape(T, D)
    pos_col = positions.astype(F32).reshape(T, 1)
    inv_a = 1.0 / (ROPE_THETA ** (jnp.arange(0, HEAD_DIM, 2, dtype=F32) / HEAD_DIM))
    inv_b = 1.0 / (ROPE_THETA ** (jnp.arange(0, B_ROPE, 2, dtype=F32) / B_ROPE))
    fa, sga = _lane_table(np.arange(32), 64 + np.arange(32), inv_a)
    fb, sgb = _lane_table(48 + np.arange(16), 112 + np.arange(16), inv_b)
    ca, sa = _rope_tables(pos_col, fa, sga, jnp.abs(sga), "rope_a")
    cb, sb = _rope_tables(pos_col, fb, sgb, jnp.ones_like(sgb), "rope_b")
    tabs = (ca, sa, cb, sb)
    ada_b = small["ada_b"]
    if dist:
        assert N_CHIPS * B == 8
        me = 2 * lax.axis_index("x") + lax.axis_index("y")
        first = shard_of(0)
        c_all, w_in_first = _xchg_call("gather", [c, first["w_in"]], "gather_first")
        c8 = c_all.reshape(N_CHIPS * B, D)
        mine = lax.dynamic_slice_in_dim(ada_b, me * N_ADA, N_ADA, axis=1).reshape(DEPTH, 1, N_ADA)
        part = _ada_fwd(c8, ada, mine)
        part = part.reshape(DEPTH, N_CHIPS, B, N_ADA).transpose(1, 0, 2, 3).reshape(N_CHIPS, DEPTH * B, N_ADA)
        (back,) = _xchg_call("a2a", [part], "mod_exchange")
        mod_all = jnp.concatenate([back[j].reshape(DEPTH, B, N_ADA) for j in range(N_CHIPS)], axis=-1)
        G = {"w_in": w_in_first}
    else:
        c8 = jnp.zeros((8, D), F32).at[:B].set(c)
        cols = [(jnp.stack([gathered[l]["ada_w"][j] for l in range(DEPTH)]),
                 ada_b[:, j * N_ADA:(j + 1) * N_ADA].reshape(DEPTH, 1, N_ADA)) for j in range(N_CHIPS)]
        mod_all = jnp.concatenate([_ada_fwd(c8, wj, bj)[:, :B] for wj, bj in cols], axis=-1)
        G = {k: v for k, v in gathered[0].items() if k != "ada_w"}
    saved, prevs, modss = [], [], []
    prev = None
    for l in range(DEPTH):
        w = _layer_weights(G, small, l)
        mods = tuple(mod_all[l, :, i * D:(i + 1) * D].reshape(B, 1, D) for i in range(N_MOD))
        more = l + 1 < DEPTH
        table = FWD_HOST_FIRST if l == 0 else FWD_HOST
        host = _to_host(table, shard_of(l + 1)) if dist and more else {}
        late = None
        if dist and l == 0:
            late = FIRST_LATE
            for name, ks in late.items():
                host[name] = [first[k] for k in ks] + host.get(name, [])
        prevs.append(prev)
        modss.append(mods)
        prev, sv, got = _layer_fwd(xt, prev, mods, w, tabs, B, S, host, late)
        saved.append(sv)
        if more:
            G = _from_host(table, got) if dist else {k: v for k, v in gathered[l + 1].items() if k != "ada_w"}
    x1, d, g2 = prev
    loss, dx1, dd, dg2, dfw = _final(x1, d, g2, small["final_norm_w"].reshape(1, D), tgt, B, S)

    landed = [None] * DEPTH
    smalls = [None] * DEPTH
    dmods = [None] * DEPTH
    pending = None
    for l in reversed(range(DEPTH)):
        host = _to_host(BWD_HOST, pending) if dist and pending is not None else None
        early = dist and l == 0 and host is not None
        gr, (dsh1, dsc1, dg1, dsh2, dsc2), nxt, got = _layer_bwd(saved[l], prevs[l], modss[l], tabs, dx1, dd, B, S, host, early)
        if pending is not None:
            landed[l + 1] = _from_host(BWD_HOST, got) if dist else pending
        dmods[l] = jnp.concatenate([dsh1, dsc1, dg1, dsh2, dsc2, dg2], axis=-1).reshape(B, N_MOD * D)
        pending = _send_buffers(gr)
        smalls[l] = _small_grads(gr, B)
        dx1, dd, dg2 = nxt
    dmod_all = jnp.stack(dmods)
    by_chip = [dmod_all[:, :, j * N_ADA:(j + 1) * N_ADA] for j in range(N_CHIPS)]
    if dist:
        last = EXPOSED if early else EXCHANGED
        send = jnp.stack([p.reshape(DEPTH * B, N_ADA) for p in by_chip])
        *res, back = _xchg_call("a2a", [pending[k] for k in last] + [send], "grad_exchange_last")
        landed[0] = dict(zip(last, res))
        if early:
            landed[0].update(zip(OWN_LAYER, got["attn_b_bwd"][-len(OWN_LAYER):]))
            landed[0]["w_out"] = got["attn_a_bwd"][-1]
        dmod8 = back.reshape(N_CHIPS, DEPTH, B, N_ADA).transpose(1, 0, 2, 3).reshape(DEPTH, N_CHIPS * B, N_ADA)
        ada_g = _ada_bwd(c8, dmod8)
    else:
        landed[0] = pending
        ada_g = jnp.stack([_ada_bwd(c8, jnp.zeros((DEPTH, 8, N_ADA), F32).at[:, :B].set(p)) for p in by_chip])
    small_g = {k: jnp.stack([smalls[l][k] for l in range(DEPTH)]) for k in SMALL if k not in ("final_norm_w", "ada_b")}
    small_g["ada_b"] = _sum_examples(dmod_all).reshape(DEPTH, N_MOD * D)
    small_g["final_norm_w"] = dfw.reshape(D)
    return loss[0, 0], dx1.reshape(B, S, D), landed, small_g, ada_g


def _pack(arrs, cols, mult):
    flat = jnp.concatenate([a.reshape(-1) for a in arrs])
    n = flat.shape[0]
    rows = -(-n // cols)
    rows = -(-rows // mult) * mult
    return jnp.pad(flat, (0, rows * cols - n)).reshape(rows, cols)


def _unpack(blob, shapes):
    flat = blob.reshape(-1)
    out, off = [], 0
    for s in shapes:
        n = int(np.prod(s))
        out.append(flat[off:off + n].reshape(s))
        off += n
    return out


def kernel(x, c, positions, ada_w, ada_b, norm1_w, w_in, a_sinks, b_q_norm_w, b_w_uq, b_kv_norm_w, b_w_ukv, c_ln_w, c_ln_b, c_w_s, c_b_s, out_norm_w, w_out, norm2_w, w_gate_up, w_down, final_norm_w, loss_target, m_ada_w, m_ada_b, m_norm1_w, m_w_in, m_a_sinks, m_b_q_norm_w, m_b_w_uq, m_b_kv_norm_w, m_b_w_ukv, m_c_ln_w, m_c_ln_b, m_c_w_s, m_c_b_s, m_out_norm_w, m_w_out, m_norm2_w, m_w_gate_up, m_w_down, m_final_norm_w, v_ada_w, v_ada_b, v_norm1_w, v_w_in, v_a_sinks, v_b_q_norm_w, v_b_w_uq, v_b_kv_norm_w, v_b_w_ukv, v_c_ln_w, v_c_ln_b, v_c_w_s, v_c_b_s, v_out_norm_w, v_w_out, v_norm2_w, v_w_gate_up, v_w_down, v_final_norm_w):
    names = ("ada_w", "ada_b", "norm1_w", "w_in", "a_sinks", "b_q_norm_w", "b_w_uq", "b_kv_norm_w", "b_w_ukv", "c_ln_w",
             "c_ln_b", "c_w_s", "c_b_s", "out_norm_w", "w_out", "norm2_w", "w_gate_up", "w_down", "final_norm_w")
    ws = dict(zip(names, (ada_w, ada_b, norm1_w, w_in, a_sinks, b_q_norm_w, b_w_uq, b_kv_norm_w, b_w_ukv, c_ln_w, c_ln_b,
                          c_w_s, c_b_s, out_norm_w, w_out, norm2_w, w_gate_up, w_down, final_norm_w)))
    ms = dict(zip(names, (m_ada_w, m_ada_b, m_norm1_w, m_w_in, m_a_sinks, m_b_q_norm_w, m_b_w_uq, m_b_kv_norm_w, m_b_w_ukv,
                          m_c_ln_w, m_c_ln_b, m_c_w_s, m_c_b_s, m_out_norm_w, m_w_out, m_norm2_w, m_w_gate_up, m_w_down,
                          m_final_norm_w)))
    vs = dict(zip(names, (v_ada_w, v_ada_b, v_norm1_w, v_w_in, v_a_sinks, v_b_q_norm_w, v_b_w_uq, v_b_kv_norm_w, v_b_w_ukv,
                          v_c_ln_w, v_c_ln_b, v_c_w_s, v_c_b_s, v_out_norm_w, v_w_out, v_norm2_w, v_w_gate_up, v_w_down,
                          v_final_norm_w)))
    shards = {k: ws[k].astype(BF16) for k in SHARDED}
    loss_local, grad_x, landed, gsmall, ada_g = _step(
        x, c, positions, loss_target, {k: ws[k] for k in SMALL}, lambda l: {k: shards[k][l] for k in EXCHANGED},
        ada=shards["ada_w"])

    mine = {k: jnp.stack([_sum_slots(landed[l][k], "grad_sum_" + k) for l in range(DEPTH)]) for k in EXCHANGED}
    mine["ada_w"] = ada_g
    small_shapes = [ws[k].shape for k in SMALL]
    sblob = _pack([gsmall[k] for k in SMALL] + [loss_local.reshape(1)], LANES, 8)
    sred, swapped = _all_reduce_small(sblob, "small_all_reduce", swaps=[mine[k] for k in SHARDED])
    theirs = dict(zip(SHARDED, swapped))
    grads, delta, new_m, new_v = {}, {}, {}, {}
    for k in SHARDED:
        shp = ws[k].shape
        two = (shp[0] * shp[1], shp[2])
        g, dlt, nm, nv = _adamw(ws[k].reshape(two), mine[k].reshape(two), theirs[k].reshape(two), ms[k].reshape(two),
                                vs[k].reshape(two), "adamw_" + k)
        grads[k], delta[k], new_m[k], new_v[k] = g.reshape(shp), dlt.reshape(shp), nm.reshape(shp), nv.reshape(shp)

    svals = _unpack(sred, small_shapes + [(1,)])
    loss = svals[-1].reshape(())
    pw = _pack([ws[k] for k in SMALL], LANES, 8)
    pg = _pack(svals[:-1], LANES, 8)
    pm = _pack([ms[k] for k in SMALL], LANES, 8)
    pv = _pack([vs[k] for k in SMALL], LANES, 8)
    g, dlt, nm, nv = _adamw(pw, pg, None, pm, pv, "adamw_small")
    for k, a, b_, c_, d_ in zip(SMALL, _unpack(g, small_shapes), _unpack(dlt, small_shapes), _unpack(nm, small_shapes),
                                _unpack(nv, small_shapes)):
        grads[k], delta[k], new_m[k], new_v[k] = a, b_, c_, d_

    return (loss, grad_x, *[grads[k] for k in names], *[delta[k] for k in names], *[new_m[k] for k in names],
            *[new_v[k] for k in names])
```

```python
import functools
import math

import numpy as np
import jax
import jax.numpy as jnp
from jax import lax
from jax.experimental import pallas as pl
from jax.experimental.pallas import tpu as pltpu

F32 = jnp.float32
BF16 = jnp.bfloat16

D_MODEL = 1024
DEPTH = 4
HEAD_DIM = 64
ROPE_THETA = 10000.0
NORM_EPS = 1e-6
NEG_INF = -1e30
LOG2_E = math.log2(math.e)
A_Q_HEADS = 6
A_KV_HEADS = 2
A_WINDOW = 128
B_HEADS = 6
B_Q_RANK = 384
B_KV_RANK = 256
B_NOPE = 64
B_ROPE = 32
B_V = 64
C_GROUPS = 4
C_GROUP_DIM = 64
C_WIDTH = 256
C_CHUNK = 128
IN_COLS = 1824
FFN_HIDDEN = 2816
N_MOD = 6
ADAM_LR = 0.001
ADAM_B1 = 0.9
ADAM_B2 = 0.999
ADAM_EPS = 1e-08
ADAM_WD = 0.01
ADAM_STEP = 10

LANES = 128
VMEM_LIMIT = 56 * 1024 * 1024
N_CHIPS = 4
WINDOW_SUB = 256

P_KR, P_AQ, P_AK, P_AV, P_CQ, P_CKV, P_MIX, P_CU, P_CV, P_END = 0, 128, 512, 640, 768, 1152, 1408, 1536, 1792, 2048


def _map_w_in():
    idx = -np.ones(P_END, np.int64)
    idx[P_AQ:P_AQ + 384] = np.arange(384)
    idx[P_AK:P_AK + 128] = 384 + np.arange(128)
    idx[P_AV:P_AV + 128] = 512 + np.arange(128)
    idx[P_CQ:P_CQ + 384] = 640 + np.arange(384)
    idx[P_CKV:P_CKV + 256] = 1024 + np.arange(256)
    idx[P_KR + 48 + np.arange(16)] = 1280 + np.arange(16)
    idx[P_KR + 112 + np.arange(16)] = 1296 + np.arange(16)
    idx[P_CU:P_CU + 256] = 1312 + np.arange(256)
    idx[P_CV:P_CV + 256] = 1568 + np.arange(256)
    return idx


def _map_w_uq():
    idx = -np.ones(B_HEADS * LANES, np.int64)
    for h in range(B_HEADS):
        b = h * (B_NOPE + B_ROPE)
        idx[h * LANES + np.arange(48)] = b + np.arange(48)
        idx[h * LANES + 48 + np.arange(16)] = b + 64 + np.arange(16)
        idx[h * LANES + 64 + np.arange(16)] = b + 48 + np.arange(16)
        idx[h * LANES + 112 + np.arange(16)] = b + 80 + np.arange(16)
    return idx


def _map_w_ukv():
    idx = -np.ones(2 * B_HEADS * LANES, np.int64)
    for h in range(B_HEADS):
        b = h * (B_NOPE + B_V)
        idx[h * LANES + np.arange(48)] = b + np.arange(48)
        idx[h * LANES + 64 + np.arange(16)] = b + 48 + np.arange(16)
        idx[B_HEADS * LANES + h * LANES + np.arange(B_V)] = b + B_NOPE + np.arange(B_V)
    return idx


def _inverse(idx, n):
    inv = np.zeros(n, np.int64)
    pos = np.nonzero(idx >= 0)[0]
    inv[idx[pos]] = pos
    return inv


def _runs(idx):
    runs, i, n = [], 0, len(idx)
    while i < n:
        j = i + 1
        while j < n and ((idx[i] < 0 and idx[j] < 0) or (idx[i] >= 0 and idx[j] == idx[i] + (j - i))):
            j += 1
        runs.append((int(idx[i]), j - i))
        i = j
    return runs


def _select_axis(w, idx, axis):
    pieces = []
    for start, length in _runs(idx):
        if start < 0:
            shape = list(w.shape)
            shape[axis] = length
            pieces.append(jnp.zeros(shape, w.dtype))
        else:
            pieces.append(lax.slice_in_dim(w, start, start + length, axis=axis))
    return jnp.concatenate(pieces, axis=axis)


def _pad_axis(w, idx, axis):
    return _select_axis(w, idx, axis)


def _unpad_axis(g, idx, n, axis):
    return _select_axis(g, _inverse(idx, n), axis)


def _params(sem):
    return pltpu.CompilerParams(dimension_semantics=sem, vmem_limit_bytes=VMEM_LIMIT)


def _tile(dim, target):
    if dim <= target:
        return dim
    best = None
    for t in range(LANES, target + 1, LANES):
        if dim % t == 0:
            best = t
    assert best is not None, dim
    return best


def _row_div(rows, target):
    if rows <= target:
        return rows
    best = None
    for t in range(8, target + 1, 8):
        if rows % t == 0:
            best = t
    assert best is not None, rows
    return best


_ANY = pl.BlockSpec(memory_space=pl.ANY)
_MESH = pl.DeviceIdType.MESH


def _xchg_out_shapes(kind, srcs):
    if kind == "gather":
        return [jax.ShapeDtypeStruct((N_CHIPS,) + s.shape, s.dtype) for s in srcs]
    return [jax.ShapeDtypeStruct(s.shape, s.dtype) for s in srcs]


def _xchg_scratch(kind, n):
    per = 1 if kind == "swap" else N_CHIPS - 1
    return [pltpu.SemaphoreType.DMA((per * n,)), pltpu.SemaphoreType.DMA((per * n,)), pltpu.SemaphoreType.DMA((n,))]


def _xchg_copies(kind, srcs, dsts, send_sems, recv_sems, local_sems, arrivals):
    x, y, c = lax.axis_index("x"), lax.axis_index("y"), lax.axis_index("c")
    me = 2 * x + y
    peers = [(1 - x, y), (x, 1 - y), (1 - x, 1 - y)]
    local, out, back = [], [], []
    for i, (s, d) in enumerate(zip(srcs, dsts)):
        if kind == "swap":
            cp = pltpu.make_async_remote_copy(src_ref=s, dst_ref=d, send_sem=send_sems.at[i], recv_sem=recv_sems.at[i],
                                              device_id=(x, y, 1 - c), device_id_type=_MESH)
            out.append(cp)
            back.append(cp)
            continue
        local.append(pltpu.make_async_copy(s if kind == "gather" else s.at[me], d.at[me], local_sems.at[i]))
        for kk, (px, py) in enumerate(peers):
            j = (N_CHIPS - 1) * i + kk
            theirs = 2 * px + py
            out.append(pltpu.make_async_remote_copy(
                src_ref=s if kind == "gather" else s.at[theirs], dst_ref=d.at[me], send_sem=send_sems.at[j],
                recv_sem=recv_sems.at[j], device_id=(px, py, c), device_id_type=_MESH))
            if arrivals:
                back.append(pltpu.make_async_remote_copy(
                    src_ref=s if kind == "gather" else s.at[me], dst_ref=d.at[theirs], send_sem=send_sems.at[j],
                    recv_sem=recv_sems.at[j], device_id=(px, py, c), device_id_type=_MESH))
    return local, out, back


def _xchg_start(kind, srcs, dsts, sems):
    local, out, _ = _xchg_copies(kind, srcs, dsts, *sems, arrivals=False)
    for cp in local + out:
        cp.start()


def _xchg_wait(kind, srcs, dsts, sems):
    local, out, back = _xchg_copies(kind, srcs, dsts, *sems, arrivals=True)
    for cp in back:
        cp.wait_recv()
    for cp in out:
        cp.wait_send()
    for cp in local:
        cp.wait()


def _xchg_at_ends(kind, srcs, dsts, sems, grid, first):
    ids = [pl.program_id(a) for a in range(len(grid))]
    cond = None
    for i, n in zip(ids, grid):
        c = (i == 0) if first else (i == n - 1)
        cond = c if cond is None else jnp.logical_and(cond, c)

    @pl.when(cond)
    def _():
        (_xchg_start if first else _xchg_wait)(kind, srcs, dsts, sems)


def _xchg_call(kind, srcs, name):
    n = len(srcs)

    def kern(*refs):
        s, d, sems = refs[:n], refs[n:2 * n], refs[2 * n:]
        _xchg_start(kind, s, d, sems)
        _xchg_wait(kind, s, d, sems)

    return pl.pallas_call(
        kern, name=name, in_specs=[_ANY] * n, out_specs=[_ANY] * n, out_shape=_xchg_out_shapes(kind, srcs),
        scratch_shapes=_xchg_scratch(kind, n),
    )(*srcs)


_DIMS = {"nn": (((1,), (0,)), ((), ())), "nt": (((1,), (1,)), ((), ())), "tn": (((0,), (0,)), ((), ()))}


def _matmul(a, b, mode, out_dtype, name, *, b_chunks=False, out_chunks=False, xchg=None):
    if b_chunks:
        nchunk, brows, bcols = b.shape
        bshape = (brows, nchunk * bcols)
    else:
        bshape = b.shape
    if mode == "nn":
        (m, k), (_, n) = a.shape, bshape
    elif mode == "nt":
        (m, k), (n, _) = a.shape, bshape
    else:
        (k, m), (_, n) = a.shape, bshape
    tm, tk = (1408, 1024) if mode == "tn" else (1024, 1408)
    tm, tn, tk = _tile(m, tm), _tile(n, 1408), _tile(k, tk)
    if b_chunks:
        if mode == "nn":
            tn = bcols
        else:
            assert mode == "nt"
            tk = bcols
    if out_chunks:
        assert n % N_CHIPS == 0
        tn = n // N_CHIPS
    ni, nj, nk = m // tm, n // tn, k // tk
    dims = _DIMS[mode]
    n_x = 0 if xchg is None else len(xchg[1])

    def kern(*refs):
        a_ref, b_ref = refs[0], refs[1]
        xs = refs[2:2 + n_x]
        o_ref = refs[2 + n_x]
        xd = refs[3 + n_x:3 + 2 * n_x]
        acc_ref = refs[3 + 2 * n_x]
        sems = refs[4 + 2 * n_x:]
        kk = pl.program_id(2)
        if n_x:
            _xchg_at_ends(xchg[0], xs, xd, sems, (ni, nj, nk), True)

        @pl.when(kk == 0)
        def _():
            acc_ref[...] = jnp.zeros_like(acc_ref)

        acc_ref[...] += lax.dot_general(a_ref[...], b_ref[...], dims, preferred_element_type=F32)

        @pl.when(kk == nk - 1)
        def _():
            o_ref[...] = acc_ref[...].astype(o_ref.dtype)

        if n_x:
            _xchg_at_ends(xchg[0], xs, xd, sems, (ni, nj, nk), False)

    if mode == "tn":
        a_spec = pl.BlockSpec((tk, tm), lambda i, j, kk: (kk, i))
    else:
        a_spec = pl.BlockSpec((tm, tk), lambda i, j, kk: (i, kk))
    if b_chunks and mode == "nn":
        b_spec = pl.BlockSpec((None, tk, tn), lambda i, j, kk: (j, kk, 0))
    elif b_chunks:
        b_spec = pl.BlockSpec((None, tn, tk), lambda i, j, kk: (kk, j, 0))
    elif mode == "nt":
        b_spec = pl.BlockSpec((tn, tk), lambda i, j, kk: (j, kk))
    else:
        b_spec = pl.BlockSpec((tk, tn), lambda i, j, kk: (kk, j))
    if out_chunks:
        o_spec = pl.BlockSpec((None, tm, tn), lambda i, j, kk: (j, i, 0))
        o_shape = jax.ShapeDtypeStruct((N_CHIPS, m, tn), out_dtype)
    else:
        o_spec = pl.BlockSpec((tm, tn), lambda i, j, kk: (i, j))
        o_shape = jax.ShapeDtypeStruct((m, n), out_dtype)
    xs = [] if xchg is None else list(xchg[1])
    res = pl.pallas_call(
        kern, name=name, grid=(ni, nj, nk),
        in_specs=[a_spec, b_spec] + [_ANY] * n_x, out_specs=[o_spec] + [_ANY] * n_x,
        out_shape=[o_shape] + (_xchg_out_shapes(xchg[0], xs) if n_x else []),
        scratch_shapes=[pltpu.VMEM((tm, tn), F32)] + (_xchg_scratch(xchg[0], n_x) if n_x else []),
        compiler_params=_params(("arbitrary", "arbitrary", "arbitrary") if n_x else ("parallel", "parallel", "arbitrary")),
    )(a, b, *xs)
    return (res[0], list(res[1:])) if n_x else res[0]


def _rowcall(body, *, name, T, S, tr, rows, exs=(), pars=(), row_outs=(), ex_outs=(), par_outs=(), aliases=None):
    assert S % tr == 0 and T % S == 0
    per_ex = S // tr
    nb = T // S
    n_rows, n_exs, n_pars = len(rows), len(exs), len(pars)
    n_ro, n_eo, n_po = len(row_outs), len(ex_outs), len(par_outs)

    def kern(*refs):
        ins = refs[:n_rows + n_exs + n_pars]
        outs = refs[n_rows + n_exs + n_pars:]
        rv = [r[...].astype(F32) for r in ins[:n_rows]]
        ev = [r[0] for r in ins[n_rows:n_rows + n_exs]]
        pv = [r[...] for r in ins[n_rows + n_exs:]]
        ro, eo, po = body(rv, ev, pv)
        i = pl.program_id(0)
        for ref, val in zip(outs[:n_ro], ro):
            if isinstance(val, (list, tuple)):
                off = 0
                for piece in val:
                    w = piece.shape[-1]
                    ref[:, off:off + w] = piece.astype(ref.dtype)
                    off += w
            else:
                ref[...] = val.astype(ref.dtype)
        first_of_ex = (i % per_ex) == 0
        for ref, val in zip(outs[n_ro:n_ro + n_eo], eo):
            @pl.when(first_of_ex)
            def _(ref=ref, val=val):
                ref[0] = val

            @pl.when(jnp.logical_not(first_of_ex))
            def _(ref=ref, val=val):
                ref[0] += val
        for ref, val in zip(outs[n_ro + n_eo:], po):
            @pl.when(i == 0)
            def _(ref=ref, val=val):
                ref[...] = val

            @pl.when(i != 0)
            def _(ref=ref, val=val):
                ref[...] += val

    in_specs = [pl.BlockSpec((tr, w), functools.partial(lambda i, cb: (i, cb), cb=cb)) for (_, w, cb) in rows]
    in_specs += [pl.BlockSpec((1, 1, e.shape[-1]), lambda i: (i // per_ex, 0, 0)) for e in exs]
    in_specs += [pl.BlockSpec(p.shape, functools.partial(lambda i, nd: (0,) * nd, nd=p.ndim)) for p in pars]
    out_specs = [pl.BlockSpec((tr, w), functools.partial(lambda i, cb: (i, cb), cb=cb)) for (_, _, w, cb) in row_outs]
    out_specs += [pl.BlockSpec((1, 1, f), lambda i: (i // per_ex, 0, 0)) for f in ex_outs]
    out_specs += [pl.BlockSpec(tuple(s), functools.partial(lambda i, nd: (0,) * nd, nd=len(s))) for s in par_outs]
    out_shape = [jax.ShapeDtypeStruct((T, tw), dt) for (tw, dt, _, _) in row_outs]
    out_shape += [jax.ShapeDtypeStruct((nb, 1, f), F32) for f in ex_outs]
    out_shape += [jax.ShapeDtypeStruct(tuple(s), F32) for s in par_outs]
    res = pl.pallas_call(
        kern, name=name, grid=(T // tr,), in_specs=in_specs, out_specs=out_specs, out_shape=out_shape,
        input_output_aliases=aliases or {}, compiler_params=_params(("arbitrary",)),
    )(*[r[0] for r in rows], *exs, *pars)
    return res[:n_ro], res[n_ro:n_ro + n_eo], res[n_ro + n_eo:]


def _rms(x, w, n=None):
    n = x.shape[-1] if n is None else n
    ms = jnp.sum(x * x, axis=-1, keepdims=True) * (1.0 / n)
    return x * lax.rsqrt(ms + NORM_EPS) * w


def _gelu(x):
    return 0.5 * x * (1.0 + lax.erf(x * np.float32(1.0 / math.sqrt(2.0))))


def _silu(x):
    return x * jax.nn.sigmoid(x)


@jax.custom_vjp
def _rope(x, cos, sin):
    return x * cos + pltpu.roll(x, 64, 1) * sin


def _rope_fwd(x, cos, sin):
    return _rope(x, cos, sin), (cos, sin)


def _rope_bwd(res, dy):
    cos, sin = res
    return dy * cos + pltpu.roll(dy * sin, 64, 1), None, None


_rope.defvjp(_rope_fwd, _rope_bwd)


def _heads(x, n):
    return [x[:, h * LANES:(h + 1) * LANES] for h in range(n)]


@functools.partial(jax.custom_vjp, nondiff_argnums=(1,))
def _lroll(x, shift):
    return pltpu.roll(x, shift, 1)


def _lroll_fwd(x, shift):
    return _lroll(x, shift), None


def _lroll_bwd(shift, _, dy):
    return (pltpu.roll(dy, (LANES - shift) % LANES, 1),)


_lroll.defvjp(_lroll_fwd, _lroll_bwd)


def _spread_rotary(v):
    low = lax.broadcasted_iota(jnp.int32, (1, LANES), 1) < 64
    return [jnp.where(low, v, _lroll(v, 32)), jnp.where(low, _lroll(v, 64), _lroll(v, 96))]


def _spread_values(v):
    low = lax.broadcasted_iota(jnp.int32, (1, LANES), 1) < 64
    return [jnp.where(low, v, 0.0), jnp.where(low, _lroll(v, 64), 0.0)]


def _f_norm_mod(x, w, sc, sh):
    return _rms(x, w) * (1.0 + sc) + sh


def _f_resid_norm_mod(xa, delta, g, w, sc, sh):
    xn = xa + g * delta
    return xn, _f_norm_mod(xn, w, sc, sh)


def _f_mixprep(proj, ca, sa, cb, sb, qnw, kvnw):
    qa = [_rope(p, ca, sa) for pair in _heads(proj[:, P_AQ:P_AK], A_Q_HEADS // 2) for p in _spread_rotary(pair)]
    ka = [_rope(p, ca, sa) for p in _spread_rotary(proj[:, P_AK:P_AV])]
    va = _spread_values(proj[:, P_AV:P_CQ])
    cqn = _rms(proj[:, P_CQ:P_CKV], qnw)
    ckvn = _rms(proj[:, P_CKV:P_MIX], kvnw)
    kr = _rope(proj[:, P_KR:P_AQ], cb, sb)
    return jnp.concatenate(qa, -1), jnp.concatenate(ka, -1), jnp.concatenate(va, -1), cqn, ckvn, kr


def _f_mlaprep(q, kv, kr, cb, sb):
    qs = [_rope(p, cb, sb) for p in _heads(q, B_HEADS)]
    ks = [p + kr for p in _heads(kv[:, :B_HEADS * LANES], B_HEADS)]
    return jnp.concatenate(qs, -1), jnp.concatenate(ks, -1), kv[:, B_HEADS * LANES:]


def _f_sgu(cu, cv, ln_w, ln_b, w_s, b_col):
    u = _gelu(cu)
    v = _gelu(cv)
    mu = jnp.mean(v, axis=-1, keepdims=True)
    var = jnp.mean(jnp.square(v - mu), axis=-1, keepdims=True)
    vn = (v - mu) * lax.rsqrt(var + NORM_EPS) * ln_w + ln_b
    r = lax.broadcasted_iota(jnp.int32, (C_CHUNK, C_CHUNK), 0)
    c = lax.broadcasted_iota(jnp.int32, (C_CHUNK, C_CHUNK), 1)
    lane = lax.broadcasted_iota(jnp.int32, (1, LANES), 1)
    per_block = LANES // C_GROUP_DIM
    blocks = []
    for blk, vb in enumerate(_heads(vn, C_WIDTH // LANES)):
        mixed = jnp.zeros(vb.shape, F32)
        for j in range(per_block):
            g = blk * per_block + j
            gm = (lane // C_GROUP_DIM == j).astype(F32)
            wg = jnp.where(r >= c, w_s[g], 0.0).astype(BF16)
            mixed = mixed + jnp.dot(wg, (vb * gm).astype(BF16), preferred_element_type=F32) + b_col[g] * gm
        blocks.append(mixed)
    return u * jnp.concatenate(blocks, -1)


@jax.custom_vjp
def _pack_pairs(x):
    heads = _heads(x, x.shape[-1] // LANES)
    return jnp.concatenate([heads[i] + pltpu.roll(heads[i + 1], 64, 1) for i in range(0, len(heads), 2)], -1)


def _pack_pairs_fwd(x):
    return _pack_pairs(x), None


def _pack_pairs_bwd(_, dy):
    out = []
    for p in _heads(dy, dy.shape[-1] // LANES):
        out += [p, pltpu.roll(p, 64, 1)]
    return (jnp.concatenate(out, -1),)


_pack_pairs.defvjp(_pack_pairs_fwd, _pack_pairs_bwd)


def _f_outnorm(oa, ob, yc, gw):
    na, nb = A_Q_HEADS * HEAD_DIM, B_HEADS * B_V
    ya = _rms(_pack_pairs(oa), gw[:, :na])
    yb = _rms(_pack_pairs(ob), gw[:, na:na + nb])
    ycn = _rms(yc, gw[:, na + nb:])
    return jnp.concatenate([ya, yb, ycn], -1)


def _f_swiglu(gate, up):
    return _silu(gate) * up


def _mask(q_start, k_start, tq, tk, window):
    qpos = q_start + lax.broadcasted_iota(jnp.int32, (tq, tk), 0)
    kpos = k_start + lax.broadcasted_iota(jnp.int32, (tq, tk), 1)
    m = kpos <= qpos
    if window is not None:
        m = jnp.logical_and(m, qpos - kpos < window)
    return m


def _tile_fwd(qv, kk, vv, q_start, k_start, n_free, scale, window, m0, l0):
    tq = qv.shape[0]
    W = kk.shape[0]
    c = scale * LOG2_E
    parts = []
    if n_free > 0:
        parts.append((lax.dot_general(qv, kk[:n_free], _DIMS["nt"], preferred_element_type=F32), vv[:n_free]))
    if W > n_free:
        s = lax.dot_general(qv, kk[n_free:], _DIMS["nt"], preferred_element_type=F32)
        s = jnp.where(_mask(q_start, k_start + n_free, tq, W - n_free, window), s, NEG_INF)
        parts.append((s, vv[n_free:]))
    m = None if m0 is None else m0 * (1.0 / scale)
    for s, _ in parts:
        mx = jnp.max(s, axis=-1, keepdims=True)
        m = mx if m is None else jnp.maximum(m, mx)
    l = None if l0 is None else l0 * jnp.exp2((m0 * (1.0 / scale) - m) * c)
    o = None
    for s, vpart in parts:
        p = jnp.exp2((s - m) * c)
        ps = jnp.sum(p, axis=-1, keepdims=True)
        l = ps if l is None else l + ps
        po = jnp.dot(p.astype(BF16), vpart, preferred_element_type=F32)
        o = po if o is None else o + po
    return o / l, m * scale + jnp.log(l)


def _tile_bwd(qv, kk, vv, dof, ov, lse, q_start, k_start, n_free, scale, window):
    tq = qv.shape[0]
    W = kk.shape[0]
    dob = dof.astype(BF16)
    delta = jnp.sum(dof * ov, axis=-1, keepdims=True)
    c = scale * LOG2_E
    lse2 = lse * LOG2_E
    dq = None
    outs = []
    for (a, b, masked) in ((0, n_free, False), (n_free, W, True)):
        if b <= a:
            continue
        kp, vp = kk[a:b], vv[a:b]
        s = lax.dot_general(qv, kp, _DIMS["nt"], preferred_element_type=F32)
        if masked:
            s = jnp.where(_mask(q_start, k_start + a, tq, b - a, window), s, NEG_INF)
        p = jnp.exp2(s * c - lse2)
        dp = lax.dot_general(dob, vp, _DIMS["nt"], preferred_element_type=F32)
        ds = (p * ((dp - delta) * scale)).astype(BF16)
        d = jnp.dot(ds, kp, preferred_element_type=F32)
        dq = d if dq is None else dq + d
        dkp = lax.dot_general(ds, qv, _DIMS["tn"], preferred_element_type=F32)
        dvp = lax.dot_general(p.astype(BF16), dob, _DIMS["tn"], preferred_element_type=F32)
        outs.append((a, dkp, dvp))
    return dq, outs


def _attn_fwd(q, k, v, sinks, *, B, S, HQ, HK, window, scale, tq, band, name, xchg=None):
    G = HQ // HK
    nq = S // tq
    T = B * S
    has_sink = sinks is not None
    n_x = 0 if xchg is None else len(xchg[1])
    n_in = 4 if has_sink else 3
    hp = G if window is not None else 1
    grid = (B, HQ // hp, nq)

    def kern(*refs):
        xs, xd, sems = refs[n_in:n_in + n_x], refs[n_in + n_x + 2:n_in + 2 * n_x + 2], refs[n_in + 2 * n_x + 2:]
        refs = refs[:n_in] + refs[n_in + n_x:n_in + n_x + 2]
        if n_x:
            _xchg_at_ends(xchg[0], xs, xd, sems, grid, True)
        if has_sink:
            q_ref, k_ref, v_ref, s_ref, o_ref, lse_ref = refs
        else:
            q_ref, k_ref, v_ref, o_ref, lse_ref = refs
        q_start = pl.program_id(2) * tq

        def sink(g, rows):
            if not has_sink:
                return None, None
            return jnp.broadcast_to(s_ref[g][:, :1], (rows, 1)), jnp.ones((rows, 1), F32)

        def finish(o, lse):
            o_ref[...] = o.astype(o_ref.dtype)
            lse_ref[...] = jnp.broadcast_to(lse, (tq, LANES))

        if window is None:
            qv = q_ref[...]
            m0, l0 = sink(0, tq)
            bidx = q_start // band
            for bb in range(S // band):
                @pl.when(bidx == bb)
                def _(bb=bb):
                    W = (bb + 1) * band
                    finish(*_tile_fwd(qv, k_ref[0:W, :], v_ref[0:W, :], q_start, 0, bb * band, scale, None, m0, l0))
        else:
            sub = min(tq, WINDOW_SUB)
            W = min(S, sub + window)
            for r in range(0, tq, sub):
                k_start = pl.multiple_of(jnp.maximum(q_start + r - window, 0), window)
                kk, vv = k_ref[pl.ds(k_start, W), :], v_ref[pl.ds(k_start, W), :]
                for g in range(hp):
                    cols = slice(g * LANES, (g + 1) * LANES)
                    o, lse = _tile_fwd(q_ref[r:r + sub, cols], kk, vv, q_start + r, k_start, 0, scale, window, *sink(g, sub))
                    o_ref[r:r + sub, cols] = o.astype(o_ref.dtype)
                    lse_ref[r:r + sub, cols] = jnp.broadcast_to(lse, (sub, LANES))
        if n_x:
            _xchg_at_ends(xchg[0], xs, xd, sems, grid, False)

    q_spec = pl.BlockSpec((tq, hp * LANES), lambda b, h, i: (b * nq + i, h))
    kv_spec = pl.BlockSpec((S, LANES), lambda b, h, i: (b, h * hp // G))
    in_specs = [q_spec, kv_spec, kv_spec]
    args = [q, k, v]
    if has_sink:
        in_specs.append(pl.BlockSpec((hp, 1, LANES), lambda b, h, i: (h, 0, 0)))
        args.append(sinks)
    xs = [] if xchg is None else list(xchg[1])
    res = pl.pallas_call(
        kern, name=name, grid=grid, in_specs=in_specs + [_ANY] * n_x, out_specs=[q_spec, q_spec] + [_ANY] * n_x,
        out_shape=[jax.ShapeDtypeStruct((T, HQ * LANES), BF16), jax.ShapeDtypeStruct((T, HQ * LANES), F32)]
        + (_xchg_out_shapes(xchg[0], xs) if n_x else []),
        scratch_shapes=_xchg_scratch(xchg[0], n_x) if n_x else [],
        compiler_params=_params(("arbitrary",) * 3 if n_x else ("parallel", "parallel", "arbitrary")),
    )(*args, *xs)
    return ((res[0], res[1]), list(res[2:])) if n_x else res


def _attn_bwd(q, k, v, o, lse, do, sinks, *, B, S, HQ, HK, window, scale, tq, band, name, xchg=None):
    G = HQ // HK
    nq = S // tq
    T = B * S
    has_sink = sinks is not None
    n_x = 0 if xchg is None else len(xchg[1])
    n_in, n_out = (7, 4) if has_sink else (6, 3)
    hp = G if window is not None else 1
    ng = G // hp
    grid = (B, HK, ng, nq)

    def kern(*refs):
        xs, xd = refs[n_in:n_in + n_x], refs[n_in + n_x + n_out:n_in + 2 * n_x + n_out]
        dk_acc, dv_acc = refs[n_in + 2 * n_x + n_out:n_in + 2 * n_x + n_out + 2]
        sems = refs[n_in + 2 * n_x + n_out + 2:]
        refs = refs[:n_in] + refs[n_in + n_x:n_in + n_x + n_out]
        if n_x:
            _xchg_at_ends(xchg[0], xs, xd, sems, grid, True)
        if has_sink:
            q_ref, k_ref, v_ref, o_ref, lse_ref, do_ref, s_ref, dq_ref, dk_ref, dv_ref, ds_ref = refs
        else:
            q_ref, k_ref, v_ref, o_ref, lse_ref, do_ref, dq_ref, dk_ref, dv_ref = refs
        gi = pl.program_id(2)
        qi = pl.program_id(3)
        q_start = qi * tq

        @pl.when(jnp.logical_and(gi == 0, qi == 0))
        def _():
            dk_acc[...] = jnp.zeros_like(dk_acc)
            dv_acc[...] = jnp.zeros_like(dv_acc)

        if window is None:
            qv = q_ref[...]
            dof = do_ref[...].astype(F32)
            ov = o_ref[...].astype(F32)
            lse_v = lse_ref[...][:, :1]
            bidx = q_start // band
            for bb in range(S // band):
                @pl.when(bidx == bb)
                def _(bb=bb):
                    W = (bb + 1) * band
                    dq, outs = _tile_bwd(qv, k_ref[0:W, :], v_ref[0:W, :], dof, ov, lse_v, q_start, 0, bb * band, scale, None)
                    dq_ref[...] = dq.astype(dq_ref.dtype)
                    for a, dkp, dvp in outs:
                        dk_acc[a:a + dkp.shape[0], :] += dkp
                        dv_acc[a:a + dvp.shape[0], :] += dvp
        else:
            sub = min(tq, WINDOW_SUB)
            W = min(S, sub + window)
            parts = []
            sink_parts = []
            for g in range(hp):
                cols = slice(g * LANES, (g + 1) * LANES)
                acc = jnp.zeros((1, LANES), F32)
                for r in range(0, tq, sub):
                    rows = slice(r, r + sub)
                    k_start = pl.multiple_of(jnp.maximum(q_start + r - window, 0), window)
                    dof, ov = do_ref[rows, cols].astype(F32), o_ref[rows, cols].astype(F32)
                    lse_v = lse_ref[rows, cols][:, :1]
                    dq, outs = _tile_bwd(q_ref[rows, cols], k_ref[pl.ds(k_start, W), :], v_ref[pl.ds(k_start, W), :],
                                         dof, ov, lse_v, q_start + r, k_start, 0, scale, window)
                    dq_ref[rows, cols] = dq.astype(dq_ref.dtype)
                    parts.append((k_start, outs[0][1], outs[0][2]))
                    if has_sink:
                        delta = jnp.sum(dof * ov, axis=-1, keepdims=True)
                        acc = acc - jnp.sum(jnp.exp(s_ref[g][:, :1] - lse_v) * delta, axis=0, keepdims=True)
                sink_parts.append(acc)
            for k_start, dkp, dvp in parts:
                dk_acc[pl.ds(k_start, W), :] += dkp
                dv_acc[pl.ds(k_start, W), :] += dvp

        @pl.when(jnp.logical_and(gi == ng - 1, qi == nq - 1))
        def _():
            dk_ref[...] = dk_acc[...].astype(dk_ref.dtype)
            dv_ref[...] = dv_acc[...].astype(dv_ref.dtype)
        if has_sink:
            assert window is not None
            for g, part in enumerate(sink_parts):
                part = jnp.broadcast_to(part, (1, LANES))

                @pl.when(qi == 0)
                def _(g=g, part=part):
                    ds_ref[g] = part

                @pl.when(qi != 0)
                def _(g=g, part=part):
                    ds_ref[g] += part
        if n_x:
            _xchg_at_ends(xchg[0], xs, xd, sems, grid, False)

    q_spec = pl.BlockSpec((tq, hp * LANES), lambda b, hk, g, i: (b * nq + i, hk * ng + g))
    kv_spec = pl.BlockSpec((S, LANES), lambda b, hk, g, i: (b, hk))
    in_specs = [q_spec, kv_spec, kv_spec, q_spec, q_spec, q_spec]
    args = [q, k, v, o, lse, do]
    out_specs = [q_spec, kv_spec, kv_spec]
    out_shape = [jax.ShapeDtypeStruct((T, HQ * LANES), BF16), jax.ShapeDtypeStruct((T, HK * LANES), BF16),
                 jax.ShapeDtypeStruct((T, HK * LANES), BF16)]
    if has_sink:
        in_specs.append(pl.BlockSpec((hp, 1, LANES), lambda b, hk, g, i: (hk * ng + g, 0, 0)))
        args.append(sinks)
        out_specs.append(pl.BlockSpec((hp, 1, LANES), lambda b, hk, g, i: (b * HK * ng + hk * ng + g, 0, 0)))
        out_shape.append(jax.ShapeDtypeStruct((B * HQ, 1, LANES), F32))
    xs = [] if xchg is None else list(xchg[1])
    res = pl.pallas_call(
        kern, name=name, grid=grid, in_specs=in_specs + [_ANY] * n_x, out_specs=out_specs + [_ANY] * n_x,
        out_shape=out_shape + (_xchg_out_shapes(xchg[0], xs) if n_x else []),
        scratch_shapes=[pltpu.VMEM((S, LANES), F32)] * 2 + (_xchg_scratch(xchg[0], n_x) if n_x else []),
        compiler_params=_params(("arbitrary",) * 4 if n_x else ("parallel", "parallel", "arbitrary", "arbitrary")),
    )(*args, *xs)
    main = tuple(res[:n_out]) if has_sink else (*res[:n_out], None)
    return (main, list(res[n_out:])) if n_x else main


def _rope_tables(pos_col, freq, sign, keep, name):
    T = pos_col.shape[0]
    tr = _tile(T, 1024)

    def kern(p_ref, f_ref, s_ref, k_ref, c_out, s_out):
        ang = p_ref[...] * f_ref[...]
        c_out[...] = jnp.cos(ang) * k_ref[...]
        s_out[...] = jnp.sin(ang) * s_ref[...]

    spec = pl.BlockSpec((tr, LANES), lambda i: (i, 0))
    par = pl.BlockSpec((1, LANES), lambda i: (0, 0))
    return pl.pallas_call(
        kern, name=name, grid=(T // tr,), in_specs=[pl.BlockSpec((tr, 1), lambda i: (i, 0)), par, par, par],
        out_specs=[spec, spec], out_shape=[jax.ShapeDtypeStruct((T, LANES), F32)] * 2,
        compiler_params=_params(("parallel",)),
    )(pos_col, freq, sign, keep)


def _ada_fwd(c8, ada_w, ada_b):
    L, D, N = ada_w.shape

    def kern(c_ref, w_ref, b_ref, o_ref):
        act = _silu(c_ref[...]).astype(BF16)
        o_ref[0] = jnp.dot(act, w_ref[0], preferred_element_type=F32) + b_ref[0]

    return pl.pallas_call(
        kern, name="ada_fwd", grid=(L,),
        in_specs=[pl.BlockSpec((8, D), lambda l: (0, 0)), pl.BlockSpec((1, D, N), lambda l: (l, 0, 0)),
                  pl.BlockSpec((1, 1, N), lambda l: (l, 0, 0))],
        out_specs=pl.BlockSpec((1, 8, N), lambda l: (l, 0, 0)),
        out_shape=jax.ShapeDtypeStruct((L, 8, N), F32), compiler_params=_params(("parallel",)),
    )(c8, ada_w, ada_b)


def _ada_bwd(c8, dmod):
    L, _, N = dmod.shape
    D = c8.shape[1]

    def kern(c_ref, d_ref, gw_ref):
        act = _silu(c_ref[...]).astype(BF16)
        gw_ref[0] = lax.dot_general(act, d_ref[0].astype(BF16), _DIMS["tn"], preferred_element_type=F32)

    return pl.pallas_call(
        kern, name="ada_bwd", grid=(L,),
        in_specs=[pl.BlockSpec((8, D), lambda l: (0, 0)), pl.BlockSpec((1, 8, N), lambda l: (l, 0, 0))],
        out_specs=pl.BlockSpec((1, D, N), lambda l: (l, 0, 0)),
        out_shape=jax.ShapeDtypeStruct((L, D, N), F32), compiler_params=_params(("parallel",)),
    )(c8, dmod)


def _sum_examples(d):
    L, _, N = d.shape

    def kern(d_ref, o_ref):
        o_ref[...] = jnp.sum(d_ref[...], axis=1, keepdims=True)

    vm = pl.BlockSpec(memory_space=pltpu.VMEM)
    return pl.pallas_call(kern, name="sum_examples", in_specs=[vm], out_specs=vm,
                          out_shape=jax.ShapeDtypeStruct((L, 1, N), F32))(d)


def _adamw(w, ga, gb, m, v, name):
    rows, cols = w.shape
    tr = _row_div(rows, 256)
    two = gb is not None

    def kern(*refs):
        if two:
            w_ref, ga_ref, gb_ref, m_ref, v_ref, g_out, d_out, m_out, v_out = refs
            gv = ga_ref[...] + gb_ref[...]
        else:
            w_ref, ga_ref, m_ref, v_ref, g_out, d_out, m_out, v_out = refs
            gv = ga_ref[...]
        mn = ADAM_B1 * m_ref[...] + (1.0 - ADAM_B1) * gv
        vn = ADAM_B2 * v_ref[...] + (1.0 - ADAM_B2) * jnp.square(gv)
        m_hat = mn / (1.0 - ADAM_B1 ** ADAM_STEP)
        v_hat = vn / (1.0 - ADAM_B2 ** ADAM_STEP)
        g_out[...] = gv
        d_out[...] = -ADAM_LR * (m_hat / (jnp.sqrt(v_hat) + ADAM_EPS) + ADAM_WD * w_ref[...])
        m_out[...] = mn
        v_out[...] = vn

    spec = pl.BlockSpec((tr, cols), lambda i: (i, 0))
    args = [w, ga, gb, m, v] if two else [w, ga, m, v]
    return pl.pallas_call(
        kern, name=name, grid=(rows // tr,), in_specs=[spec] * len(args), out_specs=[spec] * 4,
        out_shape=[jax.ShapeDtypeStruct((rows, cols), F32)] * 4, compiler_params=_params(("parallel",)),
    )(*args)


def _sum_slots(x, name):
    n, rows, cols = x.shape
    tr = _row_div(rows, 256)

    def kern(x_ref, o_ref):
        acc = x_ref[0].astype(F32)
        for j in range(1, n):
            acc = acc + x_ref[j].astype(F32)
        o_ref[...] = acc

    return pl.pallas_call(
        kern, name=name, grid=(rows // tr,), in_specs=[pl.BlockSpec((n, tr, cols), lambda i: (0, i, 0))],
        out_specs=pl.BlockSpec((tr, cols), lambda i: (i, 0)), out_shape=jax.ShapeDtypeStruct((rows, cols), F32),
        compiler_params=_params(("parallel",)),
    )(x)


def _all_reduce_small(blob, name, swaps=()):
    R, C = blob.shape
    n = len(swaps)

    def kern(*refs):
        src, xs, out, xd = refs[0], refs[1:1 + n], refs[1 + n], refs[2 + n:2 + 2 * n]
        pair, chips, send_sems, recv_sems = refs[2 + 2 * n:6 + 2 * n]
        swap_sems = refs[6 + 2 * n:]
        if n:
            _xchg_start("swap", xs, xd, swap_sems)
        x, y, c = lax.axis_index("x"), lax.axis_index("y"), lax.axis_index("c")
        me = 2 * x + y
        to_sibling = pltpu.make_async_remote_copy(
            src_ref=src, dst_ref=pair, send_sem=send_sems.at[0], recv_sem=recv_sems.at[0],
            device_id=(x, y, 1 - c), device_id_type=_MESH)
        to_sibling.start()
        to_sibling.wait()
        chips[me] = src[...] + pair[...]
        peers = [(1 - x, y), (x, 1 - y), (1 - x, 1 - y)]
        copies = [pltpu.make_async_remote_copy(
            src_ref=chips.at[me], dst_ref=chips.at[me], send_sem=send_sems.at[1 + kk], recv_sem=recv_sems.at[1 + kk],
            device_id=(px, py, c), device_id_type=_MESH) for kk, (px, py) in enumerate(peers)]
        for cp in copies:
            cp.start()
        for kk, (px, py) in enumerate(peers):
            pltpu.make_async_remote_copy(
                src_ref=chips.at[me], dst_ref=chips.at[2 * px + py], send_sem=send_sems.at[1 + kk],
                recv_sem=recv_sems.at[1 + kk], device_id=(px, py, c), device_id_type=_MESH).wait_recv()
        for cp in copies:
            cp.wait_send()
        acc = chips[0]
        for j in range(1, N_CHIPS):
            acc = acc + chips[j]
        out[...] = acc
        if n:
            _xchg_wait("swap", xs, xd, swap_sems)

    vm = pl.BlockSpec(memory_space=pltpu.VMEM)
    swaps = list(swaps)
    res = pl.pallas_call(
        kern, name=name, in_specs=[vm] + [_ANY] * n, out_specs=[vm] + [_ANY] * n,
        out_shape=[jax.ShapeDtypeStruct((R, C), F32)] + _xchg_out_shapes("swap", swaps),
        scratch_shapes=[pltpu.VMEM((R, C), F32), pltpu.VMEM((N_CHIPS, R, C), F32), pltpu.SemaphoreType.DMA((4,)),
                        pltpu.SemaphoreType.DMA((4,))] + (_xchg_scratch("swap", n) if n else []),
        compiler_params=pltpu.CompilerParams(vmem_limit_bytes=VMEM_LIMIT),
    )(blob, *swaps)
    return res[0], list(res[1:])


def _row_tile(S, wide=False):
    for t in ((256,) if wide else (512, 256)):
        if S % t == 0:
            return t
    return 128


def _attn_tiles(S):
    return min(S, 512), min(S, 256), min(S, 256)


def _hosted(fn, host, got, kind, name, *args, **kw):
    if host is not None and host.get(name):
        res, xs = fn(*args, name=name, xchg=(kind, host[name]), **kw)
        got[name] = xs
        return res
    return fn(*args, name=name, **kw)


def _mm(host, got, kind, a, b, mode, out_dtype, name, **kw):
    return _hosted(_matmul, host, got, kind, name, a, b, mode, out_dtype, **kw)


def _merge_late(w, late, got, name):
    if not late or name not in late:
        return w
    return {**w, **_large_operands(dict(zip(late[name], got[name])))}


def _sgu_rows(S):
    return 4 * C_CHUNK if S % (4 * C_CHUNK) == 0 else C_CHUNK


def _layer_fwd(xin, prev, mods, w, tabs, B, S, host=None, late=None):
    T = B * S
    tr = _row_tile(S)
    sh1, sc1, g1, sh2, sc2, g2 = mods
    ca, sa, cb, sb = tabs
    sv = {}
    got = {}
    if prev is None:
        def body(rv, ev, pv):
            return [_f_norm_mod(rv[0], pv[0], ev[0], ev[1])], [], []
        (h,), _, _ = _rowcall(body, name="f_norm_mod", T=T, S=S, tr=tr, rows=[(xin, D_MODEL, 0)], exs=[sc1, sh1],
                              pars=[w["norm1_w"]], row_outs=[(D_MODEL, BF16, D_MODEL, 0)])
        x = xin
    else:
        x1p, dp, g2p = prev

        def body(rv, ev, pv):
            xn, hh = _f_resid_norm_mod(rv[0], rv[1], ev[0], pv[0], ev[1], ev[2])
            return [xn, hh], [], []
        (x, h), _, _ = _rowcall(body, name="f_resid_norm_mod1", T=T, S=S, tr=tr,
                                rows=[(x1p, D_MODEL, 0), (dp, D_MODEL, 0)], exs=[g2p, sc1, sh1], pars=[w["norm1_w"]],
                                row_outs=[(D_MODEL, F32, D_MODEL, 0), (D_MODEL, BF16, D_MODEL, 0)])
    sv["x"], sv["h"] = x, h
    proj = _mm(host, got, "gather", h, w["w_in"], "nn", BF16, "mm_in")
    sv["proj"] = proj
    w = _merge_late(w, late, got, "mm_in")

    def body(rv, ev, pv):
        outs = _f_mixprep(rv[0], rv[1], rv[2], rv[3], rv[4], pv[0], pv[1])
        return list(outs), [], []
    (qa, ka, va, cqn, ckvn, kr), _, _ = _rowcall(
        body, name="f_mixprep", T=T, S=S, tr=tr,
        rows=[(proj, P_MIX, 0), (ca, LANES, 0), (sa, LANES, 0), (cb, LANES, 0), (sb, LANES, 0)],
        pars=[w["b_q_norm_w"], w["b_kv_norm_w"]],
        row_outs=[(768, BF16, 768, 0), (256, BF16, 256, 0), (256, BF16, 256, 0), (384, BF16, 384, 0),
                  (256, BF16, 256, 0), (LANES, F32, LANES, 0)])
    sv.update(qa=qa, ka=ka, va=va, cqn=cqn, ckvn=ckvn)
    q = _matmul(cqn, w["b_w_uq"], "nn", BF16, "mm_uq")
    kv = _matmul(ckvn, w["b_w_ukv"], "nn", BF16, "mm_ukv")

    def body(rv, ev, pv):
        Q, K, V = _f_mlaprep(rv[0], rv[1], rv[2], rv[3], rv[4])
        return [Q, K, V], [], []
    (Q, K, V), _, _ = _rowcall(
        body, name="f_mlaprep", T=T, S=S, tr=tr,
        rows=[(q, 768, 0), (kv, 1536, 0), (kr, LANES, 0), (cb, LANES, 0), (sb, LANES, 0)],
        row_outs=[(768, BF16, 768, 0)] * 3)
    sv.update(Q=Q, K=K, V=V)
    ta, tb, band = _attn_tiles(S)
    oa, lse_a = _hosted(_attn_fwd, host, got, "gather", "attn_a_fwd", qa, ka, va, w["sinks"], B=B, S=S, HQ=A_Q_HEADS,
                        HK=A_KV_HEADS, window=A_WINDOW, scale=HEAD_DIM ** -0.5, tq=ta, band=None)
    ob, lse_b = _hosted(_attn_fwd, host, got, "gather", "attn_b_fwd", Q, K, V, None, B=B, S=S, HQ=B_HEADS, HK=B_HEADS,
                        window=None, scale=(B_NOPE + B_ROPE) ** -0.5, tq=tb, band=band)
    sv.update(oa=oa, lse_a=lse_a, ob=ob, lse_b=lse_b)
    w = _merge_late(w, late, got, "attn_a_fwd")
    ts = _sgu_rows(S)

    def body(rv, ev, pv):
        outs = [_f_sgu(rv[0][r:r + C_CHUNK], rv[1][r:r + C_CHUNK], pv[0], pv[1], pv[2], pv[3])
                for r in range(0, ts, C_CHUNK)]
        return [jnp.concatenate(outs, axis=0)], [], []
    (yc,), _, _ = _rowcall(body, name="f_sgu", T=T, S=S, tr=ts,
                           rows=[(proj, C_WIDTH, P_CU // C_WIDTH), (proj, C_WIDTH, P_CV // C_WIDTH)],
                           pars=[w["c_ln_w"], w["c_ln_b"], w["c_w_s"], w["c_b_col"]],
                           row_outs=[(C_WIDTH, F32, C_WIDTH, 0)])
    sv["yc"] = yc

    def body(rv, ev, pv):
        return [_f_outnorm(rv[0], rv[1], rv[2], pv[0])], [], []
    (y,), _, _ = _rowcall(body, name="f_outnorm", T=T, S=S, tr=tr,
                          rows=[(oa, 768, 0), (ob, 768, 0), (yc, C_WIDTH, 0)], pars=[w["out_norm_w"]],
                          row_outs=[(D_MODEL, BF16, D_MODEL, 0)])
    sv["y"] = y
    o = _mm(host, got, "gather", y, w["w_out"], "nn", BF16, "mm_out")
    sv["o"] = o

    def body(rv, ev, pv):
        xn, hh = _f_resid_norm_mod(rv[0], rv[1], ev[0], pv[0], ev[1], ev[2])
        return [xn, hh], [], []
    (x1, h2), _, _ = _rowcall(body, name="f_resid_norm_mod2", T=T, S=S, tr=tr,
                              rows=[(x, D_MODEL, 0), (o, D_MODEL, 0)], exs=[g1, sc2, sh2], pars=[w["norm2_w"]],
                              row_outs=[(D_MODEL, F32, D_MODEL, 0), (D_MODEL, BF16, D_MODEL, 0)])
    sv["h2"] = h2
    gu = _mm(host, got, "gather", h2, w["w_gate_up"], "nn", BF16, "mm_gate_up", b_chunks=True)
    sv["gu"] = gu

    def body(rv, ev, pv):
        return [_f_swiglu(rv[0], rv[1])], [], []
    (act,), _, _ = _rowcall(body, name="f_swiglu", T=T, S=S, tr=_row_tile(S, wide=True),
                            rows=[(gu, FFN_HIDDEN, 0), (gu, FFN_HIDDEN, 1)], row_outs=[(FFN_HIDDEN, BF16, FFN_HIDDEN, 0)])
    sv["act"] = act
    d = _mm(host, got, "gather", act, w["w_down"], "nn", BF16, "mm_down")
    sv["x1"], sv["d"], sv["w"] = x1, d, w
    return (x1, d, g2), sv, got


def _final(x1, d, g2, fw, target, B, S):
    T = B * S
    tr = _row_tile(S)

    def loss_fn(x1v, dv, gv, wv, tv):
        yf = _rms(x1v + gv * dv, wv)
        return 0.5 * jnp.sum(jnp.mean(jnp.square(yf - tv), axis=-1))

    def body(rv, ev, pv):
        x1v, dv, tv = rv
        val, vjp = jax.vjp(lambda a, b_, g, ww: loss_fn(a, b_, g, ww, tv), x1v, dv, ev[0], pv[0])
        dx1, dd, dg, dw = vjp(jnp.ones((), F32))
        return [dx1, dd], [dg], [dw, jnp.full((1, LANES), val, F32)]
    (dx1, dd), (dg2,), (dfw, loss) = _rowcall(
        body, name="final_loss", T=T, S=S, tr=tr, rows=[(x1, D_MODEL, 0), (d, D_MODEL, 0), (target, D_MODEL, 0)],
        exs=[g2], pars=[fw], row_outs=[(D_MODEL, F32, D_MODEL, 0), (D_MODEL, BF16, D_MODEL, 0)],
        ex_outs=[D_MODEL], par_outs=[(1, D_MODEL), (1, LANES)])
    return loss, dx1, dd, dg2, dfw


def _layer_bwd(sv, prev, mods, tabs, dx1, dd, B, S, host=None, own_early=False):
    w = sv["w"]
    T = B * S
    tr = _row_tile(S)
    sh1, sc1, g1, sh2, sc2, g2 = mods
    ca, sa, cb, sb = tabs
    gr = {}
    got = {}
    dact = _mm(host, got, "a2a", dd, w["w_down"], "nt", BF16, "mm_down_dx")
    gr["w_down"] = _matmul(sv["act"], dd, "tn", BF16, "mm_down_dw")

    def body(rv, ev, pv):
        _, vjp = jax.vjp(_f_swiglu, rv[0], rv[1])
        dgate, dup = vjp(rv[2])
        return [[dgate, dup]], [], []
    (dgu,), _, _ = _rowcall(body, name="b_swiglu", T=T, S=S, tr=_row_tile(S, wide=True),
                            rows=[(sv["gu"], FFN_HIDDEN, 0), (sv["gu"], FFN_HIDDEN, 1), (dact, FFN_HIDDEN, 0)],
                            row_outs=[(2 * FFN_HIDDEN, BF16, 2 * FFN_HIDDEN, 0)])
    dh2 = _mm(host, got, "a2a", dgu, w["w_gate_up"], "nt", BF16, "mm_gate_up_dx", b_chunks=True)
    gr["w_gate_up"] = _mm(host, got, "a2a", sv["h2"], dgu, "tn", BF16, "mm_gate_up_dw", out_chunks=True)

    def body(rv, ev, pv):
        xa, delta, dh, dxn = rv
        _, vjp = jax.vjp(_f_resid_norm_mod, xa, delta, ev[0], pv[0], ev[1], ev[2])
        dxa, ddelta, dg, dw, dsc, dsh = vjp((dxn, dh))
        return [dxa, ddelta], [dg, dsc, dsh], [dw]
    (dx, do), (dg1, dsc2, dsh2), (gr["norm2_w"],) = _rowcall(
        body, name="b_resid_norm_mod2", T=T, S=S, tr=tr,
        rows=[(sv["x"], D_MODEL, 0), (sv["o"], D_MODEL, 0), (dh2, D_MODEL, 0), (dx1, D_MODEL, 0)],
        exs=[g1, sc2, sh2], pars=[w["norm2_w"]],
        row_outs=[(D_MODEL, F32, D_MODEL, 0), (D_MODEL, BF16, D_MODEL, 0)], ex_outs=[D_MODEL] * 3,
        par_outs=[(1, D_MODEL)])
    dy = _matmul(do, w["w_out"], "nt", BF16, "mm_out_dx")
    gr["w_out"] = _matmul(sv["y"], do, "tn", BF16, "mm_out_dw")

    def body(rv, ev, pv):
        _, vjp = jax.vjp(_f_outnorm, rv[0], rv[1], rv[2], pv[0])
        doa, dob, dyc, dgw = vjp(rv[3])
        return [doa, dob, dyc], [], [dgw]
    (doa, dob, dyc), _, (gr["out_norm_w"],) = _rowcall(
        body, name="b_outnorm", T=T, S=S, tr=tr,
        rows=[(sv["oa"], 768, 0), (sv["ob"], 768, 0), (sv["yc"], C_WIDTH, 0), (dy, D_MODEL, 0)], pars=[w["out_norm_w"]],
        row_outs=[(768, BF16, 768, 0), (768, BF16, 768, 0), (C_WIDTH, F32, C_WIDTH, 0)], par_outs=[(1, D_MODEL)])

    ta, tb, band = _attn_tiles(S)
    if own_early:
        host = dict(host)
        host["attn_b_bwd"] = list(host.get("attn_b_bwd", ())) + [
            gr["w_gate_up"], gr["w_down"].reshape(N_CHIPS, FFN_HIDDEN // N_CHIPS, D_MODEL)]
        host["attn_a_bwd"] = list(host.get("attn_a_bwd", ())) + [
            gr["w_out"].reshape(N_CHIPS, D_MODEL // N_CHIPS, D_MODEL)]
    dQ, dK, dV, _ = _hosted(_attn_bwd, host, got, "a2a", "attn_b_bwd", sv["Q"], sv["K"], sv["V"], sv["ob"], sv["lse_b"],
                            dob, None, B=B, S=S, HQ=B_HEADS, HK=B_HEADS, window=None,
                            scale=(B_NOPE + B_ROPE) ** -0.5, tq=tb, band=band)
    dqa, dka, dva, dsink = _hosted(_attn_bwd, host, got, "a2a", "attn_a_bwd", sv["qa"], sv["ka"], sv["va"], sv["oa"],
                                   sv["lse_a"], doa, w["sinks"], B=B, S=S, HQ=A_Q_HEADS, HK=A_KV_HEADS,
                                   window=A_WINDOW, scale=HEAD_DIM ** -0.5, tq=ta, band=None)
    gr["sinks"] = dsink

    def body(rv, ev, pv):
        dQv, dKv, dVv, cbv, sbv = rv
        dq = [_rope_bwd((cbv, sbv), p)[0] for p in _heads(dQv, B_HEADS)]
        dkr = None
        for p in _heads(dKv, B_HEADS):
            dkr = p if dkr is None else dkr + p
        return [dq, [dKv, dVv], dkr], [], []
    (dq, dkv, dkr), _, _ = _rowcall(
        body, name="b_mlaprep", T=T, S=S, tr=tr,
        rows=[(dQ, 768, 0), (dK, 768, 0), (dV, 768, 0), (cb, LANES, 0), (sb, LANES, 0)],
        row_outs=[(768, BF16, 768, 0), (1536, BF16, 1536, 0), (LANES, F32, LANES, 0)])
    dcqn = _matmul(dq, w["b_w_uq"], "nt", BF16, "mm_uq_dx")
    gr["b_w_uq"] = _matmul(sv["cqn"], dq, "tn", BF16, "mm_uq_dw")
    dckvn = _matmul(dkv, w["b_w_ukv"], "nt", BF16, "mm_ukv_dx")
    gr["b_w_ukv"] = _matmul(sv["ckvn"], dkv, "tn", BF16, "mm_ukv_dw")

    def body(rv, ev, pv):
        proj, cav, sav, cbv, sbv, dqa_, dka_, dva_, dcqn_, dckvn_, dkr_ = rv
        _, vjp = jax.vjp(lambda p, a, b_: _f_mixprep(p, cav, sav, cbv, sbv, a, b_), proj, pv[0], pv[1])
        dproj, dqn, dkvn = vjp((dqa_, dka_, dva_, dcqn_, dckvn_, dkr_))
        return [[dproj, jnp.zeros((dproj.shape[0], P_CU - P_MIX), F32)]], [], [dqn, dkvn]
    (dproj,), _, (gr["b_q_norm_w"], gr["b_kv_norm_w"]) = _rowcall(
        body, name="b_mixprep", T=T, S=S, tr=tr,
        rows=[(sv["proj"], P_MIX, 0), (ca, LANES, 0), (sa, LANES, 0), (cb, LANES, 0), (sb, LANES, 0),
              (dqa, 768, 0), (dka, 256, 0), (dva, 256, 0), (dcqn, 384, 0), (dckvn, 256, 0), (dkr, LANES, 0)],
        pars=[w["b_q_norm_w"], w["b_kv_norm_w"]], row_outs=[(P_END, BF16, P_CU, 0)],
        par_outs=[(1, B_Q_RANK), (1, B_KV_RANK)])

    ts = _sgu_rows(S)

    def body(rv, ev, pv):
        cu, cv, dycv, _ = rv
        dcus, dcvs, acc = [], [], None
        for r in range(0, ts, C_CHUNK):
            _, vjp = jax.vjp(_f_sgu, cu[r:r + C_CHUNK], cv[r:r + C_CHUNK], pv[0], pv[1], pv[2], pv[3])
            dcu, dcv, *dpar = vjp(dycv[r:r + C_CHUNK])
            dcus.append(dcu)
            dcvs.append(dcv)
            acc = dpar if acc is None else [a + b_ for a, b_ in zip(acc, dpar)]
        return [[jnp.concatenate(dcus, axis=0), jnp.concatenate(dcvs, axis=0)]], [], acc
    (dproj,), _, (gr["c_ln_w"], gr["c_ln_b"], gr["c_w_s"], gr["c_b_col"]) = _rowcall(
        body, name="b_sgu", T=T, S=S, tr=ts,
        rows=[(sv["proj"], C_WIDTH, P_CU // C_WIDTH), (sv["proj"], C_WIDTH, P_CV // C_WIDTH), (dyc, C_WIDTH, 0),
              (dproj, 2 * C_WIDTH, P_CU // (2 * C_WIDTH))],
        pars=[w["c_ln_w"], w["c_ln_b"], w["c_w_s"], w["c_b_col"]],
        row_outs=[(P_END, BF16, 2 * C_WIDTH, P_CU // (2 * C_WIDTH))],
        par_outs=[(1, C_WIDTH), (1, C_WIDTH), (C_GROUPS, C_CHUNK, C_CHUNK), (C_GROUPS, C_CHUNK, 1)],
        aliases={3: 0})
    dh = _mm(host, got, "a2a", dproj, w["w_in"], "nt", BF16, "mm_in_dx")
    gr["w_in"] = _matmul(sv["h"], dproj, "tn", BF16, "mm_in_dw")

    if prev is None:
        def body(rv, ev, pv):
            xv, dhv, dxd = rv
            _, vjp = jax.vjp(_f_norm_mod, xv, pv[0], ev[0], ev[1])
            dxa, dw, dsc, dsh = vjp(dhv)
            return [dxa + dxd], [dsc, dsh], [dw]
        (dxin,), (dsc1, dsh1), (gr["norm1_w"],) = _rowcall(
            body, name="b_norm_mod", T=T, S=S, tr=tr, rows=[(sv["x"], D_MODEL, 0), (dh, D_MODEL, 0), (dx, D_MODEL, 0)],
            exs=[sc1, sh1], pars=[w["norm1_w"]], row_outs=[(D_MODEL, F32, D_MODEL, 0)], ex_outs=[D_MODEL] * 2,
            par_outs=[(1, D_MODEL)])
        nxt = (dxin, None, None)
    else:
        x1p, dp, g2p = prev

        def body(rv, ev, pv):
            xa, delta, dhv, dxn = rv
            _, vjp = jax.vjp(_f_resid_norm_mod, xa, delta, ev[0], pv[0], ev[1], ev[2])
            dxa, ddelta, dg, dw, dsc, dsh = vjp((dxn, dhv))
            return [dxa, ddelta], [dg, dsc, dsh], [dw]
        (dx1p, ddp), (dg2p, dsc1, dsh1), (gr["norm1_w"],) = _rowcall(
            body, name="b_resid_norm_mod1", T=T, S=S, tr=tr,
            rows=[(x1p, D_MODEL, 0), (dp, D_MODEL, 0), (dh, D_MODEL, 0), (dx, D_MODEL, 0)],
            exs=[g2p, sc1, sh1], pars=[w["norm1_w"]],
            row_outs=[(D_MODEL, F32, D_MODEL, 0), (D_MODEL, BF16, D_MODEL, 0)], ex_outs=[D_MODEL] * 3,
            par_outs=[(1, D_MODEL)])
        nxt = (dx1p, ddp, dg2p)
    return gr, (dsh1, dsc1, dg1, dsh2, dsc2), nxt, got


def _lane_table(lanes_neg, lanes_pos, inv):
    freq = np.zeros((LANES,), np.int64) - 1
    sign = np.zeros((1, LANES), np.float32)
    n = len(lanes_neg)
    freq[lanes_neg] = np.arange(n)
    freq[lanes_pos] = np.arange(n)
    sign[0, lanes_neg] = -1.0
    sign[0, lanes_pos] = 1.0
    return _select_axis(inv, freq, 0).reshape(1, LANES), jnp.asarray(sign)


SHARDED = ("ada_w", "w_in", "b_w_uq", "b_w_ukv", "w_out", "w_gate_up", "w_down")
ROW_SHARDED = ("w_out", "w_down")
SMALL = ("ada_b", "norm1_w", "a_sinks", "b_q_norm_w", "b_kv_norm_w", "c_ln_w", "c_ln_b", "c_w_s", "c_b_s",
         "out_norm_w", "norm2_w", "final_norm_w")
FWD_HOST = {"attn_b_fwd": ("w_gate_up", "w_down"), "attn_a_fwd": ("w_in", "w_out", "b_w_uq", "b_w_ukv")}
FWD_HOST_FIRST = {"attn_b_fwd": ("w_gate_up", "w_down"), "mm_gate_up": ("w_in", "w_out", "b_w_uq", "b_w_ukv")}
BWD_HOST = {"attn_a_bwd": ("w_in", "w_out", "b_w_uq", "b_w_ukv"), "attn_b_bwd": ("w_gate_up", "w_down")}
FIRST_LATE = {"mm_in": ("b_w_uq", "b_w_ukv", "w_out"), "attn_a_fwd": ("w_gate_up", "w_down")}
EXPOSED = ("w_in", "b_w_uq", "b_w_ukv")
OWN_LAYER = ("w_gate_up", "w_down")
EXCHANGED = SHARDED[1:]
N_ADA = N_MOD * D_MODEL // N_CHIPS
assert not set(FIRST_LATE) & set(FWD_HOST_FIRST)


def _from_host(table, got):
    return {k: got[name][i] for name, ks in table.items() for i, k in enumerate(ks)}


def _to_host(table, arrays):
    return {name: [arrays[k] for k in ks] for name, ks in table.items()}


def _join_cols(g):
    return jnp.concatenate([g[j] for j in range(N_CHIPS)], axis=1)


def _split_cols(g):
    n = g.shape[1] // N_CHIPS
    return jnp.stack([g[:, j * n:(j + 1) * n] for j in range(N_CHIPS)])


def _layer_weights(G, small, l):
    D = D_MODEL
    return {
        **_large_operands(G),
        "norm1_w": small["norm1_w"][l].reshape(1, D),
        "sinks": jnp.broadcast_to(small["a_sinks"][l].reshape(A_Q_HEADS, 1, 1), (A_Q_HEADS, 1, LANES)),
        "b_q_norm_w": small["b_q_norm_w"][l].reshape(1, B_Q_RANK),
        "b_kv_norm_w": small["b_kv_norm_w"][l].reshape(1, B_KV_RANK),
        "c_ln_w": small["c_ln_w"][l].reshape(1, C_WIDTH), "c_ln_b": small["c_ln_b"][l].reshape(1, C_WIDTH),
        "c_w_s": small["c_w_s"][l], "c_b_col": small["c_b_s"][l].reshape(C_GROUPS, C_CHUNK, 1),
        "out_norm_w": small["out_norm_w"][l].reshape(1, D), "norm2_w": small["norm2_w"][l].reshape(1, D),
    }


def _large_operands(G):
    D = D_MODEL
    make = {
        "w_in": lambda g: _pad_axis(_join_cols(g), _map_w_in(), 1),
        "b_w_uq": lambda g: _pad_axis(_join_cols(g), _map_w_uq(), 1),
        "b_w_ukv": lambda g: _pad_axis(_join_cols(g), _map_w_ukv(), 1),
        "w_out": lambda g: g.reshape(D, D), "w_gate_up": lambda g: g, "w_down": lambda g: g.reshape(FFN_HIDDEN, D),
    }
    return {k: make[k](g) for k, g in G.items()}


def _send_buffers(gr):
    D = D_MODEL
    return {
        "w_gate_up": gr["w_gate_up"], "w_down": gr["w_down"].reshape(N_CHIPS, FFN_HIDDEN // N_CHIPS, D),
        "w_out": gr["w_out"].reshape(N_CHIPS, D // N_CHIPS, D),
        "w_in": _split_cols(_unpad_axis(gr["w_in"], _map_w_in(), IN_COLS, 1)),
        "b_w_uq": _split_cols(_unpad_axis(gr["b_w_uq"], _map_w_uq(), B_HEADS * (B_NOPE + B_ROPE), 1)),
        "b_w_ukv": _split_cols(_unpad_axis(gr["b_w_ukv"], _map_w_ukv(), B_HEADS * (B_NOPE + B_V), 1)),
    }


def _small_grads(gr, B):
    D = D_MODEL
    return {
        "norm1_w": gr["norm1_w"].reshape(D),
        "a_sinks": gr["sinks"][:, 0, 0].reshape(B, A_Q_HEADS).sum(axis=0),
        "b_q_norm_w": gr["b_q_norm_w"].reshape(B_Q_RANK), "b_kv_norm_w": gr["b_kv_norm_w"].reshape(B_KV_RANK),
        "c_ln_w": gr["c_ln_w"].reshape(C_WIDTH), "c_ln_b": gr["c_ln_b"].reshape(C_WIDTH), "c_w_s": gr["c_w_s"],
        "c_b_s": gr["c_b_col"].reshape(C_GROUPS, C_CHUNK),
        "out_norm_w": gr["out_norm_w"].reshape(D), "norm2_w": gr["norm2_w"].reshape(D),
    }


def _step(x, c, positions, target, small, shard_of, ada=None, gathered=None):
    dist = gathered is None
    B, S, D = x.shape
    T = B * S
    xt = x.reshape(T, D)
    tgt = target.reshape(T, D)
    pos_col = positions.astype(F32).reshape(T, 1)
    inv_a = 1.0 / (ROPE_THETA ** (jnp.arange(0, HEAD_DIM, 2, dtype=F32) / HEAD_DIM))
    inv_b = 1.0 / (ROPE_THETA ** (jnp.arange(0, B_ROPE, 2, dtype=F32) / B_ROPE))
    fa, sga = _lane_table(np.arange(32), 64 + np.arange(32), inv_a)
    fb, sgb = _lane_table(48 + np.arange(16), 112 + np.arange(16), inv_b)
    ca, sa = _rope_tables(pos_col, fa, sga, jnp.abs(sga), "rope_a")
    cb, sb = _rope_tables(pos_col, fb, sgb, jnp.ones_like(sgb), "rope_b")
    tabs = (ca, sa, cb, sb)
    ada_b = small["ada_b"]
    if dist:
        assert N_CHIPS * B == 8
        me = 2 * lax.axis_index("x") + lax.axis_index("y")
        first = shard_of(0)
        c_all, w_in_first = _xchg_call("gather", [c, first["w_in"]], "gather_first")
        c8 = c_all.reshape(N_CHIPS * B, D)
        mine = lax.dynamic_slice_in_dim(ada_b, me * N_ADA, N_ADA, axis=1).reshape(DEPTH, 1, N_ADA)
        part = _ada_fwd(c8, ada, mine)
        part = part.reshape(DEPTH, N_CHIPS, B, N_ADA).transpose(1, 0, 2, 3).reshape(N_CHIPS, DEPTH * B, N_ADA)
        (back,) = _xchg_call("a2a", [part], "mod_exchange")
        mod_all = jnp.concatenate([back[j].reshape(DEPTH, B, N_ADA) for j in range(N_CHIPS)], axis=-1)
        G = {"w_in": w_in_first}
    else:
        c8 = jnp.zeros((8, D), F32).at[:B].set(c)
        cols = [(jnp.stack([gathered[l]["ada_w"][j] for l in range(DEPTH)]),
                 ada_b[:, j * N_ADA:(j + 1) * N_ADA].reshape(DEPTH, 1, N_ADA)) for j in range(N_CHIPS)]
        mod_all = jnp.concatenate([_ada_fwd(c8, wj, bj)[:, :B] for wj, bj in cols], axis=-1)
        G = {k: v for k, v in gathered[0].items() if k != "ada_w"}
    saved, prevs, modss = [], [], []
    prev = None
    for l in range(DEPTH):
        w = _layer_weights(G, small, l)
        mods = tuple(mod_all[l, :, i * D:(i + 1) * D].reshape(B, 1, D) for i in range(N_MOD))
        more = l + 1 < DEPTH
        table = FWD_HOST_FIRST if l == 0 else FWD_HOST
        host = _to_host(table, shard_of(l + 1)) if dist and more else {}
        late = None
        if dist and l == 0:
            late = FIRST_LATE
            for name, ks in late.items():
                host[name] = [first[k] for k in ks] + host.get(name, [])
        prevs.append(prev)
        modss.append(mods)
        prev, sv, got = _layer_fwd(xt, prev, mods, w, tabs, B, S, host, late)
        saved.append(sv)
        if more:
            G = _from_host(table, got) if dist else {k: v for k, v in gathered[l + 1].items() if k != "ada_w"}
    x1, d, g2 = prev
    loss, dx1, dd, dg2, dfw = _final(x1, d, g2, small["final_norm_w"].reshape(1, D), tgt, B, S)

    landed = [None] * DEPTH
    smalls = [None] * DEPTH
    dmods = [None] * DEPTH
    pending = None
    for l in reversed(range(DEPTH)):
        host = _to_host(BWD_HOST, pending) if dist and pending is not None else None
        early = dist and l == 0 and host is not None
        gr, (dsh1, dsc1, dg1, dsh2, dsc2), nxt, got = _layer_bwd(saved[l], prevs[l], modss[l], tabs, dx1, dd, B, S, host, early)
        if pending is not None:
            landed[l + 1] = _from_host(BWD_HOST, got) if dist else pending
        dmods[l] = jnp.concatenate([dsh1, dsc1, dg1, dsh2, dsc2, dg2], axis=-1).reshape(B, N_MOD * D)
        pending = _send_buffers(gr)
        smalls[l] = _small_grads(gr, B)
        dx1, dd, dg2 = nxt
    dmod_all = jnp.stack(dmods)
    by_chip = [dmod_all[:, :, j * N_ADA:(j + 1) * N_ADA] for j in range(N_CHIPS)]
    if dist:
        last = EXPOSED if early else EXCHANGED
        send = jnp.stack([p.reshape(DEPTH * B, N_ADA) for p in by_chip])
        *res, back = _xchg_call("a2a", [pending[k] for k in last] + [send], "grad_exchange_last")
        landed[0] = dict(zip(last, res))
        if early:
            landed[0].update(zip(OWN_LAYER, got["attn_b_bwd"][-len(OWN_LAYER):]))
            landed[0]["w_out"] = got["attn_a_bwd"][-1]
        dmod8 = back.reshape(N_CHIPS, DEPTH, B, N_ADA).transpose(1, 0, 2, 3).reshape(DEPTH, N_CHIPS * B, N_ADA)
        ada_g = _ada_bwd(c8, dmod8)
    else:
        landed[0] = pending
        ada_g = jnp.stack([_ada_bwd(c8, jnp.zeros((DEPTH, 8, N_ADA), F32).at[:, :B].set(p)) for p in by_chip])
    small_g = {k: jnp.stack([smalls[l][k] for l in range(DEPTH)]) for k in SMALL if k not in ("final_norm_w", "ada_b")}
    small_g["ada_b"] = _sum_examples(dmod_all).reshape(DEPTH, N_MOD * D)
    small_g["final_norm_w"] = dfw.reshape(D)
    return loss[0, 0], dx1.reshape(B, S, D), landed, small_g, ada_g


def _pack(arrs, cols, mult):
    flat = jnp.concatenate([a.reshape(-1) for a in arrs])
    n = flat.shape[0]
    rows = -(-n // cols)
    rows = -(-rows // mult) * mult
    return jnp.pad(flat, (0, rows * cols - n)).reshape(rows, cols)


def _unpack(blob, shapes):
    flat = blob.reshape(-1)
    out, off = [], 0
    for s in shapes:
        n = int(np.prod(s))
        out.append(flat[off:off + n].reshape(s))
        off += n
    return out


def kernel(x, c, positions, ada_w, ada_b, norm1_w, w_in, a_sinks, b_q_norm_w, b_w_uq, b_kv_norm_w, b_w_ukv, c_ln_w, c_ln_b, c_w_s, c_b_s, out_norm_w, w_out, norm2_w, w_gate_up, w_down, final_norm_w, loss_target, m_ada_w, m_ada_b, m_norm1_w, m_w_in, m_a_sinks, m_b_q_norm_w, m_b_w_uq, m_b_kv_norm_w, m_b_w_ukv, m_c_ln_w, m_c_ln_b, m_c_w_s, m_c_b_s, m_out_norm_w, m_w_out, m_norm2_w, m_w_gate_up, m_w_down, m_final_norm_w, v_ada_w, v_ada_b, v_norm1_w, v_w_in, v_a_sinks, v_b_q_norm_w, v_b_w_uq, v_b_kv_norm_w, v_b_w_ukv, v_c_ln_w, v_c_ln_b, v_c_w_s, v_c_b_s, v_out_norm_w, v_w_out, v_norm2_w, v_w_gate_up, v_w_down, v_final_norm_w):
    names = ("ada_w", "ada_b", "norm1_w", "w_in", "a_sinks", "b_q_norm_w", "b_w_uq", "b_kv_norm_w", "b_w_ukv", "c_ln_w",
             "c_ln_b", "c_w_s", "c_b_s", "out_norm_w", "w_out", "norm2_w", "w_gate_up", "w_down", "final_norm_w")
    ws = dict(zip(names, (ada_w, ada_b, norm1_w, w_in, a_sinks, b_q_norm_w, b_w_uq, b_kv_norm_w, b_w_ukv, c_ln_w, c_ln_b,
                          c_w_s, c_b_s, out_norm_w, w_out, norm2_w, w_gate_up, w_down, final_norm_w)))
    ms = dict(zip(names, (m_ada_w, m_ada_b, m_norm1_w, m_w_in, m_a_sinks, m_b_q_norm_w, m_b_w_uq, m_b_kv_norm_w, m_b_w_ukv,
                          m_c_ln_w, m_c_ln_b, m_c_w_s, m_c_b_s, m_out_norm_w, m_w_out, m_norm2_w, m_w_gate_up, m_w_down,
                          m_final_norm_w)))
    vs = dict(zip(names, (v_ada_w, v_ada_b, v_norm1_w, v_w_in, v_a_sinks, v_b_q_norm_w, v_b_w_uq, v_b_kv_norm_w, v_b_w_ukv,
                          v_c_ln_w, v_c_ln_b, v_c_w_s, v_c_b_s, v_out_norm_w, v_w_out, v_norm2_w, v_w_gate_up, v_w_down,
                          v_final_norm_w)))
    shards = {k: ws[k].astype(BF16) for k in SHARDED}
    loss_local, grad_x, landed, gsmall, ada_g = _step(
        x, c, positions, loss_target, {k: ws[k] for k in SMALL}, lambda l: {k: shards[k][l] for k in EXCHANGED},
        ada=shards["ada_w"])

    mine = {k: jnp.stack([_sum_slots(landed[l][k], "grad_sum_" + k) for l in range(DEPTH)]) for k in EXCHANGED}
    mine["ada_w"] = ada_g
    small_shapes = [ws[k].shape for k in SMALL]
    sblob = _pack([gsmall[k] for k in SMALL] + [loss_local.reshape(1)], LANES, 8)
    sred, swapped = _all_reduce_small(sblob, "small_all_reduce", swaps=[mine[k] for k in SHARDED])
    theirs = dict(zip(SHARDED, swapped))
    grads, delta, new_m, new_v = {}, {}, {}, {}
    for k in SHARDED:
        shp = ws[k].shape
        two = (shp[0] * shp[1], shp[2])
        g, dlt, nm, nv = _adamw(ws[k].reshape(two), mine[k].reshape(two), theirs[k].reshape(two), ms[k].reshape(two),
                                vs[k].reshape(two), "adamw_" + k)
        grads[k], delta[k], new_m[k], new_v[k] = g.reshape(shp), dlt.reshape(shp), nm.reshape(shp), nv.reshape(shp)

    svals = _unpack(sred, small_shapes + [(1,)])
    loss = svals[-1].reshape(())
    pw = _pack([ws[k] for k in SMALL], LANES, 8)
    pg = _pack(svals[:-1], LANES, 8)
    pm = _pack([ms[k] for k in SMALL], LANES, 8)
    pv = _pack([vs[k] for k in SMALL], LANES, 8)
    g, dlt, nm, nv = _adamw(pw, pg, None, pm, pv, "adamw_small")
    for k, a, b_, c_, d_ in zip(SMALL, _unpack(g, small_shapes), _unpack(dlt, small_shapes), _unpack(nm, small_shapes),
                                _unpack(nv, small_shapes)):
        grads[k], delta[k], new_m[k], new_v[k] = a, b_, c_, d_

    return (loss, grad_x, *[grads[k] for k in names], *[delta[k] for k in names], *[new_m[k] for k in names],
            *[new_v[k] for k in names])
```

```python
import functools
import math

import numpy as np
import jax
import jax.numpy as jnp
from jax import lax
from jax.experimental import pallas as pl
from jax.experimental.pallas import tpu as pltpu

F32 = jnp.float32
BF16 = jnp.bfloat16

D_MODEL = 1024
DEPTH = 4
HEAD_DIM = 64
ROPE_THETA = 10000.0
NORM_EPS = 1e-6
NEG_INF = -1e30
LOG2_E = math.log2(math.e)
A_Q_HEADS = 6
A_KV_HEADS = 2
A_WINDOW = 128
B_HEADS = 6
B_Q_RANK = 384
B_KV_RANK = 256
B_NOPE = 64
B_ROPE = 32
B_V = 64
C_GROUPS = 4
C_GROUP_DIM = 64
C_WIDTH = 256
C_CHUNK = 128
IN_COLS = 1824
FFN_HIDDEN = 2816
N_MOD = 6
ADAM_LR = 0.001
ADAM_B1 = 0.9
ADAM_B2 = 0.999
ADAM_EPS = 1e-08
ADAM_WD = 0.01
ADAM_STEP = 10

LANES = 128
VMEM_LIMIT = 56 * 1024 * 1024
N_CHIPS = 4
WINDOW_SUB = 256

P_KR, P_AQ, P_AK, P_AV, P_CQ, P_CKV, P_MIX, P_CU, P_CV, P_END = 0, 128, 512, 640, 768, 1152, 1408, 1536, 1792, 2048


def _map_w_in():
    idx = -np.ones(P_END, np.int64)
    idx[P_AQ:P_AQ + 384] = np.arange(384)
    idx[P_AK:P_AK + 128] = 384 + np.arange(128)
    idx[P_AV:P_AV + 128] = 512 + np.arange(128)
    idx[P_CQ:P_CQ + 384] = 640 + np.arange(384)
    idx[P_CKV:P_CKV + 256] = 1024 + np.arange(256)
    idx[P_KR + 48 + np.arange(16)] = 1280 + np.arange(16)
    idx[P_KR + 112 + np.arange(16)] = 1296 + np.arange(16)
    idx[P_CU:P_CU + 256] = 1312 + np.arange(256)
    idx[P_CV:P_CV + 256] = 1568 + np.arange(256)
    return idx


def _map_w_uq():
    idx = -np.ones(B_HEADS * LANES, np.int64)
    for h in range(B_HEADS):
        b = h * (B_NOPE + B_ROPE)
        idx[h * LANES + np.arange(48)] = b + np.arange(48)
        idx[h * LANES + 48 + np.arange(16)] = b + 64 + np.arange(16)
        idx[h * LANES + 64 + np.arange(16)] = b + 48 + np.arange(16)
        idx[h * LANES + 112 + np.arange(16)] = b + 80 + np.arange(16)
    return idx


def _map_w_ukv():
    idx = -np.ones(2 * B_HEADS * LANES, np.int64)
    for h in range(B_HEADS):
        b = h * (B_NOPE + B_V)
        idx[h * LANES + np.arange(48)] = b + np.arange(48)
        idx[h * LANES + 64 + np.arange(16)] = b + 48 + np.arange(16)
        idx[B_HEADS * LANES + h * LANES + np.arange(B_V)] = b + B_NOPE + np.arange(B_V)
    return idx


def _inverse(idx, n):
    inv = np.zeros(n, np.int64)
    pos = np.nonzero(idx >= 0)[0]
    inv[idx[pos]] = pos
    return inv


def _runs(idx):
    runs, i, n = [], 0, len(idx)
    while i < n:
        j = i + 1
        while j < n and ((idx[i] < 0 and idx[j] < 0) or (idx[i] >= 0 and idx[j] == idx[i] + (j - i))):
            j += 1
        runs.append((int(idx[i]), j - i))
        i = j
    return runs


def _select_axis(w, idx, axis):
    pieces = []
    for start, length in _runs(idx):
        if start < 0:
            shape = list(w.shape)
            shape[axis] = length
            pieces.append(jnp.zeros(shape, w.dtype))
        else:
            pieces.append(lax.slice_in_dim(w, start, start + length, axis=axis))
    return jnp.concatenate(pieces, axis=axis)


def _pad_axis(w, idx, axis):
    return _select_axis(w, idx, axis)


def _unpad_axis(g, idx, n, axis):
    return _select_axis(g, _inverse(idx, n), axis)


def _params(sem):
    return pltpu.CompilerParams(dimension_semantics=sem, vmem_limit_bytes=VMEM_LIMIT)


def _tile(dim, target):
    if dim <= target:
        return dim
    best = None
    for t in range(LANES, target + 1, LANES):
        if dim % t == 0:
            best = t
    assert best is not None, dim
    return best


def _row_div(rows, target):
    if rows <= target:
        return rows
    best = None
    for t in range(8, target + 1, 8):
        if rows % t == 0:
            best = t
    assert best is not None, rows
    return best


_ANY = pl.BlockSpec(memory_space=pl.ANY)
_MESH = pl.DeviceIdType.MESH


def _xchg_out_shapes(kind, srcs):
    if kind == "gather":
        return [jax.ShapeDtypeStruct((N_CHIPS,) + s.shape, s.dtype) for s in srcs]
    return [jax.ShapeDtypeStruct(s.shape, s.dtype) for s in srcs]


def _xchg_scratch(kind, n):
    per = 1 if kind == "swap" else N_CHIPS - 1
    return [pltpu.SemaphoreType.DMA((per * n,)), pltpu.SemaphoreType.DMA((per * n,)), pltpu.SemaphoreType.DMA((n,))]


def _xchg_copies(kind, srcs, dsts, send_sems, recv_sems, local_sems, arrivals):
    x, y, c = lax.axis_index("x"), lax.axis_index("y"), lax.axis_index("c")
    me = 2 * x + y
    peers = [(1 - x, y), (x, 1 - y), (1 - x, 1 - y)]
    local, out, back = [], [], []
    for i, (s, d) in enumerate(zip(srcs, dsts)):
        if kind == "swap":
            cp = pltpu.make_async_remote_copy(src_ref=s, dst_ref=d, send_sem=send_sems.at[i], recv_sem=recv_sems.at[i],
                                              device_id=(x, y, 1 - c), device_id_type=_MESH)
            out.append(cp)
            back.append(cp)
            continue
        local.append(pltpu.make_async_copy(s if kind == "gather" else s.at[me], d.at[me], local_sems.at[i]))
        for kk, (px, py) in enumerate(peers):
            j = (N_CHIPS - 1) * i + kk
            theirs = 2 * px + py
            out.append(pltpu.make_async_remote_copy(
                src_ref=s if kind == "gather" else s.at[theirs], dst_ref=d.at[me], send_sem=send_sems.at[j],
                recv_sem=recv_sems.at[j], device_id=(px, py, c), device_id_type=_MESH))
            if arrivals:
                back.append(pltpu.make_async_remote_copy(
                    src_ref=s if kind == "gather" else s.at[me], dst_ref=d.at[theirs], send_sem=send_sems.at[j],
                    recv_sem=recv_sems.at[j], device_id=(px, py, c), device_id_type=_MESH))
    return local, out, back


def _xchg_start(kind, srcs, dsts, sems):
    local, out, _ = _xchg_copies(kind, srcs, dsts, *sems, arrivals=False)
    for cp in local + out:
        cp.start()


def _xchg_wait(kind, srcs, dsts, sems):
    local, out, back = _xchg_copies(kind, srcs, dsts, *sems, arrivals=True)
    for cp in back:
        cp.wait_recv()
    for cp in out:
        cp.wait_send()
    for cp in local:
        cp.wait()


def _xchg_at_ends(kind, srcs, dsts, sems, grid, first):
    ids = [pl.program_id(a) for a in range(len(grid))]
    cond = None
    for i, n in zip(ids, grid):
        c = (i == 0) if first else (i == n - 1)
        cond = c if cond is None else jnp.logical_and(cond, c)

    @pl.when(cond)
    def _():
        (_xchg_start if first else _xchg_wait)(kind, srcs, dsts, sems)


def _xchg_call(kind, srcs, name):
    n = len(srcs)

    def kern(*refs):
        s, d, sems = refs[:n], refs[n:2 * n], refs[2 * n:]
        _xchg_start(kind, s, d, sems)
        _xchg_wait(kind, s, d, sems)

    return pl.pallas_call(
        kern, name=name, in_specs=[_ANY] * n, out_specs=[_ANY] * n, out_shape=_xchg_out_shapes(kind, srcs),
        scratch_shapes=_xchg_scratch(kind, n),
    )(*srcs)


_DIMS = {"nn": (((1,), (0,)), ((), ())), "nt": (((1,), (1,)), ((), ())), "tn": (((0,), (0,)), ((), ()))}


def _matmul(a, b, mode, out_dtype, name, *, b_chunks=False, out_chunks=False, xchg=None):
    if b_chunks:
        nchunk, brows, bcols = b.shape
        bshape = (brows, nchunk * bcols)
    else:
        bshape = b.shape
    if mode == "nn":
        (m, k), (_, n) = a.shape, bshape
    elif mode == "nt":
        (m, k), (n, _) = a.shape, bshape
    else:
        (k, m), (_, n) = a.shape, bshape
    tm, tk = (1408, 1024) if mode == "tn" else (1024, 1408)
    tm, tn, tk = _tile(m, tm), _tile(n, 1408), _tile(k, tk)
    if b_chunks:
        if mode == "nn":
            tn = bcols
        else:
            assert mode == "nt"
            tk = bcols
    if out_chunks:
        assert n % N_CHIPS == 0
        tn = n // N_CHIPS
    ni, nj, nk = m // tm, n // tn, k // tk
    dims = _DIMS[mode]
    n_x = 0 if xchg is None else len(xchg[1])

    def kern(*refs):
        a_ref, b_ref = refs[0], refs[1]
        xs = refs[2:2 + n_x]
        o_ref = refs[2 + n_x]
        xd = refs[3 + n_x:3 + 2 * n_x]
        acc_ref = refs[3 + 2 * n_x]
        sems = refs[4 + 2 * n_x:]
        kk = pl.program_id(2)
        if n_x:
            _xchg_at_ends(xchg[0], xs, xd, sems, (ni, nj, nk), True)

        @pl.when(kk == 0)
        def _():
            acc_ref[...] = jnp.zeros_like(acc_ref)

        acc_ref[...] += lax.dot_general(a_ref[...], b_ref[...], dims, preferred_element_type=F32)

        @pl.when(kk == nk - 1)
        def _():
            o_ref[...] = acc_ref[...].astype(o_ref.dtype)

        if n_x:
            _xchg_at_ends(xchg[0], xs, xd, sems, (ni, nj, nk), False)

    if mode == "tn":
        a_spec = pl.BlockSpec((tk, tm), lambda i, j, kk: (kk, i))
    else:
        a_spec = pl.BlockSpec((tm, tk), lambda i, j, kk: (i, kk))
    if b_chunks and mode == "nn":
        b_spec = pl.BlockSpec((None, tk, tn), lambda i, j, kk: (j, kk, 0))
    elif b_chunks:
        b_spec = pl.BlockSpec((None, tn, tk), lambda i, j, kk: (kk, j, 0))
    elif mode == "nt":
        b_spec = pl.BlockSpec((tn, tk), lambda i, j, kk: (j, kk))
    else:
        b_spec = pl.BlockSpec((tk, tn), lambda i, j, kk: (kk, j))
    if out_chunks:
        o_spec = pl.BlockSpec((None, tm, tn), lambda i, j, kk: (j, i, 0))
        o_shape = jax.ShapeDtypeStruct((N_CHIPS, m, tn), out_dtype)
    else:
        o_spec = pl.BlockSpec((tm, tn), lambda i, j, kk: (i, j))
        o_shape = jax.ShapeDtypeStruct((m, n), out_dtype)
    xs = [] if xchg is None else list(xchg[1])
    res = pl.pallas_call(
        kern, name=name, grid=(ni, nj, nk),
        in_specs=[a_spec, b_spec] + [_ANY] * n_x, out_specs=[o_spec] + [_ANY] * n_x,
        out_shape=[o_shape] + (_xchg_out_shapes(xchg[0], xs) if n_x else []),
        scratch_shapes=[pltpu.VMEM((tm, tn), F32)] + (_xchg_scratch(xchg[0], n_x) if n_x else []),
        compiler_params=_params(("arbitrary", "arbitrary", "arbitrary") if n_x else ("parallel", "parallel", "arbitrary")),
    )(a, b, *xs)
    return (res[0], list(res[1:])) if n_x else res[0]


def _rowcall(body, *, name, T, S, tr, rows, exs=(), pars=(), row_outs=(), ex_outs=(), par_outs=(), aliases=None):
    assert S % tr == 0 and T % S == 0
    per_ex = S // tr
    nb = T // S
    n_rows, n_exs, n_pars = len(rows), len(exs), len(pars)
    n_ro, n_eo, n_po = len(row_outs), len(ex_outs), len(par_outs)

    def kern(*refs):
        ins = refs[:n_rows + n_exs + n_pars]
        outs = refs[n_rows + n_exs + n_pars:]
        rv = [r[...].astype(F32) for r in ins[:n_rows]]
        ev = [r[0] for r in ins[n_rows:n_rows + n_exs]]
        pv = [r[...] for r in ins[n_rows + n_exs:]]
        ro, eo, po = body(rv, ev, pv)
        i = pl.program_id(0)
        for ref, val in zip(outs[:n_ro], ro):
            if isinstance(val, (list, tuple)):
                off = 0
                for piece in val:
                    w = piece.shape[-1]
                    ref[:, off:off + w] = piece.astype(ref.dtype)
                    off += w
            else:
                ref[...] = val.astype(ref.dtype)
        first_of_ex = (i % per_ex) == 0
        for ref, val in zip(outs[n_ro:n_ro + n_eo], eo):
            @pl.when(first_of_ex)
            def _(ref=ref, val=val):
                ref[0] = val

            @pl.when(jnp.logical_not(first_of_ex))
            def _(ref=ref, val=val):
                ref[0] += val
        for ref, val in zip(outs[n_ro + n_eo:], po):
            @pl.when(i == 0)
            def _(ref=ref, val=val):
                ref[...] = val

            @pl.when(i != 0)
            def _(ref=ref, val=val):
                ref[...] += val

    in_specs = [pl.BlockSpec((tr, w), functools.partial(lambda i, cb: (i, cb), cb=cb)) for (_, w, cb) in rows]
    in_specs += [pl.BlockSpec((1, 1, e.shape[-1]), lambda i: (i // per_ex, 0, 0)) for e in exs]
    in_specs += [pl.BlockSpec(p.shape, functools.partial(lambda i, nd: (0,) * nd, nd=p.ndim)) for p in pars]
    out_specs = [pl.BlockSpec((tr, w), functools.partial(lambda i, cb: (i, cb), cb=cb)) for (_, _, w, cb) in row_outs]
    out_specs += [pl.BlockSpec((1, 1, f), lambda i: (i // per_ex, 0, 0)) for f in ex_outs]
    out_specs += [pl.BlockSpec(tuple(s), functools.partial(lambda i, nd: (0,) * nd, nd=len(s))) for s in par_outs]
    out_shape = [jax.ShapeDtypeStruct((T, tw), dt) for (tw, dt, _, _) in row_outs]
    out_shape += [jax.ShapeDtypeStruct((nb, 1, f), F32) for f in ex_outs]
    out_shape += [jax.ShapeDtypeStruct(tuple(s), F32) for s in par_outs]
    res = pl.pallas_call(
        kern, name=name, grid=(T // tr,), in_specs=in_specs, out_specs=out_specs, out_shape=out_shape,
        input_output_aliases=aliases or {}, compiler_params=_params(("arbitrary",)),
    )(*[r[0] for r in rows], *exs, *pars)
    return res[:n_ro], res[n_ro:n_ro + n_eo], res[n_ro + n_eo:]


def _rms(x, w, n=None):
    n = x.shape[-1] if n is None else n
    ms = jnp.sum(x * x, axis=-1, keepdims=True) * (1.0 / n)
    return x * lax.rsqrt(ms + NORM_EPS) * w


def _gelu(x):
    return 0.5 * x * (1.0 + lax.erf(x * np.float32(1.0 / math.sqrt(2.0))))


def _silu(x):
    return x * jax.nn.sigmoid(x)


@jax.custom_vjp
def _rope(x, cos, sin):
    return x * cos + pltpu.roll(x, 64, 1) * sin


def _rope_fwd(x, cos, sin):
    return _rope(x, cos, sin), (cos, sin)


def _rope_bwd(res, dy):
    cos, sin = res
    return dy * cos + pltpu.roll(dy * sin, 64, 1), None, None


_rope.defvjp(_rope_fwd, _rope_bwd)


def _heads(x, n):
    return [x[:, h * LANES:(h + 1) * LANES] for h in range(n)]


@functools.partial(jax.custom_vjp, nondiff_argnums=(1,))
def _lroll(x, shift):
    return pltpu.roll(x, shift, 1)


def _lroll_fwd(x, shift):
    return _lroll(x, shift), None


def _lroll_bwd(shift, _, dy):
    return (pltpu.roll(dy, (LANES - shift) % LANES, 1),)


_lroll.defvjp(_lroll_fwd, _lroll_bwd)


def _spread_rotary(v):
    low = lax.broadcasted_iota(jnp.int32, (1, LANES), 1) < 64
    return [jnp.where(low, v, _lroll(v, 32)), jnp.where(low, _lroll(v, 64), _lroll(v, 96))]


def _spread_values(v):
    low = lax.broadcasted_iota(jnp.int32, (1, LANES), 1) < 64
    return [jnp.where(low, v, 0.0), jnp.where(low, _lroll(v, 64), 0.0)]


def _f_norm_mod(x, w, sc, sh):
    return _rms(x, w) * (1.0 + sc) + sh


def _f_resid_norm_mod(xa, delta, g, w, sc, sh):
    xn = xa + g * delta
    return xn, _f_norm_mod(xn, w, sc, sh)


def _f_mixprep(proj, ca, sa, cb, sb, qnw, kvnw):
    qa = [_rope(p, ca, sa) for pair in _heads(proj[:, P_AQ:P_AK], A_Q_HEADS // 2) for p in _spread_rotary(pair)]
    ka = [_rope(p, ca, sa) for p in _spread_rotary(proj[:, P_AK:P_AV])]
    va = _spread_values(proj[:, P_AV:P_CQ])
    cqn = _rms(proj[:, P_CQ:P_CKV], qnw)
    ckvn = _rms(proj[:, P_CKV:P_MIX], kvnw)
    kr = _rope(proj[:, P_KR:P_AQ], cb, sb)
    return jnp.concatenate(qa, -1), jnp.concatenate(ka, -1), jnp.concatenate(va, -1), cqn, ckvn, kr


def _f_mlaprep(q, kv, kr, cb, sb):
    qs = [_rope(p, cb, sb) for p in _heads(q, B_HEADS)]
    ks = [p + kr for p in _heads(kv[:, :B_HEADS * LANES], B_HEADS)]
    return jnp.concatenate(qs, -1), jnp.concatenate(ks, -1), kv[:, B_HEADS * LANES:]


def _f_sgu(cu, cv, ln_w, ln_b, w_s, b_col):
    u = _gelu(cu)
    v = _gelu(cv)
    mu = jnp.mean(v, axis=-1, keepdims=True)
    var = jnp.mean(jnp.square(v - mu), axis=-1, keepdims=True)
    vn = (v - mu) * lax.rsqrt(var + NORM_EPS) * ln_w + ln_b
    r = lax.broadcasted_iota(jnp.int32, (C_CHUNK, C_CHUNK), 0)
    c = lax.broadcasted_iota(jnp.int32, (C_CHUNK, C_CHUNK), 1)
    lane = lax.broadcasted_iota(jnp.int32, (1, LANES), 1)
    per_block = LANES // C_GROUP_DIM
    blocks = []
    for blk, vb in enumerate(_heads(vn, C_WIDTH // LANES)):
        mixed = jnp.zeros(vb.shape, F32)
        for j in range(per_block):
            g = blk * per_block + j
            gm = (lane // C_GROUP_DIM == j).astype(F32)
            wg = jnp.where(r >= c, w_s[g], 0.0).astype(BF16)
            mixed = mixed + jnp.dot(wg, (vb * gm).astype(BF16), preferred_element_type=F32) + b_col[g] * gm
        blocks.append(mixed)
    return u * jnp.concatenate(blocks, -1)


@jax.custom_vjp
def _pack_pairs(x):
    heads = _heads(x, x.shape[-1] // LANES)
    return jnp.concatenate([heads[i] + pltpu.roll(heads[i + 1], 64, 1) for i in range(0, len(heads), 2)], -1)


def _pack_pairs_fwd(x):
    return _pack_pairs(x), None


def _pack_pairs_bwd(_, dy):
    out = []
    for p in _heads(dy, dy.shape[-1] // LANES):
        out += [p, pltpu.roll(p, 64, 1)]
    return (jnp.concatenate(out, -1),)


_pack_pairs.defvjp(_pack_pairs_fwd, _pack_pairs_bwd)


def _f_outnorm(oa, ob, yc, gw):
    na, nb = A_Q_HEADS * HEAD_DIM, B_HEADS * B_V
    ya = _rms(_pack_pairs(oa), gw[:, :na])
    yb = _rms(_pack_pairs(ob), gw[:, na:na + nb])
    ycn = _rms(yc, gw[:, na + nb:])
    return jnp.concatenate([ya, yb, ycn], -1)


def _f_swiglu(gate, up):
    return _silu(gate) * up


def _mask(q_start, k_start, tq, tk, window):
    qpos = q_start + lax.broadcasted_iota(jnp.int32, (tq, tk), 0)
    kpos = k_start + lax.broadcasted_iota(jnp.int32, (tq, tk), 1)
    m = kpos <= qpos
    if window is not None:
        m = jnp.logical_and(m, qpos - kpos < window)
    return m


def _tile_fwd(qv, kk, vv, q_start, k_start, n_free, scale, window, m0, l0):
    tq = qv.shape[0]
    W = kk.shape[0]
    c = scale * LOG2_E
    parts = []
    if n_free > 0:
        parts.append((lax.dot_general(qv, kk[:n_free], _DIMS["nt"], preferred_element_type=F32), vv[:n_free]))
    if W > n_free:
        s = lax.dot_general(qv, kk[n_free:], _DIMS["nt"], preferred_element_type=F32)
        s = jnp.where(_mask(q_start, k_start + n_free, tq, W - n_free, window), s, NEG_INF)
        parts.append((s, vv[n_free:]))
    m = None if m0 is None else m0 * (1.0 / scale)
    for s, _ in parts:
        mx = jnp.max(s, axis=-1, keepdims=True)
        m = mx if m is None else jnp.maximum(m, mx)
    l = None if l0 is None else l0 * jnp.exp2((m0 * (1.0 / scale) - m) * c)
    o = None
    for s, vpart in parts:
        p = jnp.exp2((s - m) * c)
        ps = jnp.sum(p, axis=-1, keepdims=True)
        l = ps if l is None else l + ps
        po = jnp.dot(p.astype(BF16), vpart, preferred_element_type=F32)
        o = po if o is None else o + po
    return o / l, m * scale + jnp.log(l)


def _tile_bwd(qv, kk, vv, dof, ov, lse, q_start, k_start, n_free, scale, window):
    tq = qv.shape[0]
    W = kk.shape[0]
    dob = dof.astype(BF16)
    delta = jnp.sum(dof * ov, axis=-1, keepdims=True)
    c = scale * LOG2_E
    lse2 = lse * LOG2_E
    dq = None
    outs = []
    for (a, b, masked) in ((0, n_free, False), (n_free, W, True)):
        if b <= a:
            continue
        kp, vp = kk[a:b], vv[a:b]
        s = lax.dot_general(qv, kp, _DIMS["nt"], preferred_element_type=F32)
        if masked:
            s = jnp.where(_mask(q_start, k_start + a, tq, b - a, window), s, NEG_INF)
        p = jnp.exp2(s * c - lse2)
        dp = lax.dot_general(dob, vp, _DIMS["nt"], preferred_element_type=F32)
        ds = (p * ((dp - delta) * scale)).astype(BF16)
        d = jnp.dot(ds, kp, preferred_element_type=F32)
        dq = d if dq is None else dq + d
        dkp = lax.dot_general(ds, qv, _DIMS["tn"], preferred_element_type=F32)
        dvp = lax.dot_general(p.astype(BF16), dob, _DIMS["tn"], preferred_element_type=F32)
        outs.append((a, dkp, dvp))
    return dq, outs


def _attn_fwd(q, k, v, sinks, *, B, S, HQ, HK, window, scale, tq, band, name, xchg=None):
    G = HQ // HK
    nq = S // tq
    T = B * S
    has_sink = sinks is not None
    n_x = 0 if xchg is None else len(xchg[1])
    n_in = 4 if has_sink else 3
    hp = G if window is not None else 1
    grid = (B, HQ // hp, nq)

    def kern(*refs):
        xs, xd, sems = refs[n_in:n_in + n_x], refs[n_in + n_x + 2:n_in + 2 * n_x + 2], refs[n_in + 2 * n_x + 2:]
        refs = refs[:n_in] + refs[n_in + n_x:n_in + n_x + 2]
        if n_x:
            _xchg_at_ends(xchg[0], xs, xd, sems, grid, True)
        if has_sink:
            q_ref, k_ref, v_ref, s_ref, o_ref, lse_ref = refs
        else:
            q_ref, k_ref, v_ref, o_ref, lse_ref = refs
        q_start = pl.program_id(2) * tq

        def sink(g, rows):
            if not has_sink:
                return None, None
            return jnp.broadcast_to(s_ref[g][:, :1], (rows, 1)), jnp.ones((rows, 1), F32)

        def finish(o, lse):
            o_ref[...] = o.astype(o_ref.dtype)
            lse_ref[...] = jnp.broadcast_to(lse, (tq, LANES))

        if window is None:
            qv = q_ref[...]
            m0, l0 = sink(0, tq)
            bidx = q_start // band
            for bb in range(S // band):
                @pl.when(bidx == bb)
                def _(bb=bb):
                    W = (bb + 1) * band
                    finish(*_tile_fwd(qv, k_ref[0:W, :], v_ref[0:W, :], q_start, 0, bb * band, scale, None, m0, l0))
        else:
            sub = min(tq, WINDOW_SUB)
            W = min(S, sub + window)
            for r in range(0, tq, sub):
                k_start = pl.multiple_of(jnp.maximum(q_start + r - window, 0), window)
                kk, vv = k_ref[pl.ds(k_start, W), :], v_ref[pl.ds(k_start, W), :]
                for g in range(hp):
                    cols = slice(g * LANES, (g + 1) * LANES)
                    o, lse = _tile_fwd(q_ref[r:r + sub, cols], kk, vv, q_start + r, k_start, 0, scale, window, *sink(g, sub))
                    o_ref[r:r + sub, cols] = o.astype(o_ref.dtype)
                    lse_ref[r:r + sub, cols] = jnp.broadcast_to(lse, (sub, LANES))
        if n_x:
            _xchg_at_ends(xchg[0], xs, xd, sems, grid, False)

    q_spec = pl.BlockSpec((tq, hp * LANES), lambda b, h, i: (b * nq + i, h))
    kv_spec = pl.BlockSpec((S, LANES), lambda b, h, i: (b, h * hp // G))
    in_specs = [q_spec, kv_spec, kv_spec]
    args = [q, k, v]
    if has_sink:
        in_specs.append(pl.BlockSpec((hp, 1, LANES), lambda b, h, i: (h, 0, 0)))
        args.append(sinks)
    xs = [] if xchg is None else list(xchg[1])
    res = pl.pallas_call(
        kern, name=name, grid=grid, in_specs=in_specs + [_ANY] * n_x, out_specs=[q_spec, q_spec] + [_ANY] * n_x,
        out_shape=[jax.ShapeDtypeStruct((T, HQ * LANES), BF16), jax.ShapeDtypeStruct((T, HQ * LANES), F32)]
        + (_xchg_out_shapes(xchg[0], xs) if n_x else []),
        scratch_shapes=_xchg_scratch(xchg[0], n_x) if n_x else [],
        compiler_params=_params(("arbitrary",) * 3 if n_x else ("parallel", "parallel", "arbitrary")),
    )(*args, *xs)
    return ((res[0], res[1]), list(res[2:])) if n_x else res


def _attn_bwd(q, k, v, o, lse, do, sinks, *, B, S, HQ, HK, window, scale, tq, band, name, xchg=None):
    G = HQ // HK
    nq = S // tq
    T = B * S
    has_sink = sinks is not None
    n_x = 0 if xchg is None else len(xchg[1])
    n_in, n_out = (7, 4) if has_sink else (6, 3)
    hp = G if window is not None else 1
    ng = G // hp
    grid = (B, HK, ng, nq)

    def kern(*refs):
        xs, xd = refs[n_in:n_in + n_x], refs[n_in + n_x + n_out:n_in + 2 * n_x + n_out]
        dk_acc, dv_acc = refs[n_in + 2 * n_x + n_out:n_in + 2 * n_x + n_out + 2]
        sems = refs[n_in + 2 * n_x + n_out + 2:]
        refs = refs[:n_in] + refs[n_in + n_x:n_in + n_x + n_out]
        if n_x:
            _xchg_at_ends(xchg[0], xs, xd, sems, grid, True)
        if has_sink:
            q_ref, k_ref, v_ref, o_ref, lse_ref, do_ref, s_ref, dq_ref, dk_ref, dv_ref, ds_ref = refs
        else:
            q_ref, k_ref, v_ref, o_ref, lse_ref, do_ref, dq_ref, dk_ref, dv_ref = refs
        gi = pl.program_id(2)
        qi = pl.program_id(3)
        q_start = qi * tq

        @pl.when(jnp.logical_and(gi == 0, qi == 0))
        def _():
            dk_acc[...] = jnp.zeros_like(dk_acc)
            dv_acc[...] = jnp.zeros_like(dv_acc)

        if window is None:
            qv = q_ref[...]
            dof = do_ref[...].astype(F32)
            ov = o_ref[...].astype(F32)
            lse_v = lse_ref[...][:, :1]
            bidx = q_start // band
            for bb in range(S // band):
                @pl.when(bidx == bb)
                def _(bb=bb):
                    W = (bb + 1) * band
                    dq, outs = _tile_bwd(qv, k_ref[0:W, :], v_ref[0:W, :], dof, ov, lse_v, q_start, 0, bb * band, scale, None)
                    dq_ref[...] = dq.astype(dq_ref.dtype)
                    for a, dkp, dvp in outs:
                        dk_acc[a:a + dkp.shape[0], :] += dkp
                        dv_acc[a:a + dvp.shape[0], :] += dvp
        else:
            sub = min(tq, WINDOW_SUB)
            W = min(S, sub + window)
            parts = []
            sink_parts = []
            for g in range(hp):
                cols = slice(g * LANES, (g + 1) * LANES)
                acc = jnp.zeros((1, LANES), F32)
                for r in range(0, tq, sub):
                    rows = slice(r, r + sub)
                    k_start = pl.multiple_of(jnp.maximum(q_start + r - window, 0), window)
                    dof, ov = do_ref[rows, cols].astype(F32), o_ref[rows, cols].astype(F32)
                    lse_v = lse_ref[rows, cols][:, :1]
                    dq, outs = _tile_bwd(q_ref[rows, cols], k_ref[pl.ds(k_start, W), :], v_ref[pl.ds(k_start, W), :],
                                         dof, ov, lse_v, q_start + r, k_start, 0, scale, window)
                    dq_ref[rows, cols] = dq.astype(dq_ref.dtype)
                    parts.append((k_start, outs[0][1], outs[0][2]))
                    if has_sink:
                        delta = jnp.sum(dof * ov, axis=-1, keepdims=True)
                        acc = acc - jnp.sum(jnp.exp(s_ref[g][:, :1] - lse_v) * delta, axis=0, keepdims=True)
                sink_parts.append(acc)
            for k_start, dkp, dvp in parts:
                dk_acc[pl.ds(k_start, W), :] += dkp
                dv_acc[pl.ds(k_start, W), :] += dvp

        @pl.when(jnp.logical_and(gi == ng - 1, qi == nq - 1))
        def _():
            dk_ref[...] = dk_acc[...].astype(dk_ref.dtype)
            dv_ref[...] = dv_acc[...].astype(dv_ref.dtype)
        if has_sink:
            assert window is not None
            for g, part in enumerate(sink_parts):
                part = jnp.broadcast_to(part, (1, LANES))

                @pl.when(qi == 0)
                def _(g=g, part=part):
                    ds_ref[g] = part

                @pl.when(qi != 0)
                def _(g=g, part=part):
                    ds_ref[g] += part
        if n_x:
            _xchg_at_ends(xchg[0], xs, xd, sems, grid, False)

    q_spec = pl.BlockSpec((tq, hp * LANES), lambda b, hk, g, i: (b * nq + i, hk * ng + g))
    kv_spec = pl.BlockSpec((S, LANES), lambda b, hk, g, i: (b, hk))
    in_specs = [q_spec, kv_spec, kv_spec, q_spec, q_spec, q_spec]
    args = [q, k, v, o, lse, do]
    out_specs = [q_spec, kv_spec, kv_spec]
    out_shape = [jax.ShapeDtypeStruct((T, HQ * LANES), BF16), jax.ShapeDtypeStruct((T, HK * LANES), BF16),
                 jax.ShapeDtypeStruct((T, HK * LANES), BF16)]
    if has_sink:
        in_specs.append(pl.BlockSpec((hp, 1, LANES), lambda b, hk, g, i: (hk * ng + g, 0, 0)))
        args.append(sinks)
        out_specs.append(pl.BlockSpec((hp, 1, LANES), lambda b, hk, g, i: (b * HK * ng + hk * ng + g, 0, 0)))
        out_shape.append(jax.ShapeDtypeStruct((B * HQ, 1, LANES), F32))
    xs = [] if xchg is None else list(xchg[1])
    res = pl.pallas_call(
        kern, name=name, grid=grid, in_specs=in_specs + [_ANY] * n_x, out_specs=out_specs + [_ANY] * n_x,
        out_shape=out_shape + (_xchg_out_shapes(xchg[0], xs) if n_x else []),
        scratch_shapes=[pltpu.VMEM((S, LANES), F32)] * 2 + (_xchg_scratch(xchg[0], n_x) if n_x else []),
        compiler_params=_params(("arbitrary",) * 4 if n_x else ("parallel", "parallel", "arbitrary", "arbitrary")),
    )(*args, *xs)
    main = tuple(res[:n_out]) if has_sink else (*res[:n_out], None)
    return (main, list(res[n_out:])) if n_x else main


def _rope_tables(pos_col, freq, sign, keep, name):
    T = pos_col.shape[0]
    tr = _tile(T, 1024)

    def kern(p_ref, f_ref, s_ref, k_ref, c_out, s_out):
        ang = p_ref[...] * f_ref[...]
        c_out[...] = jnp.cos(ang) * k_ref[...]
        s_out[...] = jnp.sin(ang) * s_ref[...]

    spec = pl.BlockSpec((tr, LANES), lambda i: (i, 0))
    par = pl.BlockSpec((1, LANES), lambda i: (0, 0))
    return pl.pallas_call(
        kern, name=name, grid=(T // tr,), in_specs=[pl.BlockSpec((tr, 1), lambda i: (i, 0)), par, par, par],
        out_specs=[spec, spec], out_shape=[jax.ShapeDtypeStruct((T, LANES), F32)] * 2,
        compiler_params=_params(("parallel",)),
    )(pos_col, freq, sign, keep)


def _ada_fwd(c8, ada_w, ada_b):
    L, D, N = ada_w.shape

    def kern(c_ref, w_ref, b_ref, o_ref):
        act = _silu(c_ref[...]).astype(BF16)
        o_ref[0] = jnp.dot(act, w_ref[0], preferred_element_type=F32) + b_ref[0]

    return pl.pallas_call(
        kern, name="ada_fwd", grid=(L,),
        in_specs=[pl.BlockSpec((8, D), lambda l: (0, 0)), pl.BlockSpec((1, D, N), lambda l: (l, 0, 0)),
                  pl.BlockSpec((1, 1, N), lambda l: (l, 0, 0))],
        out_specs=pl.BlockSpec((1, 8, N), lambda l: (l, 0, 0)),
        out_shape=jax.ShapeDtypeStruct((L, 8, N), F32), compiler_params=_params(("parallel",)),
    )(c8, ada_w, ada_b)


def _ada_bwd(c8, dmod):
    L, _, N = dmod.shape
    D = c8.shape[1]

    def kern(c_ref, d_ref, gw_ref):
        act = _silu(c_ref[...]).astype(BF16)
        gw_ref[0] = lax.dot_general(act, d_ref[0].astype(BF16), _DIMS["tn"], preferred_element_type=F32)

    return pl.pallas_call(
        kern, name="ada_bwd", grid=(L,),
        in_specs=[pl.BlockSpec((8, D), lambda l: (0, 0)), pl.BlockSpec((1, 8, N), lambda l: (l, 0, 0))],
        out_specs=pl.BlockSpec((1, D, N), lambda l: (l, 0, 0)),
        out_shape=jax.ShapeDtypeStruct((L, D, N), F32), compiler_params=_params(("parallel",)),
    )(c8, dmod)


def _sum_examples(d):
    L, _, N = d.shape

    def kern(d_ref, o_ref):
        o_ref[...] = jnp.sum(d_ref[...], axis=1, keepdims=True)

    vm = pl.BlockSpec(memory_space=pltpu.VMEM)
    return pl.pallas_call(kern, name="sum_examples", in_specs=[vm], out_specs=vm,
                          out_shape=jax.ShapeDtypeStruct((L, 1, N), F32))(d)


def _adamw(w, ga, gb, m, v, name):
    rows, cols = w.shape
    tr = _row_div(rows, 256)
    two = gb is not None

    def kern(*refs):
        if two:
            w_ref, ga_ref, gb_ref, m_ref, v_ref, g_out, d_out, m_out, v_out = refs
            gv = ga_ref[...] + gb_ref[...]
        else:
            w_ref, ga_ref, m_ref, v_ref, g_out, d_out, m_out, v_out = refs
            gv = ga_ref[...]
        mn = ADAM_B1 * m_ref[...] + (1.0 - ADAM_B1) * gv
        vn = ADAM_B2 * v_ref[...] + (1.0 - ADAM_B2) * jnp.square(gv)
        m_hat = mn / (1.0 - ADAM_B1 ** ADAM_STEP)
        v_hat = vn / (1.0 - ADAM_B2 ** ADAM_STEP)
        g_out[...] = gv
        d_out[...] = -ADAM_LR * (m_hat / (jnp.sqrt(v_hat) + ADAM_EPS) + ADAM_WD * w_ref[...])
        m_out[...] = mn
        v_out[...] = vn

    spec = pl.BlockSpec((tr, cols), lambda i: (i, 0))
    args = [w, ga, gb, m, v] if two else [w, ga, m, v]
    return pl.pallas_call(
        kern, name=name, grid=(rows // tr,), in_specs=[spec] * len(args), out_specs=[spec] * 4,
        out_shape=[jax.ShapeDtypeStruct((rows, cols), F32)] * 4, compiler_params=_params(("parallel",)),
    )(*args)


def _sum_slots(x, name):
    n, rows, cols = x.shape
    tr = _row_div(rows, 256)

    def kern(x_ref, o_ref):
        acc = x_ref[0].astype(F32)
        for j in range(1, n):
            acc = acc + x_ref[j].astype(F32)
        o_ref[...] = acc

    return pl.pallas_call(
        kern, name=name, grid=(rows // tr,), in_specs=[pl.BlockSpec((n, tr, cols), lambda i: (0, i, 0))],
        out_specs=pl.BlockSpec((tr, cols), lambda i: (i, 0)), out_shape=jax.ShapeDtypeStruct((rows, cols), F32),
        compiler_params=_params(("parallel",)),
    )(x)


def _all_reduce_small(blob, name, swaps=()):
    R, C = blob.shape
    n = len(swaps)

    def kern(*refs):
        src, xs, out, xd = refs[0], refs[1:1 + n], refs[1 + n], refs[2 + n:2 + 2 * n]
        pair, chips, send_sems, recv_sems = refs[2 + 2 * n:6 + 2 * n]
        swap_sems = refs[6 + 2 * n:]
        if n:
            _xchg_start("swap", xs, xd, swap_sems)
        x, y, c = lax.axis_index("x"), lax.axis_index("y"), lax.axis_index("c")
        me = 2 * x + y
        to_sibling = pltpu.make_async_remote_copy(
            src_ref=src, dst_ref=pair, send_sem=send_sems.at[0], recv_sem=recv_sems.at[0],
            device_id=(x, y, 1 - c), device_id_type=_MESH)
        to_sibling.start()
        to_sibling.wait()
        chips[me] = src[...] + pair[...]
        peers = [(1 - x, y), (x, 1 - y), (1 - x, 1 - y)]
        copies = [pltpu.make_async_remote_copy(
            src_ref=chips.at[me], dst_ref=chips.at[me], send_sem=send_sems.at[1 + kk], recv_sem=recv_sems.at[1 + kk],
            device_id=(px, py, c), device_id_type=_MESH) for kk, (px, py) in enumerate(peers)]
        for cp in copies:
            cp.start()
        for kk, (px, py) in enumerate(peers):
            pltpu.make_async_remote_copy(
                src_ref=chips.at[me], dst_ref=chips.at[2 * px + py], send_sem=send_sems.at[1 + kk],
                recv_sem=recv_sems.at[1 + kk], device_id=(px, py, c), device_id_type=_MESH).wait_recv()
        for cp in copies:
            cp.wait_send()
        acc = chips[0]
        for j in range(1, N_CHIPS):
            acc = acc + chips[j]
        out[...] = acc
        if n:
            _xchg_wait("swap", xs, xd, swap_sems)

    vm = pl.BlockSpec(memory_space=pltpu.VMEM)
    swaps = list(swaps)
    res = pl.pallas_call(
        kern, name=name, in_specs=[vm] + [_ANY] * n, out_specs=[vm] + [_ANY] * n,
        out_shape=[jax.ShapeDtypeStruct((R, C), F32)] + _xchg_out_shapes("swap", swaps),
        scratch_shapes=[pltpu.VMEM((R, C), F32), pltpu.VMEM((N_CHIPS, R, C), F32), pltpu.SemaphoreType.DMA((4,)),
                        pltpu.SemaphoreType.DMA((4,))] + (_xchg_scratch("swap", n) if n else []),
        compiler_params=pltpu.CompilerParams(vmem_limit_bytes=VMEM_LIMIT),
    )(blob, *swaps)
    return res[0], list(res[1:])


def _row_tile(S, wide=False):
    for t in ((256,) if wide else (512, 256)):
        if S % t == 0:
            return t
    return 128


def _attn_tiles(S):
    return min(S, 1024), min(S, 256), min(S, 256)


def _hosted(fn, host, got, kind, name, *args, **kw):
    if host is not None and host.get(name):
        res, xs = fn(*args, name=name, xchg=(kind, host[name]), **kw)
        got[name] = xs
        return res
    return fn(*args, name=name, **kw)


def _mm(host, got, kind, a, b, mode, out_dtype, name, **kw):
    return _hosted(_matmul, host, got, kind, name, a, b, mode, out_dtype, **kw)


def _merge_late(w, late, got, name):
    if not late or name not in late:
        return w
    return {**w, **_large_operands(dict(zip(late[name], got[name])))}


def _sgu_rows(S):
    return 4 * C_CHUNK if S % (4 * C_CHUNK) == 0 else C_CHUNK


def _layer_fwd(xin, prev, mods, w, tabs, B, S, host=None, late=None):
    T = B * S
    tr = _row_tile(S)
    sh1, sc1, g1, sh2, sc2, g2 = mods
    ca, sa, cb, sb = tabs
    sv = {}
    got = {}
    if prev is None:
        def body(rv, ev, pv):
            return [_f_norm_mod(rv[0], pv[0], ev[0], ev[1])], [], []
        (h,), _, _ = _rowcall(body, name="f_norm_mod", T=T, S=S, tr=tr, rows=[(xin, D_MODEL, 0)], exs=[sc1, sh1],
                              pars=[w["norm1_w"]], row_outs=[(D_MODEL, BF16, D_MODEL, 0)])
        x = xin
    else:
        x1p, dp, g2p = prev

        def body(rv, ev, pv):
            xn, hh = _f_resid_norm_mod(rv[0], rv[1], ev[0], pv[0], ev[1], ev[2])
            return [xn, hh], [], []
        (x, h), _, _ = _rowcall(body, name="f_resid_norm_mod1", T=T, S=S, tr=tr,
                                rows=[(x1p, D_MODEL, 0), (dp, D_MODEL, 0)], exs=[g2p, sc1, sh1], pars=[w["norm1_w"]],
                                row_outs=[(D_MODEL, F32, D_MODEL, 0), (D_MODEL, BF16, D_MODEL, 0)])
    sv["x"], sv["h"] = x, h
    proj = _mm(host, got, "gather", h, w["w_in"], "nn", BF16, "mm_in")
    sv["proj"] = proj
    w = _merge_late(w, late, got, "mm_in")

    def body(rv, ev, pv):
        outs = _f_mixprep(rv[0], rv[1], rv[2], rv[3], rv[4], pv[0], pv[1])
        return list(outs), [], []
    (qa, ka, va, cqn, ckvn, kr), _, _ = _rowcall(
        body, name="f_mixprep", T=T, S=S, tr=tr,
        rows=[(proj, P_MIX, 0), (ca, LANES, 0), (sa, LANES, 0), (cb, LANES, 0), (sb, LANES, 0)],
        pars=[w["b_q_norm_w"], w["b_kv_norm_w"]],
        row_outs=[(768, BF16, 768, 0), (256, BF16, 256, 0), (256, BF16, 256, 0), (384, BF16, 384, 0),
                  (256, BF16, 256, 0), (LANES, F32, LANES, 0)])
    sv.update(qa=qa, ka=ka, va=va, cqn=cqn, ckvn=ckvn)
    q = _matmul(cqn, w["b_w_uq"], "nn", BF16, "mm_uq")
    kv = _matmul(ckvn, w["b_w_ukv"], "nn", BF16, "mm_ukv")

    def body(rv, ev, pv):
        Q, K, V = _f_mlaprep(rv[0], rv[1], rv[2], rv[3], rv[4])
        return [Q, K, V], [], []
    (Q, K, V), _, _ = _rowcall(
        body, name="f_mlaprep", T=T, S=S, tr=tr,
        rows=[(q, 768, 0), (kv, 1536, 0), (kr, LANES, 0), (cb, LANES, 0), (sb, LANES, 0)],
        row_outs=[(768, BF16, 768, 0)] * 3)
    sv.update(Q=Q, K=K, V=V)
    ta, tb, band = _attn_tiles(S)
    oa, lse_a = _hosted(_attn_fwd, host, got, "gather", "attn_a_fwd", qa, ka, va, w["sinks"], B=B, S=S, HQ=A_Q_HEADS,
                        HK=A_KV_HEADS, window=A_WINDOW, scale=HEAD_DIM ** -0.5, tq=ta, band=None)
    ob, lse_b = _hosted(_attn_fwd, host, got, "gather", "attn_b_fwd", Q, K, V, None, B=B, S=S, HQ=B_HEADS, HK=B_HEADS,
                        window=None, scale=(B_NOPE + B_ROPE) ** -0.5, tq=tb, band=band)
    sv.update(oa=oa, lse_a=lse_a, ob=ob, lse_b=lse_b)
    w = _merge_late(w, late, got, "attn_a_fwd")
    ts = _sgu_rows(S)

    def body(rv, ev, pv):
        outs = [_f_sgu(rv[0][r:r + C_CHUNK], rv[1][r:r + C_CHUNK], pv[0], pv[1], pv[2], pv[3])
                for r in range(0, ts, C_CHUNK)]
        return [jnp.concatenate(outs, axis=0)], [], []
    (yc,), _, _ = _rowcall(body, name="f_sgu", T=T, S=S, tr=ts,
                           rows=[(proj, C_WIDTH, P_CU // C_WIDTH), (proj, C_WIDTH, P_CV // C_WIDTH)],
                           pars=[w["c_ln_w"], w["c_ln_b"], w["c_w_s"], w["c_b_col"]],
                           row_outs=[(C_WIDTH, F32, C_WIDTH, 0)])
    sv["yc"] = yc

    def body(rv, ev, pv):
        return [_f_outnorm(rv[0], rv[1], rv[2], pv[0])], [], []
    (y,), _, _ = _rowcall(body, name="f_outnorm", T=T, S=S, tr=tr,
                          rows=[(oa, 768, 0), (ob, 768, 0), (yc, C_WIDTH, 0)], pars=[w["out_norm_w"]],
                          row_outs=[(D_MODEL, BF16, D_MODEL, 0)])
    sv["y"] = y
    o = _mm(host, got, "gather", y, w["w_out"], "nn", BF16, "mm_out")
    sv["o"] = o

    def body(rv, ev, pv):
        xn, hh = _f_resid_norm_mod(rv[0], rv[1], ev[0], pv[0], ev[1], ev[2])
        return [xn, hh], [], []
    (x1, h2), _, _ = _rowcall(body, name="f_resid_norm_mod2", T=T, S=S, tr=tr,
                              rows=[(x, D_MODEL, 0), (o, D_MODEL, 0)], exs=[g1, sc2, sh2], pars=[w["norm2_w"]],
                              row_outs=[(D_MODEL, F32, D_MODEL, 0), (D_MODEL, BF16, D_MODEL, 0)])
    sv["h2"] = h2
    gu = _mm(host, got, "gather", h2, w["w_gate_up"], "nn", BF16, "mm_gate_up", b_chunks=True)
    sv["gu"] = gu

    def body(rv, ev, pv):
        return [_f_swiglu(rv[0], rv[1])], [], []
    (act,), _, _ = _rowcall(body, name="f_swiglu", T=T, S=S, tr=_row_tile(S, wide=True),
                            rows=[(gu, FFN_HIDDEN, 0), (gu, FFN_HIDDEN, 1)], row_outs=[(FFN_HIDDEN, BF16, FFN_HIDDEN, 0)])
    sv["act"] = act
    d = _mm(host, got, "gather", act, w["w_down"], "nn", BF16, "mm_down")
    sv["x1"], sv["d"], sv["w"] = x1, d, w
    return (x1, d, g2), sv, got


def _final(x1, d, g2, fw, target, B, S):
    T = B * S
    tr = _row_tile(S)

    def loss_fn(x1v, dv, gv, wv, tv):
        yf = _rms(x1v + gv * dv, wv)
        return 0.5 * jnp.sum(jnp.mean(jnp.square(yf - tv), axis=-1))

    def body(rv, ev, pv):
        x1v, dv, tv = rv
        val, vjp = jax.vjp(lambda a, b_, g, ww: loss_fn(a, b_, g, ww, tv), x1v, dv, ev[0], pv[0])
        dx1, dd, dg, dw = vjp(jnp.ones((), F32))
        return [dx1, dd], [dg], [dw, jnp.full((1, LANES), val, F32)]
    (dx1, dd), (dg2,), (dfw, loss) = _rowcall(
        body, name="final_loss", T=T, S=S, tr=tr, rows=[(x1, D_MODEL, 0), (d, D_MODEL, 0), (target, D_MODEL, 0)],
        exs=[g2], pars=[fw], row_outs=[(D_MODEL, F32, D_MODEL, 0), (D_MODEL, BF16, D_MODEL, 0)],
        ex_outs=[D_MODEL], par_outs=[(1, D_MODEL), (1, LANES)])
    return loss, dx1, dd, dg2, dfw


def _layer_bwd(sv, prev, mods, tabs, dx1, dd, B, S, host=None, own_early=False):
    w = sv["w"]
    T = B * S
    tr = _row_tile(S)
    sh1, sc1, g1, sh2, sc2, g2 = mods
    ca, sa, cb, sb = tabs
    gr = {}
    got = {}
    dact = _mm(host, got, "a2a", dd, w["w_down"], "nt", BF16, "mm_down_dx")
    gr["w_down"] = _matmul(sv["act"], dd, "tn", BF16, "mm_down_dw")

    def body(rv, ev, pv):
        _, vjp = jax.vjp(_f_swiglu, rv[0], rv[1])
        dgate, dup = vjp(rv[2])
        return [[dgate, dup]], [], []
    (dgu,), _, _ = _rowcall(body, name="b_swiglu", T=T, S=S, tr=_row_tile(S, wide=True),
                            rows=[(sv["gu"], FFN_HIDDEN, 0), (sv["gu"], FFN_HIDDEN, 1), (dact, FFN_HIDDEN, 0)],
                            row_outs=[(2 * FFN_HIDDEN, BF16, 2 * FFN_HIDDEN, 0)])
    dh2 = _mm(host, got, "a2a", dgu, w["w_gate_up"], "nt", BF16, "mm_gate_up_dx", b_chunks=True)
    gr["w_gate_up"] = _mm(host, got, "a2a", sv["h2"], dgu, "tn", BF16, "mm_gate_up_dw", out_chunks=True)

    def body(rv, ev, pv):
        xa, delta, dh, dxn = rv
        _, vjp = jax.vjp(_f_resid_norm_mod, xa, delta, ev[0], pv[0], ev[1], ev[2])
        dxa, ddelta, dg, dw, dsc, dsh = vjp((dxn, dh))
        return [dxa, ddelta], [dg, dsc, dsh], [dw]
    (dx, do), (dg1, dsc2, dsh2), (gr["norm2_w"],) = _rowcall(
        body, name="b_resid_norm_mod2", T=T, S=S, tr=tr,
        rows=[(sv["x"], D_MODEL, 0), (sv["o"], D_MODEL, 0), (dh2, D_MODEL, 0), (dx1, D_MODEL, 0)],
        exs=[g1, sc2, sh2], pars=[w["norm2_w"]],
        row_outs=[(D_MODEL, F32, D_MODEL, 0), (D_MODEL, BF16, D_MODEL, 0)], ex_outs=[D_MODEL] * 3,
        par_outs=[(1, D_MODEL)])
    dy = _matmul(do, w["w_out"], "nt", BF16, "mm_out_dx")
    gr["w_out"] = _matmul(sv["y"], do, "tn", BF16, "mm_out_dw")

    def body(rv, ev, pv):
        _, vjp = jax.vjp(_f_outnorm, rv[0], rv[1], rv[2], pv[0])
        doa, dob, dyc, dgw = vjp(rv[3])
        return [doa, dob, dyc], [], [dgw]
    (doa, dob, dyc), _, (gr["out_norm_w"],) = _rowcall(
        body, name="b_outnorm", T=T, S=S, tr=tr,
        rows=[(sv["oa"], 768, 0), (sv["ob"], 768, 0), (sv["yc"], C_WIDTH, 0), (dy, D_MODEL, 0)], pars=[w["out_norm_w"]],
        row_outs=[(768, BF16, 768, 0), (768, BF16, 768, 0), (C_WIDTH, F32, C_WIDTH, 0)], par_outs=[(1, D_MODEL)])

    ta, tb, band = _attn_tiles(S)
    if own_early:
        host = dict(host)
        host["attn_b_bwd"] = list(host.get("attn_b_bwd", ())) + [
            gr["w_gate_up"], gr["w_down"].reshape(N_CHIPS, FFN_HIDDEN // N_CHIPS, D_MODEL)]
        host["attn_a_bwd"] = list(host.get("attn_a_bwd", ())) + [
            gr["w_out"].reshape(N_CHIPS, D_MODEL // N_CHIPS, D_MODEL)]
    dQ, dK, dV, _ = _hosted(_attn_bwd, host, got, "a2a", "attn_b_bwd", sv["Q"], sv["K"], sv["V"], sv["ob"], sv["lse_b"],
                            dob, None, B=B, S=S, HQ=B_HEADS, HK=B_HEADS, window=None,
                            scale=(B_NOPE + B_ROPE) ** -0.5, tq=tb, band=band)
    dqa, dka, dva, dsink = _hosted(_attn_bwd, host, got, "a2a", "attn_a_bwd", sv["qa"], sv["ka"], sv["va"], sv["oa"],
                                   sv["lse_a"], doa, w["sinks"], B=B, S=S, HQ=A_Q_HEADS, HK=A_KV_HEADS,
                                   window=A_WINDOW, scale=HEAD_DIM ** -0.5, tq=ta, band=None)
    gr["sinks"] = dsink

    def body(rv, ev, pv):
        dQv, dKv, dVv, cbv, sbv = rv
        dq = [_rope_bwd((cbv, sbv), p)[0] for p in _heads(dQv, B_HEADS)]
        dkr = None
        for p in _heads(dKv, B_HEADS):
            dkr = p if dkr is None else dkr + p
        return [dq, [dKv, dVv], dkr], [], []
    (dq, dkv, dkr), _, _ = _rowcall(
        body, name="b_mlaprep", T=T, S=S, tr=tr,
        rows=[(dQ, 768, 0), (dK, 768, 0), (dV, 768, 0), (cb, LANES, 0), (sb, LANES, 0)],
        row_outs=[(768, BF16, 768, 0), (1536, BF16, 1536, 0), (LANES, F32, LANES, 0)])
    dcqn = _matmul(dq, w["b_w_uq"], "nt", BF16, "mm_uq_dx")
    gr["b_w_uq"] = _matmul(sv["cqn"], dq, "tn", BF16, "mm_uq_dw")
    dckvn = _matmul(dkv, w["b_w_ukv"], "nt", BF16, "mm_ukv_dx")
    gr["b_w_ukv"] = _matmul(sv["ckvn"], dkv, "tn", BF16, "mm_ukv_dw")

    def body(rv, ev, pv):
        proj, cav, sav, cbv, sbv, dqa_, dka_, dva_, dcqn_, dckvn_, dkr_ = rv
        _, vjp = jax.vjp(lambda p, a, b_: _f_mixprep(p, cav, sav, cbv, sbv, a, b_), proj, pv[0], pv[1])
        dproj, dqn, dkvn = vjp((dqa_, dka_, dva_, dcqn_, dckvn_, dkr_))
        return [[dproj, jnp.zeros((dproj.shape[0], P_CU - P_MIX), F32)]], [], [dqn, dkvn]
    (dproj,), _, (gr["b_q_norm_w"], gr["b_kv_norm_w"]) = _rowcall(
        body, name="b_mixprep", T=T, S=S, tr=tr,
        rows=[(sv["proj"], P_MIX, 0), (ca, LANES, 0), (sa, LANES, 0), (cb, LANES, 0), (sb, LANES, 0),
              (dqa, 768, 0), (dka, 256, 0), (dva, 256, 0), (dcqn, 384, 0), (dckvn, 256, 0), (dkr, LANES, 0)],
        pars=[w["b_q_norm_w"], w["b_kv_norm_w"]], row_outs=[(P_END, BF16, P_CU, 0)],
        par_outs=[(1, B_Q_RANK), (1, B_KV_RANK)])

    ts = _sgu_rows(S)

    def body(rv, ev, pv):
        cu, cv, dycv, _ = rv
        dcus, dcvs, acc = [], [], None
        for r in range(0, ts, C_CHUNK):
            _, vjp = jax.vjp(_f_sgu, cu[r:r + C_CHUNK], cv[r:r + C_CHUNK], pv[0], pv[1], pv[2], pv[3])
            dcu, dcv, *dpar = vjp(dycv[r:r + C_CHUNK])
            dcus.append(dcu)
            dcvs.append(dcv)
            acc = dpar if acc is None else [a + b_ for a, b_ in zip(acc, dpar)]
        return [[jnp.concatenate(dcus, axis=0), jnp.concatenate(dcvs, axis=0)]], [], acc
    (dproj,), _, (gr["c_ln_w"], gr["c_ln_b"], gr["c_w_s"], gr["c_b_col"]) = _rowcall(
        body, name="b_sgu", T=T, S=S, tr=ts,
        rows=[(sv["proj"], C_WIDTH, P_CU // C_WIDTH), (sv["proj"], C_WIDTH, P_CV // C_WIDTH), (dyc, C_WIDTH, 0),
              (dproj, 2 * C_WIDTH, P_CU // (2 * C_WIDTH))],
        pars=[w["c_ln_w"], w["c_ln_b"], w["c_w_s"], w["c_b_col"]],
        row_outs=[(P_END, BF16, 2 * C_WIDTH, P_CU // (2 * C_WIDTH))],
        par_outs=[(1, C_WIDTH), (1, C_WIDTH), (C_GROUPS, C_CHUNK, C_CHUNK), (C_GROUPS, C_CHUNK, 1)],
        aliases={3: 0})
    dh = _mm(host, got, "a2a", dproj, w["w_in"], "nt", BF16, "mm_in_dx")
    gr["w_in"] = _matmul(sv["h"], dproj, "tn", BF16, "mm_in_dw")

    if prev is None:
        def body(rv, ev, pv):
            xv, dhv, dxd = rv
            _, vjp = jax.vjp(_f_norm_mod, xv, pv[0], ev[0], ev[1])
            dxa, dw, dsc, dsh = vjp(dhv)
            return [dxa + dxd], [dsc, dsh], [dw]
        (dxin,), (dsc1, dsh1), (gr["norm1_w"],) = _rowcall(
            body, name="b_norm_mod", T=T, S=S, tr=tr, rows=[(sv["x"], D_MODEL, 0), (dh, D_MODEL, 0), (dx, D_MODEL, 0)],
            exs=[sc1, sh1], pars=[w["norm1_w"]], row_outs=[(D_MODEL, F32, D_MODEL, 0)], ex_outs=[D_MODEL] * 2,
            par_outs=[(1, D_MODEL)])
        nxt = (dxin, None, None)
    else:
        x1p, dp, g2p = prev

        def body(rv, ev, pv):
            xa, delta, dhv, dxn = rv
            _, vjp = jax.vjp(_f_resid_norm_mod, xa, delta, ev[0], pv[0], ev[1], ev[2])
            dxa, ddelta, dg, dw, dsc, dsh = vjp((dxn, dhv))
            return [dxa, ddelta], [dg, dsc, dsh], [dw]
        (dx1p, ddp), (dg2p, dsc1, dsh1), (gr["norm1_w"],) = _rowcall(
            body, name="b_resid_norm_mod1", T=T, S=S, tr=tr,
            rows=[(x1p, D_MODEL, 0), (dp, D_MODEL, 0), (dh, D_MODEL, 0), (dx, D_MODEL, 0)],
            exs=[g2p, sc1, sh1], pars=[w["norm1_w"]],
            row_outs=[(D_MODEL, F32, D_MODEL, 0), (D_MODEL, BF16, D_MODEL, 0)], ex_outs=[D_MODEL] * 3,
            par_outs=[(1, D_MODEL)])
        nxt = (dx1p, ddp, dg2p)
    return gr, (dsh1, dsc1, dg1, dsh2, dsc2), nxt, got


def _lane_table(lanes_neg, lanes_pos, inv):
    freq = np.zeros((LANES,), np.int64) - 1
    sign = np.zeros((1, LANES), np.float32)
    n = len(lanes_neg)
    freq[lanes_neg] = np.arange(n)
    freq[lanes_pos] = np.arange(n)
    sign[0, lanes_neg] = -1.0
    sign[0, lanes_pos] = 1.0
    return _select_axis(inv, freq, 0).reshape(1, LANES), jnp.asarray(sign)


SHARDED = ("ada_w", "w_in", "b_w_uq", "b_w_ukv", "w_out", "w_gate_up", "w_down")
ROW_SHARDED = ("w_out", "w_down")
SMALL = ("ada_b", "norm1_w", "a_sinks", "b_q_norm_w", "b_kv_norm_w", "c_ln_w", "c_ln_b", "c_w_s", "c_b_s",
         "out_norm_w", "norm2_w", "final_norm_w")
FWD_HOST = {"attn_b_fwd": ("w_gate_up", "w_down"), "attn_a_fwd": ("w_in", "w_out", "b_w_uq", "b_w_ukv")}
FWD_HOST_FIRST = {"attn_b_fwd": ("w_gate_up", "w_down"), "mm_gate_up": ("w_in", "w_out", "b_w_uq", "b_w_ukv")}
BWD_HOST = {"attn_a_bwd": ("w_in", "w_out", "b_w_uq", "b_w_ukv"), "attn_b_bwd": ("w_gate_up", "w_down")}
FIRST_LATE = {"mm_in": ("b_w_uq", "b_w_ukv", "w_out"), "attn_a_fwd": ("w_gate_up", "w_down")}
EXPOSED = ("w_in", "b_w_uq", "b_w_ukv")
OWN_LAYER = ("w_gate_up", "w_down")
EXCHANGED = SHARDED[1:]
N_ADA = N_MOD * D_MODEL // N_CHIPS
assert not set(FIRST_LATE) & set(FWD_HOST_FIRST)


def _from_host(table, got):
    return {k: got[name][i] for name, ks in table.items() for i, k in enumerate(ks)}


def _to_host(table, arrays):
    return {name: [arrays[k] for k in ks] for name, ks in table.items()}


def _join_cols(g):
    return jnp.concatenate([g[j] for j in range(N_CHIPS)], axis=1)


def _split_cols(g):
    n = g.shape[1] // N_CHIPS
    return jnp.stack([g[:, j * n:(j + 1) * n] for j in range(N_CHIPS)])


def _layer_weights(G, small, l):
    D = D_MODEL
    return {
        **_large_operands(G),
        "norm1_w": small["norm1_w"][l].reshape(1, D),
        "sinks": jnp.broadcast_to(small["a_sinks"][l].reshape(A_Q_HEADS, 1, 1), (A_Q_HEADS, 1, LANES)),
        "b_q_norm_w": small["b_q_norm_w"][l].reshape(1, B_Q_RANK),
        "b_kv_norm_w": small["b_kv_norm_w"][l].reshape(1, B_KV_RANK),
        "c_ln_w": small["c_ln_w"][l].reshape(1, C_WIDTH), "c_ln_b": small["c_ln_b"][l].reshape(1, C_WIDTH),
        "c_w_s": small["c_w_s"][l], "c_b_col": small["c_b_s"][l].reshape(C_GROUPS, C_CHUNK, 1),
        "out_norm_w": small["out_norm_w"][l].reshape(1, D), "norm2_w": small["norm2_w"][l].reshape(1, D),
    }


def _large_operands(G):
    D = D_MODEL
    make = {
        "w_in": lambda g: _pad_axis(_join_cols(g), _map_w_in(), 1),
        "b_w_uq": lambda g: _pad_axis(_join_cols(g), _map_w_uq(), 1),
        "b_w_ukv": lambda g: _pad_axis(_join_cols(g), _map_w_ukv(), 1),
        "w_out": lambda g: g.reshape(D, D), "w_gate_up": lambda g: g, "w_down": lambda g: g.reshape(FFN_HIDDEN, D),
    }
    return {k: make[k](g) for k, g in G.items()}


def _send_buffers(gr):
    D = D_MODEL
    return {
        "w_gate_up": gr["w_gate_up"], "w_down": gr["w_down"].reshape(N_CHIPS, FFN_HIDDEN // N_CHIPS, D),
        "w_out": gr["w_out"].reshape(N_CHIPS, D // N_CHIPS, D),
        "w_in": _split_cols(_unpad_axis(gr["w_in"], _map_w_in(), IN_COLS, 1)),
        "b_w_uq": _split_cols(_unpad_axis(gr["b_w_uq"], _map_w_uq(), B_HEADS * (B_NOPE + B_ROPE), 1)),
        "b_w_ukv": _split_cols(_unpad_axis(gr["b_w_ukv"], _map_w_ukv(), B_HEADS * (B_NOPE + B_V), 1)),
    }


def _small_grads(gr, B):
    D = D_MODEL
    return {
        "norm1_w": gr["norm1_w"].reshape(D),
        "a_sinks": gr["sinks"][:, 0, 0].reshape(B, A_Q_HEADS).sum(axis=0),
        "b_q_norm_w": gr["b_q_norm_w"].reshape(B_Q_RANK), "b_kv_norm_w": gr["b_kv_norm_w"].reshape(B_KV_RANK),
        "c_ln_w": gr["c_ln_w"].reshape(C_WIDTH), "c_ln_b": gr["c_ln_b"].reshape(C_WIDTH), "c_w_s": gr["c_w_s"],
        "c_b_s": gr["c_b_col"].reshape(C_GROUPS, C_CHUNK),
        "out_norm_w": gr["out_norm_w"].reshape(D), "norm2_w": gr["norm2_w"].reshape(D),
    }


def _step(x, c, positions, target, small, shard_of, ada=None, gathered=None):
    dist = gathered is None
    B, S, D = x.shape
    T = B * S
    xt = x.reshape(T, D)
    tgt = target.reshape(T, D)
    pos_col = positions.astype(F32).reshape(T, 1)
    inv_a = 1.0 / (ROPE_THETA ** (jnp.arange(0, HEAD_DIM, 2, dtype=F32) / HEAD_DIM))
    inv_b = 1.0 / (ROPE_THETA ** (jnp.arange(0, B_ROPE, 2, dtype=F32) / B_ROPE))
    fa, sga = _lane_table(np.arange(32), 64 + np.arange(32), inv_a)
    fb, sgb = _lane_table(48 + np.arange(16), 112 + np.arange(16), inv_b)
    ca, sa = _rope_tables(pos_col, fa, sga, jnp.abs(sga), "rope_a")
    cb, sb = _rope_tables(pos_col, fb, sgb, jnp.ones_like(sgb), "rope_b")
    tabs = (ca, sa, cb, sb)
    ada_b = small["ada_b"]
    if dist:
        assert N_CHIPS * B == 8
        me = 2 * lax.axis_index("x") + lax.axis_index("y")
        first = shard_of(0)
        c_all, w_in_first = _xchg_call("gather", [c, first["w_in"]], "gather_first")
        c8 = c_all.reshape(N_CHIPS * B, D)
        mine = lax.dynamic_slice_in_dim(ada_b, me * N_ADA, N_ADA, axis=1).reshape(DEPTH, 1, N_ADA)
        part = _ada_fwd(c8, ada, mine)
        part = part.reshape(DEPTH, N_CHIPS, B, N_ADA).transpose(1, 0, 2, 3).reshape(N_CHIPS, DEPTH * B, N_ADA)
        (back,) = _xchg_call("a2a", [part], "mod_exchange")
        mod_all = jnp.concatenate([back[j].reshape(DEPTH, B, N_ADA) for j in range(N_CHIPS)], axis=-1)
        G = {"w_in": w_in_first}
    else:
        c8 = jnp.zeros((8, D), F32).at[:B].set(c)
        cols = [(jnp.stack([gathered[l]["ada_w"][j] for l in range(DEPTH)]),
                 ada_b[:, j * N_ADA:(j + 1) * N_ADA].reshape(DEPTH, 1, N_ADA)) for j in range(N_CHIPS)]
        mod_all = jnp.concatenate([_ada_fwd(c8, wj, bj)[:, :B] for wj, bj in cols], axis=-1)
        G = {k: v for k, v in gathered[0].items() if k != "ada_w"}
    saved, prevs, modss = [], [], []
    prev = None
    for l in range(DEPTH):
        w = _layer_weights(G, small, l)
        mods = tuple(mod_all[l, :, i * D:(i + 1) * D].reshape(B, 1, D) for i in range(N_MOD))
        more = l + 1 < DEPTH
        table = FWD_HOST_FIRST if l == 0 else FWD_HOST
        host = _to_host(table, shard_of(l + 1)) if dist and more else {}
        late = None
        if dist and l == 0:
            late = FIRST_LATE
            for name, ks in late.items():
                host[name] = [first[k] for k in ks] + host.get(name, [])
        prevs.append(prev)
        modss.append(mods)
        prev, sv, got = _layer_fwd(xt, prev, mods, w, tabs, B, S, host, late)
        saved.append(sv)
        if more:
            G = _from_host(table, got) if dist else {k: v for k, v in gathered[l + 1].items() if k != "ada_w"}
    x1, d, g2 = prev
    loss, dx1, dd, dg2, dfw = _final(x1, d, g2, small["final_norm_w"].reshape(1, D), tgt, B, S)

    landed = [None] * DEPTH
    smalls = [None] * DEPTH
    dmods = [None] * DEPTH
    pending = None
    for l in reversed(range(DEPTH)):
        host = _to_host(BWD_HOST, pending) if dist and pending is not None else None
        early = dist and l == 0 and host is not None
        gr, (dsh1, dsc1, dg1, dsh2, dsc2), nxt, got = _layer_bwd(saved[l], prevs[l], modss[l], tabs, dx1, dd, B, S, host, early)
        if pending is not None:
            landed[l + 1] = _from_host(BWD_HOST, got) if dist else pending
        dmods[l] = jnp.concatenate([dsh1, dsc1, dg1, dsh2, dsc2, dg2], axis=-1).reshape(B, N_MOD * D)
        pending = _send_buffers(gr)
        smalls[l] = _small_grads(gr, B)
        dx1, dd, dg2 = nxt
    dmod_all = jnp.stack(dmods)
    by_chip = [dmod_all[:, :, j * N_ADA:(j + 1) * N_ADA] for j in range(N_CHIPS)]
    if dist:
        last = EXPOSED if early else EXCHANGED
        send = jnp.stack([p.reshape(DEPTH * B, N_ADA) for p in by_chip])
        *res, back = _xchg_call("a2a", [pending[k] for k in last] + [send], "grad_exchange_last")
        landed[0] = dict(zip(last, res))
        if early:
            landed[0].update(zip(OWN_LAYER, got["attn_b_bwd"][-len(OWN_LAYER):]))
            landed[0]["w_out"] = got["attn_a_bwd"][-1]
        dmod8 = back.reshape(N_CHIPS, DEPTH, B, N_ADA).transpose(1, 0, 2, 3).reshape(DEPTH, N_CHIPS * B, N_ADA)
        ada_g = _ada_bwd(c8, dmod8)
    else:
        landed[0] = pending
        ada_g = jnp.stack([_ada_bwd(c8, jnp.zeros((DEPTH, 8, N_ADA), F32).at[:, :B].set(p)) for p in by_chip])
    small_g = {k: jnp.stack([smalls[l][k] for l in range(DEPTH)]) for k in SMALL if k not in ("final_norm_w", "ada_b")}
    small_g["ada_b"] = _sum_examples(dmod_all).reshape(DEPTH, N_MOD * D)
    small_g["final_norm_w"] = dfw.reshape(D)
    return loss[0, 0], dx1.reshape(B, S, D), landed, small_g, ada_g


def _pack(arrs, cols, mult):
    flat = jnp.concatenate([a.reshape(-1) for a in arrs])
    n = flat.shape[0]
    rows = -(-n // cols)
    rows = -(-rows // mult) * mult
    return jnp.pad(flat, (0, rows * cols - n)).reshape(rows, cols)


def _unpack(blob, shapes):
    flat = blob.reshape(-1)
    out, off = [], 0
    for s in shapes:
        n = int(np.prod(s))
        out.append(flat[off:off + n].reshape(s))
        off += n
    return out


def kernel(x, c, positions, ada_w, ada_b, norm1_w, w_in, a_sinks, b_q_norm_w, b_w_uq, b_kv_norm_w, b_w_ukv, c_ln_w, c_ln_b, c_w_s, c_b_s, out_norm_w, w_out, norm2_w, w_gate_up, w_down, final_norm_w, loss_target, m_ada_w, m_ada_b, m_norm1_w, m_w_in, m_a_sinks, m_b_q_norm_w, m_b_w_uq, m_b_kv_norm_w, m_b_w_ukv, m_c_ln_w, m_c_ln_b, m_c_w_s, m_c_b_s, m_out_norm_w, m_w_out, m_norm2_w, m_w_gate_up, m_w_down, m_final_norm_w, v_ada_w, v_ada_b, v_norm1_w, v_w_in, v_a_sinks, v_b_q_norm_w, v_b_w_uq, v_b_kv_norm_w, v_b_w_ukv, v_c_ln_w, v_c_ln_b, v_c_w_s, v_c_b_s, v_out_norm_w, v_w_out, v_norm2_w, v_w_gate_up, v_w_down, v_final_norm_w):
    names = ("ada_w", "ada_b", "norm1_w", "w_in", "a_sinks", "b_q_norm_w", "b_w_uq", "b_kv_norm_w", "b_w_ukv", "c_ln_w",
             "c_ln_b", "c_w_s", "c_b_s", "out_norm_w", "w_out", "norm2_w", "w_gate_up", "w_down", "final_norm_w")
    ws = dict(zip(names, (ada_w, ada_b, norm1_w, w_in, a_sinks, b_q_norm_w, b_w_uq, b_kv_norm_w, b_w_ukv, c_ln_w, c_ln_b,
                          c_w_s, c_b_s, out_norm_w, w_out, norm2_w, w_gate_up, w_down, final_norm_w)))
    ms = dict(zip(names, (m_ada_w, m_ada_b, m_norm1_w, m_w_in, m_a_sinks, m_b_q_norm_w, m_b_w_uq, m_b_kv_norm_w, m_b_w_ukv,
                          m_c_ln_w, m_c_ln_b, m_c_w_s, m_c_b_s, m_out_norm_w, m_w_out, m_norm2_w, m_w_gate_up, m_w_down,
                          m_final_norm_w)))
    vs = dict(zip(names, (v_ada_w, v_ada_b, v_norm1_w, v_w_in, v_a_sinks, v_b_q_norm_w, v_b_w_uq, v_b_kv_norm_w, v_b_w_ukv,
                          v_c_ln_w, v_c_ln_b, v_c_w_s, v_c_b_s, v_out_norm_w, v_w_out, v_norm2_w, v_w_gate_up, v_w_down,
                          v_final_norm_w)))
    shards = {k: ws[k].astype(BF16) for k in SHARDED}
    loss_local, grad_x, landed, gsmall, ada_g = _step(
        x, c, positions, loss_target, {k: ws[k] for k in SMALL}, lambda l: {k: shards[k][l] for k in EXCHANGED},
        ada=shards["ada_w"])

    mine = {k: jnp.stack([_sum_slots(landed[l][k], "grad_sum_" + k) for l in range(DEPTH)]) for k in EXCHANGED}
    mine["ada_w"] = ada_g
    small_shapes = [ws[k].shape for k in SMALL]
    sblob = _pack([gsmall[k] for k in SMALL] + [loss_local.reshape(1)], LANES, 8)
    sred, swapped = _all_reduce_small(sblob, "small_all_reduce", swaps=[mine[k] for k in SHARDED])
    theirs = dict(zip(SHARDED, swapped))
    grads, delta, new_m, new_v = {}, {}, {}, {}
    for k in SHARDED:
        shp = ws[k].shape
        two = (shp[0] * shp[1], shp[2])
        g, dlt, nm, nv = _adamw(ws[k].reshape(two), mine[k].reshape(two), theirs[k].reshape(two), ms[k].reshape(two),
                                vs[k].reshape(two), "adamw_" + k)
        grads[k], delta[k], new_m[k], new_v[k] = g.reshape(shp), dlt.reshape(shp), nm.reshape(shp), nv.reshape(shp)

    svals = _unpack(sred, small_shapes + [(1,)])
    loss = svals[-1].reshape(())
    pw = _pack([ws[k] for k in SMALL], LANES, 8)
    pg = _pack(svals[:-1], LANES, 8)
    pm = _pack([ms[k] for k in SMALL], LANES, 8)
    pv = _pack([vs[k] for k in SMALL], LANES, 8)
    g, dlt, nm, nv = _adamw(pw, pg, None, pm, pv, "adamw_small")
    for k, a, b_, c_, d_ in zip(SMALL, _unpack(g, small_shapes), _unpack(dlt, small_shapes), _unpack(nm, small_shapes),
                                _unpack(nv, small_shapes)):
        grads[k], delta[k], new_m[k], new_v[k] = a, b_, c_, d_

    return (loss, grad_x, *[grads[k] for k in names], *[delta[k] for k in names], *[new_m[k] for k in names],
            *[new_v[k] for k in names])
```

```python
import functools
import math

import numpy as np
import jax
import jax.numpy as jnp
from jax import lax
from jax.experimental import pallas as pl
from jax.experimental.pallas import tpu as pltpu

F32 = jnp.float32
BF16 = jnp.bfloat16

D_MODEL = 1024
DEPTH = 4
HEAD_DIM = 64
ROPE_THETA = 10000.0
NORM_EPS = 1e-6
NEG_INF = -1e30
LOG2_E = math.log2(math.e)
A_Q_HEADS = 6
A_KV_HEADS = 2
A_WINDOW = 128
B_HEADS = 6
B_Q_RANK = 384
B_KV_RANK = 256
B_NOPE = 64
B_ROPE = 32
B_V = 64
C_GROUPS = 4
C_GROUP_DIM = 64
C_WIDTH = 256
C_CHUNK = 128
IN_COLS = 1824
FFN_HIDDEN = 2816
N_MOD = 6
ADAM_LR = 0.001
ADAM_B1 = 0.9
ADAM_B2 = 0.999
ADAM_EPS = 1e-08
ADAM_WD = 0.01
ADAM_STEP = 10

LANES = 128
VMEM_LIMIT = 56 * 1024 * 1024
N_CHIPS = 4
WINDOW_SUB = 256

P_KR, P_AQ, P_AK, P_AV, P_CQ, P_CKV, P_MIX, P_CU, P_CV, P_END = 0, 128, 512, 640, 768, 1152, 1408, 1536, 1792, 2048


def _map_w_in():
    idx = -np.ones(P_END, np.int64)
    idx[P_AQ:P_AQ + 384] = np.arange(384)
    idx[P_AK:P_AK + 128] = 384 + np.arange(128)
    idx[P_AV:P_AV + 128] = 512 + np.arange(128)
    idx[P_CQ:P_CQ + 384] = 640 + np.arange(384)
    idx[P_CKV:P_CKV + 256] = 1024 + np.arange(256)
    idx[P_KR + 48 + np.arange(16)] = 1280 + np.arange(16)
    idx[P_KR + 112 + np.arange(16)] = 1296 + np.arange(16)
    idx[P_CU:P_CU + 256] = 1312 + np.arange(256)
    idx[P_CV:P_CV + 256] = 1568 + np.arange(256)
    return idx


def _map_w_uq():
    idx = -np.ones(B_HEADS * LANES, np.int64)
    for h in range(B_HEADS):
        b = h * (B_NOPE + B_ROPE)
        idx[h * LANES + np.arange(48)] = b + np.arange(48)
        idx[h * LANES + 48 + np.arange(16)] = b + 64 + np.arange(16)
        idx[h * LANES + 64 + np.arange(16)] = b + 48 + np.arange(16)
        idx[h * LANES + 112 + np.arange(16)] = b + 80 + np.arange(16)
    return idx


def _map_w_ukv():
    idx = -np.ones(2 * B_HEADS * LANES, np.int64)
    for h in range(B_HEADS):
        b = h * (B_NOPE + B_V)
        idx[h * LANES + np.arange(48)] = b + np.arange(48)
        idx[h * LANES + 64 + np.arange(16)] = b + 48 + np.arange(16)
        idx[B_HEADS * LANES + h * LANES + np.arange(B_V)] = b + B_NOPE + np.arange(B_V)
    return idx


def _inverse(idx, n):
    inv = np.zeros(n, np.int64)
    pos = np.nonzero(idx >= 0)[0]
    inv[idx[pos]] = pos
    return inv


def _runs(idx):
    runs, i, n = [], 0, len(idx)
    while i < n:
        j = i + 1
        while j < n and ((idx[i] < 0 and idx[j] < 0) or (idx[i] >= 0 and idx[j] == idx[i] + (j - i))):
            j += 1
        runs.append((int(idx[i]), j - i))
        i = j
    return runs


def _select_axis(w, idx, axis):
    pieces = []
    for start, length in _runs(idx):
        if start < 0:
            shape = list(w.shape)
            shape[axis] = length
            pieces.append(jnp.zeros(shape, w.dtype))
        else:
            pieces.append(lax.slice_in_dim(w, start, start + length, axis=axis))
    return jnp.concatenate(pieces, axis=axis)


def _pad_axis(w, idx, axis):
    return _select_axis(w, idx, axis)


def _unpad_axis(g, idx, n, axis):
    return _select_axis(g, _inverse(idx, n), axis)


def _params(sem):
    return pltpu.CompilerParams(dimension_semantics=sem, vmem_limit_bytes=VMEM_LIMIT)


def _tile(dim, target):
    if dim <= target:
        return dim
    best = None
    for t in range(LANES, target + 1, LANES):
        if dim % t == 0:
            best = t
    assert best is not None, dim
    return best


def _row_div(rows, target):
    if rows <= target:
        return rows
    best = None
    for t in range(8, target + 1, 8):
        if rows % t == 0:
            best = t
    assert best is not None, rows
    return best


_ANY = pl.BlockSpec(memory_space=pl.ANY)
_MESH = pl.DeviceIdType.MESH


def _xchg_out_shapes(kind, srcs):
    if kind == "gather":
        return [jax.ShapeDtypeStruct((N_CHIPS,) + s.shape, s.dtype) for s in srcs]
    return [jax.ShapeDtypeStruct(s.shape, s.dtype) for s in srcs]


def _xchg_scratch(kind, n):
    per = 1 if kind == "swap" else N_CHIPS - 1
    return [pltpu.SemaphoreType.DMA((per * n,)), pltpu.SemaphoreType.DMA((per * n,)), pltpu.SemaphoreType.DMA((n,))]


def _xchg_copies(kind, srcs, dsts, send_sems, recv_sems, local_sems, arrivals):
    x, y, c = lax.axis_index("x"), lax.axis_index("y"), lax.axis_index("c")
    me = 2 * x + y
    peers = [(1 - x, y), (x, 1 - y), (1 - x, 1 - y)]
    local, out, back = [], [], []
    for i, (s, d) in enumerate(zip(srcs, dsts)):
        if kind == "swap":
            cp = pltpu.make_async_remote_copy(src_ref=s, dst_ref=d, send_sem=send_sems.at[i], recv_sem=recv_sems.at[i],
                                              device_id=(x, y, 1 - c), device_id_type=_MESH)
            out.append(cp)
            back.append(cp)
            continue
        local.append(pltpu.make_async_copy(s if kind == "gather" else s.at[me], d.at[me], local_sems.at[i]))
        for kk, (px, py) in enumerate(peers):
            j = (N_CHIPS - 1) * i + kk
            theirs = 2 * px + py
            out.append(pltpu.make_async_remote_copy(
                src_ref=s if kind == "gather" else s.at[theirs], dst_ref=d.at[me], send_sem=send_sems.at[j],
                recv_sem=recv_sems.at[j], device_id=(px, py, c), device_id_type=_MESH))
            if arrivals:
                back.append(pltpu.make_async_remote_copy(
                    src_ref=s if kind == "gather" else s.at[me], dst_ref=d.at[theirs], send_sem=send_sems.at[j],
                    recv_sem=recv_sems.at[j], device_id=(px, py, c), device_id_type=_MESH))
    return local, out, back


def _xchg_start(kind, srcs, dsts, sems):
    local, out, _ = _xchg_copies(kind, srcs, dsts, *sems, arrivals=False)
    for cp in local + out:
        cp.start()


def _xchg_wait(kind, srcs, dsts, sems):
    local, out, back = _xchg_copies(kind, srcs, dsts, *sems, arrivals=True)
    for cp in back:
        cp.wait_recv()
    for cp in out:
        cp.wait_send()
    for cp in local:
        cp.wait()


def _xchg_at_ends(kind, srcs, dsts, sems, grid, first):
    ids = [pl.program_id(a) for a in range(len(grid))]
    cond = None
    for i, n in zip(ids, grid):
        c = (i == 0) if first else (i == n - 1)
        cond = c if cond is None else jnp.logical_and(cond, c)

    @pl.when(cond)
    def _():
        (_xchg_start if first else _xchg_wait)(kind, srcs, dsts, sems)


def _xchg_call(kind, srcs, name):
    n = len(srcs)

    def kern(*refs):
        s, d, sems = refs[:n], refs[n:2 * n], refs[2 * n:]
        _xchg_start(kind, s, d, sems)
        _xchg_wait(kind, s, d, sems)

    return pl.pallas_call(
        kern, name=name, in_specs=[_ANY] * n, out_specs=[_ANY] * n, out_shape=_xchg_out_shapes(kind, srcs),
        scratch_shapes=_xchg_scratch(kind, n),
    )(*srcs)


_DIMS = {"nn": (((1,), (0,)), ((), ())), "nt": (((1,), (1,)), ((), ())), "tn": (((0,), (0,)), ((), ()))}


def _matmul(a, b, mode, out_dtype, name, *, b_chunks=False, out_chunks=False, xchg=None):
    if b_chunks:
        nchunk, brows, bcols = b.shape
        bshape = (brows, nchunk * bcols)
    else:
        bshape = b.shape
    if mode == "nn":
        (m, k), (_, n) = a.shape, bshape
    elif mode == "nt":
        (m, k), (n, _) = a.shape, bshape
    else:
        (k, m), (_, n) = a.shape, bshape
    tm, tk = (1408, 1024) if mode == "tn" else (1024, 1408)
    tm, tn, tk = _tile(m, tm), _tile(n, 1408), _tile(k, tk)
    if b_chunks:
        if mode == "nn":
            tn = bcols
        else:
            assert mode == "nt"
            tk = bcols
    if out_chunks:
        assert n % N_CHIPS == 0
        tn = n // N_CHIPS
    ni, nj, nk = m // tm, n // tn, k // tk
    dims = _DIMS[mode]
    n_x = 0 if xchg is None else len(xchg[1])

    def kern(*refs):
        a_ref, b_ref = refs[0], refs[1]
        xs = refs[2:2 + n_x]
        o_ref = refs[2 + n_x]
        xd = refs[3 + n_x:3 + 2 * n_x]
        acc_ref = refs[3 + 2 * n_x]
        sems = refs[4 + 2 * n_x:]
        kk = pl.program_id(2)
        if n_x:
            _xchg_at_ends(xchg[0], xs, xd, sems, (ni, nj, nk), True)

        @pl.when(kk == 0)
        def _():
            acc_ref[...] = jnp.zeros_like(acc_ref)

        acc_ref[...] += lax.dot_general(a_ref[...], b_ref[...], dims, preferred_element_type=F32)

        @pl.when(kk == nk - 1)
        def _():
            o_ref[...] = acc_ref[...].astype(o_ref.dtype)

        if n_x:
            _xchg_at_ends(xchg[0], xs, xd, sems, (ni, nj, nk), False)

    if mode == "tn":
        a_spec = pl.BlockSpec((tk, tm), lambda i, j, kk: (kk, i))
    else:
        a_spec = pl.BlockSpec((tm, tk), lambda i, j, kk: (i, kk))
    if b_chunks and mode == "nn":
        b_spec = pl.BlockSpec((None, tk, tn), lambda i, j, kk: (j, kk, 0))
    elif b_chunks:
        b_spec = pl.BlockSpec((None, tn, tk), lambda i, j, kk: (kk, j, 0))
    elif mode == "nt":
        b_spec = pl.BlockSpec((tn, tk), lambda i, j, kk: (j, kk))
    else:
        b_spec = pl.BlockSpec((tk, tn), lambda i, j, kk: (kk, j))
    if out_chunks:
        o_spec = pl.BlockSpec((None, tm, tn), lambda i, j, kk: (j, i, 0))
        o_shape = jax.ShapeDtypeStruct((N_CHIPS, m, tn), out_dtype)
    else:
        o_spec = pl.BlockSpec((tm, tn), lambda i, j, kk: (i, j))
        o_shape = jax.ShapeDtypeStruct((m, n), out_dtype)
    xs = [] if xchg is None else list(xchg[1])
    res = pl.pallas_call(
        kern, name=name, grid=(ni, nj, nk),
        in_specs=[a_spec, b_spec] + [_ANY] * n_x, out_specs=[o_spec] + [_ANY] * n_x,
        out_shape=[o_shape] + (_xchg_out_shapes(xchg[0], xs) if n_x else []),
        scratch_shapes=[pltpu.VMEM((tm, tn), F32)] + (_xchg_scratch(xchg[0], n_x) if n_x else []),
        compiler_params=_params(("arbitrary", "arbitrary", "arbitrary") if n_x else ("parallel", "parallel", "arbitrary")),
    )(a, b, *xs)
    return (res[0], list(res[1:])) if n_x else res[0]


def _rowcall(body, *, name, T, S, tr, rows, exs=(), pars=(), row_outs=(), ex_outs=(), par_outs=(), aliases=None):
    assert S % tr == 0 and T % S == 0
    per_ex = S // tr
    nb = T // S
    n_rows, n_exs, n_pars = len(rows), len(exs), len(pars)
    n_ro, n_eo, n_po = len(row_outs), len(ex_outs), len(par_outs)

    def kern(*refs):
        ins = refs[:n_rows + n_exs + n_pars]
        outs = refs[n_rows + n_exs + n_pars:]
        rv = [r[...].astype(F32) for r in ins[:n_rows]]
        ev = [r[0] for r in ins[n_rows:n_rows + n_exs]]
        pv = [r[...] for r in ins[n_rows + n_exs:]]
        ro, eo, po = body(rv, ev, pv)
        i = pl.program_id(0)
        for ref, val in zip(outs[:n_ro], ro):
            if isinstance(val, (list, tuple)):
                off = 0
                for piece in val:
                    w = piece.shape[-1]
                    ref[:, off:off + w] = piece.astype(ref.dtype)
                    off += w
            else:
                ref[...] = val.astype(ref.dtype)
        first_of_ex = (i % per_ex) == 0
        for ref, val in zip(outs[n_ro:n_ro + n_eo], eo):
            @pl.when(first_of_ex)
            def _(ref=ref, val=val):
                ref[0] = val

            @pl.when(jnp.logical_not(first_of_ex))
            def _(ref=ref, val=val):
                ref[0] += val
        for ref, val in zip(outs[n_ro + n_eo:], po):
            @pl.when(i == 0)
            def _(ref=ref, val=val):
                ref[...] = val

            @pl.when(i != 0)
            def _(ref=ref, val=val):
                ref[...] += val

    in_specs = [pl.BlockSpec((tr, w), functools.partial(lambda i, cb: (i, cb), cb=cb)) for (_, w, cb) in rows]
    in_specs += [pl.BlockSpec((1, 1, e.shape[-1]), lambda i: (i // per_ex, 0, 0)) for e in exs]
    in_specs += [pl.BlockSpec(p.shape, functools.partial(lambda i, nd: (0,) * nd, nd=p.ndim)) for p in pars]
    out_specs = [pl.BlockSpec((tr, w), functools.partial(lambda i, cb: (i, cb), cb=cb)) for (_, _, w, cb) in row_outs]
    out_specs += [pl.BlockSpec((1, 1, f), lambda i: (i // per_ex, 0, 0)) for f in ex_outs]
    out_specs += [pl.BlockSpec(tuple(s), functools.partial(lambda i, nd: (0,) * nd, nd=len(s))) for s in par_outs]
    out_shape = [jax.ShapeDtypeStruct((T, tw), dt) for (tw, dt, _, _) in row_outs]
    out_shape += [jax.ShapeDtypeStruct((nb, 1, f), F32) for f in ex_outs]
    out_shape += [jax.ShapeDtypeStruct(tuple(s), F32) for s in par_outs]
    res = pl.pallas_call(
        kern, name=name, grid=(T // tr,), in_specs=in_specs, out_specs=out_specs, out_shape=out_shape,
        input_output_aliases=aliases or {}, compiler_params=_params(("arbitrary",)),
    )(*[r[0] for r in rows], *exs, *pars)
    return res[:n_ro], res[n_ro:n_ro + n_eo], res[n_ro + n_eo:]


def _rms(x, w, n=None):
    n = x.shape[-1] if n is None else n
    ms = jnp.sum(x * x, axis=-1, keepdims=True) * (1.0 / n)
    return x * lax.rsqrt(ms + NORM_EPS) * w


def _gelu(x):
    return 0.5 * x * (1.0 + lax.erf(x * np.float32(1.0 / math.sqrt(2.0))))


def _silu(x):
    return x * jax.nn.sigmoid(x)


@jax.custom_vjp
def _rope(x, cos, sin):
    return x * cos + pltpu.roll(x, 64, 1) * sin


def _rope_fwd(x, cos, sin):
    return _rope(x, cos, sin), (cos, sin)


def _rope_bwd(res, dy):
    cos, sin = res
    return dy * cos + pltpu.roll(dy * sin, 64, 1), None, None


_rope.defvjp(_rope_fwd, _rope_bwd)


def _heads(x, n):
    return [x[:, h * LANES:(h + 1) * LANES] for h in range(n)]


@functools.partial(jax.custom_vjp, nondiff_argnums=(1,))
def _lroll(x, shift):
    return pltpu.roll(x, shift, 1)


def _lroll_fwd(x, shift):
    return _lroll(x, shift), None


def _lroll_bwd(shift, _, dy):
    return (pltpu.roll(dy, (LANES - shift) % LANES, 1),)


_lroll.defvjp(_lroll_fwd, _lroll_bwd)


def _spread_rotary(v):
    low = lax.broadcasted_iota(jnp.int32, (1, LANES), 1) < 64
    return [jnp.where(low, v, _lroll(v, 32)), jnp.where(low, _lroll(v, 64), _lroll(v, 96))]


def _spread_values(v):
    low = lax.broadcasted_iota(jnp.int32, (1, LANES), 1) < 64
    return [jnp.where(low, v, 0.0), jnp.where(low, _lroll(v, 64), 0.0)]


def _f_norm_mod(x, w, sc, sh):
    return _rms(x, w) * (1.0 + sc) + sh


def _f_resid_norm_mod(xa, delta, g, w, sc, sh):
    xn = xa + g * delta
    return xn, _f_norm_mod(xn, w, sc, sh)


def _f_mixprep(proj, ca, sa, cb, sb, qnw, kvnw):
    qa = [_rope(p, ca, sa) for pair in _heads(proj[:, P_AQ:P_AK], A_Q_HEADS // 2) for p in _spread_rotary(pair)]
    ka = [_rope(p, ca, sa) for p in _spread_rotary(proj[:, P_AK:P_AV])]
    va = _spread_values(proj[:, P_AV:P_CQ])
    cqn = _rms(proj[:, P_CQ:P_CKV], qnw)
    ckvn = _rms(proj[:, P_CKV:P_MIX], kvnw)
    kr = _rope(proj[:, P_KR:P_AQ], cb, sb)
    return jnp.concatenate(qa, -1), jnp.concatenate(ka, -1), jnp.concatenate(va, -1), cqn, ckvn, kr


def _f_mlaprep(q, kv, kr, cb, sb):
    qs = [_rope(p, cb, sb) for p in _heads(q, B_HEADS)]
    ks = [p + kr for p in _heads(kv[:, :B_HEADS * LANES], B_HEADS)]
    return jnp.concatenate(qs, -1), jnp.concatenate(ks, -1), kv[:, B_HEADS * LANES:]


def _f_sgu(cu, cv, ln_w, ln_b, w_s, b_col):
    u = _gelu(cu)
    v = _gelu(cv)
    mu = jnp.mean(v, axis=-1, keepdims=True)
    var = jnp.mean(jnp.square(v - mu), axis=-1, keepdims=True)
    vn = (v - mu) * lax.rsqrt(var + NORM_EPS) * ln_w + ln_b
    r = lax.broadcasted_iota(jnp.int32, (C_CHUNK, C_CHUNK), 0)
    c = lax.broadcasted_iota(jnp.int32, (C_CHUNK, C_CHUNK), 1)
    lane = lax.broadcasted_iota(jnp.int32, (1, LANES), 1)
    per_block = LANES // C_GROUP_DIM
    blocks = []
    for blk, vb in enumerate(_heads(vn, C_WIDTH // LANES)):
        mixed = jnp.zeros(vb.shape, F32)
        for j in range(per_block):
            g = blk * per_block + j
            gm = (lane // C_GROUP_DIM == j).astype(F32)
            wg = jnp.where(r >= c, w_s[g], 0.0).astype(BF16)
            mixed = mixed + jnp.dot(wg, (vb * gm).astype(BF16), preferred_element_type=F32) + b_col[g] * gm
        blocks.append(mixed)
    return u * jnp.concatenate(blocks, -1)


@jax.custom_vjp
def _pack_pairs(x):
    heads = _heads(x, x.shape[-1] // LANES)
    return jnp.concatenate([heads[i] + pltpu.roll(heads[i + 1], 64, 1) for i in range(0, len(heads), 2)], -1)


def _pack_pairs_fwd(x):
    return _pack_pairs(x), None


def _pack_pairs_bwd(_, dy):
    out = []
    for p in _heads(dy, dy.shape[-1] // LANES):
        out += [p, pltpu.roll(p, 64, 1)]
    return (jnp.concatenate(out, -1),)


_pack_pairs.defvjp(_pack_pairs_fwd, _pack_pairs_bwd)


def _f_outnorm(oa, ob, yc, gw):
    na, nb = A_Q_HEADS * HEAD_DIM, B_HEADS * B_V
    ya = _rms(_pack_pairs(oa), gw[:, :na])
    yb = _rms(_pack_pairs(ob), gw[:, na:na + nb])
    ycn = _rms(yc, gw[:, na + nb:])
    return jnp.concatenate([ya, yb, ycn], -1)


def _f_swiglu(gate, up):
    return _silu(gate) * up


def _mask(q_start, k_start, tq, tk, window):
    qpos = q_start + lax.broadcasted_iota(jnp.int32, (tq, tk), 0)
    kpos = k_start + lax.broadcasted_iota(jnp.int32, (tq, tk), 1)
    m = kpos <= qpos
    if window is not None:
        m = jnp.logical_and(m, qpos - kpos < window)
    return m


def _tile_fwd(qv, kk, vv, q_start, k_start, n_free, scale, window, m0, l0):
    tq = qv.shape[0]
    W = kk.shape[0]
    c = scale * LOG2_E
    parts = []
    if n_free > 0:
        parts.append((lax.dot_general(qv, kk[:n_free], _DIMS["nt"], preferred_element_type=F32), vv[:n_free]))
    if W > n_free:
        s = lax.dot_general(qv, kk[n_free:], _DIMS["nt"], preferred_element_type=F32)
        s = jnp.where(_mask(q_start, k_start + n_free, tq, W - n_free, window), s, NEG_INF)
        parts.append((s, vv[n_free:]))
    m = None if m0 is None else m0 * (1.0 / scale)
    for s, _ in parts:
        mx = jnp.max(s, axis=-1, keepdims=True)
        m = mx if m is None else jnp.maximum(m, mx)
    l = None if l0 is None else l0 * jnp.exp2((m0 * (1.0 / scale) - m) * c)
    o = None
    for s, vpart in parts:
        p = jnp.exp2((s - m) * c)
        ps = jnp.sum(p, axis=-1, keepdims=True)
        l = ps if l is None else l + ps
        po = jnp.dot(p.astype(BF16), vpart, preferred_element_type=F32)
        o = po if o is None else o + po
    return o / l, m * scale + jnp.log(l)


def _tile_bwd(qv, kk, vv, dof, ov, lse, q_start, k_start, n_free, scale, window):
    tq = qv.shape[0]
    W = kk.shape[0]
    dob = dof.astype(BF16)
    delta = jnp.sum(dof * ov, axis=-1, keepdims=True)
    c = scale * LOG2_E
    lse2 = lse * LOG2_E
    dq = None
    outs = []
    for (a, b, masked) in ((0, n_free, False), (n_free, W, True)):
        if b <= a:
            continue
        kp, vp = kk[a:b], vv[a:b]
        s = lax.dot_general(qv, kp, _DIMS["nt"], preferred_element_type=F32)
        if masked:
            s = jnp.where(_mask(q_start, k_start + a, tq, b - a, window), s, NEG_INF)
        p = jnp.exp2(s * c - lse2)
        dp = lax.dot_general(dob, vp, _DIMS["nt"], preferred_element_type=F32)
        ds = (p * ((dp - delta) * scale)).astype(BF16)
        d = jnp.dot(ds, kp, preferred_element_type=F32)
        dq = d if dq is None else dq + d
        dkp = lax.dot_general(ds, qv, _DIMS["tn"], preferred_element_type=F32)
        dvp = lax.dot_general(p.astype(BF16), dob, _DIMS["tn"], preferred_element_type=F32)
        outs.append((a, dkp, dvp))
    return dq, outs


def _attn_fwd(q, k, v, sinks, *, B, S, HQ, HK, window, scale, tq, band, name, xchg=None):
    G = HQ // HK
    nq = S // tq
    T = B * S
    has_sink = sinks is not None
    n_x = 0 if xchg is None else len(xchg[1])
    n_in = 4 if has_sink else 3
    hp = G if window is not None else 1
    grid = (B, HQ // hp, nq)

    def kern(*refs):
        xs, xd, sems = refs[n_in:n_in + n_x], refs[n_in + n_x + 2:n_in + 2 * n_x + 2], refs[n_in + 2 * n_x + 2:]
        refs = refs[:n_in] + refs[n_in + n_x:n_in + n_x + 2]
        if n_x:
            _xchg_at_ends(xchg[0], xs, xd, sems, grid, True)
        if has_sink:
            q_ref, k_ref, v_ref, s_ref, o_ref, lse_ref = refs
        else:
            q_ref, k_ref, v_ref, o_ref, lse_ref = refs
        q_start = pl.program_id(2) * tq

        def sink(g, rows):
            if not has_sink:
                return None, None
            return jnp.broadcast_to(s_ref[g][:, :1], (rows, 1)), jnp.ones((rows, 1), F32)

        def finish(o, lse):
            o_ref[...] = o.astype(o_ref.dtype)
            lse_ref[...] = jnp.broadcast_to(lse, (tq, LANES))

        if window is None:
            qv = q_ref[...]
            m0, l0 = sink(0, tq)
            bidx = q_start // band
            for bb in range(S // band):
                @pl.when(bidx == bb)
                def _(bb=bb):
                    W = (bb + 1) * band
                    finish(*_tile_fwd(qv, k_ref[0:W, :], v_ref[0:W, :], q_start, 0, bb * band, scale, None, m0, l0))
        else:
            sub = min(tq, WINDOW_SUB)
            W = min(S, sub + window)
            for r in range(0, tq, sub):
                k_start = pl.multiple_of(jnp.maximum(q_start + r - window, 0), window)
                kk, vv = k_ref[pl.ds(k_start, W), :], v_ref[pl.ds(k_start, W), :]
                for g in range(hp):
                    cols = slice(g * LANES, (g + 1) * LANES)
                    o, lse = _tile_fwd(q_ref[r:r + sub, cols], kk, vv, q_start + r, k_start, 0, scale, window, *sink(g, sub))
                    o_ref[r:r + sub, cols] = o.astype(o_ref.dtype)
                    lse_ref[r:r + sub, cols] = jnp.broadcast_to(lse, (sub, LANES))
        if n_x:
            _xchg_at_ends(xchg[0], xs, xd, sems, grid, False)

    q_spec = pl.BlockSpec((tq, hp * LANES), lambda b, h, i: (b * nq + i, h))
    kv_spec = pl.BlockSpec((S, LANES), lambda b, h, i: (b, h * hp // G))
    in_specs = [q_spec, kv_spec, kv_spec]
    args = [q, k, v]
    if has_sink:
        in_specs.append(pl.BlockSpec((hp, 1, LANES), lambda b, h, i: (h, 0, 0)))
        args.append(sinks)
    xs = [] if xchg is None else list(xchg[1])
    res = pl.pallas_call(
        kern, name=name, grid=grid, in_specs=in_specs + [_ANY] * n_x, out_specs=[q_spec, q_spec] + [_ANY] * n_x,
        out_shape=[jax.ShapeDtypeStruct((T, HQ * LANES), BF16), jax.ShapeDtypeStruct((T, HQ * LANES), F32)]
        + (_xchg_out_shapes(xchg[0], xs) if n_x else []),
        scratch_shapes=_xchg_scratch(xchg[0], n_x) if n_x else [],
        compiler_params=_params(("arbitrary",) * 3 if n_x else ("parallel", "parallel", "arbitrary")),
    )(*args, *xs)
    return ((res[0], res[1]), list(res[2:])) if n_x else res


def _attn_bwd(q, k, v, o, lse, do, sinks, *, B, S, HQ, HK, window, scale, tq, band, name, xchg=None):
    G = HQ // HK
    nq = S // tq
    T = B * S
    has_sink = sinks is not None
    n_x = 0 if xchg is None else len(xchg[1])
    n_in, n_out = (7, 4) if has_sink else (6, 3)
    hp = G if window is not None else 1
    ng = G // hp
    grid = (B, HK, ng, nq)

    def kern(*refs):
        xs, xd = refs[n_in:n_in + n_x], refs[n_in + n_x + n_out:n_in + 2 * n_x + n_out]
        dk_acc, dv_acc = refs[n_in + 2 * n_x + n_out:n_in + 2 * n_x + n_out + 2]
        sems = refs[n_in + 2 * n_x + n_out + 2:]
        refs = refs[:n_in] + refs[n_in + n_x:n_in + n_x + n_out]
        if n_x:
            _xchg_at_ends(xchg[0], xs, xd, sems, grid, True)
        if has_sink:
            q_ref, k_ref, v_ref, o_ref, lse_ref, do_ref, s_ref, dq_ref, dk_ref, dv_ref, ds_ref = refs
        else:
            q_ref, k_ref, v_ref, o_ref, lse_ref, do_ref, dq_ref, dk_ref, dv_ref = refs
        gi = pl.program_id(2)
        qi = pl.program_id(3)
        q_start = qi * tq

        @pl.when(jnp.logical_and(gi == 0, qi == 0))
        def _():
            dk_acc[...] = jnp.zeros_like(dk_acc)
            dv_acc[...] = jnp.zeros_like(dv_acc)

        if window is None:
            qv = q_ref[...]
            dof = do_ref[...].astype(F32)
            ov = o_ref[...].astype(F32)
            lse_v = lse_ref[...][:, :1]
            bidx = q_start // band
            for bb in range(S // band):
                @pl.when(bidx == bb)
                def _(bb=bb):
                    W = (bb + 1) * band
                    dq, outs = _tile_bwd(qv, k_ref[0:W, :], v_ref[0:W, :], dof, ov, lse_v, q_start, 0, bb * band, scale, None)
                    dq_ref[...] = dq.astype(dq_ref.dtype)
                    for a, dkp, dvp in outs:
                        dk_acc[a:a + dkp.shape[0], :] += dkp
                        dv_acc[a:a + dvp.shape[0], :] += dvp
        else:
            sub = min(tq, WINDOW_SUB)
            W = min(S, sub + window)
            parts = []
            sink_parts = []
            for g in range(hp):
                cols = slice(g * LANES, (g + 1) * LANES)
                acc = jnp.zeros((1, LANES), F32)
                for r in range(0, tq, sub):
                    rows = slice(r, r + sub)
                    k_start = pl.multiple_of(jnp.maximum(q_start + r - window, 0), window)
                    dof, ov = do_ref[rows, cols].astype(F32), o_ref[rows, cols].astype(F32)
                    lse_v = lse_ref[rows, cols][:, :1]
                    dq, outs = _tile_bwd(q_ref[rows, cols], k_ref[pl.ds(k_start, W), :], v_ref[pl.ds(k_start, W), :],
                                         dof, ov, lse_v, q_start + r, k_start, 0, scale, window)
                    dq_ref[rows, cols] = dq.astype(dq_ref.dtype)
                    parts.append((k_start, outs[0][1], outs[0][2]))
                    if has_sink:
                        delta = jnp.sum(dof * ov, axis=-1, keepdims=True)
                        acc = acc - jnp.sum(jnp.exp(s_ref[g][:, :1] - lse_v) * delta, axis=0, keepdims=True)
                sink_parts.append(acc)
            for k_start, dkp, dvp in parts:
                dk_acc[pl.ds(k_start, W), :] += dkp
                dv_acc[pl.ds(k_start, W), :] += dvp

        @pl.when(jnp.logical_and(gi == ng - 1, qi == nq - 1))
        def _():
            dk_ref[...] = dk_acc[...].astype(dk_ref.dtype)
            dv_ref[...] = dv_acc[...].astype(dv_ref.dtype)
        if has_sink:
            assert window is not None
            for g, part in enumerate(sink_parts):
                part = jnp.broadcast_to(part, (1, LANES))

                @pl.when(qi == 0)
                def _(g=g, part=part):
                    ds_ref[g] = part

                @pl.when(qi != 0)
                def _(g=g, part=part):
                    ds_ref[g] += part
        if n_x:
            _xchg_at_ends(xchg[0], xs, xd, sems, grid, False)

    q_spec = pl.BlockSpec((tq, hp * LANES), lambda b, hk, g, i: (b * nq + i, hk * ng + g))
    kv_spec = pl.BlockSpec((S, LANES), lambda b, hk, g, i: (b, hk))
    in_specs = [q_spec, kv_spec, kv_spec, q_spec, q_spec, q_spec]
    args = [q, k, v, o, lse, do]
    out_specs = [q_spec, kv_spec, kv_spec]
    out_shape = [jax.ShapeDtypeStruct((T, HQ * LANES), BF16), jax.ShapeDtypeStruct((T, HK * LANES), BF16),
                 jax.ShapeDtypeStruct((T, HK * LANES), BF16)]
    if has_sink:
        in_specs.append(pl.BlockSpec((hp, 1, LANES), lambda b, hk, g, i: (hk * ng + g, 0, 0)))
        args.append(sinks)
        out_specs.append(pl.BlockSpec((hp, 1, LANES), lambda b, hk, g, i: (b * HK * ng + hk * ng + g, 0, 0)))
        out_shape.append(jax.ShapeDtypeStruct((B * HQ, 1, LANES), F32))
    xs = [] if xchg is None else list(xchg[1])
    res = pl.pallas_call(
        kern, name=name, grid=grid, in_specs=in_specs + [_ANY] * n_x, out_specs=out_specs + [_ANY] * n_x,
        out_shape=out_shape + (_xchg_out_shapes(xchg[0], xs) if n_x else []),
        scratch_shapes=[pltpu.VMEM((S, LANES), F32)] * 2 + (_xchg_scratch(xchg[0], n_x) if n_x else []),
        compiler_params=_params(("arbitrary",) * 4 if n_x else ("parallel", "parallel", "arbitrary", "arbitrary")),
    )(*args, *xs)
    main = tuple(res[:n_out]) if has_sink else (*res[:n_out], None)
    return (main, list(res[n_out:])) if n_x else main


def _rope_tables(pos_col, freq, sign, keep, name):
    T = pos_col.shape[0]
    tr = _tile(T, 1024)

    def kern(p_ref, f_ref, s_ref, k_ref, c_out, s_out):
        ang = p_ref[...] * f_ref[...]
        c_out[...] = jnp.cos(ang) * k_ref[...]
        s_out[...] = jnp.sin(ang) * s_ref[...]

    spec = pl.BlockSpec((tr, LANES), lambda i: (i, 0))
    par = pl.BlockSpec((1, LANES), lambda i: (0, 0))
    return pl.pallas_call(
        kern, name=name, grid=(T // tr,), in_specs=[pl.BlockSpec((tr, 1), lambda i: (i, 0)), par, par, par],
        out_specs=[spec, spec], out_shape=[jax.ShapeDtypeStruct((T, LANES), F32)] * 2,
        compiler_params=_params(("parallel",)),
    )(pos_col, freq, sign, keep)


def _ada_fwd(c8, ada_w, ada_b):
    L, D, N = ada_w.shape

    def kern(c_ref, w_ref, b_ref, o_ref):
        act = _silu(c_ref[...]).astype(BF16)
        o_ref[0] = jnp.dot(act, w_ref[0], preferred_element_type=F32) + b_ref[0]

    return pl.pallas_call(
        kern, name="ada_fwd", grid=(L,),
        in_specs=[pl.BlockSpec((8, D), lambda l: (0, 0)), pl.BlockSpec((1, D, N), lambda l: (l, 0, 0)),
                  pl.BlockSpec((1, 1, N), lambda l: (l, 0, 0))],
        out_specs=pl.BlockSpec((1, 8, N), lambda l: (l, 0, 0)),
        out_shape=jax.ShapeDtypeStruct((L, 8, N), F32), compiler_params=_params(("parallel",)),
    )(c8, ada_w, ada_b)


def _ada_bwd(c8, dmod):
    L, _, N = dmod.shape
    D = c8.shape[1]

    def kern(c_ref, d_ref, gw_ref):
        act = _silu(c_ref[...]).astype(BF16)
        gw_ref[0] = lax.dot_general(act, d_ref[0].astype(BF16), _DIMS["tn"], preferred_element_type=F32)

    return pl.pallas_call(
        kern, name="ada_bwd", grid=(L,),
        in_specs=[pl.BlockSpec((8, D), lambda l: (0, 0)), pl.BlockSpec((1, 8, N), lambda l: (l, 0, 0))],
        out_specs=pl.BlockSpec((1, D, N), lambda l: (l, 0, 0)),
        out_shape=jax.ShapeDtypeStruct((L, D, N), F32), compiler_params=_params(("parallel",)),
    )(c8, dmod)


def _sum_examples(d):
    L, _, N = d.shape

    def kern(d_ref, o_ref):
        o_ref[...] = jnp.sum(d_ref[...], axis=1, keepdims=True)

    vm = pl.BlockSpec(memory_space=pltpu.VMEM)
    return pl.pallas_call(kern, name="sum_examples", in_specs=[vm], out_specs=vm,
                          out_shape=jax.ShapeDtypeStruct((L, 1, N), F32))(d)


def _adamw(w, ga, gb, m, v, name):
    rows, cols = w.shape
    tr = _row_div(rows, 256)
    two = gb is not None

    def kern(*refs):
        if two:
            w_ref, ga_ref, gb_ref, m_ref, v_ref, g_out, d_out, m_out, v_out = refs
            gv = ga_ref[...] + gb_ref[...]
        else:
            w_ref, ga_ref, m_ref, v_ref, g_out, d_out, m_out, v_out = refs
            gv = ga_ref[...]
        mn = ADAM_B1 * m_ref[...] + (1.0 - ADAM_B1) * gv
        vn = ADAM_B2 * v_ref[...] + (1.0 - ADAM_B2) * jnp.square(gv)
        m_hat = mn / (1.0 - ADAM_B1 ** ADAM_STEP)
        v_hat = vn / (1.0 - ADAM_B2 ** ADAM_STEP)
        g_out[...] = gv
        d_out[...] = -ADAM_LR * (m_hat / (jnp.sqrt(v_hat) + ADAM_EPS) + ADAM_WD * w_ref[...])
        m_out[...] = mn
        v_out[...] = vn

    spec = pl.BlockSpec((tr, cols), lambda i: (i, 0))
    args = [w, ga, gb, m, v] if two else [w, ga, m, v]
    return pl.pallas_call(
        kern, name=name, grid=(rows // tr,), in_specs=[spec] * len(args), out_specs=[spec] * 4,
        out_shape=[jax.ShapeDtypeStruct((rows, cols), F32)] * 4, compiler_params=_params(("parallel",)),
    )(*args)


def _sum_slots(x, name):
    n, rows, cols = x.shape
    tr = _row_div(rows, 256)

    def kern(x_ref, o_ref):
        acc = x_ref[0].astype(F32)
        for j in range(1, n):
            acc = acc + x_ref[j].astype(F32)
        o_ref[...] = acc

    return pl.pallas_call(
        kern, name=name, grid=(rows // tr,), in_specs=[pl.BlockSpec((n, tr, cols), lambda i: (0, i, 0))],
        out_specs=pl.BlockSpec((tr, cols), lambda i: (i, 0)), out_shape=jax.ShapeDtypeStruct((rows, cols), F32),
        compiler_params=_params(("parallel",)),
    )(x)


def _all_reduce_small(blob, name, swaps=()):
    R, C = blob.shape
    n = len(swaps)

    def kern(*refs):
        src, xs, out, xd = refs[0], refs[1:1 + n], refs[1 + n], refs[2 + n:2 + 2 * n]
        pair, chips, send_sems, recv_sems = refs[2 + 2 * n:6 + 2 * n]
        swap_sems = refs[6 + 2 * n:]
        if n:
            _xchg_start("swap", xs, xd, swap_sems)
        x, y, c = lax.axis_index("x"), lax.axis_index("y"), lax.axis_index("c")
        me = 2 * x + y
        to_sibling = pltpu.make_async_remote_copy(
            src_ref=src, dst_ref=pair, send_sem=send_sems.at[0], recv_sem=recv_sems.at[0],
            device_id=(x, y, 1 - c), device_id_type=_MESH)
        to_sibling.start()
        to_sibling.wait()
        chips[me] = src[...] + pair[...]
        peers = [(1 - x, y), (x, 1 - y), (1 - x, 1 - y)]
        copies = [pltpu.make_async_remote_copy(
            src_ref=chips.at[me], dst_ref=chips.at[me], send_sem=send_sems.at[1 + kk], recv_sem=recv_sems.at[1 + kk],
            device_id=(px, py, c), device_id_type=_MESH) for kk, (px, py) in enumerate(peers)]
        for cp in copies:
            cp.start()
        for kk, (px, py) in enumerate(peers):
            pltpu.make_async_remote_copy(
                src_ref=chips.at[me], dst_ref=chips.at[2 * px + py], send_sem=send_sems.at[1 + kk],
                recv_sem=recv_sems.at[1 + kk], device_id=(px, py, c), device_id_type=_MESH).wait_recv()
        for cp in copies:
            cp.wait_send()
        acc = chips[0]
        for j in range(1, N_CHIPS):
            acc = acc + chips[j]
        out[...] = acc
        if n:
            _xchg_wait("swap", xs, xd, swap_sems)

    vm = pl.BlockSpec(memory_space=pltpu.VMEM)
    swaps = list(swaps)
    res = pl.pallas_call(
        kern, name=name, in_specs=[vm] + [_ANY] * n, out_specs=[vm] + [_ANY] * n,
        out_shape=[jax.ShapeDtypeStruct((R, C), F32)] + _xchg_out_shapes("swap", swaps),
        scratch_shapes=[pltpu.VMEM((R, C), F32), pltpu.VMEM((N_CHIPS, R, C), F32), pltpu.SemaphoreType.DMA((4,)),
                        pltpu.SemaphoreType.DMA((4,))] + (_xchg_scratch("swap", n) if n else []),
        compiler_params=pltpu.CompilerParams(vmem_limit_bytes=VMEM_LIMIT),
    )(blob, *swaps)
    return res[0], list(res[1:])


def _row_tile(S, wide=False):
    for t in ((256,) if wide else (1024, 512, 256)):
        if S % t == 0:
            return t
    return 128


def _attn_tiles(S):
    return min(S, 1024), min(S, 256), min(S, 256)


def _hosted(fn, host, got, kind, name, *args, **kw):
    if host is not None and host.get(name):
        res, xs = fn(*args, name=name, xchg=(kind, host[name]), **kw)
        got[name] = xs
        return res
    return fn(*args, name=name, **kw)


def _mm(host, got, kind, a, b, mode, out_dtype, name, **kw):
    return _hosted(_matmul, host, got, kind, name, a, b, mode, out_dtype, **kw)


def _merge_late(w, late, got, name):
    if not late or name not in late:
        return w
    return {**w, **_large_operands(dict(zip(late[name], got[name])))}


def _sgu_rows(S):
    return 4 * C_CHUNK if S % (4 * C_CHUNK) == 0 else C_CHUNK


def _layer_fwd(xin, prev, mods, w, tabs, B, S, host=None, late=None):
    T = B * S
    tr = _row_tile(S)
    sh1, sc1, g1, sh2, sc2, g2 = mods
    ca, sa, cb, sb = tabs
    sv = {}
    got = {}
    if prev is None:
        def body(rv, ev, pv):
            return [_f_norm_mod(rv[0], pv[0], ev[0], ev[1])], [], []
        (h,), _, _ = _rowcall(body, name="f_norm_mod", T=T, S=S, tr=tr, rows=[(xin, D_MODEL, 0)], exs=[sc1, sh1],
                              pars=[w["norm1_w"]], row_outs=[(D_MODEL, BF16, D_MODEL, 0)])
        x = xin
    else:
        x1p, dp, g2p = prev

        def body(rv, ev, pv):
            xn, hh = _f_resid_norm_mod(rv[0], rv[1], ev[0], pv[0], ev[1], ev[2])
            return [xn, hh], [], []
        (x, h), _, _ = _rowcall(body, name="f_resid_norm_mod1", T=T, S=S, tr=tr,
                                rows=[(x1p, D_MODEL, 0), (dp, D_MODEL, 0)], exs=[g2p, sc1, sh1], pars=[w["norm1_w"]],
                                row_outs=[(D_MODEL, F32, D_MODEL, 0), (D_MODEL, BF16, D_MODEL, 0)])
    sv["x"], sv["h"] = x, h
    proj = _mm(host, got, "gather", h, w["w_in"], "nn", BF16, "mm_in")
    sv["proj"] = proj
    w = _merge_late(w, late, got, "mm_in")

    def body(rv, ev, pv):
        outs = _f_mixprep(rv[0], rv[1], rv[2], rv[3], rv[4], pv[0], pv[1])
        return list(outs), [], []
    (qa, ka, va, cqn, ckvn, kr), _, _ = _rowcall(
        body, name="f_mixprep", T=T, S=S, tr=tr,
        rows=[(proj, P_MIX, 0), (ca, LANES, 0), (sa, LANES, 0), (cb, LANES, 0), (sb, LANES, 0)],
        pars=[w["b_q_norm_w"], w["b_kv_norm_w"]],
        row_outs=[(768, BF16, 768, 0), (256, BF16, 256, 0), (256, BF16, 256, 0), (384, BF16, 384, 0),
                  (256, BF16, 256, 0), (LANES, F32, LANES, 0)])
    sv.update(qa=qa, ka=ka, va=va, cqn=cqn, ckvn=ckvn)
    q = _matmul(cqn, w["b_w_uq"], "nn", BF16, "mm_uq")
    kv = _matmul(ckvn, w["b_w_ukv"], "nn", BF16, "mm_ukv")

    def body(rv, ev, pv):
        Q, K, V = _f_mlaprep(rv[0], rv[1], rv[2], rv[3], rv[4])
        return [Q, K, V], [], []
    (Q, K, V), _, _ = _rowcall(
        body, name="f_mlaprep", T=T, S=S, tr=tr,
        rows=[(q, 768, 0), (kv, 1536, 0), (kr, LANES, 0), (cb, LANES, 0), (sb, LANES, 0)],
        row_outs=[(768, BF16, 768, 0)] * 3)
    sv.update(Q=Q, K=K, V=V)
    ta, tb, band = _attn_tiles(S)
    oa, lse_a = _hosted(_attn_fwd, host, got, "gather", "attn_a_fwd", qa, ka, va, w["sinks"], B=B, S=S, HQ=A_Q_HEADS,
                        HK=A_KV_HEADS, window=A_WINDOW, scale=HEAD_DIM ** -0.5, tq=ta, band=None)
    ob, lse_b = _hosted(_attn_fwd, host, got, "gather", "attn_b_fwd", Q, K, V, None, B=B, S=S, HQ=B_HEADS, HK=B_HEADS,
                        window=None, scale=(B_NOPE + B_ROPE) ** -0.5, tq=tb, band=band)
    sv.update(oa=oa, lse_a=lse_a, ob=ob, lse_b=lse_b)
    w = _merge_late(w, late, got, "attn_a_fwd")
    ts = _sgu_rows(S)

    def body(rv, ev, pv):
        outs = [_f_sgu(rv[0][r:r + C_CHUNK], rv[1][r:r + C_CHUNK], pv[0], pv[1], pv[2], pv[3])
                for r in range(0, ts, C_CHUNK)]
        return [jnp.concatenate(outs, axis=0)], [], []
    (yc,), _, _ = _rowcall(body, name="f_sgu", T=T, S=S, tr=ts,
                           rows=[(proj, C_WIDTH, P_CU // C_WIDTH), (proj, C_WIDTH, P_CV // C_WIDTH)],
                           pars=[w["c_ln_w"], w["c_ln_b"], w["c_w_s"], w["c_b_col"]],
                           row_outs=[(C_WIDTH, F32, C_WIDTH, 0)])
    sv["yc"] = yc

    def body(rv, ev, pv):
        return [_f_outnorm(rv[0], rv[1], rv[2], pv[0])], [], []
    (y,), _, _ = _rowcall(body, name="f_outnorm", T=T, S=S, tr=tr,
                          rows=[(oa, 768, 0), (ob, 768, 0), (yc, C_WIDTH, 0)], pars=[w["out_norm_w"]],
                          row_outs=[(D_MODEL, BF16, D_MODEL, 0)])
    sv["y"] = y
    o = _mm(host, got, "gather", y, w["w_out"], "nn", BF16, "mm_out")
    sv["o"] = o

    def body(rv, ev, pv):
        xn, hh = _f_resid_norm_mod(rv[0], rv[1], ev[0], pv[0], ev[1], ev[2])
        return [xn, hh], [], []
    (x1, h2), _, _ = _rowcall(body, name="f_resid_norm_mod2", T=T, S=S, tr=tr,
                              rows=[(x, D_MODEL, 0), (o, D_MODEL, 0)], exs=[g1, sc2, sh2], pars=[w["norm2_w"]],
                              row_outs=[(D_MODEL, F32, D_MODEL, 0), (D_MODEL, BF16, D_MODEL, 0)])
    sv["h2"] = h2
    gu = _mm(host, got, "gather", h2, w["w_gate_up"], "nn", BF16, "mm_gate_up", b_chunks=True)
    sv["gu"] = gu

    def body(rv, ev, pv):
        return [_f_swiglu(rv[0], rv[1])], [], []
    (act,), _, _ = _rowcall(body, name="f_swiglu", T=T, S=S, tr=_row_tile(S, wide=True),
                            rows=[(gu, FFN_HIDDEN, 0), (gu, FFN_HIDDEN, 1)], row_outs=[(FFN_HIDDEN, BF16, FFN_HIDDEN, 0)])
    sv["act"] = act
    d = _mm(host, got, "gather", act, w["w_down"], "nn", BF16, "mm_down")
    sv["x1"], sv["d"], sv["w"] = x1, d, w
    return (x1, d, g2), sv, got


def _final(x1, d, g2, fw, target, B, S):
    T = B * S
    tr = _row_tile(S)

    def loss_fn(x1v, dv, gv, wv, tv):
        yf = _rms(x1v + gv * dv, wv)
        return 0.5 * jnp.sum(jnp.mean(jnp.square(yf - tv), axis=-1))

    def body(rv, ev, pv):
        x1v, dv, tv = rv
        val, vjp = jax.vjp(lambda a, b_, g, ww: loss_fn(a, b_, g, ww, tv), x1v, dv, ev[0], pv[0])
        dx1, dd, dg, dw = vjp(jnp.ones((), F32))
        return [dx1, dd], [dg], [dw, jnp.full((1, LANES), val, F32)]
    (dx1, dd), (dg2,), (dfw, loss) = _rowcall(
        body, name="final_loss", T=T, S=S, tr=tr, rows=[(x1, D_MODEL, 0), (d, D_MODEL, 0), (target, D_MODEL, 0)],
        exs=[g2], pars=[fw], row_outs=[(D_MODEL, F32, D_MODEL, 0), (D_MODEL, BF16, D_MODEL, 0)],
        ex_outs=[D_MODEL], par_outs=[(1, D_MODEL), (1, LANES)])
    return loss, dx1, dd, dg2, dfw


def _layer_bwd(sv, prev, mods, tabs, dx1, dd, B, S, host=None, own_early=False):
    w = sv["w"]
    T = B * S
    tr = _row_tile(S)
    sh1, sc1, g1, sh2, sc2, g2 = mods
    ca, sa, cb, sb = tabs
    gr = {}
    got = {}
    dact = _mm(host, got, "a2a", dd, w["w_down"], "nt", BF16, "mm_down_dx")
    gr["w_down"] = _matmul(sv["act"], dd, "tn", BF16, "mm_down_dw")

    def body(rv, ev, pv):
        _, vjp = jax.vjp(_f_swiglu, rv[0], rv[1])
        dgate, dup = vjp(rv[2])
        return [[dgate, dup]], [], []
    (dgu,), _, _ = _rowcall(body, name="b_swiglu", T=T, S=S, tr=_row_tile(S, wide=True),
                            rows=[(sv["gu"], FFN_HIDDEN, 0), (sv["gu"], FFN_HIDDEN, 1), (dact, FFN_HIDDEN, 0)],
                            row_outs=[(2 * FFN_HIDDEN, BF16, 2 * FFN_HIDDEN, 0)])
    dh2 = _mm(host, got, "a2a", dgu, w["w_gate_up"], "nt", BF16, "mm_gate_up_dx", b_chunks=True)
    gr["w_gate_up"] = _mm(host, got, "a2a", sv["h2"], dgu, "tn", BF16, "mm_gate_up_dw", out_chunks=True)

    def body(rv, ev, pv):
        xa, delta, dh, dxn = rv
        _, vjp = jax.vjp(_f_resid_norm_mod, xa, delta, ev[0], pv[0], ev[1], ev[2])
        dxa, ddelta, dg, dw, dsc, dsh = vjp((dxn, dh))
        return [dxa, ddelta], [dg, dsc, dsh], [dw]
    (dx, do), (dg1, dsc2, dsh2), (gr["norm2_w"],) = _rowcall(
        body, name="b_resid_norm_mod2", T=T, S=S, tr=tr,
        rows=[(sv["x"], D_MODEL, 0), (sv["o"], D_MODEL, 0), (dh2, D_MODEL, 0), (dx1, D_MODEL, 0)],
        exs=[g1, sc2, sh2], pars=[w["norm2_w"]],
        row_outs=[(D_MODEL, F32, D_MODEL, 0), (D_MODEL, BF16, D_MODEL, 0)], ex_outs=[D_MODEL] * 3,
        par_outs=[(1, D_MODEL)])
    dy = _matmul(do, w["w_out"], "nt", BF16, "mm_out_dx")
    gr["w_out"] = _matmul(sv["y"], do, "tn", BF16, "mm_out_dw")

    def body(rv, ev, pv):
        _, vjp = jax.vjp(_f_outnorm, rv[0], rv[1], rv[2], pv[0])
        doa, dob, dyc, dgw = vjp(rv[3])
        return [doa, dob, dyc], [], [dgw]
    (doa, dob, dyc), _, (gr["out_norm_w"],) = _rowcall(
        body, name="b_outnorm", T=T, S=S, tr=tr,
        rows=[(sv["oa"], 768, 0), (sv["ob"], 768, 0), (sv["yc"], C_WIDTH, 0), (dy, D_MODEL, 0)], pars=[w["out_norm_w"]],
        row_outs=[(768, BF16, 768, 0), (768, BF16, 768, 0), (C_WIDTH, F32, C_WIDTH, 0)], par_outs=[(1, D_MODEL)])

    ta, tb, band = _attn_tiles(S)
    if own_early:
        host = dict(host)
        host["attn_b_bwd"] = list(host.get("attn_b_bwd", ())) + [
            gr["w_gate_up"], gr["w_down"].reshape(N_CHIPS, FFN_HIDDEN // N_CHIPS, D_MODEL)]
        host["attn_a_bwd"] = list(host.get("attn_a_bwd", ())) + [
            gr["w_out"].reshape(N_CHIPS, D_MODEL // N_CHIPS, D_MODEL)]
    dQ, dK, dV, _ = _hosted(_attn_bwd, host, got, "a2a", "attn_b_bwd", sv["Q"], sv["K"], sv["V"], sv["ob"], sv["lse_b"],
                            dob, None, B=B, S=S, HQ=B_HEADS, HK=B_HEADS, window=None,
                            scale=(B_NOPE + B_ROPE) ** -0.5, tq=tb, band=band)
    dqa, dka, dva, dsink = _hosted(_attn_bwd, host, got, "a2a", "attn_a_bwd", sv["qa"], sv["ka"], sv["va"], sv["oa"],
                                   sv["lse_a"], doa, w["sinks"], B=B, S=S, HQ=A_Q_HEADS, HK=A_KV_HEADS,
                                   window=A_WINDOW, scale=HEAD_DIM ** -0.5, tq=ta, band=None)
    gr["sinks"] = dsink

    def body(rv, ev, pv):
        dQv, dKv, dVv, cbv, sbv = rv
        dq = [_rope_bwd((cbv, sbv), p)[0] for p in _heads(dQv, B_HEADS)]
        dkr = None
        for p in _heads(dKv, B_HEADS):
            dkr = p if dkr is None else dkr + p
        return [dq, [dKv, dVv], dkr], [], []
    (dq, dkv, dkr), _, _ = _rowcall(
        body, name="b_mlaprep", T=T, S=S, tr=tr,
        rows=[(dQ, 768, 0), (dK, 768, 0), (dV, 768, 0), (cb, LANES, 0), (sb, LANES, 0)],
        row_outs=[(768, BF16, 768, 0), (1536, BF16, 1536, 0), (LANES, F32, LANES, 0)])
    dcqn = _matmul(dq, w["b_w_uq"], "nt", BF16, "mm_uq_dx")
    gr["b_w_uq"] = _matmul(sv["cqn"], dq, "tn", BF16, "mm_uq_dw")
    dckvn = _matmul(dkv, w["b_w_ukv"], "nt", BF16, "mm_ukv_dx")
    gr["b_w_ukv"] = _matmul(sv["ckvn"], dkv, "tn", BF16, "mm_ukv_dw")

    def body(rv, ev, pv):
        proj, cav, sav, cbv, sbv, dqa_, dka_, dva_, dcqn_, dckvn_, dkr_ = rv
        _, vjp = jax.vjp(lambda p, a, b_: _f_mixprep(p, cav, sav, cbv, sbv, a, b_), proj, pv[0], pv[1])
        dproj, dqn, dkvn = vjp((dqa_, dka_, dva_, dcqn_, dckvn_, dkr_))
        return [[dproj, jnp.zeros((dproj.shape[0], P_CU - P_MIX), F32)]], [], [dqn, dkvn]
    (dproj,), _, (gr["b_q_norm_w"], gr["b_kv_norm_w"]) = _rowcall(
        body, name="b_mixprep", T=T, S=S, tr=tr,
        rows=[(sv["proj"], P_MIX, 0), (ca, LANES, 0), (sa, LANES, 0), (cb, LANES, 0), (sb, LANES, 0),
              (dqa, 768, 0), (dka, 256, 0), (dva, 256, 0), (dcqn, 384, 0), (dckvn, 256, 0), (dkr, LANES, 0)],
        pars=[w["b_q_norm_w"], w["b_kv_norm_w"]], row_outs=[(P_END, BF16, P_CU, 0)],
        par_outs=[(1, B_Q_RANK), (1, B_KV_RANK)])

    ts = _sgu_rows(S)

    def body(rv, ev, pv):
        cu, cv, dycv, _ = rv
        dcus, dcvs, acc = [], [], None
        for r in range(0, ts, C_CHUNK):
            _, vjp = jax.vjp(_f_sgu, cu[r:r + C_CHUNK], cv[r:r + C_CHUNK], pv[0], pv[1], pv[2], pv[3])
            dcu, dcv, *dpar = vjp(dycv[r:r + C_CHUNK])
            dcus.append(dcu)
            dcvs.append(dcv)
            acc = dpar if acc is None else [a + b_ for a, b_ in zip(acc, dpar)]
        return [[jnp.concatenate(dcus, axis=0), jnp.concatenate(dcvs, axis=0)]], [], acc
    (dproj,), _, (gr["c_ln_w"], gr["c_ln_b"], gr["c_w_s"], gr["c_b_col"]) = _rowcall(
        body, name="b_sgu", T=T, S=S, tr=ts,
        rows=[(sv["proj"], C_WIDTH, P_CU // C_WIDTH), (sv["proj"], C_WIDTH, P_CV // C_WIDTH), (dyc, C_WIDTH, 0),
              (dproj, 2 * C_WIDTH, P_CU // (2 * C_WIDTH))],
        pars=[w["c_ln_w"], w["c_ln_b"], w["c_w_s"], w["c_b_col"]],
        row_outs=[(P_END, BF16, 2 * C_WIDTH, P_CU // (2 * C_WIDTH))],
        par_outs=[(1, C_WIDTH), (1, C_WIDTH), (C_GROUPS, C_CHUNK, C_CHUNK), (C_GROUPS, C_CHUNK, 1)],
        aliases={3: 0})
    dh = _mm(host, got, "a2a", dproj, w["w_in"], "nt", BF16, "mm_in_dx")
    gr["w_in"] = _matmul(sv["h"], dproj, "tn", BF16, "mm_in_dw")

    if prev is None:
        def body(rv, ev, pv):
            xv, dhv, dxd = rv
            _, vjp = jax.vjp(_f_norm_mod, xv, pv[0], ev[0], ev[1])
            dxa, dw, dsc, dsh = vjp(dhv)
            return [dxa + dxd], [dsc, dsh], [dw]
        (dxin,), (dsc1, dsh1), (gr["norm1_w"],) = _rowcall(
            body, name="b_norm_mod", T=T, S=S, tr=tr, rows=[(sv["x"], D_MODEL, 0), (dh, D_MODEL, 0), (dx, D_MODEL, 0)],
            exs=[sc1, sh1], pars=[w["norm1_w"]], row_outs=[(D_MODEL, F32, D_MODEL, 0)], ex_outs=[D_MODEL] * 2,
            par_outs=[(1, D_MODEL)])
        nxt = (dxin, None, None)
    else:
        x1p, dp, g2p = prev

        def body(rv, ev, pv):
            xa, delta, dhv, dxn = rv
            _, vjp = jax.vjp(_f_resid_norm_mod, xa, delta, ev[0], pv[0], ev[1], ev[2])
            dxa, ddelta, dg, dw, dsc, dsh = vjp((dxn, dhv))
            return [dxa, ddelta], [dg, dsc, dsh], [dw]
        (dx1p, ddp), (dg2p, dsc1, dsh1), (gr["norm1_w"],) = _rowcall(
            body, name="b_resid_norm_mod1", T=T, S=S, tr=tr,
            rows=[(x1p, D_MODEL, 0), (dp, D_MODEL, 0), (dh, D_MODEL, 0), (dx, D_MODEL, 0)],
            exs=[g2p, sc1, sh1], pars=[w["norm1_w"]],
            row_outs=[(D_MODEL, F32, D_MODEL, 0), (D_MODEL, BF16, D_MODEL, 0)], ex_outs=[D_MODEL] * 3,
            par_outs=[(1, D_MODEL)])
        nxt = (dx1p, ddp, dg2p)
    return gr, (dsh1, dsc1, dg1, dsh2, dsc2), nxt, got


def _lane_table(lanes_neg, lanes_pos, inv):
    freq = np.zeros((LANES,), np.int64) - 1
    sign = np.zeros((1, LANES), np.float32)
    n = len(lanes_neg)
    freq[lanes_neg] = np.arange(n)
    freq[lanes_pos] = np.arange(n)
    sign[0, lanes_neg] = -1.0
    sign[0, lanes_pos] = 1.0
    return _select_axis(inv, freq, 0).reshape(1, LANES), jnp.asarray(sign)


SHARDED = ("ada_w", "w_in", "b_w_uq", "b_w_ukv", "w_out", "w_gate_up", "w_down")
ROW_SHARDED = ("w_out", "w_down")
SMALL = ("ada_b", "norm1_w", "a_sinks", "b_q_norm_w", "b_kv_norm_w", "c_ln_w", "c_ln_b", "c_w_s", "c_b_s",
         "out_norm_w", "norm2_w", "final_norm_w")
FWD_HOST = {"attn_b_fwd": ("w_gate_up", "w_down"), "attn_a_fwd": ("w_in", "w_out", "b_w_uq", "b_w_ukv")}
FWD_HOST_FIRST = {"attn_b_fwd": ("w_gate_up", "w_down"), "mm_gate_up": ("w_in", "w_out", "b_w_uq", "b_w_ukv")}
BWD_HOST = {"attn_a_bwd": ("w_in", "w_out", "b_w_uq", "b_w_ukv"), "attn_b_bwd": ("w_gate_up", "w_down")}
FIRST_LATE = {"mm_in": ("b_w_uq", "b_w_ukv", "w_out"), "attn_a_fwd": ("w_gate_up", "w_down")}
EXPOSED = ("w_in", "b_w_uq", "b_w_ukv")
OWN_LAYER = ("w_gate_up", "w_down")
EXCHANGED = SHARDED[1:]
N_ADA = N_MOD * D_MODEL // N_CHIPS
assert not set(FIRST_LATE) & set(FWD_HOST_FIRST)


def _from_host(table, got):
    return {k: got[name][i] for name, ks in table.items() for i, k in enumerate(ks)}


def _to_host(table, arrays):
    return {name: [arrays[k] for k in ks] for name, ks in table.items()}


def _join_cols(g):
    return jnp.concatenate([g[j] for j in range(N_CHIPS)], axis=1)


def _split_cols(g):
    n = g.shape[1] // N_CHIPS
    return jnp.stack([g[:, j * n:(j + 1) * n] for j in range(N_CHIPS)])


def _layer_weights(G, small, l):
    D = D_MODEL
    return {
        **_large_operands(G),
        "norm1_w": small["norm1_w"][l].reshape(1, D),
        "sinks": jnp.broadcast_to(small["a_sinks"][l].reshape(A_Q_HEADS, 1, 1), (A_Q_HEADS, 1, LANES)),
        "b_q_norm_w": small["b_q_norm_w"][l].reshape(1, B_Q_RANK),
        "b_kv_norm_w": small["b_kv_norm_w"][l].reshape(1, B_KV_RANK),
        "c_ln_w": small["c_ln_w"][l].reshape(1, C_WIDTH), "c_ln_b": small["c_ln_b"][l].reshape(1, C_WIDTH),
        "c_w_s": small["c_w_s"][l], "c_b_col": small["c_b_s"][l].reshape(C_GROUPS, C_CHUNK, 1),
        "out_norm_w": small["out_norm_w"][l].reshape(1, D), "norm2_w": small["norm2_w"][l].reshape(1, D),
    }


def _large_operands(G):
    D = D_MODEL
    make = {
        "w_in": lambda g: _pad_axis(_join_cols(g), _map_w_in(), 1),
        "b_w_uq": lambda g: _pad_axis(_join_cols(g), _map_w_uq(), 1),
        "b_w_ukv": lambda g: _pad_axis(_join_cols(g), _map_w_ukv(), 1),
        "w_out": lambda g: g.reshape(D, D), "w_gate_up": lambda g: g, "w_down": lambda g: g.reshape(FFN_HIDDEN, D),
    }
    return {k: make[k](g) for k, g in G.items()}


def _send_buffers(gr):
    D = D_MODEL
    return {
        "w_gate_up": gr["w_gate_up"], "w_down": gr["w_down"].reshape(N_CHIPS, FFN_HIDDEN // N_CHIPS, D),
        "w_out": gr["w_out"].reshape(N_CHIPS, D // N_CHIPS, D),
        "w_in": _split_cols(_unpad_axis(gr["w_in"], _map_w_in(), IN_COLS, 1)),
        "b_w_uq": _split_cols(_unpad_axis(gr["b_w_uq"], _map_w_uq(), B_HEADS * (B_NOPE + B_ROPE), 1)),
        "b_w_ukv": _split_cols(_unpad_axis(gr["b_w_ukv"], _map_w_ukv(), B_HEADS * (B_NOPE + B_V), 1)),
    }


def _small_grads(gr, B):
    D = D_MODEL
    return {
        "norm1_w": gr["norm1_w"].reshape(D),
        "a_sinks": gr["sinks"][:, 0, 0].reshape(B, A_Q_HEADS).sum(axis=0),
        "b_q_norm_w": gr["b_q_norm_w"].reshape(B_Q_RANK), "b_kv_norm_w": gr["b_kv_norm_w"].reshape(B_KV_RANK),
        "c_ln_w": gr["c_ln_w"].reshape(C_WIDTH), "c_ln_b": gr["c_ln_b"].reshape(C_WIDTH), "c_w_s": gr["c_w_s"],
        "c_b_s": gr["c_b_col"].reshape(C_GROUPS, C_CHUNK),
        "out_norm_w": gr["out_norm_w"].reshape(D), "norm2_w": gr["norm2_w"].reshape(D),
    }


def _step(x, c, positions, target, small, shard_of, ada=None, gathered=None):
    dist = gathered is None
    B, S, D = x.shape
    T = B * S
    xt = x.reshape(T, D)
    tgt = target.reshape(T, D)
    pos_col = positions.astype(F32).reshape(T, 1)
    inv_a = 1.0 / (ROPE_THETA ** (jnp.arange(0, HEAD_DIM, 2, dtype=F32) / HEAD_DIM))
    inv_b = 1.0 / (ROPE_THETA ** (jnp.arange(0, B_ROPE, 2, dtype=F32) / B_ROPE))
    fa, sga = _lane_table(np.arange(32), 64 + np.arange(32), inv_a)
    fb, sgb = _lane_table(48 + np.arange(16), 112 + np.arange(16), inv_b)
    ca, sa = _rope_tables(pos_col, fa, sga, jnp.abs(sga), "rope_a")
    cb, sb = _rope_tables(pos_col, fb, sgb, jnp.ones_like(sgb), "rope_b")
    tabs = (ca, sa, cb, sb)
    ada_b = small["ada_b"]
    if dist:
        assert N_CHIPS * B == 8
        me = 2 * lax.axis_index("x") + lax.axis_index("y")
        first = shard_of(0)
        c_all, w_in_first = _xchg_call("gather", [c, first["w_in"]], "gather_first")
        c8 = c_all.reshape(N_CHIPS * B, D)
        mine = lax.dynamic_slice_in_dim(ada_b, me * N_ADA, N_ADA, axis=1).reshape(DEPTH, 1, N_ADA)
        part = _ada_fwd(c8, ada, mine)
        part = part.reshape(DEPTH, N_CHIPS, B, N_ADA).transpose(1, 0, 2, 3).reshape(N_CHIPS, DEPTH * B, N_ADA)
        (back,) = _xchg_call("a2a", [part], "mod_exchange")
        mod_all = jnp.concatenate([back[j].reshape(DEPTH, B, N_ADA) for j in range(N_CHIPS)], axis=-1)
        G = {"w_in": w_in_first}
    else:
        c8 = jnp.zeros((8, D), F32).at[:B].set(c)
        cols = [(jnp.stack([gathered[l]["ada_w"][j] for l in range(DEPTH)]),
                 ada_b[:, j * N_ADA:(j + 1) * N_ADA].reshape(DEPTH, 1, N_ADA)) for j in range(N_CHIPS)]
        mod_all = jnp.concatenate([_ada_fwd(c8, wj, bj)[:, :B] for wj, bj in cols], axis=-1)
        G = {k: v for k, v in gathered[0].items() if k != "ada_w"}
    saved, prevs, modss = [], [], []
    prev = None
    for l in range(DEPTH):
        w = _layer_weights(G, small, l)
        mods = tuple(mod_all[l, :, i * D:(i + 1) * D].reshape(B, 1, D) for i in range(N_MOD))
        more = l + 1 < DEPTH
        table = FWD_HOST_FIRST if l == 0 else FWD_HOST
        host = _to_host(table, shard_of(l + 1)) if dist and more else {}
        late = None
        if dist and l == 0:
            late = FIRST_LATE
            for name, ks in late.items():
                host[name] = [first[k] for k in ks] + host.get(name, [])
        prevs.append(prev)
        modss.append(mods)
        prev, sv, got = _layer_fwd(xt, prev, mods, w, tabs, B, S, host, late)
        saved.append(sv)
        if more:
            G = _from_host(table, got) if dist else {k: v for k, v in gathered[l + 1].items() if k != "ada_w"}
    x1, d, g2 = prev
    loss, dx1, dd, dg2, dfw = _final(x1, d, g2, small["final_norm_w"].reshape(1, D), tgt, B, S)

    landed = [None] * DEPTH
    smalls = [None] * DEPTH
    dmods = [None] * DEPTH
    pending = None
    for l in reversed(range(DEPTH)):
        host = _to_host(BWD_HOST, pending) if dist and pending is not None else None
        early = dist and l == 0 and host is not None
        gr, (dsh1, dsc1, dg1, dsh2, dsc2), nxt, got = _layer_bwd(saved[l], prevs[l], modss[l], tabs, dx1, dd, B, S, host, early)
        if pending is not None:
            landed[l + 1] = _from_host(BWD_HOST, got) if dist else pending
        dmods[l] = jnp.concatenate([dsh1, dsc1, dg1, dsh2, dsc2, dg2], axis=-1).reshape(B, N_MOD * D)
        pending = _send_buffers(gr)
        smalls[l] = _small_grads(gr, B)
        dx1, dd, dg2 = nxt
    dmod_all = jnp.stack(dmods)
    by_chip = [dmod_all[:, :, j * N_ADA:(j + 1) * N_ADA] for j in range(N_CHIPS)]
    if dist:
        last = EXPOSED if early else EXCHANGED
        send = jnp.stack([p.reshape(DEPTH * B, N_ADA) for p in by_chip])
        *res, back = _xchg_call("a2a", [pending[k] for k in last] + [send], "grad_exchange_last")
        landed[0] = dict(zip(last, res))
        if early:
            landed[0].update(zip(OWN_LAYER, got["attn_b_bwd"][-len(OWN_LAYER):]))
            landed[0]["w_out"] = got["attn_a_bwd"][-1]
        dmod8 = back.reshape(N_CHIPS, DEPTH, B, N_ADA).transpose(1, 0, 2, 3).reshape(DEPTH, N_CHIPS * B, N_ADA)
        ada_g = _ada_bwd(c8, dmod8)
    else:
        landed[0] = pending
        ada_g = jnp.stack([_ada_bwd(c8, jnp.zeros((DEPTH, 8, N_ADA), F32).at[:, :B].set(p)) for p in by_chip])
    small_g = {k: jnp.stack([smalls[l][k] for l in range(DEPTH)]) for k in SMALL if k not in ("final_norm_w", "ada_b")}
    small_g["ada_b"] = _sum_examples(dmod_all).reshape(DEPTH, N_MOD * D)
    small_g["final_norm_w"] = dfw.reshape(D)
    return loss[0, 0], dx1.reshape(B, S, D), landed, small_g, ada_g


def _pack(arrs, cols, mult):
    flat = jnp.concatenate([a.reshape(-1) for a in arrs])
    n = flat.shape[0]
    rows = -(-n // cols)
    rows = -(-rows // mult) * mult
    return jnp.pad(flat, (0, rows * cols - n)).reshape(rows, cols)


def _unpack(blob, shapes):
    flat = blob.reshape(-1)
    out, off = [], 0
    for s in shapes:
        n = int(np.prod(s))
        out.append(flat[off:off + n].reshape(s))
        off += n
    return out


def kernel(x, c, positions, ada_w, ada_b, norm1_w, w_in, a_sinks, b_q_norm_w, b_w_uq, b_kv_norm_w, b_w_ukv, c_ln_w, c_ln_b, c_w_s, c_b_s, out_norm_w, w_out, norm2_w, w_gate_up, w_down, final_norm_w, loss_target, m_ada_w, m_ada_b, m_norm1_w, m_w_in, m_a_sinks, m_b_q_norm_w, m_b_w_uq, m_b_kv_norm_w, m_b_w_ukv, m_c_ln_w, m_c_ln_b, m_c_w_s, m_c_b_s, m_out_norm_w, m_w_out, m_norm2_w, m_w_gate_up, m_w_down, m_final_norm_w, v_ada_w, v_ada_b, v_norm1_w, v_w_in, v_a_sinks, v_b_q_norm_w, v_b_w_uq, v_b_kv_norm_w, v_b_w_ukv, v_c_ln_w, v_c_ln_b, v_c_w_s, v_c_b_s, v_out_norm_w, v_w_out, v_norm2_w, v_w_gate_up, v_w_down, v_final_norm_w):
    names = ("ada_w", "ada_b", "norm1_w", "w_in", "a_sinks", "b_q_norm_w", "b_w_uq", "b_kv_norm_w", "b_w_ukv", "c_ln_w",
             "c_ln_b", "c_w_s", "c_b_s", "out_norm_w", "w_out", "norm2_w", "w_gate_up", "w_down", "final_norm_w")
    ws = dict(zip(names, (ada_w, ada_b, norm1_w, w_in, a_sinks, b_q_norm_w, b_w_uq, b_kv_norm_w, b_w_ukv, c_ln_w, c_ln_b,
                          c_w_s, c_b_s, out_norm_w, w_out, norm2_w, w_gate_up, w_down, final_norm_w)))
    ms = dict(zip(names, (m_ada_w, m_ada_b, m_norm1_w, m_w_in, m_a_sinks, m_b_q_norm_w, m_b_w_uq, m_b_kv_norm_w, m_b_w_ukv,
                          m_c_ln_w, m_c_ln_b, m_c_w_s, m_c_b_s, m_out_norm_w, m_w_out, m_norm2_w, m_w_gate_up, m_w_down,
                          m_final_norm_w)))
    vs = dict(zip(names, (v_ada_w, v_ada_b, v_norm1_w, v_w_in, v_a_sinks, v_b_q_norm_w, v_b_w_uq, v_b_kv_norm_w, v_b_w_ukv,
                          v_c_ln_w, v_c_ln_b, v_c_w_s, v_c_b_s, v_out_norm_w, v_w_out, v_norm2_w, v_w_gate_up, v_w_down,
                          v_final_norm_w)))
    shards = {k: ws[k].astype(BF16) for k in SHARDED}
    loss_local, grad_x, landed, gsmall, ada_g = _step(
        x, c, positions, loss_target, {k: ws[k] for k in SMALL}, lambda l: {k: shards[k][l] for k in EXCHANGED},
        ada=shards["ada_w"])

    mine = {k: jnp.stack([_sum_slots(landed[l][k], "grad_sum_" + k) for l in range(DEPTH)]) for k in EXCHANGED}
    mine["ada_w"] = ada_g
    small_shapes = [ws[k].shape for k in SMALL]
    sblob = _pack([gsmall[k] for k in SMALL] + [loss_local.reshape(1)], LANES, 8)
    sred, swapped = _all_reduce_small(sblob, "small_all_reduce", swaps=[mine[k] for k in SHARDED])
    theirs = dict(zip(SHARDED, swapped))
    grads, delta, new_m, new_v = {}, {}, {}, {}
    for k in SHARDED:
        shp = ws[k].shape
        two = (shp[0] * shp[1], shp[2])
        g, dlt, nm, nv = _adamw(ws[k].reshape(two), mine[k].reshape(two), theirs[k].reshape(two), ms[k].reshape(two),
                                vs[k].reshape(two), "adamw_" + k)
        grads[k], delta[k], new_m[k], new_v[k] = g.reshape(shp), dlt.reshape(shp), nm.reshape(shp), nv.reshape(shp)

    svals = _unpack(sred, small_shapes + [(1,)])
    loss = svals[-1].reshape(())
    pw = _pack([ws[k] for k in SMALL], LANES, 8)
    pg = _pack(svals[:-1], LANES, 8)
    pm = _pack([ms[k] for k in SMALL], LANES, 8)
    pv = _pack([vs[k] for k in SMALL], LANES, 8)
    g, dlt, nm, nv = _adamw(pw, pg, None, pm, pv, "adamw_small")
    for k, a, b_, c_, d_ in zip(SMALL, _unpack(g, small_shapes), _unpack(dlt, small_shapes), _unpack(nm, small_shapes),
                                _unpack(nv, small_shapes)):
        grads[k], delta[k], new_m[k], new_v[k] = a, b_, c_, d_

    return (loss, grad_x, *[grads[k] for k in names], *[delta[k] for k in names], *[new_m[k] for k in names],
            *[new_v[k] for k in names])
```

```python
import functools
import math

import numpy as np
import jax
import jax.numpy as jnp
from jax import lax
from jax.experimental import pallas as pl
from jax.experimental.pallas import tpu as pltpu

F32 = jnp.float32
BF16 = jnp.bfloat16

D_MODEL = 1024
DEPTH = 4
HEAD_DIM = 64
ROPE_THETA = 10000.0
NORM_EPS = 1e-6
NEG_INF = -1e30
LOG2_E = math.log2(math.e)
A_Q_HEADS = 6
A_KV_HEADS = 2
A_WINDOW = 128
B_HEADS = 6
B_Q_RANK = 384
B_KV_RANK = 256
B_NOPE = 64
B_ROPE = 32
B_V = 64
C_GROUPS = 4
C_GROUP_DIM = 64
C_WIDTH = 256
C_CHUNK = 128
IN_COLS = 1824
FFN_HIDDEN = 2816
N_MOD = 6
ADAM_LR = 0.001
ADAM_B1 = 0.9
ADAM_B2 = 0.999
ADAM_EPS = 1e-08
ADAM_WD = 0.01
ADAM_STEP = 10

LANES = 128
VMEM_LIMIT = 56 * 1024 * 1024
N_CHIPS = 4
WINDOW_SUB = 256

P_KR, P_AQ, P_AK, P_AV, P_CQ, P_CKV, P_MIX, P_CU, P_CV, P_END = 0, 128, 512, 640, 768, 1152, 1408, 1536, 1792, 2048


def _map_w_in():
    idx = -np.ones(P_END, np.int64)
    idx[P_AQ:P_AQ + 384] = np.arange(384)
    idx[P_AK:P_AK + 128] = 384 + np.arange(128)
    idx[P_AV:P_AV + 128] = 512 + np.arange(128)
    idx[P_CQ:P_CQ + 384] = 640 + np.arange(384)
    idx[P_CKV:P_CKV + 256] = 1024 + np.arange(256)
    idx[P_KR + 48 + np.arange(16)] = 1280 + np.arange(16)
    idx[P_KR + 112 + np.arange(16)] = 1296 + np.arange(16)
    idx[P_CU:P_CU + 256] = 1312 + np.arange(256)
    idx[P_CV:P_CV + 256] = 1568 + np.arange(256)
    return idx


def _map_w_uq():
    idx = -np.ones(B_HEADS * LANES, np.int64)
    for h in range(B_HEADS):
        b = h * (B_NOPE + B_ROPE)
        idx[h * LANES + np.arange(48)] = b + np.arange(48)
        idx[h * LANES + 48 + np.arange(16)] = b + 64 + np.arange(16)
        idx[h * LANES + 64 + np.arange(16)] = b + 48 + np.arange(16)
        idx[h * LANES + 112 + np.arange(16)] = b + 80 + np.arange(16)
    return idx


def _map_w_ukv():
    idx = -np.ones(2 * B_HEADS * LANES, np.int64)
    for h in range(B_HEADS):
        b = h * (B_NOPE + B_V)
        idx[h * LANES + np.arange(48)] = b + np.arange(48)
        idx[h * LANES + 64 + np.arange(16)] = b + 48 + np.arange(16)
        idx[B_HEADS * LANES + h * LANES + np.arange(B_V)] = b + B_NOPE + np.arange(B_V)
    return idx


def _inverse(idx, n):
    inv = np.zeros(n, np.int64)
    pos = np.nonzero(idx >= 0)[0]
    inv[idx[pos]] = pos
    return inv


def _runs(idx):
    runs, i, n = [], 0, len(idx)
    while i < n:
        j = i + 1
        while j < n and ((idx[i] < 0 and idx[j] < 0) or (idx[i] >= 0 and idx[j] == idx[i] + (j - i))):
            j += 1
        runs.append((int(idx[i]), j - i))
        i = j
    return runs


def _select_axis(w, idx, axis):
    pieces = []
    for start, length in _runs(idx):
        if start < 0:
            shape = list(w.shape)
            shape[axis] = length
            pieces.append(jnp.zeros(shape, w.dtype))
        else:
            pieces.append(lax.slice_in_dim(w, start, start + length, axis=axis))
    return jnp.concatenate(pieces, axis=axis)


def _pad_axis(w, idx, axis):
    return _select_axis(w, idx, axis)


def _unpad_axis(g, idx, n, axis):
    return _select_axis(g, _inverse(idx, n), axis)


def _params(sem):
    return pltpu.CompilerParams(dimension_semantics=sem, vmem_limit_bytes=VMEM_LIMIT)


def _tile(dim, target):
    if dim <= target:
        return dim
    best = None
    for t in range(LANES, target + 1, LANES):
        if dim % t == 0:
            best = t
    assert best is not None, dim
    return best


def _row_div(rows, target):
    if rows <= target:
        return rows
    best = None
    for t in range(8, target + 1, 8):
        if rows % t == 0:
            best = t
    assert best is not None, rows
    return best


_ANY = pl.BlockSpec(memory_space=pl.ANY)
_MESH = pl.DeviceIdType.MESH


def _xchg_out_shapes(kind, srcs):
    if kind == "gather":
        return [jax.ShapeDtypeStruct((N_CHIPS,) + s.shape, s.dtype) for s in srcs]
    return [jax.ShapeDtypeStruct(s.shape, s.dtype) for s in srcs]


def _xchg_scratch(kind, n):
    per = 1 if kind == "swap" else N_CHIPS - 1
    return [pltpu.SemaphoreType.DMA((per * n,)), pltpu.SemaphoreType.DMA((per * n,)), pltpu.SemaphoreType.DMA((n,))]


def _xchg_copies(kind, srcs, dsts, send_sems, recv_sems, local_sems, arrivals):
    x, y, c = lax.axis_index("x"), lax.axis_index("y"), lax.axis_index("c")
    me = 2 * x + y
    peers = [(1 - x, y), (x, 1 - y), (1 - x, 1 - y)]
    local, out, back = [], [], []
    for i, (s, d) in enumerate(zip(srcs, dsts)):
        if kind == "swap":
            cp = pltpu.make_async_remote_copy(src_ref=s, dst_ref=d, send_sem=send_sems.at[i], recv_sem=recv_sems.at[i],
                                              device_id=(x, y, 1 - c), device_id_type=_MESH)
            out.append(cp)
            back.append(cp)
            continue
        local.append(pltpu.make_async_copy(s if kind == "gather" else s.at[me], d.at[me], local_sems.at[i]))
        for kk, (px, py) in enumerate(peers):
            j = (N_CHIPS - 1) * i + kk
            theirs = 2 * px + py
            out.append(pltpu.make_async_remote_copy(
                src_ref=s if kind == "gather" else s.at[theirs], dst_ref=d.at[me], send_sem=send_sems.at[j],
                recv_sem=recv_sems.at[j], device_id=(px, py, c), device_id_type=_MESH))
            if arrivals:
                back.append(pltpu.make_async_remote_copy(
                    src_ref=s if kind == "gather" else s.at[me], dst_ref=d.at[theirs], send_sem=send_sems.at[j],
                    recv_sem=recv_sems.at[j], device_id=(px, py, c), device_id_type=_MESH))
    return local, out, back


def _xchg_start(kind, srcs, dsts, sems):
    local, out, _ = _xchg_copies(kind, srcs, dsts, *sems, arrivals=False)
    for cp in local + out:
        cp.start()


def _xchg_wait(kind, srcs, dsts, sems):
    local, out, back = _xchg_copies(kind, srcs, dsts, *sems, arrivals=True)
    for cp in back:
        cp.wait_recv()
    for cp in out:
        cp.wait_send()
    for cp in local:
        cp.wait()


def _xchg_at_ends(kind, srcs, dsts, sems, grid, first):
    ids = [pl.program_id(a) for a in range(len(grid))]
    cond = None
    for i, n in zip(ids, grid):
        c = (i == 0) if first else (i == n - 1)
        cond = c if cond is None else jnp.logical_and(cond, c)

    @pl.when(cond)
    def _():
        (_xchg_start if first else _xchg_wait)(kind, srcs, dsts, sems)


def _xchg_call(kind, srcs, name):
    n = len(srcs)

    def kern(*refs):
        s, d, sems = refs[:n], refs[n:2 * n], refs[2 * n:]
        _xchg_start(kind, s, d, sems)
        _xchg_wait(kind, s, d, sems)

    return pl.pallas_call(
        kern, name=name, in_specs=[_ANY] * n, out_specs=[_ANY] * n, out_shape=_xchg_out_shapes(kind, srcs),
        scratch_shapes=_xchg_scratch(kind, n),
    )(*srcs)


_DIMS = {"nn": (((1,), (0,)), ((), ())), "nt": (((1,), (1,)), ((), ())), "tn": (((0,), (0,)), ((), ()))}


def _matmul(a, b, mode, out_dtype, name, *, b_chunks=False, out_chunks=False, xchg=None):
    if b_chunks:
        nchunk, brows, bcols = b.shape
        bshape = (brows, nchunk * bcols)
    else:
        bshape = b.shape
    if mode == "nn":
        (m, k), (_, n) = a.shape, bshape
    elif mode == "nt":
        (m, k), (n, _) = a.shape, bshape
    else:
        (k, m), (_, n) = a.shape, bshape
    tm, tk = (1408, 1024) if mode == "tn" else (1024, 1408)
    tm, tn, tk = _tile(m, tm), _tile(n, 1408), _tile(k, tk)
    if b_chunks:
        if mode == "nn":
            tn = bcols
        else:
            assert mode == "nt"
            tk = bcols
    if out_chunks:
        assert n % N_CHIPS == 0
        tn = n // N_CHIPS
    ni, nj, nk = m // tm, n // tn, k // tk
    dims = _DIMS[mode]
    n_x = 0 if xchg is None else len(xchg[1])

    def kern(*refs):
        a_ref, b_ref = refs[0], refs[1]
        xs = refs[2:2 + n_x]
        o_ref = refs[2 + n_x]
        xd = refs[3 + n_x:3 + 2 * n_x]
        acc_ref = refs[3 + 2 * n_x]
        sems = refs[4 + 2 * n_x:]
        kk = pl.program_id(2)
        if n_x:
            _xchg_at_ends(xchg[0], xs, xd, sems, (ni, nj, nk), True)

        @pl.when(kk == 0)
        def _():
            acc_ref[...] = jnp.zeros_like(acc_ref)

        acc_ref[...] += lax.dot_general(a_ref[...], b_ref[...], dims, preferred_element_type=F32)

        @pl.when(kk == nk - 1)
        def _():
            o_ref[...] = acc_ref[...].astype(o_ref.dtype)

        if n_x:
            _xchg_at_ends(xchg[0], xs, xd, sems, (ni, nj, nk), False)

    if mode == "tn":
        a_spec = pl.BlockSpec((tk, tm), lambda i, j, kk: (kk, i))
    else:
        a_spec = pl.BlockSpec((tm, tk), lambda i, j, kk: (i, kk))
    if b_chunks and mode == "nn":
        b_spec = pl.BlockSpec((None, tk, tn), lambda i, j, kk: (j, kk, 0))
    elif b_chunks:
        b_spec = pl.BlockSpec((None, tn, tk), lambda i, j, kk: (kk, j, 0))
    elif mode == "nt":
        b_spec = pl.BlockSpec((tn, tk), lambda i, j, kk: (j, kk))
    else:
        b_spec = pl.BlockSpec((tk, tn), lambda i, j, kk: (kk, j))
    if out_chunks:
        o_spec = pl.BlockSpec((None, tm, tn), lambda i, j, kk: (j, i, 0))
        o_shape = jax.ShapeDtypeStruct((N_CHIPS, m, tn), out_dtype)
    else:
        o_spec = pl.BlockSpec((tm, tn), lambda i, j, kk: (i, j))
        o_shape = jax.ShapeDtypeStruct((m, n), out_dtype)
    xs = [] if xchg is None else list(xchg[1])
    res = pl.pallas_call(
        kern, name=name, grid=(ni, nj, nk),
        in_specs=[a_spec, b_spec] + [_ANY] * n_x, out_specs=[o_spec] + [_ANY] * n_x,
        out_shape=[o_shape] + (_xchg_out_shapes(xchg[0], xs) if n_x else []),
        scratch_shapes=[pltpu.VMEM((tm, tn), F32)] + (_xchg_scratch(xchg[0], n_x) if n_x else []),
        compiler_params=_params(("arbitrary", "arbitrary", "arbitrary") if n_x else ("parallel", "parallel", "arbitrary")),
    )(a, b, *xs)
    return (res[0], list(res[1:])) if n_x else res[0]


def _rowcall(body, *, name, T, S, tr, rows, exs=(), pars=(), row_outs=(), ex_outs=(), par_outs=(), aliases=None):
    assert S % tr == 0 and T % S == 0
    per_ex = S // tr
    nb = T // S
    n_rows, n_exs, n_pars = len(rows), len(exs), len(pars)
    n_ro, n_eo, n_po = len(row_outs), len(ex_outs), len(par_outs)

    def kern(*refs):
        ins = refs[:n_rows + n_exs + n_pars]
        outs = refs[n_rows + n_exs + n_pars:]
        rv = [r[...].astype(F32) for r in ins[:n_rows]]
        ev = [r[0] for r in ins[n_rows:n_rows + n_exs]]
        pv = [r[...] for r in ins[n_rows + n_exs:]]
        ro, eo, po = body(rv, ev, pv)
        i = pl.program_id(0)
        for ref, val in zip(outs[:n_ro], ro):
            if isinstance(val, (list, tuple)):
                off = 0
                for piece in val:
                    w = piece.shape[-1]
                    ref[:, off:off + w] = piece.astype(ref.dtype)
                    off += w
            else:
                ref[...] = val.astype(ref.dtype)
        first_of_ex = (i % per_ex) == 0
        for ref, val in zip(outs[n_ro:n_ro + n_eo], eo):
            @pl.when(first_of_ex)
            def _(ref=ref, val=val):
                ref[0] = val

            @pl.when(jnp.logical_not(first_of_ex))
            def _(ref=ref, val=val):
                ref[0] += val
        for ref, val in zip(outs[n_ro + n_eo:], po):
            @pl.when(i == 0)
            def _(ref=ref, val=val):
                ref[...] = val

            @pl.when(i != 0)
            def _(ref=ref, val=val):
                ref[...] += val

    in_specs = [pl.BlockSpec((tr, w), functools.partial(lambda i, cb: (i, cb), cb=cb)) for (_, w, cb) in rows]
    in_specs += [pl.BlockSpec((1, 1, e.shape[-1]), lambda i: (i // per_ex, 0, 0)) for e in exs]
    in_specs += [pl.BlockSpec(p.shape, functools.partial(lambda i, nd: (0,) * nd, nd=p.ndim)) for p in pars]
    out_specs = [pl.BlockSpec((tr, w), functools.partial(lambda i, cb: (i, cb), cb=cb)) for (_, _, w, cb) in row_outs]
    out_specs += [pl.BlockSpec((1, 1, f), lambda i: (i // per_ex, 0, 0)) for f in ex_outs]
    out_specs += [pl.BlockSpec(tuple(s), functools.partial(lambda i, nd: (0,) * nd, nd=len(s))) for s in par_outs]
    out_shape = [jax.ShapeDtypeStruct((T, tw), dt) for (tw, dt, _, _) in row_outs]
    out_shape += [jax.ShapeDtypeStruct((nb, 1, f), F32) for f in ex_outs]
    out_shape += [jax.ShapeDtypeStruct(tuple(s), F32) for s in par_outs]
    res = pl.pallas_call(
        kern, name=name, grid=(T // tr,), in_specs=in_specs, out_specs=out_specs, out_shape=out_shape,
        input_output_aliases=aliases or {}, compiler_params=_params(("arbitrary",)),
    )(*[r[0] for r in rows], *exs, *pars)
    return res[:n_ro], res[n_ro:n_ro + n_eo], res[n_ro + n_eo:]


def _rms(x, w, n=None):
    n = x.shape[-1] if n is None else n
    ms = jnp.sum(x * x, axis=-1, keepdims=True) * (1.0 / n)
    return x * lax.rsqrt(ms + NORM_EPS) * w


def _gelu(x):
    return 0.5 * x * (1.0 + lax.erf(x * np.float32(1.0 / math.sqrt(2.0))))


def _silu(x):
    return x * jax.nn.sigmoid(x)


@jax.custom_vjp
def _rope(x, cos, sin):
    return x * cos + pltpu.roll(x, 64, 1) * sin


def _rope_fwd(x, cos, sin):
    return _rope(x, cos, sin), (cos, sin)


def _rope_bwd(res, dy):
    cos, sin = res
    return dy * cos + pltpu.roll(dy * sin, 64, 1), None, None


_rope.defvjp(_rope_fwd, _rope_bwd)


def _heads(x, n):
    return [x[:, h * LANES:(h + 1) * LANES] for h in range(n)]


@functools.partial(jax.custom_vjp, nondiff_argnums=(1,))
def _lroll(x, shift):
    return pltpu.roll(x, shift, 1)


def _lroll_fwd(x, shift):
    return _lroll(x, shift), None


def _lroll_bwd(shift, _, dy):
    return (pltpu.roll(dy, (LANES - shift) % LANES, 1),)


_lroll.defvjp(_lroll_fwd, _lroll_bwd)


def _spread_rotary(v):
    low = lax.broadcasted_iota(jnp.int32, (1, LANES), 1) < 64
    return [jnp.where(low, v, _lroll(v, 32)), jnp.where(low, _lroll(v, 64), _lroll(v, 96))]


def _spread_values(v):
    low = lax.broadcasted_iota(jnp.int32, (1, LANES), 1) < 64
    return [jnp.where(low, v, 0.0), jnp.where(low, _lroll(v, 64), 0.0)]


def _f_norm_mod(x, w, sc, sh):
    return _rms(x, w) * (1.0 + sc) + sh


def _f_resid_norm_mod(xa, delta, g, w, sc, sh):
    xn = xa + g * delta
    return xn, _f_norm_mod(xn, w, sc, sh)


def _f_mixprep(proj, ca, sa, cb, sb, qnw, kvnw):
    qa = [_rope(p, ca, sa) for pair in _heads(proj[:, P_AQ:P_AK], A_Q_HEADS // 2) for p in _spread_rotary(pair)]
    ka = [_rope(p, ca, sa) for p in _spread_rotary(proj[:, P_AK:P_AV])]
    va = _spread_values(proj[:, P_AV:P_CQ])
    cqn = _rms(proj[:, P_CQ:P_CKV], qnw)
    ckvn = _rms(proj[:, P_CKV:P_MIX], kvnw)
    kr = _rope(proj[:, P_KR:P_AQ], cb, sb)
    return jnp.concatenate(qa, -1), jnp.concatenate(ka, -1), jnp.concatenate(va, -1), cqn, ckvn, kr


def _f_mlaprep(q, kv, kr, cb, sb):
    qs = [_rope(p, cb, sb) for p in _heads(q, B_HEADS)]
    ks = [p + kr for p in _heads(kv[:, :B_HEADS * LANES], B_HEADS)]
    return jnp.concatenate(qs, -1), jnp.concatenate(ks, -1), kv[:, B_HEADS * LANES:]


def _f_sgu(cu, cv, ln_w, ln_b, w_s, b_col):
    u = _gelu(cu)
    v = _gelu(cv)
    mu = jnp.mean(v, axis=-1, keepdims=True)
    var = jnp.mean(jnp.square(v - mu), axis=-1, keepdims=True)
    vn = (v - mu) * lax.rsqrt(var + NORM_EPS) * ln_w + ln_b
    r = lax.broadcasted_iota(jnp.int32, (C_CHUNK, C_CHUNK), 0)
    c = lax.broadcasted_iota(jnp.int32, (C_CHUNK, C_CHUNK), 1)
    lane = lax.broadcasted_iota(jnp.int32, (1, LANES), 1)
    per_block = LANES // C_GROUP_DIM
    blocks = []
    for blk, vb in enumerate(_heads(vn, C_WIDTH // LANES)):
        mixed = jnp.zeros(vb.shape, F32)
        for j in range(per_block):
            g = blk * per_block + j
            gm = (lane // C_GROUP_DIM == j).astype(F32)
            wg = jnp.where(r >= c, w_s[g], 0.0).astype(BF16)
            mixed = mixed + jnp.dot(wg, (vb * gm).astype(BF16), preferred_element_type=F32) + b_col[g] * gm
        blocks.append(mixed)
    return u * jnp.concatenate(blocks, -1)


@jax.custom_vjp
def _pack_pairs(x):
    heads = _heads(x, x.shape[-1] // LANES)
    return jnp.concatenate([heads[i] + pltpu.roll(heads[i + 1], 64, 1) for i in range(0, len(heads), 2)], -1)


def _pack_pairs_fwd(x):
    return _pack_pairs(x), None


def _pack_pairs_bwd(_, dy):
    out = []
    for p in _heads(dy, dy.shape[-1] // LANES):
        out += [p, pltpu.roll(p, 64, 1)]
    return (jnp.concatenate(out, -1),)


_pack_pairs.defvjp(_pack_pairs_fwd, _pack_pairs_bwd)


def _f_outnorm(oa, ob, yc, gw):
    na, nb = A_Q_HEADS * HEAD_DIM, B_HEADS * B_V
    ya = _rms(_pack_pairs(oa), gw[:, :na])
    yb = _rms(_pack_pairs(ob), gw[:, na:na + nb])
    ycn = _rms(yc, gw[:, na + nb:])
    return jnp.concatenate([ya, yb, ycn], -1)


def _f_swiglu(gate, up):
    return _silu(gate) * up


def _mask(q_start, k_start, tq, tk, window):
    qpos = q_start + lax.broadcasted_iota(jnp.int32, (tq, tk), 0)
    kpos = k_start + lax.broadcasted_iota(jnp.int32, (tq, tk), 1)
    m = kpos <= qpos
    if window is not None:
        m = jnp.logical_and(m, qpos - kpos < window)
    return m


def _tile_fwd(qv, kk, vv, q_start, k_start, n_free, scale, window, m0, l0):
    tq = qv.shape[0]
    W = kk.shape[0]
    c = scale * LOG2_E
    parts = []
    if n_free > 0:
        parts.append((lax.dot_general(qv, kk[:n_free], _DIMS["nt"], preferred_element_type=F32), vv[:n_free]))
    if W > n_free:
        s = lax.dot_general(qv, kk[n_free:], _DIMS["nt"], preferred_element_type=F32)
        s = jnp.where(_mask(q_start, k_start + n_free, tq, W - n_free, window), s, NEG_INF)
        parts.append((s, vv[n_free:]))
    m = None if m0 is None else m0 * (1.0 / scale)
    for s, _ in parts:
        mx = jnp.max(s, axis=-1, keepdims=True)
        m = mx if m is None else jnp.maximum(m, mx)
    l = None if l0 is None else l0 * jnp.exp2((m0 * (1.0 / scale) - m) * c)
    o = None
    for s, vpart in parts:
        p = jnp.exp2((s - m) * c)
        ps = jnp.sum(p, axis=-1, keepdims=True)
        l = ps if l is None else l + ps
        po = jnp.dot(p.astype(BF16), vpart, preferred_element_type=F32)
        o = po if o is None else o + po
    return o / l, m * scale + jnp.log(l)


def _tile_bwd(qv, kk, vv, dof, ov, lse, q_start, k_start, n_free, scale, window):
    tq = qv.shape[0]
    W = kk.shape[0]
    dob = dof.astype(BF16)
    delta = jnp.sum(dof * ov, axis=-1, keepdims=True)
    c = scale * LOG2_E
    lse2 = lse * LOG2_E
    dq = None
    outs = []
    for (a, b, masked) in ((0, n_free, False), (n_free, W, True)):
        if b <= a:
            continue
        kp, vp = kk[a:b], vv[a:b]
        s = lax.dot_general(qv, kp, _DIMS["nt"], preferred_element_type=F32)
        if masked:
            s = jnp.where(_mask(q_start, k_start + a, tq, b - a, window), s, NEG_INF)
        p = jnp.exp2(s * c - lse2)
        dp = lax.dot_general(dob, vp, _DIMS["nt"], preferred_element_type=F32)
        ds = (p * ((dp - delta) * scale)).astype(BF16)
        d = jnp.dot(ds, kp, preferred_element_type=F32)
        dq = d if dq is None else dq + d
        dkp = lax.dot_general(ds, qv, _DIMS["tn"], preferred_element_type=F32)
        dvp = lax.dot_general(p.astype(BF16), dob, _DIMS["tn"], preferred_element_type=F32)
        outs.append((a, dkp, dvp))
    return dq, outs


def _attn_fwd(q, k, v, sinks, *, B, S, HQ, HK, window, scale, tq, band, name, xchg=None):
    G = HQ // HK
    nq = S // tq
    T = B * S
    has_sink = sinks is not None
    n_x = 0 if xchg is None else len(xchg[1])
    n_in = 4 if has_sink else 3
    hp = G if window is not None else 1
    grid = (B, HQ // hp, nq)

    def kern(*refs):
        xs, xd, sems = refs[n_in:n_in + n_x], refs[n_in + n_x + 2:n_in + 2 * n_x + 2], refs[n_in + 2 * n_x + 2:]
        refs = refs[:n_in] + refs[n_in + n_x:n_in + n_x + 2]
        if n_x:
            _xchg_at_ends(xchg[0], xs, xd, sems, grid, True)
        if has_sink:
            q_ref, k_ref, v_ref, s_ref, o_ref, lse_ref = refs
        else:
            q_ref, k_ref, v_ref, o_ref, lse_ref = refs
        q_start = pl.program_id(2) * tq

        def sink(g, rows):
            if not has_sink:
                return None, None
            return jnp.broadcast_to(s_ref[g][:, :1], (rows, 1)), jnp.ones((rows, 1), F32)

        def finish(o, lse):
            o_ref[...] = o.astype(o_ref.dtype)
            lse_ref[...] = jnp.broadcast_to(lse, (tq, LANES))

        if window is None:
            qv = q_ref[...]
            m0, l0 = sink(0, tq)
            bidx = q_start // band
            for bb in range(S // band):
                @pl.when(bidx == bb)
                def _(bb=bb):
                    W = (bb + 1) * band
                    finish(*_tile_fwd(qv, k_ref[0:W, :], v_ref[0:W, :], q_start, 0, bb * band, scale, None, m0, l0))
        else:
            sub = min(tq, WINDOW_SUB)
            W = min(S, sub + window)
            for r in range(0, tq, sub):
                k_start = pl.multiple_of(jnp.maximum(q_start + r - window, 0), window)
                kk, vv = k_ref[pl.ds(k_start, W), :], v_ref[pl.ds(k_start, W), :]
                for g in range(hp):
                    cols = slice(g * LANES, (g + 1) * LANES)
                    o, lse = _tile_fwd(q_ref[r:r + sub, cols], kk, vv, q_start + r, k_start, 0, scale, window, *sink(g, sub))
                    o_ref[r:r + sub, cols] = o.astype(o_ref.dtype)
                    lse_ref[r:r + sub, cols] = jnp.broadcast_to(lse, (sub, LANES))
        if n_x:
            _xchg_at_ends(xchg[0], xs, xd, sems, grid, False)

    q_spec = pl.BlockSpec((tq, hp * LANES), lambda b, h, i: (b * nq + i, h))
    kv_spec = pl.BlockSpec((S, LANES), lambda b, h, i: (b, h * hp // G))
    in_specs = [q_spec, kv_spec, kv_spec]
    args = [q, k, v]
    if has_sink:
        in_specs.append(pl.BlockSpec((hp, 1, LANES), lambda b, h, i: (h, 0, 0)))
        args.append(sinks)
    xs = [] if xchg is None else list(xchg[1])
    res = pl.pallas_call(
        kern, name=name, grid=grid, in_specs=in_specs + [_ANY] * n_x, out_specs=[q_spec, q_spec] + [_ANY] * n_x,
        out_shape=[jax.ShapeDtypeStruct((T, HQ * LANES), BF16), jax.ShapeDtypeStruct((T, HQ * LANES), F32)]
        + (_xchg_out_shapes(xchg[0], xs) if n_x else []),
        scratch_shapes=_xchg_scratch(xchg[0], n_x) if n_x else [],
        compiler_params=_params(("arbitrary",) * 3 if n_x else ("parallel", "parallel", "arbitrary")),
    )(*args, *xs)
    return ((res[0], res[1]), list(res[2:])) if n_x else res


def _attn_bwd(q, k, v, o, lse, do, sinks, *, B, S, HQ, HK, window, scale, tq, band, name, xchg=None):
    G = HQ // HK
    nq = S // tq
    T = B * S
    has_sink = sinks is not None
    n_x = 0 if xchg is None else len(xchg[1])
    n_in, n_out = (7, 4) if has_sink else (6, 3)
    hp = G if window is not None else 1
    ng = G // hp
    grid = (B, HK, ng, nq)

    def kern(*refs):
        xs, xd = refs[n_in:n_in + n_x], refs[n_in + n_x + n_out:n_in + 2 * n_x + n_out]
        dk_acc, dv_acc = refs[n_in + 2 * n_x + n_out:n_in + 2 * n_x + n_out + 2]
        sems = refs[n_in + 2 * n_x + n_out + 2:]
        refs = refs[:n_in] + refs[n_in + n_x:n_in + n_x + n_out]
        if n_x:
            _xchg_at_ends(xchg[0], xs, xd, sems, grid, True)
        if has_sink:
            q_ref, k_ref, v_ref, o_ref, lse_ref, do_ref, s_ref, dq_ref, dk_ref, dv_ref, ds_ref = refs
        else:
            q_ref, k_ref, v_ref, o_ref, lse_ref, do_ref, dq_ref, dk_ref, dv_ref = refs
        gi = pl.program_id(2)
        qi = pl.program_id(3)
        q_start = qi * tq

        @pl.when(jnp.logical_and(gi == 0, qi == 0))
        def _():
            dk_acc[...] = jnp.zeros_like(dk_acc)
            dv_acc[...] = jnp.zeros_like(dv_acc)

        if window is None:
            qv = q_ref[...]
            dof = do_ref[...].astype(F32)
            ov = o_ref[...].astype(F32)
            lse_v = lse_ref[...][:, :1]
            bidx = q_start // band
            for bb in range(S // band):
                @pl.when(bidx == bb)
                def _(bb=bb):
                    W = (bb + 1) * band
                    dq, outs = _tile_bwd(qv, k_ref[0:W, :], v_ref[0:W, :], dof, ov, lse_v, q_start, 0, bb * band, scale, None)
                    dq_ref[...] = dq.astype(dq_ref.dtype)
                    for a, dkp, dvp in outs:
                        dk_acc[a:a + dkp.shape[0], :] += dkp
                        dv_acc[a:a + dvp.shape[0], :] += dvp
        else:
            sub = min(tq, WINDOW_SUB)
            W = min(S, sub + window)
            parts = []
            sink_parts = []
            for g in range(hp):
                cols = slice(g * LANES, (g + 1) * LANES)
                acc = jnp.zeros((1, LANES), F32)
                for r in range(0, tq, sub):
                    rows = slice(r, r + sub)
                    k_start = pl.multiple_of(jnp.maximum(q_start + r - window, 0), window)
                    dof, ov = do_ref[rows, cols].astype(F32), o_ref[rows, cols].astype(F32)
                    lse_v = lse_ref[rows, cols][:, :1]
                    dq, outs = _tile_bwd(q_ref[rows, cols], k_ref[pl.ds(k_start, W), :], v_ref[pl.ds(k_start, W), :],
                                         dof, ov, lse_v, q_start + r, k_start, 0, scale, window)
                    dq_ref[rows, cols] = dq.astype(dq_ref.dtype)
                    parts.append((k_start, outs[0][1], outs[0][2]))
                    if has_sink:
                        delta = jnp.sum(dof * ov, axis=-1, keepdims=True)
                        acc = acc - jnp.sum(jnp.exp(s_ref[g][:, :1] - lse_v) * delta, axis=0, keepdims=True)
                sink_parts.append(acc)
            for k_start, dkp, dvp in parts:
                dk_acc[pl.ds(k_start, W), :] += dkp
                dv_acc[pl.ds(k_start, W), :] += dvp

        @pl.when(jnp.logical_and(gi == ng - 1, qi == nq - 1))
        def _():
            dk_ref[...] = dk_acc[...].astype(dk_ref.dtype)
            dv_ref[...] = dv_acc[...].astype(dv_ref.dtype)
        if has_sink:
            assert window is not None
            for g, part in enumerate(sink_parts):
                part = jnp.broadcast_to(part, (1, LANES))

                @pl.when(qi == 0)
                def _(g=g, part=part):
                    ds_ref[g] = part

                @pl.when(qi != 0)
                def _(g=g, part=part):
                    ds_ref[g] += part
        if n_x:
            _xchg_at_ends(xchg[0], xs, xd, sems, grid, False)

    q_spec = pl.BlockSpec((tq, hp * LANES), lambda b, hk, g, i: (b * nq + i, hk * ng + g))
    kv_spec = pl.BlockSpec((S, LANES), lambda b, hk, g, i: (b, hk))
    in_specs = [q_spec, kv_spec, kv_spec, q_spec, q_spec, q_spec]
    args = [q, k, v, o, lse, do]
    out_specs = [q_spec, kv_spec, kv_spec]
    out_shape = [jax.ShapeDtypeStruct((T, HQ * LANES), BF16), jax.ShapeDtypeStruct((T, HK * LANES), BF16),
                 jax.ShapeDtypeStruct((T, HK * LANES), BF16)]
    if has_sink:
        in_specs.append(pl.BlockSpec((hp, 1, LANES), lambda b, hk, g, i: (hk * ng + g, 0, 0)))
        args.append(sinks)
        out_specs.append(pl.BlockSpec((hp, 1, LANES), lambda b, hk, g, i: (b * HK * ng + hk * ng + g, 0, 0)))
        out_shape.append(jax.ShapeDtypeStruct((B * HQ, 1, LANES), F32))
    xs = [] if xchg is None else list(xchg[1])
    res = pl.pallas_call(
        kern, name=name, grid=grid, in_specs=in_specs + [_ANY] * n_x, out_specs=out_specs + [_ANY] * n_x,
        out_shape=out_shape + (_xchg_out_shapes(xchg[0], xs) if n_x else []),
        scratch_shapes=[pltpu.VMEM((S, LANES), F32)] * 2 + (_xchg_scratch(xchg[0], n_x) if n_x else []),
        compiler_params=_params(("arbitrary",) * 4 if n_x else ("parallel", "parallel", "arbitrary", "arbitrary")),
    )(*args, *xs)
    main = tuple(res[:n_out]) if has_sink else (*res[:n_out], None)
    return (main, list(res[n_out:])) if n_x else main


def _rope_tables(pos_col, freq, sign, keep, name):
    T = pos_col.shape[0]
    tr = _tile(T, 1024)

    def kern(p_ref, f_ref, s_ref, k_ref, c_out, s_out):
        ang = p_ref[...] * f_ref[...]
        c_out[...] = jnp.cos(ang) * k_ref[...]
        s_out[...] = jnp.sin(ang) * s_ref[...]

    spec = pl.BlockSpec((tr, LANES), lambda i: (i, 0))
    par = pl.BlockSpec((1, LANES), lambda i: (0, 0))
    return pl.pallas_call(
        kern, name=name, grid=(T // tr,), in_specs=[pl.BlockSpec((tr, 1), lambda i: (i, 0)), par, par, par],
        out_specs=[spec, spec], out_shape=[jax.ShapeDtypeStruct((T, LANES), F32)] * 2,
        compiler_params=_params(("parallel",)),
    )(pos_col, freq, sign, keep)


def _ada_fwd(c8, ada_w, ada_b):
    L, D, N = ada_w.shape

    def kern(c_ref, w_ref, b_ref, o_ref):
        act = _silu(c_ref[...]).astype(BF16)
        o_ref[0] = jnp.dot(act, w_ref[0], preferred_element_type=F32) + b_ref[0]

    return pl.pallas_call(
        kern, name="ada_fwd", grid=(L,),
        in_specs=[pl.BlockSpec((8, D), lambda l: (0, 0)), pl.BlockSpec((1, D, N), lambda l: (l, 0, 0)),
                  pl.BlockSpec((1, 1, N), lambda l: (l, 0, 0))],
        out_specs=pl.BlockSpec((1, 8, N), lambda l: (l, 0, 0)),
        out_shape=jax.ShapeDtypeStruct((L, 8, N), F32), compiler_params=_params(("parallel",)),
    )(c8, ada_w, ada_b)


def _ada_bwd(c8, dmod):
    L, _, N = dmod.shape
    D = c8.shape[1]

    def kern(c_ref, d_ref, gw_ref):
        act = _silu(c_ref[...]).astype(BF16)
        gw_ref[0] = lax.dot_general(act, d_ref[0].astype(BF16), _DIMS["tn"], preferred_element_type=F32)

    return pl.pallas_call(
        kern, name="ada_bwd", grid=(L,),
        in_specs=[pl.BlockSpec((8, D), lambda l: (0, 0)), pl.BlockSpec((1, 8, N), lambda l: (l, 0, 0))],
        out_specs=pl.BlockSpec((1, D, N), lambda l: (l, 0, 0)),
        out_shape=jax.ShapeDtypeStruct((L, D, N), F32), compiler_params=_params(("parallel",)),
    )(c8, dmod)


def _sum_examples(d):
    L, _, N = d.shape

    def kern(d_ref, o_ref):
        o_ref[...] = jnp.sum(d_ref[...], axis=1, keepdims=True)

    vm = pl.BlockSpec(memory_space=pltpu.VMEM)
    return pl.pallas_call(kern, name="sum_examples", in_specs=[vm], out_specs=vm,
                          out_shape=jax.ShapeDtypeStruct((L, 1, N), F32))(d)


def _adamw(w, ga, gb, m, v, name):
    rows, cols = w.shape
    tr = _row_div(rows, 256)
    two = gb is not None

    def kern(*refs):
        if two:
            w_ref, ga_ref, gb_ref, m_ref, v_ref, g_out, d_out, m_out, v_out = refs
            gv = ga_ref[...] + gb_ref[...]
        else:
            w_ref, ga_ref, m_ref, v_ref, g_out, d_out, m_out, v_out = refs
            gv = ga_ref[...]
        mn = ADAM_B1 * m_ref[...] + (1.0 - ADAM_B1) * gv
        vn = ADAM_B2 * v_ref[...] + (1.0 - ADAM_B2) * jnp.square(gv)
        m_hat = mn / (1.0 - ADAM_B1 ** ADAM_STEP)
        v_hat = vn / (1.0 - ADAM_B2 ** ADAM_STEP)
        g_out[...] = gv
        d_out[...] = -ADAM_LR * (m_hat / (jnp.sqrt(v_hat) + ADAM_EPS) + ADAM_WD * w_ref[...])
        m_out[...] = mn
        v_out[...] = vn

    spec = pl.BlockSpec((tr, cols), lambda i: (i, 0))
    args = [w, ga, gb, m, v] if two else [w, ga, m, v]
    return pl.pallas_call(
        kern, name=name, grid=(rows // tr,), in_specs=[spec] * len(args), out_specs=[spec] * 4,
        out_shape=[jax.ShapeDtypeStruct((rows, cols), F32)] * 4, compiler_params=_params(("parallel",)),
    )(*args)


def _sum_slots(x, name):
    n, rows, cols = x.shape
    tr = _row_div(rows, 256)

    def kern(x_ref, o_ref):
        acc = x_ref[0].astype(F32)
        for j in range(1, n):
            acc = acc + x_ref[j].astype(F32)
        o_ref[...] = acc

    return pl.pallas_call(
        kern, name=name, grid=(rows // tr,), in_specs=[pl.BlockSpec((n, tr, cols), lambda i: (0, i, 0))],
        out_specs=pl.BlockSpec((tr, cols), lambda i: (i, 0)), out_shape=jax.ShapeDtypeStruct((rows, cols), F32),
        compiler_params=_params(("parallel",)),
    )(x)


def _all_reduce_small(blob, name, swaps=()):
    R, C = blob.shape
    n = len(swaps)

    def kern(*refs):
        src, xs, out, xd = refs[0], refs[1:1 + n], refs[1 + n], refs[2 + n:2 + 2 * n]
        pair, chips, send_sems, recv_sems = refs[2 + 2 * n:6 + 2 * n]
        swap_sems = refs[6 + 2 * n:]
        if n:
            _xchg_start("swap", xs, xd, swap_sems)
        x, y, c = lax.axis_index("x"), lax.axis_index("y"), lax.axis_index("c")
        me = 2 * x + y
        to_sibling = pltpu.make_async_remote_copy(
            src_ref=src, dst_ref=pair, send_sem=send_sems.at[0], recv_sem=recv_sems.at[0],
            device_id=(x, y, 1 - c), device_id_type=_MESH)
        to_sibling.start()
        to_sibling.wait()
        chips[me] = src[...] + pair[...]
        peers = [(1 - x, y), (x, 1 - y), (1 - x, 1 - y)]
        copies = [pltpu.make_async_remote_copy(
            src_ref=chips.at[me], dst_ref=chips.at[me], send_sem=send_sems.at[1 + kk], recv_sem=recv_sems.at[1 + kk],
            device_id=(px, py, c), device_id_type=_MESH) for kk, (px, py) in enumerate(peers)]
        for cp in copies:
            cp.start()
        for kk, (px, py) in enumerate(peers):
            pltpu.make_async_remote_copy(
                src_ref=chips.at[me], dst_ref=chips.at[2 * px + py], send_sem=send_sems.at[1 + kk],
                recv_sem=recv_sems.at[1 + kk], device_id=(px, py, c), device_id_type=_MESH).wait_recv()
        for cp in copies:
            cp.wait_send()
        acc = chips[0]
        for j in range(1, N_CHIPS):
            acc = acc + chips[j]
        out[...] = acc
        if n:
            _xchg_wait("swap", xs, xd, swap_sems)

    vm = pl.BlockSpec(memory_space=pltpu.VMEM)
    swaps = list(swaps)
    res = pl.pallas_call(
        kern, name=name, in_specs=[vm] + [_ANY] * n, out_specs=[vm] + [_ANY] * n,
        out_shape=[jax.ShapeDtypeStruct((R, C), F32)] + _xchg_out_shapes("swap", swaps),
        scratch_shapes=[pltpu.VMEM((R, C), F32), pltpu.VMEM((N_CHIPS, R, C), F32), pltpu.SemaphoreType.DMA((4,)),
                        pltpu.SemaphoreType.DMA((4,))] + (_xchg_scratch("swap", n) if n else []),
        compiler_params=pltpu.CompilerParams(vmem_limit_bytes=VMEM_LIMIT),
    )(blob, *swaps)
    return res[0], list(res[1:])


def _row_tile(S, wide=False):
    for t in ((512, 256) if wide else (1024, 512, 256)):
        if S % t == 0:
            return t
    return 128


def _attn_tiles(S):
    return min(S, 1024), min(S, 256), min(S, 256)


def _hosted(fn, host, got, kind, name, *args, **kw):
    if host is not None and host.get(name):
        res, xs = fn(*args, name=name, xchg=(kind, host[name]), **kw)
        got[name] = xs
        return res
    return fn(*args, name=name, **kw)


def _mm(host, got, kind, a, b, mode, out_dtype, name, **kw):
    return _hosted(_matmul, host, got, kind, name, a, b, mode, out_dtype, **kw)


def _merge_late(w, late, got, name):
    if not late or name not in late:
        return w
    return {**w, **_large_operands(dict(zip(late[name], got[name])))}


def _sgu_rows(S):
    return 4 * C_CHUNK if S % (4 * C_CHUNK) == 0 else C_CHUNK


def _layer_fwd(xin, prev, mods, w, tabs, B, S, host=None, late=None):
    T = B * S
    tr = _row_tile(S)
    sh1, sc1, g1, sh2, sc2, g2 = mods
    ca, sa, cb, sb = tabs
    sv = {}
    got = {}
    if prev is None:
        def body(rv, ev, pv):
            return [_f_norm_mod(rv[0], pv[0], ev[0], ev[1])], [], []
        (h,), _, _ = _rowcall(body, name="f_norm_mod", T=T, S=S, tr=tr, rows=[(xin, D_MODEL, 0)], exs=[sc1, sh1],
                              pars=[w["norm1_w"]], row_outs=[(D_MODEL, BF16, D_MODEL, 0)])
        x = xin
    else:
        x1p, dp, g2p = prev

        def body(rv, ev, pv):
            xn, hh = _f_resid_norm_mod(rv[0], rv[1], ev[0], pv[0], ev[1], ev[2])
            return [xn, hh], [], []
        (x, h), _, _ = _rowcall(body, name="f_resid_norm_mod1", T=T, S=S, tr=tr,
                                rows=[(x1p, D_MODEL, 0), (dp, D_MODEL, 0)], exs=[g2p, sc1, sh1], pars=[w["norm1_w"]],
                                row_outs=[(D_MODEL, F32, D_MODEL, 0), (D_MODEL, BF16, D_MODEL, 0)])
    sv["x"], sv["h"] = x, h
    proj = _mm(host, got, "gather", h, w["w_in"], "nn", BF16, "mm_in")
    sv["proj"] = proj
    w = _merge_late(w, late, got, "mm_in")

    def body(rv, ev, pv):
        outs = _f_mixprep(rv[0], rv[1], rv[2], rv[3], rv[4], pv[0], pv[1])
        return list(outs), [], []
    (qa, ka, va, cqn, ckvn, kr), _, _ = _rowcall(
        body, name="f_mixprep", T=T, S=S, tr=tr,
        rows=[(proj, P_MIX, 0), (ca, LANES, 0), (sa, LANES, 0), (cb, LANES, 0), (sb, LANES, 0)],
        pars=[w["b_q_norm_w"], w["b_kv_norm_w"]],
        row_outs=[(768, BF16, 768, 0), (256, BF16, 256, 0), (256, BF16, 256, 0), (384, BF16, 384, 0),
                  (256, BF16, 256, 0), (LANES, F32, LANES, 0)])
    sv.update(qa=qa, ka=ka, va=va, cqn=cqn, ckvn=ckvn)
    q = _matmul(cqn, w["b_w_uq"], "nn", BF16, "mm_uq")
    kv = _matmul(ckvn, w["b_w_ukv"], "nn", BF16, "mm_ukv")

    def body(rv, ev, pv):
        Q, K, V = _f_mlaprep(rv[0], rv[1], rv[2], rv[3], rv[4])
        return [Q, K, V], [], []
    (Q, K, V), _, _ = _rowcall(
        body, name="f_mlaprep", T=T, S=S, tr=tr,
        rows=[(q, 768, 0), (kv, 1536, 0), (kr, LANES, 0), (cb, LANES, 0), (sb, LANES, 0)],
        row_outs=[(768, BF16, 768, 0)] * 3)
    sv.update(Q=Q, K=K, V=V)
    ta, tb, band = _attn_tiles(S)
    oa, lse_a = _hosted(_attn_fwd, host, got, "gather", "attn_a_fwd", qa, ka, va, w["sinks"], B=B, S=S, HQ=A_Q_HEADS,
                        HK=A_KV_HEADS, window=A_WINDOW, scale=HEAD_DIM ** -0.5, tq=ta, band=None)
    ob, lse_b = _hosted(_attn_fwd, host, got, "gather", "attn_b_fwd", Q, K, V, None, B=B, S=S, HQ=B_HEADS, HK=B_HEADS,
                        window=None, scale=(B_NOPE + B_ROPE) ** -0.5, tq=tb, band=band)
    sv.update(oa=oa, lse_a=lse_a, ob=ob, lse_b=lse_b)
    w = _merge_late(w, late, got, "attn_a_fwd")
    ts = _sgu_rows(S)

    def body(rv, ev, pv):
        outs = [_f_sgu(rv[0][r:r + C_CHUNK], rv[1][r:r + C_CHUNK], pv[0], pv[1], pv[2], pv[3])
                for r in range(0, ts, C_CHUNK)]
        return [jnp.concatenate(outs, axis=0)], [], []
    (yc,), _, _ = _rowcall(body, name="f_sgu", T=T, S=S, tr=ts,
                           rows=[(proj, C_WIDTH, P_CU // C_WIDTH), (proj, C_WIDTH, P_CV // C_WIDTH)],
                           pars=[w["c_ln_w"], w["c_ln_b"], w["c_w_s"], w["c_b_col"]],
                           row_outs=[(C_WIDTH, F32, C_WIDTH, 0)])
    sv["yc"] = yc

    def body(rv, ev, pv):
        return [_f_outnorm(rv[0], rv[1], rv[2], pv[0])], [], []
    (y,), _, _ = _rowcall(body, name="f_outnorm", T=T, S=S, tr=tr,
                          rows=[(oa, 768, 0), (ob, 768, 0), (yc, C_WIDTH, 0)], pars=[w["out_norm_w"]],
                          row_outs=[(D_MODEL, BF16, D_MODEL, 0)])
    sv["y"] = y
    o = _mm(host, got, "gather", y, w["w_out"], "nn", BF16, "mm_out")
    sv["o"] = o

    def body(rv, ev, pv):
        xn, hh = _f_resid_norm_mod(rv[0], rv[1], ev[0], pv[0], ev[1], ev[2])
        return [xn, hh], [], []
    (x1, h2), _, _ = _rowcall(body, name="f_resid_norm_mod2", T=T, S=S, tr=tr,
                              rows=[(x, D_MODEL, 0), (o, D_MODEL, 0)], exs=[g1, sc2, sh2], pars=[w["norm2_w"]],
                              row_outs=[(D_MODEL, F32, D_MODEL, 0), (D_MODEL, BF16, D_MODEL, 0)])
    sv["h2"] = h2
    gu = _mm(host, got, "gather", h2, w["w_gate_up"], "nn", BF16, "mm_gate_up", b_chunks=True)
    sv["gu"] = gu

    def body(rv, ev, pv):
        return [_f_swiglu(rv[0], rv[1])], [], []
    (act,), _, _ = _rowcall(body, name="f_swiglu", T=T, S=S, tr=_row_tile(S, wide=True),
                            rows=[(gu, FFN_HIDDEN, 0), (gu, FFN_HIDDEN, 1)], row_outs=[(FFN_HIDDEN, BF16, FFN_HIDDEN, 0)])
    sv["act"] = act
    d = _mm(host, got, "gather", act, w["w_down"], "nn", BF16, "mm_down")
    sv["x1"], sv["d"], sv["w"] = x1, d, w
    return (x1, d, g2), sv, got


def _final(x1, d, g2, fw, target, B, S):
    T = B * S
    tr = _row_tile(S)

    def loss_fn(x1v, dv, gv, wv, tv):
        yf = _rms(x1v + gv * dv, wv)
        return 0.5 * jnp.sum(jnp.mean(jnp.square(yf - tv), axis=-1))

    def body(rv, ev, pv):
        x1v, dv, tv = rv
        val, vjp = jax.vjp(lambda a, b_, g, ww: loss_fn(a, b_, g, ww, tv), x1v, dv, ev[0], pv[0])
        dx1, dd, dg, dw = vjp(jnp.ones((), F32))
        return [dx1, dd], [dg], [dw, jnp.full((1, LANES), val, F32)]
    (dx1, dd), (dg2,), (dfw, loss) = _rowcall(
        body, name="final_loss", T=T, S=S, tr=tr, rows=[(x1, D_MODEL, 0), (d, D_MODEL, 0), (target, D_MODEL, 0)],
        exs=[g2], pars=[fw], row_outs=[(D_MODEL, F32, D_MODEL, 0), (D_MODEL, BF16, D_MODEL, 0)],
        ex_outs=[D_MODEL], par_outs=[(1, D_MODEL), (1, LANES)])
    return loss, dx1, dd, dg2, dfw


def _layer_bwd(sv, prev, mods, tabs, dx1, dd, B, S, host=None, own_early=False):
    w = sv["w"]
    T = B * S
    tr = _row_tile(S)
    sh1, sc1, g1, sh2, sc2, g2 = mods
    ca, sa, cb, sb = tabs
    gr = {}
    got = {}
    dact = _mm(host, got, "a2a", dd, w["w_down"], "nt", BF16, "mm_down_dx")
    gr["w_down"] = _matmul(sv["act"], dd, "tn", BF16, "mm_down_dw")

    def body(rv, ev, pv):
        _, vjp = jax.vjp(_f_swiglu, rv[0], rv[1])
        dgate, dup = vjp(rv[2])
        return [[dgate, dup]], [], []
    (dgu,), _, _ = _rowcall(body, name="b_swiglu", T=T, S=S, tr=_row_tile(S, wide=True),
                            rows=[(sv["gu"], FFN_HIDDEN, 0), (sv["gu"], FFN_HIDDEN, 1), (dact, FFN_HIDDEN, 0)],
                            row_outs=[(2 * FFN_HIDDEN, BF16, 2 * FFN_HIDDEN, 0)])
    dh2 = _mm(host, got, "a2a", dgu, w["w_gate_up"], "nt", BF16, "mm_gate_up_dx", b_chunks=True)
    gr["w_gate_up"] = _mm(host, got, "a2a", sv["h2"], dgu, "tn", BF16, "mm_gate_up_dw", out_chunks=True)

    def body(rv, ev, pv):
        xa, delta, dh, dxn = rv
        _, vjp = jax.vjp(_f_resid_norm_mod, xa, delta, ev[0], pv[0], ev[1], ev[2])
        dxa, ddelta, dg, dw, dsc, dsh = vjp((dxn, dh))
        return [dxa, ddelta], [dg, dsc, dsh], [dw]
    (dx, do), (dg1, dsc2, dsh2), (gr["norm2_w"],) = _rowcall(
        body, name="b_resid_norm_mod2", T=T, S=S, tr=tr,
        rows=[(sv["x"], D_MODEL, 0), (sv["o"], D_MODEL, 0), (dh2, D_MODEL, 0), (dx1, D_MODEL, 0)],
        exs=[g1, sc2, sh2], pars=[w["norm2_w"]],
        row_outs=[(D_MODEL, F32, D_MODEL, 0), (D_MODEL, BF16, D_MODEL, 0)], ex_outs=[D_MODEL] * 3,
        par_outs=[(1, D_MODEL)])
    dy = _matmul(do, w["w_out"], "nt", BF16, "mm_out_dx")
    gr["w_out"] = _matmul(sv["y"], do, "tn", BF16, "mm_out_dw")

    def body(rv, ev, pv):
        _, vjp = jax.vjp(_f_outnorm, rv[0], rv[1], rv[2], pv[0])
        doa, dob, dyc, dgw = vjp(rv[3])
        return [doa, dob, dyc], [], [dgw]
    (doa, dob, dyc), _, (gr["out_norm_w"],) = _rowcall(
        body, name="b_outnorm", T=T, S=S, tr=tr,
        rows=[(sv["oa"], 768, 0), (sv["ob"], 768, 0), (sv["yc"], C_WIDTH, 0), (dy, D_MODEL, 0)], pars=[w["out_norm_w"]],
        row_outs=[(768, BF16, 768, 0), (768, BF16, 768, 0), (C_WIDTH, F32, C_WIDTH, 0)], par_outs=[(1, D_MODEL)])

    ta, tb, band = _attn_tiles(S)
    if own_early:
        host = dict(host)
        host["attn_b_bwd"] = list(host.get("attn_b_bwd", ())) + [
            gr["w_gate_up"], gr["w_down"].reshape(N_CHIPS, FFN_HIDDEN // N_CHIPS, D_MODEL)]
        host["attn_a_bwd"] = list(host.get("attn_a_bwd", ())) + [
            gr["w_out"].reshape(N_CHIPS, D_MODEL // N_CHIPS, D_MODEL)]
    dQ, dK, dV, _ = _hosted(_attn_bwd, host, got, "a2a", "attn_b_bwd", sv["Q"], sv["K"], sv["V"], sv["ob"], sv["lse_b"],
                            dob, None, B=B, S=S, HQ=B_HEADS, HK=B_HEADS, window=None,
                            scale=(B_NOPE + B_ROPE) ** -0.5, tq=tb, band=band)
    dqa, dka, dva, dsink = _hosted(_attn_bwd, host, got, "a2a", "attn_a_bwd", sv["qa"], sv["ka"], sv["va"], sv["oa"],
                                   sv["lse_a"], doa, w["sinks"], B=B, S=S, HQ=A_Q_HEADS, HK=A_KV_HEADS,
                                   window=A_WINDOW, scale=HEAD_DIM ** -0.5, tq=ta, band=None)
    gr["sinks"] = dsink

    def body(rv, ev, pv):
        dQv, dKv, dVv, cbv, sbv = rv
        dq = [_rope_bwd((cbv, sbv), p)[0] for p in _heads(dQv, B_HEADS)]
        dkr = None
        for p in _heads(dKv, B_HEADS):
            dkr = p if dkr is None else dkr + p
        return [dq, [dKv, dVv], dkr], [], []
    (dq, dkv, dkr), _, _ = _rowcall(
        body, name="b_mlaprep", T=T, S=S, tr=tr,
        rows=[(dQ, 768, 0), (dK, 768, 0), (dV, 768, 0), (cb, LANES, 0), (sb, LANES, 0)],
        row_outs=[(768, BF16, 768, 0), (1536, BF16, 1536, 0), (LANES, F32, LANES, 0)])
    dcqn = _matmul(dq, w["b_w_uq"], "nt", BF16, "mm_uq_dx")
    gr["b_w_uq"] = _matmul(sv["cqn"], dq, "tn", BF16, "mm_uq_dw")
    dckvn = _matmul(dkv, w["b_w_ukv"], "nt", BF16, "mm_ukv_dx")
    gr["b_w_ukv"] = _matmul(sv["ckvn"], dkv, "tn", BF16, "mm_ukv_dw")

    def body(rv, ev, pv):
        proj, cav, sav, cbv, sbv, dqa_, dka_, dva_, dcqn_, dckvn_, dkr_ = rv
        _, vjp = jax.vjp(lambda p, a, b_: _f_mixprep(p, cav, sav, cbv, sbv, a, b_), proj, pv[0], pv[1])
        dproj, dqn, dkvn = vjp((dqa_, dka_, dva_, dcqn_, dckvn_, dkr_))
        return [[dproj, jnp.zeros((dproj.shape[0], P_CU - P_MIX), F32)]], [], [dqn, dkvn]
    (dproj,), _, (gr["b_q_norm_w"], gr["b_kv_norm_w"]) = _rowcall(
        body, name="b_mixprep", T=T, S=S, tr=tr,
        rows=[(sv["proj"], P_MIX, 0), (ca, LANES, 0), (sa, LANES, 0), (cb, LANES, 0), (sb, LANES, 0),
              (dqa, 768, 0), (dka, 256, 0), (dva, 256, 0), (dcqn, 384, 0), (dckvn, 256, 0), (dkr, LANES, 0)],
        pars=[w["b_q_norm_w"], w["b_kv_norm_w"]], row_outs=[(P_END, BF16, P_CU, 0)],
        par_outs=[(1, B_Q_RANK), (1, B_KV_RANK)])

    ts = _sgu_rows(S)

    def body(rv, ev, pv):
        cu, cv, dycv, _ = rv
        dcus, dcvs, acc = [], [], None
        for r in range(0, ts, C_CHUNK):
            _, vjp = jax.vjp(_f_sgu, cu[r:r + C_CHUNK], cv[r:r + C_CHUNK], pv[0], pv[1], pv[2], pv[3])
            dcu, dcv, *dpar = vjp(dycv[r:r + C_CHUNK])
            dcus.append(dcu)
            dcvs.append(dcv)
            acc = dpar if acc is None else [a + b_ for a, b_ in zip(acc, dpar)]
        return [[jnp.concatenate(dcus, axis=0), jnp.concatenate(dcvs, axis=0)]], [], acc
    (dproj,), _, (gr["c_ln_w"], gr["c_ln_b"], gr["c_w_s"], gr["c_b_col"]) = _rowcall(
        body, name="b_sgu", T=T, S=S, tr=ts,
        rows=[(sv["proj"], C_WIDTH, P_CU // C_WIDTH), (sv["proj"], C_WIDTH, P_CV // C_WIDTH), (dyc, C_WIDTH, 0),
              (dproj, 2 * C_WIDTH, P_CU // (2 * C_WIDTH))],
        pars=[w["c_ln_w"], w["c_ln_b"], w["c_w_s"], w["c_b_col"]],
        row_outs=[(P_END, BF16, 2 * C_WIDTH, P_CU // (2 * C_WIDTH))],
        par_outs=[(1, C_WIDTH), (1, C_WIDTH), (C_GROUPS, C_CHUNK, C_CHUNK), (C_GROUPS, C_CHUNK, 1)],
        aliases={3: 0})
    dh = _mm(host, got, "a2a", dproj, w["w_in"], "nt", BF16, "mm_in_dx")
    gr["w_in"] = _matmul(sv["h"], dproj, "tn", BF16, "mm_in_dw")

    if prev is None:
        def body(rv, ev, pv):
            xv, dhv, dxd = rv
            _, vjp = jax.vjp(_f_norm_mod, xv, pv[0], ev[0], ev[1])
            dxa, dw, dsc, dsh = vjp(dhv)
            return [dxa + dxd], [dsc, dsh], [dw]
        (dxin,), (dsc1, dsh1), (gr["norm1_w"],) = _rowcall(
            body, name="b_norm_mod", T=T, S=S, tr=tr, rows=[(sv["x"], D_MODEL, 0), (dh, D_MODEL, 0), (dx, D_MODEL, 0)],
            exs=[sc1, sh1], pars=[w["norm1_w"]], row_outs=[(D_MODEL, F32, D_MODEL, 0)], ex_outs=[D_MODEL] * 2,
            par_outs=[(1, D_MODEL)])
        nxt = (dxin, None, None)
    else:
        x1p, dp, g2p = prev

        def body(rv, ev, pv):
            xa, delta, dhv, dxn = rv
            _, vjp = jax.vjp(_f_resid_norm_mod, xa, delta, ev[0], pv[0], ev[1], ev[2])
            dxa, ddelta, dg, dw, dsc, dsh = vjp((dxn, dhv))
            return [dxa, ddelta], [dg, dsc, dsh], [dw]
        (dx1p, ddp), (dg2p, dsc1, dsh1), (gr["norm1_w"],) = _rowcall(
            body, name="b_resid_norm_mod1", T=T, S=S, tr=tr,
            rows=[(x1p, D_MODEL, 0), (dp, D_MODEL, 0), (dh, D_MODEL, 0), (dx, D_MODEL, 0)],
            exs=[g2p, sc1, sh1], pars=[w["norm1_w"]],
            row_outs=[(D_MODEL, F32, D_MODEL, 0), (D_MODEL, BF16, D_MODEL, 0)], ex_outs=[D_MODEL] * 3,
            par_outs=[(1, D_MODEL)])
        nxt = (dx1p, ddp, dg2p)
    return gr, (dsh1, dsc1, dg1, dsh2, dsc2), nxt, got


def _lane_table(lanes_neg, lanes_pos, inv):
    freq = np.zeros((LANES,), np.int64) - 1
    sign = np.zeros((1, LANES), np.float32)
    n = len(lanes_neg)
    freq[lanes_neg] = np.arange(n)
    freq[lanes_pos] = np.arange(n)
    sign[0, lanes_neg] = -1.0
    sign[0, lanes_pos] = 1.0
    return _select_axis(inv, freq, 0).reshape(1, LANES), jnp.asarray(sign)


SHARDED = ("ada_w", "w_in", "b_w_uq", "b_w_ukv", "w_out", "w_gate_up", "w_down")
ROW_SHARDED = ("w_out", "w_down")
SMALL = ("ada_b", "norm1_w", "a_sinks", "b_q_norm_w", "b_kv_norm_w", "c_ln_w", "c_ln_b", "c_w_s", "c_b_s",
         "out_norm_w", "norm2_w", "final_norm_w")
FWD_HOST = {"attn_b_fwd": ("w_gate_up", "w_down"), "attn_a_fwd": ("w_in", "w_out", "b_w_uq", "b_w_ukv")}
FWD_HOST_FIRST = {"attn_b_fwd": ("w_gate_up", "w_down"), "mm_gate_up": ("w_in", "w_out", "b_w_uq", "b_w_ukv")}
BWD_HOST = {"attn_a_bwd": ("w_in", "w_out", "b_w_uq", "b_w_ukv"), "attn_b_bwd": ("w_gate_up", "w_down")}
FIRST_LATE = {"mm_in": ("b_w_uq", "b_w_ukv", "w_out"), "attn_a_fwd": ("w_gate_up", "w_down")}
EXPOSED = ("w_in", "b_w_uq", "b_w_ukv")
OWN_LAYER = ("w_gate_up", "w_down")
EXCHANGED = SHARDED[1:]
N_ADA = N_MOD * D_MODEL // N_CHIPS
assert not set(FIRST_LATE) & set(FWD_HOST_FIRST)


def _from_host(table, got):
    return {k: got[name][i] for name, ks in table.items() for i, k in enumerate(ks)}


def _to_host(table, arrays):
    return {name: [arrays[k] for k in ks] for name, ks in table.items()}


def _join_cols(g):
    return jnp.concatenate([g[j] for j in range(N_CHIPS)], axis=1)


def _split_cols(g):
    n = g.shape[1] // N_CHIPS
    return jnp.stack([g[:, j * n:(j + 1) * n] for j in range(N_CHIPS)])


def _layer_weights(G, small, l):
    D = D_MODEL
    return {
        **_large_operands(G),
        "norm1_w": small["norm1_w"][l].reshape(1, D),
        "sinks": jnp.broadcast_to(small["a_sinks"][l].reshape(A_Q_HEADS, 1, 1), (A_Q_HEADS, 1, LANES)),
        "b_q_norm_w": small["b_q_norm_w"][l].reshape(1, B_Q_RANK),
        "b_kv_norm_w": small["b_kv_norm_w"][l].reshape(1, B_KV_RANK),
        "c_ln_w": small["c_ln_w"][l].reshape(1, C_WIDTH), "c_ln_b": small["c_ln_b"][l].reshape(1, C_WIDTH),
        "c_w_s": small["c_w_s"][l], "c_b_col": small["c_b_s"][l].reshape(C_GROUPS, C_CHUNK, 1),
        "out_norm_w": small["out_norm_w"][l].reshape(1, D), "norm2_w": small["norm2_w"][l].reshape(1, D),
    }


def _large_operands(G):
    D = D_MODEL
    make = {
        "w_in": lambda g: _pad_axis(_join_cols(g), _map_w_in(), 1),
        "b_w_uq": lambda g: _pad_axis(_join_cols(g), _map_w_uq(), 1),
        "b_w_ukv": lambda g: _pad_axis(_join_cols(g), _map_w_ukv(), 1),
        "w_out": lambda g: g.reshape(D, D), "w_gate_up": lambda g: g, "w_down": lambda g: g.reshape(FFN_HIDDEN, D),
    }
    return {k: make[k](g) for k, g in G.items()}


def _send_buffers(gr):
    D = D_MODEL
    return {
        "w_gate_up": gr["w_gate_up"], "w_down": gr["w_down"].reshape(N_CHIPS, FFN_HIDDEN // N_CHIPS, D),
        "w_out": gr["w_out"].reshape(N_CHIPS, D // N_CHIPS, D),
        "w_in": _split_cols(_unpad_axis(gr["w_in"], _map_w_in(), IN_COLS, 1)),
        "b_w_uq": _split_cols(_unpad_axis(gr["b_w_uq"], _map_w_uq(), B_HEADS * (B_NOPE + B_ROPE), 1)),
        "b_w_ukv": _split_cols(_unpad_axis(gr["b_w_ukv"], _map_w_ukv(), B_HEADS * (B_NOPE + B_V), 1)),
    }


def _small_grads(gr, B):
    D = D_MODEL
    return {
        "norm1_w": gr["norm1_w"].reshape(D),
        "a_sinks": gr["sinks"][:, 0, 0].reshape(B, A_Q_HEADS).sum(axis=0),
        "b_q_norm_w": gr["b_q_norm_w"].reshape(B_Q_RANK), "b_kv_norm_w": gr["b_kv_norm_w"].reshape(B_KV_RANK),
        "c_ln_w": gr["c_ln_w"].reshape(C_WIDTH), "c_ln_b": gr["c_ln_b"].reshape(C_WIDTH), "c_w_s": gr["c_w_s"],
        "c_b_s": gr["c_b_col"].reshape(C_GROUPS, C_CHUNK),
        "out_norm_w": gr["out_norm_w"].reshape(D), "norm2_w": gr["norm2_w"].reshape(D),
    }


def _step(x, c, positions, target, small, shard_of, ada=None, gathered=None):
    dist = gathered is None
    B, S, D = x.shape
    T = B * S
    xt = x.reshape(T, D)
    tgt = target.reshape(T, D)
    pos_col = positions.astype(F32).reshape(T, 1)
    inv_a = 1.0 / (ROPE_THETA ** (jnp.arange(0, HEAD_DIM, 2, dtype=F32) / HEAD_DIM))
    inv_b = 1.0 / (ROPE_THETA ** (jnp.arange(0, B_ROPE, 2, dtype=F32) / B_ROPE))
    fa, sga = _lane_table(np.arange(32), 64 + np.arange(32), inv_a)
    fb, sgb = _lane_table(48 + np.arange(16), 112 + np.arange(16), inv_b)
    ca, sa = _rope_tables(pos_col, fa, sga, jnp.abs(sga), "rope_a")
    cb, sb = _rope_tables(pos_col, fb, sgb, jnp.ones_like(sgb), "rope_b")
    tabs = (ca, sa, cb, sb)
    ada_b = small["ada_b"]
    if dist:
        assert N_CHIPS * B == 8
        me = 2 * lax.axis_index("x") + lax.axis_index("y")
        first = shard_of(0)
        c_all, w_in_first = _xchg_call("gather", [c, first["w_in"]], "gather_first")
        c8 = c_all.reshape(N_CHIPS * B, D)
        mine = lax.dynamic_slice_in_dim(ada_b, me * N_ADA, N_ADA, axis=1).reshape(DEPTH, 1, N_ADA)
        part = _ada_fwd(c8, ada, mine)
        part = part.reshape(DEPTH, N_CHIPS, B, N_ADA).transpose(1, 0, 2, 3).reshape(N_CHIPS, DEPTH * B, N_ADA)
        (back,) = _xchg_call("a2a", [part], "mod_exchange")
        mod_all = jnp.concatenate([back[j].reshape(DEPTH, B, N_ADA) for j in range(N_CHIPS)], axis=-1)
        G = {"w_in": w_in_first}
    else:
        c8 = jnp.zeros((8, D), F32).at[:B].set(c)
        cols = [(jnp.stack([gathered[l]["ada_w"][j] for l in range(DEPTH)]),
                 ada_b[:, j * N_ADA:(j + 1) * N_ADA].reshape(DEPTH, 1, N_ADA)) for j in range(N_CHIPS)]
        mod_all = jnp.concatenate([_ada_fwd(c8, wj, bj)[:, :B] for wj, bj in cols], axis=-1)
        G = {k: v for k, v in gathered[0].items() if k != "ada_w"}
    saved, prevs, modss = [], [], []
    prev = None
    for l in range(DEPTH):
        w = _layer_weights(G, small, l)
        mods = tuple(mod_all[l, :, i * D:(i + 1) * D].reshape(B, 1, D) for i in range(N_MOD))
        more = l + 1 < DEPTH
        table = FWD_HOST_FIRST if l == 0 else FWD_HOST
        host = _to_host(table, shard_of(l + 1)) if dist and more else {}
        late = None
        if dist and l == 0:
            late = FIRST_LATE
            for name, ks in late.items():
                host[name] = [first[k] for k in ks] + host.get(name, [])
        prevs.append(prev)
        modss.append(mods)
        prev, sv, got = _layer_fwd(xt, prev, mods, w, tabs, B, S, host, late)
        saved.append(sv)
        if more:
            G = _from_host(table, got) if dist else {k: v for k, v in gathered[l + 1].items() if k != "ada_w"}
    x1, d, g2 = prev
    loss, dx1, dd, dg2, dfw = _final(x1, d, g2, small["final_norm_w"].reshape(1, D), tgt, B, S)

    landed = [None] * DEPTH
    smalls = [None] * DEPTH
    dmods = [None] * DEPTH
    pending = None
    for l in reversed(range(DEPTH)):
        host = _to_host(BWD_HOST, pending) if dist and pending is not None else None
        early = dist and l == 0 and host is not None
        gr, (dsh1, dsc1, dg1, dsh2, dsc2), nxt, got = _layer_bwd(saved[l], prevs[l], modss[l], tabs, dx1, dd, B, S, host, early)
        if pending is not None:
            landed[l + 1] = _from_host(BWD_HOST, got) if dist else pending
        dmods[l] = jnp.concatenate([dsh1, dsc1, dg1, dsh2, dsc2, dg2], axis=-1).reshape(B, N_MOD * D)
        pending = _send_buffers(gr)
        smalls[l] = _small_grads(gr, B)
        dx1, dd, dg2 = nxt
    dmod_all = jnp.stack(dmods)
    by_chip = [dmod_all[:, :, j * N_ADA:(j + 1) * N_ADA] for j in range(N_CHIPS)]
    if dist:
        last = EXPOSED if early else EXCHANGED
        send = jnp.stack([p.reshape(DEPTH * B, N_ADA) for p in by_chip])
        *res, back = _xchg_call("a2a", [pending[k] for k in last] + [send], "grad_exchange_last")
        landed[0] = dict(zip(last, res))
        if early:
            landed[0].update(zip(OWN_LAYER, got["attn_b_bwd"][-len(OWN_LAYER):]))
            landed[0]["w_out"] = got["attn_a_bwd"][-1]
        dmod8 = back.reshape(N_CHIPS, DEPTH, B, N_ADA).transpose(1, 0, 2, 3).reshape(DEPTH, N_CHIPS * B, N_ADA)
        ada_g = _ada_bwd(c8, dmod8)
    else:
        landed[0] = pending
        ada_g = jnp.stack([_ada_bwd(c8, jnp.zeros((DEPTH, 8, N_ADA), F32).at[:, :B].set(p)) for p in by_chip])
    small_g = {k: jnp.stack([smalls[l][k] for l in range(DEPTH)]) for k in SMALL if k not in ("final_norm_w", "ada_b")}
    small_g["ada_b"] = _sum_examples(dmod_all).reshape(DEPTH, N_MOD * D)
    small_g["final_norm_w"] = dfw.reshape(D)
    return loss[0, 0], dx1.reshape(B, S, D), landed, small_g, ada_g


def _pack(arrs, cols, mult):
    flat = jnp.concatenate([a.reshape(-1) for a in arrs])
    n = flat.shape[0]
    rows = -(-n // cols)
    rows = -(-rows // mult) * mult
    return jnp.pad(flat, (0, rows * cols - n)).reshape(rows, cols)


def _unpack(blob, shapes):
    flat = blob.reshape(-1)
    out, off = [], 0
    for s in shapes:
        n = int(np.prod(s))
        out.append(flat[off:off + n].reshape(s))
        off += n
    return out


def kernel(x, c, positions, ada_w, ada_b, norm1_w, w_in, a_sinks, b_q_norm_w, b_w_uq, b_kv_norm_w, b_w_ukv, c_ln_w, c_ln_b, c_w_s, c_b_s, out_norm_w, w_out, norm2_w, w_gate_up, w_down, final_norm_w, loss_target, m_ada_w, m_ada_b, m_norm1_w, m_w_in, m_a_sinks, m_b_q_norm_w, m_b_w_uq, m_b_kv_norm_w, m_b_w_ukv, m_c_ln_w, m_c_ln_b, m_c_w_s, m_c_b_s, m_out_norm_w, m_w_out, m_norm2_w, m_w_gate_up, m_w_down, m_final_norm_w, v_ada_w, v_ada_b, v_norm1_w, v_w_in, v_a_sinks, v_b_q_norm_w, v_b_w_uq, v_b_kv_norm_w, v_b_w_ukv, v_c_ln_w, v_c_ln_b, v_c_w_s, v_c_b_s, v_out_norm_w, v_w_out, v_norm2_w, v_w_gate_up, v_w_down, v_final_norm_w):
    names = ("ada_w", "ada_b", "norm1_w", "w_in", "a_sinks", "b_q_norm_w", "b_w_uq", "b_kv_norm_w", "b_w_ukv", "c_ln_w",
             "c_ln_b", "c_w_s", "c_b_s", "out_norm_w", "w_out", "norm2_w", "w_gate_up", "w_down", "final_norm_w")
    ws = dict(zip(names, (ada_w, ada_b, norm1_w, w_in, a_sinks, b_q_norm_w, b_w_uq, b_kv_norm_w, b_w_ukv, c_ln_w, c_ln_b,
                          c_w_s, c_b_s, out_norm_w, w_out, norm2_w, w_gate_up, w_down, final_norm_w)))
    ms = dict(zip(names, (m_ada_w, m_ada_b, m_norm1_w, m_w_in, m_a_sinks, m_b_q_norm_w, m_b_w_uq, m_b_kv_norm_w, m_b_w_ukv,
                          m_c_ln_w, m_c_ln_b, m_c_w_s, m_c_b_s, m_out_norm_w, m_w_out, m_norm2_w, m_w_gate_up, m_w_down,
                          m_final_norm_w)))
    vs = dict(zip(names, (v_ada_w, v_ada_b, v_norm1_w, v_w_in, v_a_sinks, v_b_q_norm_w, v_b_w_uq, v_b_kv_norm_w, v_b_w_ukv,
                          v_c_ln_w, v_c_ln_b, v_c_w_s, v_c_b_s, v_out_norm_w, v_w_out, v_norm2_w, v_w_gate_up, v_w_down,
                          v_final_norm_w)))
    shards = {k: ws[k].astype(BF16) for k in SHARDED}
    loss_local, grad_x, landed, gsmall, ada_g = _step(
        x, c, positions, loss_target, {k: ws[k] for k in SMALL}, lambda l: {k: shards[k][l] for k in EXCHANGED},
        ada=shards["ada_w"])

    mine = {k: jnp.stack([_sum_slots(landed[l][k], "grad_sum_" + k) for l in range(DEPTH)]) for k in EXCHANGED}
    mine["ada_w"] = ada_g
    small_shapes = [ws[k].shape for k in SMALL]
    sblob = _pack([gsmall[k] for k in SMALL] + [loss_local.reshape(1)], LANES, 8)
    sred, swapped = _all_reduce_small(sblob, "small_all_reduce", swaps=[mine[k] for k in SHARDED])
    theirs = dict(zip(SHARDED, swapped))
    grads, delta, new_m, new_v = {}, {}, {}, {}
    for k in SHARDED:
        shp = ws[k].shape
        two = (shp[0] * shp[1], shp[2])
        g, dlt, nm, nv = _adamw(ws[k].reshape(two), mine[k].reshape(two), theirs[k].reshape(two), ms[k].reshape(two),
                                vs[k].reshape(two), "adamw_" + k)
        grads[k], delta[k], new_m[k], new_v[k] = g.reshape(shp), dlt.reshape(shp), nm.reshape(shp), nv.reshape(shp)

    svals = _unpack(sred, small_shapes + [(1,)])
    loss = svals[-1].reshape(())
    pw = _pack([ws[k] for k in SMALL], LANES, 8)
    pg = _pack(svals[:-1], LANES, 8)
    pm = _pack([ms[k] for k in SMALL], LANES, 8)
    pv = _pack([vs[k] for k in SMALL], LANES, 8)
    g, dlt, nm, nv = _adamw(pw, pg, None, pm, pv, "adamw_small")
    for k, a, b_, c_, d_ in zip(SMALL, _unpack(g, small_shapes), _unpack(dlt, small_shapes), _unpack(nm, small_shapes),
                                _unpack(nv, small_shapes)):
        grads[k], delta[k], new_m[k], new_v[k] = a, b_, c_, d_

    return (loss, grad_x, *[grads[k] for k in names], *[delta[k] for k in names], *[new_m[k] for k in names],
            *[new_v[k] for k in names])
```
